```python
import jax, jax.numpy as jnp
from jax import lax
import numpy as np

D_MODEL = 1024
BATCH = 8
SEQ = 4096
DEPTH = 1

HEAD_DIM = 64
ATT_WIDTH = D_MODEL // 2
N_ATT_HEADS = ATT_WIDTH // HEAD_DIM
DIL_PATTERNS = ((128, 1), (512, 4), (2048, 16))
POOL_WIDTH = D_MODEL - ATT_WIDTH
POOL_WINDOWS = (2, 4, 8, 16)
N_POOL_GROUPS = len(POOL_WINDOWS)
POOL_GROUP_DIM = POOL_WIDTH // N_POOL_GROUPS
IN_WIDTH = 3 * ATT_WIDTH + POOL_WIDTH
MIX_WIDTH = ATT_WIDTH + POOL_WIDTH
D_FF = ((8 * D_MODEL // 3 + 127) // 128) * 128
CONV_WIDTH = 3
PLE_DIM = 256
EPS = 1e-6

kernel_name = "hybrid_dilated_attn_multiscale_pool_block"


def _rmsnorm(t, g):
    tf = t.astype(jnp.float32)
    inv = lax.rsqrt(jnp.mean(tf * tf, axis=-1, keepdims=True) + EPS)
    return (tf * inv * g.astype(jnp.float32)).astype(t.dtype)


def _alibi_slopes(n_heads):
    return jnp.exp2(-8.0 * (jnp.arange(n_heads, dtype=jnp.float32) + 1.0) / n_heads)


def _dilated_branch(q, k, v, slopes, window, dil):
    B, S, H, Dh = q.shape
    span = window // dil
    unit = span * dil
    Sp = -(-S // unit) * unit
    L = Sp // dil
    nb = L // span
    pad = ((0, 0), (0, Sp - S), (0, 0), (0, 0))

    def blocks(t):
        return jnp.pad(t, pad).reshape(B, nb, span, dil, H, Dh)

    def with_prev(t):
        prev = jnp.pad(t, ((0, 0), (1, 0), (0, 0), (0, 0), (0, 0), (0, 0)))[:, :nb]
        return jnp.concatenate([prev, t], axis=2)

    qb = blocks(q)
    kk = with_prev(blocks(k))
    vv = with_prev(blocks(v))
    s = jnp.einsum('bnqrhd,bnkrhd->bnrhqk', qb, kk,
                   preferred_element_type=jnp.float32)
    qi = jnp.arange(span)[:, None]
    kj = jnp.arange(2 * span)[None, :]
    diff = qi + span - kj
    blk = jnp.arange(nb)[:, None, None]
    valid = (diff >= 0) & (diff <= span) & (blk * span - span + kj[None] >= 0)
    dist = (diff * dil).astype(jnp.float32)
    s = s - slopes[:, None, None] * dist
    s = jnp.where(valid[:, None, None], s, -jnp.inf)
    m = jnp.max(s, axis=-1, keepdims=True)
    e = jnp.exp(s - m)
    den = jnp.sum(e, axis=-1)
    o = jnp.einsum('bnrhqk,bnkrhd->bnqrhd', e, vv.astype(jnp.float32))
    den_t = jnp.transpose(den, (0, 1, 4, 2, 3))
    m_t = jnp.transpose(m[..., 0], (0, 1, 4, 2, 3))
    o = o / den_t[..., None]
    o = o.reshape(B, Sp, H, Dh)[:, :S]
    return o, m_t.reshape(B, Sp, H)[:, :S], den_t.reshape(B, Sp, H)[:, :S]


def _dilated_attention(q, k, v, slopes):
    outs, maxes, dens = [], [], []
    for window, dil in DIL_PATTERNS:
        o, m, d = _dilated_branch(q, k, v, slopes, window, dil)
        outs.append(o)
        maxes.append(m)
        dens.append(d)
    o = jnp.stack(outs)
    m = jnp.stack(maxes)
    d = jnp.stack(dens)
    w = d * jnp.exp(m - jnp.max(m, axis=0, keepdims=True))
    w = w / jnp.sum(w, axis=0, keepdims=True)
    return jnp.sum(w[..., None] * o, axis=0)


def _multiscale_pool(u, pool_w, pool_scale):
    B, S, _ = u.shape
    ug = u.astype(jnp.float32).reshape(B, S, N_POOL_GROUPS, POOL_GROUP_DIM)
    cs0 = jnp.pad(jnp.cumsum(ug, axis=1), ((0, 0), (1, 0), (0, 0), (0, 0)))
    upper = cs0[:, 1:]
    t = jnp.arange(S)
    diffs = []
    for g, w in enumerate(POOL_WINDOWS):
        lower = jnp.pad(cs0[:, :, g], ((0, 0), (w - 1, 0), (0, 0)))[:, :S]
        count = jnp.minimum(t + 1, w).astype(jnp.float32)[None, :, None]
        diffs.append((upper[:, :, g] - lower) / count - ug[:, :, g])
    dlt = jnp.stack(diffs, axis=2).astype(u.dtype)
    y = jnp.einsum('bsgc,gce->bsge', dlt, pool_w)
    y = y * pool_scale.reshape(N_POOL_GROUPS, POOL_GROUP_DIM)
    return y.reshape(B, S, POOL_WIDTH)


def _shift(t, s):
    if s == 0:
        return t
    return jnp.pad(t, ((0, 0), (s, 0), (0, 0)))[:, :t.shape[1]]


def _causal_dwconv(t, w, b):
    y = b
    for kk in range(CONV_WIDTH):
        y = y + w[kk] * _shift(t, CONV_WIDTH - 1 - kk)
    return y


def _fwd_setup_inputs(seed: int = 0) -> dict:
    key = jax.random.key(seed)
    ks = jax.random.split(key, 20)
    f32 = jnp.float32

    def nrm(k, shape, scale):
        return jax.random.normal(k, shape, f32) * scale

    return {
        "x": nrm(ks[0], (BATCH, SEQ, D_MODEL), 1.0),
        "p": nrm(ks[1], (DEPTH, BATCH, SEQ, PLE_DIM), 1.0),
        "ln_mix": 1.0 + nrm(ks[2], (DEPTH, D_MODEL), 0.02),
        "w_in": nrm(ks[3], (DEPTH, D_MODEL, IN_WIDTH), D_MODEL ** -0.5),
        "pool_w": nrm(ks[4], (DEPTH, N_POOL_GROUPS, POOL_GROUP_DIM, POOL_GROUP_DIM), POOL_GROUP_DIM ** -0.5),
        "pool_scale": 1.0 + nrm(ks[5], (DEPTH, POOL_WIDTH), 0.1),
        "w_out": nrm(ks[6], (DEPTH, MIX_WIDTH, D_MODEL), MIX_WIDTH ** -0.5),
        "ln_ffn": 1.0 + nrm(ks[7], (DEPTH, D_MODEL), 0.02),
        "w_up": nrm(ks[8], (DEPTH, D_MODEL, 2 * D_FF), D_MODEL ** -0.5),
        "conv_w": nrm(ks[9], (DEPTH, CONV_WIDTH, 2 * D_FF), CONV_WIDTH ** -0.5),
        "conv_b": nrm(ks[10], (DEPTH, 2 * D_FF), 0.02),
        "w_down": nrm(ks[11], (DEPTH, D_FF, D_MODEL), D_FF ** -0.5),
        "ln_ple": 1.0 + nrm(ks[12], (DEPTH, D_MODEL), 0.02),
        "w_ple_gate": nrm(ks[13], (DEPTH, D_MODEL, D_MODEL), D_MODEL ** -0.5),
        "w_ple": nrm(ks[14], (DEPTH, PLE_DIM, D_MODEL), PLE_DIM ** -0.5),
        "ln_final": 1.0 + nrm(ks[15], (D_MODEL,), 0.02),
    }


def _fwd_reference(x, p, ln_mix, w_in, pool_w, pool_scale, w_out, ln_ffn, w_up, conv_w, conv_b,
              w_down, ln_ple, w_ple_gate, w_ple, ln_final):
    B, S, _ = x.shape
    slopes = _alibi_slopes(N_ATT_HEADS)
    h = x
    for i in range(DEPTH):
        hn = _rmsnorm(h, ln_mix[i])
        z = hn @ w_in[i]
        q = z[..., :ATT_WIDTH].reshape(B, S, N_ATT_HEADS, HEAD_DIM) * (HEAD_DIM ** -0.5)
        k = z[..., ATT_WIDTH:2 * ATT_WIDTH].reshape(B, S, N_ATT_HEADS, HEAD_DIM)
        v = z[..., 2 * ATT_WIDTH:3 * ATT_WIDTH].reshape(B, S, N_ATT_HEADS, HEAD_DIM)
        u = z[..., 3 * ATT_WIDTH:]
        att = _dilated_attention(q, k, v, slopes).reshape(B, S, ATT_WIDTH).astype(h.dtype)
        pool = _multiscale_pool(u, pool_w[i], pool_scale[i]).astype(h.dtype)
        h = h + jnp.concatenate([att, pool], axis=-1) @ w_out[i]
        hn = _rmsnorm(h, ln_ffn[i])
        up = _causal_dwconv(hn @ w_up[i], conv_w[i], conv_b[i])
        gate, val = jnp.split(up, 2, axis=-1)
        h = h + (jax.nn.silu(gate) * val) @ w_down[i]
        g = jax.nn.sigmoid(_rmsnorm(h, ln_ple[i]) @ w_ple_gate[i])
        h = h + g * (p[i] @ w_ple[i])
    return _rmsnorm(h, ln_final)


import jax as _jax
import jax.numpy as _jnp

TWIN_FORMAT = 'train_step'
FWD_PARAMS = ['x', 'p', 'ln_mix', 'w_in', 'pool_w', 'pool_scale', 'w_out', 'ln_ffn', 'w_up', 'conv_w', 'conv_b', 'w_down', 'ln_ple', 'w_ple_gate', 'w_ple', 'ln_final']
TWIN_WEIGHTS = ['ln_mix', 'w_in', 'pool_w', 'pool_scale', 'w_out', 'ln_ffn', 'w_up', 'conv_w', 'conv_b', 'w_down', 'ln_ple', 'w_ple_gate', 'w_ple', 'ln_final']
TWIN_DIFF_INPUT = 'x'
TWIN_INPUTS = ['x', 'p', 'ln_mix', 'w_in', 'pool_w', 'pool_scale', 'w_out', 'ln_ffn', 'w_up', 'conv_w', 'conv_b', 'w_down', 'ln_ple', 'w_ple_gate', 'w_ple', 'ln_final', 'loss_target', 'm_ln_mix', 'm_w_in', 'm_pool_w', 'm_pool_scale', 'm_w_out', 'm_ln_ffn', 'm_w_up', 'm_conv_w', 'm_conv_b', 'm_w_down', 'm_ln_ple', 'm_w_ple_gate', 'm_w_ple', 'm_ln_final', 'v_ln_mix', 'v_w_in', 'v_pool_w', 'v_pool_scale', 'v_w_out', 'v_ln_ffn', 'v_w_up', 'v_conv_w', 'v_conv_b', 'v_w_down', 'v_ln_ple', 'v_w_ple_gate', 'v_w_ple', 'v_ln_final']
TWIN_OUTPUTS = ['loss', 'grad_x', 'grad_ln_mix', 'grad_w_in', 'grad_pool_w', 'grad_pool_scale', 'grad_w_out', 'grad_ln_ffn', 'grad_w_up', 'grad_conv_w', 'grad_conv_b', 'grad_w_down', 'grad_ln_ple', 'grad_w_ple_gate', 'grad_w_ple', 'grad_ln_final', 'delta_ln_mix', 'delta_w_in', 'delta_pool_w', 'delta_pool_scale', 'delta_w_out', 'delta_ln_ffn', 'delta_w_up', 'delta_conv_w', 'delta_conv_b', 'delta_w_down', 'delta_ln_ple', 'delta_w_ple_gate', 'delta_w_ple', 'delta_ln_final', 'new_m_ln_mix', 'new_m_w_in', 'new_m_pool_w', 'new_m_pool_scale', 'new_m_w_out', 'new_m_ln_ffn', 'new_m_w_up', 'new_m_conv_w', 'new_m_conv_b', 'new_m_w_down', 'new_m_ln_ple', 'new_m_w_ple_gate', 'new_m_w_ple', 'new_m_ln_final', 'new_v_ln_mix', 'new_v_w_in', 'new_v_pool_w', 'new_v_pool_scale', 'new_v_w_out', 'new_v_ln_ffn', 'new_v_w_up', 'new_v_conv_w', 'new_v_conv_b', 'new_v_w_down', 'new_v_ln_ple', 'new_v_w_ple_gate', 'new_v_w_ple', 'new_v_ln_final']
TWIN_LEAF_KINDS = {'loss': 'loss', 'grad_x': 'grad_x', 'grad_ln_mix': 'grad_w', 'grad_w_in': 'grad_w', 'grad_pool_w': 'grad_w', 'grad_pool_scale': 'grad_w', 'grad_w_out': 'grad_w', 'grad_ln_ffn': 'grad_w', 'grad_w_up': 'grad_w', 'grad_conv_w': 'grad_w', 'grad_conv_b': 'grad_w', 'grad_w_down': 'grad_w', 'grad_ln_ple': 'grad_w', 'grad_w_ple_gate': 'grad_w', 'grad_w_ple': 'grad_w', 'grad_ln_final': 'grad_w', 'delta_ln_mix': 'delta_w', 'delta_w_in': 'delta_w', 'delta_pool_w': 'delta_w', 'delta_pool_scale': 'delta_w', 'delta_w_out': 'delta_w', 'delta_ln_ffn': 'delta_w', 'delta_w_up': 'delta_w', 'delta_conv_w': 'delta_w', 'delta_conv_b': 'delta_w', 'delta_w_down': 'delta_w', 'delta_ln_ple': 'delta_w', 'delta_w_ple_gate': 'delta_w', 'delta_w_ple': 'delta_w', 'delta_ln_final': 'delta_w', 'new_m_ln_mix': 'new_m', 'new_m_w_in': 'new_m', 'new_m_pool_w': 'new_m', 'new_m_pool_scale': 'new_m', 'new_m_w_out': 'new_m', 'new_m_ln_ffn': 'new_m', 'new_m_w_up': 'new_m', 'new_m_conv_w': 'new_m', 'new_m_conv_b': 'new_m', 'new_m_w_down': 'new_m', 'new_m_ln_ple': 'new_m', 'new_m_w_ple_gate': 'new_m', 'new_m_w_ple': 'new_m', 'new_m_ln_final': 'new_m', 'new_v_ln_mix': 'new_v', 'new_v_w_in': 'new_v', 'new_v_pool_w': 'new_v', 'new_v_pool_scale': 'new_v', 'new_v_w_out': 'new_v', 'new_v_ln_ffn': 'new_v', 'new_v_w_up': 'new_v', 'new_v_conv_w': 'new_v', 'new_v_conv_b': 'new_v', 'new_v_w_down': 'new_v', 'new_v_ln_ple': 'new_v', 'new_v_w_ple_gate': 'new_v', 'new_v_w_ple': 'new_v', 'new_v_ln_final': 'new_v'}


def _forward(args):
    return _fwd_reference(*[args[k] for k in FWD_PARAMS])


def _output_shape():
    out = _jax.eval_shape(lambda: _forward(_fwd_setup_inputs(0)))
    return out.shape, out.dtype

N_MICROBATCH = 1
ADAM_LR = 0.001
ADAM_B1 = 0.9
ADAM_B2 = 0.999
ADAM_EPS = 1e-08
ADAM_WD = 0.01
ADAM_STEP = 10
PER_EXAMPLE_BATCH_AXIS = {'x': 0, 'p': 1, 'loss_target': 0}
SHARED_INPUTS = []
_WEIGHT_DTYPES = {'ln_mix': _jnp.float32, 'w_in': _jnp.float32, 'pool_w': _jnp.float32, 'pool_scale': _jnp.float32, 'w_out': _jnp.float32, 'ln_ffn': _jnp.float32, 'w_up': _jnp.float32, 'conv_w': _jnp.float32, 'conv_b': _jnp.float32, 'w_down': _jnp.float32, 'ln_ple': _jnp.float32, 'w_ple_gate': _jnp.float32, 'w_ple': _jnp.float32, 'ln_final': _jnp.float32}
MOMENT_SCALE = {'ln_mix': 1.096745e-01, 'w_in': 7.812795e-02, 'pool_w': 1.352350e-01, 'pool_scale': 1.317732e-01, 'w_out': 1.030449e-01, 'ln_ffn': 1.123988e-01, 'w_up': 4.631756e-02, 'conv_w': 4.664915e-02, 'conv_b': 4.553392e-02, 'w_down': 7.534703e-02, 'ln_ple': 2.730068e-02, 'w_ple_gate': 2.586832e-02, 'w_ple': 6.615369e-02, 'ln_final': 3.198148e+01}


def _to_microbatches(a, axis):
    t = _jnp.moveaxis(a, axis, 0)
    t = t.reshape((N_MICROBATCH, t.shape[0] // N_MICROBATCH) + t.shape[1:])
    return _jnp.moveaxis(t, 1, axis + 1)


def setup_inputs(seed: int = 0) -> dict:
    inp = _fwd_setup_inputs(seed)
    key = _jax.random.fold_in(_jax.random.key(seed), 7919)
    shape, _ = _output_shape()
    out = dict(inp)
    out["loss_target"] = _jax.random.normal(_jax.random.fold_in(key, 0), shape, _jnp.float32)
    for i, name in enumerate(TWIN_WEIGHTS):
        w = inp[name].astype(_jnp.float32)
        if MOMENT_SCALE is None:
            s = _jnp.sqrt(_jnp.mean(_jnp.square(w)) + 1e-30)
        else:
            s = MOMENT_SCALE[name]
        km, kv = _jax.random.split(_jax.random.fold_in(key, i + 1))
        out[name] = w
        out["m_" + name] = s * _jax.random.normal(km, w.shape, _jnp.float32)
        out["v_" + name] = (s * s) * _jax.random.uniform(kv, w.shape, _jnp.float32, 0.5, 1.5)
    if N_MICROBATCH > 1:
        for name, axis in PER_EXAMPLE_BATCH_AXIS.items():
            out[name] = _to_microbatches(out[name], axis)
    return {'x': out['x'], 'p': out['p'], 'ln_mix': out['ln_mix'], 'w_in': out['w_in'], 'pool_w': out['pool_w'], 'pool_scale': out['pool_scale'], 'w_out': out['w_out'], 'ln_ffn': out['ln_ffn'], 'w_up': out['w_up'], 'conv_w': out['conv_w'], 'conv_b': out['conv_b'], 'w_down': out['w_down'], 'ln_ple': out['ln_ple'], 'w_ple_gate': out['w_ple_gate'], 'w_ple': out['w_ple'], 'ln_final': out['ln_final'], 'loss_target': out['loss_target'], 'm_ln_mix': out['m_ln_mix'], 'm_w_in': out['m_w_in'], 'm_pool_w': out['m_pool_w'], 'm_pool_scale': out['m_pool_scale'], 'm_w_out': out['m_w_out'], 'm_ln_ffn': out['m_ln_ffn'], 'm_w_up': out['m_w_up'], 'm_conv_w': out['m_conv_w'], 'm_conv_b': out['m_conv_b'], 'm_w_down': out['m_w_down'], 'm_ln_ple': out['m_ln_ple'], 'm_w_ple_gate': out['m_w_ple_gate'], 'm_w_ple': out['m_w_ple'], 'm_ln_final': out['m_ln_final'], 'v_ln_mix': out['v_ln_mix'], 'v_w_in': out['v_w_in'], 'v_pool_w': out['v_pool_w'], 'v_pool_scale': out['v_pool_scale'], 'v_w_out': out['v_w_out'], 'v_ln_ffn': out['v_ln_ffn'], 'v_w_up': out['v_w_up'], 'v_conv_w': out['v_conv_w'], 'v_conv_b': out['v_conv_b'], 'v_w_down': out['v_w_down'], 'v_ln_ple': out['v_ln_ple'], 'v_w_ple_gate': out['v_w_ple_gate'], 'v_w_ple': out['v_w_ple'], 'v_ln_final': out['v_ln_final']}


def _loss(weights, diff, rest, loss_target):
    with _jax.named_scope("forward"):
        args = {**rest, TWIN_DIFF_INPUT: diff, **{k: w.astype(_WEIGHT_DTYPES[k]) for k, w in weights.items()}}
        y = _forward(args)
    with _jax.named_scope("loss_head"):
        err = _jnp.square(y.astype(_jnp.float32) - loss_target)
        return 0.5 * _jnp.sum(_jnp.mean(err, axis=-1)) if err.ndim else 0.5 * err


def _adamw(w, g, m, v):
    m = ADAM_B1 * m + (1.0 - ADAM_B1) * g
    v = ADAM_B2 * v + (1.0 - ADAM_B2) * _jnp.square(g)
    m_hat = m / (1.0 - ADAM_B1 ** ADAM_STEP)
    v_hat = v / (1.0 - ADAM_B2 ** ADAM_STEP)
    delta = -ADAM_LR * (m_hat / (_jnp.sqrt(v_hat) + ADAM_EPS) + ADAM_WD * w)
    return delta, m, v


def reference(x, p, ln_mix, w_in, pool_w, pool_scale, w_out, ln_ffn, w_up, conv_w, conv_b, w_down, ln_ple, w_ple_gate, w_ple, ln_final, loss_target, m_ln_mix, m_w_in, m_pool_w, m_pool_scale, m_w_out, m_ln_ffn, m_w_up, m_conv_w, m_conv_b, m_w_down, m_ln_ple, m_w_ple_gate, m_w_ple, m_ln_final, v_ln_mix, v_w_in, v_pool_w, v_pool_scale, v_w_out, v_ln_ffn, v_w_up, v_conv_w, v_conv_b, v_w_down, v_ln_ple, v_w_ple_gate, v_w_ple, v_ln_final):
    given = dict(x=x, p=p, ln_mix=ln_mix, w_in=w_in, pool_w=pool_w, pool_scale=pool_scale, w_out=w_out, ln_ffn=ln_ffn, w_up=w_up, conv_w=conv_w, conv_b=conv_b, w_down=w_down, ln_ple=ln_ple, w_ple_gate=w_ple_gate, w_ple=w_ple, ln_final=ln_final, loss_target=loss_target, m_ln_mix=m_ln_mix, m_w_in=m_w_in, m_pool_w=m_pool_w, m_pool_scale=m_pool_scale, m_w_out=m_w_out, m_ln_ffn=m_ln_ffn, m_w_up=m_w_up, m_conv_w=m_conv_w, m_conv_b=m_conv_b, m_w_down=m_w_down, m_ln_ple=m_ln_ple, m_w_ple_gate=m_w_ple_gate, m_w_ple=m_w_ple, m_ln_final=m_ln_final, v_ln_mix=v_ln_mix, v_w_in=v_w_in, v_pool_w=v_pool_w, v_pool_scale=v_pool_scale, v_w_out=v_w_out, v_ln_ffn=v_ln_ffn, v_w_up=v_w_up, v_conv_w=v_conv_w, v_conv_b=v_conv_b, v_w_down=v_w_down, v_ln_ple=v_ln_ple, v_w_ple_gate=v_w_ple_gate, v_w_ple=v_w_ple, v_ln_final=v_ln_final)
    weights = {n: given[n] for n in TWIN_WEIGHTS}
    shared = {n: given[n] for n in SHARED_INPUTS}
    per_example = {n: given[n] for n in ['x', 'p']}
    grad_fn = _jax.value_and_grad(_loss, argnums=(0, 1))

    def one_microbatch(ex, loss_target):
        ex = dict(ex)
        diff = ex.pop(TWIN_DIFF_INPUT)
        return grad_fn(weights, diff, {**shared, **ex}, loss_target)

    if N_MICROBATCH == 1:
        loss, (grad_w, grad_x) = one_microbatch(per_example, given["loss_target"])
    else:
        def body(carry, xs):
            loss_sum, grad_sum = carry
            l_k, (gw_k, gx_k) = one_microbatch(xs[0], xs[1])
            with _jax.named_scope("update"):
                return (loss_sum + l_k, _jax.tree.map(_jnp.add, grad_sum, gw_k)), gx_k

        init = (_jnp.zeros((), _jnp.float32), _jax.tree.map(_jnp.zeros_like, weights))
        (loss, grad_w), grad_x = _jax.lax.scan(body, init, (per_example, given["loss_target"]))
    with _jax.named_scope("update"):
        delta_w, new_m, new_v = {}, {}, {}
        for n in TWIN_WEIGHTS:
            delta_w[n], new_m[n], new_v[n] = _adamw(weights[n], grad_w[n], given["m_" + n], given["v_" + n])
    return (loss, grad_x, *[grad_w[n] for n in TWIN_WEIGHTS], *[delta_w[n] for n in TWIN_WEIGHTS],
            *[new_m[n] for n in TWIN_WEIGHTS], *[new_v[n] for n in TWIN_WEIGHTS])
```

```python
import functools

import jax
import jax.numpy as jnp
from jax import lax
from jax.experimental import pallas as pl
from jax.experimental.pallas import tpu as pltpu

F32 = jnp.float32
BF16 = jnp.bfloat16

N_DEV = 8
D_MODEL = 1024
ATT_WIDTH = 512
POOL_WIDTH = 512
N_HEADS = 8
HEAD_DIM = 64
SPAN = 128
DILATIONS = (1, 4, 16)
POOL_WINDOWS = (2, 4, 8, 16)
POOL_GROUP = 128
D_FF = 2816
FF_SHARD = 2 * D_FF // N_DEV
PLE_DIM = 256
EPS = 1e-6
NEG = -1e30

ADAM_LR = 0.001
ADAM_B1 = 0.9
ADAM_B2 = 0.999
ADAM_EPS = 1e-08
ADAM_WD = 0.01
ADAM_STEP = 10

LANES = 128
HALO = 16
TM = 512
TM_FF = 256
TK = 1024
VMEM_LIMIT = 56 * 1024 * 1024

MESH = pl.DeviceIdType.MESH
NT = (((1,), (1,)), ((), ()))
TN = (((0,), (0,)), ((), ()))


def _params(*sem):
    return pltpu.CompilerParams(dimension_semantics=sem or None, vmem_limit_bytes=VMEM_LIMIT)


def _const(shape):
    n = len(shape)
    return pl.BlockSpec(shape, lambda *_: (0,) * n, pipeline_mode=pl.Buffered(1))


def _rms(h):
    r = lax.rsqrt(jnp.mean(h * h, axis=-1, keepdims=True) + EPS)
    return r, h * r


def _rms_bwd(r, n, g, dhn):
    dn = dhn * g
    return r * (dn - n * jnp.mean(dn * n, axis=-1, keepdims=True))


def _colsum(a):
    return jnp.sum(a, axis=0, keepdims=True)


def _exchange(name, arrays, scatter):
    n = len(arrays)
    out_shapes = []
    for a, s in zip(arrays, scatter):
        slab = a.shape[1:] if s else a.shape
        out_shapes.append(jax.ShapeDtypeStruct((N_DEV,) + tuple(slab), a.dtype))

    def body(*refs):
        ins, outs = refs[:n], refs[n:2 * n]
        send_sems, recv_sems, local_sems = refs[2 * n:]
        x, y, c = lax.axis_index("x"), lax.axis_index("y"), lax.axis_index("c")
        me = 4 * x + 2 * y + c
        copies = []
        for a in range(n):
            src = ins[a].at[me] if scatter[a] else ins[a]
            cp = pltpu.make_async_copy(src, outs[a].at[me], local_sems.at[a])
            cp.start()
            copies.append(cp)
        for k in range(1, N_DEV):
            px = 1 - x if k & 4 else x
            py = 1 - y if k & 2 else y
            pc = 1 - c if k & 1 else c
            pid = 4 * px + 2 * py + pc
            for a in range(n):
                src = ins[a].at[pid] if scatter[a] else ins[a]
                cp = pltpu.make_async_remote_copy(
                    src_ref=src, dst_ref=outs[a].at[me],
                    send_sem=send_sems.at[a * (N_DEV - 1) + k - 1], recv_sem=recv_sems.at[a * (N_DEV - 1) + k - 1],
                    device_id=(px, py, pc), device_id_type=MESH)
                cp.start()
                copies.append(cp)
        for cp in copies:
            cp.wait()

    any_spec = pl.BlockSpec(memory_space=pl.ANY)
    return pl.pallas_call(
        body, name=name,
        in_specs=[any_spec] * n, out_specs=[any_spec] * n, out_shape=out_shapes,
        scratch_shapes=[pltpu.SemaphoreType.DMA((n * (N_DEV - 1),)), pltpu.SemaphoreType.DMA((n * (N_DEV - 1),)),
                        pltpu.SemaphoreType.DMA((n,))],
    )(*arrays)


def _qkvu(x, g1, w_in):
    S = x.shape[0]

    def body(x_ref, g_ref, w_ref, q_ref, k_ref, v_ref, u_ref, hn_ref):
        r, n = _rms(x_ref[...])
        hn = (n * g_ref[...]).astype(BF16)
        hn_ref[...] = hn
        outs = (q_ref, k_ref, v_ref, u_ref)
        for j in range(N_DEV):
            z = jnp.dot(hn, w_ref[j], preferred_element_type=F32)
            if j < 2:
                z = z * (HEAD_DIM ** -0.5)
            outs[j // 2][:, (j % 2) * 256:(j % 2 + 1) * 256] = z

    tok = lambda w: pl.BlockSpec((TM, w), lambda i: (i, 0))
    return pl.pallas_call(
        body, name="qkvu", grid=(S // TM,),
        in_specs=[tok(D_MODEL), _const((1, D_MODEL)), _const(w_in.shape)],
        out_specs=[tok(512)] * 4 + [tok(D_MODEL)],
        out_shape=[jax.ShapeDtypeStruct((S, 512), F32)] * 4 + [jax.ShapeDtypeStruct((S, D_MODEL), BF16)],
        compiler_params=_params("arbitrary"),
    )(x, g1, w_in)


def _attn_bias(slope_ref, hp, d):
    qi = lax.broadcasted_iota(jnp.int32, (SPAN, 2 * SPAN), 0)
    kj = lax.broadcasted_iota(jnp.int32, (SPAN, 2 * SPAN), 1)
    diff = qi + SPAN - kj
    valid = (diff >= 0) & (diff <= SPAN)
    dist = diff.astype(F32) * float(d)
    return [jnp.where(valid, -slope_ref[2 * hp + h] * dist, NEG) for h in range(2)]


def _attn_rows(r, n, d, nrows):
    start = n * (SPAN * d) + r
    return pl.ds(start, nrows, stride=d) if d > 1 else pl.ds(start, nrows)


def _attn_loops(S, d, blk):
    nb = S // d // SPAN

    def r_body(r, carry):
        blk(r, 0, True)

        def n_body(n, c2):
            blk(r, n, False)
            return c2

        lax.fori_loop(1, nb, n_body, 0)
        return carry

    if d == 1:
        r_body(0, 0)
    else:
        lax.fori_loop(0, d, r_body, 0)


def _attn_fwd(slopes, q, k, v):
    S = q.shape[0]

    def body(slope_ref, q_ref, k_ref, v_ref, o_ref, lse_ref, m_s, l_s):
        hp = pl.program_id(0)
        is0 = lax.broadcasted_iota(jnp.int32, (SPAN, LANES), 1) < HEAD_DIM
        for pi, d in enumerate(DILATIONS):
            bias = _attn_bias(slope_ref, hp, d)
            last = pi == len(DILATIONS) - 1

            def blk(r, n, first, d=d, pi=pi, bias=bias, last=last):
                rows = _attn_rows(r, n, d, SPAN)
                krows = rows if first else _attn_rows(r, n - 1, d, 2 * SPAN)
                qb = q_ref[rows, :]
                kb = k_ref[krows, :].astype(BF16)
                vb = v_ref[krows, :].astype(BF16)
                parts = []
                for h, is_h in enumerate((is0, ~is0)):
                    b = bias[h][:, SPAN:] if first else bias[h]
                    qh = jnp.where(is_h, qb, 0.0).astype(BF16)
                    s = lax.dot_general(qh, kb, NT, preferred_element_type=F32) + b
                    m = jnp.max(s, axis=-1, keepdims=True)
                    e = jnp.exp(s - m)
                    l = jnp.sum(e, axis=-1, keepdims=True)
                    pv = jnp.dot(e.astype(BF16), vb, preferred_element_type=F32)
                    parts.append((m, l, pv))
                m_b = jnp.where(is0, parts[0][0], parts[1][0])
                l_b = jnp.where(is0, parts[0][1], parts[1][1])
                acc = jnp.where(is0, parts[0][2], parts[1][2])
                if pi > 0:
                    m_o = m_s[rows, :]
                    m_n = jnp.maximum(m_o, m_b)
                    a_o = jnp.exp(m_o - m_n)
                    a_b = jnp.exp(m_b - m_n)
                    l_b = a_o * l_s[rows, :] + a_b * l_b
                    acc = a_o * o_ref[rows, :] + a_b * acc
                    m_b = m_n
                if last:
                    o_ref[rows, :] = acc / l_b
                    lse_ref[rows, :] = m_b + jnp.log(l_b)
                else:
                    o_ref[rows, :] = acc
                    m_s[rows, :] = m_b
                    l_s[rows, :] = l_b

            _attn_loops(S, d, blk)

    col = pl.BlockSpec((S, LANES), lambda i: (0, i))
    return pl.pallas_call(
        body, name="attn_fwd", grid=(ATT_WIDTH // LANES,),
        in_specs=[pl.BlockSpec(memory_space=pltpu.SMEM), col, col, col],
        out_specs=[col, col],
        out_shape=[jax.ShapeDtypeStruct((S, ATT_WIDTH), F32)] * 2,
        scratch_shapes=[pltpu.VMEM((S, LANES), F32), pltpu.VMEM((S, LANES), F32)],
        compiler_params=_params("arbitrary"),
    )(slopes, q, k, v)


def _pool_count(i, w):
    t = i * TM + lax.broadcasted_iota(jnp.int32, (TM, 1), 0)
    return jnp.minimum(t + 1, w).astype(F32)


def _mix_out(x, att, u, pool_w, pool_scale, w_out):
    S = x.shape[0]

    def body(x_ref, att_ref, u_ref, pw_ref, ps_ref, w_ref, h1_ref, mix_ref, dlt_ref, ubuf):
        i = pl.program_id(0)

        @pl.when(i == 0)
        def _():
            ubuf[0:HALO, :] = jnp.zeros((HALO, POOL_WIDTH), F32)

        ubuf[HALO:HALO + TM, :] = u_ref[...]
        mix_ref[:, 0:ATT_WIDTH] = att_ref[...].astype(BF16)
        for g, w in enumerate(POOL_WINDOWS):
            cols = slice(g * POOL_GROUP, (g + 1) * POOL_GROUP)
            ug = ubuf[HALO:HALO + TM, cols]
            acc = ug
            for j in range(1, w):
                acc = acc + ubuf[HALO - j:HALO - j + TM, cols]
            dlt = (acc / _pool_count(i, w) - ug).astype(BF16)
            dlt_ref[:, cols] = dlt
            yg = jnp.dot(dlt, pw_ref[g].astype(BF16), preferred_element_type=F32) * ps_ref[:, cols]
            mix_ref[:, ATT_WIDTH + g * POOL_GROUP:ATT_WIDTH + (g + 1) * POOL_GROUP] = yg.astype(BF16)
        ubuf[0:HALO, :] = ubuf[TM:TM + HALO, :]
        h1_ref[...] = x_ref[...] + jnp.dot(mix_ref[...], w_ref[...], preferred_element_type=F32)

    tok = lambda w: pl.BlockSpec((TM, w), lambda i: (i, 0))
    return pl.pallas_call(
        body, name="mix_out", grid=(S // TM,),
        in_specs=[tok(D_MODEL), tok(ATT_WIDTH), tok(POOL_WIDTH), _const(pool_w.shape), _const((1, POOL_WIDTH)),
                  _const(w_out.shape)],
        out_specs=[tok(D_MODEL), tok(D_MODEL), tok(POOL_WIDTH)],
        out_shape=[jax.ShapeDtypeStruct((S, D_MODEL), F32), jax.ShapeDtypeStruct((S, D_MODEL), BF16),
                   jax.ShapeDtypeStruct((S, POOL_WIDTH), BF16)],
        scratch_shapes=[pltpu.VMEM((TM + HALO, POOL_WIDTH), F32)],
        compiler_params=_params("arbitrary"),
    )(x, att, u, pool_w, pool_scale, w_out)


def _conv_fwd(stage, upre, prev, cw, cb):
    T = upre.shape[0]
    stage[0:HALO, :] = prev
    stage[HALO:HALO + T, :] = upre
    return cb + cw[0:1, :] * stage[HALO - 2:HALO - 2 + T, :] + cw[1:2, :] * stage[HALO - 1:HALO - 1 + T, :] + cw[2:3, :] * upre


def _ffn_fwd(h1, g2, w_up, conv_w, conv_b, w_down):
    S = h1.shape[0]
    T = TM_FF

    def body(h1_ref, g_ref, wu_ref, cw_ref, cb_ref, wd_ref, h2_ref, hn_ref, up_ref, carry, stage):
        i = pl.program_id(0)

        @pl.when(i == 0)
        def _():
            carry[...] = jnp.zeros(carry.shape, F32)

        h1t = h1_ref[...]
        r, n = _rms(h1t)
        hn = (n * g_ref[...]).astype(BF16)
        hn_ref[...] = hn
        acc = h1t
        for j in range(4):
            conv = []
            for jj in (j, j + 4):
                upre = jnp.dot(hn, wu_ref[jj], preferred_element_type=F32)
                up_ref[jj] = upre.astype(BF16)
                conv.append(_conv_fwd(stage, upre, carry[jj], cw_ref[jj], cb_ref[jj]))
                carry[jj] = stage[T:T + HALO, :]
            gate, val = conv
            a = gate * jax.nn.sigmoid(gate) * val
            acc = acc + jnp.dot(a.astype(BF16), wd_ref[j], preferred_element_type=F32)
        h2_ref[...] = acc

    tok = lambda w: pl.BlockSpec((T, w), lambda i: (i, 0))
    return pl.pallas_call(
        body, name="ffn_fwd", grid=(S // T,),
        in_specs=[tok(D_MODEL), _const((1, D_MODEL)), _const(w_up.shape), _const(conv_w.shape), _const(conv_b.shape),
                  _const(w_down.shape)],
        out_specs=[tok(D_MODEL), tok(D_MODEL), pl.BlockSpec((N_DEV, T, FF_SHARD), lambda i: (0, i, 0))],
        out_shape=[jax.ShapeDtypeStruct((S, D_MODEL), F32), jax.ShapeDtypeStruct((S, D_MODEL), BF16),
                   jax.ShapeDtypeStruct((N_DEV, S, FF_SHARD), BF16)],
        scratch_shapes=[pltpu.VMEM((N_DEV, HALO, FF_SHARD), F32), pltpu.VMEM((T + HALO, FF_SHARD), F32)],
        compiler_params=_params("arbitrary"),
    )(h1, g2, w_up, conv_w, conv_b, w_down)


def _head(h2, p, g3, w_pg, w_ple, g4, target):
    S = h2.shape[0]
    nt = S // TM

    def body(h2_ref, p_ref, g3_ref, wpg_ref, wple_ref, g4_ref, t_ref,
             loss_ref, dh2_ref, dh2b_ref, hn3_ref, dgl_ref, dpe_ref, dg3_ref, dg4_ref, lacc):
        i = pl.program_id(0)

        @pl.when(i == 0)
        def _():
            lacc[...] = jnp.zeros(lacc.shape, F32)
            dg3_ref[...] = jnp.zeros(dg3_ref.shape, F32)
            dg4_ref[...] = jnp.zeros(dg4_ref.shape, F32)

        h2t = h2_ref[...]
        g3, g4 = g3_ref[...], g4_ref[...]
        r3, n3 = _rms(h2t)
        hn3 = (n3 * g3).astype(BF16)
        hn3_ref[...] = hn3
        gs = jax.nn.sigmoid(jnp.dot(hn3, wpg_ref[...], preferred_element_type=F32))
        pe = jnp.dot(p_ref[...].astype(BF16), wple_ref[...], preferred_element_type=F32)
        h3 = h2t + gs * pe
        r4, n4 = _rms(h3)
        err = n4 * g4 - t_ref[...]
        lacc[...] += _colsum(err * err)
        dy = err * (1.0 / D_MODEL)
        dg4_ref[...] += _colsum(dy * n4)
        dh3 = _rms_bwd(r4, n4, g4, dy)
        dpe_ref[...] = (dh3 * gs).astype(BF16)
        dgl = (dh3 * pe * gs * (1.0 - gs)).astype(BF16)
        dgl_ref[...] = dgl
        dhn3 = lax.dot_general(dgl, wpg_ref[...], NT, preferred_element_type=F32)
        dg3_ref[...] += _colsum(dhn3 * n3)
        dh2 = dh3 + _rms_bwd(r3, n3, g3, dhn3)
        dh2_ref[...] = dh2
        dh2b_ref[...] = dh2.astype(BF16)

        @pl.when(i == nt - 1)
        def _():
            tot = 0.5 / D_MODEL * jnp.sum(lacc[...], axis=-1, keepdims=True)
            loss_ref[...] = jnp.broadcast_to(tot, loss_ref.shape)

    tok = lambda w: pl.BlockSpec((TM, w), lambda i: (i, 0))
    row = pl.BlockSpec((1, D_MODEL), lambda i: (0, 0))
    act = lambda dt: jax.ShapeDtypeStruct((S, D_MODEL), dt)
    return pl.pallas_call(
        body, name="head", grid=(nt,),
        in_specs=[tok(D_MODEL), tok(PLE_DIM), _const((1, D_MODEL)), _const(w_pg.shape), _const(w_ple.shape),
                  _const((1, D_MODEL)), tok(D_MODEL)],
        out_specs=[pl.BlockSpec((8, LANES), lambda i: (0, 0)), tok(D_MODEL), tok(D_MODEL), tok(D_MODEL), tok(D_MODEL),
                   tok(D_MODEL), row, row],
        out_shape=[jax.ShapeDtypeStruct((8, LANES), F32), act(F32), act(BF16), act(BF16), act(BF16), act(BF16),
                   jax.ShapeDtypeStruct((1, D_MODEL), F32), jax.ShapeDtypeStruct((1, D_MODEL), F32)],
        scratch_shapes=[pltpu.VMEM((1, D_MODEL), F32)],
        compiler_params=_params("arbitrary"),
    )(h2, p, g3, w_pg, w_ple, g4, target)


def _wgrad(name, x, dy, x_kind, dy_kind, nj, k_dim, n_dim):
    S = x.shape[-2]
    nt = S // TK

    def spec(kind, width):
        if kind == "full":
            return pl.BlockSpec((TK, width), lambda j, t: (t, 0))
        if kind == "cols":
            return pl.BlockSpec((TK, width), lambda j, t: (t, j))
        return pl.BlockSpec((None, TK, width), lambda j, t: (j, t, 0))

    def body(x_ref, dy_ref, o_ref):
        @pl.when(pl.program_id(1) == 0)
        def _():
            o_ref[...] = jnp.zeros(o_ref.shape, F32)

        o_ref[...] += lax.dot_general(x_ref[...].astype(BF16), dy_ref[...], TN, preferred_element_type=F32)

    return pl.pallas_call(
        body, name=name, grid=(nj, nt),
        in_specs=[spec(x_kind, k_dim), spec(dy_kind, n_dim)],
        out_specs=pl.BlockSpec((None, k_dim, n_dim), lambda j, t: (j, 0, 0)),
        out_shape=jax.ShapeDtypeStruct((nj, k_dim, n_dim), F32),
        compiler_params=_params("arbitrary", "arbitrary"),
    )(x, dy)


def _ffn_bwd_a(dh2b, up, w_down, conv_w, conv_b):
    S = dh2b.shape[0]
    T = TM_FF
    hb = T // HALO

    def body(dh_ref, up_ref, halo_ref, wd_ref, cw_ref, cb_ref, a_ref, dup_ref, dcw_ref, dcb_ref, stage):
        i = pl.program_id(0)

        @pl.when(i == 0)
        def _():
            dcw_ref[...] = jnp.zeros(dcw_ref.shape, F32)
            dcb_ref[...] = jnp.zeros(dcb_ref.shape, F32)

        dh = dh_ref[...]
        for j in range(4):
            da = lax.dot_general(dh, wd_ref[j], NT, preferred_element_type=F32)
            conv, taps = [], []
            for jj in (j, j + 4):
                upre = up_ref[jj].astype(F32)
                prev = jnp.where(i > 0, halo_ref[jj].astype(F32), 0.0)
                conv.append(_conv_fwd(stage, upre, prev, cw_ref[jj], cb_ref[jj]))
                taps.append((stage[HALO - 2:HALO - 2 + T, :], stage[HALO - 1:HALO - 1 + T, :], upre))
            gate, val = conv
            sg = jax.nn.sigmoid(gate)
            silu = gate * sg
            a_ref[j] = (silu * val).astype(BF16)
            dgate = da * val * (sg * (1.0 + gate * (1.0 - sg)))
            dval = da * silu
            for jj, dup, tp in ((j, dgate, taps[0]), (j + 4, dval, taps[1])):
                dup_ref[jj] = dup.astype(BF16)
                dcb_ref[jj] += _colsum(dup)
                for kk in range(3):
                    dcw_ref[jj, kk:kk + 1, :] += _colsum(dup * tp[kk])

    tok = lambda w: pl.BlockSpec((T, w), lambda i: (i, 0))
    shard = pl.BlockSpec((N_DEV, T, FF_SHARD), lambda i: (0, i, 0))
    return pl.pallas_call(
        body, name="ffn_bwd_a", grid=(S // T,),
        in_specs=[tok(D_MODEL), shard,
                  pl.BlockSpec((N_DEV, HALO, FF_SHARD), lambda i: (0, jnp.maximum(i * hb - 1, 0), 0)),
                  _const(w_down.shape), _const(conv_w.shape), _const(conv_b.shape)],
        out_specs=[pl.BlockSpec((4, T, FF_SHARD), lambda i: (0, i, 0)), shard,
                   pl.BlockSpec((N_DEV, 3, FF_SHARD), lambda i: (0, 0, 0)),
                   pl.BlockSpec((N_DEV, 1, FF_SHARD), lambda i: (0, 0, 0))],
        out_shape=[jax.ShapeDtypeStruct((4, S, FF_SHARD), BF16), jax.ShapeDtypeStruct((N_DEV, S, FF_SHARD), BF16),
                   jax.ShapeDtypeStruct((N_DEV, 3, FF_SHARD), F32), jax.ShapeDtypeStruct((N_DEV, 1, FF_SHARD), F32)],
        scratch_shapes=[pltpu.VMEM((T + HALO, FF_SHARD), F32)],
        compiler_params=_params("arbitrary"),
    )(dh2b, up, up, w_down, conv_w, conv_b)


def _ffn_bwd_b(dup, conv_w, w_up, h1, g2, dh2):
    S = h1.shape[0]
    T = TM_FF
    hb = T // HALO
    nt = S // T

    def body(dup_ref, halo_ref, cw_ref, wu_ref, h1_ref, g_ref, dh2_ref, dpre_ref, dh1_ref, dh1b_ref, dg_ref, stage):
        i = pl.program_id(0)

        @pl.when(i == 0)
        def _():
            dg_ref[...] = jnp.zeros(dg_ref.shape, F32)

        dhn = jnp.zeros((T, D_MODEL), F32)
        for jj in range(N_DEV):
            dup = dup_ref[jj].astype(F32)
            stage[0:T, :] = dup
            stage[T:T + HALO, :] = jnp.where(i < nt - 1, halo_ref[jj].astype(F32), 0.0)
            cw = cw_ref[jj]
            dpre = (cw[2:3, :] * dup + cw[1:2, :] * stage[1:1 + T, :] + cw[0:1, :] * stage[2:2 + T, :]).astype(BF16)
            dpre_ref[jj] = dpre
            dhn = dhn + lax.dot_general(dpre, wu_ref[jj], NT, preferred_element_type=F32)
        g = g_ref[...]
        r, n = _rms(h1_ref[...])
        dg_ref[...] += _colsum(dhn * n)
        dh1 = dh2_ref[...] + _rms_bwd(r, n, g, dhn)
        dh1_ref[...] = dh1
        dh1b_ref[...] = dh1.astype(BF16)

    tok = lambda w: pl.BlockSpec((T, w), lambda i: (i, 0))
    shard = pl.BlockSpec((N_DEV, T, FF_SHARD), lambda i: (0, i, 0))
    return pl.pallas_call(
        body, name="ffn_bwd_b", grid=(nt,),
        in_specs=[shard,
                  pl.BlockSpec((N_DEV, HALO, FF_SHARD), lambda i: (0, jnp.minimum((i + 1) * hb, S // HALO - 1), 0)),
                  _const(conv_w.shape), _const(w_up.shape), tok(D_MODEL), _const((1, D_MODEL)), tok(D_MODEL)],
        out_specs=[shard, tok(D_MODEL), tok(D_MODEL), pl.BlockSpec((1, D_MODEL), lambda i: (0, 0))],
        out_shape=[jax.ShapeDtypeStruct((N_DEV, S, FF_SHARD), BF16), jax.ShapeDtypeStruct((S, D_MODEL), F32),
                   jax.ShapeDtypeStruct((S, D_MODEL), BF16), jax.ShapeDtypeStruct((1, D_MODEL), F32)],
        scratch_shapes=[pltpu.VMEM((T + HALO, FF_SHARD), F32)],
        compiler_params=_params("arbitrary"),
    )(dup, dup, conv_w, w_up, h1, g2, dh2)


def _mix_bwd(dh1b, w_out, dlt, pool_w, pool_scale):
    S = dh1b.shape[0]
    nt = S // TM

    def body(dh_ref, w_ref, dlt_ref, pw_ref, ps_ref, datt_ref, du_ref, dpw_ref, dps_ref, stage, carry):
        i = pl.program_id(0)
        tile = nt - 1 - i

        @pl.when(i == 0)
        def _():
            dpw_ref[...] = jnp.zeros(dpw_ref.shape, F32)
            dps_ref[...] = jnp.zeros(dps_ref.shape, F32)
            carry[...] = jnp.zeros(carry.shape, F32)

        dmix = lax.dot_general(dh_ref[...], w_ref[...], NT, preferred_element_type=F32)
        datt_ref[...] = dmix[:, 0:ATT_WIDTH]
        for g, w in enumerate(POOL_WINDOWS):
            cols = slice(g * POOL_GROUP, (g + 1) * POOL_GROUP)
            dpool = dmix[:, ATT_WIDTH + g * POOL_GROUP:ATT_WIDTH + (g + 1) * POOL_GROUP]
            dl = dlt_ref[:, cols]
            pw = pw_ref[g].astype(BF16)
            yg = jnp.dot(dl, pw, preferred_element_type=F32)
            dps_ref[:, cols] += _colsum(dpool * yg)
            dy = (dpool * ps_ref[:, cols]).astype(BF16)
            dpw_ref[g] += lax.dot_general(dl, dy, TN, preferred_element_type=F32)
            ddlt = lax.dot_general(dy, pw, NT, preferred_element_type=F32)
            cg = ddlt / _pool_count(tile, w)
            stage[0:TM, :] = cg
            stage[TM:TM + HALO, :] = carry[:, cols]
            acc = cg
            for j in range(1, w):
                acc = acc + stage[j:j + TM, :]
            du_ref[:, cols] = acc - ddlt
            carry[:, cols] = cg[0:HALO, :]

    tok = lambda w: pl.BlockSpec((TM, w), lambda i: (nt - 1 - i, 0))
    return pl.pallas_call(
        body, name="mix_bwd", grid=(nt,),
        in_specs=[tok(D_MODEL), _const(w_out.shape), tok(POOL_WIDTH), _const(pool_w.shape), _const((1, POOL_WIDTH))],
        out_specs=[tok(ATT_WIDTH), tok(POOL_WIDTH), pl.BlockSpec(pool_w.shape, lambda i: (0, 0, 0)),
                   pl.BlockSpec((1, POOL_WIDTH), lambda i: (0, 0))],
        out_shape=[jax.ShapeDtypeStruct((S, ATT_WIDTH), F32), jax.ShapeDtypeStruct((S, POOL_WIDTH), F32),
                   jax.ShapeDtypeStruct(pool_w.shape, F32), jax.ShapeDtypeStruct((1, POOL_WIDTH), F32)],
        scratch_shapes=[pltpu.VMEM((TM + HALO, POOL_GROUP), F32), pltpu.VMEM((HALO, POOL_WIDTH), F32)],
        compiler_params=_params("arbitrary"),
    )(dh1b, w_out, dlt, pool_w, pool_scale)


def _attn_bwd(slopes, q, k, v, o, lse, do):
    S = q.shape[0]
    CH = 512

    def body(slope_ref, q_ref, k_ref, v_ref, o_ref, lse_ref, do_ref, dq_ref, dk_ref, dv_ref, dl_s):
        hp = pl.program_id(0)
        is0 = lax.broadcasted_iota(jnp.int32, (SPAN, LANES), 1) < HEAD_DIM
        is0c = lax.broadcasted_iota(jnp.int32, (CH, LANES), 1) < HEAD_DIM
        is0k = lax.broadcasted_iota(jnp.int32, (2 * SPAN, LANES), 1) < HEAD_DIM

        def prep(ci, carry):
            rows = pl.ds(pl.multiple_of(ci * CH, CH), CH)
            prod = do_ref[rows, :] * o_ref[rows, :]
            d0 = jnp.sum(jnp.where(is0c, prod, 0.0), axis=-1, keepdims=True)
            d1 = jnp.sum(jnp.where(is0c, 0.0, prod), axis=-1, keepdims=True)
            dl_s[rows, :] = jnp.where(is0c, d0, d1)
            zero = jnp.zeros((CH, LANES), F32)
            dq_ref[rows, :] = zero
            dk_ref[rows, :] = zero
            dv_ref[rows, :] = zero
            return carry

        lax.fori_loop(0, S // CH, prep, 0)

        for d in DILATIONS:
            bias = _attn_bias(slope_ref, hp, d)

            def blk(r, n, first, d=d, bias=bias):
                rows = _attn_rows(r, n, d, SPAN)
                krows = rows if first else _attn_rows(r, n - 1, d, 2 * SPAN)
                qb = q_ref[rows, :]
                dob = do_ref[rows, :]
                lse_b = lse_ref[rows, :]
                dl_b = dl_s[rows, :]
                kf = k_ref[krows, :]
                kb = kf.astype(BF16)
                vb = v_ref[krows, :].astype(BF16)
                dq_c = jnp.zeros((SPAN, LANES), F32)
                dk_c = jnp.zeros(kf.shape, F32)
                dv_c = jnp.zeros(kf.shape, F32)
                for h, is_h in enumerate((is0, ~is0)):
                    lo = h * HEAD_DIM
                    b = bias[h][:, SPAN:] if first else bias[h]
                    qh = jnp.where(is_h, qb, 0.0).astype(BF16)
                    doh = jnp.where(is_h, dob, 0.0).astype(BF16)
                    s = lax.dot_general(qh, kb, NT, preferred_element_type=F32) + b
                    pr = jnp.exp(s - lse_b[:, lo:lo + 1])
                    dp = lax.dot_general(doh, vb, NT, preferred_element_type=F32)
                    ds = (pr * (dp - dl_b[:, lo:lo + 1])).astype(BF16)
                    is_hk = is_h if first else (is0k if h == 0 else ~is0k)
                    kh = jnp.where(is_hk, kf, 0.0).astype(BF16)
                    dv_c = dv_c + lax.dot_general(pr.astype(BF16), doh, TN, preferred_element_type=F32)
                    dk_c = dk_c + lax.dot_general(ds, qh, TN, preferred_element_type=F32)
                    dq_c = dq_c + jnp.dot(ds, kh, preferred_element_type=F32)
                dq_ref[rows, :] += dq_c
                dk_ref[krows, :] += dk_c
                dv_ref[krows, :] += dv_c

            _attn_loops(S, d, blk)

    col = pl.BlockSpec((S, LANES), lambda i: (0, i))
    return pl.pallas_call(
        body, name="attn_bwd", grid=(ATT_WIDTH // LANES,),
        in_specs=[pl.BlockSpec(memory_space=pltpu.SMEM)] + [col] * 6,
        out_specs=[col] * 3,
        out_shape=[jax.ShapeDtypeStruct((S, ATT_WIDTH), F32)] * 3,
        scratch_shapes=[pltpu.VMEM((S, LANES), F32)],
        compiler_params=_params("arbitrary"),
    )(slopes, q, k, v, o, lse, do)


def _in_bwd(dq, dk, dv, du, w_in, x, g1, dh1):
    S = x.shape[0]

    def body(dq_ref, dk_ref, dv_ref, du_ref, w_ref, x_ref, g_ref, dh1_ref, dz_ref, dx_ref, dg_ref):
        @pl.when(pl.program_id(0) == 0)
        def _():
            dg_ref[...] = jnp.zeros(dg_ref.shape, F32)

        srcs = (dq_ref, dk_ref, dv_ref, du_ref)
        dhn = jnp.zeros((TM, D_MODEL), F32)
        for j in range(N_DEV):
            dz = srcs[j // 2][:, (j % 2) * 256:(j % 2 + 1) * 256]
            if j < 2:
                dz = dz * (HEAD_DIM ** -0.5)
            dz = dz.astype(BF16)
            dz_ref[j] = dz
            dhn = dhn + lax.dot_general(dz, w_ref[j], NT, preferred_element_type=F32)
        g = g_ref[...]
        r, n = _rms(x_ref[...])
        dg_ref[...] += _colsum(dhn * n)
        dx_ref[...] = dh1_ref[...] + _rms_bwd(r, n, g, dhn)

    tok = lambda w: pl.BlockSpec((TM, w), lambda i: (i, 0))
    return pl.pallas_call(
        body, name="in_bwd", grid=(S // TM,),
        in_specs=[tok(512)] * 4 + [_const(w_in.shape), tok(D_MODEL), _const((1, D_MODEL)), tok(D_MODEL)],
        out_specs=[pl.BlockSpec((N_DEV, TM, 256), lambda i: (0, i, 0)), tok(D_MODEL),
                   pl.BlockSpec((1, D_MODEL), lambda i: (0, 0))],
        out_shape=[jax.ShapeDtypeStruct((N_DEV, S, 256), BF16), jax.ShapeDtypeStruct((S, D_MODEL), F32),
                   jax.ShapeDtypeStruct((1, D_MODEL), F32)],
        compiler_params=_params("arbitrary"),
    )(dq, dk, dv, du, w_in, x, g1, dh1)


def _adamw(name, parts, w, m, v):
    R, C = w.shape
    rb = R
    for cand in (256, 128, 64, 32, 16, 8):
        if R % cand == 0 and R > cand:
            rb = cand
            break

    def body(p_ref, w_ref, m_ref, v_ref, g_ref, d_ref, mo_ref, vo_ref):
        g = p_ref[0]
        for s in range(1, N_DEV):
            g = g + p_ref[s]
        m_new = ADAM_B1 * m_ref[...] + (1.0 - ADAM_B1) * g
        v_new = ADAM_B2 * v_ref[...] + (1.0 - ADAM_B2) * (g * g)
        m_hat = m_new / (1.0 - ADAM_B1 ** ADAM_STEP)
        v_hat = v_new / (1.0 - ADAM_B2 ** ADAM_STEP)
        g_ref[...] = g
        d_ref[...] = -ADAM_LR * (m_hat / (jnp.sqrt(v_hat) + ADAM_EPS) + ADAM_WD * w_ref[...])
        mo_ref[...] = m_new
        vo_ref[...] = v_new

    blk = pl.BlockSpec((rb, C), lambda i: (i, 0))
    return pl.pallas_call(
        body, name=name, grid=(R // rb,),
        in_specs=[pl.BlockSpec((N_DEV, rb, C), lambda i: (0, i, 0)), blk, blk, blk],
        out_specs=[blk] * 4,
        out_shape=[jax.ShapeDtypeStruct((R, C), F32)] * 4,
        compiler_params=_params("arbitrary"),
    )(parts, w, m, v)


def _rows(a):
    flat = a.reshape(-1)
    rows = -(-flat.shape[0] // LANES)
    rows8 = -(-rows // 8) * 8
    flat = jnp.pad(flat, (0, rows8 * LANES - flat.shape[0]))
    return flat.reshape(rows8, LANES)


def kernel(x, p, ln_mix, w_in, pool_w, pool_scale, w_out, ln_ffn, w_up, conv_w, conv_b, w_down, ln_ple, w_ple_gate, w_ple, ln_final, loss_target, m_ln_mix, m_w_in, m_pool_w, m_pool_scale, m_w_out, m_ln_ffn, m_w_up, m_conv_w, m_conv_b, m_w_down, m_ln_ple, m_w_ple_gate, m_w_ple, m_ln_final, v_ln_mix, v_w_in, v_pool_w, v_pool_scale, v_w_out, v_ln_ffn, v_w_up, v_conv_w, v_conv_b, v_w_down, v_ln_ple, v_w_ple_gate, v_w_ple, v_ln_final):
    gathered = _exchange(
        "gather_weights",
        [w_in[0].astype(BF16), w_out[0].astype(BF16), w_up[0].astype(BF16), w_down[0].astype(BF16),
         w_ple_gate[0].astype(BF16), w_ple[0].astype(BF16), conv_w[0]],
        [False] * 7)
    loss_blk, grad_x, shard_g, rep_g = _local_step(x[0], p[0, 0], loss_target[0], ln_mix, pool_w[0], pool_scale, ln_ffn,
                                                   conv_b, ln_ple, ln_final, *gathered)

    rep_names = ("ln_mix", "pool_w", "pool_scale", "ln_ffn", "conv_b", "ln_ple", "ln_final")
    rep_w = dict(ln_mix=ln_mix, pool_w=pool_w, pool_scale=pool_scale, ln_ffn=ln_ffn, conv_b=conv_b, ln_ple=ln_ple,
                 ln_final=ln_final)
    rep_m = dict(ln_mix=m_ln_mix, pool_w=m_pool_w, pool_scale=m_pool_scale, ln_ffn=m_ln_ffn, conv_b=m_conv_b,
                 ln_ple=m_ln_ple, ln_final=m_ln_final)
    rep_v = dict(ln_mix=v_ln_mix, pool_w=v_pool_w, pool_scale=v_pool_scale, ln_ffn=v_ln_ffn, conv_b=v_conv_b,
                 ln_ple=v_ln_ple, ln_final=v_ln_final)
    pack = lambda d: jnp.concatenate([_rows(d[n]) for n in rep_names], axis=0)
    g_pack = jnp.concatenate([pack(rep_g), loss_blk], axis=0)
    zero_blk = jnp.zeros((8, LANES), F32)
    w_pack = jnp.concatenate([pack(rep_w), zero_blk], axis=0)
    m_pack = jnp.concatenate([pack(rep_m), zero_blk], axis=0)
    v_pack = jnp.concatenate([pack(rep_v), zero_blk + 1.0], axis=0)

    r_w_in, r_w_out, r_w_up, r_conv_w, r_w_down, r_w_pg, r_w_ple, r_pack = _exchange(
        "exchange_grads", list(shard_g) + [g_pack], [True] * 7 + [False])

    sharded = {}
    sharded["w_in"] = _adamw("adamw_w_in", r_w_in, w_in[0], m_w_in[0], v_w_in[0])
    sharded["w_out"] = _adamw("adamw_w_out", r_w_out, w_out[0], m_w_out[0], v_w_out[0])
    sharded["w_up"] = _adamw("adamw_w_up", r_w_up, w_up[0], m_w_up[0], v_w_up[0])
    sharded["conv_w"] = _adamw("adamw_conv_w", r_conv_w, conv_w[0], m_conv_w[0], v_conv_w[0])
    sharded["w_down"] = _adamw("adamw_w_down", r_w_down, w_down[0], m_w_down[0], v_w_down[0])
    sharded["w_ple_gate"] = _adamw("adamw_w_ple_gate", r_w_pg, w_ple_gate[0], m_w_ple_gate[0], v_w_ple_gate[0])
    sharded["w_ple"] = _adamw("adamw_w_ple", r_w_ple, w_ple[0], m_w_ple[0], v_w_ple[0])
    packed = _adamw("adamw_replicated", r_pack, w_pack, m_pack, v_pack)

    loss = packed[0][-8, 0]
    offs, o = {}, 0
    for n in rep_names:
        offs[n] = o
        o += _rows(rep_w[n]).shape[0]

    weights = dict(w_in=w_in, w_out=w_out, w_up=w_up, conv_w=conv_w, w_down=w_down, w_ple_gate=w_ple_gate, w_ple=w_ple,
                   **rep_w)

    def leaf(kind, n):
        shape = weights[n].shape
        if n in sharded:
            return sharded[n][kind].reshape(shape)
        size = 1
        for s in shape:
            size *= s
        rows = -(-size // LANES)
        return packed[kind][offs[n]:offs[n] + rows].reshape(-1)[:size].reshape(shape)

    order = ("ln_mix", "w_in", "pool_w", "pool_scale", "w_out", "ln_ffn", "w_up", "conv_w", "conv_b", "w_down", "ln_ple",
             "w_ple_gate", "w_ple", "ln_final")
    outs = [loss, grad_x[None]]
    for kind in range(4):
        outs += [leaf(kind, n) for n in order]
    return tuple(outs)


def _local_step(xs, ps, tgt, ln_mix, pool_w0, pool_scale, ln_ffn, conv_b, ln_ple, ln_final,
                w_in_g, w_out_g, w_up_g, w_down_g, w_pg_g, w_ple_g, conv_w_g):
    slopes = jnp.exp2(-8.0 * (jnp.arange(N_HEADS, dtype=F32) + 1.0) / N_HEADS)
    w_out_f = w_out_g.reshape(D_MODEL, D_MODEL)
    w_down_f = w_down_g.reshape(4, FF_SHARD, D_MODEL)
    w_pg_f = w_pg_g.reshape(D_MODEL, D_MODEL)
    w_ple_f = jnp.transpose(w_ple_g, (1, 0, 2)).reshape(PLE_DIM, D_MODEL)
    conv_b_s = conv_b.reshape(N_DEV, 1, FF_SHARD)

    q, k, v, u, hn1 = _qkvu(xs, ln_mix, w_in_g)
    att, lse = _attn_fwd(slopes, q, k, v)
    h1, mix, dlt = _mix_out(xs, att, u, pool_w0, pool_scale, w_out_f)
    h2, hn2, up = _ffn_fwd(h1, ln_ffn, w_up_g, conv_w_g, conv_b_s, w_down_f)
    loss_blk, dh2, dh2b, hn3, dgl, dpe, d_ln_ple, d_ln_final = _head(
        h2, ps, ln_ple, w_pg_f, w_ple_f, ln_final.reshape(1, D_MODEL), tgt)

    a, dup, d_conv_w, d_conv_b = _ffn_bwd_a(dh2b, up, w_down_f, conv_w_g, conv_b_s)
    dpre, dh1, dh1b, d_ln_ffn = _ffn_bwd_b(dup, conv_w_g, w_up_g, h1, ln_ffn, dh2)
    datt, du, d_pool_w, d_pool_scale = _mix_bwd(dh1b, w_out_f, dlt, pool_w0, pool_scale)
    dq, dk, dv = _attn_bwd(slopes, q, k, v, att, lse, datt)
    dz, grad_x, d_ln_mix = _in_bwd(dq, dk, dv, du, w_in_g, xs, ln_mix, dh1)

    d_w_in = _wgrad("dw_in", hn1, dz, "full", "lead", N_DEV, D_MODEL, 256)
    d_w_out = _wgrad("dw_out", mix, dh1b, "cols", "full", N_DEV, LANES, D_MODEL)
    d_w_up = _wgrad("dw_up", hn2, dpre, "full", "lead", N_DEV, D_MODEL, FF_SHARD)
    d_w_down = _wgrad("dw_down", a, dh2b, "lead", "full", 4, FF_SHARD, D_MODEL).reshape(N_DEV, D_FF // N_DEV, D_MODEL)
    d_w_pg = _wgrad("dw_ple_gate", hn3, dgl, "cols", "full", N_DEV, LANES, D_MODEL)
    d_w_ple = _wgrad("dw_ple", ps, dpe, "full", "cols", N_DEV, PLE_DIM, LANES)

    rep_g = dict(ln_mix=d_ln_mix, pool_w=d_pool_w, pool_scale=d_pool_scale, ln_ffn=d_ln_ffn, conv_b=d_conv_b,
                 ln_ple=d_ln_ple, ln_final=d_ln_final)
    return loss_blk, grad_x, (d_w_in, d_w_out, d_w_up, d_conv_w, d_w_down, d_w_pg, d_w_ple), rep_g
```

```python
import jax
import jax.numpy as jnp
from jax import lax
from jax.experimental import pallas as pl
from jax.experimental.pallas import tpu as pltpu

F32 = jnp.float32
BF16 = jnp.bfloat16

N_DEV = 8
D_MODEL = 1024
ATT_WIDTH = 512
POOL_WIDTH = 512
N_HEADS = 8
HEAD_DIM = 64
SPAN = 128
DILATIONS = (1, 4, 16)
POOL_WINDOWS = (2, 4, 8, 16)
POOL_GROUP = 128
D_FF = 2816
FF_SHARD = 2 * D_FF // N_DEV
PLE_DIM = 256
EPS = 1e-6
NEG = -1e30

ADAM_LR = 0.001
ADAM_B1 = 0.9
ADAM_B2 = 0.999
ADAM_EPS = 1e-08
ADAM_WD = 0.01
ADAM_STEP = 10

LANES = 128
HALO = 16
TM = 512
TM_FF = 256
TK = 1024
ATTN_UNROLL = 2
VMEM_LIMIT = 56 * 1024 * 1024

MESH = pl.DeviceIdType.MESH
NT = (((1,), (1,)), ((), ()))
TN = (((0,), (0,)), ((), ()))


def _params(*sem):
    return pltpu.CompilerParams(dimension_semantics=sem or None, vmem_limit_bytes=VMEM_LIMIT)


def _const(shape):
    n = len(shape)
    return pl.BlockSpec(shape, lambda *_: (0,) * n, pipeline_mode=pl.Buffered(1))


def _rms(h):
    r = lax.rsqrt(jnp.mean(h * h, axis=-1, keepdims=True) + EPS)
    return r, h * r


def _rms_bwd(r, n, g, dhn):
    dn = dhn * g
    return r * (dn - n * jnp.mean(dn * n, axis=-1, keepdims=True))


def _colsum(a):
    return jnp.sum(a, axis=0, keepdims=True)


def _exchange(name, arrays, scatter):
    n = len(arrays)

    def body(*refs):
        copies = _exchange_copies(refs[:n], refs[n:2 * n], scatter, *refs[2 * n:])
        for cp in copies:
            cp.start()
        for cp in copies:
            cp.wait()

    return pl.pallas_call(
        body, name=name,
        in_specs=[ANY_SPEC] * n, out_specs=[ANY_SPEC] * n, out_shape=_exchange_shapes(arrays, scatter),
        scratch_shapes=_exchange_sems(n),
    )(*arrays)


ANY_SPEC = pl.BlockSpec(memory_space=pl.ANY)


def _exchange_shapes(arrays, scatter):
    out = []
    for a, s in zip(arrays, scatter):
        slab = a.shape[1:] if s else a.shape
        out.append(jax.ShapeDtypeStruct((N_DEV,) + tuple(slab), a.dtype))
    return out


def _exchange_sems(n):
    return [pltpu.SemaphoreType.DMA((n * (N_DEV - 1),)), pltpu.SemaphoreType.DMA((n * (N_DEV - 1),)),
            pltpu.SemaphoreType.DMA((n,))]


def _exchange_copies(ins, outs, scatter, send_sems, recv_sems, local_sems):
    n = len(ins)
    x, y, c = lax.axis_index("x"), lax.axis_index("y"), lax.axis_index("c")
    me = 4 * x + 2 * y + c
    copies = []
    for a in range(n):
        src = ins[a].at[me] if scatter[a] else ins[a]
        copies.append(pltpu.make_async_copy(src, outs[a].at[me], local_sems.at[a]))
    for k in range(1, N_DEV):
        px = 1 - x if k & 4 else x
        py = 1 - y if k & 2 else y
        pc = 1 - c if k & 1 else c
        pid = 4 * px + 2 * py + pc
        for a in range(n):
            src = ins[a].at[pid] if scatter[a] else ins[a]
            copies.append(pltpu.make_async_remote_copy(
                src_ref=src, dst_ref=outs[a].at[me],
                send_sem=send_sems.at[a * (N_DEV - 1) + k - 1], recv_sem=recv_sems.at[a * (N_DEV - 1) + k - 1],
                device_id=(px, py, pc), device_id_type=MESH))
    return copies


def _qkvu(x, g1, w_in):
    S = x.shape[0]

    def body(x_ref, g_ref, w_ref, q_ref, k_ref, v_ref, u_ref, hn_ref):
        r, n = _rms(x_ref[...])
        hn = (n * g_ref[...]).astype(BF16)
        hn_ref[...] = hn
        outs = (q_ref, k_ref, v_ref, u_ref)
        for j in range(N_DEV):
            z = jnp.dot(hn, w_ref[j], preferred_element_type=F32)
            if j < 2:
                z = z * (HEAD_DIM ** -0.5)
            outs[j // 2][:, (j % 2) * 256:(j % 2 + 1) * 256] = z

    tok = lambda w: pl.BlockSpec((TM, w), lambda i: (i, 0))
    return pl.pallas_call(
        body, name="qkvu", grid=(S // TM,),
        in_specs=[tok(D_MODEL), _const((1, D_MODEL)), _const(w_in.shape)],
        out_specs=[tok(512)] * 4 + [tok(D_MODEL)],
        out_shape=[jax.ShapeDtypeStruct((S, 512), F32)] * 4 + [jax.ShapeDtypeStruct((S, D_MODEL), BF16)],
        compiler_params=_params("arbitrary"),
    )(x, g1, w_in)


def _attn_bias(slope_ref, hp, d):
    qi = lax.broadcasted_iota(jnp.int32, (SPAN, 2 * SPAN), 0)
    kj = lax.broadcasted_iota(jnp.int32, (SPAN, 2 * SPAN), 1)
    diff = qi + SPAN - kj
    valid = (diff >= 0) & (diff <= SPAN)
    dist = diff.astype(F32) * float(d)
    return [jnp.where(valid, -slope_ref[2 * hp + h] * dist, NEG) for h in range(2)]


def _attn_rows(r, n, d, nrows):
    start = n * (SPAN * d) + r
    return pl.ds(start, nrows, stride=d) if d > 1 else pl.ds(start, nrows)


def _attn_loops(S, d, blk):
    nb = S // d // SPAN
    shift = d.bit_length() - 1

    def first(r, carry):
        blk(r, 0, True)
        return carry

    def rest(i, carry):
        blk(i & (d - 1), 1 + (i >> shift), False)
        return carry

    if d == 1:
        blk(0, 0, True)
    else:
        lax.fori_loop(0, d, first, 0, unroll=ATTN_UNROLL)
    lax.fori_loop(0, d * (nb - 1), rest, 0, unroll=ATTN_UNROLL)


def _attn_fwd(slopes, q, k, v, shards):
    S = q.shape[0]
    ns = len(shards)
    steps = ATT_WIDTH // LANES

    def body(slope_ref, q_ref, k_ref, v_ref, *rest):
        o_ref, lse_ref = rest[ns:ns + 2]
        m_s, l_s = rest[2 * ns + 2:2 * ns + 4]
        hp = pl.program_id(0)
        copies = _exchange_copies(rest[:ns], rest[ns + 2:2 * ns + 2], (False,) * ns, *rest[2 * ns + 4:])

        @pl.when(hp == 0)
        def _():
            for cp in copies:
                cp.start()

        is0 = lax.broadcasted_iota(jnp.int32, (SPAN, LANES), 1) < HEAD_DIM
        for pi, d in enumerate(DILATIONS):
            bias = _attn_bias(slope_ref, hp, d)
            last = pi == len(DILATIONS) - 1

            def blk(r, n, first, d=d, pi=pi, bias=bias, last=last):
                rows = _attn_rows(r, n, d, SPAN)
                krows = rows if first else _attn_rows(r, n - 1, d, 2 * SPAN)
                qb = q_ref[rows, :]
                kb = k_ref[krows, :].astype(BF16)
                vb = v_ref[krows, :].astype(BF16)
                parts = []
                for h, is_h in enumerate((is0, ~is0)):
                    b = bias[h][:, SPAN:] if first else bias[h]
                    qh = jnp.where(is_h, qb, 0.0).astype(BF16)
                    s = lax.dot_general(qh, kb, NT, preferred_element_type=F32) + b
                    m = jnp.max(s, axis=-1, keepdims=True)
                    e = jnp.exp(s - m)
                    l = jnp.sum(e, axis=-1, keepdims=True)
                    pv = jnp.dot(e.astype(BF16), vb, preferred_element_type=F32)
                    parts.append((m, l, pv))
                m_b = jnp.where(is0, parts[0][0], parts[1][0])
                l_b = jnp.where(is0, parts[0][1], parts[1][1])
                acc = jnp.where(is0, parts[0][2], parts[1][2])
                if pi > 0:
                    m_o = m_s[rows, :]
                    m_n = jnp.maximum(m_o, m_b)
                    a_o = jnp.exp(m_o - m_n)
                    a_b = jnp.exp(m_b - m_n)
                    l_b = a_o * l_s[rows, :] + a_b * l_b
                    acc = a_o * o_ref[rows, :] + a_b * acc
                    m_b = m_n
                if last:
                    o_ref[rows, :] = acc / l_b
                    lse_ref[rows, :] = m_b + jnp.log(l_b)
                else:
                    o_ref[rows, :] = acc
                    m_s[rows, :] = m_b
                    l_s[rows, :] = l_b

            _attn_loops(S, d, blk)

        @pl.when(hp == steps - 1)
        def _():
            for cp in copies:
                cp.wait()

    col = pl.BlockSpec((S, LANES), lambda i: (0, i))
    res = pl.pallas_call(
        body, name="attn_fwd", grid=(steps,),
        in_specs=[pl.BlockSpec(memory_space=pltpu.SMEM), col, col, col] + [ANY_SPEC] * ns,
        out_specs=[col, col] + [ANY_SPEC] * ns,
        out_shape=[jax.ShapeDtypeStruct((S, ATT_WIDTH), F32)] * 2 + _exchange_shapes(shards, (False,) * ns),
        scratch_shapes=[pltpu.VMEM((S, LANES), F32), pltpu.VMEM((S, LANES), F32)] + _exchange_sems(ns),
        compiler_params=_params("arbitrary"),
    )(slopes, q, k, v, *shards)
    return res[0], res[1], res[2:]


def _pool_count(i, w):
    t = i * TM + lax.broadcasted_iota(jnp.int32, (TM, 1), 0)
    return jnp.minimum(t + 1, w).astype(F32)


def _mix_out(x, att, u, pool_w, pool_scale, w_out):
    S = x.shape[0]

    def body(x_ref, att_ref, u_ref, pw_ref, ps_ref, w_ref, h1_ref, mix_ref, dlt_ref, ubuf):
        i = pl.program_id(0)

        @pl.when(i == 0)
        def _():
            ubuf[0:HALO, :] = jnp.zeros((HALO, POOL_WIDTH), F32)

        ubuf[HALO:HALO + TM, :] = u_ref[...]
        mix_ref[:, 0:ATT_WIDTH] = att_ref[...].astype(BF16)
        for g, w in enumerate(POOL_WINDOWS):
            cols = slice(g * POOL_GROUP, (g + 1) * POOL_GROUP)
            ug = ubuf[HALO:HALO + TM, cols]
            acc = ug
            for j in range(1, w):
                acc = acc + ubuf[HALO - j:HALO - j + TM, cols]
            dlt = (acc / _pool_count(i, w) - ug).astype(BF16)
            dlt_ref[:, cols] = dlt
            yg = jnp.dot(dlt, pw_ref[g].astype(BF16), preferred_element_type=F32) * ps_ref[:, cols]
            mix_ref[:, ATT_WIDTH + g * POOL_GROUP:ATT_WIDTH + (g + 1) * POOL_GROUP] = yg.astype(BF16)
        ubuf[0:HALO, :] = ubuf[TM:TM + HALO, :]
        h1_ref[...] = x_ref[...] + jnp.dot(mix_ref[...], w_ref[...], preferred_element_type=F32)

    tok = lambda w: pl.BlockSpec((TM, w), lambda i: (i, 0))
    return pl.pallas_call(
        body, name="mix_out", grid=(S // TM,),
        in_specs=[tok(D_MODEL), tok(ATT_WIDTH), tok(POOL_WIDTH), _const(pool_w.shape), _const((1, POOL_WIDTH)),
                  _const(w_out.shape)],
        out_specs=[tok(D_MODEL), tok(D_MODEL), tok(POOL_WIDTH)],
        out_shape=[jax.ShapeDtypeStruct((S, D_MODEL), F32), jax.ShapeDtypeStruct((S, D_MODEL), BF16),
                   jax.ShapeDtypeStruct((S, POOL_WIDTH), BF16)],
        scratch_shapes=[pltpu.VMEM((TM + HALO, POOL_WIDTH), F32)],
        compiler_params=_params("arbitrary"),
    )(x, att, u, pool_w, pool_scale, w_out)


def _conv_fwd(stage, upre, prev, cw, cb):
    T = upre.shape[0]
    stage[0:HALO, :] = prev
    stage[HALO:HALO + T, :] = upre
    return cb + cw[0:1, :] * stage[HALO - 2:HALO - 2 + T, :] + cw[1:2, :] * stage[HALO - 1:HALO - 1 + T, :] + cw[2:3, :] * upre


def _ffn_fwd(h1, g2, w_up, conv_w, conv_b, w_down):
    S = h1.shape[0]
    T = TM_FF

    def body(h1_ref, g_ref, wu_ref, cw_ref, cb_ref, wd_ref, h2_ref, hn_ref, up_ref, carry, stage):
        i = pl.program_id(0)

        @pl.when(i == 0)
        def _():
            carry[...] = jnp.zeros(carry.shape, F32)

        h1t = h1_ref[...]
        r, n = _rms(h1t)
        hn = (n * g_ref[...]).astype(BF16)
        hn_ref[...] = hn
        acc = h1t
        for j in range(4):
            conv = []
            for jj in (j, j + 4):
                upre = jnp.dot(hn, wu_ref[jj], preferred_element_type=F32)
                up_ref[jj] = upre.astype(BF16)
                conv.append(_conv_fwd(stage, upre, carry[jj], cw_ref[jj], cb_ref[jj]))
                carry[jj] = stage[T:T + HALO, :]
            gate, val = conv
            a = gate * jax.nn.sigmoid(gate) * val
            acc = acc + jnp.dot(a.astype(BF16), wd_ref[j], preferred_element_type=F32)
        h2_ref[...] = acc

    tok = lambda w: pl.BlockSpec((T, w), lambda i: (i, 0))
    return pl.pallas_call(
        body, name="ffn_fwd", grid=(S // T,),
        in_specs=[tok(D_MODEL), _const((1, D_MODEL)), _const(w_up.shape), _const(conv_w.shape), _const(conv_b.shape),
                  _const(w_down.shape)],
        out_specs=[tok(D_MODEL), tok(D_MODEL), pl.BlockSpec((N_DEV, T, FF_SHARD), lambda i: (0, i, 0))],
        out_shape=[jax.ShapeDtypeStruct((S, D_MODEL), F32), jax.ShapeDtypeStruct((S, D_MODEL), BF16),
                   jax.ShapeDtypeStruct((N_DEV, S, FF_SHARD), BF16)],
        scratch_shapes=[pltpu.VMEM((N_DEV, HALO, FF_SHARD), F32), pltpu.VMEM((T + HALO, FF_SHARD), F32)],
        compiler_params=_params("arbitrary"),
    )(h1, g2, w_up, conv_w, conv_b, w_down)


def _head(h2, p, g3, w_pg, w_ple, g4, target):
    S = h2.shape[0]
    nt = S // TM

    def body(h2_ref, p_ref, g3_ref, wpg_ref, wple_ref, g4_ref, t_ref,
             loss_ref, dh2_ref, dh2b_ref, hn3_ref, dgl_ref, dpe_ref, dg3_ref, dg4_ref, lacc):
        i = pl.program_id(0)

        @pl.when(i == 0)
        def _():
            lacc[...] = jnp.zeros(lacc.shape, F32)
            dg3_ref[...] = jnp.zeros(dg3_ref.shape, F32)
            dg4_ref[...] = jnp.zeros(dg4_ref.shape, F32)

        h2t = h2_ref[...]
        g3, g4 = g3_ref[...], g4_ref[...]
        r3, n3 = _rms(h2t)
        hn3 = (n3 * g3).astype(BF16)
        hn3_ref[...] = hn3
        gs = jax.nn.sigmoid(jnp.dot(hn3, wpg_ref[...], preferred_element_type=F32))
        pe = jnp.dot(p_ref[...].astype(BF16), wple_ref[...], preferred_element_type=F32)
        h3 = h2t + gs * pe
        r4, n4 = _rms(h3)
        err = n4 * g4 - t_ref[...]
        lacc[...] += _colsum(err * err)
        dy = err * (1.0 / D_MODEL)
        dg4_ref[...] += _colsum(dy * n4)
        dh3 = _rms_bwd(r4, n4, g4, dy)
        dpe_ref[...] = (dh3 * gs).astype(BF16)
        dgl = (dh3 * pe * gs * (1.0 - gs)).astype(BF16)
        dgl_ref[...] = dgl
        dhn3 = lax.dot_general(dgl, wpg_ref[...], NT, preferred_element_type=F32)
        dg3_ref[...] += _colsum(dhn3 * n3)
        dh2 = dh3 + _rms_bwd(r3, n3, g3, dhn3)
        dh2_ref[...] = dh2
        dh2b_ref[...] = dh2.astype(BF16)

        @pl.when(i == nt - 1)
        def _():
            tot = 0.5 / D_MODEL * jnp.sum(lacc[...], axis=-1, keepdims=True)
            loss_ref[...] = jnp.broadcast_to(tot, loss_ref.shape)

    tok = lambda w: pl.BlockSpec((TM, w), lambda i: (i, 0))
    row = pl.BlockSpec((1, D_MODEL), lambda i: (0, 0))
    act = lambda dt: jax.ShapeDtypeStruct((S, D_MODEL), dt)
    return pl.pallas_call(
        body, name="head", grid=(nt,),
        in_specs=[tok(D_MODEL), tok(PLE_DIM), _const((1, D_MODEL)), _const(w_pg.shape), _const(w_ple.shape),
                  _const((1, D_MODEL)), tok(D_MODEL)],
        out_specs=[pl.BlockSpec((8, LANES), lambda i: (0, 0)), tok(D_MODEL), tok(D_MODEL), tok(D_MODEL), tok(D_MODEL),
                   tok(D_MODEL), row, row],
        out_shape=[jax.ShapeDtypeStruct((8, LANES), F32), act(F32), act(BF16), act(BF16), act(BF16), act(BF16),
                   jax.ShapeDtypeStruct((1, D_MODEL), F32), jax.ShapeDtypeStruct((1, D_MODEL), F32)],
        scratch_shapes=[pltpu.VMEM((1, D_MODEL), F32)],
        compiler_params=_params("arbitrary"),
    )(h2, p, g3, w_pg, w_ple, g4, target)


def _wgrad(name, x, dy, x_kind, dy_kind, nj, k_dim, n_dim):
    S = x.shape[-2]
    nt = S // TK

    def spec(kind, width):
        if kind == "full":
            return pl.BlockSpec((TK, width), lambda j, t: (t, 0))
        return pl.BlockSpec((None, TK, width), lambda j, t: (j, t, 0))

    def body(x_ref, dy_ref, o_ref, acc):
        t = pl.program_id(1)

        @pl.when(t == 0)
        def _():
            acc[...] = jnp.zeros(acc.shape, F32)

        acc[...] += lax.dot_general(x_ref[...].astype(BF16), dy_ref[...], TN, preferred_element_type=F32)

        @pl.when(t == nt - 1)
        def _():
            o_ref[...] = acc[...].astype(BF16)

    return pl.pallas_call(
        body, name=name, grid=(nj, nt),
        in_specs=[spec(x_kind, k_dim), spec(dy_kind, n_dim)],
        out_specs=pl.BlockSpec((None, k_dim, n_dim), lambda j, t: (j, 0, 0)),
        out_shape=jax.ShapeDtypeStruct((nj, k_dim, n_dim), BF16),
        scratch_shapes=[pltpu.VMEM((k_dim, n_dim), F32)],
        compiler_params=_params("arbitrary", "arbitrary"),
    )(x, dy)


def _ffn_bwd_a(dh2b, up, w_down, conv_w, conv_b):
    S = dh2b.shape[0]
    T = TM_FF
    hb = T // HALO

    def body(dh_ref, up_ref, halo_ref, wd_ref, cw_ref, cb_ref, a_ref, dup_ref, dcw_ref, dcb_ref, stage):
        i = pl.program_id(0)

        @pl.when(i == 0)
        def _():
            dcw_ref[...] = jnp.zeros(dcw_ref.shape, F32)
            dcb_ref[...] = jnp.zeros(dcb_ref.shape, F32)

        dh = dh_ref[...]
        for j in range(4):
            da = lax.dot_general(dh, wd_ref[j], NT, preferred_element_type=F32)
            conv, taps = [], []
            for jj in (j, j + 4):
                upre = up_ref[jj].astype(F32)
                prev = jnp.where(i > 0, halo_ref[jj].astype(F32), 0.0)
                conv.append(_conv_fwd(stage, upre, prev, cw_ref[jj], cb_ref[jj]))
                taps.append((stage[HALO - 2:HALO - 2 + T, :], stage[HALO - 1:HALO - 1 + T, :], upre))
            gate, val = conv
            sg = jax.nn.sigmoid(gate)
            silu = gate * sg
            a_ref[j] = (silu * val).astype(BF16)
            dgate = da * val * (sg * (1.0 + gate * (1.0 - sg)))
            dval = da * silu
            for jj, dup, tp in ((j, dgate, taps[0]), (j + 4, dval, taps[1])):
                dup_ref[jj] = dup.astype(BF16)
                dcb_ref[jj] += _colsum(dup)
                for kk in range(3):
                    dcw_ref[jj, kk:kk + 1, :] += _colsum(dup * tp[kk])

    tok = lambda w: pl.BlockSpec((T, w), lambda i: (i, 0))
    shard = pl.BlockSpec((N_DEV, T, FF_SHARD), lambda i: (0, i, 0))
    return pl.pallas_call(
        body, name="ffn_bwd_a", grid=(S // T,),
        in_specs=[tok(D_MODEL), shard,
                  pl.BlockSpec((N_DEV, HALO, FF_SHARD), lambda i: (0, jnp.maximum(i * hb - 1, 0), 0)),
                  _const(w_down.shape), _const(conv_w.shape), _const(conv_b.shape)],
        out_specs=[pl.BlockSpec((4, T, FF_SHARD), lambda i: (0, i, 0)), shard,
                   pl.BlockSpec((N_DEV, 3, FF_SHARD), lambda i: (0, 0, 0)),
                   pl.BlockSpec((N_DEV, 1, FF_SHARD), lambda i: (0, 0, 0))],
        out_shape=[jax.ShapeDtypeStruct((4, S, FF_SHARD), BF16), jax.ShapeDtypeStruct((N_DEV, S, FF_SHARD), BF16),
                   jax.ShapeDtypeStruct((N_DEV, 3, FF_SHARD), F32), jax.ShapeDtypeStruct((N_DEV, 1, FF_SHARD), F32)],
        scratch_shapes=[pltpu.VMEM((T + HALO, FF_SHARD), F32)],
        compiler_params=_params("arbitrary"),
    )(dh2b, up, up, w_down, conv_w, conv_b)


def _ffn_bwd_b(dup, conv_w, w_up, h1, g2, dh2):
    S = h1.shape[0]
    T = TM_FF
    hb = T // HALO
    nt = S // T

    def body(dup_ref, halo_ref, cw_ref, wu_ref, h1_ref, g_ref, dh2_ref, dpre_ref, dh1_ref, dh1b_ref, dg_ref, stage):
        i = pl.program_id(0)

        @pl.when(i == 0)
        def _():
            dg_ref[...] = jnp.zeros(dg_ref.shape, F32)

        dhn = jnp.zeros((T, D_MODEL), F32)
        for jj in range(N_DEV):
            dup = dup_ref[jj].astype(F32)
            stage[0:T, :] = dup
            stage[T:T + HALO, :] = jnp.where(i < nt - 1, halo_ref[jj].astype(F32), 0.0)
            cw = cw_ref[jj]
            dpre = (cw[2:3, :] * dup + cw[1:2, :] * stage[1:1 + T, :] + cw[0:1, :] * stage[2:2 + T, :]).astype(BF16)
            dpre_ref[jj] = dpre
            dhn = dhn + lax.dot_general(dpre, wu_ref[jj], NT, preferred_element_type=F32)
        g = g_ref[...]
        r, n = _rms(h1_ref[...])
        dg_ref[...] += _colsum(dhn * n)
        dh1 = dh2_ref[...] + _rms_bwd(r, n, g, dhn)
        dh1_ref[...] = dh1
        dh1b_ref[...] = dh1.astype(BF16)

    tok = lambda w: pl.BlockSpec((T, w), lambda i: (i, 0))
    shard = pl.BlockSpec((N_DEV, T, FF_SHARD), lambda i: (0, i, 0))
    return pl.pallas_call(
        body, name="ffn_bwd_b", grid=(nt,),
        in_specs=[shard,
                  pl.BlockSpec((N_DEV, HALO, FF_SHARD), lambda i: (0, jnp.minimum((i + 1) * hb, S // HALO - 1), 0)),
                  _const(conv_w.shape), _const(w_up.shape), tok(D_MODEL), _const((1, D_MODEL)), tok(D_MODEL)],
        out_specs=[shard, tok(D_MODEL), tok(D_MODEL), pl.BlockSpec((1, D_MODEL), lambda i: (0, 0))],
        out_shape=[jax.ShapeDtypeStruct((N_DEV, S, FF_SHARD), BF16), jax.ShapeDtypeStruct((S, D_MODEL), F32),
                   jax.ShapeDtypeStruct((S, D_MODEL), BF16), jax.ShapeDtypeStruct((1, D_MODEL), F32)],
        scratch_shapes=[pltpu.VMEM((T + HALO, FF_SHARD), F32)],
        compiler_params=_params("arbitrary"),
    )(dup, dup, conv_w, w_up, h1, g2, dh2)


def _mix_bwd(dh1b, w_out, dlt, pool_w, pool_scale):
    S = dh1b.shape[0]
    nt = S // TM

    def body(dh_ref, w_ref, dlt_ref, pw_ref, ps_ref, datt_ref, du_ref, dpw_ref, dps_ref, stage, carry):
        i = pl.program_id(0)
        tile = nt - 1 - i

        @pl.when(i == 0)
        def _():
            dpw_ref[...] = jnp.zeros(dpw_ref.shape, F32)
            dps_ref[...] = jnp.zeros(dps_ref.shape, F32)
            carry[...] = jnp.zeros(carry.shape, F32)

        dmix = lax.dot_general(dh_ref[...], w_ref[...], NT, preferred_element_type=F32)
        datt_ref[...] = dmix[:, 0:ATT_WIDTH]
        for g, w in enumerate(POOL_WINDOWS):
            cols = slice(g * POOL_GROUP, (g + 1) * POOL_GROUP)
            dpool = dmix[:, ATT_WIDTH + g * POOL_GROUP:ATT_WIDTH + (g + 1) * POOL_GROUP]
            dl = dlt_ref[:, cols]
            pw = pw_ref[g].astype(BF16)
            yg = jnp.dot(dl, pw, preferred_element_type=F32)
            dps_ref[:, cols] += _colsum(dpool * yg)
            dy = (dpool * ps_ref[:, cols]).astype(BF16)
            dpw_ref[g] += lax.dot_general(dl, dy, TN, preferred_element_type=F32)
            ddlt = lax.dot_general(dy, pw, NT, preferred_element_type=F32)
            cg = ddlt / _pool_count(tile, w)
            stage[0:TM, :] = cg
            stage[TM:TM + HALO, :] = carry[:, cols]
            acc = cg
            for j in range(1, w):
                acc = acc + stage[j:j + TM, :]
            du_ref[:, cols] = acc - ddlt
            carry[:, cols] = cg[0:HALO, :]

    tok = lambda w: pl.BlockSpec((TM, w), lambda i: (nt - 1 - i, 0))
    return pl.pallas_call(
        body, name="mix_bwd", grid=(nt,),
        in_specs=[tok(D_MODEL), _const(w_out.shape), tok(POOL_WIDTH), _const(pool_w.shape), _const((1, POOL_WIDTH))],
        out_specs=[tok(ATT_WIDTH), tok(POOL_WIDTH), pl.BlockSpec(pool_w.shape, lambda i: (0, 0, 0)),
                   pl.BlockSpec((1, POOL_WIDTH), lambda i: (0, 0))],
        out_shape=[jax.ShapeDtypeStruct((S, ATT_WIDTH), F32), jax.ShapeDtypeStruct((S, POOL_WIDTH), F32),
                   jax.ShapeDtypeStruct(pool_w.shape, F32), jax.ShapeDtypeStruct((1, POOL_WIDTH), F32)],
        scratch_shapes=[pltpu.VMEM((TM + HALO, POOL_GROUP), F32), pltpu.VMEM((HALO, POOL_WIDTH), F32)],
        compiler_params=_params("arbitrary"),
    )(dh1b, w_out, dlt, pool_w, pool_scale)


def _attn_bwd(slopes, q, k, v, o, lse, do, grads):
    S = q.shape[0]
    CH = 512
    ng = len(grads)
    steps = ATT_WIDTH // LANES

    def body(slope_ref, q_ref, k_ref, v_ref, o_ref, lse_ref, do_ref, *rest):
        dq_ref, dk_ref, dv_ref = rest[ng:ng + 3]
        dl_s = rest[2 * ng + 3]
        hp = pl.program_id(0)
        copies = _exchange_copies(rest[:ng], rest[ng + 3:2 * ng + 3], (True,) * ng, *rest[2 * ng + 4:])

        @pl.when(hp == 0)
        def _():
            for cp in copies:
                cp.start()

        is0 = lax.broadcasted_iota(jnp.int32, (SPAN, LANES), 1) < HEAD_DIM
        is0c = lax.broadcasted_iota(jnp.int32, (CH, LANES), 1) < HEAD_DIM
        is0k = lax.broadcasted_iota(jnp.int32, (2 * SPAN, LANES), 1) < HEAD_DIM

        def prep(ci, carry):
            rows = pl.ds(pl.multiple_of(ci * CH, CH), CH)
            prod = do_ref[rows, :] * o_ref[rows, :]
            d0 = jnp.sum(jnp.where(is0c, prod, 0.0), axis=-1, keepdims=True)
            d1 = jnp.sum(jnp.where(is0c, 0.0, prod), axis=-1, keepdims=True)
            dl_s[rows, :] = jnp.where(is0c, d0, d1)
            zero = jnp.zeros((CH, LANES), F32)
            dq_ref[rows, :] = zero
            dk_ref[rows, :] = zero
            dv_ref[rows, :] = zero
            return carry

        lax.fori_loop(0, S // CH, prep, 0)

        for d in DILATIONS:
            bias = _attn_bias(slope_ref, hp, d)

            def blk(r, n, first, d=d, bias=bias):
                rows = _attn_rows(r, n, d, SPAN)
                krows = rows if first else _attn_rows(r, n - 1, d, 2 * SPAN)
                qb = q_ref[rows, :]
                dob = do_ref[rows, :]
                lse_b = lse_ref[rows, :]
                dl_b = dl_s[rows, :]
                kf = k_ref[krows, :]
                kb = kf.astype(BF16)
                vb = v_ref[krows, :].astype(BF16)
                dq_c = jnp.zeros((SPAN, LANES), F32)
                dk_c = jnp.zeros(kf.shape, F32)
                dv_c = jnp.zeros(kf.shape, F32)
                for h, is_h in enumerate((is0, ~is0)):
                    lo = h * HEAD_DIM
                    b = bias[h][:, SPAN:] if first else bias[h]
                    qh = jnp.where(is_h, qb, 0.0).astype(BF16)
                    doh = jnp.where(is_h, dob, 0.0).astype(BF16)
                    s = lax.dot_general(qh, kb, NT, preferred_element_type=F32) + b
                    pr = jnp.exp(s - lse_b[:, lo:lo + 1])
                    dp = lax.dot_general(doh, vb, NT, preferred_element_type=F32)
                    ds = (pr * (dp - dl_b[:, lo:lo + 1])).astype(BF16)
                    is_hk = is_h if first else (is0k if h == 0 else ~is0k)
                    kh = jnp.where(is_hk, kf, 0.0).astype(BF16)
                    dv_c = dv_c + lax.dot_general(pr.astype(BF16), doh, TN, preferred_element_type=F32)
                    dk_c = dk_c + lax.dot_general(ds, qh, TN, preferred_element_type=F32)
                    dq_c = dq_c + jnp.dot(ds, kh, preferred_element_type=F32)
                dq_ref[rows, :] += dq_c
                dk_ref[krows, :] += dk_c
                dv_ref[krows, :] += dv_c

            _attn_loops(S, d, blk)

        @pl.when(hp == steps - 1)
        def _():
            for cp in copies:
                cp.wait()

    col = pl.BlockSpec((S, LANES), lambda i: (0, i))
    res = pl.pallas_call(
        body, name="attn_bwd", grid=(steps,),
        in_specs=[pl.BlockSpec(memory_space=pltpu.SMEM)] + [col] * 6 + [ANY_SPEC] * ng,
        out_specs=[col] * 3 + [ANY_SPEC] * ng,
        out_shape=[jax.ShapeDtypeStruct((S, ATT_WIDTH), F32)] * 3 + _exchange_shapes(grads, (True,) * ng),
        scratch_shapes=[pltpu.VMEM((S, LANES), F32)] + _exchange_sems(ng),
        compiler_params=_params("arbitrary"),
    )(slopes, q, k, v, o, lse, do, *grads)
    return res[0], res[1], res[2], res[3:]


def _in_bwd(dq, dk, dv, du, w_in, x, g1, dh1):
    S = x.shape[0]

    def body(dq_ref, dk_ref, dv_ref, du_ref, w_ref, x_ref, g_ref, dh1_ref, dz_ref, dx_ref, dg_ref):
        @pl.when(pl.program_id(0) == 0)
        def _():
            dg_ref[...] = jnp.zeros(dg_ref.shape, F32)

        srcs = (dq_ref, dk_ref, dv_ref, du_ref)
        dhn = jnp.zeros((TM, D_MODEL), F32)
        for j in range(N_DEV):
            dz = srcs[j // 2][:, (j % 2) * 256:(j % 2 + 1) * 256]
            if j < 2:
                dz = dz * (HEAD_DIM ** -0.5)
            dz = dz.astype(BF16)
            dz_ref[j] = dz
            dhn = dhn + lax.dot_general(dz, w_ref[j], NT, preferred_element_type=F32)
        g = g_ref[...]
        r, n = _rms(x_ref[...])
        dg_ref[...] += _colsum(dhn * n)
        dx_ref[...] = dh1_ref[...] + _rms_bwd(r, n, g, dhn)

    tok = lambda w: pl.BlockSpec((TM, w), lambda i: (i, 0))
    return pl.pallas_call(
        body, name="in_bwd", grid=(S // TM,),
        in_specs=[tok(512)] * 4 + [_const(w_in.shape), tok(D_MODEL), _const((1, D_MODEL)), tok(D_MODEL)],
        out_specs=[pl.BlockSpec((N_DEV, TM, 256), lambda i: (0, i, 0)), tok(D_MODEL),
                   pl.BlockSpec((1, D_MODEL), lambda i: (0, 0))],
        out_shape=[jax.ShapeDtypeStruct((N_DEV, S, 256), BF16), jax.ShapeDtypeStruct((S, D_MODEL), F32),
                   jax.ShapeDtypeStruct((1, D_MODEL), F32)],
        compiler_params=_params("arbitrary"),
    )(dq, dk, dv, du, w_in, x, g1, dh1)


def _adamw(name, parts, w, m, v):
    R, C = w.shape
    rb = R
    for cand in (256, 128, 64, 32, 16, 8):
        if R % cand == 0 and R > cand:
            rb = cand
            break

    def body(p_ref, w_ref, m_ref, v_ref, g_ref, d_ref, mo_ref, vo_ref):
        g = p_ref[0].astype(F32)
        for s in range(1, N_DEV):
            g = g + p_ref[s].astype(F32)
        m_new = ADAM_B1 * m_ref[...] + (1.0 - ADAM_B1) * g
        v_new = ADAM_B2 * v_ref[...] + (1.0 - ADAM_B2) * (g * g)
        m_hat = m_new / (1.0 - ADAM_B1 ** ADAM_STEP)
        v_hat = v_new / (1.0 - ADAM_B2 ** ADAM_STEP)
        g_ref[...] = g
        d_ref[...] = -ADAM_LR * (m_hat / (jnp.sqrt(v_hat) + ADAM_EPS) + ADAM_WD * w_ref[...])
        mo_ref[...] = m_new
        vo_ref[...] = v_new

    blk = pl.BlockSpec((rb, C), lambda i: (i, 0))
    return pl.pallas_call(
        body, name=name, grid=(R // rb,),
        in_specs=[pl.BlockSpec((N_DEV, rb, C), lambda i: (0, i, 0)), blk, blk, blk],
        out_specs=[blk] * 4,
        out_shape=[jax.ShapeDtypeStruct((R, C), F32)] * 4,
        compiler_params=_params("arbitrary"),
    )(parts, w, m, v)


def _rows(a):
    flat = a.reshape(-1)
    rows = -(-flat.shape[0] // LANES)
    rows8 = -(-rows // 8) * 8
    flat = jnp.pad(flat, (0, rows8 * LANES - flat.shape[0]))
    return flat.reshape(rows8, LANES)


def kernel(x, p, ln_mix, w_in, pool_w, pool_scale, w_out, ln_ffn, w_up, conv_w, conv_b, w_down, ln_ple, w_ple_gate, w_ple, ln_final, loss_target, m_ln_mix, m_w_in, m_pool_w, m_pool_scale, m_w_out, m_ln_ffn, m_w_up, m_conv_w, m_conv_b, m_w_down, m_ln_ple, m_w_ple_gate, m_w_ple, m_ln_final, v_ln_mix, v_w_in, v_pool_w, v_pool_scale, v_w_out, v_ln_ffn, v_w_up, v_conv_w, v_conv_b, v_w_down, v_ln_ple, v_w_ple_gate, v_w_ple, v_ln_final):
    xs, ps, tgt, pool_w0 = x[0], p[0, 0], loss_target[0], pool_w[0]
    slopes = jnp.exp2(-8.0 * (jnp.arange(N_HEADS, dtype=F32) + 1.0) / N_HEADS)
    conv_b_s = conv_b.reshape(N_DEV, 1, FF_SHARD)

    (w_in_g,) = _exchange("gather_w_in", [w_in[0].astype(BF16)], (False,))
    q, k, v, u, hn1 = _qkvu(xs, ln_mix, w_in_g)
    att, lse, (w_out_g, w_up_g, w_down_g, w_pg_g, w_ple_g, conv_w_g) = _attn_fwd(
        slopes, q, k, v,
        [w_out[0].astype(BF16), w_up[0].astype(BF16), w_down[0].astype(BF16), w_ple_gate[0].astype(BF16),
         w_ple[0].astype(BF16), conv_w[0]])
    w_out_f = w_out_g.reshape(D_MODEL, D_MODEL)
    w_down_f = w_down_g.reshape(4, FF_SHARD, D_MODEL)
    w_pg_f = w_pg_g.reshape(D_MODEL, D_MODEL)
    w_ple_f = jnp.transpose(w_ple_g, (1, 0, 2)).reshape(PLE_DIM, D_MODEL)
    h1, mix, dlt = _mix_out(xs, att, u, pool_w0, pool_scale, w_out_f)
    h2, hn2, up = _ffn_fwd(h1, ln_ffn, w_up_g, conv_w_g, conv_b_s, w_down_f)
    loss_blk, dh2, dh2b, hn3, dgl, dpe, d_ln_ple, d_ln_final = _head(
        h2, ps, ln_ple, w_pg_f, w_ple_f, ln_final.reshape(1, D_MODEL), tgt)

    a, dup, d_conv_w, d_conv_b = _ffn_bwd_a(dh2b, up, w_down_f, conv_w_g, conv_b_s)
    dpre, dh1, dh1b, d_ln_ffn = _ffn_bwd_b(dup, conv_w_g, w_up_g, h1, ln_ffn, dh2)
    datt, du, d_pool_w, d_pool_scale = _mix_bwd(dh1b, w_out_f, dlt, pool_w0, pool_scale)
    d_w_out = _wgrad("dw_out", mix, dh1b, "full", "full", 1, D_MODEL, D_MODEL).reshape(N_DEV, D_MODEL // N_DEV, D_MODEL)
    d_w_up = _wgrad("dw_up", hn2, dpre, "full", "lead", N_DEV, D_MODEL, FF_SHARD)
    d_w_down = _wgrad("dw_down", a, dh2b, "lead", "full", 4, FF_SHARD, D_MODEL).reshape(N_DEV, D_FF // N_DEV, D_MODEL)
    d_w_pg = _wgrad("dw_ple_gate", hn3, dgl, "full", "full", 1, D_MODEL, D_MODEL).reshape(N_DEV, D_MODEL // N_DEV, D_MODEL)
    d_w_ple = _wgrad("dw_ple", ps, dpe, "full", "full", 1, PLE_DIM, D_MODEL)
    d_w_ple = jnp.transpose(d_w_ple.reshape(PLE_DIM, N_DEV, LANES), (1, 0, 2))
    dq, dk, dv, (r_w_out, r_w_up, r_conv_w, r_w_down, r_w_pg, r_w_ple) = _attn_bwd(
        slopes, q, k, v, att, lse, datt, [d_w_out, d_w_up, d_conv_w, d_w_down, d_w_pg, d_w_ple])
    dz, grad_x, d_ln_mix = _in_bwd(dq, dk, dv, du, w_in_g, xs, ln_mix, dh1)
    d_w_in = _wgrad("dw_in", hn1, dz, "full", "lead", N_DEV, D_MODEL, 256)
    rep_g = dict(ln_mix=d_ln_mix, pool_w=d_pool_w, pool_scale=d_pool_scale, ln_ffn=d_ln_ffn, conv_b=d_conv_b,
                 ln_ple=d_ln_ple, ln_final=d_ln_final)

    rep_names = ("ln_mix", "pool_w", "pool_scale", "ln_ffn", "conv_b", "ln_ple", "ln_final")
    rep_w = dict(ln_mix=ln_mix, pool_w=pool_w, pool_scale=pool_scale, ln_ffn=ln_ffn, conv_b=conv_b, ln_ple=ln_ple,
                 ln_final=ln_final)
    rep_m = dict(ln_mix=m_ln_mix, pool_w=m_pool_w, pool_scale=m_pool_scale, ln_ffn=m_ln_ffn, conv_b=m_conv_b,
                 ln_ple=m_ln_ple, ln_final=m_ln_final)
    rep_v = dict(ln_mix=v_ln_mix, pool_w=v_pool_w, pool_scale=v_pool_scale, ln_ffn=v_ln_ffn, conv_b=v_conv_b,
                 ln_ple=v_ln_ple, ln_final=v_ln_final)
    pack = lambda d: jnp.concatenate([_rows(d[n]) for n in rep_names], axis=0)
    g_pack = jnp.concatenate([pack(rep_g), loss_blk], axis=0)
    zero_blk = jnp.zeros((8, LANES), F32)
    w_pack = jnp.concatenate([pack(rep_w), zero_blk], axis=0)
    m_pack = jnp.concatenate([pack(rep_m), zero_blk], axis=0)
    v_pack = jnp.concatenate([pack(rep_v), zero_blk + 1.0], axis=0)

    r_w_in, r_pack = _exchange("exchange_tail", [d_w_in, g_pack], (True, False))

    sharded = {}
    sharded["w_in"] = _adamw("adamw_w_in", r_w_in, w_in[0], m_w_in[0], v_w_in[0])
    sharded["w_out"] = _adamw("adamw_w_out", r_w_out, w_out[0], m_w_out[0], v_w_out[0])
    sharded["w_up"] = _adamw("adamw_w_up", r_w_up, w_up[0], m_w_up[0], v_w_up[0])
    sharded["conv_w"] = _adamw("adamw_conv_w", r_conv_w, conv_w[0], m_conv_w[0], v_conv_w[0])
    sharded["w_down"] = _adamw("adamw_w_down", r_w_down, w_down[0], m_w_down[0], v_w_down[0])
    sharded["w_ple_gate"] = _adamw("adamw_w_ple_gate", r_w_pg, w_ple_gate[0], m_w_ple_gate[0], v_w_ple_gate[0])
    sharded["w_ple"] = _adamw("adamw_w_ple", r_w_ple, w_ple[0], m_w_ple[0], v_w_ple[0])
    packed = _adamw("adamw_replicated", r_pack, w_pack, m_pack, v_pack)

    loss = packed[0][-8, 0]
    offs, o = {}, 0
    for n in rep_names:
        offs[n] = o
        o += _rows(rep_w[n]).shape[0]

    weights = dict(w_in=w_in, w_out=w_out, w_up=w_up, conv_w=conv_w, w_down=w_down, w_ple_gate=w_ple_gate, w_ple=w_ple,
                   **rep_w)

    def leaf(kind, n):
        shape = weights[n].shape
        if n in sharded:
            return sharded[n][kind].reshape(shape)
        size = 1
        for s in shape:
            size *= s
        rows = -(-size // LANES)
        return packed[kind][offs[n]:offs[n] + rows].reshape(-1)[:size].reshape(shape)

    order = ("ln_mix", "w_in", "pool_w", "pool_scale", "w_out", "ln_ffn", "w_up", "conv_w", "conv_b", "w_down", "ln_ple",
             "w_ple_gate", "w_ple", "ln_final")
    outs = [loss, grad_x[None]]
    for kind in range(4):
        outs += [leaf(kind, n) for n in order]
    return tuple(outs)
```

```python
import jax
import jax.numpy as jnp
from jax import lax
from jax.experimental import pallas as pl
from jax.experimental.pallas import tpu as pltpu

F32 = jnp.float32
BF16 = jnp.bfloat16

N_DEV = 8
D_MODEL = 1024
ATT_WIDTH = 512
POOL_WIDTH = 512
N_HEADS = 8
HEAD_DIM = 64
SPAN = 128
DILATIONS = (1, 4, 16)
POOL_WINDOWS = (2, 4, 8, 16)
POOL_GROUP = 128
D_FF = 2816
FF_SHARD = 2 * D_FF // N_DEV
PLE_DIM = 256
EPS = 1e-6
NEG = -1e30

ADAM_LR = 0.001
ADAM_B1 = 0.9
ADAM_B2 = 0.999
ADAM_EPS = 1e-08
ADAM_WD = 0.01
ADAM_STEP = 10

LANES = 128
HALO = 16
TM = 512
TM_FF = 256
TK = 1024
ATTN_GROUP = 4
VMEM_LIMIT = 56 * 1024 * 1024

MESH = pl.DeviceIdType.MESH
NT = (((1,), (1,)), ((), ()))
TN = (((0,), (0,)), ((), ()))


def _params(*sem):
    return pltpu.CompilerParams(dimension_semantics=sem or None, vmem_limit_bytes=VMEM_LIMIT)


def _const(shape):
    n = len(shape)
    return pl.BlockSpec(shape, lambda *_: (0,) * n, pipeline_mode=pl.Buffered(1))


def _rms(h):
    r = lax.rsqrt(jnp.mean(h * h, axis=-1, keepdims=True) + EPS)
    return r, h * r


def _rms_bwd(r, n, g, dhn):
    dn = dhn * g
    return r * (dn - n * jnp.mean(dn * n, axis=-1, keepdims=True))


def _colsum(a):
    return jnp.sum(a, axis=0, keepdims=True)


def _gather2_copies(ins, outs, send_sems, recv_sems, local_sems):
    n = len(ins)
    x, y, c = lax.axis_index("x"), lax.axis_index("y"), lax.axis_index("c")
    slot = lambda px, py, pc: 4 * px + 2 * py + pc
    chips = [(x, 1 - y), (1 - x, y), (1 - x, 1 - y)]
    first, passed, last = [], [], []

    def remote(a, r, src, dst_slot, to):
        return pltpu.make_async_remote_copy(
            src_ref=src, dst_ref=outs[a].at[dst_slot],
            send_sem=send_sems.at[a * (N_DEV - 1) + r], recv_sem=recv_sems.at[a * (N_DEV - 1) + r],
            device_id=to, device_id_type=MESH)

    for a in range(n):
        mine = pltpu.make_async_copy(ins[a], outs[a].at[slot(x, y, c)], local_sems.at[a])
        to_sibling = remote(a, 0, ins[a], slot(x, y, c), (x, y, 1 - c))
        first += [mine, to_sibling]
        last += [mine.wait, to_sibling.wait_send, to_sibling.wait_recv]
        for r, (px, py) in enumerate(chips, start=1):
            to_chip = remote(a, r, ins[a], slot(x, y, c), (px, py, c))
            onward = remote(a, 3 + r, outs[a].at[slot(px, py, c)], slot(px, py, c), (x, y, 1 - c))
            first.append(to_chip)
            passed.append((to_chip, onward))
            last += [to_chip.wait_send, onward.wait_send, onward.wait_recv]
    return first, passed, last


def _exchange(name, arrays, scatter):
    n = len(arrays)

    def body(*refs):
        copies = _exchange_copies(refs[:n], refs[n:2 * n], scatter, *refs[2 * n:])
        for cp in copies:
            cp.start()
        for cp in copies:
            cp.wait()

    return pl.pallas_call(
        body, name=name,
        in_specs=[ANY_SPEC] * n, out_specs=[ANY_SPEC] * n, out_shape=_exchange_shapes(arrays, scatter),
        scratch_shapes=_exchange_sems(n),
    )(*arrays)


ANY_SPEC = pl.BlockSpec(memory_space=pl.ANY)


def _exchange_shapes(arrays, scatter):
    out = []
    for a, s in zip(arrays, scatter):
        slab = a.shape[1:] if s else a.shape
        out.append(jax.ShapeDtypeStruct((N_DEV,) + tuple(slab), a.dtype))
    return out


def _exchange_sems(n):
    return [pltpu.SemaphoreType.DMA((n * (N_DEV - 1),)), pltpu.SemaphoreType.DMA((n * (N_DEV - 1),)),
            pltpu.SemaphoreType.DMA((n,))]


def _exchange_copies(ins, outs, scatter, send_sems, recv_sems, local_sems):
    n = len(ins)
    x, y, c = lax.axis_index("x"), lax.axis_index("y"), lax.axis_index("c")
    me = 4 * x + 2 * y + c
    copies = []
    for a in range(n):
        src = ins[a].at[me] if scatter[a] else ins[a]
        copies.append(pltpu.make_async_copy(src, outs[a].at[me], local_sems.at[a]))
    for k in range(1, N_DEV):
        px = 1 - x if k & 4 else x
        py = 1 - y if k & 2 else y
        pc = 1 - c if k & 1 else c
        pid = 4 * px + 2 * py + pc
        for a in range(n):
            src = ins[a].at[pid] if scatter[a] else ins[a]
            copies.append(pltpu.make_async_remote_copy(
                src_ref=src, dst_ref=outs[a].at[me],
                send_sem=send_sems.at[a * (N_DEV - 1) + k - 1], recv_sem=recv_sems.at[a * (N_DEV - 1) + k - 1],
                device_id=(px, py, pc), device_id_type=MESH))
    return copies


def _qkvu(x, g1, w_in):
    S = x.shape[0]

    def body(x_ref, g_ref, w_ref, q_ref, k_ref, v_ref, u_ref, hn_ref):
        r, n = _rms(x_ref[...])
        hn = (n * g_ref[...]).astype(BF16)
        hn_ref[...] = hn
        outs = (q_ref, k_ref, v_ref, u_ref)
        for j in range(N_DEV):
            z = jnp.dot(hn, w_ref[j], preferred_element_type=F32)
            if j < 2:
                z = z * (HEAD_DIM ** -0.5)
            outs[j // 2][:, (j % 2) * 256:(j % 2 + 1) * 256] = z

    tok = lambda w: pl.BlockSpec((TM, w), lambda i: (i, 0))
    return pl.pallas_call(
        body, name="qkvu", grid=(S // TM,),
        in_specs=[tok(D_MODEL), _const((1, D_MODEL)), _const(w_in.shape)],
        out_specs=[tok(512)] * 4 + [tok(D_MODEL)],
        out_shape=[jax.ShapeDtypeStruct((S, 512), F32)] * 4 + [jax.ShapeDtypeStruct((S, D_MODEL), BF16)],
        compiler_params=_params("arbitrary"),
    )(x, g1, w_in)


def _attn_fill_bias(bias_s, slope_ref, hp, d):
    qi = lax.broadcasted_iota(jnp.int32, (SPAN, 2 * SPAN), 0)
    kj = lax.broadcasted_iota(jnp.int32, (SPAN, 2 * SPAN), 1)
    for t, diff in enumerate((qi + SPAN - kj, qi - kj)):
        valid = (diff >= 0) & (diff <= SPAN)
        dist = diff.astype(F32) * float(d)
        for h in range(2):
            bias_s[t, h] = jnp.where(valid, -slope_ref[2 * hp + h] * dist, NEG)


def _attn_block(i, g, d, nb):
    if d >= ATTN_GROUP:
        per = d // ATTN_GROUP
        r = (i & (per - 1)) * ATTN_GROUP + g
        n = i >> (per.bit_length() - 1)
    else:
        r = 0
        n = i + g * (nb // ATTN_GROUP)
    k0 = jnp.maximum(n - 1, 0)

    def ds(block, nrows):
        start = block * (SPAN * d) + r
        return pl.ds(start, nrows, stride=d) if d > 1 else pl.ds(start, nrows)

    return ds(n, SPAN), ds(k0, 2 * SPAN), jnp.where(n == 0, 1, 0)


def _attn_groups(S, d):
    nb = S // d // SPAN
    assert nb >= 2 and (d * nb) % ATTN_GROUP == 0 and (d >= ATTN_GROUP or nb % ATTN_GROUP == 0)
    return nb, d * nb // ATTN_GROUP


def _attn_fwd(slopes, q, k, v, shards):
    S = q.shape[0]
    ns = len(shards)
    steps = ATT_WIDTH // LANES

    def body(slope_ref, q_ref, k_ref, v_ref, *rest):
        o_ref, lse_ref = rest[ns:ns + 2]
        m_s, l_s, bias_s = rest[2 * ns + 2:2 * ns + 5]
        hp = pl.program_id(0)
        first, passed, last = _gather2_copies(rest[:ns], rest[ns + 2:2 * ns + 2], *rest[2 * ns + 5:])

        @pl.when(hp == 0)
        def _():
            for cp in first:
                cp.start()

        @pl.when(hp == steps - 2)
        def _():
            for arrival, cp in passed:
                arrival.wait_recv()
                cp.start()

        is0 = lax.broadcasted_iota(jnp.int32, (SPAN, LANES), 1) < HEAD_DIM
        for pi, d in enumerate(DILATIONS):
            nb, ngroups = _attn_groups(S, d)
            _attn_fill_bias(bias_s, slope_ref, hp, d)

            def group(i, carry, d=d, pi=pi, nb=nb):
                blocks = [_attn_block(i, g, d, nb) for g in range(ATTN_GROUP)]
                loaded = [(q_ref[rows, :], k_ref[krows, :].astype(BF16), v_ref[krows, :].astype(BF16))
                          for rows, krows, _ in blocks]
                new = []
                for (rows, krows, tab), (qb, kb, vb) in zip(blocks, loaded):
                    parts = []
                    for h, is_h in enumerate((is0, ~is0)):
                        qh = jnp.where(is_h, qb, 0.0).astype(BF16)
                        s = lax.dot_general(qh, kb, NT, preferred_element_type=F32) + bias_s[tab, h]
                        m = jnp.max(s, axis=-1, keepdims=True)
                        e = jnp.exp(s - m)
                        l = jnp.sum(e, axis=-1, keepdims=True)
                        pv = jnp.dot(e.astype(BF16), vb, preferred_element_type=F32)
                        parts.append((m, l, pv))
                    new.append([jnp.where(is0, a0, a1) for a0, a1 in zip(*parts)])
                if pi > 0:
                    old = [(m_s[rows, :], l_s[rows, :], o_ref[rows, :]) for rows, _, _ in blocks]
                    for st, (m_o, l_o, o_o) in zip(new, old):
                        m_n = jnp.maximum(m_o, st[0])
                        a_o = jnp.exp(m_o - m_n)
                        a_b = jnp.exp(st[0] - m_n)
                        st[:] = [m_n, a_o * l_o + a_b * st[1], a_o * o_o + a_b * st[2]]
                for (rows, _, _), (m_b, l_b, acc) in zip(blocks, new):
                    if pi == len(DILATIONS) - 1:
                        o_ref[rows, :] = acc / l_b
                        lse_ref[rows, :] = m_b + jnp.log(l_b)
                    else:
                        o_ref[rows, :] = acc
                        m_s[rows, :] = m_b
                        l_s[rows, :] = l_b
                return carry

            lax.fori_loop(0, ngroups, group, 0)

        @pl.when(hp == steps - 1)
        def _():
            for cp in last:
                cp()

    col = pl.BlockSpec((S, LANES), lambda i: (0, i))
    res = pl.pallas_call(
        body, name="attn_fwd", grid=(steps,),
        in_specs=[pl.BlockSpec(memory_space=pltpu.SMEM), col, col, col] + [ANY_SPEC] * ns,
        out_specs=[col, col] + [ANY_SPEC] * ns,
        out_shape=[jax.ShapeDtypeStruct((S, ATT_WIDTH), F32)] * 2 + _exchange_shapes(shards, (False,) * ns),
        scratch_shapes=[pltpu.VMEM((S, LANES), F32), pltpu.VMEM((S, LANES), F32),
                        pltpu.VMEM((2, 2, SPAN, 2 * SPAN), F32)] + _exchange_sems(ns),
        compiler_params=_params("arbitrary"),
    )(slopes, q, k, v, *shards)
    return res[0], res[1], res[2:]


def _pool_count(i, w):
    t = i * TM + lax.broadcasted_iota(jnp.int32, (TM, 1), 0)
    return jnp.minimum(t + 1, w).astype(F32)


def _mix_out(x, att, u, pool_w, pool_scale, w_out):
    S = x.shape[0]

    def body(x_ref, att_ref, u_ref, pw_ref, ps_ref, w_ref, h1_ref, mix_ref, dlt_ref, ubuf):
        i = pl.program_id(0)

        @pl.when(i == 0)
        def _():
            ubuf[0:HALO, :] = jnp.zeros((HALO, POOL_WIDTH), F32)

        ubuf[HALO:HALO + TM, :] = u_ref[...]
        mix_ref[:, 0:ATT_WIDTH] = att_ref[...].astype(BF16)
        for g, w in enumerate(POOL_WINDOWS):
            cols = slice(g * POOL_GROUP, (g + 1) * POOL_GROUP)
            ug = ubuf[HALO:HALO + TM, cols]
            acc = ug
            for j in range(1, w):
                acc = acc + ubuf[HALO - j:HALO - j + TM, cols]
            dlt = (acc / _pool_count(i, w) - ug).astype(BF16)
            dlt_ref[:, cols] = dlt
            yg = jnp.dot(dlt, pw_ref[g].astype(BF16), preferred_element_type=F32) * ps_ref[:, cols]
            mix_ref[:, ATT_WIDTH + g * POOL_GROUP:ATT_WIDTH + (g + 1) * POOL_GROUP] = yg.astype(BF16)
        ubuf[0:HALO, :] = ubuf[TM:TM + HALO, :]
        h1_ref[...] = x_ref[...] + jnp.dot(mix_ref[...], w_ref[...], preferred_element_type=F32)

    tok = lambda w: pl.BlockSpec((TM, w), lambda i: (i, 0))
    return pl.pallas_call(
        body, name="mix_out", grid=(S // TM,),
        in_specs=[tok(D_MODEL), tok(ATT_WIDTH), tok(POOL_WIDTH), _const(pool_w.shape), _const((1, POOL_WIDTH)),
                  _const(w_out.shape)],
        out_specs=[tok(D_MODEL), tok(D_MODEL), tok(POOL_WIDTH)],
        out_shape=[jax.ShapeDtypeStruct((S, D_MODEL), F32), jax.ShapeDtypeStruct((S, D_MODEL), BF16),
                   jax.ShapeDtypeStruct((S, POOL_WIDTH), BF16)],
        scratch_shapes=[pltpu.VMEM((TM + HALO, POOL_WIDTH), F32)],
        compiler_params=_params("arbitrary"),
    )(x, att, u, pool_w, pool_scale, w_out)


def _conv_fwd(stage, upre, prev, cw, cb):
    T = upre.shape[0]
    stage[0:HALO, :] = prev
    stage[HALO:HALO + T, :] = upre
    return cb + cw[0:1, :] * stage[HALO - 2:HALO - 2 + T, :] + cw[1:2, :] * stage[HALO - 1:HALO - 1 + T, :] + cw[2:3, :] * upre


def _ffn_fwd(h1, g2, w_up, conv_w, conv_b, w_down):
    S = h1.shape[0]
    T = TM_FF

    def body(h1_ref, g_ref, wu_ref, cw_ref, cb_ref, wd_ref, h2_ref, hn_ref, up_ref, carry, stage):
        i = pl.program_id(0)

        @pl.when(i == 0)
        def _():
            carry[...] = jnp.zeros(carry.shape, F32)

        h1t = h1_ref[...]
        r, n = _rms(h1t)
        hn = (n * g_ref[...]).astype(BF16)
        hn_ref[...] = hn
        acc = h1t
        for j in range(4):
            conv = []
            for jj in (j, j + 4):
                upre = jnp.dot(hn, wu_ref[jj], preferred_element_type=F32)
                up_ref[jj] = upre.astype(BF16)
                conv.append(_conv_fwd(stage, upre, carry[jj], cw_ref[jj], cb_ref[jj]))
                carry[jj] = stage[T:T + HALO, :]
            gate, val = conv
            a = gate * jax.nn.sigmoid(gate) * val
            acc = acc + jnp.dot(a.astype(BF16), wd_ref[j], preferred_element_type=F32)
        h2_ref[...] = acc

    tok = lambda w: pl.BlockSpec((T, w), lambda i: (i, 0))
    return pl.pallas_call(
        body, name="ffn_fwd", grid=(S // T,),
        in_specs=[tok(D_MODEL), _const((1, D_MODEL)), _const(w_up.shape), _const(conv_w.shape), _const(conv_b.shape),
                  _const(w_down.shape)],
        out_specs=[tok(D_MODEL), tok(D_MODEL), pl.BlockSpec((N_DEV, T, FF_SHARD), lambda i: (0, i, 0))],
        out_shape=[jax.ShapeDtypeStruct((S, D_MODEL), F32), jax.ShapeDtypeStruct((S, D_MODEL), BF16),
                   jax.ShapeDtypeStruct((N_DEV, S, FF_SHARD), BF16)],
        scratch_shapes=[pltpu.VMEM((N_DEV, HALO, FF_SHARD), F32), pltpu.VMEM((T + HALO, FF_SHARD), F32)],
        compiler_params=_params("arbitrary"),
    )(h1, g2, w_up, conv_w, conv_b, w_down)


def _head(h2, p, g3, w_pg, w_ple, g4, target):
    S = h2.shape[0]
    nt = S // TM

    def body(h2_ref, p_ref, g3_ref, wpg_ref, wple_ref, g4_ref, t_ref,
             loss_ref, dh2_ref, dh2b_ref, hn3_ref, dgl_ref, dpe_ref, dg3_ref, dg4_ref, lacc):
        i = pl.program_id(0)

        @pl.when(i == 0)
        def _():
            lacc[...] = jnp.zeros(lacc.shape, F32)
            dg3_ref[...] = jnp.zeros(dg3_ref.shape, F32)
            dg4_ref[...] = jnp.zeros(dg4_ref.shape, F32)

        h2t = h2_ref[...]
        g3, g4 = g3_ref[...], g4_ref[...]
        r3, n3 = _rms(h2t)
        hn3 = (n3 * g3).astype(BF16)
        hn3_ref[...] = hn3
        gs = jax.nn.sigmoid(jnp.dot(hn3, wpg_ref[...], preferred_element_type=F32))
        pe = jnp.dot(p_ref[...].astype(BF16), wple_ref[...], preferred_element_type=F32)
        h3 = h2t + gs * pe
        r4, n4 = _rms(h3)
        err = n4 * g4 - t_ref[...]
        lacc[...] += _colsum(err * err)
        dy = err * (1.0 / D_MODEL)
        dg4_ref[...] += _colsum(dy * n4)
        dh3 = _rms_bwd(r4, n4, g4, dy)
        dpe_ref[...] = (dh3 * gs).astype(BF16)
        dgl = (dh3 * pe * gs * (1.0 - gs)).astype(BF16)
        dgl_ref[...] = dgl
        dhn3 = lax.dot_general(dgl, wpg_ref[...], NT, preferred_element_type=F32)
        dg3_ref[...] += _colsum(dhn3 * n3)
        dh2 = dh3 + _rms_bwd(r3, n3, g3, dhn3)
        dh2_ref[...] = dh2
        dh2b_ref[...] = dh2.astype(BF16)

        @pl.when(i == nt - 1)
        def _():
            tot = 0.5 / D_MODEL * jnp.sum(lacc[...], axis=-1, keepdims=True)
            loss_ref[...] = jnp.broadcast_to(tot, loss_ref.shape)

    tok = lambda w: pl.BlockSpec((TM, w), lambda i: (i, 0))
    row = pl.BlockSpec((1, D_MODEL), lambda i: (0, 0))
    act = lambda dt: jax.ShapeDtypeStruct((S, D_MODEL), dt)
    return pl.pallas_call(
        body, name="head", grid=(nt,),
        in_specs=[tok(D_MODEL), tok(PLE_DIM), _const((1, D_MODEL)), _const(w_pg.shape), _const(w_ple.shape),
                  _const((1, D_MODEL)), tok(D_MODEL)],
        out_specs=[pl.BlockSpec((8, LANES), lambda i: (0, 0)), tok(D_MODEL), tok(D_MODEL), tok(D_MODEL), tok(D_MODEL),
                   tok(D_MODEL), row, row],
        out_shape=[jax.ShapeDtypeStruct((8, LANES), F32), act(F32), act(BF16), act(BF16), act(BF16), act(BF16),
                   jax.ShapeDtypeStruct((1, D_MODEL), F32), jax.ShapeDtypeStruct((1, D_MODEL), F32)],
        scratch_shapes=[pltpu.VMEM((1, D_MODEL), F32)],
        compiler_params=_params("arbitrary"),
    )(h2, p, g3, w_pg, w_ple, g4, target)


def _wgrad(name, x, dy, x_kind, dy_kind, nj, k_dim, n_dim):
    S = x.shape[-2]
    nt = S // TK

    def spec(kind, width):
        if kind == "full":
            return pl.BlockSpec((TK, width), lambda j, t: (t, 0))
        return pl.BlockSpec((None, TK, width), lambda j, t: (j, t, 0))

    def body(x_ref, dy_ref, o_ref, acc):
        t = pl.program_id(1)

        @pl.when(t == 0)
        def _():
            acc[...] = jnp.zeros(acc.shape, F32)

        acc[...] += lax.dot_general(x_ref[...].astype(BF16), dy_ref[...], TN, preferred_element_type=F32)

        @pl.when(t == nt - 1)
        def _():
            o_ref[...] = acc[...].astype(BF16)

    return pl.pallas_call(
        body, name=name, grid=(nj, nt),
        in_specs=[spec(x_kind, k_dim), spec(dy_kind, n_dim)],
        out_specs=pl.BlockSpec((None, k_dim, n_dim), lambda j, t: (j, 0, 0)),
        out_shape=jax.ShapeDtypeStruct((nj, k_dim, n_dim), BF16),
        scratch_shapes=[pltpu.VMEM((k_dim, n_dim), F32)],
        compiler_params=_params("arbitrary", "arbitrary"),
    )(x, dy)


def _row_picker(T, off0, off1):
    r = lax.broadcasted_iota(jnp.int32, (2 * T, T + HALO), 0)
    c = lax.broadcasted_iota(jnp.int32, (2 * T, T + HALO), 1)
    want = jnp.where(r < T, r + off0, r - T + off1)
    return jnp.where(c == want, 1.0, 0.0).astype(BF16)


def _ffn_bwd_a(dh2b, up, w_down, conv_w, conv_b):
    S = dh2b.shape[0]
    T = TM_FF
    hb = T // HALO

    def body(dh_ref, up_ref, halo_ref, wd_ref, cw_ref, cb_ref, a_ref, dup_ref, dcw_ref, dcb_ref, stage):
        i = pl.program_id(0)

        @pl.when(i == 0)
        def _():
            dcw_ref[...] = jnp.zeros(dcw_ref.shape, F32)
            dcb_ref[...] = jnp.zeros(dcb_ref.shape, F32)

        dh = dh_ref[...]
        pick = _row_picker(T, HALO - 2, HALO - 1)
        for j in range(4):
            da = lax.dot_general(dh, wd_ref[j], NT, preferred_element_type=F32)
            conv, taps = [], []
            for jj in (j, j + 4):
                upre = up_ref[jj]
                stage[0:HALO, :] = jnp.where(i > 0, halo_ref[jj], jnp.zeros((HALO, FF_SHARD), BF16))
                stage[HALO:HALO + T, :] = upre
                prv = jnp.dot(pick, stage[...], preferred_element_type=F32)
                tp = (prv[0:T], prv[T:2 * T], upre.astype(F32))
                cw = cw_ref[jj]
                conv.append(cb_ref[jj] + cw[0:1, :] * tp[0] + cw[1:2, :] * tp[1] + cw[2:3, :] * tp[2])
                taps.append(tp)
            gate, val = conv
            sg = jax.nn.sigmoid(gate)
            silu = gate * sg
            a_ref[j] = (silu * val).astype(BF16)
            dgate = da * val * (sg * (1.0 + gate * (1.0 - sg)))
            dval = da * silu
            for jj, dup, tp in ((j, dgate, taps[0]), (j + 4, dval, taps[1])):
                dup_ref[jj] = dup.astype(BF16)
                dcb_ref[jj] += _colsum(dup)
                for kk in range(3):
                    dcw_ref[jj, kk:kk + 1, :] += _colsum(dup * tp[kk])

    tok = lambda w: pl.BlockSpec((T, w), lambda i: (i, 0))
    shard = pl.BlockSpec((N_DEV, T, FF_SHARD), lambda i: (0, i, 0))
    return pl.pallas_call(
        body, name="ffn_bwd_a", grid=(S // T,),
        in_specs=[tok(D_MODEL), shard,
                  pl.BlockSpec((N_DEV, HALO, FF_SHARD), lambda i: (0, jnp.maximum(i * hb - 1, 0), 0)),
                  _const(w_down.shape), _const(conv_w.shape), _const(conv_b.shape)],
        out_specs=[pl.BlockSpec((4, T, FF_SHARD), lambda i: (0, i, 0)), shard,
                   pl.BlockSpec((N_DEV, 3, FF_SHARD), lambda i: (0, 0, 0)),
                   pl.BlockSpec((N_DEV, 1, FF_SHARD), lambda i: (0, 0, 0))],
        out_shape=[jax.ShapeDtypeStruct((4, S, FF_SHARD), BF16), jax.ShapeDtypeStruct((N_DEV, S, FF_SHARD), BF16),
                   jax.ShapeDtypeStruct((N_DEV, 3, FF_SHARD), F32), jax.ShapeDtypeStruct((N_DEV, 1, FF_SHARD), F32)],
        scratch_shapes=[pltpu.VMEM((T + HALO, FF_SHARD), BF16)],
        compiler_params=_params("arbitrary"),
    )(dh2b, up, up, w_down, conv_w, conv_b)


def _ffn_bwd_b(dup, conv_w, w_up, h1, g2, dh2):
    S = h1.shape[0]
    T = TM_FF
    hb = T // HALO
    nt = S // T

    def body(dup_ref, halo_ref, cw_ref, wu_ref, h1_ref, g_ref, dh2_ref, dpre_ref, dh1_ref, dh1b_ref, dg_ref, stage):
        i = pl.program_id(0)

        @pl.when(i == 0)
        def _():
            dg_ref[...] = jnp.zeros(dg_ref.shape, F32)

        dhn = jnp.zeros((T, D_MODEL), F32)
        pick = _row_picker(T, 1, 2)
        for jj in range(N_DEV):
            dup = dup_ref[jj]
            stage[0:T, :] = dup
            stage[T:T + HALO, :] = jnp.where(i < nt - 1, halo_ref[jj], jnp.zeros((HALO, FF_SHARD), BF16))
            nxt = jnp.dot(pick, stage[...], preferred_element_type=F32)
            cw = cw_ref[jj]
            dpre = (cw[2:3, :] * dup.astype(F32) + cw[1:2, :] * nxt[0:T] + cw[0:1, :] * nxt[T:2 * T]).astype(BF16)
            dpre_ref[jj] = dpre
            dhn = dhn + lax.dot_general(dpre, wu_ref[jj], NT, preferred_element_type=F32)
        g = g_ref[...]
        r, n = _rms(h1_ref[...])
        dg_ref[...] += _colsum(dhn * n)
        dh1 = dh2_ref[...] + _rms_bwd(r, n, g, dhn)
        dh1_ref[...] = dh1
        dh1b_ref[...] = dh1.astype(BF16)

    tok = lambda w: pl.BlockSpec((T, w), lambda i: (i, 0))
    shard = pl.BlockSpec((N_DEV, T, FF_SHARD), lambda i: (0, i, 0))
    return pl.pallas_call(
        body, name="ffn_bwd_b", grid=(nt,),
        in_specs=[shard,
                  pl.BlockSpec((N_DEV, HALO, FF_SHARD), lambda i: (0, jnp.minimum((i + 1) * hb, S // HALO - 1), 0)),
                  _const(conv_w.shape), _const(w_up.shape), tok(D_MODEL), _const((1, D_MODEL)), tok(D_MODEL)],
        out_specs=[shard, tok(D_MODEL), tok(D_MODEL), pl.BlockSpec((1, D_MODEL), lambda i: (0, 0))],
        out_shape=[jax.ShapeDtypeStruct((N_DEV, S, FF_SHARD), BF16), jax.ShapeDtypeStruct((S, D_MODEL), F32),
                   jax.ShapeDtypeStruct((S, D_MODEL), BF16), jax.ShapeDtypeStruct((1, D_MODEL), F32)],
        scratch_shapes=[pltpu.VMEM((T + HALO, FF_SHARD), BF16)],
        compiler_params=_params("arbitrary"),
    )(dup, dup, conv_w, w_up, h1, g2, dh2)


def _mix_bwd(dh1b, w_out, dlt, pool_w, pool_scale):
    S = dh1b.shape[0]
    nt = S // TM

    def body(dh_ref, w_ref, dlt_ref, pw_ref, ps_ref, datt_ref, du_ref, dpw_ref, dps_ref, stage, carry):
        i = pl.program_id(0)
        tile = nt - 1 - i

        @pl.when(i == 0)
        def _():
            dpw_ref[...] = jnp.zeros(dpw_ref.shape, F32)
            dps_ref[...] = jnp.zeros(dps_ref.shape, F32)
            carry[...] = jnp.zeros(carry.shape, F32)

        dmix = lax.dot_general(dh_ref[...], w_ref[...], NT, preferred_element_type=F32)
        datt_ref[...] = dmix[:, 0:ATT_WIDTH]
        for g, w in enumerate(POOL_WINDOWS):
            cols = slice(g * POOL_GROUP, (g + 1) * POOL_GROUP)
            dpool = dmix[:, ATT_WIDTH + g * POOL_GROUP:ATT_WIDTH + (g + 1) * POOL_GROUP]
            dl = dlt_ref[:, cols]
            pw = pw_ref[g].astype(BF16)
            yg = jnp.dot(dl, pw, preferred_element_type=F32)
            dps_ref[:, cols] += _colsum(dpool * yg)
            dy = (dpool * ps_ref[:, cols]).astype(BF16)
            dpw_ref[g] += lax.dot_general(dl, dy, TN, preferred_element_type=F32)
            ddlt = lax.dot_general(dy, pw, NT, preferred_element_type=F32)
            cg = ddlt / _pool_count(tile, w)
            stage[0:TM, :] = cg
            stage[TM:TM + HALO, :] = carry[:, cols]
            acc = cg
            for j in range(1, w):
                acc = acc + stage[j:j + TM, :]
            du_ref[:, cols] = acc - ddlt
            carry[:, cols] = cg[0:HALO, :]

    tok = lambda w: pl.BlockSpec((TM, w), lambda i: (nt - 1 - i, 0))
    return pl.pallas_call(
        body, name="mix_bwd", grid=(nt,),
        in_specs=[tok(D_MODEL), _const(w_out.shape), tok(POOL_WIDTH), _const(pool_w.shape), _const((1, POOL_WIDTH))],
        out_specs=[tok(ATT_WIDTH), tok(POOL_WIDTH), pl.BlockSpec(pool_w.shape, lambda i: (0, 0, 0)),
                   pl.BlockSpec((1, POOL_WIDTH), lambda i: (0, 0))],
        out_shape=[jax.ShapeDtypeStruct((S, ATT_WIDTH), F32), jax.ShapeDtypeStruct((S, POOL_WIDTH), F32),
                   jax.ShapeDtypeStruct(pool_w.shape, F32), jax.ShapeDtypeStruct((1, POOL_WIDTH), F32)],
        scratch_shapes=[pltpu.VMEM((TM + HALO, POOL_GROUP), F32), pltpu.VMEM((HALO, POOL_WIDTH), F32)],
        compiler_params=_params("arbitrary"),
    )(dh1b, w_out, dlt, pool_w, pool_scale)


def _attn_bwd(slopes, q, k, v, o, lse, do, grads):
    S = q.shape[0]
    CH = 512
    ng = len(grads)
    steps = ATT_WIDTH // LANES

    def body(slope_ref, q_ref, k_ref, v_ref, o_ref, lse_ref, do_ref, *rest):
        dq_ref, dk_ref, dv_ref = rest[ng:ng + 3]
        dl_s, bias_s = rest[2 * ng + 3:2 * ng + 5]
        hp = pl.program_id(0)
        copies = _exchange_copies(rest[:ng], rest[ng + 3:2 * ng + 3], (True,) * ng, *rest[2 * ng + 5:])

        @pl.when(hp == 0)
        def _():
            for cp in copies:
                cp.start()

        is0 = lax.broadcasted_iota(jnp.int32, (SPAN, LANES), 1) < HEAD_DIM
        is0c = lax.broadcasted_iota(jnp.int32, (CH, LANES), 1) < HEAD_DIM
        is0k = lax.broadcasted_iota(jnp.int32, (2 * SPAN, LANES), 1) < HEAD_DIM

        def prep(ci, carry):
            rows = pl.ds(pl.multiple_of(ci * CH, CH), CH)
            prod = do_ref[rows, :] * o_ref[rows, :]
            d0 = jnp.sum(jnp.where(is0c, prod, 0.0), axis=-1, keepdims=True)
            d1 = jnp.sum(jnp.where(is0c, 0.0, prod), axis=-1, keepdims=True)
            dl_s[rows, :] = jnp.where(is0c, d0, d1)
            zero = jnp.zeros((CH, LANES), F32)
            dq_ref[rows, :] = zero
            dk_ref[rows, :] = zero
            dv_ref[rows, :] = zero
            return carry

        lax.fori_loop(0, S // CH, prep, 0)

        for d in DILATIONS:
            nb, ngroups = _attn_groups(S, d)
            _attn_fill_bias(bias_s, slope_ref, hp, d)

            def group(i, carry, d=d, nb=nb):
                blocks = [_attn_block(i, g, d, nb) for g in range(ATTN_GROUP)]
                loaded = [(q_ref[rows, :], do_ref[rows, :], lse_ref[rows, :], dl_s[rows, :], k_ref[krows, :],
                           v_ref[krows, :].astype(BF16)) for rows, krows, _ in blocks]
                new = []
                for (rows, krows, tab), (qb, dob, lse_b, dl_b, kf, vb) in zip(blocks, loaded):
                    kb = kf.astype(BF16)
                    dq_c = jnp.zeros((SPAN, LANES), F32)
                    dk_c = jnp.zeros((2 * SPAN, LANES), F32)
                    dv_c = jnp.zeros((2 * SPAN, LANES), F32)
                    for h, (is_h, is_hk) in enumerate(((is0, is0k), (~is0, ~is0k))):
                        lo = h * HEAD_DIM
                        qh = jnp.where(is_h, qb, 0.0).astype(BF16)
                        doh = jnp.where(is_h, dob, 0.0).astype(BF16)
                        kh = jnp.where(is_hk, kf, 0.0).astype(BF16)
                        s = lax.dot_general(qh, kb, NT, preferred_element_type=F32) + bias_s[tab, h]
                        pr = jnp.exp(s - lse_b[:, lo:lo + 1])
                        dp = lax.dot_general(doh, vb, NT, preferred_element_type=F32)
                        ds = (pr * (dp - dl_b[:, lo:lo + 1])).astype(BF16)
                        dv_c = dv_c + lax.dot_general(pr.astype(BF16), doh, TN, preferred_element_type=F32)
                        dk_c = dk_c + lax.dot_general(ds, qh, TN, preferred_element_type=F32)
                        dq_c = dq_c + jnp.dot(ds, kh, preferred_element_type=F32)
                    new.append((dq_c, dk_c, dv_c))
                old = [(dq_ref[rows, :], dk_ref[krows, :], dv_ref[krows, :]) for rows, krows, _ in blocks]
                for (rows, krows, _), (dq_c, dk_c, dv_c), (dq_o, dk_o, dv_o) in zip(blocks, new, old):
                    dq_ref[rows, :] = dq_o + dq_c
                    dk_ref[krows, :] = dk_o + dk_c
                    dv_ref[krows, :] = dv_o + dv_c
                return carry

            lax.fori_loop(0, ngroups, group, 0)

        @pl.when(hp == steps - 1)
        def _():
            for cp in copies:
                cp.wait()

    col = pl.BlockSpec((S, LANES), lambda i: (0, i))
    res = pl.pallas_call(
        body, name="attn_bwd", grid=(steps,),
        in_specs=[pl.BlockSpec(memory_space=pltpu.SMEM)] + [col] * 6 + [ANY_SPEC] * ng,
        out_specs=[col] * 3 + [ANY_SPEC] * ng,
        out_shape=[jax.ShapeDtypeStruct((S, ATT_WIDTH), F32)] * 3 + _exchange_shapes(grads, (True,) * ng),
        scratch_shapes=[pltpu.VMEM((S, LANES), F32), pltpu.VMEM((2, 2, SPAN, 2 * SPAN), F32)] + _exchange_sems(ng),
        compiler_params=_params("arbitrary"),
    )(slopes, q, k, v, o, lse, do, *grads)
    return res[0], res[1], res[2], res[3:]


def _in_bwd(dq, dk, dv, du, w_in, x, g1, dh1):
    S = x.shape[0]

    def body(dq_ref, dk_ref, dv_ref, du_ref, w_ref, x_ref, g_ref, dh1_ref, dz_ref, dx_ref, dg_ref):
        @pl.when(pl.program_id(0) == 0)
        def _():
            dg_ref[...] = jnp.zeros(dg_ref.shape, F32)

        srcs = (dq_ref, dk_ref, dv_ref, du_ref)
        dhn = jnp.zeros((TM, D_MODEL), F32)
        for j in range(N_DEV):
            dz = srcs[j // 2][:, (j % 2) * 256:(j % 2 + 1) * 256]
            if j < 2:
                dz = dz * (HEAD_DIM ** -0.5)
            dz = dz.astype(BF16)
            dz_ref[j] = dz
            dhn = dhn + lax.dot_general(dz, w_ref[j], NT, preferred_element_type=F32)
        g = g_ref[...]
        r, n = _rms(x_ref[...])
        dg_ref[...] += _colsum(dhn * n)
        dx_ref[...] = dh1_ref[...] + _rms_bwd(r, n, g, dhn)

    tok = lambda w: pl.BlockSpec((TM, w), lambda i: (i, 0))
    return pl.pallas_call(
        body, name="in_bwd", grid=(S // TM,),
        in_specs=[tok(512)] * 4 + [_const(w_in.shape), tok(D_MODEL), _const((1, D_MODEL)), tok(D_MODEL)],
        out_specs=[pl.BlockSpec((N_DEV, TM, 256), lambda i: (0, i, 0)), tok(D_MODEL),
                   pl.BlockSpec((1, D_MODEL), lambda i: (0, 0))],
        out_shape=[jax.ShapeDtypeStruct((N_DEV, S, 256), BF16), jax.ShapeDtypeStruct((S, D_MODEL), F32),
                   jax.ShapeDtypeStruct((1, D_MODEL), F32)],
        compiler_params=_params("arbitrary"),
    )(dq, dk, dv, du, w_in, x, g1, dh1)


def _adamw(name, parts, w, m, v):
    R, C = w.shape
    rb = R
    for cand in (256, 128, 64, 32, 16, 8):
        if R % cand == 0 and R > cand:
            rb = cand
            break

    def body(p_ref, w_ref, m_ref, v_ref, g_ref, d_ref, mo_ref, vo_ref):
        g = p_ref[0].astype(F32)
        for s in range(1, N_DEV):
            g = g + p_ref[s].astype(F32)
        m_new = ADAM_B1 * m_ref[...] + (1.0 - ADAM_B1) * g
        v_new = ADAM_B2 * v_ref[...] + (1.0 - ADAM_B2) * (g * g)
        m_hat = m_new / (1.0 - ADAM_B1 ** ADAM_STEP)
        v_hat = v_new / (1.0 - ADAM_B2 ** ADAM_STEP)
        g_ref[...] = g
        d_ref[...] = -ADAM_LR * (m_hat / (jnp.sqrt(v_hat) + ADAM_EPS) + ADAM_WD * w_ref[...])
        mo_ref[...] = m_new
        vo_ref[...] = v_new

    blk = pl.BlockSpec((rb, C), lambda i: (i, 0))
    return pl.pallas_call(
        body, name=name, grid=(R // rb,),
        in_specs=[pl.BlockSpec((N_DEV, rb, C), lambda i: (0, i, 0)), blk, blk, blk],
        out_specs=[blk] * 4,
        out_shape=[jax.ShapeDtypeStruct((R, C), F32)] * 4,
        compiler_params=_params("arbitrary"),
    )(parts, w, m, v)


def _rows(a):
    flat = a.reshape(-1)
    rows = -(-flat.shape[0] // LANES)
    rows8 = -(-rows // 8) * 8
    flat = jnp.pad(flat, (0, rows8 * LANES - flat.shape[0]))
    return flat.reshape(rows8, LANES)


def kernel(x, p, ln_mix, w_in, pool_w, pool_scale, w_out, ln_ffn, w_up, conv_w, conv_b, w_down, ln_ple, w_ple_gate, w_ple, ln_final, loss_target, m_ln_mix, m_w_in, m_pool_w, m_pool_scale, m_w_out, m_ln_ffn, m_w_up, m_conv_w, m_conv_b, m_w_down, m_ln_ple, m_w_ple_gate, m_w_ple, m_ln_final, v_ln_mix, v_w_in, v_pool_w, v_pool_scale, v_w_out, v_ln_ffn, v_w_up, v_conv_w, v_conv_b, v_w_down, v_ln_ple, v_w_ple_gate, v_w_ple, v_ln_final):
    xs, ps, tgt, pool_w0 = x[0], p[0, 0], loss_target[0], pool_w[0]
    slopes = jnp.exp2(-8.0 * (jnp.arange(N_HEADS, dtype=F32) + 1.0) / N_HEADS)
    conv_b_s = conv_b.reshape(N_DEV, 1, FF_SHARD)

    (w_in_g,) = _exchange("gather_w_in", [w_in[0].astype(BF16)], (False,))
    q, k, v, u, hn1 = _qkvu(xs, ln_mix, w_in_g)
    att, lse, (w_out_g, w_up_g, w_down_g, w_pg_g, w_ple_g, conv_w_g) = _attn_fwd(
        slopes, q, k, v,
        [w_out[0].astype(BF16), w_up[0].astype(BF16), w_down[0].astype(BF16), w_ple_gate[0].astype(BF16),
         w_ple[0].astype(BF16), conv_w[0]])
    w_out_f = w_out_g.reshape(D_MODEL, D_MODEL)
    w_down_f = w_down_g.reshape(4, FF_SHARD, D_MODEL)
    w_pg_f = w_pg_g.reshape(D_MODEL, D_MODEL)
    w_ple_f = jnp.transpose(w_ple_g, (1, 0, 2)).reshape(PLE_DIM, D_MODEL)
    h1, mix, dlt = _mix_out(xs, att, u, pool_w0, pool_scale, w_out_f)
    h2, hn2, up = _ffn_fwd(h1, ln_ffn, w_up_g, conv_w_g, conv_b_s, w_down_f)
    loss_blk, dh2, dh2b, hn3, dgl, dpe, d_ln_ple, d_ln_final = _head(
        h2, ps, ln_ple, w_pg_f, w_ple_f, ln_final.reshape(1, D_MODEL), tgt)

    a, dup, d_conv_w, d_conv_b = _ffn_bwd_a(dh2b, up, w_down_f, conv_w_g, conv_b_s)
    dpre, dh1, dh1b, d_ln_ffn = _ffn_bwd_b(dup, conv_w_g, w_up_g, h1, ln_ffn, dh2)
    datt, du, d_pool_w, d_pool_scale = _mix_bwd(dh1b, w_out_f, dlt, pool_w0, pool_scale)
    d_w_out = _wgrad("dw_out", mix, dh1b, "full", "full", 1, D_MODEL, D_MODEL).reshape(N_DEV, D_MODEL // N_DEV, D_MODEL)
    d_w_up = _wgrad("dw_up", hn2, dpre, "full", "lead", N_DEV, D_MODEL, FF_SHARD)
    d_w_down = _wgrad("dw_down", a, dh2b, "lead", "full", 4, FF_SHARD, D_MODEL).reshape(N_DEV, D_FF // N_DEV, D_MODEL)
    d_w_pg = _wgrad("dw_ple_gate", hn3, dgl, "full", "full", 1, D_MODEL, D_MODEL).reshape(N_DEV, D_MODEL // N_DEV, D_MODEL)
    d_w_ple = _wgrad("dw_ple", ps, dpe, "full", "full", 1, PLE_DIM, D_MODEL)
    d_w_ple = jnp.transpose(d_w_ple.reshape(PLE_DIM, N_DEV, LANES), (1, 0, 2))
    dq, dk, dv, (r_w_out, r_w_up, r_conv_w, r_w_down, r_w_pg, r_w_ple) = _attn_bwd(
        slopes, q, k, v, att, lse, datt, [d_w_out, d_w_up, d_conv_w, d_w_down, d_w_pg, d_w_ple])
    dz, grad_x, d_ln_mix = _in_bwd(dq, dk, dv, du, w_in_g, xs, ln_mix, dh1)
    d_w_in = _wgrad("dw_in", hn1, dz, "full", "lead", N_DEV, D_MODEL, 256)
    rep_g = dict(ln_mix=d_ln_mix, pool_w=d_pool_w, pool_scale=d_pool_scale, ln_ffn=d_ln_ffn, conv_b=d_conv_b,
                 ln_ple=d_ln_ple, ln_final=d_ln_final)

    rep_names = ("ln_mix", "pool_w", "pool_scale", "ln_ffn", "conv_b", "ln_ple", "ln_final")
    rep_w = dict(ln_mix=ln_mix, pool_w=pool_w, pool_scale=pool_scale, ln_ffn=ln_ffn, conv_b=conv_b, ln_ple=ln_ple,
                 ln_final=ln_final)
    rep_m = dict(ln_mix=m_ln_mix, pool_w=m_pool_w, pool_scale=m_pool_scale, ln_ffn=m_ln_ffn, conv_b=m_conv_b,
                 ln_ple=m_ln_ple, ln_final=m_ln_final)
    rep_v = dict(ln_mix=v_ln_mix, pool_w=v_pool_w, pool_scale=v_pool_scale, ln_ffn=v_ln_ffn, conv_b=v_conv_b,
                 ln_ple=v_ln_ple, ln_final=v_ln_final)
    pack = lambda d: jnp.concatenate([_rows(d[n]) for n in rep_names], axis=0)
    g_pack = jnp.concatenate([pack(rep_g), loss_blk], axis=0)
    zero_blk = jnp.zeros((8, LANES), F32)
    w_pack = jnp.concatenate([pack(rep_w), zero_blk], axis=0)
    m_pack = jnp.concatenate([pack(rep_m), zero_blk], axis=0)
    v_pack = jnp.concatenate([pack(rep_v), zero_blk + 1.0], axis=0)

    r_w_in, r_pack = _exchange("exchange_tail", [d_w_in, g_pack], (True, False))

    sharded = {}
    sharded["w_in"] = _adamw("adamw_w_in", r_w_in, w_in[0], m_w_in[0], v_w_in[0])
    sharded["w_out"] = _adamw("adamw_w_out", r_w_out, w_out[0], m_w_out[0], v_w_out[0])
    sharded["w_up"] = _adamw("adamw_w_up", r_w_up, w_up[0], m_w_up[0], v_w_up[0])
    sharded["conv_w"] = _adamw("adamw_conv_w", r_conv_w, conv_w[0], m_conv_w[0], v_conv_w[0])
    sharded["w_down"] = _adamw("adamw_w_down", r_w_down, w_down[0], m_w_down[0], v_w_down[0])
    sharded["w_ple_gate"] = _adamw("adamw_w_ple_gate", r_w_pg, w_ple_gate[0], m_w_ple_gate[0], v_w_ple_gate[0])
    sharded["w_ple"] = _adamw("adamw_w_ple", r_w_ple, w_ple[0], m_w_ple[0], v_w_ple[0])
    packed = _adamw("adamw_replicated", r_pack, w_pack, m_pack, v_pack)

    loss = packed[0][-8, 0]
    offs, o = {}, 0
    for n in rep_names:
        offs[n] = o
        o += _rows(rep_w[n]).shape[0]

    weights = dict(w_in=w_in, w_out=w_out, w_up=w_up, conv_w=conv_w, w_down=w_down, w_ple_gate=w_ple_gate, w_ple=w_ple,
                   **rep_w)

    def leaf(kind, n):
        shape = weights[n].shape
        if n in sharded:
            return sharded[n][kind].reshape(shape)
        size = 1
        for s in shape:
            size *= s
        rows = -(-size // LANES)
        return packed[kind][offs[n]:offs[n] + rows].reshape(-1)[:size].reshape(shape)

    order = ("ln_mix", "w_in", "pool_w", "pool_scale", "w_out", "ln_ffn", "w_up", "conv_w", "conv_b", "w_down", "ln_ple",
             "w_ple_gate", "w_ple", "ln_final")
    outs = [loss, grad_x[None]]
    for kind in range(4):
        outs += [leaf(kind, n) for n in order]
    return tuple(outs)
```

```python
import jax
import jax.numpy as jnp
from jax import lax
from jax.experimental import pallas as pl
from jax.experimental.pallas import tpu as pltpu

F32 = jnp.float32
BF16 = jnp.bfloat16

N_DEV = 8
D_MODEL = 1024
ATT_WIDTH = 512
POOL_WIDTH = 512
N_HEADS = 8
HEAD_DIM = 64
SPAN = 128
DILATIONS = (1, 4, 16)
POOL_WINDOWS = (2, 4, 8, 16)
POOL_GROUP = 128
D_FF = 2816
FF_SHARD = 2 * D_FF // N_DEV
PLE_DIM = 256
EPS = 1e-6
NEG = -1e30

ADAM_LR = 0.001
ADAM_B1 = 0.9
ADAM_B2 = 0.999
ADAM_EPS = 1e-08
ADAM_WD = 0.01
ADAM_STEP = 10

LANES = 128
HALO = 16
TM = 512
TM_FF = 256
TK = 1024
ATTN_GROUP = 4
VMEM_LIMIT = 56 * 1024 * 1024

MESH = pl.DeviceIdType.MESH
NT = (((1,), (1,)), ((), ()))
TN = (((0,), (0,)), ((), ()))


def _params(*sem):
    return pltpu.CompilerParams(dimension_semantics=sem or None, vmem_limit_bytes=VMEM_LIMIT)


def _const(shape):
    n = len(shape)
    return pl.BlockSpec(shape, lambda *_: (0,) * n, pipeline_mode=pl.Buffered(1))


def _rms(h):
    r = lax.rsqrt(jnp.mean(h * h, axis=-1, keepdims=True) + EPS)
    return r, h * r


def _rms_bwd(r, n, g, dhn):
    dn = dhn * g
    return r * (dn - n * jnp.mean(dn * n, axis=-1, keepdims=True))


def _colsum(a):
    return jnp.sum(a, axis=0, keepdims=True)


def _gather2_copies(ins, outs, send_sems, recv_sems, local_sems):
    n = len(ins)
    x, y, c = lax.axis_index("x"), lax.axis_index("y"), lax.axis_index("c")
    slot = lambda px, py, pc: 4 * px + 2 * py + pc
    chips = [(x, 1 - y), (1 - x, y), (1 - x, 1 - y)]
    first, passed, last = [], [], []

    def remote(a, r, src, dst_slot, to):
        return pltpu.make_async_remote_copy(
            src_ref=src, dst_ref=outs[a].at[dst_slot],
            send_sem=send_sems.at[a * (N_DEV - 1) + r], recv_sem=recv_sems.at[a * (N_DEV - 1) + r],
            device_id=to, device_id_type=MESH)

    for a in range(n):
        mine = pltpu.make_async_copy(ins[a], outs[a].at[slot(x, y, c)], local_sems.at[a])
        to_sibling = remote(a, 0, ins[a], slot(x, y, c), (x, y, 1 - c))
        first += [mine, to_sibling]
        last += [mine.wait, to_sibling.wait_send, to_sibling.wait_recv]
        for r, (px, py) in enumerate(chips, start=1):
            to_chip = remote(a, r, ins[a], slot(x, y, c), (px, py, c))
            onward = remote(a, 3 + r, outs[a].at[slot(px, py, c)], slot(px, py, c), (x, y, 1 - c))
            first.append(to_chip)
            passed.append((to_chip, onward))
            last += [to_chip.wait_send, onward.wait_send, onward.wait_recv]
    return first, passed, last


def _exchange(name, arrays, scatter):
    n = len(arrays)

    def body(*refs):
        copies = _exchange_copies(refs[:n], refs[n:2 * n], scatter, *refs[2 * n:])
        for cp in copies:
            cp.start()
        for cp in copies:
            cp.wait()

    return pl.pallas_call(
        body, name=name,
        in_specs=[ANY_SPEC] * n, out_specs=[ANY_SPEC] * n, out_shape=_exchange_shapes(arrays, scatter),
        scratch_shapes=_exchange_sems(n),
    )(*arrays)


ANY_SPEC = pl.BlockSpec(memory_space=pl.ANY)


def _exchange_shapes(arrays, scatter):
    out = []
    for a, s in zip(arrays, scatter):
        slab = a.shape[1:] if s else a.shape
        out.append(jax.ShapeDtypeStruct((N_DEV,) + tuple(slab), a.dtype))
    return out


def _exchange_sems(n):
    return [pltpu.SemaphoreType.DMA((n * (N_DEV - 1),)), pltpu.SemaphoreType.DMA((n * (N_DEV - 1),)),
            pltpu.SemaphoreType.DMA((n,))]


def _exchange_copies(ins, outs, scatter, send_sems, recv_sems, local_sems):
    n = len(ins)
    x, y, c = lax.axis_index("x"), lax.axis_index("y"), lax.axis_index("c")
    me = 4 * x + 2 * y + c
    copies = []
    for a in range(n):
        src = ins[a].at[me] if scatter[a] else ins[a]
        copies.append(pltpu.make_async_copy(src, outs[a].at[me], local_sems.at[a]))
    for k in range(1, N_DEV):
        px = 1 - x if k & 4 else x
        py = 1 - y if k & 2 else y
        pc = 1 - c if k & 1 else c
        pid = 4 * px + 2 * py + pc
        for a in range(n):
            src = ins[a].at[pid] if scatter[a] else ins[a]
            copies.append(pltpu.make_async_remote_copy(
                src_ref=src, dst_ref=outs[a].at[me],
                send_sem=send_sems.at[a * (N_DEV - 1) + k - 1], recv_sem=recv_sems.at[a * (N_DEV - 1) + k - 1],
                device_id=(px, py, pc), device_id_type=MESH))
    return copies


def _qkvu(x, g1, w_in):
    S = x.shape[0]

    def body(x_ref, g_ref, w_ref, q_ref, k_ref, v_ref, u_ref, hn_ref):
        r, n = _rms(x_ref[...])
        hn = (n * g_ref[...]).astype(BF16)
        hn_ref[...] = hn
        outs = (q_ref, k_ref, v_ref, u_ref)
        for j in range(N_DEV):
            z = jnp.dot(hn, w_ref[j], preferred_element_type=F32)
            if j < 2:
                z = z * (HEAD_DIM ** -0.5)
            outs[j // 2][:, (j % 2) * 256:(j % 2 + 1) * 256] = z

    tok = lambda w: pl.BlockSpec((TM, w), lambda i: (i, 0))
    return pl.pallas_call(
        body, name="qkvu", grid=(S // TM,),
        in_specs=[tok(D_MODEL), _const((1, D_MODEL)), _const(w_in.shape)],
        out_specs=[tok(512)] * 4 + [tok(D_MODEL)],
        out_shape=[jax.ShapeDtypeStruct((S, 512), F32)] * 4 + [jax.ShapeDtypeStruct((S, D_MODEL), BF16)],
        compiler_params=_params("arbitrary"),
    )(x, g1, w_in)


def _attn_fill_bias(bias_s, slope_ref, hp, d):
    qi = lax.broadcasted_iota(jnp.int32, (SPAN, 2 * SPAN), 0)
    kj = lax.broadcasted_iota(jnp.int32, (SPAN, 2 * SPAN), 1)
    for t, diff in enumerate((qi + SPAN - kj, qi - kj)):
        valid = (diff >= 0) & (diff <= SPAN)
        dist = diff.astype(F32) * float(d)
        for h in range(2):
            bias_s[t, h * SPAN:(h + 1) * SPAN, :] = jnp.where(valid, -slope_ref[2 * hp + h] * dist, NEG)


def _stack_heads(x, is0):
    return jnp.concatenate([jnp.where(is0, x, 0.0), jnp.where(is0, 0.0, x)], axis=0)


def _unstack_heads(y, is0):
    return jnp.where(is0, y[0:SPAN], y[SPAN:2 * SPAN])


def _attn_block(i, g, d, nb):
    if d >= ATTN_GROUP:
        per = d // ATTN_GROUP
        r = (i & (per - 1)) * ATTN_GROUP + g
        n = i >> (per.bit_length() - 1)
    else:
        r = 0
        n = i + g * (nb // ATTN_GROUP)
    k0 = jnp.maximum(n - 1, 0)

    def ds(block, nrows):
        start = block * (SPAN * d) + r
        return pl.ds(start, nrows, stride=d) if d > 1 else pl.ds(start, nrows)

    return ds(n, SPAN), ds(k0, 2 * SPAN), jnp.where(n == 0, 1, 0)


def _attn_groups(S, d):
    nb = S // d // SPAN
    assert nb >= 2 and (d * nb) % ATTN_GROUP == 0 and (d >= ATTN_GROUP or nb % ATTN_GROUP == 0)
    return nb, d * nb // ATTN_GROUP


def _attn_fwd(slopes, q, k, v, shards):
    S = q.shape[0]
    ns = len(shards)
    steps = ATT_WIDTH // LANES

    def body(slope_ref, q_ref, k_ref, v_ref, *rest):
        o_ref, lse_ref = rest[ns:ns + 2]
        m_s, l_s, bias_s = rest[2 * ns + 2:2 * ns + 5]
        hp = pl.program_id(0)
        first, passed, last = _gather2_copies(rest[:ns], rest[ns + 2:2 * ns + 2], *rest[2 * ns + 5:])

        @pl.when(hp == 0)
        def _():
            for cp in first:
                cp.start()

        @pl.when(hp == steps - 1)
        def _():
            for arrival, cp in passed:
                arrival.wait_recv()
                cp.start()

        is0 = lax.broadcasted_iota(jnp.int32, (SPAN, LANES), 1) < HEAD_DIM
        for pi, d in enumerate(DILATIONS):
            nb, ngroups = _attn_groups(S, d)
            _attn_fill_bias(bias_s, slope_ref, hp, d)

            def group(i, carry, d=d, pi=pi, nb=nb):
                blocks = [_attn_block(i, g, d, nb) for g in range(ATTN_GROUP)]
                loaded = [(q_ref[rows, :], k_ref[krows, :].astype(BF16), v_ref[krows, :].astype(BF16))
                          for rows, krows, _ in blocks]
                new = []
                for (rows, krows, tab), (qb, kb, vb) in zip(blocks, loaded):
                    qs = _stack_heads(qb, is0).astype(BF16)
                    s = lax.dot_general(qs, kb, NT, preferred_element_type=F32) + bias_s[tab]
                    m = jnp.max(s, axis=-1, keepdims=True)
                    e = jnp.exp(s - m)
                    l = jnp.sum(e, axis=-1, keepdims=True)
                    pv = jnp.dot(e.astype(BF16), vb, preferred_element_type=F32)
                    new.append([_unstack_heads(jnp.broadcast_to(m, pv.shape), is0),
                                _unstack_heads(jnp.broadcast_to(l, pv.shape), is0), _unstack_heads(pv, is0)])
                if pi > 0:
                    old = [(m_s[rows, :], l_s[rows, :], o_ref[rows, :]) for rows, _, _ in blocks]
                    for st, (m_o, l_o, o_o) in zip(new, old):
                        m_n = jnp.maximum(m_o, st[0])
                        a_o = jnp.exp(m_o - m_n)
                        a_b = jnp.exp(st[0] - m_n)
                        st[:] = [m_n, a_o * l_o + a_b * st[1], a_o * o_o + a_b * st[2]]
                for (rows, _, _), (m_b, l_b, acc) in zip(blocks, new):
                    if pi == len(DILATIONS) - 1:
                        o_ref[rows, :] = acc / l_b
                        lse_ref[rows, :] = m_b + jnp.log(l_b)
                    else:
                        o_ref[rows, :] = acc
                        m_s[rows, :] = m_b
                        l_s[rows, :] = l_b
                return carry

            lax.fori_loop(0, ngroups, group, 0)

        @pl.when(hp == steps - 1)
        def _():
            for cp in last:
                cp()

    col = pl.BlockSpec((S, LANES), lambda i: (0, i))
    res = pl.pallas_call(
        body, name="attn_fwd", grid=(steps,),
        in_specs=[pl.BlockSpec(memory_space=pltpu.SMEM), col, col, col] + [ANY_SPEC] * ns,
        out_specs=[col, col] + [ANY_SPEC] * ns,
        out_shape=[jax.ShapeDtypeStruct((S, ATT_WIDTH), F32)] * 2 + _exchange_shapes(shards, (False,) * ns),
        scratch_shapes=[pltpu.VMEM((S, LANES), F32), pltpu.VMEM((S, LANES), F32),
                        pltpu.VMEM((2, 2 * SPAN, 2 * SPAN), F32)] + _exchange_sems(ns),
        compiler_params=_params("arbitrary"),
    )(slopes, q, k, v, *shards)
    return res[0], res[1], res[2:]


def _pool_count(i, w):
    t = i * TM + lax.broadcasted_iota(jnp.int32, (TM, 1), 0)
    return jnp.minimum(t + 1, w).astype(F32)


def _mix_out(x, att, u, pool_w, pool_scale, w_out):
    S = x.shape[0]

    def body(x_ref, att_ref, u_ref, pw_ref, ps_ref, w_ref, h1_ref, mix_ref, dlt_ref, ubuf):
        i = pl.program_id(0)

        @pl.when(i == 0)
        def _():
            ubuf[0:HALO, :] = jnp.zeros((HALO, POOL_WIDTH), F32)

        ubuf[HALO:HALO + TM, :] = u_ref[...]
        mix_ref[:, 0:ATT_WIDTH] = att_ref[...].astype(BF16)
        for g, w in enumerate(POOL_WINDOWS):
            cols = slice(g * POOL_GROUP, (g + 1) * POOL_GROUP)
            ug = ubuf[HALO:HALO + TM, cols]
            acc = ug
            for j in range(1, w):
                acc = acc + ubuf[HALO - j:HALO - j + TM, cols]
            dlt = (acc / _pool_count(i, w) - ug).astype(BF16)
            dlt_ref[:, cols] = dlt
            yg = jnp.dot(dlt, pw_ref[g].astype(BF16), preferred_element_type=F32) * ps_ref[:, cols]
            mix_ref[:, ATT_WIDTH + g * POOL_GROUP:ATT_WIDTH + (g + 1) * POOL_GROUP] = yg.astype(BF16)
        ubuf[0:HALO, :] = ubuf[TM:TM + HALO, :]
        h1_ref[...] = x_ref[...] + jnp.dot(mix_ref[...], w_ref[...], preferred_element_type=F32)

    tok = lambda w: pl.BlockSpec((TM, w), lambda i: (i, 0))
    return pl.pallas_call(
        body, name="mix_out", grid=(S // TM,),
        in_specs=[tok(D_MODEL), tok(ATT_WIDTH), tok(POOL_WIDTH), _const(pool_w.shape), _const((1, POOL_WIDTH)),
                  _const(w_out.shape)],
        out_specs=[tok(D_MODEL), tok(D_MODEL), tok(POOL_WIDTH)],
        out_shape=[jax.ShapeDtypeStruct((S, D_MODEL), F32), jax.ShapeDtypeStruct((S, D_MODEL), BF16),
                   jax.ShapeDtypeStruct((S, POOL_WIDTH), BF16)],
        scratch_shapes=[pltpu.VMEM((TM + HALO, POOL_WIDTH), F32)],
        compiler_params=_params("arbitrary"),
    )(x, att, u, pool_w, pool_scale, w_out)


def _conv_fwd(stage, upre, prev, cw, cb):
    T = upre.shape[0]
    stage[0:HALO, :] = prev
    stage[HALO:HALO + T, :] = upre
    return cb + cw[0:1, :] * stage[HALO - 2:HALO - 2 + T, :] + cw[1:2, :] * stage[HALO - 1:HALO - 1 + T, :] + cw[2:3, :] * upre


def _ffn_fwd(h1, g2, w_up, conv_w, conv_b, w_down):
    S = h1.shape[0]
    T = TM_FF

    def body(h1_ref, g_ref, wu_ref, cw_ref, cb_ref, wd_ref, h2_ref, hn_ref, up_ref, carry, stage):
        i = pl.program_id(0)

        @pl.when(i == 0)
        def _():
            carry[...] = jnp.zeros(carry.shape, F32)

        h1t = h1_ref[...]
        r, n = _rms(h1t)
        hn = (n * g_ref[...]).astype(BF16)
        hn_ref[...] = hn
        acc = h1t
        for j in range(4):
            conv = []
            for jj in (j, j + 4):
                upre = jnp.dot(hn, wu_ref[jj], preferred_element_type=F32)
                up_ref[jj] = upre.astype(BF16)
                conv.append(_conv_fwd(stage, upre, carry[jj], cw_ref[jj], cb_ref[jj]))
                carry[jj] = stage[T:T + HALO, :]
            gate, val = conv
            a = gate * jax.nn.sigmoid(gate) * val
            acc = acc + jnp.dot(a.astype(BF16), wd_ref[j], preferred_element_type=F32)
        h2_ref[...] = acc

    tok = lambda w: pl.BlockSpec((T, w), lambda i: (i, 0))
    return pl.pallas_call(
        body, name="ffn_fwd", grid=(S // T,),
        in_specs=[tok(D_MODEL), _const((1, D_MODEL)), _const(w_up.shape), _const(conv_w.shape), _const(conv_b.shape),
                  _const(w_down.shape)],
        out_specs=[tok(D_MODEL), tok(D_MODEL), pl.BlockSpec((N_DEV, T, FF_SHARD), lambda i: (0, i, 0))],
        out_shape=[jax.ShapeDtypeStruct((S, D_MODEL), F32), jax.ShapeDtypeStruct((S, D_MODEL), BF16),
                   jax.ShapeDtypeStruct((N_DEV, S, FF_SHARD), BF16)],
        scratch_shapes=[pltpu.VMEM((N_DEV, HALO, FF_SHARD), F32), pltpu.VMEM((T + HALO, FF_SHARD), F32)],
        compiler_params=_params("arbitrary"),
    )(h1, g2, w_up, conv_w, conv_b, w_down)


def _head(h2, p, g3, w_pg, w_ple, g4, target):
    S = h2.shape[0]
    nt = S // TM

    def body(h2_ref, p_ref, g3_ref, wpg_ref, wple_ref, g4_ref, t_ref,
             loss_ref, dh2_ref, dh2b_ref, hn3_ref, dgl_ref, dpe_ref, dg3_ref, dg4_ref, lacc):
        i = pl.program_id(0)

        @pl.when(i == 0)
        def _():
            lacc[...] = jnp.zeros(lacc.shape, F32)
            dg3_ref[...] = jnp.zeros(dg3_ref.shape, F32)
            dg4_ref[...] = jnp.zeros(dg4_ref.shape, F32)

        h2t = h2_ref[...]
        g3, g4 = g3_ref[...], g4_ref[...]
        r3, n3 = _rms(h2t)
        hn3 = (n3 * g3).astype(BF16)
        hn3_ref[...] = hn3
        gs = jax.nn.sigmoid(jnp.dot(hn3, wpg_ref[...], preferred_element_type=F32))
        pe = jnp.dot(p_ref[...].astype(BF16), wple_ref[...], preferred_element_type=F32)
        h3 = h2t + gs * pe
        r4, n4 = _rms(h3)
        err = n4 * g4 - t_ref[...]
        lacc[...] += _colsum(err * err)
        dy = err * (1.0 / D_MODEL)
        dg4_ref[...] += _colsum(dy * n4)
        dh3 = _rms_bwd(r4, n4, g4, dy)
        dpe_ref[...] = (dh3 * gs).astype(BF16)
        dgl = (dh3 * pe * gs * (1.0 - gs)).astype(BF16)
        dgl_ref[...] = dgl
        dhn3 = lax.dot_general(dgl, wpg_ref[...], NT, preferred_element_type=F32)
        dg3_ref[...] += _colsum(dhn3 * n3)
        dh2 = dh3 + _rms_bwd(r3, n3, g3, dhn3)
        dh2_ref[...] = dh2
        dh2b_ref[...] = dh2.astype(BF16)

        @pl.when(i == nt - 1)
        def _():
            tot = 0.5 / D_MODEL * jnp.sum(lacc[...], axis=-1, keepdims=True)
            loss_ref[...] = jnp.broadcast_to(tot, loss_ref.shape)

    tok = lambda w: pl.BlockSpec((TM, w), lambda i: (i, 0))
    row = pl.BlockSpec((1, D_MODEL), lambda i: (0, 0))
    act = lambda dt: jax.ShapeDtypeStruct((S, D_MODEL), dt)
    return pl.pallas_call(
        body, name="head", grid=(nt,),
        in_specs=[tok(D_MODEL), tok(PLE_DIM), _const((1, D_MODEL)), _const(w_pg.shape), _const(w_ple.shape),
                  _const((1, D_MODEL)), tok(D_MODEL)],
        out_specs=[pl.BlockSpec((8, LANES), lambda i: (0, 0)), tok(D_MODEL), tok(D_MODEL), tok(D_MODEL), tok(D_MODEL),
                   tok(D_MODEL), row, row],
        out_shape=[jax.ShapeDtypeStruct((8, LANES), F32), act(F32), act(BF16), act(BF16), act(BF16), act(BF16),
                   jax.ShapeDtypeStruct((1, D_MODEL), F32), jax.ShapeDtypeStruct((1, D_MODEL), F32)],
        scratch_shapes=[pltpu.VMEM((1, D_MODEL), F32)],
        compiler_params=_params("arbitrary"),
    )(h2, p, g3, w_pg, w_ple, g4, target)


def _wgrad(name, x, dy, x_kind, dy_kind, nj, k_dim, n_dim):
    S = x.shape[-2]
    nt = S // TK

    def spec(kind, width):
        if kind == "full":
            return pl.BlockSpec((TK, width), lambda j, t: (t, 0))
        return pl.BlockSpec((None, TK, width), lambda j, t: (j, t, 0))

    def body(x_ref, dy_ref, o_ref, acc):
        t = pl.program_id(1)

        @pl.when(t == 0)
        def _():
            acc[...] = jnp.zeros(acc.shape, F32)

        acc[...] += lax.dot_general(x_ref[...].astype(BF16), dy_ref[...], TN, preferred_element_type=F32)

        @pl.when(t == nt - 1)
        def _():
            o_ref[...] = acc[...].astype(BF16)

    return pl.pallas_call(
        body, name=name, grid=(nj, nt),
        in_specs=[spec(x_kind, k_dim), spec(dy_kind, n_dim)],
        out_specs=pl.BlockSpec((None, k_dim, n_dim), lambda j, t: (j, 0, 0)),
        out_shape=jax.ShapeDtypeStruct((nj, k_dim, n_dim), BF16),
        scratch_shapes=[pltpu.VMEM((k_dim, n_dim), F32)],
        compiler_params=_params("arbitrary", "arbitrary"),
    )(x, dy)


def _row_picker(T, off0, off1):
    r = lax.broadcasted_iota(jnp.int32, (2 * T, T + HALO), 0)
    c = lax.broadcasted_iota(jnp.int32, (2 * T, T + HALO), 1)
    want = jnp.where(r < T, r + off0, r - T + off1)
    return jnp.where(c == want, 1.0, 0.0).astype(BF16)


def _ffn_bwd_a(dh2b, up, w_down, conv_w, conv_b):
    S = dh2b.shape[0]
    T = TM_FF
    hb = T // HALO

    def body(dh_ref, up_ref, halo_ref, wd_ref, cw_ref, cb_ref, a_ref, dup_ref, dcw_ref, dcb_ref, stage):
        i = pl.program_id(0)

        @pl.when(i == 0)
        def _():
            dcw_ref[...] = jnp.zeros(dcw_ref.shape, F32)
            dcb_ref[...] = jnp.zeros(dcb_ref.shape, F32)

        dh = dh_ref[...]
        pick = _row_picker(T, HALO - 2, HALO - 1)
        for j in range(4):
            da = lax.dot_general(dh, wd_ref[j], NT, preferred_element_type=F32)
            conv, taps = [], []
            for jj in (j, j + 4):
                upre = up_ref[jj]
                stage[0:HALO, :] = jnp.where(i > 0, halo_ref[jj], jnp.zeros((HALO, FF_SHARD), BF16))
                stage[HALO:HALO + T, :] = upre
                prv = jnp.dot(pick, stage[...], preferred_element_type=F32)
                tp = (prv[0:T], prv[T:2 * T], upre.astype(F32))
                cw = cw_ref[jj]
                conv.append(cb_ref[jj] + cw[0:1, :] * tp[0] + cw[1:2, :] * tp[1] + cw[2:3, :] * tp[2])
                taps.append(tp)
            gate, val = conv
            sg = jax.nn.sigmoid(gate)
            silu = gate * sg
            a_ref[j] = (silu * val).astype(BF16)
            dgate = da * val * (sg * (1.0 + gate * (1.0 - sg)))
            dval = da * silu
            for jj, dup, tp in ((j, dgate, taps[0]), (j + 4, dval, taps[1])):
                dup_ref[jj] = dup.astype(BF16)
                dcb_ref[jj] += _colsum(dup)
                for kk in range(3):
                    dcw_ref[jj, kk:kk + 1, :] += _colsum(dup * tp[kk])

    tok = lambda w: pl.BlockSpec((T, w), lambda i: (i, 0))
    shard = pl.BlockSpec((N_DEV, T, FF_SHARD), lambda i: (0, i, 0))
    return pl.pallas_call(
        body, name="ffn_bwd_a", grid=(S // T,),
        in_specs=[tok(D_MODEL), shard,
                  pl.BlockSpec((N_DEV, HALO, FF_SHARD), lambda i: (0, jnp.maximum(i * hb - 1, 0), 0)),
                  _const(w_down.shape), _const(conv_w.shape), _const(conv_b.shape)],
        out_specs=[pl.BlockSpec((4, T, FF_SHARD), lambda i: (0, i, 0)), shard,
                   pl.BlockSpec((N_DEV, 3, FF_SHARD), lambda i: (0, 0, 0)),
                   pl.BlockSpec((N_DEV, 1, FF_SHARD), lambda i: (0, 0, 0))],
        out_shape=[jax.ShapeDtypeStruct((4, S, FF_SHARD), BF16), jax.ShapeDtypeStruct((N_DEV, S, FF_SHARD), BF16),
                   jax.ShapeDtypeStruct((N_DEV, 3, FF_SHARD), F32), jax.ShapeDtypeStruct((N_DEV, 1, FF_SHARD), F32)],
        scratch_shapes=[pltpu.VMEM((T + HALO, FF_SHARD), BF16)],
        compiler_params=_params("arbitrary"),
    )(dh2b, up, up, w_down, conv_w, conv_b)


def _ffn_bwd_b(dup, conv_w, w_up, h1, g2, dh2, grads):
    S = h1.shape[0]
    T = TM_FF
    hb = T // HALO
    nt = S // T
    ng = len(grads)

    def body(dup_ref, halo_ref, cw_ref, wu_ref, h1_ref, g_ref, dh2_ref, *rest):
        dpre_ref, dh1_ref, dh1b_ref, dg_ref = rest[ng:ng + 4]
        stage = rest[2 * ng + 4]
        i = pl.program_id(0)
        copies = _exchange_copies(rest[:ng], rest[ng + 4:2 * ng + 4], (True,) * ng, *rest[2 * ng + 5:])

        @pl.when(i == 0)
        def _():
            dg_ref[...] = jnp.zeros(dg_ref.shape, F32)
            for cp in copies:
                cp.start()

        dhn = jnp.zeros((T, D_MODEL), F32)
        for jj in range(N_DEV):
            dup = dup_ref[jj].astype(F32)
            stage[0:T, :] = dup
            stage[T:T + HALO, :] = jnp.where(i < nt - 1, halo_ref[jj].astype(F32), 0.0)
            cw = cw_ref[jj]
            dpre = (cw[2:3, :] * dup + cw[1:2, :] * stage[1:1 + T, :] + cw[0:1, :] * stage[2:2 + T, :]).astype(BF16)
            dpre_ref[jj] = dpre
            dhn = dhn + lax.dot_general(dpre, wu_ref[jj], NT, preferred_element_type=F32)
        g = g_ref[...]
        r, n = _rms(h1_ref[...])
        dg_ref[...] += _colsum(dhn * n)
        dh1 = dh2_ref[...] + _rms_bwd(r, n, g, dhn)
        dh1_ref[...] = dh1
        dh1b_ref[...] = dh1.astype(BF16)

        @pl.when(i == nt - 1)
        def _():
            for cp in copies:
                cp.wait()

    tok = lambda w: pl.BlockSpec((T, w), lambda i: (i, 0))
    shard = pl.BlockSpec((N_DEV, T, FF_SHARD), lambda i: (0, i, 0))
    res = pl.pallas_call(
        body, name="ffn_bwd_b", grid=(nt,),
        in_specs=[shard,
                  pl.BlockSpec((N_DEV, HALO, FF_SHARD), lambda i: (0, jnp.minimum((i + 1) * hb, S // HALO - 1), 0)),
                  _const(conv_w.shape), _const(w_up.shape), tok(D_MODEL), _const((1, D_MODEL)), tok(D_MODEL)]
        + [ANY_SPEC] * ng,
        out_specs=[shard, tok(D_MODEL), tok(D_MODEL), pl.BlockSpec((1, D_MODEL), lambda i: (0, 0))] + [ANY_SPEC] * ng,
        out_shape=[jax.ShapeDtypeStruct((N_DEV, S, FF_SHARD), BF16), jax.ShapeDtypeStruct((S, D_MODEL), F32),
                   jax.ShapeDtypeStruct((S, D_MODEL), BF16), jax.ShapeDtypeStruct((1, D_MODEL), F32)]
        + _exchange_shapes(grads, (True,) * ng),
        scratch_shapes=[pltpu.VMEM((T + HALO, FF_SHARD), F32)] + _exchange_sems(ng),
        compiler_params=_params("arbitrary"),
    )(dup, dup, conv_w, w_up, h1, g2, dh2, *grads)
    return res[0], res[1], res[2], res[3], res[4:]


def _mix_bwd(dh1b, w_out, dlt, pool_w, pool_scale):
    S = dh1b.shape[0]
    nt = S // TM

    def body(dh_ref, w_ref, dlt_ref, pw_ref, ps_ref, datt_ref, du_ref, dpw_ref, dps_ref, stage, carry):
        i = pl.program_id(0)
        tile = nt - 1 - i

        @pl.when(i == 0)
        def _():
            dpw_ref[...] = jnp.zeros(dpw_ref.shape, F32)
            dps_ref[...] = jnp.zeros(dps_ref.shape, F32)
            carry[...] = jnp.zeros(carry.shape, F32)

        dmix = lax.dot_general(dh_ref[...], w_ref[...], NT, preferred_element_type=F32)
        datt_ref[...] = dmix[:, 0:ATT_WIDTH]
        for g, w in enumerate(POOL_WINDOWS):
            cols = slice(g * POOL_GROUP, (g + 1) * POOL_GROUP)
            dpool = dmix[:, ATT_WIDTH + g * POOL_GROUP:ATT_WIDTH + (g + 1) * POOL_GROUP]
            dl = dlt_ref[:, cols]
            pw = pw_ref[g].astype(BF16)
            yg = jnp.dot(dl, pw, preferred_element_type=F32)
            dps_ref[:, cols] += _colsum(dpool * yg)
            dy = (dpool * ps_ref[:, cols]).astype(BF16)
            dpw_ref[g] += lax.dot_general(dl, dy, TN, preferred_element_type=F32)
            ddlt = lax.dot_general(dy, pw, NT, preferred_element_type=F32)
            cg = ddlt / _pool_count(tile, w)
            stage[0:TM, :] = cg
            stage[TM:TM + HALO, :] = carry[:, cols]
            acc = cg
            for j in range(1, w):
                acc = acc + stage[j:j + TM, :]
            du_ref[:, cols] = acc - ddlt
            carry[:, cols] = cg[0:HALO, :]

    tok = lambda w: pl.BlockSpec((TM, w), lambda i: (nt - 1 - i, 0))
    return pl.pallas_call(
        body, name="mix_bwd", grid=(nt,),
        in_specs=[tok(D_MODEL), _const(w_out.shape), tok(POOL_WIDTH), _const(pool_w.shape), _const((1, POOL_WIDTH))],
        out_specs=[tok(ATT_WIDTH), tok(POOL_WIDTH), pl.BlockSpec(pool_w.shape, lambda i: (0, 0, 0)),
                   pl.BlockSpec((1, POOL_WIDTH), lambda i: (0, 0))],
        out_shape=[jax.ShapeDtypeStruct((S, ATT_WIDTH), F32), jax.ShapeDtypeStruct((S, POOL_WIDTH), F32),
                   jax.ShapeDtypeStruct(pool_w.shape, F32), jax.ShapeDtypeStruct((1, POOL_WIDTH), F32)],
        scratch_shapes=[pltpu.VMEM((TM + HALO, POOL_GROUP), F32), pltpu.VMEM((HALO, POOL_WIDTH), F32)],
        compiler_params=_params("arbitrary"),
    )(dh1b, w_out, dlt, pool_w, pool_scale)


def _attn_bwd(slopes, q, k, v, o, lse, do, grads):
    S = q.shape[0]
    CH = 512
    ng = len(grads)
    steps = ATT_WIDTH // LANES

    def body(slope_ref, q_ref, k_ref, v_ref, o_ref, lse_ref, do_ref, *rest):
        dq_ref, dk_ref, dv_ref = rest[ng:ng + 3]
        dl_s, bias_s = rest[2 * ng + 3:2 * ng + 5]
        hp = pl.program_id(0)
        copies = _exchange_copies(rest[:ng], rest[ng + 3:2 * ng + 3], (True,) * ng, *rest[2 * ng + 5:])

        @pl.when(hp == 0)
        def _():
            for cp in copies:
                cp.start()

        is0 = lax.broadcasted_iota(jnp.int32, (SPAN, LANES), 1) < HEAD_DIM
        is0c = lax.broadcasted_iota(jnp.int32, (CH, LANES), 1) < HEAD_DIM

        def prep(ci, carry):
            rows = pl.ds(pl.multiple_of(ci * CH, CH), CH)
            prod = do_ref[rows, :] * o_ref[rows, :]
            d0 = jnp.sum(jnp.where(is0c, prod, 0.0), axis=-1, keepdims=True)
            d1 = jnp.sum(jnp.where(is0c, 0.0, prod), axis=-1, keepdims=True)
            dl_s[rows, :] = jnp.where(is0c, d0, d1)
            zero = jnp.zeros((CH, LANES), F32)
            dq_ref[rows, :] = zero
            dk_ref[rows, :] = zero
            dv_ref[rows, :] = zero
            return carry

        lax.fori_loop(0, S // CH, prep, 0)

        for d in DILATIONS:
            nb, ngroups = _attn_groups(S, d)
            _attn_fill_bias(bias_s, slope_ref, hp, d)

            def group(i, carry, d=d, nb=nb):
                blocks = [_attn_block(i, g, d, nb) for g in range(ATTN_GROUP)]
                loaded = [(q_ref[rows, :], do_ref[rows, :], lse_ref[rows, :], dl_s[rows, :], k_ref[krows, :],
                           v_ref[krows, :].astype(BF16)) for rows, krows, _ in blocks]
                new = []
                for (rows, krows, tab), (qb, dob, lse_b, dl_b, kf, vb) in zip(blocks, loaded):
                    kb = kf.astype(BF16)
                    qs = _stack_heads(qb, is0).astype(BF16)
                    dos = _stack_heads(dob, is0).astype(BF16)
                    lse_s = jnp.concatenate([lse_b[:, 0:1], lse_b[:, HEAD_DIM:HEAD_DIM + 1]], axis=0)
                    dl_s2 = jnp.concatenate([dl_b[:, 0:1], dl_b[:, HEAD_DIM:HEAD_DIM + 1]], axis=0)
                    s = lax.dot_general(qs, kb, NT, preferred_element_type=F32) + bias_s[tab]
                    pr = jnp.exp(s - lse_s)
                    dp = lax.dot_general(dos, vb, NT, preferred_element_type=F32)
                    ds = (pr * (dp - dl_s2)).astype(BF16)
                    dv_c = lax.dot_general(pr.astype(BF16), dos, TN, preferred_element_type=F32)
                    dk_c = lax.dot_general(ds, qs, TN, preferred_element_type=F32)
                    dq_c = _unstack_heads(jnp.dot(ds, kb, preferred_element_type=F32), is0)
                    new.append((dq_c, dk_c, dv_c))
                old = [(dq_ref[rows, :], dk_ref[krows, :], dv_ref[krows, :]) for rows, krows, _ in blocks]
                for (rows, krows, _), (dq_c, dk_c, dv_c), (dq_o, dk_o, dv_o) in zip(blocks, new, old):
                    dq_ref[rows, :] = dq_o + dq_c
                    dk_ref[krows, :] = dk_o + dk_c
                    dv_ref[krows, :] = dv_o + dv_c
                return carry

            lax.fori_loop(0, ngroups, group, 0)

        @pl.when(hp == steps - 1)
        def _():
            for cp in copies:
                cp.wait()

    col = pl.BlockSpec((S, LANES), lambda i: (0, i))
    res = pl.pallas_call(
        body, name="attn_bwd", grid=(steps,),
        in_specs=[pl.BlockSpec(memory_space=pltpu.SMEM)] + [col] * 6 + [ANY_SPEC] * ng,
        out_specs=[col] * 3 + [ANY_SPEC] * ng,
        out_shape=[jax.ShapeDtypeStruct((S, ATT_WIDTH), F32)] * 3 + _exchange_shapes(grads, (True,) * ng),
        scratch_shapes=[pltpu.VMEM((S, LANES), F32), pltpu.VMEM((2, 2 * SPAN, 2 * SPAN), F32)] + _exchange_sems(ng),
        compiler_params=_params("arbitrary"),
    )(slopes, q, k, v, o, lse, do, *grads)
    return res[0], res[1], res[2], res[3:]


def _in_bwd(dq, dk, dv, du, w_in, x, g1, dh1):
    S = x.shape[0]

    def body(dq_ref, dk_ref, dv_ref, du_ref, w_ref, x_ref, g_ref, dh1_ref, dz_ref, dx_ref, dg_ref):
        @pl.when(pl.program_id(0) == 0)
        def _():
            dg_ref[...] = jnp.zeros(dg_ref.shape, F32)

        srcs = (dq_ref, dk_ref, dv_ref, du_ref)
        dhn = jnp.zeros((TM, D_MODEL), F32)
        for j in range(N_DEV):
            dz = srcs[j // 2][:, (j % 2) * 256:(j % 2 + 1) * 256]
            if j < 2:
                dz = dz * (HEAD_DIM ** -0.5)
            dz = dz.astype(BF16)
            dz_ref[j] = dz
            dhn = dhn + lax.dot_general(dz, w_ref[j], NT, preferred_element_type=F32)
        g = g_ref[...]
        r, n = _rms(x_ref[...])
        dg_ref[...] += _colsum(dhn * n)
        dx_ref[...] = dh1_ref[...] + _rms_bwd(r, n, g, dhn)

    tok = lambda w: pl.BlockSpec((TM, w), lambda i: (i, 0))
    return pl.pallas_call(
        body, name="in_bwd", grid=(S // TM,),
        in_specs=[tok(512)] * 4 + [_const(w_in.shape), tok(D_MODEL), _const((1, D_MODEL)), tok(D_MODEL)],
        out_specs=[pl.BlockSpec((N_DEV, TM, 256), lambda i: (0, i, 0)), tok(D_MODEL),
                   pl.BlockSpec((1, D_MODEL), lambda i: (0, 0))],
        out_shape=[jax.ShapeDtypeStruct((N_DEV, S, 256), BF16), jax.ShapeDtypeStruct((S, D_MODEL), F32),
                   jax.ShapeDtypeStruct((1, D_MODEL), F32)],
        compiler_params=_params("arbitrary"),
    )(dq, dk, dv, du, w_in, x, g1, dh1)


def _adamw(name, parts, w, m, v):
    R, C = w.shape
    rb = R
    for cand in (256, 128, 64, 32, 16, 8):
        if R % cand == 0 and R > cand:
            rb = cand
            break

    def body(p_ref, w_ref, m_ref, v_ref, g_ref, d_ref, mo_ref, vo_ref):
        g = p_ref[0].astype(F32)
        for s in range(1, N_DEV):
            g = g + p_ref[s].astype(F32)
        m_new = ADAM_B1 * m_ref[...] + (1.0 - ADAM_B1) * g
        v_new = ADAM_B2 * v_ref[...] + (1.0 - ADAM_B2) * (g * g)
        m_hat = m_new / (1.0 - ADAM_B1 ** ADAM_STEP)
        v_hat = v_new / (1.0 - ADAM_B2 ** ADAM_STEP)
        g_ref[...] = g
        d_ref[...] = -ADAM_LR * (m_hat / (jnp.sqrt(v_hat) + ADAM_EPS) + ADAM_WD * w_ref[...])
        mo_ref[...] = m_new
        vo_ref[...] = v_new

    blk = pl.BlockSpec((rb, C), lambda i: (i, 0))
    return pl.pallas_call(
        body, name=name, grid=(R // rb,),
        in_specs=[pl.BlockSpec((N_DEV, rb, C), lambda i: (0, i, 0)), blk, blk, blk],
        out_specs=[blk] * 4,
        out_shape=[jax.ShapeDtypeStruct((R, C), F32)] * 4,
        compiler_params=_params("arbitrary"),
    )(parts, w, m, v)


def _rows(a):
    flat = a.reshape(-1)
    rows = -(-flat.shape[0] // LANES)
    rows8 = -(-rows // 8) * 8
    flat = jnp.pad(flat, (0, rows8 * LANES - flat.shape[0]))
    return flat.reshape(rows8, LANES)


def kernel(x, p, ln_mix, w_in, pool_w, pool_scale, w_out, ln_ffn, w_up, conv_w, conv_b, w_down, ln_ple, w_ple_gate, w_ple, ln_final, loss_target, m_ln_mix, m_w_in, m_pool_w, m_pool_scale, m_w_out, m_ln_ffn, m_w_up, m_conv_w, m_conv_b, m_w_down, m_ln_ple, m_w_ple_gate, m_w_ple, m_ln_final, v_ln_mix, v_w_in, v_pool_w, v_pool_scale, v_w_out, v_ln_ffn, v_w_up, v_conv_w, v_conv_b, v_w_down, v_ln_ple, v_w_ple_gate, v_w_ple, v_ln_final):
    xs, ps, tgt, pool_w0 = x[0], p[0, 0], loss_target[0], pool_w[0]
    slopes = jnp.exp2(-8.0 * (jnp.arange(N_HEADS, dtype=F32) + 1.0) / N_HEADS)
    conv_b_s = conv_b.reshape(N_DEV, 1, FF_SHARD)

    (w_in_g,) = _exchange("gather_w_in", [w_in[0].astype(BF16)], (False,))
    q, k, v, u, hn1 = _qkvu(xs, ln_mix, w_in_g)
    att, lse, (w_out_g, w_up_g, w_down_g, w_pg_g, w_ple_g, conv_w_g) = _attn_fwd(
        slopes, q, k, v,
        [w_out[0].astype(BF16), w_up[0].astype(BF16), w_down[0].astype(BF16), w_ple_gate[0].astype(BF16),
         w_ple[0].astype(BF16), conv_w[0]])
    w_out_f = w_out_g.reshape(D_MODEL, D_MODEL)
    w_down_f = w_down_g.reshape(4, FF_SHARD, D_MODEL)
    w_pg_f = w_pg_g.reshape(D_MODEL, D_MODEL)
    w_ple_f = jnp.transpose(w_ple_g, (1, 0, 2)).reshape(PLE_DIM, D_MODEL)
    h1, mix, dlt = _mix_out(xs, att, u, pool_w0, pool_scale, w_out_f)
    h2, hn2, up = _ffn_fwd(h1, ln_ffn, w_up_g, conv_w_g, conv_b_s, w_down_f)
    loss_blk, dh2, dh2b, hn3, dgl, dpe, d_ln_ple, d_ln_final = _head(
        h2, ps, ln_ple, w_pg_f, w_ple_f, ln_final.reshape(1, D_MODEL), tgt)

    a, dup, d_conv_w, d_conv_b = _ffn_bwd_a(dh2b, up, w_down_f, conv_w_g, conv_b_s)
    d_w_down = _wgrad("dw_down", a, dh2b, "lead", "full", 4, FF_SHARD, D_MODEL).reshape(N_DEV, D_FF // N_DEV, D_MODEL)
    d_w_pg = _wgrad("dw_ple_gate", hn3, dgl, "full", "full", 1, D_MODEL, D_MODEL).reshape(N_DEV, D_MODEL // N_DEV, D_MODEL)
    d_w_ple = _wgrad("dw_ple", ps, dpe, "full", "full", 1, PLE_DIM, D_MODEL)
    d_w_ple = jnp.transpose(d_w_ple.reshape(PLE_DIM, N_DEV, LANES), (1, 0, 2))
    dpre, dh1, dh1b, d_ln_ffn, (r_conv_w, r_w_down, r_w_pg, r_w_ple) = _ffn_bwd_b(
        dup, conv_w_g, w_up_g, h1, ln_ffn, dh2, [d_conv_w, d_w_down, d_w_pg, d_w_ple])
    datt, du, d_pool_w, d_pool_scale = _mix_bwd(dh1b, w_out_f, dlt, pool_w0, pool_scale)
    d_w_out = _wgrad("dw_out", mix, dh1b, "full", "full", 1, D_MODEL, D_MODEL).reshape(N_DEV, D_MODEL // N_DEV, D_MODEL)
    d_w_up = _wgrad("dw_up", hn2, dpre, "full", "lead", N_DEV, D_MODEL, FF_SHARD)
    dq, dk, dv, (r_w_out, r_w_up) = _attn_bwd(slopes, q, k, v, att, lse, datt, [d_w_out, d_w_up])
    dz, grad_x, d_ln_mix = _in_bwd(dq, dk, dv, du, w_in_g, xs, ln_mix, dh1)
    d_w_in = _wgrad("dw_in", hn1, dz, "full", "lead", N_DEV, D_MODEL, 256)
    rep_g = dict(ln_mix=d_ln_mix, pool_w=d_pool_w, pool_scale=d_pool_scale, ln_ffn=d_ln_ffn, conv_b=d_conv_b,
                 ln_ple=d_ln_ple, ln_final=d_ln_final)

    rep_names = ("ln_mix", "pool_w", "pool_scale", "ln_ffn", "conv_b", "ln_ple", "ln_final")
    rep_w = dict(ln_mix=ln_mix, pool_w=pool_w, pool_scale=pool_scale, ln_ffn=ln_ffn, conv_b=conv_b, ln_ple=ln_ple,
                 ln_final=ln_final)
    rep_m = dict(ln_mix=m_ln_mix, pool_w=m_pool_w, pool_scale=m_pool_scale, ln_ffn=m_ln_ffn, conv_b=m_conv_b,
                 ln_ple=m_ln_ple, ln_final=m_ln_final)
    rep_v = dict(ln_mix=v_ln_mix, pool_w=v_pool_w, pool_scale=v_pool_scale, ln_ffn=v_ln_ffn, conv_b=v_conv_b,
                 ln_ple=v_ln_ple, ln_final=v_ln_final)
    pack = lambda d: jnp.concatenate([_rows(d[n]) for n in rep_names], axis=0)
    g_pack = jnp.concatenate([pack(rep_g), loss_blk], axis=0)
    zero_blk = jnp.zeros((8, LANES), F32)
    w_pack = jnp.concatenate([pack(rep_w), zero_blk], axis=0)
    m_pack = jnp.concatenate([pack(rep_m), zero_blk], axis=0)
    v_pack = jnp.concatenate([pack(rep_v), zero_blk + 1.0], axis=0)

    r_w_in, r_pack = _exchange("exchange_tail", [d_w_in, g_pack], (True, False))

    sharded = {}
    sharded["w_in"] = _adamw("adamw_w_in", r_w_in, w_in[0], m_w_in[0], v_w_in[0])
    sharded["w_out"] = _adamw("adamw_w_out", r_w_out, w_out[0], m_w_out[0], v_w_out[0])
    sharded["w_up"] = _adamw("adamw_w_up", r_w_up, w_up[0], m_w_up[0], v_w_up[0])
    sharded["conv_w"] = _adamw("adamw_conv_w", r_conv_w, conv_w[0], m_conv_w[0], v_conv_w[0])
    sharded["w_down"] = _adamw("adamw_w_down", r_w_down, w_down[0], m_w_down[0], v_w_down[0])
    sharded["w_ple_gate"] = _adamw("adamw_w_ple_gate", r_w_pg, w_ple_gate[0], m_w_ple_gate[0], v_w_ple_gate[0])
    sharded["w_ple"] = _adamw("adamw_w_ple", r_w_ple, w_ple[0], m_w_ple[0], v_w_ple[0])
    packed = _adamw("adamw_replicated", r_pack, w_pack, m_pack, v_pack)

    loss = packed[0][-8, 0]
    offs, o = {}, 0
    for n in rep_names:
        offs[n] = o
        o += _rows(rep_w[n]).shape[0]

    weights = dict(w_in=w_in, w_out=w_out, w_up=w_up, conv_w=conv_w, w_down=w_down, w_ple_gate=w_ple_gate, w_ple=w_ple,
                   **rep_w)

    def leaf(kind, n):
        shape = weights[n].shape
        if n in sharded:
            return sharded[n][kind].reshape(shape)
        size = 1
        for s in shape:
            size *= s
        rows = -(-size // LANES)
        return packed[kind][offs[n]:offs[n] + rows].reshape(-1)[:size].reshape(shape)

    order = ("ln_mix", "w_in", "pool_w", "pool_scale", "w_out", "ln_ffn", "w_up", "conv_w", "conv_b", "w_down", "ln_ple",
             "w_ple_gate", "w_ple", "ln_final")
    outs = [loss, grad_x[None]]
    for kind in range(4):
        outs += [leaf(kind, n) for n in order]
    return tuple(outs)
```

```python
import jax
import jax.numpy as jnp
from jax import lax
from jax.experimental import pallas as pl
from jax.experimental.pallas import tpu as pltpu

F32 = jnp.float32
BF16 = jnp.bfloat16

N_DEV = 8
D_MODEL = 1024
ATT_WIDTH = 512
POOL_WIDTH = 512
N_HEADS = 8
HEAD_DIM = 64
SPAN = 128
DILATIONS = (1, 4, 16)
POOL_WINDOWS = (2, 4, 8, 16)
POOL_GROUP = 128
D_FF = 2816
FF_SHARD = 2 * D_FF // N_DEV
PLE_DIM = 256
EPS = 1e-6
NEG = -1e30

ADAM_LR = 0.001
ADAM_B1 = 0.9
ADAM_B2 = 0.999
ADAM_EPS = 1e-08
ADAM_WD = 0.01
ADAM_STEP = 10

LANES = 128
HALO = 16
TM = 512
TM_FF = 256
TK = 1024
ATTN_GROUP = 4
VMEM_LIMIT = 56 * 1024 * 1024

MESH = pl.DeviceIdType.MESH
NT = (((1,), (1,)), ((), ()))
TN = (((0,), (0,)), ((), ()))


def _params(*sem):
    return pltpu.CompilerParams(dimension_semantics=sem or None, vmem_limit_bytes=VMEM_LIMIT)


def _const(shape):
    n = len(shape)
    return pl.BlockSpec(shape, lambda *_: (0,) * n, pipeline_mode=pl.Buffered(1))


def _rms(h):
    r = lax.rsqrt(jnp.mean(h * h, axis=-1, keepdims=True) + EPS)
    return r, h * r


def _rms_bwd(r, n, g, dhn):
    dn = dhn * g
    return r * (dn - n * jnp.mean(dn * n, axis=-1, keepdims=True))


def _colsum(a):
    return jnp.sum(a, axis=0, keepdims=True)


def _gather2_copies(ins, outs, send_sems, recv_sems, local_sems):
    n = len(ins)
    x, y, c = lax.axis_index("x"), lax.axis_index("y"), lax.axis_index("c")
    slot = lambda px, py, pc: 4 * px + 2 * py + pc
    chips = [(x, 1 - y), (1 - x, y), (1 - x, 1 - y)]
    first, passed, last = [], [], []

    def remote(a, r, src, dst_slot, to):
        return pltpu.make_async_remote_copy(
            src_ref=src, dst_ref=outs[a].at[dst_slot],
            send_sem=send_sems.at[a * (N_DEV - 1) + r], recv_sem=recv_sems.at[a * (N_DEV - 1) + r],
            device_id=to, device_id_type=MESH)

    for a in range(n):
        mine = pltpu.make_async_copy(ins[a], outs[a].at[slot(x, y, c)], local_sems.at[a])
        to_sibling = remote(a, 0, ins[a], slot(x, y, c), (x, y, 1 - c))
        first += [mine, to_sibling]
        last += [mine.wait, to_sibling.wait_send, to_sibling.wait_recv]
        for r, (px, py) in enumerate(chips, start=1):
            to_chip = remote(a, r, ins[a], slot(x, y, c), (px, py, c))
            onward = remote(a, 3 + r, outs[a].at[slot(px, py, c)], slot(px, py, c), (x, y, 1 - c))
            first.append(to_chip)
            passed.append((to_chip, onward))
            last += [to_chip.wait_send, onward.wait_send, onward.wait_recv]
    return first, passed, last


ANY_SPEC = pl.BlockSpec(memory_space=pl.ANY)


def _exchange_shapes(arrays, scatter):
    out = []
    for a, s in zip(arrays, scatter):
        slab = a.shape[1:] if s else a.shape
        out.append(jax.ShapeDtypeStruct((N_DEV,) + tuple(slab), a.dtype))
    return out


def _exchange_sems(n):
    return [pltpu.SemaphoreType.DMA((n * (N_DEV - 1),)), pltpu.SemaphoreType.DMA((n * (N_DEV - 1),)),
            pltpu.SemaphoreType.DMA((n,))]


def _exchange_copies(ins, outs, scatter, send_sems, recv_sems, local_sems):
    n = len(ins)
    x, y, c = lax.axis_index("x"), lax.axis_index("y"), lax.axis_index("c")
    me = 4 * x + 2 * y + c
    copies = []
    for a in range(n):
        src = ins[a].at[me] if scatter[a] else ins[a]
        copies.append(pltpu.make_async_copy(src, outs[a].at[me], local_sems.at[a]))
    for k in range(1, N_DEV):
        px = 1 - x if k & 4 else x
        py = 1 - y if k & 2 else y
        pc = 1 - c if k & 1 else c
        pid = 4 * px + 2 * py + pc
        for a in range(n):
            src = ins[a].at[pid] if scatter[a] else ins[a]
            copies.append(pltpu.make_async_remote_copy(
                src_ref=src, dst_ref=outs[a].at[me],
                send_sem=send_sems.at[a * (N_DEV - 1) + k - 1], recv_sem=recv_sems.at[a * (N_DEV - 1) + k - 1],
                device_id=(px, py, pc), device_id_type=MESH))
    return copies


def _qkvu(x, g1, w_in):
    S = x.shape[0]

    def body(x_ref, g_ref, w_ref, q_ref, k_ref, v_ref, u_ref, hn_ref):
        r, n = _rms(x_ref[...])
        hn = (n * g_ref[...]).astype(BF16)
        hn_ref[...] = hn
        outs = (q_ref, k_ref, v_ref, u_ref)
        for j in range(N_DEV):
            z = jnp.dot(hn, w_ref[j], preferred_element_type=F32)
            if j < 2:
                z = z * (HEAD_DIM ** -0.5)
            outs[j // 2][:, (j % 2) * 256:(j % 2 + 1) * 256] = z

    tok = lambda w: pl.BlockSpec((TM, w), lambda i: (i, 0))
    return pl.pallas_call(
        body, name="qkvu", grid=(S // TM,),
        in_specs=[tok(D_MODEL), _const((1, D_MODEL)), _const(w_in.shape)],
        out_specs=[tok(512)] * 4 + [tok(D_MODEL)],
        out_shape=[jax.ShapeDtypeStruct((S, 512), F32)] * 4 + [jax.ShapeDtypeStruct((S, D_MODEL), BF16)],
        compiler_params=_params("arbitrary"),
    )(x, g1, w_in)


def _attn_fill_bias(bias_s, slope_ref, hp, d):
    qi = lax.broadcasted_iota(jnp.int32, (SPAN, 2 * SPAN), 0)
    kj = lax.broadcasted_iota(jnp.int32, (SPAN, 2 * SPAN), 1)
    for t, diff in enumerate((qi + SPAN - kj, qi - kj)):
        valid = (diff >= 0) & (diff <= SPAN)
        dist = diff.astype(F32) * float(d)
        for h in range(2):
            bias_s[t, h * SPAN:(h + 1) * SPAN, :] = jnp.where(valid, -slope_ref[2 * hp + h] * dist, NEG)


def _stack_heads(x, is0):
    return jnp.concatenate([jnp.where(is0, x, 0.0), jnp.where(is0, 0.0, x)], axis=0)


def _unstack_heads(y, is0):
    return jnp.where(is0, y[0:SPAN], y[SPAN:2 * SPAN])


def _attn_block(i, g, d, nb):
    if d >= ATTN_GROUP:
        per = d // ATTN_GROUP
        r = (i & (per - 1)) * ATTN_GROUP + g
        n = i >> (per.bit_length() - 1)
    else:
        r = 0
        n = i + g * (nb // ATTN_GROUP)
    k0 = jnp.maximum(n - 1, 0)

    def ds(block, nrows):
        start = block * (SPAN * d) + r
        return pl.ds(start, nrows, stride=d) if d > 1 else pl.ds(start, nrows)

    return ds(n, SPAN), ds(k0, 2 * SPAN), jnp.where(n == 0, 1, 0)


def _attn_groups(S, d):
    nb = S // d // SPAN
    assert nb >= 2 and (d * nb) % ATTN_GROUP == 0 and (d >= ATTN_GROUP or nb % ATTN_GROUP == 0)
    return nb, d * nb // ATTN_GROUP


def _attn_fwd(slopes, q, k, v, shards):
    S = q.shape[0]
    ns = len(shards)
    steps = ATT_WIDTH // LANES

    def body(slope_ref, q_ref, k_ref, v_ref, *rest):
        o_ref, lse_ref = rest[ns:ns + 2]
        m_s, l_s, bias_s = rest[2 * ns + 2:2 * ns + 5]
        hp = pl.program_id(0)
        first, passed, last = _gather2_copies(rest[:ns], rest[ns + 2:2 * ns + 2], *rest[2 * ns + 5:])

        @pl.when(hp == 0)
        def _():
            for cp in first:
                cp.start()

        @pl.when(hp == steps - 1)
        def _():
            for arrival, cp in passed:
                arrival.wait_recv()
                cp.start()

        is0 = lax.broadcasted_iota(jnp.int32, (SPAN, LANES), 1) < HEAD_DIM
        for pi, d in enumerate(DILATIONS):
            nb, ngroups = _attn_groups(S, d)
            _attn_fill_bias(bias_s, slope_ref, hp, d)

            def group(i, carry, d=d, pi=pi, nb=nb):
                blocks = [_attn_block(i, g, d, nb) for g in range(ATTN_GROUP)]
                loaded = [(q_ref[rows, :], k_ref[krows, :].astype(BF16), v_ref[krows, :].astype(BF16))
                          for rows, krows, _ in blocks]
                new = []
                for (rows, krows, tab), (qb, kb, vb) in zip(blocks, loaded):
                    qs = _stack_heads(qb, is0).astype(BF16)
                    s = lax.dot_general(qs, kb, NT, preferred_element_type=F32) + bias_s[tab]
                    m = jnp.max(s, axis=-1, keepdims=True)
                    e = jnp.exp(s - m)
                    l = jnp.sum(e, axis=-1, keepdims=True)
                    pv = jnp.dot(e.astype(BF16), vb, preferred_element_type=F32)
                    new.append([_unstack_heads(jnp.broadcast_to(m, pv.shape), is0),
                                _unstack_heads(jnp.broadcast_to(l, pv.shape), is0), _unstack_heads(pv, is0)])
                if pi > 0:
                    old = [(m_s[rows, :], l_s[rows, :], o_ref[rows, :]) for rows, _, _ in blocks]
                    for st, (m_o, l_o, o_o) in zip(new, old):
                        m_n = jnp.maximum(m_o, st[0])
                        a_o = jnp.exp(m_o - m_n)
                        a_b = jnp.exp(st[0] - m_n)
                        st[:] = [m_n, a_o * l_o + a_b * st[1], a_o * o_o + a_b * st[2]]
                for (rows, _, _), (m_b, l_b, acc) in zip(blocks, new):
                    if pi == len(DILATIONS) - 1:
                        o_ref[rows, :] = acc / l_b
                        lse_ref[rows, :] = m_b + jnp.log(l_b)
                    else:
                        o_ref[rows, :] = acc
                        m_s[rows, :] = m_b
                        l_s[rows, :] = l_b
                return carry

            lax.fori_loop(0, ngroups, group, 0)

        @pl.when(hp == steps - 1)
        def _():
            for cp in last:
                cp()

    col = pl.BlockSpec((S, LANES), lambda i: (0, i))
    res = pl.pallas_call(
        body, name="attn_fwd", grid=(steps,),
        in_specs=[pl.BlockSpec(memory_space=pltpu.SMEM), col, col, col] + [ANY_SPEC] * ns,
        out_specs=[col, col] + [ANY_SPEC] * ns,
        out_shape=[jax.ShapeDtypeStruct((S, ATT_WIDTH), F32)] * 2 + _exchange_shapes(shards, (False,) * ns),
        scratch_shapes=[pltpu.VMEM((S, LANES), F32), pltpu.VMEM((S, LANES), F32),
                        pltpu.VMEM((2, 2 * SPAN, 2 * SPAN), F32)] + _exchange_sems(ns),
        compiler_params=_params("arbitrary"),
    )(slopes, q, k, v, *shards)
    return res[0], res[1], res[2:]


def _pool_count(i, w):
    t = i * TM + lax.broadcasted_iota(jnp.int32, (TM, 1), 0)
    return jnp.minimum(t + 1, w).astype(F32)


def _mix_out(x, att, u, pool_w, pool_scale, w_out):
    S = x.shape[0]

    def body(x_ref, att_ref, u_ref, pw_ref, ps_ref, w_ref, h1_ref, mix_ref, dlt_ref, ubuf):
        i = pl.program_id(0)

        @pl.when(i == 0)
        def _():
            ubuf[0:HALO, :] = jnp.zeros((HALO, POOL_WIDTH), F32)

        ubuf[HALO:HALO + TM, :] = u_ref[...]
        mix_ref[:, 0:ATT_WIDTH] = att_ref[...].astype(BF16)
        for g, w in enumerate(POOL_WINDOWS):
            cols = slice(g * POOL_GROUP, (g + 1) * POOL_GROUP)
            ug = ubuf[HALO:HALO + TM, cols]
            acc = ug
            for j in range(1, w):
                acc = acc + ubuf[HALO - j:HALO - j + TM, cols]
            dlt = (acc / _pool_count(i, w) - ug).astype(BF16)
            dlt_ref[:, cols] = dlt
            yg = jnp.dot(dlt, pw_ref[g].astype(BF16), preferred_element_type=F32) * ps_ref[:, cols]
            mix_ref[:, ATT_WIDTH + g * POOL_GROUP:ATT_WIDTH + (g + 1) * POOL_GROUP] = yg.astype(BF16)
        ubuf[0:HALO, :] = ubuf[TM:TM + HALO, :]
        h1_ref[...] = x_ref[...] + jnp.dot(mix_ref[...], w_ref[...], preferred_element_type=F32)

    tok = lambda w: pl.BlockSpec((TM, w), lambda i: (i, 0))
    return pl.pallas_call(
        body, name="mix_out", grid=(S // TM,),
        in_specs=[tok(D_MODEL), tok(ATT_WIDTH), tok(POOL_WIDTH), _const(pool_w.shape), _const((1, POOL_WIDTH)),
                  _const(w_out.shape)],
        out_specs=[tok(D_MODEL), tok(D_MODEL), tok(POOL_WIDTH)],
        out_shape=[jax.ShapeDtypeStruct((S, D_MODEL), F32), jax.ShapeDtypeStruct((S, D_MODEL), BF16),
                   jax.ShapeDtypeStruct((S, POOL_WIDTH), BF16)],
        scratch_shapes=[pltpu.VMEM((TM + HALO, POOL_WIDTH), F32)],
        compiler_params=_params("arbitrary"),
    )(x, att, u, pool_w, pool_scale, w_out)


def _conv_fwd(stage, upre, prev, cw, cb):
    T = upre.shape[0]
    stage[0:HALO, :] = prev
    stage[HALO:HALO + T, :] = upre
    return cb + cw[0:1, :] * stage[HALO - 2:HALO - 2 + T, :] + cw[1:2, :] * stage[HALO - 1:HALO - 1 + T, :] + cw[2:3, :] * upre


def _ffn_fwd(h1, g2, w_up, conv_w, conv_b, w_down):
    S = h1.shape[0]
    T = TM_FF

    def body(h1_ref, g_ref, wu_ref, cw_ref, cb_ref, wd_ref, h2_ref, hn_ref, up_ref, carry, stage):
        i = pl.program_id(0)

        @pl.when(i == 0)
        def _():
            carry[...] = jnp.zeros(carry.shape, F32)

        h1t = h1_ref[...]
        r, n = _rms(h1t)
        hn = (n * g_ref[...]).astype(BF16)
        hn_ref[...] = hn
        acc = h1t
        for j in range(4):
            conv = []
            for jj in (j, j + 4):
                upre = jnp.dot(hn, wu_ref[jj], preferred_element_type=F32)
                up_ref[jj] = upre.astype(BF16)
                conv.append(_conv_fwd(stage, upre, carry[jj], cw_ref[jj], cb_ref[jj]))
                carry[jj] = stage[T:T + HALO, :]
            gate, val = conv
            a = gate * jax.nn.sigmoid(gate) * val
            acc = acc + jnp.dot(a.astype(BF16), wd_ref[j], preferred_element_type=F32)
        h2_ref[...] = acc

    tok = lambda w: pl.BlockSpec((T, w), lambda i: (i, 0))
    return pl.pallas_call(
        body, name="ffn_fwd", grid=(S // T,),
        in_specs=[tok(D_MODEL), _const((1, D_MODEL)), _const(w_up.shape), _const(conv_w.shape), _const(conv_b.shape),
                  _const(w_down.shape)],
        out_specs=[tok(D_MODEL), tok(D_MODEL), pl.BlockSpec((N_DEV, T, FF_SHARD), lambda i: (0, i, 0))],
        out_shape=[jax.ShapeDtypeStruct((S, D_MODEL), F32), jax.ShapeDtypeStruct((S, D_MODEL), BF16),
                   jax.ShapeDtypeStruct((N_DEV, S, FF_SHARD), BF16)],
        scratch_shapes=[pltpu.VMEM((N_DEV, HALO, FF_SHARD), F32), pltpu.VMEM((T + HALO, FF_SHARD), F32)],
        compiler_params=_params("arbitrary"),
    )(h1, g2, w_up, conv_w, conv_b, w_down)


def _head(h2, p, g3, w_pg, w_ple, g4, target):
    S = h2.shape[0]
    nt = S // TM

    def body(h2_ref, p_ref, g3_ref, wpg_ref, wple_ref, g4_ref, t_ref,
             loss_ref, dh2_ref, dh2b_ref, hn3_ref, dgl_ref, dpe_ref, dg3_ref, dg4_ref, lacc):
        i = pl.program_id(0)

        @pl.when(i == 0)
        def _():
            lacc[...] = jnp.zeros(lacc.shape, F32)
            dg3_ref[...] = jnp.zeros(dg3_ref.shape, F32)
            dg4_ref[...] = jnp.zeros(dg4_ref.shape, F32)

        h2t = h2_ref[...]
        g3, g4 = g3_ref[...], g4_ref[...]
        r3, n3 = _rms(h2t)
        hn3 = (n3 * g3).astype(BF16)
        hn3_ref[...] = hn3
        gs = jax.nn.sigmoid(jnp.dot(hn3, wpg_ref[...], preferred_element_type=F32))
        pe = jnp.dot(p_ref[...].astype(BF16), wple_ref[...], preferred_element_type=F32)
        h3 = h2t + gs * pe
        r4, n4 = _rms(h3)
        err = n4 * g4 - t_ref[...]
        lacc[...] += _colsum(err * err)
        dy = err * (1.0 / D_MODEL)
        dg4_ref[...] += _colsum(dy * n4)
        dh3 = _rms_bwd(r4, n4, g4, dy)
        dpe_ref[...] = (dh3 * gs).astype(BF16)
        dgl = (dh3 * pe * gs * (1.0 - gs)).astype(BF16)
        dgl_ref[...] = dgl
        dhn3 = lax.dot_general(dgl, wpg_ref[...], NT, preferred_element_type=F32)
        dg3_ref[...] += _colsum(dhn3 * n3)
        dh2 = dh3 + _rms_bwd(r3, n3, g3, dhn3)
        dh2_ref[...] = dh2
        dh2b_ref[...] = dh2.astype(BF16)

        @pl.when(i == nt - 1)
        def _():
            tot = 0.5 / D_MODEL * jnp.sum(lacc[...], axis=-1, keepdims=True)
            loss_ref[...] = jnp.broadcast_to(tot, loss_ref.shape)

    tok = lambda w: pl.BlockSpec((TM, w), lambda i: (i, 0))
    row = pl.BlockSpec((1, D_MODEL), lambda i: (0, 0))
    act = lambda dt: jax.ShapeDtypeStruct((S, D_MODEL), dt)
    return pl.pallas_call(
        body, name="head", grid=(nt,),
        in_specs=[tok(D_MODEL), tok(PLE_DIM), _const((1, D_MODEL)), _const(w_pg.shape), _const(w_ple.shape),
                  _const((1, D_MODEL)), tok(D_MODEL)],
        out_specs=[pl.BlockSpec((8, LANES), lambda i: (0, 0)), tok(D_MODEL), tok(D_MODEL), tok(D_MODEL), tok(D_MODEL),
                   tok(D_MODEL), row, row],
        out_shape=[jax.ShapeDtypeStruct((8, LANES), F32), act(F32), act(BF16), act(BF16), act(BF16), act(BF16),
                   jax.ShapeDtypeStruct((1, D_MODEL), F32), jax.ShapeDtypeStruct((1, D_MODEL), F32)],
        scratch_shapes=[pltpu.VMEM((1, D_MODEL), F32)],
        compiler_params=_params("arbitrary"),
    )(h2, p, g3, w_pg, w_ple, g4, target)


def _wgrad(name, x, dy, x_kind, dy_kind, nj, k_dim, n_dim):
    S = x.shape[-2]
    nt = S // TK

    def spec(kind, width):
        if kind == "full":
            return pl.BlockSpec((TK, width), lambda j, t: (t, 0))
        return pl.BlockSpec((None, TK, width), lambda j, t: (j, t, 0))

    def body(x_ref, dy_ref, o_ref, acc):
        t = pl.program_id(1)

        @pl.when(t == 0)
        def _():
            acc[...] = jnp.zeros(acc.shape, F32)

        acc[...] += lax.dot_general(x_ref[...].astype(BF16), dy_ref[...], TN, preferred_element_type=F32)

        @pl.when(t == nt - 1)
        def _():
            o_ref[...] = acc[...].astype(BF16)

    return pl.pallas_call(
        body, name=name, grid=(nj, nt),
        in_specs=[spec(x_kind, k_dim), spec(dy_kind, n_dim)],
        out_specs=pl.BlockSpec((None, k_dim, n_dim), lambda j, t: (j, 0, 0)),
        out_shape=jax.ShapeDtypeStruct((nj, k_dim, n_dim), BF16),
        scratch_shapes=[pltpu.VMEM((k_dim, n_dim), F32)],
        compiler_params=_params("arbitrary", "arbitrary"),
    )(x, dy)


def _row_picker(T, off0, off1):
    r = lax.broadcasted_iota(jnp.int32, (2 * T, T + HALO), 0)
    c = lax.broadcasted_iota(jnp.int32, (2 * T, T + HALO), 1)
    want = jnp.where(r < T, r + off0, r - T + off1)
    return jnp.where(c == want, 1.0, 0.0).astype(BF16)


def _ffn_bwd_a(dh2b, up, w_down, conv_w, conv_b):
    S = dh2b.shape[0]
    T = TM_FF
    hb = T // HALO

    def body(dh_ref, up_ref, halo_ref, wd_ref, cw_ref, cb_ref, a_ref, dup_ref, dcw_ref, dcb_ref, stage):
        i = pl.program_id(0)

        @pl.when(i == 0)
        def _():
            dcw_ref[...] = jnp.zeros(dcw_ref.shape, F32)
            dcb_ref[...] = jnp.zeros(dcb_ref.shape, F32)

        dh = dh_ref[...]
        pick = _row_picker(T, HALO - 2, HALO - 1)
        for j in range(4):
            da = lax.dot_general(dh, wd_ref[j], NT, preferred_element_type=F32)
            conv, taps = [], []
            for jj in (j, j + 4):
                upre = up_ref[jj]
                stage[0:HALO, :] = jnp.where(i > 0, halo_ref[jj], jnp.zeros((HALO, FF_SHARD), BF16))
                stage[HALO:HALO + T, :] = upre
                prv = jnp.dot(pick, stage[...], preferred_element_type=F32)
                tp = (prv[0:T], prv[T:2 * T], upre.astype(F32))
                cw = cw_ref[jj]
                conv.append(cb_ref[jj] + cw[0:1, :] * tp[0] + cw[1:2, :] * tp[1] + cw[2:3, :] * tp[2])
                taps.append(tp)
            gate, val = conv
            sg = jax.nn.sigmoid(gate)
            silu = gate * sg
            a_ref[j] = (silu * val).astype(BF16)
            dgate = da * val * (sg * (1.0 + gate * (1.0 - sg)))
            dval = da * silu
            for jj, dup, tp in ((j, dgate, taps[0]), (j + 4, dval, taps[1])):
                dup_ref[jj] = dup.astype(BF16)
                dcb_ref[jj] += _colsum(dup)
                for kk in range(3):
                    dcw_ref[jj, kk:kk + 1, :] += _colsum(dup * tp[kk])

    tok = lambda w: pl.BlockSpec((T, w), lambda i: (i, 0))
    shard = pl.BlockSpec((N_DEV, T, FF_SHARD), lambda i: (0, i, 0))
    return pl.pallas_call(
        body, name="ffn_bwd_a", grid=(S // T,),
        in_specs=[tok(D_MODEL), shard,
                  pl.BlockSpec((N_DEV, HALO, FF_SHARD), lambda i: (0, jnp.maximum(i * hb - 1, 0), 0)),
                  _const(w_down.shape), _const(conv_w.shape), _const(conv_b.shape)],
        out_specs=[pl.BlockSpec((4, T, FF_SHARD), lambda i: (0, i, 0)), shard,
                   pl.BlockSpec((N_DEV, 3, FF_SHARD), lambda i: (0, 0, 0)),
                   pl.BlockSpec((N_DEV, 1, FF_SHARD), lambda i: (0, 0, 0))],
        out_shape=[jax.ShapeDtypeStruct((4, S, FF_SHARD), BF16), jax.ShapeDtypeStruct((N_DEV, S, FF_SHARD), BF16),
                   jax.ShapeDtypeStruct((N_DEV, 3, FF_SHARD), F32), jax.ShapeDtypeStruct((N_DEV, 1, FF_SHARD), F32)],
        scratch_shapes=[pltpu.VMEM((T + HALO, FF_SHARD), BF16)],
        compiler_params=_params("arbitrary"),
    )(dh2b, up, up, w_down, conv_w, conv_b)


def _ffn_bwd_b(dup, conv_w, w_up, h1, g2, dh2, grads):
    S = h1.shape[0]
    T = TM_FF
    hb = T // HALO
    nt = S // T
    ng = len(grads)

    def body(dup_ref, halo_ref, cw_ref, wu_ref, h1_ref, g_ref, dh2_ref, *rest):
        dpre_ref, dh1_ref, dh1b_ref, dg_ref = rest[ng:ng + 4]
        stage = rest[2 * ng + 4]
        i = pl.program_id(0)
        copies = _exchange_copies(rest[:ng], rest[ng + 4:2 * ng + 4], (True,) * ng, *rest[2 * ng + 5:])

        @pl.when(i == 0)
        def _():
            dg_ref[...] = jnp.zeros(dg_ref.shape, F32)
            for cp in copies:
                cp.start()

        dhn = jnp.zeros((T, D_MODEL), F32)
        for jj in range(N_DEV):
            dup = dup_ref[jj].astype(F32)
            stage[0:T, :] = dup
            stage[T:T + HALO, :] = jnp.where(i < nt - 1, halo_ref[jj].astype(F32), 0.0)
            cw = cw_ref[jj]
            dpre = (cw[2:3, :] * dup + cw[1:2, :] * stage[1:1 + T, :] + cw[0:1, :] * stage[2:2 + T, :]).astype(BF16)
            dpre_ref[jj] = dpre
            dhn = dhn + lax.dot_general(dpre, wu_ref[jj], NT, preferred_element_type=F32)
        g = g_ref[...]
        r, n = _rms(h1_ref[...])
        dg_ref[...] += _colsum(dhn * n)
        dh1 = dh2_ref[...] + _rms_bwd(r, n, g, dhn)
        dh1_ref[...] = dh1
        dh1b_ref[...] = dh1.astype(BF16)

        @pl.when(i == nt - 1)
        def _():
            for cp in copies:
                cp.wait()

    tok = lambda w: pl.BlockSpec((T, w), lambda i: (i, 0))
    shard = pl.BlockSpec((N_DEV, T, FF_SHARD), lambda i: (0, i, 0))
    res = pl.pallas_call(
        body, name="ffn_bwd_b", grid=(nt,),
        in_specs=[shard,
                  pl.BlockSpec((N_DEV, HALO, FF_SHARD), lambda i: (0, jnp.minimum((i + 1) * hb, S // HALO - 1), 0)),
                  _const(conv_w.shape), _const(w_up.shape), tok(D_MODEL), _const((1, D_MODEL)), tok(D_MODEL)]
        + [ANY_SPEC] * ng,
        out_specs=[shard, tok(D_MODEL), tok(D_MODEL), pl.BlockSpec((1, D_MODEL), lambda i: (0, 0))] + [ANY_SPEC] * ng,
        out_shape=[jax.ShapeDtypeStruct((N_DEV, S, FF_SHARD), BF16), jax.ShapeDtypeStruct((S, D_MODEL), F32),
                   jax.ShapeDtypeStruct((S, D_MODEL), BF16), jax.ShapeDtypeStruct((1, D_MODEL), F32)]
        + _exchange_shapes(grads, (True,) * ng),
        scratch_shapes=[pltpu.VMEM((T + HALO, FF_SHARD), F32)] + _exchange_sems(ng),
        compiler_params=_params("arbitrary"),
    )(dup, dup, conv_w, w_up, h1, g2, dh2, *grads)
    return res[0], res[1], res[2], res[3], res[4:]


def _mix_bwd(dh1b, w_out, dlt, pool_w, pool_scale):
    S = dh1b.shape[0]
    nt = S // TM

    def body(dh_ref, w_ref, dlt_ref, pw_ref, ps_ref, datt_ref, du_ref, dpw_ref, dps_ref, stage, carry):
        i = pl.program_id(0)
        tile = nt - 1 - i

        @pl.when(i == 0)
        def _():
            dpw_ref[...] = jnp.zeros(dpw_ref.shape, F32)
            dps_ref[...] = jnp.zeros(dps_ref.shape, F32)
            carry[...] = jnp.zeros(carry.shape, F32)

        dmix = lax.dot_general(dh_ref[...], w_ref[...], NT, preferred_element_type=F32)
        datt_ref[...] = dmix[:, 0:ATT_WIDTH]
        for g, w in enumerate(POOL_WINDOWS):
            cols = slice(g * POOL_GROUP, (g + 1) * POOL_GROUP)
            dpool = dmix[:, ATT_WIDTH + g * POOL_GROUP:ATT_WIDTH + (g + 1) * POOL_GROUP]
            dl = dlt_ref[:, cols]
            pw = pw_ref[g].astype(BF16)
            yg = jnp.dot(dl, pw, preferred_element_type=F32)
            dps_ref[:, cols] += _colsum(dpool * yg)
            dy = (dpool * ps_ref[:, cols]).astype(BF16)
            dpw_ref[g] += lax.dot_general(dl, dy, TN, preferred_element_type=F32)
            ddlt = lax.dot_general(dy, pw, NT, preferred_element_type=F32)
            cg = ddlt / _pool_count(tile, w)
            stage[0:TM, :] = cg
            stage[TM:TM + HALO, :] = carry[:, cols]
            acc = cg
            for j in range(1, w):
                acc = acc + stage[j:j + TM, :]
            du_ref[:, cols] = acc - ddlt
            carry[:, cols] = cg[0:HALO, :]

    tok = lambda w: pl.BlockSpec((TM, w), lambda i: (nt - 1 - i, 0))
    return pl.pallas_call(
        body, name="mix_bwd", grid=(nt,),
        in_specs=[tok(D_MODEL), _const(w_out.shape), tok(POOL_WIDTH), _const(pool_w.shape), _const((1, POOL_WIDTH))],
        out_specs=[tok(ATT_WIDTH), tok(POOL_WIDTH), pl.BlockSpec(pool_w.shape, lambda i: (0, 0, 0)),
                   pl.BlockSpec((1, POOL_WIDTH), lambda i: (0, 0))],
        out_shape=[jax.ShapeDtypeStruct((S, ATT_WIDTH), F32), jax.ShapeDtypeStruct((S, POOL_WIDTH), F32),
                   jax.ShapeDtypeStruct(pool_w.shape, F32), jax.ShapeDtypeStruct((1, POOL_WIDTH), F32)],
        scratch_shapes=[pltpu.VMEM((TM + HALO, POOL_GROUP), F32), pltpu.VMEM((HALO, POOL_WIDTH), F32)],
        compiler_params=_params("arbitrary"),
    )(dh1b, w_out, dlt, pool_w, pool_scale)


def _attn_bwd(slopes, q, k, v, o, lse, do, grads):
    S = q.shape[0]
    CH = 512
    ng = len(grads)
    steps = ATT_WIDTH // LANES

    def body(slope_ref, q_ref, k_ref, v_ref, o_ref, lse_ref, do_ref, *rest):
        dq_ref, dk_ref, dv_ref = rest[ng:ng + 3]
        dl_s, bias_s = rest[2 * ng + 3:2 * ng + 5]
        hp = pl.program_id(0)
        copies = _exchange_copies(rest[:ng], rest[ng + 3:2 * ng + 3], (True,) * ng, *rest[2 * ng + 5:])

        @pl.when(hp == 0)
        def _():
            for cp in copies:
                cp.start()

        is0 = lax.broadcasted_iota(jnp.int32, (SPAN, LANES), 1) < HEAD_DIM
        is0c = lax.broadcasted_iota(jnp.int32, (CH, LANES), 1) < HEAD_DIM

        def prep(ci, carry):
            rows = pl.ds(pl.multiple_of(ci * CH, CH), CH)
            prod = do_ref[rows, :] * o_ref[rows, :]
            d0 = jnp.sum(jnp.where(is0c, prod, 0.0), axis=-1, keepdims=True)
            d1 = jnp.sum(jnp.where(is0c, 0.0, prod), axis=-1, keepdims=True)
            dl_s[rows, :] = jnp.where(is0c, d0, d1)
            zero = jnp.zeros((CH, LANES), F32)
            dq_ref[rows, :] = zero
            dk_ref[rows, :] = zero
            dv_ref[rows, :] = zero
            return carry

        lax.fori_loop(0, S // CH, prep, 0)

        for d in DILATIONS:
            nb, ngroups = _attn_groups(S, d)
            _attn_fill_bias(bias_s, slope_ref, hp, d)

            def group(i, carry, d=d, nb=nb):
                blocks = [_attn_block(i, g, d, nb) for g in range(ATTN_GROUP)]
                loaded = [(q_ref[rows, :], do_ref[rows, :], lse_ref[rows, :], dl_s[rows, :], k_ref[krows, :],
                           v_ref[krows, :].astype(BF16)) for rows, krows, _ in blocks]
                new = []
                for (rows, krows, tab), (qb, dob, lse_b, dl_b, kf, vb) in zip(blocks, loaded):
                    kb = kf.astype(BF16)
                    qs = _stack_heads(qb, is0).astype(BF16)
                    dos = _stack_heads(dob, is0).astype(BF16)
                    lse_s = jnp.concatenate([lse_b[:, 0:1], lse_b[:, HEAD_DIM:HEAD_DIM + 1]], axis=0)
                    dl_s2 = jnp.concatenate([dl_b[:, 0:1], dl_b[:, HEAD_DIM:HEAD_DIM + 1]], axis=0)
                    s = lax.dot_general(qs, kb, NT, preferred_element_type=F32) + bias_s[tab]
                    pr = jnp.exp(s - lse_s)
                    dp = lax.dot_general(dos, vb, NT, preferred_element_type=F32)
                    ds = (pr * (dp - dl_s2)).astype(BF16)
                    dv_c = lax.dot_general(pr.astype(BF16), dos, TN, preferred_element_type=F32)
                    dk_c = lax.dot_general(ds, qs, TN, preferred_element_type=F32)
                    dq_c = _unstack_heads(jnp.dot(ds, kb, preferred_element_type=F32), is0)
                    new.append((dq_c, dk_c, dv_c))
                old = [(dq_ref[rows, :], dk_ref[krows, :], dv_ref[krows, :]) for rows, krows, _ in blocks]
                for (rows, krows, _), (dq_c, dk_c, dv_c), (dq_o, dk_o, dv_o) in zip(blocks, new, old):
                    dq_ref[rows, :] = dq_o + dq_c
                    dk_ref[krows, :] = dk_o + dk_c
                    dv_ref[krows, :] = dv_o + dv_c
                return carry

            lax.fori_loop(0, ngroups, group, 0)

        @pl.when(hp == steps - 1)
        def _():
            for cp in copies:
                cp.wait()

    col = pl.BlockSpec((S, LANES), lambda i: (0, i))
    res = pl.pallas_call(
        body, name="attn_bwd", grid=(steps,),
        in_specs=[pl.BlockSpec(memory_space=pltpu.SMEM)] + [col] * 6 + [ANY_SPEC] * ng,
        out_specs=[col] * 3 + [ANY_SPEC] * ng,
        out_shape=[jax.ShapeDtypeStruct((S, ATT_WIDTH), F32)] * 3 + _exchange_shapes(grads, (True,) * ng),
        scratch_shapes=[pltpu.VMEM((S, LANES), F32), pltpu.VMEM((2, 2 * SPAN, 2 * SPAN), F32)] + _exchange_sems(ng),
        compiler_params=_params("arbitrary"),
    )(slopes, q, k, v, o, lse, do, *grads)
    return res[0], res[1], res[2], res[3:]


def _in_bwd(dq, dk, dv, du, w_in, x, g1, dh1):
    S = x.shape[0]

    def body(dq_ref, dk_ref, dv_ref, du_ref, w_ref, x_ref, g_ref, dh1_ref, dz_ref, dx_ref, dg_ref):
        @pl.when(pl.program_id(0) == 0)
        def _():
            dg_ref[...] = jnp.zeros(dg_ref.shape, F32)

        srcs = (dq_ref, dk_ref, dv_ref, du_ref)
        dhn = jnp.zeros((TM, D_MODEL), F32)
        for j in range(N_DEV):
            dz = srcs[j // 2][:, (j % 2) * 256:(j % 2 + 1) * 256]
            if j < 2:
                dz = dz * (HEAD_DIM ** -0.5)
            dz = dz.astype(BF16)
            dz_ref[j] = dz
            dhn = dhn + lax.dot_general(dz, w_ref[j], NT, preferred_element_type=F32)
        g = g_ref[...]
        r, n = _rms(x_ref[...])
        dg_ref[...] += _colsum(dhn * n)
        dx_ref[...] = dh1_ref[...] + _rms_bwd(r, n, g, dhn)

    tok = lambda w: pl.BlockSpec((TM, w), lambda i: (i, 0))
    return pl.pallas_call(
        body, name="in_bwd", grid=(S // TM,),
        in_specs=[tok(512)] * 4 + [_const(w_in.shape), tok(D_MODEL), _const((1, D_MODEL)), tok(D_MODEL)],
        out_specs=[pl.BlockSpec((N_DEV, TM, 256), lambda i: (0, i, 0)), tok(D_MODEL),
                   pl.BlockSpec((1, D_MODEL), lambda i: (0, 0))],
        out_shape=[jax.ShapeDtypeStruct((N_DEV, S, 256), BF16), jax.ShapeDtypeStruct((S, D_MODEL), F32),
                   jax.ShapeDtypeStruct((1, D_MODEL), F32)],
        compiler_params=_params("arbitrary"),
    )(dq, dk, dv, du, w_in, x, g1, dh1)


def _adamw(name, parts, w, m, v):
    R, C = w.shape
    rb = R
    for cand in (256, 128, 64, 32, 16, 8):
        if R % cand == 0 and R > cand:
            rb = cand
            break

    def body(p_ref, w_ref, m_ref, v_ref, g_ref, d_ref, mo_ref, vo_ref):
        g = p_ref[0].astype(F32)
        for s in range(1, N_DEV):
            g = g + p_ref[s].astype(F32)
        g_ref[...] = g
        d_ref[...], mo_ref[...], vo_ref[...] = _adam_update(g, w_ref[...], m_ref[...], v_ref[...])

    blk = pl.BlockSpec((rb, C), lambda i: (i, 0))
    return pl.pallas_call(
        body, name=name, grid=(R // rb,),
        in_specs=[pl.BlockSpec((N_DEV, rb, C), lambda i: (0, i, 0)), blk, blk, blk],
        out_specs=[blk] * 4,
        out_shape=[jax.ShapeDtypeStruct((R, C), F32)] * 4,
        compiler_params=_params("arbitrary"),
    )(parts, w, m, v)


def _adam_update(g, w, m, v):
    m_new = ADAM_B1 * m + (1.0 - ADAM_B1) * g
    v_new = ADAM_B2 * v + (1.0 - ADAM_B2) * (g * g)
    m_hat = m_new / (1.0 - ADAM_B1 ** ADAM_STEP)
    v_hat = v_new / (1.0 - ADAM_B2 ** ADAM_STEP)
    return -ADAM_LR * (m_hat / (jnp.sqrt(v_hat) + ADAM_EPS) + ADAM_WD * w), m_new, v_new


def _adamw_small(parts, loss_parts, ws, ms, vs):
    n = len(ws)

    def body(*refs):
        p_refs, lp_ref = refs[:n], refs[n]
        w_refs, m_refs, v_refs = refs[n + 1:2 * n + 1], refs[2 * n + 1:3 * n + 1], refs[3 * n + 1:4 * n + 1]
        outs = refs[4 * n + 1:]
        for i in range(n):
            g = p_refs[i][0]
            for s in range(1, N_DEV):
                g = g + p_refs[i][s]
            d, m_new, v_new = _adam_update(g, w_refs[i][...], m_refs[i][...], v_refs[i][...])
            outs[i][...] = g
            outs[n + i][...] = d
            outs[2 * n + i][...] = m_new
            outs[3 * n + i][...] = v_new
        tot = lp_ref[0]
        for s in range(1, N_DEV):
            tot = tot + lp_ref[s]
        outs[4 * n][...] = tot

    shapes = [jax.ShapeDtypeStruct(w.shape, F32) for w in ws]
    res = pl.pallas_call(
        body, name="adamw_replicated",
        out_shape=shapes * 4 + [jax.ShapeDtypeStruct(loss_parts.shape[1:], F32)],
        compiler_params=_params(),
    )(*parts, loss_parts, *ws, *ms, *vs)
    return res[:n], res[n:2 * n], res[2 * n:3 * n], res[3 * n:4 * n], res[4 * n]


def _gather2(name, arrays):
    n = len(arrays)

    def body(*refs):
        first, passed, last = _gather2_copies(refs[:n], refs[n:2 * n], *refs[2 * n:])
        for cp in first:
            cp.start()
        for arrival, cp in passed:
            arrival.wait_recv()
            cp.start()
        for wait in last:
            wait()

    return pl.pallas_call(
        body, name=name,
        in_specs=[ANY_SPEC] * n, out_specs=[ANY_SPEC] * n, out_shape=_exchange_shapes(arrays, (False,) * n),
        scratch_shapes=_exchange_sems(n),
    )(*arrays)


def _dw_in_exchange(hn, dz, small):
    S = hn.shape[0]
    nt = S // TK
    ns = len(small)
    kd, nd = hn.shape[1], dz.shape[2]
    me_arr = (4 * lax.axis_index("x") + 2 * lax.axis_index("y") + lax.axis_index("c")).astype(jnp.int32).reshape(1)

    def body(me_ref, x_ref, dy_ref, *rest):
        recv_ref = rest[ns]
        acc, stage, send_sems, recv_sems, own_sem = rest[2 * ns + 1:2 * ns + 6]
        j, t = pl.program_id(0), pl.program_id(1)
        x, y, c = lax.axis_index("x"), lax.axis_index("y"), lax.axis_index("c")
        me = 4 * x + 2 * y + c
        small_copies = _exchange_copies(rest[:ns], rest[ns + 1:2 * ns + 1], (False,) * ns, *rest[2 * ns + 6:])

        @pl.when((j == 0) & (t == 0))
        def _():
            for cp in small_copies:
                cp.start()

        @pl.when(t == 0)
        def _():
            acc[...] = jnp.zeros(acc.shape, F32)

        acc[...] += lax.dot_general(x_ref[...], dy_ref[...], TN, preferred_element_type=F32)

        def to_owner(k, owner):
            return pltpu.make_async_remote_copy(
                src_ref=stage.at[owner], dst_ref=recv_ref.at[me], send_sem=send_sems.at[k], recv_sem=recv_sems.at[k],
                device_id=(owner // 4, (owner // 2) % 2, owner % 2), device_id_type=MESH)

        own = pltpu.make_async_copy(stage.at[me], recv_ref.at[me], own_sem)

        @pl.when(t == nt - 1)
        def _():
            owner = (me + 1 + j) % N_DEV
            stage[owner] = acc[...].astype(BF16)

            @pl.when(j < N_DEV - 1)
            def _():
                to_owner(j, owner).start()

            @pl.when(j == N_DEV - 1)
            def _():
                own.start()
                own.wait()
                for k in range(N_DEV - 1):
                    to_owner(k, me).wait_send()
                    to_owner(k, me).wait_recv()
                for cp in small_copies:
                    cp.wait()

    slab = lambda j, me_ref: (me_ref[0] + 1 + j) % N_DEV
    grid_spec = pltpu.PrefetchScalarGridSpec(
        num_scalar_prefetch=1, grid=(N_DEV, nt),
        in_specs=[pl.BlockSpec((TK, kd), lambda j, t, me_ref: (t, 0)),
                  pl.BlockSpec((None, TK, nd), lambda j, t, me_ref: (slab(j, me_ref), t, 0))] + [ANY_SPEC] * ns,
        out_specs=[ANY_SPEC] * (ns + 1),
        scratch_shapes=[pltpu.VMEM((kd, nd), F32), pltpu.VMEM((N_DEV, kd, nd), BF16),
                        pltpu.SemaphoreType.DMA((N_DEV - 1,)), pltpu.SemaphoreType.DMA((N_DEV - 1,)),
                        pltpu.SemaphoreType.DMA] + _exchange_sems(ns))
    res = pl.pallas_call(
        body, name="dw_in_exchange", grid_spec=grid_spec,
        out_shape=[jax.ShapeDtypeStruct((N_DEV, kd, nd), BF16)] + _exchange_shapes(small, (False,) * ns),
        compiler_params=_params("arbitrary", "arbitrary"),
    )(me_arr, hn, dz, *small)
    return res[0], res[1:]


def kernel(x, p, ln_mix, w_in, pool_w, pool_scale, w_out, ln_ffn, w_up, conv_w, conv_b, w_down, ln_ple, w_ple_gate, w_ple, ln_final, loss_target, m_ln_mix, m_w_in, m_pool_w, m_pool_scale, m_w_out, m_ln_ffn, m_w_up, m_conv_w, m_conv_b, m_w_down, m_ln_ple, m_w_ple_gate, m_w_ple, m_ln_final, v_ln_mix, v_w_in, v_pool_w, v_pool_scale, v_w_out, v_ln_ffn, v_w_up, v_conv_w, v_conv_b, v_w_down, v_ln_ple, v_w_ple_gate, v_w_ple, v_ln_final):
    xs, ps, tgt, pool_w0 = x[0], p[0, 0], loss_target[0], pool_w[0]
    slopes = jnp.exp2(-8.0 * (jnp.arange(N_HEADS, dtype=F32) + 1.0) / N_HEADS)
    conv_b_s = conv_b.reshape(N_DEV, 1, FF_SHARD)

    (w_in_g,) = _gather2("gather_w_in", [w_in[0].astype(BF16)])
    q, k, v, u, hn1 = _qkvu(xs, ln_mix, w_in_g)
    att, lse, (w_out_g, w_up_g, w_down_g, w_pg_g, w_ple_g, conv_w_g) = _attn_fwd(
        slopes, q, k, v,
        [w_out[0].astype(BF16), w_up[0].astype(BF16), w_down[0].astype(BF16), w_ple_gate[0].astype(BF16),
         w_ple[0].astype(BF16), conv_w[0]])
    w_out_f = w_out_g.reshape(D_MODEL, D_MODEL)
    w_down_f = w_down_g.reshape(4, FF_SHARD, D_MODEL)
    w_pg_f = w_pg_g.reshape(D_MODEL, D_MODEL)
    w_ple_f = jnp.transpose(w_ple_g, (1, 0, 2)).reshape(PLE_DIM, D_MODEL)
    h1, mix, dlt = _mix_out(xs, att, u, pool_w0, pool_scale, w_out_f)
    h2, hn2, up = _ffn_fwd(h1, ln_ffn, w_up_g, conv_w_g, conv_b_s, w_down_f)
    loss_blk, dh2, dh2b, hn3, dgl, dpe, d_ln_ple, d_ln_final = _head(
        h2, ps, ln_ple, w_pg_f, w_ple_f, ln_final.reshape(1, D_MODEL), tgt)

    a, dup, d_conv_w, d_conv_b = _ffn_bwd_a(dh2b, up, w_down_f, conv_w_g, conv_b_s)
    d_w_down = _wgrad("dw_down", a, dh2b, "lead", "full", 4, FF_SHARD, D_MODEL).reshape(N_DEV, D_FF // N_DEV, D_MODEL)
    d_w_pg = _wgrad("dw_ple_gate", hn3, dgl, "full", "full", 1, D_MODEL, D_MODEL).reshape(N_DEV, D_MODEL // N_DEV, D_MODEL)
    d_w_ple = _wgrad("dw_ple", ps, dpe, "full", "full", 1, PLE_DIM, D_MODEL)
    d_w_ple = jnp.transpose(d_w_ple.reshape(PLE_DIM, N_DEV, LANES), (1, 0, 2))
    dpre, dh1, dh1b, d_ln_ffn, (r_conv_w, r_w_down, r_w_pg, r_w_ple) = _ffn_bwd_b(
        dup, conv_w_g, w_up_g, h1, ln_ffn, dh2, [d_conv_w, d_w_down, d_w_pg, d_w_ple])
    datt, du, d_pool_w, d_pool_scale = _mix_bwd(dh1b, w_out_f, dlt, pool_w0, pool_scale)
    d_w_out = _wgrad("dw_out", mix, dh1b, "full", "full", 1, D_MODEL, D_MODEL).reshape(N_DEV, D_MODEL // N_DEV, D_MODEL)
    d_w_up = _wgrad("dw_up", hn2, dpre, "full", "lead", N_DEV, D_MODEL, FF_SHARD)
    dq, dk, dv, (r_w_out, r_w_up) = _attn_bwd(slopes, q, k, v, att, lse, datt, [d_w_out, d_w_up])
    dz, grad_x, d_ln_mix = _in_bwd(dq, dk, dv, du, w_in_g, xs, ln_mix, dh1)
    rep_names = ("ln_mix", "pool_w", "pool_scale", "ln_ffn", "conv_b", "ln_ple", "ln_final")
    rep_g = [d_ln_mix, d_pool_w, d_pool_scale, d_ln_ffn, d_conv_b.reshape(1, 2 * D_FF), d_ln_ple, d_ln_final]
    rep_w = [ln_mix, pool_w0, pool_scale, ln_ffn, conv_b, ln_ple, ln_final.reshape(1, D_MODEL)]
    rep_m = [m_ln_mix, m_pool_w[0], m_pool_scale, m_ln_ffn, m_conv_b, m_ln_ple, m_ln_final.reshape(1, D_MODEL)]
    rep_v = [v_ln_mix, v_pool_w[0], v_pool_scale, v_ln_ffn, v_conv_b, v_ln_ple, v_ln_final.reshape(1, D_MODEL)]
    r_w_in, r_small = _dw_in_exchange(hn1, dz, rep_g + [loss_blk])
    small = _adamw_small(r_small[:-1], r_small[-1], rep_w, rep_m, rep_v)
    loss = small[4][0, 0]

    sharded = {}
    sharded["w_in"] = _adamw("adamw_w_in", r_w_in, w_in[0], m_w_in[0], v_w_in[0])
    sharded["w_out"] = _adamw("adamw_w_out", r_w_out, w_out[0], m_w_out[0], v_w_out[0])
    sharded["w_up"] = _adamw("adamw_w_up", r_w_up, w_up[0], m_w_up[0], v_w_up[0])
    sharded["conv_w"] = _adamw("adamw_conv_w", r_conv_w, conv_w[0], m_conv_w[0], v_conv_w[0])
    sharded["w_down"] = _adamw("adamw_w_down", r_w_down, w_down[0], m_w_down[0], v_w_down[0])
    sharded["w_ple_gate"] = _adamw("adamw_w_ple_gate", r_w_pg, w_ple_gate[0], m_w_ple_gate[0], v_w_ple_gate[0])
    sharded["w_ple"] = _adamw("adamw_w_ple", r_w_ple, w_ple[0], m_w_ple[0], v_w_ple[0])

    shapes = dict(w_in=w_in, w_out=w_out, w_up=w_up, conv_w=conv_w, w_down=w_down, w_ple_gate=w_ple_gate, w_ple=w_ple,
                  ln_mix=ln_mix, pool_w=pool_w, pool_scale=pool_scale, ln_ffn=ln_ffn, conv_b=conv_b, ln_ple=ln_ple,
                  ln_final=ln_final)

    def leaf(kind, n):
        src = sharded[n][kind] if n in sharded else small[kind][rep_names.index(n)]
        return src.reshape(shapes[n].shape)

    order = ("ln_mix", "w_in", "pool_w", "pool_scale", "w_out", "ln_ffn", "w_up", "conv_w", "conv_b", "w_down", "ln_ple",
             "w_ple_gate", "w_ple", "ln_final")
    outs = [loss, grad_x[None]]
    for kind in range(4):
        outs += [leaf(kind, n) for n in order]
    return tuple(outs)
```

```python
import jax
import jax.numpy as jnp
from jax import lax
from jax.experimental import pallas as pl
from jax.experimental.pallas import tpu as pltpu

F32 = jnp.float32
BF16 = jnp.bfloat16

N_DEV = 8
D_MODEL = 1024
ATT_WIDTH = 512
POOL_WIDTH = 512
N_HEADS = 8
HEAD_DIM = 64
SPAN = 128
DILATIONS = (1, 4, 16)
POOL_WINDOWS = (2, 4, 8, 16)
POOL_GROUP = 128
D_FF = 2816
FF_SHARD = 2 * D_FF // N_DEV
PLE_DIM = 256
EPS = 1e-6
NEG = -1e30

ADAM_LR = 0.001
ADAM_B1 = 0.9
ADAM_B2 = 0.999
ADAM_EPS = 1e-08
ADAM_WD = 0.01
ADAM_STEP = 10

LANES = 128
HALO = 16
TM = 512
TM_FF = 256
TK = 1024
ATTN_GROUP = 8
VMEM_LIMIT = 56 * 1024 * 1024

MESH = pl.DeviceIdType.MESH
NT = (((1,), (1,)), ((), ()))
TN = (((0,), (0,)), ((), ()))


def _params(*sem):
    return pltpu.CompilerParams(dimension_semantics=sem or None, vmem_limit_bytes=VMEM_LIMIT)


def _const(shape):
    n = len(shape)
    return pl.BlockSpec(shape, lambda *_: (0,) * n, pipeline_mode=pl.Buffered(1))


def _rms(h):
    r = lax.rsqrt(jnp.mean(h * h, axis=-1, keepdims=True) + EPS)
    return r, h * r


def _rms_bwd(r, n, g, dhn):
    dn = dhn * g
    return r * (dn - n * jnp.mean(dn * n, axis=-1, keepdims=True))


def _colsum(a):
    return jnp.sum(a, axis=0, keepdims=True)


def _gather2_copies(ins, outs, send_sems, recv_sems, local_sems):
    n = len(ins)
    x, y, c = lax.axis_index("x"), lax.axis_index("y"), lax.axis_index("c")
    slot = lambda px, py, pc: 4 * px + 2 * py + pc
    chips = [(x, 1 - y), (1 - x, y), (1 - x, 1 - y)]
    first, passed, last = [], [], []

    def remote(a, r, src, dst_slot, to):
        return pltpu.make_async_remote_copy(
            src_ref=src, dst_ref=outs[a].at[dst_slot],
            send_sem=send_sems.at[a * (N_DEV - 1) + r], recv_sem=recv_sems.at[a * (N_DEV - 1) + r],
            device_id=to, device_id_type=MESH)

    for a in range(n):
        mine = pltpu.make_async_copy(ins[a], outs[a].at[slot(x, y, c)], local_sems.at[a])
        to_sibling = remote(a, 0, ins[a], slot(x, y, c), (x, y, 1 - c))
        first += [mine, to_sibling]
        last += [mine.wait, to_sibling.wait_send, to_sibling.wait_recv]
        for r, (px, py) in enumerate(chips, start=1):
            to_chip = remote(a, r, ins[a], slot(x, y, c), (px, py, c))
            onward = remote(a, 3 + r, outs[a].at[slot(px, py, c)], slot(px, py, c), (x, y, 1 - c))
            first.append(to_chip)
            passed.append((to_chip, onward))
            last += [to_chip.wait_send, onward.wait_send, onward.wait_recv]
    return first, passed, last


def _gather2_begin(plan, step, pass_step):
    first, passed, _ = plan

    @pl.when(step == 0)
    def _():
        for cp in first:
            cp.start()

    @pl.when(step == pass_step)
    def _():
        for arrival, cp in passed:
            arrival.wait_recv()
            cp.start()


def _gather2_end(plan, step, nsteps):
    @pl.when(step == nsteps - 1)
    def _():
        for wait in plan[2]:
            wait()


ANY_SPEC = pl.BlockSpec(memory_space=pl.ANY)


def _exchange_shapes(arrays, scatter):
    out = []
    for a, s in zip(arrays, scatter):
        slab = a.shape[1:] if s else a.shape
        out.append(jax.ShapeDtypeStruct((N_DEV,) + tuple(slab), a.dtype))
    return out


def _exchange_sems(n):
    return [pltpu.SemaphoreType.DMA((n * (N_DEV - 1),)), pltpu.SemaphoreType.DMA((n * (N_DEV - 1),)),
            pltpu.SemaphoreType.DMA((n,))]


def _exchange_copies(ins, outs, scatter, send_sems, recv_sems, local_sems):
    n = len(ins)
    x, y, c = lax.axis_index("x"), lax.axis_index("y"), lax.axis_index("c")
    me = 4 * x + 2 * y + c
    copies = []
    for a in range(n):
        src = ins[a].at[me] if scatter[a] else ins[a]
        copies.append(pltpu.make_async_copy(src, outs[a].at[me], local_sems.at[a]))
    for k in range(1, N_DEV):
        px = 1 - x if k & 4 else x
        py = 1 - y if k & 2 else y
        pc = 1 - c if k & 1 else c
        pid = 4 * px + 2 * py + pc
        for a in range(n):
            src = ins[a].at[pid] if scatter[a] else ins[a]
            copies.append(pltpu.make_async_remote_copy(
                src_ref=src, dst_ref=outs[a].at[me],
                send_sem=send_sems.at[a * (N_DEV - 1) + k - 1], recv_sem=recv_sems.at[a * (N_DEV - 1) + k - 1],
                device_id=(px, py, pc), device_id_type=MESH))
    return copies


def _qkvu(x, g1, w_in, shards):
    S = x.shape[0]
    ns = len(shards)
    nsteps = S // TM

    def body(x_ref, g_ref, w_ref, *rest):
        q_ref, k_ref, v_ref, u_ref, hn_ref = rest[ns:ns + 5]
        plan = _gather2_copies(rest[:ns], rest[ns + 5:2 * ns + 5], *rest[2 * ns + 5:])
        _gather2_begin(plan, pl.program_id(0), nsteps - 2)
        r, n = _rms(x_ref[...])
        hn = (n * g_ref[...]).astype(BF16)
        hn_ref[...] = hn
        outs = (q_ref, k_ref, v_ref, u_ref)
        for j in range(N_DEV):
            z = jnp.dot(hn, w_ref[j], preferred_element_type=F32)
            if j < 2:
                z = z * (HEAD_DIM ** -0.5)
            outs[j // 2][:, (j % 2) * 256:(j % 2 + 1) * 256] = z
        _gather2_end(plan, pl.program_id(0), nsteps)

    tok = lambda w: pl.BlockSpec((TM, w), lambda i: (i, 0))
    res = pl.pallas_call(
        body, name="qkvu", grid=(nsteps,),
        in_specs=[tok(D_MODEL), _const((1, D_MODEL)), _const(w_in.shape)] + [ANY_SPEC] * ns,
        out_specs=[tok(512)] * 4 + [tok(D_MODEL)] + [ANY_SPEC] * ns,
        out_shape=[jax.ShapeDtypeStruct((S, 512), F32)] * 4 + [jax.ShapeDtypeStruct((S, D_MODEL), BF16)]
        + _exchange_shapes(shards, (False,) * ns),
        scratch_shapes=_exchange_sems(ns),
        compiler_params=_params("arbitrary"),
    )(x, g1, w_in, *shards)
    return res[:5], res[5:]


def _attn_fill_bias(bias_s, slope_ref, hp, d):
    qi = lax.broadcasted_iota(jnp.int32, (SPAN, 2 * SPAN), 0)
    kj = lax.broadcasted_iota(jnp.int32, (SPAN, 2 * SPAN), 1)
    for t, diff in enumerate((qi + SPAN - kj, qi - kj)):
        valid = (diff >= 0) & (diff <= SPAN)
        dist = diff.astype(F32) * float(d)
        for h in range(2):
            bias_s[t, h * SPAN:(h + 1) * SPAN, :] = jnp.where(valid, -slope_ref[2 * hp + h] * dist, NEG)


def _stack_heads(x, is0):
    return jnp.concatenate([jnp.where(is0, x, 0.0), jnp.where(is0, 0.0, x)], axis=0)


def _unstack_heads(y, is0):
    return jnp.where(is0, y[0:SPAN], y[SPAN:2 * SPAN])


def _attn_block(i, g, d, nb):
    gr = min(d, ATTN_GROUP)
    gn = ATTN_GROUP // gr
    per = d // gr
    r = (i & (per - 1)) * gr + g % gr
    n = (i >> (per.bit_length() - 1)) + (g // gr) * (nb // gn)
    k0 = jnp.maximum(n - 1, 0)

    def ds(block, nrows):
        start = block * (SPAN * d) + r
        return pl.ds(start, nrows, stride=d) if d > 1 else pl.ds(start, nrows)

    return ds(n, SPAN), ds(k0, 2 * SPAN), jnp.where(n == 0, 1, 0)


def _attn_groups(S, d):
    nb = S // d // SPAN
    gn = ATTN_GROUP // min(d, ATTN_GROUP)
    assert nb >= 2 and nb % gn == 0 and (gn == 1 or nb // gn >= 2)
    return nb, d * nb // ATTN_GROUP


def _attn_fwd(slopes, q, k, v, shards):
    S = q.shape[0]
    ns = len(shards)
    steps = ATT_WIDTH // LANES

    def body(slope_ref, q_ref, k_ref, v_ref, *rest):
        o_ref, lse_ref = rest[ns:ns + 2]
        m_s, l_s, bias_s = rest[2 * ns + 2:2 * ns + 5]
        hp = pl.program_id(0)
        plan = _gather2_copies(rest[:ns], rest[ns + 2:2 * ns + 2], *rest[2 * ns + 5:])
        _gather2_begin(plan, hp, steps - 1)

        is0 = lax.broadcasted_iota(jnp.int32, (SPAN, LANES), 1) < HEAD_DIM
        for pi, d in enumerate(DILATIONS):
            nb, ngroups = _attn_groups(S, d)
            _attn_fill_bias(bias_s, slope_ref, hp, d)

            def group(i, carry, d=d, pi=pi, nb=nb):
                blocks = [_attn_block(i, g, d, nb) for g in range(ATTN_GROUP)]
                loaded = [(q_ref[rows, :], k_ref[krows, :].astype(BF16), v_ref[krows, :].astype(BF16))
                          for rows, krows, _ in blocks]
                new = []
                for (rows, krows, tab), (qb, kb, vb) in zip(blocks, loaded):
                    qs = _stack_heads(qb, is0).astype(BF16)
                    s = lax.dot_general(qs, kb, NT, preferred_element_type=F32) + bias_s[tab]
                    m = jnp.max(s, axis=-1, keepdims=True)
                    e = jnp.exp(s - m)
                    l = jnp.sum(e, axis=-1, keepdims=True)
                    pv = jnp.dot(e.astype(BF16), vb, preferred_element_type=F32)
                    new.append([_unstack_heads(jnp.broadcast_to(m, pv.shape), is0),
                                _unstack_heads(jnp.broadcast_to(l, pv.shape), is0), _unstack_heads(pv, is0)])
                if pi > 0:
                    old = [(m_s[rows, :], l_s[rows, :], o_ref[rows, :]) for rows, _, _ in blocks]
                    for st, (m_o, l_o, o_o) in zip(new, old):
                        m_n = jnp.maximum(m_o, st[0])
                        a_o = jnp.exp(m_o - m_n)
                        a_b = jnp.exp(st[0] - m_n)
                        st[:] = [m_n, a_o * l_o + a_b * st[1], a_o * o_o + a_b * st[2]]
                for (rows, _, _), (m_b, l_b, acc) in zip(blocks, new):
                    if pi == len(DILATIONS) - 1:
                        o_ref[rows, :] = acc / l_b
                        lse_ref[rows, :] = m_b + jnp.log(l_b)
                    else:
                        o_ref[rows, :] = acc
                        m_s[rows, :] = m_b
                        l_s[rows, :] = l_b
                return carry

            lax.fori_loop(0, ngroups, group, 0)

        _gather2_end(plan, hp, steps)

    col = pl.BlockSpec((S, LANES), lambda i: (0, i))
    res = pl.pallas_call(
        body, name="attn_fwd", grid=(steps,),
        in_specs=[pl.BlockSpec(memory_space=pltpu.SMEM), col, col, col] + [ANY_SPEC] * ns,
        out_specs=[col, col] + [ANY_SPEC] * ns,
        out_shape=[jax.ShapeDtypeStruct((S, ATT_WIDTH), F32)] * 2 + _exchange_shapes(shards, (False,) * ns),
        scratch_shapes=[pltpu.VMEM((S, LANES), F32), pltpu.VMEM((S, LANES), F32),
                        pltpu.VMEM((2, 2 * SPAN, 2 * SPAN), F32)] + _exchange_sems(ns),
        compiler_params=_params("arbitrary"),
    )(slopes, q, k, v, *shards)
    return res[0], res[1], res[2:]


def _pool_count(i, w):
    t = i * TM + lax.broadcasted_iota(jnp.int32, (TM, 1), 0)
    return jnp.minimum(t + 1, w).astype(F32)


def _mix_out(x, att, u, pool_w, pool_scale, w_out):
    S = x.shape[0]

    def body(x_ref, att_ref, u_ref, pw_ref, ps_ref, w_ref, h1_ref, mix_ref, dlt_ref, ubuf):
        i = pl.program_id(0)

        @pl.when(i == 0)
        def _():
            ubuf[0:HALO, :] = jnp.zeros((HALO, POOL_WIDTH), F32)

        ubuf[HALO:HALO + TM, :] = u_ref[...]
        mix_ref[:, 0:ATT_WIDTH] = att_ref[...].astype(BF16)
        for g, w in enumerate(POOL_WINDOWS):
            cols = slice(g * POOL_GROUP, (g + 1) * POOL_GROUP)
            ug = ubuf[HALO:HALO + TM, cols]
            acc = ug
            for j in range(1, w):
                acc = acc + ubuf[HALO - j:HALO - j + TM, cols]
            dlt = (acc / _pool_count(i, w) - ug).astype(BF16)
            dlt_ref[:, cols] = dlt
            yg = jnp.dot(dlt, pw_ref[g].astype(BF16), preferred_element_type=F32) * ps_ref[:, cols]
            mix_ref[:, ATT_WIDTH + g * POOL_GROUP:ATT_WIDTH + (g + 1) * POOL_GROUP] = yg.astype(BF16)
        ubuf[0:HALO, :] = ubuf[TM:TM + HALO, :]
        h1_ref[...] = x_ref[...] + jnp.dot(mix_ref[...], w_ref[...], preferred_element_type=F32)

    tok = lambda w: pl.BlockSpec((TM, w), lambda i: (i, 0))
    return pl.pallas_call(
        body, name="mix_out", grid=(S // TM,),
        in_specs=[tok(D_MODEL), tok(ATT_WIDTH), tok(POOL_WIDTH), _const(pool_w.shape), _const((1, POOL_WIDTH)),
                  _const(w_out.shape)],
        out_specs=[tok(D_MODEL), tok(D_MODEL), tok(POOL_WIDTH)],
        out_shape=[jax.ShapeDtypeStruct((S, D_MODEL), F32), jax.ShapeDtypeStruct((S, D_MODEL), BF16),
                   jax.ShapeDtypeStruct((S, POOL_WIDTH), BF16)],
        scratch_shapes=[pltpu.VMEM((TM + HALO, POOL_WIDTH), F32)],
        compiler_params=_params("arbitrary"),
    )(x, att, u, pool_w, pool_scale, w_out)


def _conv_fwd(stage, upre, prev, cw, cb):
    T = upre.shape[0]
    stage[0:HALO, :] = prev
    stage[HALO:HALO + T, :] = upre
    return cb + cw[0:1, :] * stage[HALO - 2:HALO - 2 + T, :] + cw[1:2, :] * stage[HALO - 1:HALO - 1 + T, :] + cw[2:3, :] * upre


def _ffn_fwd(h1, g2, w_up, conv_w, conv_b, w_down, shards):
    S = h1.shape[0]
    T = TM_FF
    ns = len(shards)
    nsteps = S // T

    def body(h1_ref, g_ref, wu_ref, cw_ref, cb_ref, wd_ref, *rest):
        h2_ref, hn_ref, up_ref = rest[ns:ns + 3]
        carry, stage = rest[2 * ns + 3:2 * ns + 5]
        i = pl.program_id(0)
        plan = _gather2_copies(rest[:ns], rest[ns + 3:2 * ns + 3], *rest[2 * ns + 5:])
        _gather2_begin(plan, i, nsteps // 2)

        @pl.when(i == 0)
        def _():
            carry[...] = jnp.zeros(carry.shape, F32)

        h1t = h1_ref[...]
        r, n = _rms(h1t)
        hn = (n * g_ref[...]).astype(BF16)
        hn_ref[...] = hn
        acc = h1t
        for j in range(4):
            conv = []
            for jj in (j, j + 4):
                upre = jnp.dot(hn, wu_ref[jj], preferred_element_type=F32)
                up_ref[jj] = upre.astype(BF16)
                conv.append(_conv_fwd(stage, upre, carry[jj], cw_ref[jj], cb_ref[jj]))
                carry[jj] = stage[T:T + HALO, :]
            gate, val = conv
            a = gate * jax.nn.sigmoid(gate) * val
            acc = acc + jnp.dot(a.astype(BF16), wd_ref[j], preferred_element_type=F32)
        h2_ref[...] = acc
        _gather2_end(plan, i, nsteps)

    tok = lambda w: pl.BlockSpec((T, w), lambda i: (i, 0))
    res = pl.pallas_call(
        body, name="ffn_fwd", grid=(nsteps,),
        in_specs=[tok(D_MODEL), _const((1, D_MODEL)), _const(w_up.shape), _const(conv_w.shape), _const(conv_b.shape),
                  _const(w_down.shape)] + [ANY_SPEC] * ns,
        out_specs=[tok(D_MODEL), tok(D_MODEL), pl.BlockSpec((N_DEV, T, FF_SHARD), lambda i: (0, i, 0))]
        + [ANY_SPEC] * ns,
        out_shape=[jax.ShapeDtypeStruct((S, D_MODEL), F32), jax.ShapeDtypeStruct((S, D_MODEL), BF16),
                   jax.ShapeDtypeStruct((N_DEV, S, FF_SHARD), BF16)] + _exchange_shapes(shards, (False,) * ns),
        scratch_shapes=[pltpu.VMEM((N_DEV, HALO, FF_SHARD), F32), pltpu.VMEM((T + HALO, FF_SHARD), F32)]
        + _exchange_sems(ns),
        compiler_params=_params("arbitrary"),
    )(h1, g2, w_up, conv_w, conv_b, w_down, *shards)
    return res[0], res[1], res[2], res[3:]


def _head(h2, p, g3, w_pg, w_ple, g4, target):
    S = h2.shape[0]
    nt = S // TM

    def body(h2_ref, p_ref, g3_ref, wpg_ref, wple_ref, g4_ref, t_ref,
             loss_ref, dh2_ref, dh2b_ref, hn3_ref, dgl_ref, dpe_ref, dg3_ref, dg4_ref, lacc):
        i = pl.program_id(0)

        @pl.when(i == 0)
        def _():
            lacc[...] = jnp.zeros(lacc.shape, F32)
            dg3_ref[...] = jnp.zeros(dg3_ref.shape, F32)
            dg4_ref[...] = jnp.zeros(dg4_ref.shape, F32)

        h2t = h2_ref[...]
        g3, g4 = g3_ref[...], g4_ref[...]
        r3, n3 = _rms(h2t)
        hn3 = (n3 * g3).astype(BF16)
        hn3_ref[...] = hn3
        gs = jax.nn.sigmoid(jnp.dot(hn3, wpg_ref[...], preferred_element_type=F32))
        pe = jnp.dot(p_ref[...].astype(BF16), wple_ref[...], preferred_element_type=F32)
        h3 = h2t + gs * pe
        r4, n4 = _rms(h3)
        err = n4 * g4 - t_ref[...]
        lacc[...] += _colsum(err * err)
        dy = err * (1.0 / D_MODEL)
        dg4_ref[...] += _colsum(dy * n4)
        dh3 = _rms_bwd(r4, n4, g4, dy)
        dpe_ref[...] = (dh3 * gs).astype(BF16)
        dgl = (dh3 * pe * gs * (1.0 - gs)).astype(BF16)
        dgl_ref[...] = dgl
        dhn3 = lax.dot_general(dgl, wpg_ref[...], NT, preferred_element_type=F32)
        dg3_ref[...] += _colsum(dhn3 * n3)
        dh2 = dh3 + _rms_bwd(r3, n3, g3, dhn3)
        dh2_ref[...] = dh2
        dh2b_ref[...] = dh2.astype(BF16)

        @pl.when(i == nt - 1)
        def _():
            tot = 0.5 / D_MODEL * jnp.sum(lacc[...], axis=-1, keepdims=True)
            loss_ref[...] = jnp.broadcast_to(tot, loss_ref.shape)

    tok = lambda w: pl.BlockSpec((TM, w), lambda i: (i, 0))
    row = pl.BlockSpec((1, D_MODEL), lambda i: (0, 0))
    act = lambda dt: jax.ShapeDtypeStruct((S, D_MODEL), dt)
    return pl.pallas_call(
        body, name="head", grid=(nt,),
        in_specs=[tok(D_MODEL), tok(PLE_DIM), _const((1, D_MODEL)), _const(w_pg.shape), _const(w_ple.shape),
                  _const((1, D_MODEL)), tok(D_MODEL)],
        out_specs=[pl.BlockSpec((8, LANES), lambda i: (0, 0)), tok(D_MODEL), tok(D_MODEL), tok(D_MODEL), tok(D_MODEL),
                   tok(D_MODEL), row, row],
        out_shape=[jax.ShapeDtypeStruct((8, LANES), F32), act(F32), act(BF16), act(BF16), act(BF16), act(BF16),
                   jax.ShapeDtypeStruct((1, D_MODEL), F32), jax.ShapeDtypeStruct((1, D_MODEL), F32)],
        scratch_shapes=[pltpu.VMEM((1, D_MODEL), F32)],
        compiler_params=_params("arbitrary"),
    )(h2, p, g3, w_pg, w_ple, g4, target)


def _wgrad(name, x, dy, x_kind, dy_kind, nj, k_dim, n_dim):
    S = x.shape[-2]
    nt = S // TK

    def spec(kind, width):
        if kind == "full":
            return pl.BlockSpec((TK, width), lambda j, t: (t, 0))
        return pl.BlockSpec((None, TK, width), lambda j, t: (j, t, 0))

    def body(x_ref, dy_ref, o_ref, acc):
        t = pl.program_id(1)

        @pl.when(t == 0)
        def _():
            acc[...] = jnp.zeros(acc.shape, F32)

        acc[...] += lax.dot_general(x_ref[...].astype(BF16), dy_ref[...], TN, preferred_element_type=F32)

        @pl.when(t == nt - 1)
        def _():
            o_ref[...] = acc[...].astype(BF16)

    return pl.pallas_call(
        body, name=name, grid=(nj, nt),
        in_specs=[spec(x_kind, k_dim), spec(dy_kind, n_dim)],
        out_specs=pl.BlockSpec((None, k_dim, n_dim), lambda j, t: (j, 0, 0)),
        out_shape=jax.ShapeDtypeStruct((nj, k_dim, n_dim), BF16),
        scratch_shapes=[pltpu.VMEM((k_dim, n_dim), F32)],
        compiler_params=_params("arbitrary", "arbitrary"),
    )(x, dy)


def _row_picker(T, off0, off1):
    r = lax.broadcasted_iota(jnp.int32, (2 * T, T + HALO), 0)
    c = lax.broadcasted_iota(jnp.int32, (2 * T, T + HALO), 1)
    want = jnp.where(r < T, r + off0, r - T + off1)
    return jnp.where(c == want, 1.0, 0.0).astype(BF16)


def _ffn_bwd_a(dh2b, up, w_down, conv_w, conv_b, grads):
    S = dh2b.shape[0]
    T = TM_FF
    hb = T // HALO
    nsteps = S // T
    ng = len(grads)

    def body(dh_ref, up_ref, halo_ref, wd_ref, cw_ref, cb_ref, *rest):
        a_ref, dup_ref, dcw_ref, dcb_ref = rest[ng:ng + 4]
        stage = rest[2 * ng + 4]
        i = pl.program_id(0)
        copies = _exchange_copies(rest[:ng], rest[ng + 4:2 * ng + 4], (True,) * ng, *rest[2 * ng + 5:])

        @pl.when(i == 0)
        def _():
            dcw_ref[...] = jnp.zeros(dcw_ref.shape, F32)
            dcb_ref[...] = jnp.zeros(dcb_ref.shape, F32)
            for cp in copies:
                cp.start()

        @pl.when(i == nsteps - 1)
        def _():
            for cp in copies:
                cp.wait()

        dh = dh_ref[...]
        pick = _row_picker(T, HALO - 2, HALO - 1)
        for j in range(4):
            da = lax.dot_general(dh, wd_ref[j], NT, preferred_element_type=F32)
            conv, taps = [], []
            for jj in (j, j + 4):
                upre = up_ref[jj]
                stage[0:HALO, :] = jnp.where(i > 0, halo_ref[jj], jnp.zeros((HALO, FF_SHARD), BF16))
                stage[HALO:HALO + T, :] = upre
                prv = jnp.dot(pick, stage[...], preferred_element_type=F32)
                tp = (prv[0:T], prv[T:2 * T], upre.astype(F32))
                cw = cw_ref[jj]
                conv.append(cb_ref[jj] + cw[0:1, :] * tp[0] + cw[1:2, :] * tp[1] + cw[2:3, :] * tp[2])
                taps.append(tp)
            gate, val = conv
            sg = jax.nn.sigmoid(gate)
            silu = gate * sg
            a_ref[j] = (silu * val).astype(BF16)
            dgate = da * val * (sg * (1.0 + gate * (1.0 - sg)))
            dval = da * silu
            for jj, dup, tp in ((j, dgate, taps[0]), (j + 4, dval, taps[1])):
                dup_ref[jj] = dup.astype(BF16)
                dcb_ref[jj] += _colsum(dup)
                for kk in range(3):
                    dcw_ref[jj, kk:kk + 1, :] += _colsum(dup * tp[kk])

    tok = lambda w: pl.BlockSpec((T, w), lambda i: (i, 0))
    shard = pl.BlockSpec((N_DEV, T, FF_SHARD), lambda i: (0, i, 0))
    res = pl.pallas_call(
        body, name="ffn_bwd_a", grid=(nsteps,),
        in_specs=[tok(D_MODEL), shard,
                  pl.BlockSpec((N_DEV, HALO, FF_SHARD), lambda i: (0, jnp.maximum(i * hb - 1, 0), 0)),
                  _const(w_down.shape), _const(conv_w.shape), _const(conv_b.shape)] + [ANY_SPEC] * ng,
        out_specs=[pl.BlockSpec((4, T, FF_SHARD), lambda i: (0, i, 0)), shard,
                   pl.BlockSpec((N_DEV, 3, FF_SHARD), lambda i: (0, 0, 0)),
                   pl.BlockSpec((N_DEV, 1, FF_SHARD), lambda i: (0, 0, 0))] + [ANY_SPEC] * ng,
        out_shape=[jax.ShapeDtypeStruct((4, S, FF_SHARD), BF16), jax.ShapeDtypeStruct((N_DEV, S, FF_SHARD), BF16),
                   jax.ShapeDtypeStruct((N_DEV, 3, FF_SHARD), F32), jax.ShapeDtypeStruct((N_DEV, 1, FF_SHARD), F32)]
        + _exchange_shapes(grads, (True,) * ng),
        scratch_shapes=[pltpu.VMEM((T + HALO, FF_SHARD), BF16)] + _exchange_sems(ng),
        compiler_params=_params("arbitrary"),
    )(dh2b, up, up, w_down, conv_w, conv_b, *grads)
    return res[0], res[1], res[2], res[3], res[4:]


def _ffn_bwd_b(dup, conv_w, w_up, h1, g2, dh2, grads):
    S = h1.shape[0]
    T = TM_FF
    hb = T // HALO
    nt = S // T
    ng = len(grads)

    def body(dup_ref, halo_ref, cw_ref, wu_ref, h1_ref, g_ref, dh2_ref, *rest):
        dpre_ref, dh1_ref, dh1b_ref, dg_ref = rest[ng:ng + 4]
        stage = rest[2 * ng + 4]
        i = pl.program_id(0)
        copies = _exchange_copies(rest[:ng], rest[ng + 4:2 * ng + 4], (True,) * ng, *rest[2 * ng + 5:])

        @pl.when(i == 0)
        def _():
            dg_ref[...] = jnp.zeros(dg_ref.shape, F32)
            for cp in copies:
                cp.start()

        dhn = jnp.zeros((T, D_MODEL), F32)
        for jj in range(N_DEV):
            dup = dup_ref[jj].astype(F32)
            stage[0:T, :] = dup
            stage[T:T + HALO, :] = jnp.where(i < nt - 1, halo_ref[jj].astype(F32), 0.0)
            cw = cw_ref[jj]
            dpre = (cw[2:3, :] * dup + cw[1:2, :] * stage[1:1 + T, :] + cw[0:1, :] * stage[2:2 + T, :]).astype(BF16)
            dpre_ref[jj] = dpre
            dhn = dhn + lax.dot_general(dpre, wu_ref[jj], NT, preferred_element_type=F32)
        g = g_ref[...]
        r, n = _rms(h1_ref[...])
        dg_ref[...] += _colsum(dhn * n)
        dh1 = dh2_ref[...] + _rms_bwd(r, n, g, dhn)
        dh1_ref[...] = dh1
        dh1b_ref[...] = dh1.astype(BF16)

        @pl.when(i == nt - 1)
        def _():
            for cp in copies:
                cp.wait()

    tok = lambda w: pl.BlockSpec((T, w), lambda i: (i, 0))
    shard = pl.BlockSpec((N_DEV, T, FF_SHARD), lambda i: (0, i, 0))
    res = pl.pallas_call(
        body, name="ffn_bwd_b", grid=(nt,),
        in_specs=[shard,
                  pl.BlockSpec((N_DEV, HALO, FF_SHARD), lambda i: (0, jnp.minimum((i + 1) * hb, S // HALO - 1), 0)),
                  _const(conv_w.shape), _const(w_up.shape), tok(D_MODEL), _const((1, D_MODEL)), tok(D_MODEL)]
        + [ANY_SPEC] * ng,
        out_specs=[shard, tok(D_MODEL), tok(D_MODEL), pl.BlockSpec((1, D_MODEL), lambda i: (0, 0))] + [ANY_SPEC] * ng,
        out_shape=[jax.ShapeDtypeStruct((N_DEV, S, FF_SHARD), BF16), jax.ShapeDtypeStruct((S, D_MODEL), F32),
                   jax.ShapeDtypeStruct((S, D_MODEL), BF16), jax.ShapeDtypeStruct((1, D_MODEL), F32)]
        + _exchange_shapes(grads, (True,) * ng),
        scratch_shapes=[pltpu.VMEM((T + HALO, FF_SHARD), F32)] + _exchange_sems(ng),
        compiler_params=_params("arbitrary"),
    )(dup, dup, conv_w, w_up, h1, g2, dh2, *grads)
    return res[0], res[1], res[2], res[3], res[4:]


def _mix_bwd(dh1b, w_out, dlt, pool_w, pool_scale):
    S = dh1b.shape[0]
    nt = S // TM

    def body(dh_ref, w_ref, dlt_ref, pw_ref, ps_ref, datt_ref, du_ref, dpw_ref, dps_ref, stage, carry):
        i = pl.program_id(0)
        tile = nt - 1 - i

        @pl.when(i == 0)
        def _():
            dpw_ref[...] = jnp.zeros(dpw_ref.shape, F32)
            dps_ref[...] = jnp.zeros(dps_ref.shape, F32)
            carry[...] = jnp.zeros(carry.shape, F32)

        dmix = lax.dot_general(dh_ref[...], w_ref[...], NT, preferred_element_type=F32)
        datt_ref[...] = dmix[:, 0:ATT_WIDTH]
        for g, w in enumerate(POOL_WINDOWS):
            cols = slice(g * POOL_GROUP, (g + 1) * POOL_GROUP)
            dpool = dmix[:, ATT_WIDTH + g * POOL_GROUP:ATT_WIDTH + (g + 1) * POOL_GROUP]
            dl = dlt_ref[:, cols]
            pw = pw_ref[g].astype(BF16)
            yg = jnp.dot(dl, pw, preferred_element_type=F32)
            dps_ref[:, cols] += _colsum(dpool * yg)
            dy = (dpool * ps_ref[:, cols]).astype(BF16)
            dpw_ref[g] += lax.dot_general(dl, dy, TN, preferred_element_type=F32)
            ddlt = lax.dot_general(dy, pw, NT, preferred_element_type=F32)
            cg = ddlt / _pool_count(tile, w)
            stage[0:TM, :] = cg
            stage[TM:TM + HALO, :] = carry[:, cols]
            acc = cg
            for j in range(1, w):
                acc = acc + stage[j:j + TM, :]
            du_ref[:, cols] = acc - ddlt
            carry[:, cols] = cg[0:HALO, :]

    tok = lambda w: pl.BlockSpec((TM, w), lambda i: (nt - 1 - i, 0))
    return pl.pallas_call(
        body, name="mix_bwd", grid=(nt,),
        in_specs=[tok(D_MODEL), _const(w_out.shape), tok(POOL_WIDTH), _const(pool_w.shape), _const((1, POOL_WIDTH))],
        out_specs=[tok(ATT_WIDTH), tok(POOL_WIDTH), pl.BlockSpec(pool_w.shape, lambda i: (0, 0, 0)),
                   pl.BlockSpec((1, POOL_WIDTH), lambda i: (0, 0))],
        out_shape=[jax.ShapeDtypeStruct((S, ATT_WIDTH), F32), jax.ShapeDtypeStruct((S, POOL_WIDTH), F32),
                   jax.ShapeDtypeStruct(pool_w.shape, F32), jax.ShapeDtypeStruct((1, POOL_WIDTH), F32)],
        scratch_shapes=[pltpu.VMEM((TM + HALO, POOL_GROUP), F32), pltpu.VMEM((HALO, POOL_WIDTH), F32)],
        compiler_params=_params("arbitrary"),
    )(dh1b, w_out, dlt, pool_w, pool_scale)


def _attn_bwd(slopes, q, k, v, o, lse, do, grads, scatter):
    S = q.shape[0]
    CH = 512
    ng = len(grads)
    steps = ATT_WIDTH // LANES

    def body(slope_ref, q_ref, k_ref, v_ref, o_ref, lse_ref, do_ref, *rest):
        dq_ref, dk_ref, dv_ref = rest[ng:ng + 3]
        dl_s, bias_s = rest[2 * ng + 3:2 * ng + 5]
        hp = pl.program_id(0)
        copies = _exchange_copies(rest[:ng], rest[ng + 3:2 * ng + 3], scatter, *rest[2 * ng + 5:])

        @pl.when(hp == 0)
        def _():
            for cp in copies:
                cp.start()

        is0 = lax.broadcasted_iota(jnp.int32, (SPAN, LANES), 1) < HEAD_DIM
        is0c = lax.broadcasted_iota(jnp.int32, (CH, LANES), 1) < HEAD_DIM

        def prep(ci, carry):
            rows = pl.ds(pl.multiple_of(ci * CH, CH), CH)
            prod = do_ref[rows, :] * o_ref[rows, :]
            d0 = jnp.sum(jnp.where(is0c, prod, 0.0), axis=-1, keepdims=True)
            d1 = jnp.sum(jnp.where(is0c, 0.0, prod), axis=-1, keepdims=True)
            dl_s[rows, :] = jnp.where(is0c, d0, d1)
            zero = jnp.zeros((CH, LANES), F32)
            dq_ref[rows, :] = zero
            dk_ref[rows, :] = zero
            dv_ref[rows, :] = zero
            return carry

        lax.fori_loop(0, S // CH, prep, 0)

        for d in DILATIONS:
            nb, ngroups = _attn_groups(S, d)
            _attn_fill_bias(bias_s, slope_ref, hp, d)

            def group(i, carry, d=d, nb=nb):
                blocks = [_attn_block(i, g, d, nb) for g in range(ATTN_GROUP)]
                loaded = [(q_ref[rows, :], do_ref[rows, :], lse_ref[rows, :], dl_s[rows, :], k_ref[krows, :],
                           v_ref[krows, :].astype(BF16)) for rows, krows, _ in blocks]
                new = []
                for (rows, krows, tab), (qb, dob, lse_b, dl_b, kf, vb) in zip(blocks, loaded):
                    kb = kf.astype(BF16)
                    qs = _stack_heads(qb, is0).astype(BF16)
                    dos = _stack_heads(dob, is0).astype(BF16)
                    lse_s = jnp.concatenate([lse_b[:, 0:1], lse_b[:, HEAD_DIM:HEAD_DIM + 1]], axis=0)
                    dl_s2 = jnp.concatenate([dl_b[:, 0:1], dl_b[:, HEAD_DIM:HEAD_DIM + 1]], axis=0)
                    s = lax.dot_general(qs, kb, NT, preferred_element_type=F32) + bias_s[tab]
                    pr = jnp.exp(s - lse_s)
                    dp = lax.dot_general(dos, vb, NT, preferred_element_type=F32)
                    ds = (pr * (dp - dl_s2)).astype(BF16)
                    dv_c = lax.dot_general(pr.astype(BF16), dos, TN, preferred_element_type=F32)
                    dk_c = lax.dot_general(ds, qs, TN, preferred_element_type=F32)
                    dq_c = _unstack_heads(jnp.dot(ds, kb, preferred_element_type=F32), is0)
                    new.append((dq_c, dk_c, dv_c))
                old = [(dq_ref[rows, :], dk_ref[krows, :], dv_ref[krows, :]) for rows, krows, _ in blocks]
                for (rows, krows, _), (dq_c, dk_c, dv_c), (dq_o, dk_o, dv_o) in zip(blocks, new, old):
                    dq_ref[rows, :] = dq_o + dq_c
                    dk_ref[krows, :] = dk_o + dk_c
                    dv_ref[krows, :] = dv_o + dv_c
                return carry

            lax.fori_loop(0, ngroups, group, 0)

        @pl.when(hp == steps - 1)
        def _():
            for cp in copies:
                cp.wait()

    col = pl.BlockSpec((S, LANES), lambda i: (0, i))
    res = pl.pallas_call(
        body, name="attn_bwd", grid=(steps,),
        in_specs=[pl.BlockSpec(memory_space=pltpu.SMEM)] + [col] * 6 + [ANY_SPEC] * ng,
        out_specs=[col] * 3 + [ANY_SPEC] * ng,
        out_shape=[jax.ShapeDtypeStruct((S, ATT_WIDTH), F32)] * 3 + _exchange_shapes(grads, scatter),
        scratch_shapes=[pltpu.VMEM((S, LANES), F32), pltpu.VMEM((2, 2 * SPAN, 2 * SPAN), F32)] + _exchange_sems(ng),
        compiler_params=_params("arbitrary"),
    )(slopes, q, k, v, o, lse, do, *grads)
    return res[0], res[1], res[2], res[3:]


def _in_bwd(dq, dk, dv, du, w_in, x, g1, dh1):
    S = x.shape[0]

    def body(dq_ref, dk_ref, dv_ref, du_ref, w_ref, x_ref, g_ref, dh1_ref, dz_ref, dx_ref, dg_ref):
        @pl.when(pl.program_id(0) == 0)
        def _():
            dg_ref[...] = jnp.zeros(dg_ref.shape, F32)

        srcs = (dq_ref, dk_ref, dv_ref, du_ref)
        dhn = jnp.zeros((TM, D_MODEL), F32)
        for j in range(N_DEV):
            dz = srcs[j // 2][:, (j % 2) * 256:(j % 2 + 1) * 256]
            if j < 2:
                dz = dz * (HEAD_DIM ** -0.5)
            dz = dz.astype(BF16)
            dz_ref[j] = dz
            dhn = dhn + lax.dot_general(dz, w_ref[j], NT, preferred_element_type=F32)
        g = g_ref[...]
        r, n = _rms(x_ref[...])
        dg_ref[...] += _colsum(dhn * n)
        dx_ref[...] = dh1_ref[...] + _rms_bwd(r, n, g, dhn)

    tok = lambda w: pl.BlockSpec((TM, w), lambda i: (i, 0))
    return pl.pallas_call(
        body, name="in_bwd", grid=(S // TM,),
        in_specs=[tok(512)] * 4 + [_const(w_in.shape), tok(D_MODEL), _const((1, D_MODEL)), tok(D_MODEL)],
        out_specs=[pl.BlockSpec((N_DEV, TM, 256), lambda i: (0, i, 0)), tok(D_MODEL),
                   pl.BlockSpec((1, D_MODEL), lambda i: (0, 0))],
        out_shape=[jax.ShapeDtypeStruct((N_DEV, S, 256), BF16), jax.ShapeDtypeStruct((S, D_MODEL), F32),
                   jax.ShapeDtypeStruct((1, D_MODEL), F32)],
        compiler_params=_params("arbitrary"),
    )(dq, dk, dv, du, w_in, x, g1, dh1)


def _adamw(name, parts, w, m, v):
    R, C = w.shape
    rb = R
    for cand in (256, 128, 64, 32, 16, 8):
        if R % cand == 0 and R > cand:
            rb = cand
            break

    def body(p_ref, w_ref, m_ref, v_ref, g_ref, d_ref, mo_ref, vo_ref):
        g = p_ref[0].astype(F32)
        for s in range(1, N_DEV):
            g = g + p_ref[s].astype(F32)
        g_ref[...] = g
        d_ref[...], mo_ref[...], vo_ref[...] = _adam_update(g, w_ref[...], m_ref[...], v_ref[...])

    blk = pl.BlockSpec((rb, C), lambda i: (i, 0))
    return pl.pallas_call(
        body, name=name, grid=(R // rb,),
        in_specs=[pl.BlockSpec((N_DEV, rb, C), lambda i: (0, i, 0)), blk, blk, blk],
        out_specs=[blk] * 4,
        out_shape=[jax.ShapeDtypeStruct((R, C), F32)] * 4,
        compiler_params=_params("arbitrary"),
    )(parts, w, m, v)


def _adam_update(g, w, m, v):
    m_new = ADAM_B1 * m + (1.0 - ADAM_B1) * g
    v_new = ADAM_B2 * v + (1.0 - ADAM_B2) * (g * g)
    m_hat = m_new / (1.0 - ADAM_B1 ** ADAM_STEP)
    v_hat = v_new / (1.0 - ADAM_B2 ** ADAM_STEP)
    return -ADAM_LR * (m_hat / (jnp.sqrt(v_hat) + ADAM_EPS) + ADAM_WD * w), m_new, v_new


def _adamw_small(parts, loss_parts, ws, ms, vs):
    n = len(ws)

    def body(*refs):
        p_refs, lp_ref = refs[:n], refs[n]
        w_refs, m_refs, v_refs = refs[n + 1:2 * n + 1], refs[2 * n + 1:3 * n + 1], refs[3 * n + 1:4 * n + 1]
        outs = refs[4 * n + 1:]
        for i in range(n):
            g = p_refs[i][0]
            for s in range(1, N_DEV):
                g = g + p_refs[i][s]
            d, m_new, v_new = _adam_update(g, w_refs[i][...], m_refs[i][...], v_refs[i][...])
            outs[i][...] = g
            outs[n + i][...] = d
            outs[2 * n + i][...] = m_new
            outs[3 * n + i][...] = v_new
        tot = lp_ref[0]
        for s in range(1, N_DEV):
            tot = tot + lp_ref[s]
        outs[4 * n][...] = tot

    shapes = [jax.ShapeDtypeStruct(w.shape, F32) for w in ws]
    res = pl.pallas_call(
        body, name="adamw_replicated",
        out_shape=shapes * 4 + [jax.ShapeDtypeStruct(loss_parts.shape[1:], F32)],
        compiler_params=_params(),
    )(*parts, loss_parts, *ws, *ms, *vs)
    return res[:n], res[n:2 * n], res[2 * n:3 * n], res[3 * n:4 * n], res[4 * n]


def _gather2(name, arrays):
    n = len(arrays)

    def body(*refs):
        first, passed, last = _gather2_copies(refs[:n], refs[n:2 * n], *refs[2 * n:])
        for cp in first:
            cp.start()
        for arrival, cp in passed:
            arrival.wait_recv()
            cp.start()
        for wait in last:
            wait()

    return pl.pallas_call(
        body, name=name,
        in_specs=[ANY_SPEC] * n, out_specs=[ANY_SPEC] * n, out_shape=_exchange_shapes(arrays, (False,) * n),
        scratch_shapes=_exchange_sems(n),
    )(*arrays)


def _dw_in_exchange(hn, dz, small):
    S = hn.shape[0]
    nt = S // TK
    ns = len(small)
    kd, nd = hn.shape[1], dz.shape[2]
    me_arr = (4 * lax.axis_index("x") + 2 * lax.axis_index("y") + lax.axis_index("c")).astype(jnp.int32).reshape(1)

    def body(me_ref, x_ref, dy_ref, *rest):
        recv_ref = rest[ns]
        acc, stage, send_sems, recv_sems, own_sem = rest[2 * ns + 1:2 * ns + 6]
        j, t = pl.program_id(0), pl.program_id(1)
        x, y, c = lax.axis_index("x"), lax.axis_index("y"), lax.axis_index("c")
        me = 4 * x + 2 * y + c
        small_copies = _exchange_copies(rest[:ns], rest[ns + 1:2 * ns + 1], (False,) * ns, *rest[2 * ns + 6:])

        @pl.when((j == 0) & (t == 0))
        def _():
            for cp in small_copies:
                cp.start()

        @pl.when(t == 0)
        def _():
            acc[...] = jnp.zeros(acc.shape, F32)

        acc[...] += lax.dot_general(x_ref[...], dy_ref[...], TN, preferred_element_type=F32)

        def to_owner(k, owner):
            return pltpu.make_async_remote_copy(
                src_ref=stage.at[owner], dst_ref=recv_ref.at[me], send_sem=send_sems.at[k], recv_sem=recv_sems.at[k],
                device_id=(owner // 4, (owner // 2) % 2, owner % 2), device_id_type=MESH)

        own = pltpu.make_async_copy(stage.at[me], recv_ref.at[me], own_sem)

        @pl.when(t == nt - 1)
        def _():
            owner = (me + 1 + j) % N_DEV
            stage[owner] = acc[...].astype(BF16)

            @pl.when(j < N_DEV - 1)
            def _():
                to_owner(j, owner).start()

            @pl.when(j == N_DEV - 1)
            def _():
                own.start()
                own.wait()
                for k in range(N_DEV - 1):
                    to_owner(k, me).wait_send()
                    to_owner(k, me).wait_recv()
                for cp in small_copies:
                    cp.wait()

    slab = lambda j, me_ref: (me_ref[0] + 1 + j) % N_DEV
    grid_spec = pltpu.PrefetchScalarGridSpec(
        num_scalar_prefetch=1, grid=(N_DEV, nt),
        in_specs=[pl.BlockSpec((TK, kd), lambda j, t, me_ref: (t, 0)),
                  pl.BlockSpec((None, TK, nd), lambda j, t, me_ref: (slab(j, me_ref), t, 0))] + [ANY_SPEC] * ns,
        out_specs=[ANY_SPEC] * (ns + 1),
        scratch_shapes=[pltpu.VMEM((kd, nd), F32), pltpu.VMEM((N_DEV, kd, nd), BF16),
                        pltpu.SemaphoreType.DMA((N_DEV - 1,)), pltpu.SemaphoreType.DMA((N_DEV - 1,)),
                        pltpu.SemaphoreType.DMA] + _exchange_sems(ns))
    res = pl.pallas_call(
        body, name="dw_in_exchange", grid_spec=grid_spec,
        out_shape=[jax.ShapeDtypeStruct((N_DEV, kd, nd), BF16)] + _exchange_shapes(small, (False,) * ns),
        compiler_params=_params("arbitrary", "arbitrary"),
    )(me_arr, hn, dz, *small)
    return res[0], res[1:]


def kernel(x, p, ln_mix, w_in, pool_w, pool_scale, w_out, ln_ffn, w_up, conv_w, conv_b, w_down, ln_ple, w_ple_gate, w_ple, ln_final, loss_target, m_ln_mix, m_w_in, m_pool_w, m_pool_scale, m_w_out, m_ln_ffn, m_w_up, m_conv_w, m_conv_b, m_w_down, m_ln_ple, m_w_ple_gate, m_w_ple, m_ln_final, v_ln_mix, v_w_in, v_pool_w, v_pool_scale, v_w_out, v_ln_ffn, v_w_up, v_conv_w, v_conv_b, v_w_down, v_ln_ple, v_w_ple_gate, v_w_ple, v_ln_final):
    xs, ps, tgt, pool_w0 = x[0], p[0, 0], loss_target[0], pool_w[0]
    slopes = jnp.exp2(-8.0 * (jnp.arange(N_HEADS, dtype=F32) + 1.0) / N_HEADS)
    conv_b_s = conv_b.reshape(N_DEV, 1, FF_SHARD)

    (w_in_g,) = _gather2("gather_w_in", [w_in[0].astype(BF16)])
    (q, k, v, u, hn1), (w_out_g,) = _qkvu(xs, ln_mix, w_in_g, [w_out[0].astype(BF16)])
    att, lse, (w_up_g, w_down_g, conv_w_g) = _attn_fwd(
        slopes, q, k, v, [w_up[0].astype(BF16), w_down[0].astype(BF16), conv_w[0]])
    w_out_f = w_out_g.reshape(D_MODEL, D_MODEL)
    w_down_f = w_down_g.reshape(4, FF_SHARD, D_MODEL)
    h1, mix, dlt = _mix_out(xs, att, u, pool_w0, pool_scale, w_out_f)
    h2, hn2, up, (w_pg_g, w_ple_g) = _ffn_fwd(h1, ln_ffn, w_up_g, conv_w_g, conv_b_s, w_down_f,
                                              [w_ple_gate[0].astype(BF16), w_ple[0].astype(BF16)])
    w_pg_f = w_pg_g.reshape(D_MODEL, D_MODEL)
    w_ple_f = jnp.transpose(w_ple_g, (1, 0, 2)).reshape(PLE_DIM, D_MODEL)
    loss_blk, dh2, dh2b, hn3, dgl, dpe, d_ln_ple, d_ln_final = _head(
        h2, ps, ln_ple, w_pg_f, w_ple_f, ln_final.reshape(1, D_MODEL), tgt)

    d_w_pg = _wgrad("dw_ple_gate", hn3, dgl, "full", "full", 1, D_MODEL, D_MODEL).reshape(N_DEV, D_MODEL // N_DEV, D_MODEL)
    d_w_ple = _wgrad("dw_ple", ps, dpe, "full", "full", 1, PLE_DIM, D_MODEL)
    d_w_ple = jnp.transpose(d_w_ple.reshape(PLE_DIM, N_DEV, LANES), (1, 0, 2))
    a, dup, d_conv_w, d_conv_b, (r_w_pg, r_w_ple) = _ffn_bwd_a(dh2b, up, w_down_f, conv_w_g, conv_b_s, [d_w_pg, d_w_ple])
    d_w_down = _wgrad("dw_down", a, dh2b, "lead", "full", 4, FF_SHARD, D_MODEL).reshape(N_DEV, D_FF // N_DEV, D_MODEL)
    dpre, dh1, dh1b, d_ln_ffn, (r_conv_w, r_w_down) = _ffn_bwd_b(
        dup, conv_w_g, w_up_g, h1, ln_ffn, dh2, [d_conv_w, d_w_down])
    datt, du, d_pool_w, d_pool_scale = _mix_bwd(dh1b, w_out_f, dlt, pool_w0, pool_scale)
    d_w_out = _wgrad("dw_out", mix, dh1b, "full", "full", 1, D_MODEL, D_MODEL).reshape(N_DEV, D_MODEL // N_DEV, D_MODEL)
    d_w_up = _wgrad("dw_up", hn2, dpre, "full", "lead", N_DEV, D_MODEL, FF_SHARD)
    rep_late = [d_pool_w, d_pool_scale, d_ln_ffn, d_conv_b.reshape(1, 2 * D_FF), d_ln_ple, d_ln_final, loss_blk]
    dq, dk, dv, received = _attn_bwd(slopes, q, k, v, att, lse, datt, [d_w_out, d_w_up] + rep_late,
                                     (True, True) + (False,) * len(rep_late))
    r_w_out, r_w_up, r_rep = received[0], received[1], list(received[2:])
    dz, grad_x, d_ln_mix = _in_bwd(dq, dk, dv, du, w_in_g, xs, ln_mix, dh1)

    rep_names = ("ln_mix", "pool_w", "pool_scale", "ln_ffn", "conv_b", "ln_ple", "ln_final")
    rep_w = [ln_mix, pool_w0, pool_scale, ln_ffn, conv_b, ln_ple, ln_final.reshape(1, D_MODEL)]
    rep_m = [m_ln_mix, m_pool_w[0], m_pool_scale, m_ln_ffn, m_conv_b, m_ln_ple, m_ln_final.reshape(1, D_MODEL)]
    rep_v = [v_ln_mix, v_pool_w[0], v_pool_scale, v_ln_ffn, v_conv_b, v_ln_ple, v_ln_final.reshape(1, D_MODEL)]
    r_w_in, (r_ln_mix,) = _dw_in_exchange(hn1, dz, [d_ln_mix])
    small = _adamw_small([r_ln_mix] + r_rep[:-1], r_rep[-1], rep_w, rep_m, rep_v)
    loss = small[4][0, 0]

    sharded = {}
    sharded["w_in"] = _adamw("adamw_w_in", r_w_in, w_in[0], m_w_in[0], v_w_in[0])
    sharded["w_out"] = _adamw("adamw_w_out", r_w_out, w_out[0], m_w_out[0], v_w_out[0])
    sharded["w_up"] = _adamw("adamw_w_up", r_w_up, w_up[0], m_w_up[0], v_w_up[0])
    sharded["conv_w"] = _adamw("adamw_conv_w", r_conv_w, conv_w[0], m_conv_w[0], v_conv_w[0])
    sharded["w_down"] = _adamw("adamw_w_down", r_w_down, w_down[0], m_w_down[0], v_w_down[0])
    sharded["w_ple_gate"] = _adamw("adamw_w_ple_gate", r_w_pg, w_ple_gate[0], m_w_ple_gate[0], v_w_ple_gate[0])
    sharded["w_ple"] = _adamw("adamw_w_ple", r_w_ple, w_ple[0], m_w_ple[0], v_w_ple[0])

    shapes = dict(w_in=w_in, w_out=w_out, w_up=w_up, conv_w=conv_w, w_down=w_down, w_ple_gate=w_ple_gate, w_ple=w_ple,
                  ln_mix=ln_mix, pool_w=pool_w, pool_scale=pool_scale, ln_ffn=ln_ffn, conv_b=conv_b, ln_ple=ln_ple,
                  ln_final=ln_final)

    def leaf(kind, n):
        src = sharded[n][kind] if n in sharded else small[kind][rep_names.index(n)]
        return src.reshape(shapes[n].shape)

    order = ("ln_mix", "w_in", "pool_w", "pool_scale", "w_out", "ln_ffn", "w_up", "conv_w", "conv_b", "w_down", "ln_ple",
             "w_ple_gate", "w_ple", "ln_final")
    outs = [loss, grad_x[None]]
    for kind in range(4):
        outs += [leaf(kind, n) for n in order]
    return tuple(outs)
```

```python
import jax
import jax.numpy as jnp
from jax import lax
from jax.experimental import pallas as pl
from jax.experimental.pallas import tpu as pltpu

F32 = jnp.float32
BF16 = jnp.bfloat16

N_DEV = 8
D_MODEL = 1024
ATT_WIDTH = 512
POOL_WIDTH = 512
N_HEADS = 8
HEAD_DIM = 64
SPAN = 128
DILATIONS = (1, 4, 16)
POOL_WINDOWS = (2, 4, 8, 16)
POOL_GROUP = 128
D_FF = 2816
FF_SHARD = 2 * D_FF // N_DEV
PLE_DIM = 256
EPS = 1e-6
NEG = -1e30

ADAM_LR = 0.001
ADAM_B1 = 0.9
ADAM_B2 = 0.999
ADAM_EPS = 1e-08
ADAM_WD = 0.01
ADAM_STEP = 10

LANES = 128
HALO = 16
TM = 512
TM_FF = 256
TK = 1024
ATTN_GROUP = 8
VMEM_LIMIT = 56 * 1024 * 1024

MESH = pl.DeviceIdType.MESH
NT = (((1,), (1,)), ((), ()))
TN = (((0,), (0,)), ((), ()))


def _params(*sem):
    return pltpu.CompilerParams(dimension_semantics=sem or None, vmem_limit_bytes=VMEM_LIMIT)


def _const(shape):
    n = len(shape)
    return pl.BlockSpec(shape, lambda *_: (0,) * n, pipeline_mode=pl.Buffered(1))


def _rms(h):
    r = lax.rsqrt(jnp.mean(h * h, axis=-1, keepdims=True) + EPS)
    return r, h * r


def _rms_bwd(r, n, g, dhn):
    dn = dhn * g
    return r * (dn - n * jnp.mean(dn * n, axis=-1, keepdims=True))


def _colsum(a):
    return jnp.sum(a, axis=0, keepdims=True)


def _gather2_copies(ins, outs, send_sems, recv_sems, local_sems):
    n = len(ins)
    x, y, c = lax.axis_index("x"), lax.axis_index("y"), lax.axis_index("c")
    slot = lambda px, py, pc: 4 * px + 2 * py + pc
    chips = [(x, 1 - y), (1 - x, y), (1 - x, 1 - y)]
    first, passed, last = [], [], []

    def remote(a, r, src, dst_slot, to):
        return pltpu.make_async_remote_copy(
            src_ref=src, dst_ref=outs[a].at[dst_slot],
            send_sem=send_sems.at[a * (N_DEV - 1) + r], recv_sem=recv_sems.at[a * (N_DEV - 1) + r],
            device_id=to, device_id_type=MESH)

    for a in range(n):
        mine = pltpu.make_async_copy(ins[a], outs[a].at[slot(x, y, c)], local_sems.at[a])
        to_sibling = remote(a, 0, ins[a], slot(x, y, c), (x, y, 1 - c))
        first += [mine, to_sibling]
        last += [mine.wait, to_sibling.wait_send, to_sibling.wait_recv]
        for r, (px, py) in enumerate(chips, start=1):
            to_chip = remote(a, r, ins[a], slot(x, y, c), (px, py, c))
            onward = remote(a, 3 + r, outs[a].at[slot(px, py, c)], slot(px, py, c), (x, y, 1 - c))
            first.append(to_chip)
            passed.append((to_chip, onward))
            last += [to_chip.wait_send, onward.wait_send, onward.wait_recv]
    return first, passed, last


def _gather2_begin(plan, step, pass_step):
    first, passed, _ = plan

    @pl.when(step == 0)
    def _():
        for cp in first:
            cp.start()

    @pl.when(step == pass_step)
    def _():
        for arrival, cp in passed:
            arrival.wait_recv()
            cp.start()


def _gather2_end(plan, step, nsteps):
    @pl.when(step == nsteps - 1)
    def _():
        for wait in plan[2]:
            wait()


ANY_SPEC = pl.BlockSpec(memory_space=pl.ANY)


def _exchange_shapes(arrays, scatter):
    out = []
    for a, s in zip(arrays, scatter):
        slab = a.shape[1:] if s else a.shape
        out.append(jax.ShapeDtypeStruct((N_DEV,) + tuple(slab), a.dtype))
    return out


def _exchange_sems(n):
    return [pltpu.SemaphoreType.DMA((n * (N_DEV - 1),)), pltpu.SemaphoreType.DMA((n * (N_DEV - 1),)),
            pltpu.SemaphoreType.DMA((n,))]


def _exchange_copies(ins, outs, scatter, send_sems, recv_sems, local_sems):
    n = len(ins)
    x, y, c = lax.axis_index("x"), lax.axis_index("y"), lax.axis_index("c")
    me = 4 * x + 2 * y + c
    copies = []
    for a in range(n):
        src = ins[a].at[me] if scatter[a] else ins[a]
        copies.append(pltpu.make_async_copy(src, outs[a].at[me], local_sems.at[a]))
    for k in range(1, N_DEV):
        px = 1 - x if k & 4 else x
        py = 1 - y if k & 2 else y
        pc = 1 - c if k & 1 else c
        pid = 4 * px + 2 * py + pc
        for a in range(n):
            src = ins[a].at[pid] if scatter[a] else ins[a]
            copies.append(pltpu.make_async_remote_copy(
                src_ref=src, dst_ref=outs[a].at[me],
                send_sem=send_sems.at[a * (N_DEV - 1) + k - 1], recv_sem=recv_sems.at[a * (N_DEV - 1) + k - 1],
                device_id=(px, py, pc), device_id_type=MESH))
    return copies


def _qkvu(x, g1, w_in, shards):
    S = x.shape[0]
    ns = len(shards)
    nsteps = S // TM

    def body(x_ref, g_ref, w_ref, *rest):
        q_ref, k_ref, v_ref, u_ref, hn_ref = rest[ns:ns + 5]
        plan = _gather2_copies(rest[:ns], rest[ns + 5:2 * ns + 5], *rest[2 * ns + 5:])
        _gather2_begin(plan, pl.program_id(0), nsteps - 2)
        r, n = _rms(x_ref[...])
        hn = (n * g_ref[...]).astype(BF16)
        hn_ref[...] = hn
        outs = (q_ref, k_ref, v_ref, u_ref)
        for j in range(N_DEV):
            z = jnp.dot(hn, w_ref[j], preferred_element_type=F32)
            if j < 2:
                z = z * (HEAD_DIM ** -0.5)
            outs[j // 2][:, (j % 2) * 256:(j % 2 + 1) * 256] = z
        _gather2_end(plan, pl.program_id(0), nsteps)

    tok = lambda w: pl.BlockSpec((TM, w), lambda i: (i, 0))
    res = pl.pallas_call(
        body, name="qkvu", grid=(nsteps,),
        in_specs=[tok(D_MODEL), _const((1, D_MODEL)), _const(w_in.shape)] + [ANY_SPEC] * ns,
        out_specs=[tok(512)] * 4 + [tok(D_MODEL)] + [ANY_SPEC] * ns,
        out_shape=[jax.ShapeDtypeStruct((S, 512), F32)] * 4 + [jax.ShapeDtypeStruct((S, D_MODEL), BF16)]
        + _exchange_shapes(shards, (False,) * ns),
        scratch_shapes=_exchange_sems(ns),
        compiler_params=_params("arbitrary"),
    )(x, g1, w_in, *shards)
    return res[:5], res[5:]


def _attn_fill_bias(bias_s, slope_ref, hp, d):
    qi = lax.broadcasted_iota(jnp.int32, (SPAN, 2 * SPAN), 0)
    kj = lax.broadcasted_iota(jnp.int32, (SPAN, 2 * SPAN), 1)
    for t, diff in enumerate((qi + SPAN - kj, qi - kj)):
        valid = (diff >= 0) & (diff <= SPAN)
        dist = diff.astype(F32) * float(d)
        for h in range(2):
            bias_s[t, h * SPAN:(h + 1) * SPAN, :] = jnp.where(valid, -slope_ref[2 * hp + h] * dist, NEG)


def _stack_heads(x, is0):
    return jnp.concatenate([jnp.where(is0, x, 0.0), jnp.where(is0, 0.0, x)], axis=0)


def _unstack_heads(y, is0):
    return jnp.where(is0, y[0:SPAN], y[SPAN:2 * SPAN])


def _attn_block(i, g, d, nb):
    gr = min(d, ATTN_GROUP)
    gn = ATTN_GROUP // gr
    per = d // gr
    r = (i & (per - 1)) * gr + g % gr
    n = (i >> (per.bit_length() - 1)) + (g // gr) * (nb // gn)
    k0 = jnp.maximum(n - 1, 0)

    def ds(block, nrows):
        start = block * (SPAN * d) + r
        return pl.ds(start, nrows, stride=d) if d > 1 else pl.ds(start, nrows)

    return ds(n, SPAN), ds(k0, 2 * SPAN), jnp.where(n == 0, 1, 0)


def _attn_groups(S, d):
    nb = S // d // SPAN
    gn = ATTN_GROUP // min(d, ATTN_GROUP)
    assert nb >= 2 and nb % gn == 0 and (gn == 1 or nb // gn >= 2)
    return nb, d * nb // ATTN_GROUP


def _attn_fwd(slopes, q, k, v, shards):
    S = q.shape[0]
    ns = len(shards)
    steps = ATT_WIDTH // LANES

    def body(slope_ref, q_ref, k_ref, v_ref, *rest):
        o_ref, lse_ref = rest[ns:ns + 2]
        m_s, l_s, bias_s = rest[2 * ns + 2:2 * ns + 5]
        hp = pl.program_id(0)
        plan = _gather2_copies(rest[:ns], rest[ns + 2:2 * ns + 2], *rest[2 * ns + 5:])
        _gather2_begin(plan, hp, steps - 1)

        is0 = lax.broadcasted_iota(jnp.int32, (SPAN, LANES), 1) < HEAD_DIM
        for pi, d in enumerate(DILATIONS):
            nb, ngroups = _attn_groups(S, d)
            _attn_fill_bias(bias_s, slope_ref, hp, d)

            def group(i, carry, d=d, pi=pi, nb=nb):
                blocks = [_attn_block(i, g, d, nb) for g in range(ATTN_GROUP)]
                loaded = [(q_ref[rows, :], k_ref[krows, :].astype(BF16), v_ref[krows, :].astype(BF16))
                          for rows, krows, _ in blocks]
                new = []
                for (rows, krows, tab), (qb, kb, vb) in zip(blocks, loaded):
                    qs = _stack_heads(qb, is0).astype(BF16)
                    s = lax.dot_general(qs, kb, NT, preferred_element_type=F32) + bias_s[tab]
                    m = jnp.max(s, axis=-1, keepdims=True)
                    e = jnp.exp(s - m)
                    l = jnp.sum(e, axis=-1, keepdims=True)
                    pv = jnp.dot(e.astype(BF16), vb, preferred_element_type=F32)
                    new.append([_unstack_heads(jnp.broadcast_to(m, pv.shape), is0),
                                _unstack_heads(jnp.broadcast_to(l, pv.shape), is0), _unstack_heads(pv, is0)])
                if pi > 0:
                    old = [(m_s[rows, :], l_s[rows, :], o_ref[rows, :]) for rows, _, _ in blocks]
                    for st, (m_o, l_o, o_o) in zip(new, old):
                        m_n = jnp.maximum(m_o, st[0])
                        a_o = jnp.exp(m_o - m_n)
                        a_b = jnp.exp(st[0] - m_n)
                        st[:] = [m_n, a_o * l_o + a_b * st[1], a_o * o_o + a_b * st[2]]
                for (rows, _, _), (m_b, l_b, acc) in zip(blocks, new):
                    if pi == len(DILATIONS) - 1:
                        o_ref[rows, :] = acc / l_b
                        lse_ref[rows, :] = m_b + jnp.log(l_b)
                    else:
                        o_ref[rows, :] = acc
                        m_s[rows, :] = m_b
                        l_s[rows, :] = l_b
                return carry

            lax.fori_loop(0, ngroups, group, 0)

        _gather2_end(plan, hp, steps)

    col = pl.BlockSpec((S, LANES), lambda i: (0, i))
    res = pl.pallas_call(
        body, name="attn_fwd", grid=(steps,),
        in_specs=[pl.BlockSpec(memory_space=pltpu.SMEM), col, col, col] + [ANY_SPEC] * ns,
        out_specs=[col, col] + [ANY_SPEC] * ns,
        out_shape=[jax.ShapeDtypeStruct((S, ATT_WIDTH), F32)] * 2 + _exchange_shapes(shards, (False,) * ns),
        scratch_shapes=[pltpu.VMEM((S, LANES), F32), pltpu.VMEM((S, LANES), F32),
                        pltpu.VMEM((2, 2 * SPAN, 2 * SPAN), F32)] + _exchange_sems(ns),
        compiler_params=_params("arbitrary"),
    )(slopes, q, k, v, *shards)
    return res[0], res[1], res[2:]


def _pool_count(i, w):
    t = i * TM + lax.broadcasted_iota(jnp.int32, (TM, 1), 0)
    return jnp.minimum(t + 1, w).astype(F32)


def _mix_out(x, att, u, pool_w, pool_scale, w_out, shards):
    S = x.shape[0]
    ns = len(shards)
    nsteps = S // TM

    def body(x_ref, att_ref, u_ref, pw_ref, ps_ref, w_ref, *rest):
        h1_ref, mix_ref, dlt_ref = rest[ns:ns + 3]
        ubuf = rest[2 * ns + 3]
        i = pl.program_id(0)
        plan = _gather2_copies(rest[:ns], rest[ns + 3:2 * ns + 3], *rest[2 * ns + 4:])
        _gather2_begin(plan, i, nsteps - 1)

        @pl.when(i == 0)
        def _():
            ubuf[0:HALO, :] = jnp.zeros((HALO, POOL_WIDTH), F32)

        ubuf[HALO:HALO + TM, :] = u_ref[...]
        mix_ref[:, 0:ATT_WIDTH] = att_ref[...].astype(BF16)
        for g, w in enumerate(POOL_WINDOWS):
            cols = slice(g * POOL_GROUP, (g + 1) * POOL_GROUP)
            ug = ubuf[HALO:HALO + TM, cols]
            acc = ug
            for j in range(1, w):
                acc = acc + ubuf[HALO - j:HALO - j + TM, cols]
            dlt = (acc / _pool_count(i, w) - ug).astype(BF16)
            dlt_ref[:, cols] = dlt
            yg = jnp.dot(dlt, pw_ref[g].astype(BF16), preferred_element_type=F32) * ps_ref[:, cols]
            mix_ref[:, ATT_WIDTH + g * POOL_GROUP:ATT_WIDTH + (g + 1) * POOL_GROUP] = yg.astype(BF16)
        ubuf[0:HALO, :] = ubuf[TM:TM + HALO, :]
        h1_ref[...] = x_ref[...] + jnp.dot(mix_ref[...], w_ref[...], preferred_element_type=F32)
        _gather2_end(plan, i, nsteps)

    tok = lambda w: pl.BlockSpec((TM, w), lambda i: (i, 0))
    res = pl.pallas_call(
        body, name="mix_out", grid=(nsteps,),
        in_specs=[tok(D_MODEL), tok(ATT_WIDTH), tok(POOL_WIDTH), _const(pool_w.shape), _const((1, POOL_WIDTH)),
                  _const(w_out.shape)] + [ANY_SPEC] * ns,
        out_specs=[tok(D_MODEL), tok(D_MODEL), tok(POOL_WIDTH)] + [ANY_SPEC] * ns,
        out_shape=[jax.ShapeDtypeStruct((S, D_MODEL), F32), jax.ShapeDtypeStruct((S, D_MODEL), BF16),
                   jax.ShapeDtypeStruct((S, POOL_WIDTH), BF16)] + _exchange_shapes(shards, (False,) * ns),
        scratch_shapes=[pltpu.VMEM((TM + HALO, POOL_WIDTH), F32)] + _exchange_sems(ns),
        compiler_params=_params("arbitrary"),
    )(x, att, u, pool_w, pool_scale, w_out, *shards)
    return res[0], res[1], res[2], res[3:]


def _conv_fwd(stage, upre, prev, cw, cb):
    T = upre.shape[0]
    stage[0:HALO, :] = prev
    stage[HALO:HALO + T, :] = upre
    return cb + cw[0:1, :] * stage[HALO - 2:HALO - 2 + T, :] + cw[1:2, :] * stage[HALO - 1:HALO - 1 + T, :] + cw[2:3, :] * upre


def _ffn_fwd(h1, g2, w_up, conv_w, conv_b, w_down, shards):
    S = h1.shape[0]
    T = TM_FF
    ns = len(shards)
    nsteps = S // T

    def body(h1_ref, g_ref, wu_ref, cw_ref, cb_ref, wd_ref, *rest):
        h2_ref, hn_ref, up_ref = rest[ns:ns + 3]
        carry, stage = rest[2 * ns + 3:2 * ns + 5]
        i = pl.program_id(0)
        plan = _gather2_copies(rest[:ns], rest[ns + 3:2 * ns + 3], *rest[2 * ns + 5:])
        _gather2_begin(plan, i, nsteps // 2)

        @pl.when(i == 0)
        def _():
            carry[...] = jnp.zeros(carry.shape, F32)

        h1t = h1_ref[...]
        r, n = _rms(h1t)
        hn = (n * g_ref[...]).astype(BF16)
        hn_ref[...] = hn
        acc = h1t
        for j in range(4):
            conv = []
            for jj in (j, j + 4):
                upre = jnp.dot(hn, wu_ref[jj], preferred_element_type=F32)
                up_ref[jj] = upre.astype(BF16)
                conv.append(_conv_fwd(stage, upre, carry[jj], cw_ref[jj], cb_ref[jj]))
                carry[jj] = stage[T:T + HALO, :]
            gate, val = conv
            a = gate * jax.nn.sigmoid(gate) * val
            acc = acc + jnp.dot(a.astype(BF16), wd_ref[j], preferred_element_type=F32)
        h2_ref[...] = acc
        _gather2_end(plan, i, nsteps)

    tok = lambda w: pl.BlockSpec((T, w), lambda i: (i, 0))
    res = pl.pallas_call(
        body, name="ffn_fwd", grid=(nsteps,),
        in_specs=[tok(D_MODEL), _const((1, D_MODEL)), _const(w_up.shape), _const(conv_w.shape), _const(conv_b.shape),
                  _const(w_down.shape)] + [ANY_SPEC] * ns,
        out_specs=[tok(D_MODEL), tok(D_MODEL), pl.BlockSpec((N_DEV, T, FF_SHARD), lambda i: (0, i, 0))]
        + [ANY_SPEC] * ns,
        out_shape=[jax.ShapeDtypeStruct((S, D_MODEL), F32), jax.ShapeDtypeStruct((S, D_MODEL), BF16),
                   jax.ShapeDtypeStruct((N_DEV, S, FF_SHARD), BF16)] + _exchange_shapes(shards, (False,) * ns),
        scratch_shapes=[pltpu.VMEM((N_DEV, HALO, FF_SHARD), F32), pltpu.VMEM((T + HALO, FF_SHARD), F32)]
        + _exchange_sems(ns),
        compiler_params=_params("arbitrary"),
    )(h1, g2, w_up, conv_w, conv_b, w_down, *shards)
    return res[0], res[1], res[2], res[3:]


def _head(h2, p, g3, w_pg, w_ple, g4, target):
    S = h2.shape[0]
    nt = S // TM

    def body(h2_ref, p_ref, g3_ref, wpg_ref, wple_ref, g4_ref, t_ref,
             loss_ref, dh2_ref, dh2b_ref, hn3_ref, dgl_ref, dpe_ref, dg3_ref, dg4_ref, lacc):
        i = pl.program_id(0)

        @pl.when(i == 0)
        def _():
            lacc[...] = jnp.zeros(lacc.shape, F32)
            dg3_ref[...] = jnp.zeros(dg3_ref.shape, F32)
            dg4_ref[...] = jnp.zeros(dg4_ref.shape, F32)

        h2t = h2_ref[...]
        g3, g4 = g3_ref[...], g4_ref[...]
        r3, n3 = _rms(h2t)
        hn3 = (n3 * g3).astype(BF16)
        hn3_ref[...] = hn3
        gs = jax.nn.sigmoid(jnp.dot(hn3, wpg_ref[...], preferred_element_type=F32))
        pe = jnp.dot(p_ref[...].astype(BF16), wple_ref[...], preferred_element_type=F32)
        h3 = h2t + gs * pe
        r4, n4 = _rms(h3)
        err = n4 * g4 - t_ref[...]
        lacc[...] += _colsum(err * err)
        dy = err * (1.0 / D_MODEL)
        dg4_ref[...] += _colsum(dy * n4)
        dh3 = _rms_bwd(r4, n4, g4, dy)
        dpe_ref[...] = (dh3 * gs).astype(BF16)
        dgl = (dh3 * pe * gs * (1.0 - gs)).astype(BF16)
        dgl_ref[...] = dgl
        dhn3 = lax.dot_general(dgl, wpg_ref[...], NT, preferred_element_type=F32)
        dg3_ref[...] += _colsum(dhn3 * n3)
        dh2 = dh3 + _rms_bwd(r3, n3, g3, dhn3)
        dh2_ref[...] = dh2
        dh2b_ref[...] = dh2.astype(BF16)

        @pl.when(i == nt - 1)
        def _():
            tot = 0.5 / D_MODEL * jnp.sum(lacc[...], axis=-1, keepdims=True)
            loss_ref[...] = jnp.broadcast_to(tot, loss_ref.shape)

    tok = lambda w: pl.BlockSpec((TM, w), lambda i: (i, 0))
    row = pl.BlockSpec((1, D_MODEL), lambda i: (0, 0))
    act = lambda dt: jax.ShapeDtypeStruct((S, D_MODEL), dt)
    return pl.pallas_call(
        body, name="head", grid=(nt,),
        in_specs=[tok(D_MODEL), tok(PLE_DIM), _const((1, D_MODEL)), _const(w_pg.shape), _const(w_ple.shape),
                  _const((1, D_MODEL)), tok(D_MODEL)],
        out_specs=[pl.BlockSpec((8, LANES), lambda i: (0, 0)), tok(D_MODEL), tok(D_MODEL), tok(D_MODEL), tok(D_MODEL),
                   tok(D_MODEL), row, row],
        out_shape=[jax.ShapeDtypeStruct((8, LANES), F32), act(F32), act(BF16), act(BF16), act(BF16), act(BF16),
                   jax.ShapeDtypeStruct((1, D_MODEL), F32), jax.ShapeDtypeStruct((1, D_MODEL), F32)],
        scratch_shapes=[pltpu.VMEM((1, D_MODEL), F32)],
        compiler_params=_params("arbitrary"),
    )(h2, p, g3, w_pg, w_ple, g4, target)


def _wgrad(name, x, dy, x_kind, dy_kind, nj, k_dim, n_dim):
    S = x.shape[-2]
    nt = S // TK

    def spec(kind, width):
        if kind == "full":
            return pl.BlockSpec((TK, width), lambda j, t: (t, 0))
        return pl.BlockSpec((None, TK, width), lambda j, t: (j, t, 0))

    def body(x_ref, dy_ref, o_ref, acc):
        t = pl.program_id(1)

        @pl.when(t == 0)
        def _():
            acc[...] = jnp.zeros(acc.shape, F32)

        acc[...] += lax.dot_general(x_ref[...].astype(BF16), dy_ref[...], TN, preferred_element_type=F32)

        @pl.when(t == nt - 1)
        def _():
            o_ref[...] = acc[...].astype(BF16)

    return pl.pallas_call(
        body, name=name, grid=(nj, nt),
        in_specs=[spec(x_kind, k_dim), spec(dy_kind, n_dim)],
        out_specs=pl.BlockSpec((None, k_dim, n_dim), lambda j, t: (j, 0, 0)),
        out_shape=jax.ShapeDtypeStruct((nj, k_dim, n_dim), BF16),
        scratch_shapes=[pltpu.VMEM((k_dim, n_dim), F32)],
        compiler_params=_params("arbitrary", "arbitrary"),
    )(x, dy)


def _row_picker(T, off0, off1):
    r = lax.broadcasted_iota(jnp.int32, (2 * T, T + HALO), 0)
    c = lax.broadcasted_iota(jnp.int32, (2 * T, T + HALO), 1)
    want = jnp.where(r < T, r + off0, r - T + off1)
    return jnp.where(c == want, 1.0, 0.0).astype(BF16)


def _ffn_bwd_a(dh2b, up, w_down, conv_w, conv_b, grads):
    S = dh2b.shape[0]
    T = TM_FF
    hb = T // HALO
    nsteps = S // T
    ng = len(grads)

    def body(dh_ref, up_ref, halo_ref, wd_ref, cw_ref, cb_ref, *rest):
        a_ref, dup_ref, dcw_ref, dcb_ref = rest[ng:ng + 4]
        stage = rest[2 * ng + 4]
        i = pl.program_id(0)
        copies = _exchange_copies(rest[:ng], rest[ng + 4:2 * ng + 4], (True,) * ng, *rest[2 * ng + 5:])

        @pl.when(i == 0)
        def _():
            dcw_ref[...] = jnp.zeros(dcw_ref.shape, F32)
            dcb_ref[...] = jnp.zeros(dcb_ref.shape, F32)
            for cp in copies:
                cp.start()

        @pl.when(i == nsteps - 1)
        def _():
            for cp in copies:
                cp.wait()

        dh = dh_ref[...]
        pick = _row_picker(T, HALO - 2, HALO - 1)
        for j in range(4):
            da = lax.dot_general(dh, wd_ref[j], NT, preferred_element_type=F32)
            conv, taps = [], []
            for jj in (j, j + 4):
                upre = up_ref[jj]
                stage[0:HALO, :] = jnp.where(i > 0, halo_ref[jj], jnp.zeros((HALO, FF_SHARD), BF16))
                stage[HALO:HALO + T, :] = upre
                prv = jnp.dot(pick, stage[...], preferred_element_type=F32)
                tp = (prv[0:T], prv[T:2 * T], upre.astype(F32))
                cw = cw_ref[jj]
                conv.append(cb_ref[jj] + cw[0:1, :] * tp[0] + cw[1:2, :] * tp[1] + cw[2:3, :] * tp[2])
                taps.append(tp)
            gate, val = conv
            sg = jax.nn.sigmoid(gate)
            silu = gate * sg
            a_ref[j] = (silu * val).astype(BF16)
            dgate = da * val * (sg * (1.0 + gate * (1.0 - sg)))
            dval = da * silu
            for jj, dup, tp in ((j, dgate, taps[0]), (j + 4, dval, taps[1])):
                dup_ref[jj] = dup.astype(BF16)
                dcb_ref[jj] += _colsum(dup)
                for kk in range(3):
                    dcw_ref[jj, kk:kk + 1, :] += _colsum(dup * tp[kk])

    tok = lambda w: pl.BlockSpec((T, w), lambda i: (i, 0))
    shard = pl.BlockSpec((N_DEV, T, FF_SHARD), lambda i: (0, i, 0))
    res = pl.pallas_call(
        body, name="ffn_bwd_a", grid=(nsteps,),
        in_specs=[tok(D_MODEL), shard,
                  pl.BlockSpec((N_DEV, HALO, FF_SHARD), lambda i: (0, jnp.maximum(i * hb - 1, 0), 0)),
                  _const(w_down.shape), _const(conv_w.shape), _const(conv_b.shape)] + [ANY_SPEC] * ng,
        out_specs=[pl.BlockSpec((4, T, FF_SHARD), lambda i: (0, i, 0)), shard,
                   pl.BlockSpec((N_DEV, 3, FF_SHARD), lambda i: (0, 0, 0)),
                   pl.BlockSpec((N_DEV, 1, FF_SHARD), lambda i: (0, 0, 0))] + [ANY_SPEC] * ng,
        out_shape=[jax.ShapeDtypeStruct((4, S, FF_SHARD), BF16), jax.ShapeDtypeStruct((N_DEV, S, FF_SHARD), BF16),
                   jax.ShapeDtypeStruct((N_DEV, 3, FF_SHARD), F32), jax.ShapeDtypeStruct((N_DEV, 1, FF_SHARD), F32)]
        + _exchange_shapes(grads, (True,) * ng),
        scratch_shapes=[pltpu.VMEM((T + HALO, FF_SHARD), BF16)] + _exchange_sems(ng),
        compiler_params=_params("arbitrary"),
    )(dh2b, up, up, w_down, conv_w, conv_b, *grads)
    return res[0], res[1], res[2], res[3], res[4:]


def _ffn_bwd_b(dup, conv_w, w_up, h1, g2, dh2, grads):
    S = h1.shape[0]
    T = TM_FF
    hb = T // HALO
    nt = S // T
    ng = len(grads)

    def body(dup_ref, halo_ref, cw_ref, wu_ref, h1_ref, g_ref, dh2_ref, *rest):
        dpre_ref, dh1_ref, dh1b_ref, dg_ref = rest[ng:ng + 4]
        stage = rest[2 * ng + 4]
        i = pl.program_id(0)
        copies = _exchange_copies(rest[:ng], rest[ng + 4:2 * ng + 4], (True,) * ng, *rest[2 * ng + 5:])

        @pl.when(i == 0)
        def _():
            dg_ref[...] = jnp.zeros(dg_ref.shape, F32)
            for cp in copies:
                cp.start()

        dhn = jnp.zeros((T, D_MODEL), F32)
        for jj in range(N_DEV):
            dup = dup_ref[jj].astype(F32)
            stage[0:T, :] = dup
            stage[T:T + HALO, :] = jnp.where(i < nt - 1, halo_ref[jj].astype(F32), 0.0)
            cw = cw_ref[jj]
            dpre = (cw[2:3, :] * dup + cw[1:2, :] * stage[1:1 + T, :] + cw[0:1, :] * stage[2:2 + T, :]).astype(BF16)
            dpre_ref[jj] = dpre
            dhn = dhn + lax.dot_general(dpre, wu_ref[jj], NT, preferred_element_type=F32)
        g = g_ref[...]
        r, n = _rms(h1_ref[...])
        dg_ref[...] += _colsum(dhn * n)
        dh1 = dh2_ref[...] + _rms_bwd(r, n, g, dhn)
        dh1_ref[...] = dh1
        dh1b_ref[...] = dh1.astype(BF16)

        @pl.when(i == nt - 1)
        def _():
            for cp in copies:
                cp.wait()

    tok = lambda w: pl.BlockSpec((T, w), lambda i: (i, 0))
    shard = pl.BlockSpec((N_DEV, T, FF_SHARD), lambda i: (0, i, 0))
    res = pl.pallas_call(
        body, name="ffn_bwd_b", grid=(nt,),
        in_specs=[shard,
                  pl.BlockSpec((N_DEV, HALO, FF_SHARD), lambda i: (0, jnp.minimum((i + 1) * hb, S // HALO - 1), 0)),
                  _const(conv_w.shape), _const(w_up.shape), tok(D_MODEL), _const((1, D_MODEL)), tok(D_MODEL)]
        + [ANY_SPEC] * ng,
        out_specs=[shard, tok(D_MODEL), tok(D_MODEL), pl.BlockSpec((1, D_MODEL), lambda i: (0, 0))] + [ANY_SPEC] * ng,
        out_shape=[jax.ShapeDtypeStruct((N_DEV, S, FF_SHARD), BF16), jax.ShapeDtypeStruct((S, D_MODEL), F32),
                   jax.ShapeDtypeStruct((S, D_MODEL), BF16), jax.ShapeDtypeStruct((1, D_MODEL), F32)]
        + _exchange_shapes(grads, (True,) * ng),
        scratch_shapes=[pltpu.VMEM((T + HALO, FF_SHARD), F32)] + _exchange_sems(ng),
        compiler_params=_params("arbitrary"),
    )(dup, dup, conv_w, w_up, h1, g2, dh2, *grads)
    return res[0], res[1], res[2], res[3], res[4:]


def _mix_bwd(dh1b, w_out, dlt, pool_w, pool_scale):
    S = dh1b.shape[0]
    nt = S // TM

    def body(dh_ref, w_ref, dlt_ref, pw_ref, ps_ref, datt_ref, du_ref, dpw_ref, dps_ref, stage, carry):
        i = pl.program_id(0)
        tile = nt - 1 - i

        @pl.when(i == 0)
        def _():
            dpw_ref[...] = jnp.zeros(dpw_ref.shape, F32)
            dps_ref[...] = jnp.zeros(dps_ref.shape, F32)
            carry[...] = jnp.zeros(carry.shape, F32)

        dmix = lax.dot_general(dh_ref[...], w_ref[...], NT, preferred_element_type=F32)
        datt_ref[...] = dmix[:, 0:ATT_WIDTH]
        for g, w in enumerate(POOL_WINDOWS):
            cols = slice(g * POOL_GROUP, (g + 1) * POOL_GROUP)
            dpool = dmix[:, ATT_WIDTH + g * POOL_GROUP:ATT_WIDTH + (g + 1) * POOL_GROUP]
            dl = dlt_ref[:, cols]
            pw = pw_ref[g].astype(BF16)
            yg = jnp.dot(dl, pw, preferred_element_type=F32)
            dps_ref[:, cols] += _colsum(dpool * yg)
            dy = (dpool * ps_ref[:, cols]).astype(BF16)
            dpw_ref[g] += lax.dot_general(dl, dy, TN, preferred_element_type=F32)
            ddlt = lax.dot_general(dy, pw, NT, preferred_element_type=F32)
            cg = ddlt / _pool_count(tile, w)
            stage[0:TM, :] = cg
            stage[TM:TM + HALO, :] = carry[:, cols]
            acc = cg
            for j in range(1, w):
                acc = acc + stage[j:j + TM, :]
            du_ref[:, cols] = acc - ddlt
            carry[:, cols] = cg[0:HALO, :]

    tok = lambda w: pl.BlockSpec((TM, w), lambda i: (nt - 1 - i, 0))
    return pl.pallas_call(
        body, name="mix_bwd", grid=(nt,),
        in_specs=[tok(D_MODEL), _const(w_out.shape), tok(POOL_WIDTH), _const(pool_w.shape), _const((1, POOL_WIDTH))],
        out_specs=[tok(ATT_WIDTH), tok(POOL_WIDTH), pl.BlockSpec(pool_w.shape, lambda i: (0, 0, 0)),
                   pl.BlockSpec((1, POOL_WIDTH), lambda i: (0, 0))],
        out_shape=[jax.ShapeDtypeStruct((S, ATT_WIDTH), F32), jax.ShapeDtypeStruct((S, POOL_WIDTH), F32),
                   jax.ShapeDtypeStruct(pool_w.shape, F32), jax.ShapeDtypeStruct((1, POOL_WIDTH), F32)],
        scratch_shapes=[pltpu.VMEM((TM + HALO, POOL_GROUP), F32), pltpu.VMEM((HALO, POOL_WIDTH), F32)],
        compiler_params=_params("arbitrary"),
    )(dh1b, w_out, dlt, pool_w, pool_scale)


def _attn_bwd(slopes, q, k, v, o, lse, do, grads, scatter):
    S = q.shape[0]
    CH = 512
    ng = len(grads)
    steps = ATT_WIDTH // LANES

    def body(slope_ref, q_ref, k_ref, v_ref, o_ref, lse_ref, do_ref, *rest):
        dq_ref, dk_ref, dv_ref = rest[ng:ng + 3]
        dl_s, bias_s = rest[2 * ng + 3:2 * ng + 5]
        hp = pl.program_id(0)
        copies = _exchange_copies(rest[:ng], rest[ng + 3:2 * ng + 3], scatter, *rest[2 * ng + 5:])

        @pl.when(hp == 0)
        def _():
            for cp in copies:
                cp.start()

        is0 = lax.broadcasted_iota(jnp.int32, (SPAN, LANES), 1) < HEAD_DIM
        is0c = lax.broadcasted_iota(jnp.int32, (CH, LANES), 1) < HEAD_DIM

        def prep(ci, carry):
            rows = pl.ds(pl.multiple_of(ci * CH, CH), CH)
            prod = do_ref[rows, :] * o_ref[rows, :]
            d0 = jnp.sum(jnp.where(is0c, prod, 0.0), axis=-1, keepdims=True)
            d1 = jnp.sum(jnp.where(is0c, 0.0, prod), axis=-1, keepdims=True)
            dl_s[rows, :] = jnp.where(is0c, d0, d1)
            zero = jnp.zeros((CH, LANES), F32)
            dq_ref[rows, :] = zero
            dk_ref[rows, :] = zero
            dv_ref[rows, :] = zero
            return carry

        lax.fori_loop(0, S // CH, prep, 0)

        for d in DILATIONS:
            nb, ngroups = _attn_groups(S, d)
            _attn_fill_bias(bias_s, slope_ref, hp, d)

            def group(i, carry, d=d, nb=nb):
                blocks = [_attn_block(i, g, d, nb) for g in range(ATTN_GROUP)]
                loaded = [(q_ref[rows, :], do_ref[rows, :], lse_ref[rows, :], dl_s[rows, :], k_ref[krows, :],
                           v_ref[krows, :].astype(BF16)) for rows, krows, _ in blocks]
                new = []
                for (rows, krows, tab), (qb, dob, lse_b, dl_b, kf, vb) in zip(blocks, loaded):
                    kb = kf.astype(BF16)
                    qs = _stack_heads(qb, is0).astype(BF16)
                    dos = _stack_heads(dob, is0).astype(BF16)
                    lse_s = jnp.concatenate([lse_b[:, 0:1], lse_b[:, HEAD_DIM:HEAD_DIM + 1]], axis=0)
                    dl_s2 = jnp.concatenate([dl_b[:, 0:1], dl_b[:, HEAD_DIM:HEAD_DIM + 1]], axis=0)
                    s = lax.dot_general(qs, kb, NT, preferred_element_type=F32) + bias_s[tab]
                    pr = jnp.exp(s - lse_s)
                    dp = lax.dot_general(dos, vb, NT, preferred_element_type=F32)
                    ds = (pr * (dp - dl_s2)).astype(BF16)
                    dv_c = lax.dot_general(pr.astype(BF16), dos, TN, preferred_element_type=F32)
                    dk_c = lax.dot_general(ds, qs, TN, preferred_element_type=F32)
                    dq_c = _unstack_heads(jnp.dot(ds, kb, preferred_element_type=F32), is0)
                    new.append((dq_c, dk_c, dv_c))
                old = [(dq_ref[rows, :], dk_ref[krows, :], dv_ref[krows, :]) for rows, krows, _ in blocks]
                for (rows, krows, _), (dq_c, dk_c, dv_c), (dq_o, dk_o, dv_o) in zip(blocks, new, old):
                    dq_ref[rows, :] = dq_o + dq_c
                    dk_ref[krows, :] = dk_o + dk_c
                    dv_ref[krows, :] = dv_o + dv_c
                return carry

            lax.fori_loop(0, ngroups, group, 0)

        @pl.when(hp == steps - 1)
        def _():
            for cp in copies:
                cp.wait()

    col = pl.BlockSpec((S, LANES), lambda i: (0, i))
    res = pl.pallas_call(
        body, name="attn_bwd", grid=(steps,),
        in_specs=[pl.BlockSpec(memory_space=pltpu.SMEM)] + [col] * 6 + [ANY_SPEC] * ng,
        out_specs=[col] * 3 + [ANY_SPEC] * ng,
        out_shape=[jax.ShapeDtypeStruct((S, ATT_WIDTH), F32)] * 3 + _exchange_shapes(grads, scatter),
        scratch_shapes=[pltpu.VMEM((S, LANES), F32), pltpu.VMEM((2, 2 * SPAN, 2 * SPAN), F32)] + _exchange_sems(ng),
        compiler_params=_params("arbitrary"),
    )(slopes, q, k, v, o, lse, do, *grads)
    return res[0], res[1], res[2], res[3:]


def _in_bwd(dq, dk, dv, du, w_in, x, g1, dh1):
    S = x.shape[0]

    def body(dq_ref, dk_ref, dv_ref, du_ref, w_ref, x_ref, g_ref, dh1_ref, dz_ref, dx_ref, dg_ref):
        @pl.when(pl.program_id(0) == 0)
        def _():
            dg_ref[...] = jnp.zeros(dg_ref.shape, F32)

        srcs = (dq_ref, dk_ref, dv_ref, du_ref)
        dhn = jnp.zeros((TM, D_MODEL), F32)
        for j in range(N_DEV):
            dz = srcs[j // 2][:, (j % 2) * 256:(j % 2 + 1) * 256]
            if j < 2:
                dz = dz * (HEAD_DIM ** -0.5)
            dz = dz.astype(BF16)
            dz_ref[j] = dz
            dhn = dhn + lax.dot_general(dz, w_ref[j], NT, preferred_element_type=F32)
        g = g_ref[...]
        r, n = _rms(x_ref[...])
        dg_ref[...] += _colsum(dhn * n)
        dx_ref[...] = dh1_ref[...] + _rms_bwd(r, n, g, dhn)

    tok = lambda w: pl.BlockSpec((TM, w), lambda i: (i, 0))
    return pl.pallas_call(
        body, name="in_bwd", grid=(S // TM,),
        in_specs=[tok(512)] * 4 + [_const(w_in.shape), tok(D_MODEL), _const((1, D_MODEL)), tok(D_MODEL)],
        out_specs=[pl.BlockSpec((N_DEV, TM, 256), lambda i: (0, i, 0)), tok(D_MODEL),
                   pl.BlockSpec((1, D_MODEL), lambda i: (0, 0))],
        out_shape=[jax.ShapeDtypeStruct((N_DEV, S, 256), BF16), jax.ShapeDtypeStruct((S, D_MODEL), F32),
                   jax.ShapeDtypeStruct((1, D_MODEL), F32)],
        compiler_params=_params("arbitrary"),
    )(dq, dk, dv, du, w_in, x, g1, dh1)


def _adamw(name, parts, w, m, v):
    R, C = w.shape
    rb = R
    for cand in (256, 128, 64, 32, 16, 8):
        if R % cand == 0 and R > cand:
            rb = cand
            break

    def body(p_ref, w_ref, m_ref, v_ref, g_ref, d_ref, mo_ref, vo_ref):
        g = p_ref[0].astype(F32)
        for s in range(1, N_DEV):
            g = g + p_ref[s].astype(F32)
        g_ref[...] = g
        d_ref[...], mo_ref[...], vo_ref[...] = _adam_update(g, w_ref[...], m_ref[...], v_ref[...])

    blk = pl.BlockSpec((rb, C), lambda i: (i, 0))
    return pl.pallas_call(
        body, name=name, grid=(R // rb,),
        in_specs=[pl.BlockSpec((N_DEV, rb, C), lambda i: (0, i, 0)), blk, blk, blk],
        out_specs=[blk] * 4,
        out_shape=[jax.ShapeDtypeStruct((R, C), F32)] * 4,
        compiler_params=_params("arbitrary"),
    )(parts, w, m, v)


def _adam_update(g, w, m, v):
    m_new = ADAM_B1 * m + (1.0 - ADAM_B1) * g
    v_new = ADAM_B2 * v + (1.0 - ADAM_B2) * (g * g)
    m_hat = m_new / (1.0 - ADAM_B1 ** ADAM_STEP)
    v_hat = v_new / (1.0 - ADAM_B2 ** ADAM_STEP)
    return -ADAM_LR * (m_hat / (jnp.sqrt(v_hat) + ADAM_EPS) + ADAM_WD * w), m_new, v_new


def _adamw_small(parts, loss_parts, ws, ms, vs):
    n = len(ws)

    def body(*refs):
        p_refs, lp_ref = refs[:n], refs[n]
        w_refs, m_refs, v_refs = refs[n + 1:2 * n + 1], refs[2 * n + 1:3 * n + 1], refs[3 * n + 1:4 * n + 1]
        outs = refs[4 * n + 1:]
        for i in range(n):
            g = p_refs[i][0]
            for s in range(1, N_DEV):
                g = g + p_refs[i][s]
            d, m_new, v_new = _adam_update(g, w_refs[i][...], m_refs[i][...], v_refs[i][...])
            outs[i][...] = g
            outs[n + i][...] = d
            outs[2 * n + i][...] = m_new
            outs[3 * n + i][...] = v_new
        tot = lp_ref[0]
        for s in range(1, N_DEV):
            tot = tot + lp_ref[s]
        outs[4 * n][...] = tot

    shapes = [jax.ShapeDtypeStruct(w.shape, F32) for w in ws]
    res = pl.pallas_call(
        body, name="adamw_replicated",
        out_shape=shapes * 4 + [jax.ShapeDtypeStruct(loss_parts.shape[1:], F32)],
        compiler_params=_params(),
    )(*parts, loss_parts, *ws, *ms, *vs)
    return res[:n], res[n:2 * n], res[2 * n:3 * n], res[3 * n:4 * n], res[4 * n]


def _gather2(name, arrays):
    n = len(arrays)

    def body(*refs):
        first, passed, last = _gather2_copies(refs[:n], refs[n:2 * n], *refs[2 * n:])
        for cp in first:
            cp.start()
        for arrival, cp in passed:
            arrival.wait_recv()
            cp.start()
        for wait in last:
            wait()

    return pl.pallas_call(
        body, name=name,
        in_specs=[ANY_SPEC] * n, out_specs=[ANY_SPEC] * n, out_shape=_exchange_shapes(arrays, (False,) * n),
        scratch_shapes=_exchange_sems(n),
    )(*arrays)


def _dw_in_exchange(hn, dz, small):
    S = hn.shape[0]
    nt = S // TK
    ns = len(small)
    kd, nd = hn.shape[1], dz.shape[2]
    me_arr = (4 * lax.axis_index("x") + 2 * lax.axis_index("y") + lax.axis_index("c")).astype(jnp.int32).reshape(1)

    def body(me_ref, x_ref, dy_ref, *rest):
        recv_ref = rest[ns]
        acc, stage, send_sems, recv_sems, own_sem = rest[2 * ns + 1:2 * ns + 6]
        j, t = pl.program_id(0), pl.program_id(1)
        x, y, c = lax.axis_index("x"), lax.axis_index("y"), lax.axis_index("c")
        me = 4 * x + 2 * y + c
        small_copies = _exchange_copies(rest[:ns], rest[ns + 1:2 * ns + 1], (False,) * ns, *rest[2 * ns + 6:])

        @pl.when((j == 0) & (t == 0))
        def _():
            for cp in small_copies:
                cp.start()

        @pl.when(t == 0)
        def _():
            acc[...] = jnp.zeros(acc.shape, F32)

        acc[...] += lax.dot_general(x_ref[...], dy_ref[...], TN, preferred_element_type=F32)

        def to_owner(k, owner):
            return pltpu.make_async_remote_copy(
                src_ref=stage.at[owner], dst_ref=recv_ref.at[me], send_sem=send_sems.at[k], recv_sem=recv_sems.at[k],
                device_id=(owner // 4, (owner // 2) % 2, owner % 2), device_id_type=MESH)

        own = pltpu.make_async_copy(stage.at[me], recv_ref.at[me], own_sem)

        @pl.when(t == nt - 1)
        def _():
            owner = (me + 1 + j) % N_DEV
            stage[owner] = acc[...].astype(BF16)

            @pl.when(j < N_DEV - 1)
            def _():
                to_owner(j, owner).start()

            @pl.when(j == N_DEV - 1)
            def _():
                own.start()
                own.wait()
                for k in range(N_DEV - 1):
                    to_owner(k, me).wait_send()
                    to_owner(k, me).wait_recv()
                for cp in small_copies:
                    cp.wait()

    slab = lambda j, me_ref: (me_ref[0] + 1 + j) % N_DEV
    grid_spec = pltpu.PrefetchScalarGridSpec(
        num_scalar_prefetch=1, grid=(N_DEV, nt),
        in_specs=[pl.BlockSpec((TK, kd), lambda j, t, me_ref: (t, 0)),
                  pl.BlockSpec((None, TK, nd), lambda j, t, me_ref: (slab(j, me_ref), t, 0))] + [ANY_SPEC] * ns,
        out_specs=[ANY_SPEC] * (ns + 1),
        scratch_shapes=[pltpu.VMEM((kd, nd), F32), pltpu.VMEM((N_DEV, kd, nd), BF16),
                        pltpu.SemaphoreType.DMA((N_DEV - 1,)), pltpu.SemaphoreType.DMA((N_DEV - 1,)),
                        pltpu.SemaphoreType.DMA] + _exchange_sems(ns))
    res = pl.pallas_call(
        body, name="dw_in_exchange", grid_spec=grid_spec,
        out_shape=[jax.ShapeDtypeStruct((N_DEV, kd, nd), BF16)] + _exchange_shapes(small, (False,) * ns),
        compiler_params=_params("arbitrary", "arbitrary"),
    )(me_arr, hn, dz, *small)
    return res[0], res[1:]


def kernel(x, p, ln_mix, w_in, pool_w, pool_scale, w_out, ln_ffn, w_up, conv_w, conv_b, w_down, ln_ple, w_ple_gate, w_ple, ln_final, loss_target, m_ln_mix, m_w_in, m_pool_w, m_pool_scale, m_w_out, m_ln_ffn, m_w_up, m_conv_w, m_conv_b, m_w_down, m_ln_ple, m_w_ple_gate, m_w_ple, m_ln_final, v_ln_mix, v_w_in, v_pool_w, v_pool_scale, v_w_out, v_ln_ffn, v_w_up, v_conv_w, v_conv_b, v_w_down, v_ln_ple, v_w_ple_gate, v_w_ple, v_ln_final):
    xs, ps, tgt, pool_w0 = x[0], p[0, 0], loss_target[0], pool_w[0]
    slopes = jnp.exp2(-8.0 * (jnp.arange(N_HEADS, dtype=F32) + 1.0) / N_HEADS)
    conv_b_s = conv_b.reshape(N_DEV, 1, FF_SHARD)

    (w_in_g,) = _gather2("gather_w_in", [w_in[0].astype(BF16)])
    (q, k, v, u, hn1), (w_out_g,) = _qkvu(xs, ln_mix, w_in_g, [w_out[0].astype(BF16)])
    att, lse, (w_up_g, conv_w_g) = _attn_fwd(slopes, q, k, v, [w_up[0].astype(BF16), conv_w[0]])
    w_out_f = w_out_g.reshape(D_MODEL, D_MODEL)
    h1, mix, dlt, (w_down_g,) = _mix_out(xs, att, u, pool_w0, pool_scale, w_out_f, [w_down[0].astype(BF16)])
    w_down_f = w_down_g.reshape(4, FF_SHARD, D_MODEL)
    h2, hn2, up, (w_pg_g, w_ple_g) = _ffn_fwd(h1, ln_ffn, w_up_g, conv_w_g, conv_b_s, w_down_f,
                                              [w_ple_gate[0].astype(BF16), w_ple[0].astype(BF16)])
    w_pg_f = w_pg_g.reshape(D_MODEL, D_MODEL)
    w_ple_f = jnp.transpose(w_ple_g, (1, 0, 2)).reshape(PLE_DIM, D_MODEL)
    loss_blk, dh2, dh2b, hn3, dgl, dpe, d_ln_ple, d_ln_final = _head(
        h2, ps, ln_ple, w_pg_f, w_ple_f, ln_final.reshape(1, D_MODEL), tgt)

    d_w_pg = _wgrad("dw_ple_gate", hn3, dgl, "full", "full", 1, D_MODEL, D_MODEL).reshape(N_DEV, D_MODEL // N_DEV, D_MODEL)
    d_w_ple = _wgrad("dw_ple", ps, dpe, "full", "full", 1, PLE_DIM, D_MODEL)
    d_w_ple = jnp.transpose(d_w_ple.reshape(PLE_DIM, N_DEV, LANES), (1, 0, 2))
    a, dup, d_conv_w, d_conv_b, (r_w_pg, r_w_ple) = _ffn_bwd_a(dh2b, up, w_down_f, conv_w_g, conv_b_s, [d_w_pg, d_w_ple])
    d_w_down = _wgrad("dw_down", a, dh2b, "lead", "full", 4, FF_SHARD, D_MODEL).reshape(N_DEV, D_FF // N_DEV, D_MODEL)
    dpre, dh1, dh1b, d_ln_ffn, (r_conv_w, r_w_down) = _ffn_bwd_b(
        dup, conv_w_g, w_up_g, h1, ln_ffn, dh2, [d_conv_w, d_w_down])
    datt, du, d_pool_w, d_pool_scale = _mix_bwd(dh1b, w_out_f, dlt, pool_w0, pool_scale)
    d_w_out = _wgrad("dw_out", mix, dh1b, "full", "full", 1, D_MODEL, D_MODEL).reshape(N_DEV, D_MODEL // N_DEV, D_MODEL)
    d_w_up = _wgrad("dw_up", dpre, hn2, "lead", "full", N_DEV, FF_SHARD, D_MODEL)
    rep_late = [d_pool_w, d_pool_scale, d_ln_ffn, d_conv_b.reshape(1, 2 * D_FF), d_ln_ple, d_ln_final, loss_blk]
    dq, dk, dv, received = _attn_bwd(slopes, q, k, v, att, lse, datt, [d_w_out, d_w_up] + rep_late,
                                     (True, True) + (False,) * len(rep_late))
    r_w_out, r_w_up, r_rep = received[0], received[1], list(received[2:])
    dz, grad_x, d_ln_mix = _in_bwd(dq, dk, dv, du, w_in_g, xs, ln_mix, dh1)

    rep_names = ("ln_mix", "pool_w", "pool_scale", "ln_ffn", "conv_b", "ln_ple", "ln_final")
    rep_w = [ln_mix, pool_w0, pool_scale, ln_ffn, conv_b, ln_ple, ln_final.reshape(1, D_MODEL)]
    rep_m = [m_ln_mix, m_pool_w[0], m_pool_scale, m_ln_ffn, m_conv_b, m_ln_ple, m_ln_final.reshape(1, D_MODEL)]
    rep_v = [v_ln_mix, v_pool_w[0], v_pool_scale, v_ln_ffn, v_conv_b, v_ln_ple, v_ln_final.reshape(1, D_MODEL)]
    r_w_in, (r_ln_mix,) = _dw_in_exchange(hn1, dz, [d_ln_mix])
    small = _adamw_small([r_ln_mix] + r_rep[:-1], r_rep[-1], rep_w, rep_m, rep_v)
    loss = small[4][0, 0]

    sharded = {}
    sharded["w_in"] = _adamw("adamw_w_in", r_w_in, w_in[0], m_w_in[0], v_w_in[0])
    sharded["w_out"] = _adamw("adamw_w_out", r_w_out, w_out[0], m_w_out[0], v_w_out[0])
    sharded["w_up"] = [t.T for t in _adamw("adamw_w_up", r_w_up, w_up[0].T, m_w_up[0].T, v_w_up[0].T)]
    sharded["conv_w"] = _adamw("adamw_conv_w", r_conv_w, conv_w[0], m_conv_w[0], v_conv_w[0])
    sharded["w_down"] = _adamw("adamw_w_down", r_w_down, w_down[0], m_w_down[0], v_w_down[0])
    sharded["w_ple_gate"] = _adamw("adamw_w_ple_gate", r_w_pg, w_ple_gate[0], m_w_ple_gate[0], v_w_ple_gate[0])
    sharded["w_ple"] = _adamw("adamw_w_ple", r_w_ple, w_ple[0], m_w_ple[0], v_w_ple[0])

    shapes = dict(w_in=w_in, w_out=w_out, w_up=w_up, conv_w=conv_w, w_down=w_down, w_ple_gate=w_ple_gate, w_ple=w_ple,
                  ln_mix=ln_mix, pool_w=pool_w, pool_scale=pool_scale, ln_ffn=ln_ffn, conv_b=conv_b, ln_ple=ln_ple,
                  ln_final=ln_final)

    def leaf(kind, n):
        src = sharded[n][kind] if n in sharded else small[kind][rep_names.index(n)]
        return src.reshape(shapes[n].shape)

    order = ("ln_mix", "w_in", "pool_w", "pool_scale", "w_out", "ln_ffn", "w_up", "conv_w", "conv_b", "w_down", "ln_ple",
             "w_ple_gate", "w_ple", "ln_final")
    outs = [loss, grad_x[None]]
    for kind in range(4):
        outs += [leaf(kind, n) for n in order]
    return tuple(outs)
```

```python
import jax
import jax.numpy as jnp
from jax import lax
from jax.experimental import pallas as pl
from jax.experimental.pallas import tpu as pltpu

F32 = jnp.float32
BF16 = jnp.bfloat16

N_DEV = 8
D_MODEL = 1024
ATT_WIDTH = 512
POOL_WIDTH = 512
N_HEADS = 8
HEAD_DIM = 64
SPAN = 128
DILATIONS = (1, 4, 16)
POOL_WINDOWS = (2, 4, 8, 16)
POOL_GROUP = 128
D_FF = 2816
FF_SHARD = 2 * D_FF // N_DEV
PLE_DIM = 256
EPS = 1e-6
NEG = -1e30

ADAM_LR = 0.001
ADAM_B1 = 0.9
ADAM_B2 = 0.999
ADAM_EPS = 1e-08
ADAM_WD = 0.01
ADAM_STEP = 10

LANES = 128
HALO = 16
TM = 512
TM_FF = 256
TK = 1024
ATTN_GROUP = 8
VMEM_LIMIT = 56 * 1024 * 1024

MESH = pl.DeviceIdType.MESH
NT = (((1,), (1,)), ((), ()))
TN = (((0,), (0,)), ((), ()))


def _params(*sem):
    return pltpu.CompilerParams(dimension_semantics=sem or None, vmem_limit_bytes=VMEM_LIMIT)


def _const(shape):
    n = len(shape)
    return pl.BlockSpec(shape, lambda *_: (0,) * n, pipeline_mode=pl.Buffered(1))


def _rms(h):
    r = lax.rsqrt(jnp.mean(h * h, axis=-1, keepdims=True) + EPS)
    return r, h * r


def _rms_bwd(r, n, g, dhn):
    dn = dhn * g
    return r * (dn - n * jnp.mean(dn * n, axis=-1, keepdims=True))


def _colsum(a):
    return jnp.sum(a, axis=0, keepdims=True)


def _gather2_copies(ins, outs, send_sems, recv_sems, local_sems):
    n = len(ins)
    x, y, c = lax.axis_index("x"), lax.axis_index("y"), lax.axis_index("c")
    slot = lambda px, py, pc: 4 * px + 2 * py + pc
    chips = [(x, 1 - y), (1 - x, y), (1 - x, 1 - y)]
    first, passed, last = [], [], []

    def remote(a, r, src, dst_slot, to):
        return pltpu.make_async_remote_copy(
            src_ref=src, dst_ref=outs[a].at[dst_slot],
            send_sem=send_sems.at[a * (N_DEV - 1) + r], recv_sem=recv_sems.at[a * (N_DEV - 1) + r],
            device_id=to, device_id_type=MESH)

    for a in range(n):
        mine = pltpu.make_async_copy(ins[a], outs[a].at[slot(x, y, c)], local_sems.at[a])
        to_sibling = remote(a, 0, ins[a], slot(x, y, c), (x, y, 1 - c))
        first += [mine, to_sibling]
        last += [mine.wait, to_sibling.wait_send, to_sibling.wait_recv]
        for r, (px, py) in enumerate(chips, start=1):
            to_chip = remote(a, r, ins[a], slot(x, y, c), (px, py, c))
            onward = remote(a, 3 + r, outs[a].at[slot(px, py, c)], slot(px, py, c), (x, y, 1 - c))
            first.append(to_chip)
            passed.append((to_chip, onward))
            last += [to_chip.wait_send, onward.wait_send, onward.wait_recv]
    return first, passed, last


def _gather2_begin(plan, step, pass_step):
    first, passed, _ = plan

    @pl.when(step == 0)
    def _():
        for cp in first:
            cp.start()

    @pl.when(step == pass_step)
    def _():
        for arrival, cp in passed:
            arrival.wait_recv()
            cp.start()


def _gather2_end(plan, step, nsteps):
    @pl.when(step == nsteps - 1)
    def _():
        for wait in plan[2]:
            wait()


ANY_SPEC = pl.BlockSpec(memory_space=pl.ANY)


def _exchange_shapes(arrays, scatter):
    out = []
    for a, s in zip(arrays, scatter):
        slab = a.shape[1:] if s else a.shape
        out.append(jax.ShapeDtypeStruct((N_DEV,) + tuple(slab), a.dtype))
    return out


def _exchange_sems(n):
    return [pltpu.SemaphoreType.DMA((n * (N_DEV - 1),)), pltpu.SemaphoreType.DMA((n * (N_DEV - 1),)),
            pltpu.SemaphoreType.DMA((n,))]


def _exchange_copies(ins, outs, scatter, send_sems, recv_sems, local_sems):
    n = len(ins)
    x, y, c = lax.axis_index("x"), lax.axis_index("y"), lax.axis_index("c")
    me = 4 * x + 2 * y + c
    copies = []
    for a in range(n):
        src = ins[a].at[me] if scatter[a] else ins[a]
        copies.append(pltpu.make_async_copy(src, outs[a].at[me], local_sems.at[a]))
    for k in range(1, N_DEV):
        px = 1 - x if k & 4 else x
        py = 1 - y if k & 2 else y
        pc = 1 - c if k & 1 else c
        pid = 4 * px + 2 * py + pc
        for a in range(n):
            src = ins[a].at[pid] if scatter[a] else ins[a]
            copies.append(pltpu.make_async_remote_copy(
                src_ref=src, dst_ref=outs[a].at[me],
                send_sem=send_sems.at[a * (N_DEV - 1) + k - 1], recv_sem=recv_sems.at[a * (N_DEV - 1) + k - 1],
                device_id=(px, py, pc), device_id_type=MESH))
    return copies


def _qkvu(x, g1, w_in, shards):
    S = x.shape[0]
    ns = len(shards)
    nsteps = S // TM

    def body(x_ref, g_ref, w_ref, *rest):
        q_ref, k_ref, v_ref, u_ref, hn_ref = rest[ns:ns + 5]
        plan = _gather2_copies(rest[:ns], rest[ns + 5:2 * ns + 5], *rest[2 * ns + 5:])
        _gather2_begin(plan, pl.program_id(0), nsteps - 2)
        r, n = _rms(x_ref[...])
        hn = (n * g_ref[...]).astype(BF16)
        hn_ref[...] = hn
        outs = (q_ref, k_ref, v_ref, u_ref)
        for j in range(N_DEV):
            z = jnp.dot(hn, w_ref[j], preferred_element_type=F32)
            if j < 2:
                z = z * (HEAD_DIM ** -0.5)
            outs[j // 2][:, (j % 2) * 256:(j % 2 + 1) * 256] = z
        _gather2_end(plan, pl.program_id(0), nsteps)

    tok = lambda w: pl.BlockSpec((TM, w), lambda i: (i, 0))
    res = pl.pallas_call(
        body, name="qkvu", grid=(nsteps,),
        in_specs=[tok(D_MODEL), _const((1, D_MODEL)), _const(w_in.shape)] + [ANY_SPEC] * ns,
        out_specs=[tok(512)] * 4 + [tok(D_MODEL)] + [ANY_SPEC] * ns,
        out_shape=[jax.ShapeDtypeStruct((S, 512), F32)] * 4 + [jax.ShapeDtypeStruct((S, D_MODEL), BF16)]
        + _exchange_shapes(shards, (False,) * ns),
        scratch_shapes=_exchange_sems(ns),
        compiler_params=_params("arbitrary"),
    )(x, g1, w_in, *shards)
    return res[:5], res[5:]


def _attn_fill_bias(bias_s, slope_ref, hp, d):
    qi = lax.broadcasted_iota(jnp.int32, (SPAN, 2 * SPAN), 0)
    kj = lax.broadcasted_iota(jnp.int32, (SPAN, 2 * SPAN), 1)
    for t, diff in enumerate((qi + SPAN - kj, qi - kj)):
        valid = (diff >= 0) & (diff <= SPAN)
        dist = diff.astype(F32) * float(d)
        for h in range(2):
            bias_s[t, h * SPAN:(h + 1) * SPAN, :] = jnp.where(valid, -slope_ref[2 * hp + h] * dist, NEG)


def _stack_heads(x, is0):
    return jnp.concatenate([jnp.where(is0, x, 0.0), jnp.where(is0, 0.0, x)], axis=0)


def _unstack_heads(y, is0):
    return jnp.where(is0, y[0:SPAN], y[SPAN:2 * SPAN])


def _attn_block(i, g, d, nb):
    gr = min(d, ATTN_GROUP)
    gn = ATTN_GROUP // gr
    per = d // gr
    r = (i & (per - 1)) * gr + g % gr
    n = (i >> (per.bit_length() - 1)) + (g // gr) * (nb // gn)
    k0 = jnp.maximum(n - 1, 0)

    def ds(block, nrows):
        start = block * (SPAN * d) + r
        return pl.ds(start, nrows, stride=d) if d > 1 else pl.ds(start, nrows)

    return ds(n, SPAN), ds(k0, 2 * SPAN), jnp.where(n == 0, 1, 0)


def _attn_groups(S, d):
    nb = S // d // SPAN
    gn = ATTN_GROUP // min(d, ATTN_GROUP)
    assert nb >= 2 and nb % gn == 0 and (gn == 1 or nb // gn >= 2)
    return nb, d * nb // ATTN_GROUP


def _attn_fwd(slopes, q, k, v, shards):
    S = q.shape[0]
    ns = len(shards)
    steps = ATT_WIDTH // LANES

    def body(slope_ref, q_ref, k_ref, v_ref, *rest):
        o_ref, lse_ref = rest[ns:ns + 2]
        m_s, l_s, bias_s = rest[2 * ns + 2:2 * ns + 5]
        hp = pl.program_id(0)
        plan = _gather2_copies(rest[:ns], rest[ns + 2:2 * ns + 2], *rest[2 * ns + 5:])
        _gather2_begin(plan, hp, steps - 1)

        is0 = lax.broadcasted_iota(jnp.int32, (SPAN, LANES), 1) < HEAD_DIM
        for pi, d in enumerate(DILATIONS):
            nb, ngroups = _attn_groups(S, d)
            _attn_fill_bias(bias_s, slope_ref, hp, d)

            def group(i, carry, d=d, pi=pi, nb=nb):
                blocks = [_attn_block(i, g, d, nb) for g in range(ATTN_GROUP)]
                loaded = [(q_ref[rows, :], k_ref[krows, :].astype(BF16), v_ref[krows, :].astype(BF16))
                          for rows, krows, _ in blocks]
                new = []
                for (rows, krows, tab), (qb, kb, vb) in zip(blocks, loaded):
                    qs = _stack_heads(qb, is0).astype(BF16)
                    s = lax.dot_general(qs, kb, NT, preferred_element_type=F32) + bias_s[tab]
                    m = jnp.max(s, axis=-1, keepdims=True)
                    e = jnp.exp(s - m)
                    l = jnp.sum(e, axis=-1, keepdims=True)
                    pv = jnp.dot(e.astype(BF16), vb, preferred_element_type=F32)
                    new.append([_unstack_heads(jnp.broadcast_to(m, pv.shape), is0),
                                _unstack_heads(jnp.broadcast_to(l, pv.shape), is0), _unstack_heads(pv, is0)])
                if pi > 0:
                    old = [(m_s[rows, :], l_s[rows, :], o_ref[rows, :]) for rows, _, _ in blocks]
                    for st, (m_o, l_o, o_o) in zip(new, old):
                        m_n = jnp.maximum(m_o, st[0])
                        a_o = jnp.exp(m_o - m_n)
                        a_b = jnp.exp(st[0] - m_n)
                        st[:] = [m_n, a_o * l_o + a_b * st[1], a_o * o_o + a_b * st[2]]
                for (rows, _, _), (m_b, l_b, acc) in zip(blocks, new):
                    if pi == len(DILATIONS) - 1:
                        o_ref[rows, :] = acc / l_b
                        lse_ref[rows, :] = m_b + jnp.log(l_b)
                    else:
                        o_ref[rows, :] = acc
                        m_s[rows, :] = m_b
                        l_s[rows, :] = l_b
                return carry

            lax.fori_loop(0, ngroups, group, 0)

        _gather2_end(plan, hp, steps)

    col = pl.BlockSpec((S, LANES), lambda i: (0, i))
    res = pl.pallas_call(
        body, name="attn_fwd", grid=(steps,),
        in_specs=[pl.BlockSpec(memory_space=pltpu.SMEM), col, col, col] + [ANY_SPEC] * ns,
        out_specs=[col, col] + [ANY_SPEC] * ns,
        out_shape=[jax.ShapeDtypeStruct((S, ATT_WIDTH), F32)] * 2 + _exchange_shapes(shards, (False,) * ns),
        scratch_shapes=[pltpu.VMEM((S, LANES), F32), pltpu.VMEM((S, LANES), F32),
                        pltpu.VMEM((2, 2 * SPAN, 2 * SPAN), F32)] + _exchange_sems(ns),
        compiler_params=_params("arbitrary"),
    )(slopes, q, k, v, *shards)
    return res[0], res[1], res[2:]


def _pool_count(i, w):
    t = i * TM + lax.broadcasted_iota(jnp.int32, (TM, 1), 0)
    return jnp.minimum(t + 1, w).astype(F32)


def _mix_out(x, att, u, pool_w, pool_scale, w_out, shards):
    S = x.shape[0]
    ns = len(shards)
    nsteps = S // TM

    def body(x_ref, att_ref, u_ref, pw_ref, ps_ref, w_ref, *rest):
        h1_ref, mix_ref, dlt_ref = rest[ns:ns + 3]
        ubuf = rest[2 * ns + 3]
        i = pl.program_id(0)
        plan = _gather2_copies(rest[:ns], rest[ns + 3:2 * ns + 3], *rest[2 * ns + 4:])
        _gather2_begin(plan, i, nsteps - 1)

        @pl.when(i == 0)
        def _():
            ubuf[0:HALO, :] = jnp.zeros((HALO, POOL_WIDTH), F32)

        ubuf[HALO:HALO + TM, :] = u_ref[...]
        mix_ref[:, 0:ATT_WIDTH] = att_ref[...].astype(BF16)
        for g, w in enumerate(POOL_WINDOWS):
            cols = slice(g * POOL_GROUP, (g + 1) * POOL_GROUP)
            ug = ubuf[HALO:HALO + TM, cols]
            acc = ug
            for j in range(1, w):
                acc = acc + ubuf[HALO - j:HALO - j + TM, cols]
            dlt = (acc / _pool_count(i, w) - ug).astype(BF16)
            dlt_ref[:, cols] = dlt
            yg = jnp.dot(dlt, pw_ref[g].astype(BF16), preferred_element_type=F32) * ps_ref[:, cols]
            mix_ref[:, ATT_WIDTH + g * POOL_GROUP:ATT_WIDTH + (g + 1) * POOL_GROUP] = yg.astype(BF16)
        ubuf[0:HALO, :] = ubuf[TM:TM + HALO, :]
        h1_ref[...] = x_ref[...] + jnp.dot(mix_ref[...], w_ref[...], preferred_element_type=F32)
        _gather2_end(plan, i, nsteps)

    tok = lambda w: pl.BlockSpec((TM, w), lambda i: (i, 0))
    res = pl.pallas_call(
        body, name="mix_out", grid=(nsteps,),
        in_specs=[tok(D_MODEL), tok(ATT_WIDTH), tok(POOL_WIDTH), _const(pool_w.shape), _const((1, POOL_WIDTH)),
                  _const(w_out.shape)] + [ANY_SPEC] * ns,
        out_specs=[tok(D_MODEL), tok(D_MODEL), tok(POOL_WIDTH)] + [ANY_SPEC] * ns,
        out_shape=[jax.ShapeDtypeStruct((S, D_MODEL), F32), jax.ShapeDtypeStruct((S, D_MODEL), BF16),
                   jax.ShapeDtypeStruct((S, POOL_WIDTH), BF16)] + _exchange_shapes(shards, (False,) * ns),
        scratch_shapes=[pltpu.VMEM((TM + HALO, POOL_WIDTH), F32)] + _exchange_sems(ns),
        compiler_params=_params("arbitrary"),
    )(x, att, u, pool_w, pool_scale, w_out, *shards)
    return res[0], res[1], res[2], res[3:]


def _conv_fwd(stage, upre, prev, cw, cb):
    T = upre.shape[0]
    stage[0:HALO, :] = prev
    stage[HALO:HALO + T, :] = upre
    return cb + cw[0:1, :] * stage[HALO - 2:HALO - 2 + T, :] + cw[1:2, :] * stage[HALO - 1:HALO - 1 + T, :] + cw[2:3, :] * upre


def _ffn_fwd(h1, g2, w_up, conv_w, conv_b, w_down, shards):
    S = h1.shape[0]
    T = TM_FF
    ns = len(shards)
    nsteps = S // T

    def body(h1_ref, g_ref, wu_ref, cw_ref, cb_ref, wd_ref, *rest):
        h2_ref, hn_ref, up_ref, upc_ref = rest[ns:ns + 4]
        carry, stage = rest[2 * ns + 4:2 * ns + 6]
        i = pl.program_id(0)
        plan = _gather2_copies(rest[:ns], rest[ns + 4:2 * ns + 4], *rest[2 * ns + 6:])
        _gather2_begin(plan, i, nsteps // 2)

        @pl.when(i == 0)
        def _():
            carry[...] = jnp.zeros(carry.shape, F32)

        h1t = h1_ref[...]
        r, n = _rms(h1t)
        hn = (n * g_ref[...]).astype(BF16)
        hn_ref[...] = hn
        acc = h1t
        for j in range(4):
            conv = []
            for jj in (j, j + 4):
                upre = jnp.dot(hn, wu_ref[jj], preferred_element_type=F32)
                up_ref[jj] = upre.astype(BF16)
                conv.append(_conv_fwd(stage, upre, carry[jj], cw_ref[jj], cb_ref[jj]))
                upc_ref[jj] = conv[-1].astype(BF16)
                carry[jj] = stage[T:T + HALO, :]
            gate, val = conv
            a = gate * jax.nn.sigmoid(gate) * val
            acc = acc + jnp.dot(a.astype(BF16), wd_ref[j], preferred_element_type=F32)
        h2_ref[...] = acc
        _gather2_end(plan, i, nsteps)

    tok = lambda w: pl.BlockSpec((T, w), lambda i: (i, 0))
    res = pl.pallas_call(
        body, name="ffn_fwd", grid=(nsteps,),
        in_specs=[tok(D_MODEL), _const((1, D_MODEL)), _const(w_up.shape), _const(conv_w.shape), _const(conv_b.shape),
                  _const(w_down.shape)] + [ANY_SPEC] * ns,
        out_specs=[tok(D_MODEL), tok(D_MODEL)] + [pl.BlockSpec((N_DEV, T, FF_SHARD), lambda i: (0, i, 0))] * 2
        + [ANY_SPEC] * ns,
        out_shape=[jax.ShapeDtypeStruct((S, D_MODEL), F32), jax.ShapeDtypeStruct((S, D_MODEL), BF16)]
        + [jax.ShapeDtypeStruct((N_DEV, S, FF_SHARD), BF16)] * 2 + _exchange_shapes(shards, (False,) * ns),
        scratch_shapes=[pltpu.VMEM((N_DEV, HALO, FF_SHARD), F32), pltpu.VMEM((T + HALO, FF_SHARD), F32)]
        + _exchange_sems(ns),
        compiler_params=_params("arbitrary"),
    )(h1, g2, w_up, conv_w, conv_b, w_down, *shards)
    return res[0], res[1], res[2], res[3], res[4:]


def _head(h2, p, g3, w_pg, w_ple, g4, target):
    S = h2.shape[0]
    nt = S // TM

    def body(h2_ref, p_ref, g3_ref, wpg_ref, wple_ref, g4_ref, t_ref,
             loss_ref, dh2_ref, dh2b_ref, hn3_ref, dgl_ref, dpe_ref, dg3_ref, dg4_ref, lacc):
        i = pl.program_id(0)

        @pl.when(i == 0)
        def _():
            lacc[...] = jnp.zeros(lacc.shape, F32)
            dg3_ref[...] = jnp.zeros(dg3_ref.shape, F32)
            dg4_ref[...] = jnp.zeros(dg4_ref.shape, F32)

        h2t = h2_ref[...]
        g3, g4 = g3_ref[...], g4_ref[...]
        r3, n3 = _rms(h2t)
        hn3 = (n3 * g3).astype(BF16)
        hn3_ref[...] = hn3
        gs = jax.nn.sigmoid(jnp.dot(hn3, wpg_ref[...], preferred_element_type=F32))
        pe = jnp.dot(p_ref[...].astype(BF16), wple_ref[...], preferred_element_type=F32)
        h3 = h2t + gs * pe
        r4, n4 = _rms(h3)
        err = n4 * g4 - t_ref[...]
        lacc[...] += _colsum(err * err)
        dy = err * (1.0 / D_MODEL)
        dg4_ref[...] += _colsum(dy * n4)
        dh3 = _rms_bwd(r4, n4, g4, dy)
        dpe_ref[...] = (dh3 * gs).astype(BF16)
        dgl = (dh3 * pe * gs * (1.0 - gs)).astype(BF16)
        dgl_ref[...] = dgl
        dhn3 = lax.dot_general(dgl, wpg_ref[...], NT, preferred_element_type=F32)
        dg3_ref[...] += _colsum(dhn3 * n3)
        dh2 = dh3 + _rms_bwd(r3, n3, g3, dhn3)
        dh2_ref[...] = dh2
        dh2b_ref[...] = dh2.astype(BF16)

        @pl.when(i == nt - 1)
        def _():
            tot = 0.5 / D_MODEL * jnp.sum(lacc[...], axis=-1, keepdims=True)
            loss_ref[...] = jnp.broadcast_to(tot, loss_ref.shape)

    tok = lambda w: pl.BlockSpec((TM, w), lambda i: (i, 0))
    row = pl.BlockSpec((1, D_MODEL), lambda i: (0, 0))
    act = lambda dt: jax.ShapeDtypeStruct((S, D_MODEL), dt)
    return pl.pallas_call(
        body, name="head", grid=(nt,),
        in_specs=[tok(D_MODEL), tok(PLE_DIM), _const((1, D_MODEL)), _const(w_pg.shape), _const(w_ple.shape),
                  _const((1, D_MODEL)), tok(D_MODEL)],
        out_specs=[pl.BlockSpec((8, LANES), lambda i: (0, 0)), tok(D_MODEL), tok(D_MODEL), tok(D_MODEL), tok(D_MODEL),
                   tok(D_MODEL), row, row],
        out_shape=[jax.ShapeDtypeStruct((8, LANES), F32), act(F32), act(BF16), act(BF16), act(BF16), act(BF16),
                   jax.ShapeDtypeStruct((1, D_MODEL), F32), jax.ShapeDtypeStruct((1, D_MODEL), F32)],
        scratch_shapes=[pltpu.VMEM((1, D_MODEL), F32)],
        compiler_params=_params("arbitrary"),
    )(h2, p, g3, w_pg, w_ple, g4, target)


def _wgrad(name, x, dy, x_kind, dy_kind, nj, k_dim, n_dim):
    S = x.shape[-2]
    nt = S // TK

    def spec(kind, width):
        if kind == "full":
            return pl.BlockSpec((TK, width), lambda j, t: (t, 0))
        return pl.BlockSpec((None, TK, width), lambda j, t: (j, t, 0))

    def body(x_ref, dy_ref, o_ref, acc):
        t = pl.program_id(1)

        @pl.when(t == 0)
        def _():
            acc[...] = jnp.zeros(acc.shape, F32)

        acc[...] += lax.dot_general(x_ref[...].astype(BF16), dy_ref[...], TN, preferred_element_type=F32)

        @pl.when(t == nt - 1)
        def _():
            o_ref[...] = acc[...].astype(BF16)

    return pl.pallas_call(
        body, name=name, grid=(nj, nt),
        in_specs=[spec(x_kind, k_dim), spec(dy_kind, n_dim)],
        out_specs=pl.BlockSpec((None, k_dim, n_dim), lambda j, t: (j, 0, 0)),
        out_shape=jax.ShapeDtypeStruct((nj, k_dim, n_dim), BF16),
        scratch_shapes=[pltpu.VMEM((k_dim, n_dim), F32)],
        compiler_params=_params("arbitrary", "arbitrary"),
    )(x, dy)


def _row_picker(T, off0, off1):
    r = lax.broadcasted_iota(jnp.int32, (2 * T, T + HALO), 0)
    c = lax.broadcasted_iota(jnp.int32, (2 * T, T + HALO), 1)
    want = jnp.where(r < T, r + off0, r - T + off1)
    return jnp.where(c == want, 1.0, 0.0).astype(BF16)


def _ffn_bwd_a(dh2b, up, upc, w_down, grads):
    S = dh2b.shape[0]
    T = TM_FF
    hb = T // HALO
    nsteps = S // T
    ng = len(grads)

    def body(dh_ref, up_ref, halo_ref, upc_ref, wd_ref, *rest):
        a_ref, dup_ref, dcw_ref, dcb_ref = rest[ng:ng + 4]
        stage = rest[2 * ng + 4]
        i = pl.program_id(0)
        copies = _exchange_copies(rest[:ng], rest[ng + 4:2 * ng + 4], (True,) * ng, *rest[2 * ng + 5:])

        @pl.when(i == 0)
        def _():
            dcw_ref[...] = jnp.zeros(dcw_ref.shape, F32)
            dcb_ref[...] = jnp.zeros(dcb_ref.shape, F32)
            for cp in copies:
                cp.start()

        @pl.when(i == nsteps - 1)
        def _():
            for cp in copies:
                cp.wait()

        dh = dh_ref[...]
        pick = _row_picker(T, HALO - 2, HALO - 1)
        for j in range(4):
            da = lax.dot_general(dh, wd_ref[j], NT, preferred_element_type=F32)
            taps = []
            for jj in (j, j + 4):
                upre = up_ref[jj]
                stage[0:HALO, :] = jnp.where(i > 0, halo_ref[jj], jnp.zeros((HALO, FF_SHARD), BF16))
                stage[HALO:HALO + T, :] = upre
                prv = jnp.dot(pick, stage[...], preferred_element_type=F32)
                taps.append((prv[0:T], prv[T:2 * T], upre.astype(F32)))
            gate, val = upc_ref[j].astype(F32), upc_ref[j + 4].astype(F32)
            sg = jax.nn.sigmoid(gate)
            silu = gate * sg
            a_ref[j] = (silu * val).astype(BF16)
            dgate = (da * val) * (sg + silu * (1.0 - sg))
            dval = da * silu
            for jj, dup, tp in ((j, dgate, taps[0]), (j + 4, dval, taps[1])):
                dup_ref[jj] = dup.astype(BF16)
                dcb_ref[jj] += _colsum(dup)
                for kk in range(3):
                    dcw_ref[jj, kk:kk + 1, :] += _colsum(dup * tp[kk])

    tok = lambda w: pl.BlockSpec((T, w), lambda i: (i, 0))
    shard = pl.BlockSpec((N_DEV, T, FF_SHARD), lambda i: (0, i, 0))
    res = pl.pallas_call(
        body, name="ffn_bwd_a", grid=(nsteps,),
        in_specs=[tok(D_MODEL), shard,
                  pl.BlockSpec((N_DEV, HALO, FF_SHARD), lambda i: (0, jnp.maximum(i * hb - 1, 0), 0)),
                  shard, _const(w_down.shape)] + [ANY_SPEC] * ng,
        out_specs=[pl.BlockSpec((4, T, FF_SHARD), lambda i: (0, i, 0)), shard,
                   pl.BlockSpec((N_DEV, 3, FF_SHARD), lambda i: (0, 0, 0)),
                   pl.BlockSpec((N_DEV, 1, FF_SHARD), lambda i: (0, 0, 0))] + [ANY_SPEC] * ng,
        out_shape=[jax.ShapeDtypeStruct((4, S, FF_SHARD), BF16), jax.ShapeDtypeStruct((N_DEV, S, FF_SHARD), BF16),
                   jax.ShapeDtypeStruct((N_DEV, 3, FF_SHARD), F32), jax.ShapeDtypeStruct((N_DEV, 1, FF_SHARD), F32)]
        + _exchange_shapes(grads, (True,) * ng),
        scratch_shapes=[pltpu.VMEM((T + HALO, FF_SHARD), BF16)] + _exchange_sems(ng),
        compiler_params=_params("arbitrary"),
    )(dh2b, up, up, upc, w_down, *grads)
    return res[0], res[1], res[2], res[3], res[4:]


def _ffn_bwd_b(dup, conv_w, w_up, h1, g2, dh2, grads):
    S = h1.shape[0]
    T = TM_FF
    hb = T // HALO
    nt = S // T
    ng = len(grads)

    def body(dup_ref, halo_ref, cw_ref, wu_ref, h1_ref, g_ref, dh2_ref, *rest):
        dpre_ref, dh1_ref, dh1b_ref, dg_ref = rest[ng:ng + 4]
        stage = rest[2 * ng + 4]
        i = pl.program_id(0)
        copies = _exchange_copies(rest[:ng], rest[ng + 4:2 * ng + 4], (True,) * ng, *rest[2 * ng + 5:])

        @pl.when(i == 0)
        def _():
            dg_ref[...] = jnp.zeros(dg_ref.shape, F32)
            for cp in copies:
                cp.start()

        dhn = jnp.zeros((T, D_MODEL), F32)
        for jj in range(N_DEV):
            dup = dup_ref[jj].astype(F32)
            stage[0:T, :] = dup
            stage[T:T + HALO, :] = jnp.where(i < nt - 1, halo_ref[jj].astype(F32), 0.0)
            cw = cw_ref[jj]
            dpre = (cw[2:3, :] * dup + cw[1:2, :] * stage[1:1 + T, :] + cw[0:1, :] * stage[2:2 + T, :]).astype(BF16)
            dpre_ref[jj] = dpre
            dhn = dhn + lax.dot_general(dpre, wu_ref[jj], NT, preferred_element_type=F32)
        g = g_ref[...]
        r, n = _rms(h1_ref[...])
        dg_ref[...] += _colsum(dhn * n)
        dh1 = dh2_ref[...] + _rms_bwd(r, n, g, dhn)
        dh1_ref[...] = dh1
        dh1b_ref[...] = dh1.astype(BF16)

        @pl.when(i == nt - 1)
        def _():
            for cp in copies:
                cp.wait()

    tok = lambda w: pl.BlockSpec((T, w), lambda i: (i, 0))
    shard = pl.BlockSpec((N_DEV, T, FF_SHARD), lambda i: (0, i, 0))
    res = pl.pallas_call(
        body, name="ffn_bwd_b", grid=(nt,),
        in_specs=[shard,
                  pl.BlockSpec((N_DEV, HALO, FF_SHARD), lambda i: (0, jnp.minimum((i + 1) * hb, S // HALO - 1), 0)),
                  _const(conv_w.shape), _const(w_up.shape), tok(D_MODEL), _const((1, D_MODEL)), tok(D_MODEL)]
        + [ANY_SPEC] * ng,
        out_specs=[shard, tok(D_MODEL), tok(D_MODEL), pl.BlockSpec((1, D_MODEL), lambda i: (0, 0))] + [ANY_SPEC] * ng,
        out_shape=[jax.ShapeDtypeStruct((N_DEV, S, FF_SHARD), BF16), jax.ShapeDtypeStruct((S, D_MODEL), F32),
                   jax.ShapeDtypeStruct((S, D_MODEL), BF16), jax.ShapeDtypeStruct((1, D_MODEL), F32)]
        + _exchange_shapes(grads, (True,) * ng),
        scratch_shapes=[pltpu.VMEM((T + HALO, FF_SHARD), F32)] + _exchange_sems(ng),
        compiler_params=_params("arbitrary"),
    )(dup, dup, conv_w, w_up, h1, g2, dh2, *grads)
    return res[0], res[1], res[2], res[3], res[4:]


def _mix_bwd(dh1b, w_out, dlt, pool_w, pool_scale):
    S = dh1b.shape[0]
    nt = S // TM

    def body(dh_ref, w_ref, dlt_ref, pw_ref, ps_ref, datt_ref, du_ref, dpw_ref, dps_ref, stage, carry):
        i = pl.program_id(0)
        tile = nt - 1 - i

        @pl.when(i == 0)
        def _():
            dpw_ref[...] = jnp.zeros(dpw_ref.shape, F32)
            dps_ref[...] = jnp.zeros(dps_ref.shape, F32)
            carry[...] = jnp.zeros(carry.shape, F32)

        dmix = lax.dot_general(dh_ref[...], w_ref[...], NT, preferred_element_type=F32)
        datt_ref[...] = dmix[:, 0:ATT_WIDTH]
        for g, w in enumerate(POOL_WINDOWS):
            cols = slice(g * POOL_GROUP, (g + 1) * POOL_GROUP)
            dpool = dmix[:, ATT_WIDTH + g * POOL_GROUP:ATT_WIDTH + (g + 1) * POOL_GROUP]
            dl = dlt_ref[:, cols]
            pw = pw_ref[g].astype(BF16)
            yg = jnp.dot(dl, pw, preferred_element_type=F32)
            dps_ref[:, cols] += _colsum(dpool * yg)
            dy = (dpool * ps_ref[:, cols]).astype(BF16)
            dpw_ref[g] += lax.dot_general(dl, dy, TN, preferred_element_type=F32)
            ddlt = lax.dot_general(dy, pw, NT, preferred_element_type=F32)
            cg = ddlt / _pool_count(tile, w)
            stage[0:TM, :] = cg
            stage[TM:TM + HALO, :] = carry[:, cols]
            acc = cg
            for j in range(1, w):
                acc = acc + stage[j:j + TM, :]
            du_ref[:, cols] = acc - ddlt
            carry[:, cols] = cg[0:HALO, :]

    tok = lambda w: pl.BlockSpec((TM, w), lambda i: (nt - 1 - i, 0))
    return pl.pallas_call(
        body, name="mix_bwd", grid=(nt,),
        in_specs=[tok(D_MODEL), _const(w_out.shape), tok(POOL_WIDTH), _const(pool_w.shape), _const((1, POOL_WIDTH))],
        out_specs=[tok(ATT_WIDTH), tok(POOL_WIDTH), pl.BlockSpec(pool_w.shape, lambda i: (0, 0, 0)),
                   pl.BlockSpec((1, POOL_WIDTH), lambda i: (0, 0))],
        out_shape=[jax.ShapeDtypeStruct((S, ATT_WIDTH), F32), jax.ShapeDtypeStruct((S, POOL_WIDTH), F32),
                   jax.ShapeDtypeStruct(pool_w.shape, F32), jax.ShapeDtypeStruct((1, POOL_WIDTH), F32)],
        scratch_shapes=[pltpu.VMEM((TM + HALO, POOL_GROUP), F32), pltpu.VMEM((HALO, POOL_WIDTH), F32)],
        compiler_params=_params("arbitrary"),
    )(dh1b, w_out, dlt, pool_w, pool_scale)


def _attn_bwd(slopes, q, k, v, o, lse, do, grads, scatter):
    S = q.shape[0]
    CH = 512
    ng = len(grads)
    steps = ATT_WIDTH // LANES

    def body(slope_ref, q_ref, k_ref, v_ref, o_ref, lse_ref, do_ref, *rest):
        dq_ref, dk_ref, dv_ref = rest[ng:ng + 3]
        dl_s, bias_s = rest[2 * ng + 3:2 * ng + 5]
        hp = pl.program_id(0)
        copies = _exchange_copies(rest[:ng], rest[ng + 3:2 * ng + 3], scatter, *rest[2 * ng + 5:])

        @pl.when(hp == 0)
        def _():
            for cp in copies:
                cp.start()

        is0 = lax.broadcasted_iota(jnp.int32, (SPAN, LANES), 1) < HEAD_DIM
        is0c = lax.broadcasted_iota(jnp.int32, (CH, LANES), 1) < HEAD_DIM

        def prep(ci, carry):
            rows = pl.ds(pl.multiple_of(ci * CH, CH), CH)
            prod = do_ref[rows, :] * o_ref[rows, :]
            d0 = jnp.sum(jnp.where(is0c, prod, 0.0), axis=-1, keepdims=True)
            d1 = jnp.sum(jnp.where(is0c, 0.0, prod), axis=-1, keepdims=True)
            dl_s[rows, :] = jnp.where(is0c, d0, d1)
            zero = jnp.zeros((CH, LANES), F32)
            dq_ref[rows, :] = zero
            dk_ref[rows, :] = zero
            dv_ref[rows, :] = zero
            return carry

        lax.fori_loop(0, S // CH, prep, 0)

        for d in DILATIONS:
            nb, ngroups = _attn_groups(S, d)
            _attn_fill_bias(bias_s, slope_ref, hp, d)

            def group(i, carry, d=d, nb=nb):
                blocks = [_attn_block(i, g, d, nb) for g in range(ATTN_GROUP)]
                loaded = [(q_ref[rows, :], do_ref[rows, :], lse_ref[rows, :], dl_s[rows, :], k_ref[krows, :],
                           v_ref[krows, :].astype(BF16)) for rows, krows, _ in blocks]
                new = []
                for (rows, krows, tab), (qb, dob, lse_b, dl_b, kf, vb) in zip(blocks, loaded):
                    kb = kf.astype(BF16)
                    qs = _stack_heads(qb, is0).astype(BF16)
                    dos = _stack_heads(dob, is0).astype(BF16)
                    lse_s = jnp.concatenate([lse_b[:, 0:1], lse_b[:, HEAD_DIM:HEAD_DIM + 1]], axis=0)
                    dl_s2 = jnp.concatenate([dl_b[:, 0:1], dl_b[:, HEAD_DIM:HEAD_DIM + 1]], axis=0)
                    s = lax.dot_general(qs, kb, NT, preferred_element_type=F32) + bias_s[tab]
                    pr = jnp.exp(s - lse_s)
                    dp = lax.dot_general(dos, vb, NT, preferred_element_type=F32)
                    ds = (pr * (dp - dl_s2)).astype(BF16)
                    dv_c = lax.dot_general(pr.astype(BF16), dos, TN, preferred_element_type=F32)
                    dk_c = lax.dot_general(ds, qs, TN, preferred_element_type=F32)
                    dq_c = _unstack_heads(jnp.dot(ds, kb, preferred_element_type=F32), is0)
                    new.append((dq_c, dk_c, dv_c))
                old = [(dq_ref[rows, :], dk_ref[krows, :], dv_ref[krows, :]) for rows, krows, _ in blocks]
                for (rows, krows, _), (dq_c, dk_c, dv_c), (dq_o, dk_o, dv_o) in zip(blocks, new, old):
                    dq_ref[rows, :] = dq_o + dq_c
                    dk_ref[krows, :] = dk_o + dk_c
                    dv_ref[krows, :] = dv_o + dv_c
                return carry

            lax.fori_loop(0, ngroups, group, 0)

        @pl.when(hp == steps - 1)
        def _():
            for cp in copies:
                cp.wait()

    col = pl.BlockSpec((S, LANES), lambda i: (0, i))
    res = pl.pallas_call(
        body, name="attn_bwd", grid=(steps,),
        in_specs=[pl.BlockSpec(memory_space=pltpu.SMEM)] + [col] * 6 + [ANY_SPEC] * ng,
        out_specs=[col] * 3 + [ANY_SPEC] * ng,
        out_shape=[jax.ShapeDtypeStruct((S, ATT_WIDTH), F32)] * 3 + _exchange_shapes(grads, scatter),
        scratch_shapes=[pltpu.VMEM((S, LANES), F32), pltpu.VMEM((2, 2 * SPAN, 2 * SPAN), F32)] + _exchange_sems(ng),
        compiler_params=_params("arbitrary"),
    )(slopes, q, k, v, o, lse, do, *grads)
    return res[0], res[1], res[2], res[3:]


def _in_bwd(dq, dk, dv, du, w_in, x, g1, dh1):
    S = x.shape[0]

    def body(dq_ref, dk_ref, dv_ref, du_ref, w_ref, x_ref, g_ref, dh1_ref, dz_ref, dx_ref, dg_ref):
        @pl.when(pl.program_id(0) == 0)
        def _():
            dg_ref[...] = jnp.zeros(dg_ref.shape, F32)

        srcs = (dq_ref, dk_ref, dv_ref, du_ref)
        dhn = jnp.zeros((TM, D_MODEL), F32)
        for j in range(N_DEV):
            dz = srcs[j // 2][:, (j % 2) * 256:(j % 2 + 1) * 256]
            if j < 2:
                dz = dz * (HEAD_DIM ** -0.5)
            dz = dz.astype(BF16)
            dz_ref[j] = dz
            dhn = dhn + lax.dot_general(dz, w_ref[j], NT, preferred_element_type=F32)
        g = g_ref[...]
        r, n = _rms(x_ref[...])
        dg_ref[...] += _colsum(dhn * n)
        dx_ref[...] = dh1_ref[...] + _rms_bwd(r, n, g, dhn)

    tok = lambda w: pl.BlockSpec((TM, w), lambda i: (i, 0))
    return pl.pallas_call(
        body, name="in_bwd", grid=(S // TM,),
        in_specs=[tok(512)] * 4 + [_const(w_in.shape), tok(D_MODEL), _const((1, D_MODEL)), tok(D_MODEL)],
        out_specs=[pl.BlockSpec((N_DEV, TM, 256), lambda i: (0, i, 0)), tok(D_MODEL),
                   pl.BlockSpec((1, D_MODEL), lambda i: (0, 0))],
        out_shape=[jax.ShapeDtypeStruct((N_DEV, S, 256), BF16), jax.ShapeDtypeStruct((S, D_MODEL), F32),
                   jax.ShapeDtypeStruct((1, D_MODEL), F32)],
        compiler_params=_params("arbitrary"),
    )(dq, dk, dv, du, w_in, x, g1, dh1)


def _adamw(name, parts, w, m, v):
    R, C = w.shape
    rb = R
    for cand in (256, 128, 64, 32, 16, 8):
        if R % cand == 0 and R > cand:
            rb = cand
            break

    def body(p_ref, w_ref, m_ref, v_ref, g_ref, d_ref, mo_ref, vo_ref):
        g = p_ref[0].astype(F32)
        for s in range(1, N_DEV):
            g = g + p_ref[s].astype(F32)
        g_ref[...] = g
        d_ref[...], mo_ref[...], vo_ref[...] = _adam_update(g, w_ref[...], m_ref[...], v_ref[...])

    blk = pl.BlockSpec((rb, C), lambda i: (i, 0))
    return pl.pallas_call(
        body, name=name, grid=(R // rb,),
        in_specs=[pl.BlockSpec((N_DEV, rb, C), lambda i: (0, i, 0)), blk, blk, blk],
        out_specs=[blk] * 4,
        out_shape=[jax.ShapeDtypeStruct((R, C), F32)] * 4,
        compiler_params=_params("arbitrary"),
    )(parts, w, m, v)


def _adam_update(g, w, m, v):
    m_new = ADAM_B1 * m + (1.0 - ADAM_B1) * g
    v_new = ADAM_B2 * v + (1.0 - ADAM_B2) * (g * g)
    m_hat = m_new / (1.0 - ADAM_B1 ** ADAM_STEP)
    v_hat = v_new / (1.0 - ADAM_B2 ** ADAM_STEP)
    return -ADAM_LR * (m_hat / (jnp.sqrt(v_hat) + ADAM_EPS) + ADAM_WD * w), m_new, v_new


def _adamw_small(parts, loss_parts, ws, ms, vs):
    n = len(ws)

    def body(*refs):
        p_refs, lp_ref = refs[:n], refs[n]
        w_refs, m_refs, v_refs = refs[n + 1:2 * n + 1], refs[2 * n + 1:3 * n + 1], refs[3 * n + 1:4 * n + 1]
        outs = refs[4 * n + 1:]
        for i in range(n):
            g = p_refs[i][0]
            for s in range(1, N_DEV):
                g = g + p_refs[i][s]
            d, m_new, v_new = _adam_update(g, w_refs[i][...], m_refs[i][...], v_refs[i][...])
            outs[i][...] = g
            outs[n + i][...] = d
            outs[2 * n + i][...] = m_new
            outs[3 * n + i][...] = v_new
        tot = lp_ref[0]
        for s in range(1, N_DEV):
            tot = tot + lp_ref[s]
        outs[4 * n][...] = tot

    shapes = [jax.ShapeDtypeStruct(w.shape, F32) for w in ws]
    res = pl.pallas_call(
        body, name="adamw_replicated",
        out_shape=shapes * 4 + [jax.ShapeDtypeStruct(loss_parts.shape[1:], F32)],
        compiler_params=_params(),
    )(*parts, loss_parts, *ws, *ms, *vs)
    return res[:n], res[n:2 * n], res[2 * n:3 * n], res[3 * n:4 * n], res[4 * n]


def _gather2(name, arrays):
    n = len(arrays)

    def body(*refs):
        first, passed, last = _gather2_copies(refs[:n], refs[n:2 * n], *refs[2 * n:])
        for cp in first:
            cp.start()
        for arrival, cp in passed:
            arrival.wait_recv()
            cp.start()
        for wait in last:
            wait()

    return pl.pallas_call(
        body, name=name,
        in_specs=[ANY_SPEC] * n, out_specs=[ANY_SPEC] * n, out_shape=_exchange_shapes(arrays, (False,) * n),
        scratch_shapes=_exchange_sems(n),
    )(*arrays)


def _dw_in_exchange(hn, dz, small):
    S = hn.shape[0]
    nt = S // TK
    ns = len(small)
    kd, nd = hn.shape[1], dz.shape[2]
    me_arr = (4 * lax.axis_index("x") + 2 * lax.axis_index("y") + lax.axis_index("c")).astype(jnp.int32).reshape(1)

    def body(me_ref, x_ref, dy_ref, *rest):
        recv_ref = rest[ns]
        acc, stage, send_sems, recv_sems, own_sem = rest[2 * ns + 1:2 * ns + 6]
        j, t = pl.program_id(0), pl.program_id(1)
        x, y, c = lax.axis_index("x"), lax.axis_index("y"), lax.axis_index("c")
        me = 4 * x + 2 * y + c
        small_copies = _exchange_copies(rest[:ns], rest[ns + 1:2 * ns + 1], (False,) * ns, *rest[2 * ns + 6:])

        @pl.when((j == 0) & (t == 0))
        def _():
            for cp in small_copies:
                cp.start()

        @pl.when(t == 0)
        def _():
            acc[...] = jnp.zeros(acc.shape, F32)

        acc[...] += lax.dot_general(x_ref[...], dy_ref[...], TN, preferred_element_type=F32)

        def to_owner(k, owner):
            return pltpu.make_async_remote_copy(
                src_ref=stage.at[owner], dst_ref=recv_ref.at[me], send_sem=send_sems.at[k], recv_sem=recv_sems.at[k],
                device_id=(owner // 4, (owner // 2) % 2, owner % 2), device_id_type=MESH)

        own = pltpu.make_async_copy(stage.at[me], recv_ref.at[me], own_sem)

        @pl.when(t == nt - 1)
        def _():
            owner = (me + 1 + j) % N_DEV
            stage[owner] = acc[...].astype(BF16)

            @pl.when(j < N_DEV - 1)
            def _():
                to_owner(j, owner).start()

            @pl.when(j == N_DEV - 1)
            def _():
                own.start()
                own.wait()
                for k in range(N_DEV - 1):
                    to_owner(k, me).wait_send()
                    to_owner(k, me).wait_recv()
                for cp in small_copies:
                    cp.wait()

    slab = lambda j, me_ref: (me_ref[0] + 1 + j) % N_DEV
    grid_spec = pltpu.PrefetchScalarGridSpec(
        num_scalar_prefetch=1, grid=(N_DEV, nt),
        in_specs=[pl.BlockSpec((TK, kd), lambda j, t, me_ref: (t, 0)),
                  pl.BlockSpec((None, TK, nd), lambda j, t, me_ref: (slab(j, me_ref), t, 0))] + [ANY_SPEC] * ns,
        out_specs=[ANY_SPEC] * (ns + 1),
        scratch_shapes=[pltpu.VMEM((kd, nd), F32), pltpu.VMEM((N_DEV, kd, nd), BF16),
                        pltpu.SemaphoreType.DMA((N_DEV - 1,)), pltpu.SemaphoreType.DMA((N_DEV - 1,)),
                        pltpu.SemaphoreType.DMA] + _exchange_sems(ns))
    res = pl.pallas_call(
        body, name="dw_in_exchange", grid_spec=grid_spec,
        out_shape=[jax.ShapeDtypeStruct((N_DEV, kd, nd), BF16)] + _exchange_shapes(small, (False,) * ns),
        compiler_params=_params("arbitrary", "arbitrary"),
    )(me_arr, hn, dz, *small)
    return res[0], res[1:]


def kernel(x, p, ln_mix, w_in, pool_w, pool_scale, w_out, ln_ffn, w_up, conv_w, conv_b, w_down, ln_ple, w_ple_gate, w_ple, ln_final, loss_target, m_ln_mix, m_w_in, m_pool_w, m_pool_scale, m_w_out, m_ln_ffn, m_w_up, m_conv_w, m_conv_b, m_w_down, m_ln_ple, m_w_ple_gate, m_w_ple, m_ln_final, v_ln_mix, v_w_in, v_pool_w, v_pool_scale, v_w_out, v_ln_ffn, v_w_up, v_conv_w, v_conv_b, v_w_down, v_ln_ple, v_w_ple_gate, v_w_ple, v_ln_final):
    xs, ps, tgt, pool_w0 = x[0], p[0, 0], loss_target[0], pool_w[0]
    slopes = jnp.exp2(-8.0 * (jnp.arange(N_HEADS, dtype=F32) + 1.0) / N_HEADS)
    conv_b_s = conv_b.reshape(N_DEV, 1, FF_SHARD)

    (w_in_g,) = _gather2("gather_w_in", [w_in[0].astype(BF16)])
    (q, k, v, u, hn1), (w_out_g,) = _qkvu(xs, ln_mix, w_in_g, [w_out[0].astype(BF16)])
    att, lse, (w_up_g, conv_w_g) = _attn_fwd(slopes, q, k, v, [w_up[0].astype(BF16), conv_w[0]])
    w_out_f = w_out_g.reshape(D_MODEL, D_MODEL)
    h1, mix, dlt, (w_down_g,) = _mix_out(xs, att, u, pool_w0, pool_scale, w_out_f, [w_down[0].astype(BF16)])
    w_down_f = w_down_g.reshape(4, FF_SHARD, D_MODEL)
    h2, hn2, up, upc, (w_pg_g, w_ple_g) = _ffn_fwd(h1, ln_ffn, w_up_g, conv_w_g, conv_b_s, w_down_f,
                                                   [w_ple_gate[0].astype(BF16), w_ple[0].astype(BF16)])
    w_pg_f = w_pg_g.reshape(D_MODEL, D_MODEL)
    w_ple_f = jnp.transpose(w_ple_g, (1, 0, 2)).reshape(PLE_DIM, D_MODEL)
    loss_blk, dh2, dh2b, hn3, dgl, dpe, d_ln_ple, d_ln_final = _head(
        h2, ps, ln_ple, w_pg_f, w_ple_f, ln_final.reshape(1, D_MODEL), tgt)

    d_w_pg = _wgrad("dw_ple_gate", hn3, dgl, "full", "full", 1, D_MODEL, D_MODEL).reshape(N_DEV, D_MODEL // N_DEV, D_MODEL)
    d_w_ple = _wgrad("dw_ple", ps, dpe, "full", "full", 1, PLE_DIM, D_MODEL)
    d_w_ple = jnp.transpose(d_w_ple.reshape(PLE_DIM, N_DEV, LANES), (1, 0, 2))
    a, dup, d_conv_w, d_conv_b, (r_w_pg, r_w_ple) = _ffn_bwd_a(dh2b, up, upc, w_down_f, [d_w_pg, d_w_ple])
    d_w_down = _wgrad("dw_down", a, dh2b, "lead", "full", 4, FF_SHARD, D_MODEL).reshape(N_DEV, D_FF // N_DEV, D_MODEL)
    dpre, dh1, dh1b, d_ln_ffn, (r_conv_w, r_w_down) = _ffn_bwd_b(
        dup, conv_w_g, w_up_g, h1, ln_ffn, dh2, [d_conv_w, d_w_down])
    datt, du, d_pool_w, d_pool_scale = _mix_bwd(dh1b, w_out_f, dlt, pool_w0, pool_scale)
    d_w_out = _wgrad("dw_out", mix, dh1b, "full", "full", 1, D_MODEL, D_MODEL).reshape(N_DEV, D_MODEL // N_DEV, D_MODEL)
    d_w_up = _wgrad("dw_up", dpre, hn2, "lead", "full", N_DEV, FF_SHARD, D_MODEL)
    rep_late = [d_pool_w, d_pool_scale, d_ln_ffn, d_conv_b.reshape(1, 2 * D_FF), d_ln_ple, d_ln_final, loss_blk]
    dq, dk, dv, received = _attn_bwd(slopes, q, k, v, att, lse, datt, [d_w_out, d_w_up] + rep_late,
                                     (True, True) + (False,) * len(rep_late))
    r_w_out, r_w_up, r_rep = received[0], received[1], list(received[2:])
    dz, grad_x, d_ln_mix = _in_bwd(dq, dk, dv, du, w_in_g, xs, ln_mix, dh1)

    rep_names = ("ln_mix", "pool_w", "pool_scale", "ln_ffn", "conv_b", "ln_ple", "ln_final")
    rep_w = [ln_mix, pool_w0, pool_scale, ln_ffn, conv_b, ln_ple, ln_final.reshape(1, D_MODEL)]
    rep_m = [m_ln_mix, m_pool_w[0], m_pool_scale, m_ln_ffn, m_conv_b, m_ln_ple, m_ln_final.reshape(1, D_MODEL)]
    rep_v = [v_ln_mix, v_pool_w[0], v_pool_scale, v_ln_ffn, v_conv_b, v_ln_ple, v_ln_final.reshape(1, D_MODEL)]
    r_w_in, (r_ln_mix,) = _dw_in_exchange(hn1, dz, [d_ln_mix])
    small = _adamw_small([r_ln_mix] + r_rep[:-1], r_rep[-1], rep_w, rep_m, rep_v)
    loss = small[4][0, 0]

    sharded = {}
    sharded["w_in"] = _adamw("adamw_w_in", r_w_in, w_in[0], m_w_in[0], v_w_in[0])
    sharded["w_out"] = _adamw("adamw_w_out", r_w_out, w_out[0], m_w_out[0], v_w_out[0])
    sharded["w_up"] = [t.T for t in _adamw("adamw_w_up", r_w_up, w_up[0].T, m_w_up[0].T, v_w_up[0].T)]
    sharded["conv_w"] = _adamw("adamw_conv_w", r_conv_w, conv_w[0], m_conv_w[0], v_conv_w[0])
    sharded["w_down"] = _adamw("adamw_w_down", r_w_down, w_down[0], m_w_down[0], v_w_down[0])
    sharded["w_ple_gate"] = _adamw("adamw_w_ple_gate", r_w_pg, w_ple_gate[0], m_w_ple_gate[0], v_w_ple_gate[0])
    sharded["w_ple"] = _adamw("adamw_w_ple", r_w_ple, w_ple[0], m_w_ple[0], v_w_ple[0])

    shapes = dict(w_in=w_in, w_out=w_out, w_up=w_up, conv_w=conv_w, w_down=w_down, w_ple_gate=w_ple_gate, w_ple=w_ple,
                  ln_mix=ln_mix, pool_w=pool_w, pool_scale=pool_scale, ln_ffn=ln_ffn, conv_b=conv_b, ln_ple=ln_ple,
                  ln_final=ln_final)

    def leaf(kind, n):
        src = sharded[n][kind] if n in sharded else small[kind][rep_names.index(n)]
        return src.reshape(shapes[n].shape)

    order = ("ln_mix", "w_in", "pool_w", "pool_scale", "w_out", "ln_ffn", "w_up", "conv_w", "conv_b", "w_down", "ln_ple",
             "w_ple_gate", "w_ple", "ln_final")
    outs = [loss, grad_x[None]]
    for kind in range(4):
        outs += [leaf(kind, n) for n in order]
    return tuple(outs)
```

```python
import jax
import jax.numpy as jnp
from jax import lax
from jax.experimental import pallas as pl
from jax.experimental.pallas import tpu as pltpu

F32 = jnp.float32
BF16 = jnp.bfloat16

N_DEV = 8
D_MODEL = 1024
ATT_WIDTH = 512
POOL_WIDTH = 512
N_HEADS = 8
HEAD_DIM = 64
SPAN = 128
DILATIONS = (1, 4, 16)
POOL_WINDOWS = (2, 4, 8, 16)
POOL_GROUP = 128
D_FF = 2816
FF_SHARD = 2 * D_FF // N_DEV
PLE_DIM = 256
EPS = 1e-6
NEG = -1e30

ADAM_LR = 0.001
ADAM_B1 = 0.9
ADAM_B2 = 0.999
ADAM_EPS = 1e-08
ADAM_WD = 0.01
ADAM_STEP = 10

LANES = 128
HALO = 16
TM = 512
TM_FF = 256
TK = 4096
ATTN_GROUP = 8
VMEM_LIMIT = 56 * 1024 * 1024

MESH = pl.DeviceIdType.MESH
NT = (((1,), (1,)), ((), ()))
TN = (((0,), (0,)), ((), ()))


def _params(*sem):
    return pltpu.CompilerParams(dimension_semantics=sem or None, vmem_limit_bytes=VMEM_LIMIT)


def _const(shape):
    n = len(shape)
    return pl.BlockSpec(shape, lambda *_: (0,) * n, pipeline_mode=pl.Buffered(1))


def _rms(h):
    r = lax.rsqrt(jnp.mean(h * h, axis=-1, keepdims=True) + EPS)
    return r, h * r


def _rms_bwd(r, n, g, dhn):
    dn = dhn * g
    return r * (dn - n * jnp.mean(dn * n, axis=-1, keepdims=True))


def _colsum(a):
    return jnp.sum(a, axis=0, keepdims=True)


def _gather2_copies(ins, outs, send_sems, recv_sems, local_sems):
    n = len(ins)
    x, y, c = lax.axis_index("x"), lax.axis_index("y"), lax.axis_index("c")
    slot = lambda px, py, pc: 4 * px + 2 * py + pc
    chips = [(x, 1 - y), (1 - x, y), (1 - x, 1 - y)]
    first, passed, last = [], [], []

    def remote(a, r, src, dst_slot, to):
        return pltpu.make_async_remote_copy(
            src_ref=src, dst_ref=outs[a].at[dst_slot],
            send_sem=send_sems.at[a * (N_DEV - 1) + r], recv_sem=recv_sems.at[a * (N_DEV - 1) + r],
            device_id=to, device_id_type=MESH)

    for a in range(n):
        mine = pltpu.make_async_copy(ins[a], outs[a].at[slot(x, y, c)], local_sems.at[a])
        to_sibling = remote(a, 0, ins[a], slot(x, y, c), (x, y, 1 - c))
        first += [mine, to_sibling]
        last += [mine.wait, to_sibling.wait_send, to_sibling.wait_recv]
        for r, (px, py) in enumerate(chips, start=1):
            to_chip = remote(a, r, ins[a], slot(x, y, c), (px, py, c))
            onward = remote(a, 3 + r, outs[a].at[slot(px, py, c)], slot(px, py, c), (x, y, 1 - c))
            first.append(to_chip)
            passed.append((to_chip, onward))
            last += [to_chip.wait_send, onward.wait_send, onward.wait_recv]
    return first, passed, last


def _gather2_begin(plan, step, pass_step):
    first, passed, _ = plan

    @pl.when(step == 0)
    def _():
        for cp in first:
            cp.start()

    @pl.when(step == pass_step)
    def _():
        for arrival, cp in passed:
            arrival.wait_recv()
            cp.start()


def _gather2_end(plan, step, nsteps):
    @pl.when(step == nsteps - 1)
    def _():
        for wait in plan[2]:
            wait()


ANY_SPEC = pl.BlockSpec(memory_space=pl.ANY)


def _exchange_shapes(arrays, scatter):
    out = []
    for a, s in zip(arrays, scatter):
        slab = a.shape[1:] if s else a.shape
        out.append(jax.ShapeDtypeStruct((N_DEV,) + tuple(slab), a.dtype))
    return out


def _exchange_sems(n):
    return [pltpu.SemaphoreType.DMA((n * (N_DEV - 1),)), pltpu.SemaphoreType.DMA((n * (N_DEV - 1),)),
            pltpu.SemaphoreType.DMA((n,))]


def _exchange_copies(ins, outs, scatter, send_sems, recv_sems, local_sems):
    n = len(ins)
    x, y, c = lax.axis_index("x"), lax.axis_index("y"), lax.axis_index("c")
    me = 4 * x + 2 * y + c
    copies = []
    for a in range(n):
        src = ins[a].at[me] if scatter[a] else ins[a]
        copies.append(pltpu.make_async_copy(src, outs[a].at[me], local_sems.at[a]))
    for k in range(1, N_DEV):
        px = 1 - x if k & 4 else x
        py = 1 - y if k & 2 else y
        pc = 1 - c if k & 1 else c
        pid = 4 * px + 2 * py + pc
        for a in range(n):
            src = ins[a].at[pid] if scatter[a] else ins[a]
            copies.append(pltpu.make_async_remote_copy(
                src_ref=src, dst_ref=outs[a].at[me],
                send_sem=send_sems.at[a * (N_DEV - 1) + k - 1], recv_sem=recv_sems.at[a * (N_DEV - 1) + k - 1],
                device_id=(px, py, pc), device_id_type=MESH))
    return copies


def _qkvu(x, g1, w_in, shards):
    S = x.shape[0]
    ns = len(shards)
    nsteps = S // TM

    def body(x_ref, g_ref, w_ref, *rest):
        q_ref, k_ref, v_ref, u_ref, hn_ref = rest[ns:ns + 5]
        plan = _gather2_copies(rest[:ns], rest[ns + 5:2 * ns + 5], *rest[2 * ns + 5:])
        _gather2_begin(plan, pl.program_id(0), nsteps - 2)
        r, n = _rms(x_ref[...])
        hn = (n * g_ref[...]).astype(BF16)
        hn_ref[...] = hn
        outs = (q_ref, k_ref, v_ref, u_ref)
        for j in range(N_DEV):
            z = jnp.dot(hn, w_ref[j], preferred_element_type=F32)
            if j < 2:
                z = z * (HEAD_DIM ** -0.5)
            outs[j // 2][:, (j % 2) * 256:(j % 2 + 1) * 256] = z
        _gather2_end(plan, pl.program_id(0), nsteps)

    tok = lambda w: pl.BlockSpec((TM, w), lambda i: (i, 0))
    res = pl.pallas_call(
        body, name="qkvu", grid=(nsteps,),
        in_specs=[tok(D_MODEL), _const((1, D_MODEL)), _const(w_in.shape)] + [ANY_SPEC] * ns,
        out_specs=[tok(512)] * 4 + [tok(D_MODEL)] + [ANY_SPEC] * ns,
        out_shape=[jax.ShapeDtypeStruct((S, 512), F32)] * 4 + [jax.ShapeDtypeStruct((S, D_MODEL), BF16)]
        + _exchange_shapes(shards, (False,) * ns),
        scratch_shapes=_exchange_sems(ns),
        compiler_params=_params("arbitrary"),
    )(x, g1, w_in, *shards)
    return res[:5], res[5:]


def _attn_fill_bias(bias_s, slope_ref, hp, d):
    qi = lax.broadcasted_iota(jnp.int32, (SPAN, 2 * SPAN), 0)
    kj = lax.broadcasted_iota(jnp.int32, (SPAN, 2 * SPAN), 1)
    for t, diff in enumerate((qi + SPAN - kj, qi - kj)):
        valid = (diff >= 0) & (diff <= SPAN)
        dist = diff.astype(F32) * float(d)
        for h in range(2):
            bias_s[t, h * SPAN:(h + 1) * SPAN, :] = jnp.where(valid, -slope_ref[2 * hp + h] * dist, NEG)


def _stack_heads(x, is0):
    return jnp.concatenate([jnp.where(is0, x, 0.0), jnp.where(is0, 0.0, x)], axis=0)


def _unstack_heads(y, is0):
    return jnp.where(is0, y[0:SPAN], y[SPAN:2 * SPAN])


def _attn_block(i, g, d, nb):
    gr = min(d, ATTN_GROUP)
    gn = ATTN_GROUP // gr
    per = d // gr
    r = (i & (per - 1)) * gr + g % gr
    n = (i >> (per.bit_length() - 1)) + (g // gr) * (nb // gn)
    k0 = jnp.maximum(n - 1, 0)

    def ds(block, nrows):
        start = block * (SPAN * d) + r
        return pl.ds(start, nrows, stride=d) if d > 1 else pl.ds(start, nrows)

    return ds(n, SPAN), ds(k0, 2 * SPAN), jnp.where(n == 0, 1, 0)


def _attn_groups(S, d):
    nb = S // d // SPAN
    gn = ATTN_GROUP // min(d, ATTN_GROUP)
    assert nb >= 2 and nb % gn == 0 and (gn == 1 or nb // gn >= 2)
    return nb, d * nb // ATTN_GROUP


def _attn_fwd(slopes, q, k, v, shards):
    S = q.shape[0]
    ns = len(shards)
    steps = ATT_WIDTH // LANES

    def body(slope_ref, q_ref, k_ref, v_ref, *rest):
        o_ref, lse_ref = rest[ns:ns + 2]
        m_s, l_s, bias_s = rest[2 * ns + 2:2 * ns + 5]
        hp = pl.program_id(0)
        plan = _gather2_copies(rest[:ns], rest[ns + 2:2 * ns + 2], *rest[2 * ns + 5:])
        _gather2_begin(plan, hp, steps - 1)

        is0 = lax.broadcasted_iota(jnp.int32, (SPAN, LANES), 1) < HEAD_DIM
        for pi, d in enumerate(DILATIONS):
            nb, ngroups = _attn_groups(S, d)
            _attn_fill_bias(bias_s, slope_ref, hp, d)

            def group(i, carry, d=d, pi=pi, nb=nb):
                blocks = [_attn_block(i, g, d, nb) for g in range(ATTN_GROUP)]
                loaded = [(q_ref[rows, :], k_ref[krows, :].astype(BF16), v_ref[krows, :].astype(BF16))
                          for rows, krows, _ in blocks]
                new = []
                for (rows, krows, tab), (qb, kb, vb) in zip(blocks, loaded):
                    qs = _stack_heads(qb, is0).astype(BF16)
                    s = lax.dot_general(qs, kb, NT, preferred_element_type=F32) + bias_s[tab]
                    m = jnp.max(s, axis=-1, keepdims=True)
                    e = jnp.exp(s - m)
                    l = jnp.sum(e, axis=-1, keepdims=True)
                    pv = jnp.dot(e.astype(BF16), vb, preferred_element_type=F32)
                    new.append([_unstack_heads(jnp.broadcast_to(m, pv.shape), is0),
                                _unstack_heads(jnp.broadcast_to(l, pv.shape), is0), _unstack_heads(pv, is0)])
                if pi > 0:
                    old = [(m_s[rows, :], l_s[rows, :], o_ref[rows, :]) for rows, _, _ in blocks]
                    for st, (m_o, l_o, o_o) in zip(new, old):
                        m_n = jnp.maximum(m_o, st[0])
                        a_o = jnp.exp(m_o - m_n)
                        a_b = jnp.exp(st[0] - m_n)
                        st[:] = [m_n, a_o * l_o + a_b * st[1], a_o * o_o + a_b * st[2]]
                for (rows, _, _), (m_b, l_b, acc) in zip(blocks, new):
                    if pi == len(DILATIONS) - 1:
                        o_ref[rows, :] = acc / l_b
                        lse_ref[rows, :] = m_b + jnp.log(l_b)
                    else:
                        o_ref[rows, :] = acc
                        m_s[rows, :] = m_b
                        l_s[rows, :] = l_b
                return carry

            lax.fori_loop(0, ngroups, group, 0)

        _gather2_end(plan, hp, steps)

    col = pl.BlockSpec((S, LANES), lambda i: (0, i))
    res = pl.pallas_call(
        body, name="attn_fwd", grid=(steps,),
        in_specs=[pl.BlockSpec(memory_space=pltpu.SMEM), col, col, col] + [ANY_SPEC] * ns,
        out_specs=[col, col] + [ANY_SPEC] * ns,
        out_shape=[jax.ShapeDtypeStruct((S, ATT_WIDTH), F32)] * 2 + _exchange_shapes(shards, (False,) * ns),
        scratch_shapes=[pltpu.VMEM((S, LANES), F32), pltpu.VMEM((S, LANES), F32),
                        pltpu.VMEM((2, 2 * SPAN, 2 * SPAN), F32)] + _exchange_sems(ns),
        compiler_params=_params("arbitrary"),
    )(slopes, q, k, v, *shards)
    return res[0], res[1], res[2:]


def _pool_count(i, w):
    t = i * TM + lax.broadcasted_iota(jnp.int32, (TM, 1), 0)
    return jnp.minimum(t + 1, w).astype(F32)


def _mix_out(x, att, u, pool_w, pool_scale, w_out, shards):
    S = x.shape[0]
    ns = len(shards)
    nsteps = S // TM

    def body(x_ref, att_ref, u_ref, pw_ref, ps_ref, w_ref, *rest):
        h1_ref, mix_ref, dlt_ref = rest[ns:ns + 3]
        ubuf = rest[2 * ns + 3]
        i = pl.program_id(0)
        plan = _gather2_copies(rest[:ns], rest[ns + 3:2 * ns + 3], *rest[2 * ns + 4:])
        _gather2_begin(plan, i, nsteps - 1)

        @pl.when(i == 0)
        def _():
            ubuf[0:HALO, :] = jnp.zeros((HALO, POOL_WIDTH), F32)

        ubuf[HALO:HALO + TM, :] = u_ref[...]
        mix_ref[:, 0:ATT_WIDTH] = att_ref[...].astype(BF16)
        for g, w in enumerate(POOL_WINDOWS):
            cols = slice(g * POOL_GROUP, (g + 1) * POOL_GROUP)
            ug = ubuf[HALO:HALO + TM, cols]
            acc = ug
            for j in range(1, w):
                acc = acc + ubuf[HALO - j:HALO - j + TM, cols]
            dlt = (acc / _pool_count(i, w) - ug).astype(BF16)
            dlt_ref[:, cols] = dlt
            yg = jnp.dot(dlt, pw_ref[g].astype(BF16), preferred_element_type=F32) * ps_ref[:, cols]
            mix_ref[:, ATT_WIDTH + g * POOL_GROUP:ATT_WIDTH + (g + 1) * POOL_GROUP] = yg.astype(BF16)
        ubuf[0:HALO, :] = ubuf[TM:TM + HALO, :]
        h1_ref[...] = x_ref[...] + jnp.dot(mix_ref[...], w_ref[...], preferred_element_type=F32)
        _gather2_end(plan, i, nsteps)

    tok = lambda w: pl.BlockSpec((TM, w), lambda i: (i, 0))
    res = pl.pallas_call(
        body, name="mix_out", grid=(nsteps,),
        in_specs=[tok(D_MODEL), tok(ATT_WIDTH), tok(POOL_WIDTH), _const(pool_w.shape), _const((1, POOL_WIDTH)),
                  _const(w_out.shape)] + [ANY_SPEC] * ns,
        out_specs=[tok(D_MODEL), tok(D_MODEL), tok(POOL_WIDTH)] + [ANY_SPEC] * ns,
        out_shape=[jax.ShapeDtypeStruct((S, D_MODEL), F32), jax.ShapeDtypeStruct((S, D_MODEL), BF16),
                   jax.ShapeDtypeStruct((S, POOL_WIDTH), BF16)] + _exchange_shapes(shards, (False,) * ns),
        scratch_shapes=[pltpu.VMEM((TM + HALO, POOL_WIDTH), F32)] + _exchange_sems(ns),
        compiler_params=_params("arbitrary"),
    )(x, att, u, pool_w, pool_scale, w_out, *shards)
    return res[0], res[1], res[2], res[3:]


def _conv_fwd(stage, upre, prev, cw, cb):
    T = upre.shape[0]
    stage[0:HALO, :] = prev
    stage[HALO:HALO + T, :] = upre
    return cb + cw[0:1, :] * stage[HALO - 2:HALO - 2 + T, :] + cw[1:2, :] * stage[HALO - 1:HALO - 1 + T, :] + cw[2:3, :] * upre


def _ffn_fwd(h1, g2, w_up, conv_w, conv_b, w_down, shards):
    S = h1.shape[0]
    T = TM_FF
    ns = len(shards)
    nsteps = S // T

    def body(h1_ref, g_ref, wu_ref, cw_ref, cb_ref, wd_ref, *rest):
        h2_ref, hn_ref, up_ref, upc_ref = rest[ns:ns + 4]
        carry, stage = rest[2 * ns + 4:2 * ns + 6]
        i = pl.program_id(0)
        plan = _gather2_copies(rest[:ns], rest[ns + 4:2 * ns + 4], *rest[2 * ns + 6:])
        _gather2_begin(plan, i, nsteps // 2)

        @pl.when(i == 0)
        def _():
            carry[...] = jnp.zeros(carry.shape, F32)

        h1t = h1_ref[...]
        r, n = _rms(h1t)
        hn = (n * g_ref[...]).astype(BF16)
        hn_ref[...] = hn
        acc = h1t
        for j in range(4):
            conv = []
            for jj in (j, j + 4):
                upre = jnp.dot(hn, wu_ref[jj], preferred_element_type=F32)
                up_ref[jj] = upre.astype(BF16)
                conv.append(_conv_fwd(stage, upre, carry[jj], cw_ref[jj], cb_ref[jj]))
                upc_ref[jj] = conv[-1].astype(BF16)
                carry[jj] = stage[T:T + HALO, :]
            gate, val = conv
            a = gate * jax.nn.sigmoid(gate) * val
            acc = acc + jnp.dot(a.astype(BF16), wd_ref[j], preferred_element_type=F32)
        h2_ref[...] = acc
        _gather2_end(plan, i, nsteps)

    tok = lambda w: pl.BlockSpec((T, w), lambda i: (i, 0))
    res = pl.pallas_call(
        body, name="ffn_fwd", grid=(nsteps,),
        in_specs=[tok(D_MODEL), _const((1, D_MODEL)), _const(w_up.shape), _const(conv_w.shape), _const(conv_b.shape),
                  _const(w_down.shape)] + [ANY_SPEC] * ns,
        out_specs=[tok(D_MODEL), tok(D_MODEL)] + [pl.BlockSpec((N_DEV, T, FF_SHARD), lambda i: (0, i, 0))] * 2
        + [ANY_SPEC] * ns,
        out_shape=[jax.ShapeDtypeStruct((S, D_MODEL), F32), jax.ShapeDtypeStruct((S, D_MODEL), BF16)]
        + [jax.ShapeDtypeStruct((N_DEV, S, FF_SHARD), BF16)] * 2 + _exchange_shapes(shards, (False,) * ns),
        scratch_shapes=[pltpu.VMEM((N_DEV, HALO, FF_SHARD), F32), pltpu.VMEM((T + HALO, FF_SHARD), F32)]
        + _exchange_sems(ns),
        compiler_params=_params("arbitrary"),
    )(h1, g2, w_up, conv_w, conv_b, w_down, *shards)
    return res[0], res[1], res[2], res[3], res[4:]


def _head(h2, p, g3, w_pg, w_ple, g4, target):
    S = h2.shape[0]
    nt = S // TM

    def body(h2_ref, p_ref, g3_ref, wpg_ref, wple_ref, g4_ref, t_ref,
             loss_ref, dh2_ref, dh2b_ref, hn3_ref, dgl_ref, dpe_ref, dg3_ref, dg4_ref, lacc):
        i = pl.program_id(0)

        @pl.when(i == 0)
        def _():
            lacc[...] = jnp.zeros(lacc.shape, F32)
            dg3_ref[...] = jnp.zeros(dg3_ref.shape, F32)
            dg4_ref[...] = jnp.zeros(dg4_ref.shape, F32)

        h2t = h2_ref[...]
        g3, g4 = g3_ref[...], g4_ref[...]
        r3, n3 = _rms(h2t)
        hn3 = (n3 * g3).astype(BF16)
        hn3_ref[...] = hn3
        gs = jax.nn.sigmoid(jnp.dot(hn3, wpg_ref[...], preferred_element_type=F32))
        pe = jnp.dot(p_ref[...].astype(BF16), wple_ref[...], preferred_element_type=F32)
        h3 = h2t + gs * pe
        r4, n4 = _rms(h3)
        err = n4 * g4 - t_ref[...]
        lacc[...] += _colsum(err * err)
        dy = err * (1.0 / D_MODEL)
        dg4_ref[...] += _colsum(dy * n4)
        dh3 = _rms_bwd(r4, n4, g4, dy)
        dpe_ref[...] = (dh3 * gs).astype(BF16)
        dgl = (dh3 * pe * gs * (1.0 - gs)).astype(BF16)
        dgl_ref[...] = dgl
        dhn3 = lax.dot_general(dgl, wpg_ref[...], NT, preferred_element_type=F32)
        dg3_ref[...] += _colsum(dhn3 * n3)
        dh2 = dh3 + _rms_bwd(r3, n3, g3, dhn3)
        dh2_ref[...] = dh2
        dh2b_ref[...] = dh2.astype(BF16)

        @pl.when(i == nt - 1)
        def _():
            tot = 0.5 / D_MODEL * jnp.sum(lacc[...], axis=-1, keepdims=True)
            loss_ref[...] = jnp.broadcast_to(tot, loss_ref.shape)

    tok = lambda w: pl.BlockSpec((TM, w), lambda i: (i, 0))
    row = pl.BlockSpec((1, D_MODEL), lambda i: (0, 0))
    act = lambda dt: jax.ShapeDtypeStruct((S, D_MODEL), dt)
    return pl.pallas_call(
        body, name="head", grid=(nt,),
        in_specs=[tok(D_MODEL), tok(PLE_DIM), _const((1, D_MODEL)), _const(w_pg.shape), _const(w_ple.shape),
                  _const((1, D_MODEL)), tok(D_MODEL)],
        out_specs=[pl.BlockSpec((8, LANES), lambda i: (0, 0)), tok(D_MODEL), tok(D_MODEL), tok(D_MODEL), tok(D_MODEL),
                   tok(D_MODEL), row, row],
        out_shape=[jax.ShapeDtypeStruct((8, LANES), F32), act(F32), act(BF16), act(BF16), act(BF16), act(BF16),
                   jax.ShapeDtypeStruct((1, D_MODEL), F32), jax.ShapeDtypeStruct((1, D_MODEL), F32)],
        scratch_shapes=[pltpu.VMEM((1, D_MODEL), F32)],
        compiler_params=_params("arbitrary"),
    )(h2, p, g3, w_pg, w_ple, g4, target)


def _wgrad(name, x, dy, x_kind, dy_kind, nj, k_dim, n_dim):
    S = x.shape[-2]
    nt = S // TK

    def spec(kind, width):
        if kind == "full":
            return pl.BlockSpec((TK, width), lambda j, t: (t, 0))
        return pl.BlockSpec((None, TK, width), lambda j, t: (j, t, 0))

    def body(x_ref, dy_ref, o_ref, acc):
        t = pl.program_id(1)

        @pl.when(t == 0)
        def _():
            acc[...] = jnp.zeros(acc.shape, F32)

        acc[...] += lax.dot_general(x_ref[...].astype(BF16), dy_ref[...], TN, preferred_element_type=F32)

        @pl.when(t == nt - 1)
        def _():
            o_ref[...] = acc[...].astype(BF16)

    return pl.pallas_call(
        body, name=name, grid=(nj, nt),
        in_specs=[spec(x_kind, k_dim), spec(dy_kind, n_dim)],
        out_specs=pl.BlockSpec((None, k_dim, n_dim), lambda j, t: (j, 0, 0)),
        out_shape=jax.ShapeDtypeStruct((nj, k_dim, n_dim), BF16),
        scratch_shapes=[pltpu.VMEM((k_dim, n_dim), F32)],
        compiler_params=_params("arbitrary", "arbitrary"),
    )(x, dy)


def _row_picker(T, off0, off1):
    r = lax.broadcasted_iota(jnp.int32, (2 * T, T + HALO), 0)
    c = lax.broadcasted_iota(jnp.int32, (2 * T, T + HALO), 1)
    want = jnp.where(r < T, r + off0, r - T + off1)
    return jnp.where(c == want, 1.0, 0.0).astype(BF16)


def _ffn_bwd_a(dh2b, up, upc, w_down, grads):
    S = dh2b.shape[0]
    T = TM_FF
    hb = T // HALO
    nsteps = S // T
    ng = len(grads)

    def body(dh_ref, up_ref, halo_ref, upc_ref, wd_ref, *rest):
        a_ref, dup_ref, dcw_ref, dcb_ref = rest[ng:ng + 4]
        stage = rest[2 * ng + 4]
        i = pl.program_id(0)
        copies = _exchange_copies(rest[:ng], rest[ng + 4:2 * ng + 4], (True,) * ng, *rest[2 * ng + 5:])

        @pl.when(i == 0)
        def _():
            dcw_ref[...] = jnp.zeros(dcw_ref.shape, F32)
            dcb_ref[...] = jnp.zeros(dcb_ref.shape, F32)
            for cp in copies:
                cp.start()

        @pl.when(i == nsteps - 1)
        def _():
            for cp in copies:
                cp.wait()

        dh = dh_ref[...]
        pick = _row_picker(T, HALO - 2, HALO - 1)
        for j in range(4):
            da = lax.dot_general(dh, wd_ref[j], NT, preferred_element_type=F32)
            taps = []
            for jj in (j, j + 4):
                upre = up_ref[jj]
                stage[0:HALO, :] = jnp.where(i > 0, halo_ref[jj], jnp.zeros((HALO, FF_SHARD), BF16))
                stage[HALO:HALO + T, :] = upre
                prv = jnp.dot(pick, stage[...], preferred_element_type=F32)
                taps.append((prv[0:T], prv[T:2 * T], upre.astype(F32)))
            gate, val = upc_ref[j].astype(F32), upc_ref[j + 4].astype(F32)
            sg = jax.nn.sigmoid(gate)
            silu = gate * sg
            a_ref[j] = (silu * val).astype(BF16)
            dgate = (da * val) * (sg + silu * (1.0 - sg))
            dval = da * silu
            for jj, dup, tp in ((j, dgate, taps[0]), (j + 4, dval, taps[1])):
                dup_ref[jj] = dup.astype(BF16)
                dcb_ref[jj] += _colsum(dup)
                for kk in range(3):
                    dcw_ref[jj, kk:kk + 1, :] += _colsum(dup * tp[kk])

    tok = lambda w: pl.BlockSpec((T, w), lambda i: (i, 0))
    shard = pl.BlockSpec((N_DEV, T, FF_SHARD), lambda i: (0, i, 0))
    res = pl.pallas_call(
        body, name="ffn_bwd_a", grid=(nsteps,),
        in_specs=[tok(D_MODEL), shard,
                  pl.BlockSpec((N_DEV, HALO, FF_SHARD), lambda i: (0, jnp.maximum(i * hb - 1, 0), 0)),
                  shard, _const(w_down.shape)] + [ANY_SPEC] * ng,
        out_specs=[pl.BlockSpec((4, T, FF_SHARD), lambda i: (0, i, 0)), shard,
                   pl.BlockSpec((N_DEV, 3, FF_SHARD), lambda i: (0, 0, 0)),
                   pl.BlockSpec((N_DEV, 1, FF_SHARD), lambda i: (0, 0, 0))] + [ANY_SPEC] * ng,
        out_shape=[jax.ShapeDtypeStruct((4, S, FF_SHARD), BF16), jax.ShapeDtypeStruct((N_DEV, S, FF_SHARD), BF16),
                   jax.ShapeDtypeStruct((N_DEV, 3, FF_SHARD), F32), jax.ShapeDtypeStruct((N_DEV, 1, FF_SHARD), F32)]
        + _exchange_shapes(grads, (True,) * ng),
        scratch_shapes=[pltpu.VMEM((T + HALO, FF_SHARD), BF16)] + _exchange_sems(ng),
        compiler_params=_params("arbitrary"),
    )(dh2b, up, up, upc, w_down, *grads)
    return res[0], res[1], res[2], res[3], res[4:]


def _ffn_bwd_b(dup, conv_w, w_up, h1, g2, dh2, grads):
    S = h1.shape[0]
    T = TM_FF
    hb = T // HALO
    nt = S // T
    ng = len(grads)

    def body(dup_ref, halo_ref, cw_ref, wu_ref, h1_ref, g_ref, dh2_ref, *rest):
        dpre_ref, dh1_ref, dh1b_ref, dg_ref = rest[ng:ng + 4]
        stage = rest[2 * ng + 4]
        i = pl.program_id(0)
        copies = _exchange_copies(rest[:ng], rest[ng + 4:2 * ng + 4], (True,) * ng, *rest[2 * ng + 5:])

        @pl.when(i == 0)
        def _():
            dg_ref[...] = jnp.zeros(dg_ref.shape, F32)
            for cp in copies:
                cp.start()

        dhn = jnp.zeros((T, D_MODEL), F32)
        for jj in range(N_DEV):
            dup = dup_ref[jj].astype(F32)
            stage[0:T, :] = dup
            stage[T:T + HALO, :] = jnp.where(i < nt - 1, halo_ref[jj].astype(F32), 0.0)
            cw = cw_ref[jj]
            dpre = (cw[2:3, :] * dup + cw[1:2, :] * stage[1:1 + T, :] + cw[0:1, :] * stage[2:2 + T, :]).astype(BF16)
            dpre_ref[jj] = dpre
            dhn = dhn + lax.dot_general(dpre, wu_ref[jj], NT, preferred_element_type=F32)
        g = g_ref[...]
        r, n = _rms(h1_ref[...])
        dg_ref[...] += _colsum(dhn * n)
        dh1 = dh2_ref[...] + _rms_bwd(r, n, g, dhn)
        dh1_ref[...] = dh1
        dh1b_ref[...] = dh1.astype(BF16)

        @pl.when(i == nt - 1)
        def _():
            for cp in copies:
                cp.wait()

    tok = lambda w: pl.BlockSpec((T, w), lambda i: (i, 0))
    shard = pl.BlockSpec((N_DEV, T, FF_SHARD), lambda i: (0, i, 0))
    res = pl.pallas_call(
        body, name="ffn_bwd_b", grid=(nt,),
        in_specs=[shard,
                  pl.BlockSpec((N_DEV, HALO, FF_SHARD), lambda i: (0, jnp.minimum((i + 1) * hb, S // HALO - 1), 0)),
                  _const(conv_w.shape), _const(w_up.shape), tok(D_MODEL), _const((1, D_MODEL)), tok(D_MODEL)]
        + [ANY_SPEC] * ng,
        out_specs=[shard, tok(D_MODEL), tok(D_MODEL), pl.BlockSpec((1, D_MODEL), lambda i: (0, 0))] + [ANY_SPEC] * ng,
        out_shape=[jax.ShapeDtypeStruct((N_DEV, S, FF_SHARD), BF16), jax.ShapeDtypeStruct((S, D_MODEL), F32),
                   jax.ShapeDtypeStruct((S, D_MODEL), BF16), jax.ShapeDtypeStruct((1, D_MODEL), F32)]
        + _exchange_shapes(grads, (True,) * ng),
        scratch_shapes=[pltpu.VMEM((T + HALO, FF_SHARD), F32)] + _exchange_sems(ng),
        compiler_params=_params("arbitrary"),
    )(dup, dup, conv_w, w_up, h1, g2, dh2, *grads)
    return res[0], res[1], res[2], res[3], res[4:]


def _mix_bwd(dh1b, w_out, dlt, pool_w, pool_scale):
    S = dh1b.shape[0]
    nt = S // TM

    def body(dh_ref, w_ref, dlt_ref, pw_ref, ps_ref, datt_ref, du_ref, dpw_ref, dps_ref, stage, carry):
        i = pl.program_id(0)
        tile = nt - 1 - i

        @pl.when(i == 0)
        def _():
            dpw_ref[...] = jnp.zeros(dpw_ref.shape, F32)
            dps_ref[...] = jnp.zeros(dps_ref.shape, F32)
            carry[...] = jnp.zeros(carry.shape, F32)

        dmix = lax.dot_general(dh_ref[...], w_ref[...], NT, preferred_element_type=F32)
        datt_ref[...] = dmix[:, 0:ATT_WIDTH]
        for g, w in enumerate(POOL_WINDOWS):
            cols = slice(g * POOL_GROUP, (g + 1) * POOL_GROUP)
            dpool = dmix[:, ATT_WIDTH + g * POOL_GROUP:ATT_WIDTH + (g + 1) * POOL_GROUP]
            dl = dlt_ref[:, cols]
            pw = pw_ref[g].astype(BF16)
            yg = jnp.dot(dl, pw, preferred_element_type=F32)
            dps_ref[:, cols] += _colsum(dpool * yg)
            dy = (dpool * ps_ref[:, cols]).astype(BF16)
            dpw_ref[g] += lax.dot_general(dl, dy, TN, preferred_element_type=F32)
            ddlt = lax.dot_general(dy, pw, NT, preferred_element_type=F32)
            cg = ddlt / _pool_count(tile, w)
            stage[0:TM, :] = cg
            stage[TM:TM + HALO, :] = carry[:, cols]
            acc = cg
            for j in range(1, w):
                acc = acc + stage[j:j + TM, :]
            du_ref[:, cols] = acc - ddlt
            carry[:, cols] = cg[0:HALO, :]

    tok = lambda w: pl.BlockSpec((TM, w), lambda i: (nt - 1 - i, 0))
    return pl.pallas_call(
        body, name="mix_bwd", grid=(nt,),
        in_specs=[tok(D_MODEL), _const(w_out.shape), tok(POOL_WIDTH), _const(pool_w.shape), _const((1, POOL_WIDTH))],
        out_specs=[tok(ATT_WIDTH), tok(POOL_WIDTH), pl.BlockSpec(pool_w.shape, lambda i: (0, 0, 0)),
                   pl.BlockSpec((1, POOL_WIDTH), lambda i: (0, 0))],
        out_shape=[jax.ShapeDtypeStruct((S, ATT_WIDTH), F32), jax.ShapeDtypeStruct((S, POOL_WIDTH), F32),
                   jax.ShapeDtypeStruct(pool_w.shape, F32), jax.ShapeDtypeStruct((1, POOL_WIDTH), F32)],
        scratch_shapes=[pltpu.VMEM((TM + HALO, POOL_GROUP), F32), pltpu.VMEM((HALO, POOL_WIDTH), F32)],
        compiler_params=_params("arbitrary"),
    )(dh1b, w_out, dlt, pool_w, pool_scale)


def _attn_bwd(slopes, q, k, v, o, lse, do, grads, scatter):
    S = q.shape[0]
    CH = 512
    ng = len(grads)
    steps = ATT_WIDTH // LANES

    def body(slope_ref, q_ref, k_ref, v_ref, o_ref, lse_ref, do_ref, *rest):
        dq_ref, dk_ref, dv_ref = rest[ng:ng + 3]
        dl_s, bias_s = rest[2 * ng + 3:2 * ng + 5]
        hp = pl.program_id(0)
        copies = _exchange_copies(rest[:ng], rest[ng + 3:2 * ng + 3], scatter, *rest[2 * ng + 5:])

        @pl.when(hp == 0)
        def _():
            for cp in copies:
                cp.start()

        is0 = lax.broadcasted_iota(jnp.int32, (SPAN, LANES), 1) < HEAD_DIM
        is0c = lax.broadcasted_iota(jnp.int32, (CH, LANES), 1) < HEAD_DIM

        def prep(ci, carry):
            rows = pl.ds(pl.multiple_of(ci * CH, CH), CH)
            prod = do_ref[rows, :] * o_ref[rows, :]
            d0 = jnp.sum(jnp.where(is0c, prod, 0.0), axis=-1, keepdims=True)
            d1 = jnp.sum(jnp.where(is0c, 0.0, prod), axis=-1, keepdims=True)
            dl_s[rows, :] = jnp.where(is0c, d0, d1)
            zero = jnp.zeros((CH, LANES), F32)
            dq_ref[rows, :] = zero
            dk_ref[rows, :] = zero
            dv_ref[rows, :] = zero
            return carry

        lax.fori_loop(0, S // CH, prep, 0)

        for d in DILATIONS:
            nb, ngroups = _attn_groups(S, d)
            _attn_fill_bias(bias_s, slope_ref, hp, d)

            def group(i, carry, d=d, nb=nb):
                blocks = [_attn_block(i, g, d, nb) for g in range(ATTN_GROUP)]
                loaded = [(q_ref[rows, :], do_ref[rows, :], lse_ref[rows, :], dl_s[rows, :], k_ref[krows, :],
                           v_ref[krows, :].astype(BF16)) for rows, krows, _ in blocks]
                new = []
                for (rows, krows, tab), (qb, dob, lse_b, dl_b, kf, vb) in zip(blocks, loaded):
                    kb = kf.astype(BF16)
                    qs = _stack_heads(qb, is0).astype(BF16)
                    dos = _stack_heads(dob, is0).astype(BF16)
                    lse_s = jnp.concatenate([lse_b[:, 0:1], lse_b[:, HEAD_DIM:HEAD_DIM + 1]], axis=0)
                    dl_s2 = jnp.concatenate([dl_b[:, 0:1], dl_b[:, HEAD_DIM:HEAD_DIM + 1]], axis=0)
                    s = lax.dot_general(qs, kb, NT, preferred_element_type=F32) + bias_s[tab]
                    pr = jnp.exp(s - lse_s)
                    dp = lax.dot_general(dos, vb, NT, preferred_element_type=F32)
                    ds = (pr * (dp - dl_s2)).astype(BF16)
                    dv_c = lax.dot_general(pr.astype(BF16), dos, TN, preferred_element_type=F32)
                    dk_c = lax.dot_general(ds, qs, TN, preferred_element_type=F32)
                    dq_c = _unstack_heads(jnp.dot(ds, kb, preferred_element_type=F32), is0)
                    new.append((dq_c, dk_c, dv_c))
                old = [(dq_ref[rows, :], dk_ref[krows, :], dv_ref[krows, :]) for rows, krows, _ in blocks]
                for (rows, krows, _), (dq_c, dk_c, dv_c), (dq_o, dk_o, dv_o) in zip(blocks, new, old):
                    dq_ref[rows, :] = dq_o + dq_c
                    dk_ref[krows, :] = dk_o + dk_c
                    dv_ref[krows, :] = dv_o + dv_c
                return carry

            lax.fori_loop(0, ngroups, group, 0)

        @pl.when(hp == steps - 1)
        def _():
            for cp in copies:
                cp.wait()

    col = pl.BlockSpec((S, LANES), lambda i: (0, i))
    res = pl.pallas_call(
        body, name="attn_bwd", grid=(steps,),
        in_specs=[pl.BlockSpec(memory_space=pltpu.SMEM)] + [col] * 6 + [ANY_SPEC] * ng,
        out_specs=[col] * 3 + [ANY_SPEC] * ng,
        out_shape=[jax.ShapeDtypeStruct((S, ATT_WIDTH), F32)] * 3 + _exchange_shapes(grads, scatter),
        scratch_shapes=[pltpu.VMEM((S, LANES), F32), pltpu.VMEM((2, 2 * SPAN, 2 * SPAN), F32)] + _exchange_sems(ng),
        compiler_params=_params("arbitrary"),
    )(slopes, q, k, v, o, lse, do, *grads)
    return res[0], res[1], res[2], res[3:]


def _in_bwd(dq, dk, dv, du, w_in, x, g1, dh1):
    S = x.shape[0]

    def body(dq_ref, dk_ref, dv_ref, du_ref, w_ref, x_ref, g_ref, dh1_ref, dz_ref, dx_ref, dg_ref):
        @pl.when(pl.program_id(0) == 0)
        def _():
            dg_ref[...] = jnp.zeros(dg_ref.shape, F32)

        srcs = (dq_ref, dk_ref, dv_ref, du_ref)
        dhn = jnp.zeros((TM, D_MODEL), F32)
        for j in range(N_DEV):
            dz = srcs[j // 2][:, (j % 2) * 256:(j % 2 + 1) * 256]
            if j < 2:
                dz = dz * (HEAD_DIM ** -0.5)
            dz = dz.astype(BF16)
            dz_ref[j] = dz
            dhn = dhn + lax.dot_general(dz, w_ref[j], NT, preferred_element_type=F32)
        g = g_ref[...]
        r, n = _rms(x_ref[...])
        dg_ref[...] += _colsum(dhn * n)
        dx_ref[...] = dh1_ref[...] + _rms_bwd(r, n, g, dhn)

    tok = lambda w: pl.BlockSpec((TM, w), lambda i: (i, 0))
    return pl.pallas_call(
        body, name="in_bwd", grid=(S // TM,),
        in_specs=[tok(512)] * 4 + [_const(w_in.shape), tok(D_MODEL), _const((1, D_MODEL)), tok(D_MODEL)],
        out_specs=[pl.BlockSpec((N_DEV, TM, 256), lambda i: (0, i, 0)), tok(D_MODEL),
                   pl.BlockSpec((1, D_MODEL), lambda i: (0, 0))],
        out_shape=[jax.ShapeDtypeStruct((N_DEV, S, 256), BF16), jax.ShapeDtypeStruct((S, D_MODEL), F32),
                   jax.ShapeDtypeStruct((1, D_MODEL), F32)],
        compiler_params=_params("arbitrary"),
    )(dq, dk, dv, du, w_in, x, g1, dh1)


def _adamw(name, parts, w, m, v):
    R, C = w.shape
    rb = R
    for cand in (256, 128, 64, 32, 16, 8):
        if R % cand == 0 and R > cand:
            rb = cand
            break

    def body(p_ref, w_ref, m_ref, v_ref, g_ref, d_ref, mo_ref, vo_ref):
        g = p_ref[0].astype(F32)
        for s in range(1, N_DEV):
            g = g + p_ref[s].astype(F32)
        g_ref[...] = g
        d_ref[...], mo_ref[...], vo_ref[...] = _adam_update(g, w_ref[...], m_ref[...], v_ref[...])

    blk = pl.BlockSpec((rb, C), lambda i: (i, 0))
    return pl.pallas_call(
        body, name=name, grid=(R // rb,),
        in_specs=[pl.BlockSpec((N_DEV, rb, C), lambda i: (0, i, 0)), blk, blk, blk],
        out_specs=[blk] * 4,
        out_shape=[jax.ShapeDtypeStruct((R, C), F32)] * 4,
        compiler_params=_params("arbitrary"),
    )(parts, w, m, v)


def _adam_update(g, w, m, v):
    m_new = ADAM_B1 * m + (1.0 - ADAM_B1) * g
    v_new = ADAM_B2 * v + (1.0 - ADAM_B2) * (g * g)
    m_hat = m_new / (1.0 - ADAM_B1 ** ADAM_STEP)
    v_hat = v_new / (1.0 - ADAM_B2 ** ADAM_STEP)
    return -ADAM_LR * (m_hat / (jnp.sqrt(v_hat) + ADAM_EPS) + ADAM_WD * w), m_new, v_new


def _adamw_small(parts, loss_parts, ws, ms, vs):
    n = len(ws)

    def body(*refs):
        p_refs, lp_ref = refs[:n], refs[n]
        w_refs, m_refs, v_refs = refs[n + 1:2 * n + 1], refs[2 * n + 1:3 * n + 1], refs[3 * n + 1:4 * n + 1]
        outs = refs[4 * n + 1:]
        for i in range(n):
            g = p_refs[i][0]
            for s in range(1, N_DEV):
                g = g + p_refs[i][s]
            d, m_new, v_new = _adam_update(g, w_refs[i][...], m_refs[i][...], v_refs[i][...])
            outs[i][...] = g
            outs[n + i][...] = d
            outs[2 * n + i][...] = m_new
            outs[3 * n + i][...] = v_new
        tot = lp_ref[0]
        for s in range(1, N_DEV):
            tot = tot + lp_ref[s]
        outs[4 * n][...] = tot

    shapes = [jax.ShapeDtypeStruct(w.shape, F32) for w in ws]
    res = pl.pallas_call(
        body, name="adamw_replicated",
        out_shape=shapes * 4 + [jax.ShapeDtypeStruct(loss_parts.shape[1:], F32)],
        compiler_params=_params(),
    )(*parts, loss_parts, *ws, *ms, *vs)
    return res[:n], res[n:2 * n], res[2 * n:3 * n], res[3 * n:4 * n], res[4 * n]


def _gather2(name, arrays):
    n = len(arrays)

    def body(*refs):
        first, passed, last = _gather2_copies(refs[:n], refs[n:2 * n], *refs[2 * n:])
        for cp in first:
            cp.start()
        for arrival, cp in passed:
            arrival.wait_recv()
            cp.start()
        for wait in last:
            wait()

    return pl.pallas_call(
        body, name=name,
        in_specs=[ANY_SPEC] * n, out_specs=[ANY_SPEC] * n, out_shape=_exchange_shapes(arrays, (False,) * n),
        scratch_shapes=_exchange_sems(n),
    )(*arrays)


def _dw_in_exchange(hn, dz, small):
    S = hn.shape[0]
    nt = S // TK
    ns = len(small)
    kd, nd = hn.shape[1], dz.shape[2]
    me_arr = (4 * lax.axis_index("x") + 2 * lax.axis_index("y") + lax.axis_index("c")).astype(jnp.int32).reshape(1)

    def body(me_ref, x_ref, dy_ref, *rest):
        recv_ref = rest[ns]
        acc, stage, send_sems, recv_sems, own_sem = rest[2 * ns + 1:2 * ns + 6]
        j, t = pl.program_id(0), pl.program_id(1)
        x, y, c = lax.axis_index("x"), lax.axis_index("y"), lax.axis_index("c")
        me = 4 * x + 2 * y + c
        small_copies = _exchange_copies(rest[:ns], rest[ns + 1:2 * ns + 1], (False,) * ns, *rest[2 * ns + 6:])

        @pl.when((j == 0) & (t == 0))
        def _():
            for cp in small_copies:
                cp.start()

        @pl.when(t == 0)
        def _():
            acc[...] = jnp.zeros(acc.shape, F32)

        acc[...] += lax.dot_general(x_ref[...], dy_ref[...], TN, preferred_element_type=F32)

        def to_owner(k, owner):
            return pltpu.make_async_remote_copy(
                src_ref=stage.at[owner], dst_ref=recv_ref.at[me], send_sem=send_sems.at[k], recv_sem=recv_sems.at[k],
                device_id=(owner // 4, (owner // 2) % 2, owner % 2), device_id_type=MESH)

        own = pltpu.make_async_copy(stage.at[me], recv_ref.at[me], own_sem)

        @pl.when(t == nt - 1)
        def _():
            owner = (me + 1 + j) % N_DEV
            stage[owner] = acc[...].astype(BF16)

            @pl.when(j < N_DEV - 1)
            def _():
                to_owner(j, owner).start()

            @pl.when(j == N_DEV - 1)
            def _():
                own.start()
                own.wait()
                for k in range(N_DEV - 1):
                    to_owner(k, me).wait_send()
                    to_owner(k, me).wait_recv()
                for cp in small_copies:
                    cp.wait()

    slab = lambda j, me_ref: (me_ref[0] + 1 + j) % N_DEV
    grid_spec = pltpu.PrefetchScalarGridSpec(
        num_scalar_prefetch=1, grid=(N_DEV, nt),
        in_specs=[pl.BlockSpec((TK, kd), lambda j, t, me_ref: (t, 0)),
                  pl.BlockSpec((None, TK, nd), lambda j, t, me_ref: (slab(j, me_ref), t, 0))] + [ANY_SPEC] * ns,
        out_specs=[ANY_SPEC] * (ns + 1),
        scratch_shapes=[pltpu.VMEM((kd, nd), F32), pltpu.VMEM((N_DEV, kd, nd), BF16),
                        pltpu.SemaphoreType.DMA((N_DEV - 1,)), pltpu.SemaphoreType.DMA((N_DEV - 1,)),
                        pltpu.SemaphoreType.DMA] + _exchange_sems(ns))
    res = pl.pallas_call(
        body, name="dw_in_exchange", grid_spec=grid_spec,
        out_shape=[jax.ShapeDtypeStruct((N_DEV, kd, nd), BF16)] + _exchange_shapes(small, (False,) * ns),
        compiler_params=_params("arbitrary", "arbitrary"),
    )(me_arr, hn, dz, *small)
    return res[0], res[1:]


def kernel(x, p, ln_mix, w_in, pool_w, pool_scale, w_out, ln_ffn, w_up, conv_w, conv_b, w_down, ln_ple, w_ple_gate, w_ple, ln_final, loss_target, m_ln_mix, m_w_in, m_pool_w, m_pool_scale, m_w_out, m_ln_ffn, m_w_up, m_conv_w, m_conv_b, m_w_down, m_ln_ple, m_w_ple_gate, m_w_ple, m_ln_final, v_ln_mix, v_w_in, v_pool_w, v_pool_scale, v_w_out, v_ln_ffn, v_w_up, v_conv_w, v_conv_b, v_w_down, v_ln_ple, v_w_ple_gate, v_w_ple, v_ln_final):
    xs, ps, tgt, pool_w0 = x[0], p[0, 0], loss_target[0], pool_w[0]
    slopes = jnp.exp2(-8.0 * (jnp.arange(N_HEADS, dtype=F32) + 1.0) / N_HEADS)
    conv_b_s = conv_b.reshape(N_DEV, 1, FF_SHARD)

    (w_in_g,) = _gather2("gather_w_in", [w_in[0].astype(BF16)])
    (q, k, v, u, hn1), (w_out_g,) = _qkvu(xs, ln_mix, w_in_g, [w_out[0].astype(BF16)])
    att, lse, (w_up_g, conv_w_g) = _attn_fwd(slopes, q, k, v, [w_up[0].astype(BF16), conv_w[0]])
    w_out_f = w_out_g.reshape(D_MODEL, D_MODEL)
    h1, mix, dlt, (w_down_g,) = _mix_out(xs, att, u, pool_w0, pool_scale, w_out_f, [w_down[0].astype(BF16)])
    w_down_f = w_down_g.reshape(4, FF_SHARD, D_MODEL)
    h2, hn2, up, upc, (w_pg_g, w_ple_g) = _ffn_fwd(h1, ln_ffn, w_up_g, conv_w_g, conv_b_s, w_down_f,
                                                   [w_ple_gate[0].astype(BF16), w_ple[0].astype(BF16)])
    w_pg_f = w_pg_g.reshape(D_MODEL, D_MODEL)
    w_ple_f = jnp.transpose(w_ple_g, (1, 0, 2)).reshape(PLE_DIM, D_MODEL)
    loss_blk, dh2, dh2b, hn3, dgl, dpe, d_ln_ple, d_ln_final = _head(
        h2, ps, ln_ple, w_pg_f, w_ple_f, ln_final.reshape(1, D_MODEL), tgt)

    d_w_pg = _wgrad("dw_ple_gate", hn3, dgl, "full", "full", 1, D_MODEL, D_MODEL).reshape(N_DEV, D_MODEL // N_DEV, D_MODEL)
    d_w_ple = _wgrad("dw_ple", ps, dpe, "full", "full", 1, PLE_DIM, D_MODEL)
    d_w_ple = jnp.transpose(d_w_ple.reshape(PLE_DIM, N_DEV, LANES), (1, 0, 2))
    a, dup, d_conv_w, d_conv_b, (r_w_pg, r_w_ple) = _ffn_bwd_a(dh2b, up, upc, w_down_f, [d_w_pg, d_w_ple])
    d_w_down = _wgrad("dw_down", a, dh2b, "lead", "full", 4, FF_SHARD, D_MODEL).reshape(N_DEV, D_FF // N_DEV, D_MODEL)
    dpre, dh1, dh1b, d_ln_ffn, (r_conv_w, r_w_down) = _ffn_bwd_b(
        dup, conv_w_g, w_up_g, h1, ln_ffn, dh2, [d_conv_w, d_w_down])
    datt, du, d_pool_w, d_pool_scale = _mix_bwd(dh1b, w_out_f, dlt, pool_w0, pool_scale)
    d_w_out = _wgrad("dw_out", mix, dh1b, "full", "full", 1, D_MODEL, D_MODEL).reshape(N_DEV, D_MODEL // N_DEV, D_MODEL)
    d_w_up = _wgrad("dw_up", dpre, hn2, "lead", "full", N_DEV, FF_SHARD, D_MODEL)
    rep_late = [d_pool_w, d_pool_scale, d_ln_ffn, d_conv_b.reshape(1, 2 * D_FF), d_ln_ple, d_ln_final, loss_blk]
    dq, dk, dv, received = _attn_bwd(slopes, q, k, v, att, lse, datt, [d_w_out, d_w_up] + rep_late,
                                     (True, True) + (False,) * len(rep_late))
    r_w_out, r_w_up, r_rep = received[0], received[1], list(received[2:])
    dz, grad_x, d_ln_mix = _in_bwd(dq, dk, dv, du, w_in_g, xs, ln_mix, dh1)

    rep_names = ("ln_mix", "pool_w", "pool_scale", "ln_ffn", "conv_b", "ln_ple", "ln_final")
    rep_w = [ln_mix, pool_w0, pool_scale, ln_ffn, conv_b, ln_ple, ln_final.reshape(1, D_MODEL)]
    rep_m = [m_ln_mix, m_pool_w[0], m_pool_scale, m_ln_ffn, m_conv_b, m_ln_ple, m_ln_final.reshape(1, D_MODEL)]
    rep_v = [v_ln_mix, v_pool_w[0], v_pool_scale, v_ln_ffn, v_conv_b, v_ln_ple, v_ln_final.reshape(1, D_MODEL)]
    r_w_in, (r_ln_mix,) = _dw_in_exchange(hn1, dz, [d_ln_mix])
    small = _adamw_small([r_ln_mix] + r_rep[:-1], r_rep[-1], rep_w, rep_m, rep_v)
    loss = small[4][0, 0]

    sharded = {}
    sharded["w_in"] = _adamw("adamw_w_in", r_w_in, w_in[0], m_w_in[0], v_w_in[0])
    sharded["w_out"] = _adamw("adamw_w_out", r_w_out, w_out[0], m_w_out[0], v_w_out[0])
    sharded["w_up"] = [t.T for t in _adamw("adamw_w_up", r_w_up, w_up[0].T, m_w_up[0].T, v_w_up[0].T)]
    sharded["conv_w"] = _adamw("adamw_conv_w", r_conv_w, conv_w[0], m_conv_w[0], v_conv_w[0])
    sharded["w_down"] = _adamw("adamw_w_down", r_w_down, w_down[0], m_w_down[0], v_w_down[0])
    sharded["w_ple_gate"] = _adamw("adamw_w_ple_gate", r_w_pg, w_ple_gate[0], m_w_ple_gate[0], v_w_ple_gate[0])
    sharded["w_ple"] = _adamw("adamw_w_ple", r_w_ple, w_ple[0], m_w_ple[0], v_w_ple[0])

    shapes = dict(w_in=w_in, w_out=w_out, w_up=w_up, conv_w=conv_w, w_down=w_down, w_ple_gate=w_ple_gate, w_ple=w_ple,
                  ln_mix=ln_mix, pool_w=pool_w, pool_scale=pool_scale, ln_ffn=ln_ffn, conv_b=conv_b, ln_ple=ln_ple,
                  ln_final=ln_final)

    def leaf(kind, n):
        src = sharded[n][kind] if n in sharded else small[kind][rep_names.index(n)]
        return src.reshape(shapes[n].shape)

    order = ("ln_mix", "w_in", "pool_w", "pool_scale", "w_out", "ln_ffn", "w_up", "conv_w", "conv_b", "w_down", "ln_ple",
             "w_ple_gate", "w_ple", "ln_final")
    outs = [loss, grad_x[None]]
    for kind in range(4):
        outs += [leaf(kind, n) for n in order]
    return tuple(outs)
```

```python
import jax
import jax.numpy as jnp
from jax import lax
from jax.experimental import pallas as pl
from jax.experimental.pallas import tpu as pltpu

F32 = jnp.float32
BF16 = jnp.bfloat16

N_DEV = 8
D_MODEL = 1024
ATT_WIDTH = 512
POOL_WIDTH = 512
N_HEADS = 8
HEAD_DIM = 64
SPAN = 128
DILATIONS = (1, 4, 16)
POOL_WINDOWS = (2, 4, 8, 16)
POOL_GROUP = 128
D_FF = 2816
FF_SHARD = 2 * D_FF // N_DEV
PLE_DIM = 256
EPS = 1e-6
NEG = -1e30

ADAM_LR = 0.001
ADAM_B1 = 0.9
ADAM_B2 = 0.999
ADAM_EPS = 1e-08
ADAM_WD = 0.01
ADAM_STEP = 10

LANES = 128
HALO = 16
TM = 512
TM_FF = 256
TK = 4096
ATTN_GROUP_FWD = 16
ATTN_GROUP_BWD = 8
VMEM_LIMIT = 56 * 1024 * 1024

MESH = pl.DeviceIdType.MESH
NT = (((1,), (1,)), ((), ()))
TN = (((0,), (0,)), ((), ()))


def _params(*sem):
    return pltpu.CompilerParams(dimension_semantics=sem or None, vmem_limit_bytes=VMEM_LIMIT)


def _const(shape):
    n = len(shape)
    return pl.BlockSpec(shape, lambda *_: (0,) * n, pipeline_mode=pl.Buffered(1))


def _rms(h):
    r = lax.rsqrt(jnp.mean(h * h, axis=-1, keepdims=True) + EPS)
    return r, h * r


def _rms_bwd(r, n, g, dhn):
    dn = dhn * g
    return r * (dn - n * jnp.mean(dn * n, axis=-1, keepdims=True))


def _colsum(a):
    return jnp.sum(a, axis=0, keepdims=True)


def _gather2_copies(ins, outs, send_sems, recv_sems, local_sems):
    n = len(ins)
    x, y, c = lax.axis_index("x"), lax.axis_index("y"), lax.axis_index("c")
    slot = lambda px, py, pc: 4 * px + 2 * py + pc
    chips = [(x, 1 - y), (1 - x, y), (1 - x, 1 - y)]
    first, passed, last = [], [], []

    def remote(a, r, src, dst_slot, to):
        return pltpu.make_async_remote_copy(
            src_ref=src, dst_ref=outs[a].at[dst_slot],
            send_sem=send_sems.at[a * (N_DEV - 1) + r], recv_sem=recv_sems.at[a * (N_DEV - 1) + r],
            device_id=to, device_id_type=MESH)

    for a in range(n):
        mine = pltpu.make_async_copy(ins[a], outs[a].at[slot(x, y, c)], local_sems.at[a])
        to_sibling = remote(a, 0, ins[a], slot(x, y, c), (x, y, 1 - c))
        first += [mine, to_sibling]
        last += [mine.wait, to_sibling.wait_send, to_sibling.wait_recv]
        for r, (px, py) in enumerate(chips, start=1):
            to_chip = remote(a, r, ins[a], slot(x, y, c), (px, py, c))
            onward = remote(a, 3 + r, outs[a].at[slot(px, py, c)], slot(px, py, c), (x, y, 1 - c))
            first.append(to_chip)
            passed.append((to_chip, onward))
            last += [to_chip.wait_send, onward.wait_send, onward.wait_recv]
    return first, passed, last


def _gather2_begin(plan, step, pass_step):
    first, passed, _ = plan

    @pl.when(step == 0)
    def _():
        for cp in first:
            cp.start()

    @pl.when(step == pass_step)
    def _():
        for arrival, cp in passed:
            arrival.wait_recv()
            cp.start()


def _gather2_end(plan, step, nsteps):
    @pl.when(step == nsteps - 1)
    def _():
        for wait in plan[2]:
            wait()


ANY_SPEC = pl.BlockSpec(memory_space=pl.ANY)


def _exchange_shapes(arrays, scatter):
    out = []
    for a, s in zip(arrays, scatter):
        slab = a.shape[1:] if s else a.shape
        out.append(jax.ShapeDtypeStruct((N_DEV,) + tuple(slab), a.dtype))
    return out


def _exchange_sems(n):
    return [pltpu.SemaphoreType.DMA((n * (N_DEV - 1),)), pltpu.SemaphoreType.DMA((n * (N_DEV - 1),)),
            pltpu.SemaphoreType.DMA((n,))]


def _exchange_copies(ins, outs, scatter, send_sems, recv_sems, local_sems):
    n = len(ins)
    x, y, c = lax.axis_index("x"), lax.axis_index("y"), lax.axis_index("c")
    me = 4 * x + 2 * y + c
    copies = []
    for a in range(n):
        src = ins[a].at[me] if scatter[a] else ins[a]
        copies.append(pltpu.make_async_copy(src, outs[a].at[me], local_sems.at[a]))
    for k in range(1, N_DEV):
        px = 1 - x if k & 4 else x
        py = 1 - y if k & 2 else y
        pc = 1 - c if k & 1 else c
        pid = 4 * px + 2 * py + pc
        for a in range(n):
            src = ins[a].at[pid] if scatter[a] else ins[a]
            copies.append(pltpu.make_async_remote_copy(
                src_ref=src, dst_ref=outs[a].at[me],
                send_sem=send_sems.at[a * (N_DEV - 1) + k - 1], recv_sem=recv_sems.at[a * (N_DEV - 1) + k - 1],
                device_id=(px, py, pc), device_id_type=MESH))
    return copies


def _qkvu(x, g1, w_in, shards):
    S = x.shape[0]
    ns = len(shards)
    nsteps = S // TM

    def body(x_ref, g_ref, w_ref, *rest):
        q_ref, k_ref, v_ref, u_ref, hn_ref = rest[ns:ns + 5]
        plan = _gather2_copies(rest[:ns], rest[ns + 5:2 * ns + 5], *rest[2 * ns + 5:])
        _gather2_begin(plan, pl.program_id(0), nsteps - 2)
        r, n = _rms(x_ref[...])
        hn = (n * g_ref[...]).astype(BF16)
        hn_ref[...] = hn
        outs = (q_ref, k_ref, v_ref, u_ref)
        for j in range(N_DEV):
            z = jnp.dot(hn, w_ref[j], preferred_element_type=F32)
            if j < 2:
                z = z * (HEAD_DIM ** -0.5)
            outs[j // 2][:, (j % 2) * 256:(j % 2 + 1) * 256] = z
        _gather2_end(plan, pl.program_id(0), nsteps)

    tok = lambda w: pl.BlockSpec((TM, w), lambda i: (i, 0))
    res = pl.pallas_call(
        body, name="qkvu", grid=(nsteps,),
        in_specs=[tok(D_MODEL), _const((1, D_MODEL)), _const(w_in.shape)] + [ANY_SPEC] * ns,
        out_specs=[tok(512)] * 4 + [tok(D_MODEL)] + [ANY_SPEC] * ns,
        out_shape=[jax.ShapeDtypeStruct((S, 512), F32)] * 4 + [jax.ShapeDtypeStruct((S, D_MODEL), BF16)]
        + _exchange_shapes(shards, (False,) * ns),
        scratch_shapes=_exchange_sems(ns),
        compiler_params=_params("arbitrary"),
    )(x, g1, w_in, *shards)
    return res[:5], res[5:]


def _attn_fill_bias(bias_s, slope_ref, hp, d):
    qi = lax.broadcasted_iota(jnp.int32, (SPAN, 2 * SPAN), 0)
    kj = lax.broadcasted_iota(jnp.int32, (SPAN, 2 * SPAN), 1)
    for t, diff in enumerate((qi + SPAN - kj, qi - kj)):
        valid = (diff >= 0) & (diff <= SPAN)
        dist = diff.astype(F32) * float(d)
        for h in range(2):
            bias_s[t, h * SPAN:(h + 1) * SPAN, :] = jnp.where(valid, -slope_ref[2 * hp + h] * dist, NEG)


def _stack_heads(x, is0):
    return jnp.concatenate([jnp.where(is0, x, 0.0), jnp.where(is0, 0.0, x)], axis=0)


def _unstack_heads(y, is0):
    return jnp.where(is0, y[0:SPAN], y[SPAN:2 * SPAN])


def _attn_block(i, g, d, nb, group):
    gr = min(d, group)
    gn = group // gr
    per = d // gr
    r = (i & (per - 1)) * gr + g % gr
    n = (i >> (per.bit_length() - 1)) + (g // gr) * (nb // gn)
    k0 = jnp.maximum(n - 1, 0)

    def ds(block, nrows):
        start = block * (SPAN * d) + r
        return pl.ds(start, nrows, stride=d) if d > 1 else pl.ds(start, nrows)

    return ds(n, SPAN), ds(k0, 2 * SPAN), jnp.where(n == 0, 1, 0)


def _attn_groups(S, d, group):
    nb = S // d // SPAN
    gn = group // min(d, group)
    assert nb >= 2 and nb % gn == 0 and (gn == 1 or nb // gn >= 2)
    return nb, d * nb // group


def _attn_fwd(slopes, q, k, v, shards):
    S = q.shape[0]
    ns = len(shards)
    steps = ATT_WIDTH // LANES

    def body(slope_ref, q_ref, k_ref, v_ref, *rest):
        o_ref, lse_ref = rest[ns:ns + 2]
        m_s, l_s, bias_s = rest[2 * ns + 2:2 * ns + 5]
        hp = pl.program_id(0)
        plan = _gather2_copies(rest[:ns], rest[ns + 2:2 * ns + 2], *rest[2 * ns + 5:])
        _gather2_begin(plan, hp, steps - 1)

        is0 = lax.broadcasted_iota(jnp.int32, (SPAN, LANES), 1) < HEAD_DIM
        for pi, d in enumerate(DILATIONS):
            nb, ngroups = _attn_groups(S, d, ATTN_GROUP_FWD)
            _attn_fill_bias(bias_s, slope_ref, hp, d)

            def group(i, carry, d=d, pi=pi, nb=nb):
                blocks = [_attn_block(i, g, d, nb, ATTN_GROUP_FWD) for g in range(ATTN_GROUP_FWD)]
                loaded = [(q_ref[rows, :], k_ref[krows, :].astype(BF16), v_ref[krows, :].astype(BF16))
                          for rows, krows, _ in blocks]
                new = []
                for (rows, krows, tab), (qb, kb, vb) in zip(blocks, loaded):
                    qs = _stack_heads(qb, is0).astype(BF16)
                    s = lax.dot_general(qs, kb, NT, preferred_element_type=F32) + bias_s[tab]
                    m = jnp.max(s, axis=-1, keepdims=True)
                    e = jnp.exp(s - m)
                    l = jnp.sum(e, axis=-1, keepdims=True)
                    pv = jnp.dot(e.astype(BF16), vb, preferred_element_type=F32)
                    new.append([_unstack_heads(jnp.broadcast_to(m, pv.shape), is0),
                                _unstack_heads(jnp.broadcast_to(l, pv.shape), is0), _unstack_heads(pv, is0)])
                if pi > 0:
                    old = [(m_s[rows, :], l_s[rows, :], o_ref[rows, :]) for rows, _, _ in blocks]
                    for st, (m_o, l_o, o_o) in zip(new, old):
                        m_n = jnp.maximum(m_o, st[0])
                        a_o = jnp.exp(m_o - m_n)
                        a_b = jnp.exp(st[0] - m_n)
                        st[:] = [m_n, a_o * l_o + a_b * st[1], a_o * o_o + a_b * st[2]]
                for (rows, _, _), (m_b, l_b, acc) in zip(blocks, new):
                    if pi == len(DILATIONS) - 1:
                        o_ref[rows, :] = acc / l_b
                        lse_ref[rows, :] = m_b + jnp.log(l_b)
                    else:
                        o_ref[rows, :] = acc
                        m_s[rows, :] = m_b
                        l_s[rows, :] = l_b
                return carry

            lax.fori_loop(0, ngroups, group, 0)

        _gather2_end(plan, hp, steps)

    col = pl.BlockSpec((S, LANES), lambda i: (0, i))
    res = pl.pallas_call(
        body, name="attn_fwd", grid=(steps,),
        in_specs=[pl.BlockSpec(memory_space=pltpu.SMEM), col, col, col] + [ANY_SPEC] * ns,
        out_specs=[col, col] + [ANY_SPEC] * ns,
        out_shape=[jax.ShapeDtypeStruct((S, ATT_WIDTH), F32)] * 2 + _exchange_shapes(shards, (False,) * ns),
        scratch_shapes=[pltpu.VMEM((S, LANES), F32), pltpu.VMEM((S, LANES), F32),
                        pltpu.VMEM((2, 2 * SPAN, 2 * SPAN), F32)] + _exchange_sems(ns),
        compiler_params=_params("arbitrary"),
    )(slopes, q, k, v, *shards)
    return res[0], res[1], res[2:]


def _pool_count(i, w):
    t = i * TM + lax.broadcasted_iota(jnp.int32, (TM, 1), 0)
    return jnp.minimum(t + 1, w).astype(F32)


def _mix_out(x, att, u, pool_w, pool_scale, w_out, shards):
    S = x.shape[0]
    ns = len(shards)
    nsteps = S // TM

    def body(x_ref, att_ref, u_ref, pw_ref, ps_ref, w_ref, *rest):
        h1_ref, mix_ref, dlt_ref = rest[ns:ns + 3]
        ubuf = rest[2 * ns + 3]
        i = pl.program_id(0)
        plan = _gather2_copies(rest[:ns], rest[ns + 3:2 * ns + 3], *rest[2 * ns + 4:])
        _gather2_begin(plan, i, nsteps - 1)

        @pl.when(i == 0)
        def _():
            ubuf[0:HALO, :] = jnp.zeros((HALO, POOL_WIDTH), F32)

        ubuf[HALO:HALO + TM, :] = u_ref[...]
        mix_ref[:, 0:ATT_WIDTH] = att_ref[...].astype(BF16)
        for g, w in enumerate(POOL_WINDOWS):
            cols = slice(g * POOL_GROUP, (g + 1) * POOL_GROUP)
            ug = ubuf[HALO:HALO + TM, cols]
            acc = ug
            for j in range(1, w):
                acc = acc + ubuf[HALO - j:HALO - j + TM, cols]
            dlt = (acc / _pool_count(i, w) - ug).astype(BF16)
            dlt_ref[:, cols] = dlt
            yg = jnp.dot(dlt, pw_ref[g].astype(BF16), preferred_element_type=F32) * ps_ref[:, cols]
            mix_ref[:, ATT_WIDTH + g * POOL_GROUP:ATT_WIDTH + (g + 1) * POOL_GROUP] = yg.astype(BF16)
        ubuf[0:HALO, :] = ubuf[TM:TM + HALO, :]
        h1_ref[...] = x_ref[...] + jnp.dot(mix_ref[...], w_ref[...], preferred_element_type=F32)
        _gather2_end(plan, i, nsteps)

    tok = lambda w: pl.BlockSpec((TM, w), lambda i: (i, 0))
    res = pl.pallas_call(
        body, name="mix_out", grid=(nsteps,),
        in_specs=[tok(D_MODEL), tok(ATT_WIDTH), tok(POOL_WIDTH), _const(pool_w.shape), _const((1, POOL_WIDTH)),
                  _const(w_out.shape)] + [ANY_SPEC] * ns,
        out_specs=[tok(D_MODEL), tok(D_MODEL), tok(POOL_WIDTH)] + [ANY_SPEC] * ns,
        out_shape=[jax.ShapeDtypeStruct((S, D_MODEL), F32), jax.ShapeDtypeStruct((S, D_MODEL), BF16),
                   jax.ShapeDtypeStruct((S, POOL_WIDTH), BF16)] + _exchange_shapes(shards, (False,) * ns),
        scratch_shapes=[pltpu.VMEM((TM + HALO, POOL_WIDTH), F32)] + _exchange_sems(ns),
        compiler_params=_params("arbitrary"),
    )(x, att, u, pool_w, pool_scale, w_out, *shards)
    return res[0], res[1], res[2], res[3:]


def _conv_fwd(stage, upre, prev, cw, cb):
    T = upre.shape[0]
    stage[0:HALO, :] = prev
    stage[HALO:HALO + T, :] = upre
    return cb + cw[0:1, :] * stage[HALO - 2:HALO - 2 + T, :] + cw[1:2, :] * stage[HALO - 1:HALO - 1 + T, :] + cw[2:3, :] * upre


def _ffn_fwd(h1, g2, w_up, conv_w, conv_b, w_down, shards):
    S = h1.shape[0]
    T = TM_FF
    ns = len(shards)
    nsteps = S // T

    def body(h1_ref, g_ref, wu_ref, cw_ref, cb_ref, wd_ref, *rest):
        h2_ref, hn_ref, up_ref, upc_ref = rest[ns:ns + 4]
        carry, stage = rest[2 * ns + 4:2 * ns + 6]
        i = pl.program_id(0)
        plan = _gather2_copies(rest[:ns], rest[ns + 4:2 * ns + 4], *rest[2 * ns + 6:])
        _gather2_begin(plan, i, nsteps // 2)

        @pl.when(i == 0)
        def _():
            carry[...] = jnp.zeros(carry.shape, F32)

        h1t = h1_ref[...]
        r, n = _rms(h1t)
        hn = (n * g_ref[...]).astype(BF16)
        hn_ref[...] = hn
        acc = h1t
        for j in range(4):
            conv = []
            for jj in (j, j + 4):
                upre = jnp.dot(hn, wu_ref[jj], preferred_element_type=F32)
                up_ref[jj] = upre.astype(BF16)
                conv.append(_conv_fwd(stage, upre, carry[jj], cw_ref[jj], cb_ref[jj]))
                upc_ref[jj] = conv[-1].astype(BF16)
                carry[jj] = stage[T:T + HALO, :]
            gate, val = conv
            a = gate * jax.nn.sigmoid(gate) * val
            acc = acc + jnp.dot(a.astype(BF16), wd_ref[j], preferred_element_type=F32)
        h2_ref[...] = acc
        _gather2_end(plan, i, nsteps)

    tok = lambda w: pl.BlockSpec((T, w), lambda i: (i, 0))
    res = pl.pallas_call(
        body, name="ffn_fwd", grid=(nsteps,),
        in_specs=[tok(D_MODEL), _const((1, D_MODEL)), _const(w_up.shape), _const(conv_w.shape), _const(conv_b.shape),
                  _const(w_down.shape)] + [ANY_SPEC] * ns,
        out_specs=[tok(D_MODEL), tok(D_MODEL)] + [pl.BlockSpec((N_DEV, T, FF_SHARD), lambda i: (0, i, 0))] * 2
        + [ANY_SPEC] * ns,
        out_shape=[jax.ShapeDtypeStruct((S, D_MODEL), F32), jax.ShapeDtypeStruct((S, D_MODEL), BF16)]
        + [jax.ShapeDtypeStruct((N_DEV, S, FF_SHARD), BF16)] * 2 + _exchange_shapes(shards, (False,) * ns),
        scratch_shapes=[pltpu.VMEM((N_DEV, HALO, FF_SHARD), F32), pltpu.VMEM((T + HALO, FF_SHARD), F32)]
        + _exchange_sems(ns),
        compiler_params=_params("arbitrary"),
    )(h1, g2, w_up, conv_w, conv_b, w_down, *shards)
    return res[0], res[1], res[2], res[3], res[4:]


def _head(h2, p, g3, w_pg, w_ple, g4, target):
    S = h2.shape[0]
    nt = S // TM

    def body(h2_ref, p_ref, g3_ref, wpg_ref, wple_ref, g4_ref, t_ref,
             loss_ref, dh2_ref, dh2b_ref, hn3_ref, dgl_ref, dpe_ref, dg3_ref, dg4_ref, lacc):
        i = pl.program_id(0)

        @pl.when(i == 0)
        def _():
            lacc[...] = jnp.zeros(lacc.shape, F32)
            dg3_ref[...] = jnp.zeros(dg3_ref.shape, F32)
            dg4_ref[...] = jnp.zeros(dg4_ref.shape, F32)

        h2t = h2_ref[...]
        g3, g4 = g3_ref[...], g4_ref[...]
        r3, n3 = _rms(h2t)
        hn3 = (n3 * g3).astype(BF16)
        hn3_ref[...] = hn3
        gs = jax.nn.sigmoid(jnp.dot(hn3, wpg_ref[...], preferred_element_type=F32))
        pe = jnp.dot(p_ref[...].astype(BF16), wple_ref[...], preferred_element_type=F32)
        h3 = h2t + gs * pe
        r4, n4 = _rms(h3)
        err = n4 * g4 - t_ref[...]
        lacc[...] += _colsum(err * err)
        dy = err * (1.0 / D_MODEL)
        dg4_ref[...] += _colsum(dy * n4)
        dh3 = _rms_bwd(r4, n4, g4, dy)
        dpe_ref[...] = (dh3 * gs).astype(BF16)
        dgl = (dh3 * pe * gs * (1.0 - gs)).astype(BF16)
        dgl_ref[...] = dgl
        dhn3 = lax.dot_general(dgl, wpg_ref[...], NT, preferred_element_type=F32)
        dg3_ref[...] += _colsum(dhn3 * n3)
        dh2 = dh3 + _rms_bwd(r3, n3, g3, dhn3)
        dh2_ref[...] = dh2
        dh2b_ref[...] = dh2.astype(BF16)

        @pl.when(i == nt - 1)
        def _():
            tot = 0.5 / D_MODEL * jnp.sum(lacc[...], axis=-1, keepdims=True)
            loss_ref[...] = jnp.broadcast_to(tot, loss_ref.shape)

    tok = lambda w: pl.BlockSpec((TM, w), lambda i: (i, 0))
    row = pl.BlockSpec((1, D_MODEL), lambda i: (0, 0))
    act = lambda dt: jax.ShapeDtypeStruct((S, D_MODEL), dt)
    return pl.pallas_call(
        body, name="head", grid=(nt,),
        in_specs=[tok(D_MODEL), tok(PLE_DIM), _const((1, D_MODEL)), _const(w_pg.shape), _const(w_ple.shape),
                  _const((1, D_MODEL)), tok(D_MODEL)],
        out_specs=[pl.BlockSpec((8, LANES), lambda i: (0, 0)), tok(D_MODEL), tok(D_MODEL), tok(D_MODEL), tok(D_MODEL),
                   tok(D_MODEL), row, row],
        out_shape=[jax.ShapeDtypeStruct((8, LANES), F32), act(F32), act(BF16), act(BF16), act(BF16), act(BF16),
                   jax.ShapeDtypeStruct((1, D_MODEL), F32), jax.ShapeDtypeStruct((1, D_MODEL), F32)],
        scratch_shapes=[pltpu.VMEM((1, D_MODEL), F32)],
        compiler_params=_params("arbitrary"),
    )(h2, p, g3, w_pg, w_ple, g4, target)


def _wgrad(name, x, dy, x_kind, dy_kind, nj, k_dim, n_dim, tk=TK):
    S = x.shape[-2]
    nt = S // tk

    def spec(kind, width):
        if kind == "full":
            return pl.BlockSpec((tk, width), lambda j, t: (t, 0))
        return pl.BlockSpec((None, tk, width), lambda j, t: (j, t, 0))

    def body(x_ref, dy_ref, o_ref, acc):
        t = pl.program_id(1)

        @pl.when(t == 0)
        def _():
            acc[...] = jnp.zeros(acc.shape, F32)

        acc[...] += lax.dot_general(x_ref[...].astype(BF16), dy_ref[...], TN, preferred_element_type=F32)

        @pl.when(t == nt - 1)
        def _():
            o_ref[...] = acc[...].astype(BF16)

    return pl.pallas_call(
        body, name=name, grid=(nj, nt),
        in_specs=[spec(x_kind, k_dim), spec(dy_kind, n_dim)],
        out_specs=pl.BlockSpec((None, k_dim, n_dim), lambda j, t: (j, 0, 0)),
        out_shape=jax.ShapeDtypeStruct((nj, k_dim, n_dim), BF16),
        scratch_shapes=[pltpu.VMEM((k_dim, n_dim), F32)],
        compiler_params=_params("arbitrary", "arbitrary"),
    )(x, dy)


def _row_picker(T, off0, off1):
    r = lax.broadcasted_iota(jnp.int32, (2 * T, T + HALO), 0)
    c = lax.broadcasted_iota(jnp.int32, (2 * T, T + HALO), 1)
    want = jnp.where(r < T, r + off0, r - T + off1)
    return jnp.where(c == want, 1.0, 0.0).astype(BF16)


def _ffn_bwd_a(dh2b, up, upc, w_down, grads):
    S = dh2b.shape[0]
    T = TM_FF
    hb = T // HALO
    nsteps = S // T
    ng = len(grads)

    def body(dh_ref, up_ref, halo_ref, upc_ref, wd_ref, *rest):
        a_ref, dup_ref, dcw_ref, dcb_ref = rest[ng:ng + 4]
        stage = rest[2 * ng + 4]
        i = pl.program_id(0)
        copies = _exchange_copies(rest[:ng], rest[ng + 4:2 * ng + 4], (True,) * ng, *rest[2 * ng + 5:])

        @pl.when(i == 0)
        def _():
            dcw_ref[...] = jnp.zeros(dcw_ref.shape, F32)
            dcb_ref[...] = jnp.zeros(dcb_ref.shape, F32)
            for cp in copies:
                cp.start()

        @pl.when(i == nsteps - 1)
        def _():
            for cp in copies:
                cp.wait()

        dh = dh_ref[...]
        pick = _row_picker(T, HALO - 2, HALO - 1)
        for j in range(4):
            da = lax.dot_general(dh, wd_ref[j], NT, preferred_element_type=F32)
            taps = []
            for jj in (j, j + 4):
                upre = up_ref[jj]
                stage[0:HALO, :] = jnp.where(i > 0, halo_ref[jj], jnp.zeros((HALO, FF_SHARD), BF16))
                stage[HALO:HALO + T, :] = upre
                prv = jnp.dot(pick, stage[...], preferred_element_type=F32)
                taps.append((prv[0:T], prv[T:2 * T], upre.astype(F32)))
            gate, val = upc_ref[j].astype(F32), upc_ref[j + 4].astype(F32)
            sg = jax.nn.sigmoid(gate)
            silu = gate * sg
            a_ref[j] = (silu * val).astype(BF16)
            dgate = (da * val) * (sg + silu * (1.0 - sg))
            dval = da * silu
            for jj, dup, tp in ((j, dgate, taps[0]), (j + 4, dval, taps[1])):
                dup_ref[jj] = dup.astype(BF16)
                dcb_ref[jj] += _colsum(dup)
                for kk in range(3):
                    dcw_ref[jj, kk:kk + 1, :] += _colsum(dup * tp[kk])

    tok = lambda w: pl.BlockSpec((T, w), lambda i: (i, 0))
    shard = pl.BlockSpec((N_DEV, T, FF_SHARD), lambda i: (0, i, 0))
    res = pl.pallas_call(
        body, name="ffn_bwd_a", grid=(nsteps,),
        in_specs=[tok(D_MODEL), shard,
                  pl.BlockSpec((N_DEV, HALO, FF_SHARD), lambda i: (0, jnp.maximum(i * hb - 1, 0), 0)),
                  shard, _const(w_down.shape)] + [ANY_SPEC] * ng,
        out_specs=[pl.BlockSpec((4, T, FF_SHARD), lambda i: (0, i, 0)), shard,
                   pl.BlockSpec((N_DEV, 3, FF_SHARD), lambda i: (0, 0, 0)),
                   pl.BlockSpec((N_DEV, 1, FF_SHARD), lambda i: (0, 0, 0))] + [ANY_SPEC] * ng,
        out_shape=[jax.ShapeDtypeStruct((4, S, FF_SHARD), BF16), jax.ShapeDtypeStruct((N_DEV, S, FF_SHARD), BF16),
                   jax.ShapeDtypeStruct((N_DEV, 3, FF_SHARD), F32), jax.ShapeDtypeStruct((N_DEV, 1, FF_SHARD), F32)]
        + _exchange_shapes(grads, (True,) * ng),
        scratch_shapes=[pltpu.VMEM((T + HALO, FF_SHARD), BF16)] + _exchange_sems(ng),
        compiler_params=_params("arbitrary"),
    )(dh2b, up, up, upc, w_down, *grads)
    return res[0], res[1], res[2], res[3], res[4:]


def _ffn_bwd_b(dup, conv_w, w_up, h1, g2, dh2, grads):
    S = h1.shape[0]
    T = TM_FF
    hb = T // HALO
    nt = S // T
    ng = len(grads)

    def body(dup_ref, halo_ref, cw_ref, wu_ref, h1_ref, g_ref, dh2_ref, *rest):
        dpre_ref, dh1_ref, dh1b_ref, dg_ref = rest[ng:ng + 4]
        stage = rest[2 * ng + 4]
        i = pl.program_id(0)
        copies = _exchange_copies(rest[:ng], rest[ng + 4:2 * ng + 4], (True,) * ng, *rest[2 * ng + 5:])

        @pl.when(i == 0)
        def _():
            dg_ref[...] = jnp.zeros(dg_ref.shape, F32)
            for cp in copies:
                cp.start()

        dhn = jnp.zeros((T, D_MODEL), F32)
        for jj in range(N_DEV):
            dup = dup_ref[jj].astype(F32)
            stage[0:T, :] = dup
            stage[T:T + HALO, :] = jnp.where(i < nt - 1, halo_ref[jj].astype(F32), 0.0)
            cw = cw_ref[jj]
            dpre = (cw[2:3, :] * dup + cw[1:2, :] * stage[1:1 + T, :] + cw[0:1, :] * stage[2:2 + T, :]).astype(BF16)
            dpre_ref[jj] = dpre
            dhn = dhn + lax.dot_general(dpre, wu_ref[jj], NT, preferred_element_type=F32)
        g = g_ref[...]
        r, n = _rms(h1_ref[...])
        dg_ref[...] += _colsum(dhn * n)
        dh1 = dh2_ref[...] + _rms_bwd(r, n, g, dhn)
        dh1_ref[...] = dh1
        dh1b_ref[...] = dh1.astype(BF16)

        @pl.when(i == nt - 1)
        def _():
            for cp in copies:
                cp.wait()

    tok = lambda w: pl.BlockSpec((T, w), lambda i: (i, 0))
    shard = pl.BlockSpec((N_DEV, T, FF_SHARD), lambda i: (0, i, 0))
    res = pl.pallas_call(
        body, name="ffn_bwd_b", grid=(nt,),
        in_specs=[shard,
                  pl.BlockSpec((N_DEV, HALO, FF_SHARD), lambda i: (0, jnp.minimum((i + 1) * hb, S // HALO - 1), 0)),
                  _const(conv_w.shape), _const(w_up.shape), tok(D_MODEL), _const((1, D_MODEL)), tok(D_MODEL)]
        + [ANY_SPEC] * ng,
        out_specs=[shard, tok(D_MODEL), tok(D_MODEL), pl.BlockSpec((1, D_MODEL), lambda i: (0, 0))] + [ANY_SPEC] * ng,
        out_shape=[jax.ShapeDtypeStruct((N_DEV, S, FF_SHARD), BF16), jax.ShapeDtypeStruct((S, D_MODEL), F32),
                   jax.ShapeDtypeStruct((S, D_MODEL), BF16), jax.ShapeDtypeStruct((1, D_MODEL), F32)]
        + _exchange_shapes(grads, (True,) * ng),
        scratch_shapes=[pltpu.VMEM((T + HALO, FF_SHARD), F32)] + _exchange_sems(ng),
        compiler_params=_params("arbitrary"),
    )(dup, dup, conv_w, w_up, h1, g2, dh2, *grads)
    return res[0], res[1], res[2], res[3], res[4:]


def _mix_bwd(dh1b, w_out, dlt, pool_w, pool_scale):
    S = dh1b.shape[0]
    nt = S // TM

    def body(dh_ref, w_ref, dlt_ref, pw_ref, ps_ref, datt_ref, du_ref, dpw_ref, dps_ref, stage, carry):
        i = pl.program_id(0)
        tile = nt - 1 - i

        @pl.when(i == 0)
        def _():
            dpw_ref[...] = jnp.zeros(dpw_ref.shape, F32)
            dps_ref[...] = jnp.zeros(dps_ref.shape, F32)
            carry[...] = jnp.zeros(carry.shape, F32)

        dmix = lax.dot_general(dh_ref[...], w_ref[...], NT, preferred_element_type=F32)
        datt_ref[...] = dmix[:, 0:ATT_WIDTH]
        for g, w in enumerate(POOL_WINDOWS):
            cols = slice(g * POOL_GROUP, (g + 1) * POOL_GROUP)
            dpool = dmix[:, ATT_WIDTH + g * POOL_GROUP:ATT_WIDTH + (g + 1) * POOL_GROUP]
            dl = dlt_ref[:, cols]
            pw = pw_ref[g].astype(BF16)
            yg = jnp.dot(dl, pw, preferred_element_type=F32)
            dps_ref[:, cols] += _colsum(dpool * yg)
            dy = (dpool * ps_ref[:, cols]).astype(BF16)
            dpw_ref[g] += lax.dot_general(dl, dy, TN, preferred_element_type=F32)
            ddlt = lax.dot_general(dy, pw, NT, preferred_element_type=F32)
            cg = ddlt / _pool_count(tile, w)
            stage[0:TM, :] = cg
            stage[TM:TM + HALO, :] = carry[:, cols]
            acc = cg
            for j in range(1, w):
                acc = acc + stage[j:j + TM, :]
            du_ref[:, cols] = acc - ddlt
            carry[:, cols] = cg[0:HALO, :]

    tok = lambda w: pl.BlockSpec((TM, w), lambda i: (nt - 1 - i, 0))
    return pl.pallas_call(
        body, name="mix_bwd", grid=(nt,),
        in_specs=[tok(D_MODEL), _const(w_out.shape), tok(POOL_WIDTH), _const(pool_w.shape), _const((1, POOL_WIDTH))],
        out_specs=[tok(ATT_WIDTH), tok(POOL_WIDTH), pl.BlockSpec(pool_w.shape, lambda i: (0, 0, 0)),
                   pl.BlockSpec((1, POOL_WIDTH), lambda i: (0, 0))],
        out_shape=[jax.ShapeDtypeStruct((S, ATT_WIDTH), F32), jax.ShapeDtypeStruct((S, POOL_WIDTH), F32),
                   jax.ShapeDtypeStruct(pool_w.shape, F32), jax.ShapeDtypeStruct((1, POOL_WIDTH), F32)],
        scratch_shapes=[pltpu.VMEM((TM + HALO, POOL_GROUP), F32), pltpu.VMEM((HALO, POOL_WIDTH), F32)],
        compiler_params=_params("arbitrary"),
    )(dh1b, w_out, dlt, pool_w, pool_scale)


def _attn_bwd(slopes, q, k, v, o, lse, do, grads, scatter):
    S = q.shape[0]
    CH = 512
    ng = len(grads)
    steps = ATT_WIDTH // LANES

    def body(slope_ref, q_ref, k_ref, v_ref, o_ref, lse_ref, do_ref, *rest):
        dq_ref, dk_ref, dv_ref = rest[ng:ng + 3]
        dl_s, bias_s = rest[2 * ng + 3:2 * ng + 5]
        hp = pl.program_id(0)
        copies = _exchange_copies(rest[:ng], rest[ng + 3:2 * ng + 3], scatter, *rest[2 * ng + 5:])

        @pl.when(hp == 0)
        def _():
            for cp in copies:
                cp.start()

        is0 = lax.broadcasted_iota(jnp.int32, (SPAN, LANES), 1) < HEAD_DIM
        is0c = lax.broadcasted_iota(jnp.int32, (CH, LANES), 1) < HEAD_DIM

        def prep(ci, carry):
            rows = pl.ds(pl.multiple_of(ci * CH, CH), CH)
            prod = do_ref[rows, :] * o_ref[rows, :]
            d0 = jnp.sum(jnp.where(is0c, prod, 0.0), axis=-1, keepdims=True)
            d1 = jnp.sum(jnp.where(is0c, 0.0, prod), axis=-1, keepdims=True)
            dl_s[rows, :] = jnp.where(is0c, d0, d1)
            zero = jnp.zeros((CH, LANES), F32)
            dq_ref[rows, :] = zero
            dk_ref[rows, :] = zero
            dv_ref[rows, :] = zero
            return carry

        lax.fori_loop(0, S // CH, prep, 0)

        for d in DILATIONS:
            nb, ngroups = _attn_groups(S, d, ATTN_GROUP_BWD)
            _attn_fill_bias(bias_s, slope_ref, hp, d)

            def group(i, carry, d=d, nb=nb):
                blocks = [_attn_block(i, g, d, nb, ATTN_GROUP_BWD) for g in range(ATTN_GROUP_BWD)]
                loaded = [(q_ref[rows, :], do_ref[rows, :], lse_ref[rows, :], dl_s[rows, :], k_ref[krows, :],
                           v_ref[krows, :].astype(BF16)) for rows, krows, _ in blocks]
                new = []
                for (rows, krows, tab), (qb, dob, lse_b, dl_b, kf, vb) in zip(blocks, loaded):
                    kb = kf.astype(BF16)
                    qs = _stack_heads(qb, is0).astype(BF16)
                    dos = _stack_heads(dob, is0).astype(BF16)
                    lse_s = jnp.concatenate([lse_b[:, 0:1], lse_b[:, HEAD_DIM:HEAD_DIM + 1]], axis=0)
                    dl_s2 = jnp.concatenate([dl_b[:, 0:1], dl_b[:, HEAD_DIM:HEAD_DIM + 1]], axis=0)
                    s = lax.dot_general(qs, kb, NT, preferred_element_type=F32) + bias_s[tab]
                    pr = jnp.exp(s - lse_s)
                    dp = lax.dot_general(dos, vb, NT, preferred_element_type=F32)
                    ds = (pr * (dp - dl_s2)).astype(BF16)
                    dv_c = lax.dot_general(pr.astype(BF16), dos, TN, preferred_element_type=F32)
                    dk_c = lax.dot_general(ds, qs, TN, preferred_element_type=F32)
                    dq_c = _unstack_heads(jnp.dot(ds, kb, preferred_element_type=F32), is0)
                    new.append((dq_c, dk_c, dv_c))
                old = [(dq_ref[rows, :], dk_ref[krows, :], dv_ref[krows, :]) for rows, krows, _ in blocks]
                for (rows, krows, _), (dq_c, dk_c, dv_c), (dq_o, dk_o, dv_o) in zip(blocks, new, old):
                    dq_ref[rows, :] = dq_o + dq_c
                    dk_ref[krows, :] = dk_o + dk_c
                    dv_ref[krows, :] = dv_o + dv_c
                return carry

            lax.fori_loop(0, ngroups, group, 0)

        @pl.when(hp == steps - 1)
        def _():
            for cp in copies:
                cp.wait()

    col = pl.BlockSpec((S, LANES), lambda i: (0, i))
    res = pl.pallas_call(
        body, name="attn_bwd", grid=(steps,),
        in_specs=[pl.BlockSpec(memory_space=pltpu.SMEM)] + [col] * 6 + [ANY_SPEC] * ng,
        out_specs=[col] * 3 + [ANY_SPEC] * ng,
        out_shape=[jax.ShapeDtypeStruct((S, ATT_WIDTH), F32)] * 3 + _exchange_shapes(grads, scatter),
        scratch_shapes=[pltpu.VMEM((S, LANES), F32), pltpu.VMEM((2, 2 * SPAN, 2 * SPAN), F32)] + _exchange_sems(ng),
        compiler_params=_params("arbitrary"),
    )(slopes, q, k, v, o, lse, do, *grads)
    return res[0], res[1], res[2], res[3:]


def _in_bwd(dq, dk, dv, du, w_in, x, g1, dh1):
    S = x.shape[0]

    def body(dq_ref, dk_ref, dv_ref, du_ref, w_ref, x_ref, g_ref, dh1_ref, dz_ref, dx_ref, dg_ref):
        @pl.when(pl.program_id(0) == 0)
        def _():
            dg_ref[...] = jnp.zeros(dg_ref.shape, F32)

        srcs = (dq_ref, dk_ref, dv_ref, du_ref)
        dhn = jnp.zeros((TM, D_MODEL), F32)
        for j in range(N_DEV):
            dz = srcs[j // 2][:, (j % 2) * 256:(j % 2 + 1) * 256]
            if j < 2:
                dz = dz * (HEAD_DIM ** -0.5)
            dz = dz.astype(BF16)
            dz_ref[j] = dz
            dhn = dhn + lax.dot_general(dz, w_ref[j], NT, preferred_element_type=F32)
        g = g_ref[...]
        r, n = _rms(x_ref[...])
        dg_ref[...] += _colsum(dhn * n)
        dx_ref[...] = dh1_ref[...] + _rms_bwd(r, n, g, dhn)

    tok = lambda w: pl.BlockSpec((TM, w), lambda i: (i, 0))
    return pl.pallas_call(
        body, name="in_bwd", grid=(S // TM,),
        in_specs=[tok(512)] * 4 + [_const(w_in.shape), tok(D_MODEL), _const((1, D_MODEL)), tok(D_MODEL)],
        out_specs=[pl.BlockSpec((N_DEV, TM, 256), lambda i: (0, i, 0)), tok(D_MODEL),
                   pl.BlockSpec((1, D_MODEL), lambda i: (0, 0))],
        out_shape=[jax.ShapeDtypeStruct((N_DEV, S, 256), BF16), jax.ShapeDtypeStruct((S, D_MODEL), F32),
                   jax.ShapeDtypeStruct((1, D_MODEL), F32)],
        compiler_params=_params("arbitrary"),
    )(dq, dk, dv, du, w_in, x, g1, dh1)


def _adamw(name, parts, w, m, v):
    R, C = w.shape
    rb = R
    for cand in (256, 128, 64, 32, 16, 8):
        if R % cand == 0 and R > cand:
            rb = cand
            break

    def body(p_ref, w_ref, m_ref, v_ref, g_ref, d_ref, mo_ref, vo_ref):
        g = p_ref[0].astype(F32)
        for s in range(1, N_DEV):
            g = g + p_ref[s].astype(F32)
        g_ref[...] = g
        d_ref[...], mo_ref[...], vo_ref[...] = _adam_update(g, w_ref[...], m_ref[...], v_ref[...])

    blk = pl.BlockSpec((rb, C), lambda i: (i, 0))
    return pl.pallas_call(
        body, name=name, grid=(R // rb,),
        in_specs=[pl.BlockSpec((N_DEV, rb, C), lambda i: (0, i, 0)), blk, blk, blk],
        out_specs=[blk] * 4,
        out_shape=[jax.ShapeDtypeStruct((R, C), F32)] * 4,
        compiler_params=_params("arbitrary"),
    )(parts, w, m, v)


def _adam_update(g, w, m, v):
    m_new = ADAM_B1 * m + (1.0 - ADAM_B1) * g
    v_new = ADAM_B2 * v + (1.0 - ADAM_B2) * (g * g)
    m_hat = m_new / (1.0 - ADAM_B1 ** ADAM_STEP)
    v_hat = v_new / (1.0 - ADAM_B2 ** ADAM_STEP)
    return -ADAM_LR * (m_hat / (jnp.sqrt(v_hat) + ADAM_EPS) + ADAM_WD * w), m_new, v_new


def _adamw_small(parts, loss_parts, ws, ms, vs):
    n = len(ws)

    def body(*refs):
        p_refs, lp_ref = refs[:n], refs[n]
        w_refs, m_refs, v_refs = refs[n + 1:2 * n + 1], refs[2 * n + 1:3 * n + 1], refs[3 * n + 1:4 * n + 1]
        outs = refs[4 * n + 1:]
        for i in range(n):
            g = p_refs[i][0]
            for s in range(1, N_DEV):
                g = g + p_refs[i][s]
            d, m_new, v_new = _adam_update(g, w_refs[i][...], m_refs[i][...], v_refs[i][...])
            outs[i][...] = g
            outs[n + i][...] = d
            outs[2 * n + i][...] = m_new
            outs[3 * n + i][...] = v_new
        tot = lp_ref[0]
        for s in range(1, N_DEV):
            tot = tot + lp_ref[s]
        outs[4 * n][...] = tot

    shapes = [jax.ShapeDtypeStruct(w.shape, F32) for w in ws]
    res = pl.pallas_call(
        body, name="adamw_replicated",
        out_shape=shapes * 4 + [jax.ShapeDtypeStruct(loss_parts.shape[1:], F32)],
        compiler_params=_params(),
    )(*parts, loss_parts, *ws, *ms, *vs)
    return res[:n], res[n:2 * n], res[2 * n:3 * n], res[3 * n:4 * n], res[4 * n]


def _gather2(name, arrays):
    n = len(arrays)

    def body(*refs):
        first, passed, last = _gather2_copies(refs[:n], refs[n:2 * n], *refs[2 * n:])
        for cp in first:
            cp.start()
        for arrival, cp in passed:
            arrival.wait_recv()
            cp.start()
        for wait in last:
            wait()

    return pl.pallas_call(
        body, name=name,
        in_specs=[ANY_SPEC] * n, out_specs=[ANY_SPEC] * n, out_shape=_exchange_shapes(arrays, (False,) * n),
        scratch_shapes=_exchange_sems(n),
    )(*arrays)


def _dw_in_exchange(hn, dz, small):
    S = hn.shape[0]
    nt = S // TK
    ns = len(small)
    kd, nd = hn.shape[1], dz.shape[2]
    me_arr = (4 * lax.axis_index("x") + 2 * lax.axis_index("y") + lax.axis_index("c")).astype(jnp.int32).reshape(1)

    def body(me_ref, x_ref, dy_ref, *rest):
        recv_ref = rest[ns]
        acc, stage, send_sems, recv_sems, own_sem = rest[2 * ns + 1:2 * ns + 6]
        j, t = pl.program_id(0), pl.program_id(1)
        x, y, c = lax.axis_index("x"), lax.axis_index("y"), lax.axis_index("c")
        me = 4 * x + 2 * y + c
        small_copies = _exchange_copies(rest[:ns], rest[ns + 1:2 * ns + 1], (False,) * ns, *rest[2 * ns + 6:])

        @pl.when((j == 0) & (t == 0))
        def _():
            for cp in small_copies:
                cp.start()

        @pl.when(t == 0)
        def _():
            acc[...] = jnp.zeros(acc.shape, F32)

        acc[...] += lax.dot_general(x_ref[...], dy_ref[...], TN, preferred_element_type=F32)

        def to_owner(k, owner):
            return pltpu.make_async_remote_copy(
                src_ref=stage.at[owner], dst_ref=recv_ref.at[me], send_sem=send_sems.at[k], recv_sem=recv_sems.at[k],
                device_id=(owner // 4, (owner // 2) % 2, owner % 2), device_id_type=MESH)

        own = pltpu.make_async_copy(stage.at[me], recv_ref.at[me], own_sem)

        @pl.when(t == nt - 1)
        def _():
            owner = (me + 1 + j) % N_DEV
            stage[owner] = acc[...].astype(BF16)

            @pl.when(j < N_DEV - 1)
            def _():
                to_owner(j, owner).start()

            @pl.when(j == N_DEV - 1)
            def _():
                own.start()
                own.wait()
                for k in range(N_DEV - 1):
                    to_owner(k, me).wait_send()
                    to_owner(k, me).wait_recv()
                for cp in small_copies:
                    cp.wait()

    slab = lambda j, me_ref: (me_ref[0] + 1 + j) % N_DEV
    grid_spec = pltpu.PrefetchScalarGridSpec(
        num_scalar_prefetch=1, grid=(N_DEV, nt),
        in_specs=[pl.BlockSpec((TK, kd), lambda j, t, me_ref: (t, 0)),
                  pl.BlockSpec((None, TK, nd), lambda j, t, me_ref: (slab(j, me_ref), t, 0))] + [ANY_SPEC] * ns,
        out_specs=[ANY_SPEC] * (ns + 1),
        scratch_shapes=[pltpu.VMEM((kd, nd), F32), pltpu.VMEM((N_DEV, kd, nd), BF16),
                        pltpu.SemaphoreType.DMA((N_DEV - 1,)), pltpu.SemaphoreType.DMA((N_DEV - 1,)),
                        pltpu.SemaphoreType.DMA] + _exchange_sems(ns))
    res = pl.pallas_call(
        body, name="dw_in_exchange", grid_spec=grid_spec,
        out_shape=[jax.ShapeDtypeStruct((N_DEV, kd, nd), BF16)] + _exchange_shapes(small, (False,) * ns),
        compiler_params=_params("arbitrary", "arbitrary"),
    )(me_arr, hn, dz, *small)
    return res[0], res[1:]


def kernel(x, p, ln_mix, w_in, pool_w, pool_scale, w_out, ln_ffn, w_up, conv_w, conv_b, w_down, ln_ple, w_ple_gate, w_ple, ln_final, loss_target, m_ln_mix, m_w_in, m_pool_w, m_pool_scale, m_w_out, m_ln_ffn, m_w_up, m_conv_w, m_conv_b, m_w_down, m_ln_ple, m_w_ple_gate, m_w_ple, m_ln_final, v_ln_mix, v_w_in, v_pool_w, v_pool_scale, v_w_out, v_ln_ffn, v_w_up, v_conv_w, v_conv_b, v_w_down, v_ln_ple, v_w_ple_gate, v_w_ple, v_ln_final):
    xs, ps, tgt, pool_w0 = x[0], p[0, 0], loss_target[0], pool_w[0]
    slopes = jnp.exp2(-8.0 * (jnp.arange(N_HEADS, dtype=F32) + 1.0) / N_HEADS)
    conv_b_s = conv_b.reshape(N_DEV, 1, FF_SHARD)

    (w_in_g,) = _gather2("gather_w_in", [w_in[0].astype(BF16)])
    (q, k, v, u, hn1), (w_out_g,) = _qkvu(xs, ln_mix, w_in_g, [w_out[0].astype(BF16)])
    att, lse, (w_up_g, conv_w_g) = _attn_fwd(slopes, q, k, v, [w_up[0].astype(BF16), conv_w[0]])
    w_out_f = w_out_g.reshape(D_MODEL, D_MODEL)
    h1, mix, dlt, (w_down_g,) = _mix_out(xs, att, u, pool_w0, pool_scale, w_out_f, [w_down[0].astype(BF16)])
    w_down_f = w_down_g.reshape(4, FF_SHARD, D_MODEL)
    h2, hn2, up, upc, (w_pg_g, w_ple_g) = _ffn_fwd(h1, ln_ffn, w_up_g, conv_w_g, conv_b_s, w_down_f,
                                                   [w_ple_gate[0].astype(BF16), w_ple[0].astype(BF16)])
    w_pg_f = w_pg_g.reshape(D_MODEL, D_MODEL)
    w_ple_f = jnp.transpose(w_ple_g, (1, 0, 2)).reshape(PLE_DIM, D_MODEL)
    loss_blk, dh2, dh2b, hn3, dgl, dpe, d_ln_ple, d_ln_final = _head(
        h2, ps, ln_ple, w_pg_f, w_ple_f, ln_final.reshape(1, D_MODEL), tgt)

    d_w_pg = _wgrad("dw_ple_gate", hn3, dgl, "full", "full", 1, D_MODEL, D_MODEL, tk=TM * 2)
    d_w_pg = d_w_pg.reshape(N_DEV, D_MODEL // N_DEV, D_MODEL)
    d_w_ple = _wgrad("dw_ple", ps, dpe, "full", "full", 1, PLE_DIM, D_MODEL, tk=TM * 2)
    d_w_ple = jnp.transpose(d_w_ple.reshape(PLE_DIM, N_DEV, LANES), (1, 0, 2))
    a, dup, d_conv_w, d_conv_b, (r_w_pg, r_w_ple) = _ffn_bwd_a(dh2b, up, upc, w_down_f, [d_w_pg, d_w_ple])
    d_w_down = _wgrad("dw_down", a, dh2b, "lead", "full", 4, FF_SHARD, D_MODEL).reshape(N_DEV, D_FF // N_DEV, D_MODEL)
    dpre, dh1, dh1b, d_ln_ffn, (r_conv_w, r_w_down) = _ffn_bwd_b(
        dup, conv_w_g, w_up_g, h1, ln_ffn, dh2, [d_conv_w, d_w_down])
    datt, du, d_pool_w, d_pool_scale = _mix_bwd(dh1b, w_out_f, dlt, pool_w0, pool_scale)
    d_w_out = _wgrad("dw_out", mix, dh1b, "full", "full", 1, D_MODEL, D_MODEL, tk=TM * 2)
    d_w_out = d_w_out.reshape(N_DEV, D_MODEL // N_DEV, D_MODEL)
    d_w_up = _wgrad("dw_up", dpre, hn2, "lead", "full", N_DEV, FF_SHARD, D_MODEL)
    rep_late = [d_pool_w, d_pool_scale, d_ln_ffn, d_conv_b.reshape(1, 2 * D_FF), d_ln_ple, d_ln_final, loss_blk]
    dq, dk, dv, received = _attn_bwd(slopes, q, k, v, att, lse, datt, [d_w_out, d_w_up] + rep_late,
                                     (True, True) + (False,) * len(rep_late))
    r_w_out, r_w_up, r_rep = received[0], received[1], list(received[2:])
    dz, grad_x, d_ln_mix = _in_bwd(dq, dk, dv, du, w_in_g, xs, ln_mix, dh1)

    rep_names = ("ln_mix", "pool_w", "pool_scale", "ln_ffn", "conv_b", "ln_ple", "ln_final")
    rep_w = [ln_mix, pool_w0, pool_scale, ln_ffn, conv_b, ln_ple, ln_final.reshape(1, D_MODEL)]
    rep_m = [m_ln_mix, m_pool_w[0], m_pool_scale, m_ln_ffn, m_conv_b, m_ln_ple, m_ln_final.reshape(1, D_MODEL)]
    rep_v = [v_ln_mix, v_pool_w[0], v_pool_scale, v_ln_ffn, v_conv_b, v_ln_ple, v_ln_final.reshape(1, D_MODEL)]
    r_w_in, (r_ln_mix,) = _dw_in_exchange(hn1, dz, [d_ln_mix])
    small = _adamw_small([r_ln_mix] + r_rep[:-1], r_rep[-1], rep_w, rep_m, rep_v)
    loss = small[4][0, 0]

    sharded = {}
    sharded["w_in"] = _adamw("adamw_w_in", r_w_in, w_in[0], m_w_in[0], v_w_in[0])
    sharded["w_out"] = _adamw("adamw_w_out", r_w_out, w_out[0], m_w_out[0], v_w_out[0])
    sharded["w_up"] = [t.T for t in _adamw("adamw_w_up", r_w_up, w_up[0].T, m_w_up[0].T, v_w_up[0].T)]
    sharded["conv_w"] = _adamw("adamw_conv_w", r_conv_w, conv_w[0], m_conv_w[0], v_conv_w[0])
    sharded["w_down"] = _adamw("adamw_w_down", r_w_down, w_down[0], m_w_down[0], v_w_down[0])
    sharded["w_ple_gate"] = _adamw("adamw_w_ple_gate", r_w_pg, w_ple_gate[0], m_w_ple_gate[0], v_w_ple_gate[0])
    sharded["w_ple"] = _adamw("adamw_w_ple", r_w_ple, w_ple[0], m_w_ple[0], v_w_ple[0])

    shapes = dict(w_in=w_in, w_out=w_out, w_up=w_up, conv_w=conv_w, w_down=w_down, w_ple_gate=w_ple_gate, w_ple=w_ple,
                  ln_mix=ln_mix, pool_w=pool_w, pool_scale=pool_scale, ln_ffn=ln_ffn, conv_b=conv_b, ln_ple=ln_ple,
                  ln_final=ln_final)

    def leaf(kind, n):
        src = sharded[n][kind] if n in sharded else small[kind][rep_names.index(n)]
        return src.reshape(shapes[n].shape)

    order = ("ln_mix", "w_in", "pool_w", "pool_scale", "w_out", "ln_ffn", "w_up", "conv_w", "conv_b", "w_down", "ln_ple",
             "w_ple_gate", "w_ple", "ln_final")
    outs = [loss, grad_x[None]]
    for kind in range(4):
        outs += [leaf(kind, n) for n in order]
    return tuple(outs)
```

```python
import jax
import jax.numpy as jnp
from jax import lax
from jax.experimental import pallas as pl
from jax.experimental.pallas import tpu as pltpu

F32 = jnp.float32
BF16 = jnp.bfloat16

N_DEV = 8
D_MODEL = 1024
ATT_WIDTH = 512
POOL_WIDTH = 512
N_HEADS = 8
HEAD_DIM = 64
SPAN = 128
DILATIONS = (1, 4, 16)
POOL_WINDOWS = (2, 4, 8, 16)
POOL_GROUP = 128
D_FF = 2816
FF_SHARD = 2 * D_FF // N_DEV
PLE_DIM = 256
EPS = 1e-6
NEG = -1e30

ADAM_LR = 0.001
ADAM_B1 = 0.9
ADAM_B2 = 0.999
ADAM_EPS = 1e-08
ADAM_WD = 0.01
ADAM_STEP = 10

LANES = 128
HALO = 16
TM = 512
TM_FF = 256
TK = 4096
ATTN_GROUP_FWD = 16
ATTN_GROUP_BWD = 8
VMEM_LIMIT = 56 * 1024 * 1024

MESH = pl.DeviceIdType.MESH
NT = (((1,), (1,)), ((), ()))
TN = (((0,), (0,)), ((), ()))


def _params(*sem):
    return pltpu.CompilerParams(dimension_semantics=sem or None, vmem_limit_bytes=VMEM_LIMIT)


def _const(shape):
    n = len(shape)
    return pl.BlockSpec(shape, lambda *_: (0,) * n, pipeline_mode=pl.Buffered(1))


def _rms(h):
    r = lax.rsqrt(jnp.mean(h * h, axis=-1, keepdims=True) + EPS)
    return r, h * r


def _rms_bwd(r, n, g, dhn):
    dn = dhn * g
    return r * (dn - n * jnp.mean(dn * n, axis=-1, keepdims=True))


def _colsum(a):
    return jnp.sum(a, axis=0, keepdims=True)


def _gather2_copies(ins, outs, send_sems, recv_sems, local_sems):
    n = len(ins)
    x, y, c = lax.axis_index("x"), lax.axis_index("y"), lax.axis_index("c")
    slot = lambda px, py, pc: 4 * px + 2 * py + pc
    chips = [(x, 1 - y), (1 - x, y), (1 - x, 1 - y)]
    first, passed, last = [], [], []

    def remote(a, r, src, dst_slot, to):
        return pltpu.make_async_remote_copy(
            src_ref=src, dst_ref=outs[a].at[dst_slot],
            send_sem=send_sems.at[a * (N_DEV - 1) + r], recv_sem=recv_sems.at[a * (N_DEV - 1) + r],
            device_id=to, device_id_type=MESH)

    for a in range(n):
        mine = pltpu.make_async_copy(ins[a], outs[a].at[slot(x, y, c)], local_sems.at[a])
        to_sibling = remote(a, 0, ins[a], slot(x, y, c), (x, y, 1 - c))
        first += [mine, to_sibling]
        last += [mine.wait, to_sibling.wait_send, to_sibling.wait_recv]
        for r, (px, py) in enumerate(chips, start=1):
            to_chip = remote(a, r, ins[a], slot(x, y, c), (px, py, c))
            onward = remote(a, 3 + r, outs[a].at[slot(px, py, c)], slot(px, py, c), (x, y, 1 - c))
            first.append(to_chip)
            passed.append((to_chip, onward))
            last += [to_chip.wait_send, onward.wait_send, onward.wait_recv]
    return first, passed, last


def _gather2_begin(plan, step, pass_step):
    first, passed, _ = plan

    @pl.when(step == 0)
    def _():
        for cp in first:
            cp.start()

    @pl.when(step == pass_step)
    def _():
        for arrival, cp in passed:
            arrival.wait_recv()
            cp.start()


def _gather2_end(plan, step, nsteps):
    @pl.when(step == nsteps - 1)
    def _():
        for wait in plan[2]:
            wait()


ANY_SPEC = pl.BlockSpec(memory_space=pl.ANY)


def _exchange_shapes(arrays, scatter):
    out = []
    for a, s in zip(arrays, scatter):
        slab = a.shape[1:] if s else a.shape
        out.append(jax.ShapeDtypeStruct((N_DEV,) + tuple(slab), a.dtype))
    return out


def _exchange_sems(n):
    return [pltpu.SemaphoreType.DMA((n * (N_DEV - 1),)), pltpu.SemaphoreType.DMA((n * (N_DEV - 1),)),
            pltpu.SemaphoreType.DMA((n,))]


def _exchange_copies(ins, outs, scatter, send_sems, recv_sems, local_sems):
    n = len(ins)
    x, y, c = lax.axis_index("x"), lax.axis_index("y"), lax.axis_index("c")
    me = 4 * x + 2 * y + c
    copies = []
    for a in range(n):
        src = ins[a].at[me] if scatter[a] else ins[a]
        copies.append(pltpu.make_async_copy(src, outs[a].at[me], local_sems.at[a]))
    for k in range(1, N_DEV):
        px = 1 - x if k & 4 else x
        py = 1 - y if k & 2 else y
        pc = 1 - c if k & 1 else c
        pid = 4 * px + 2 * py + pc
        for a in range(n):
            src = ins[a].at[pid] if scatter[a] else ins[a]
            copies.append(pltpu.make_async_remote_copy(
                src_ref=src, dst_ref=outs[a].at[me],
                send_sem=send_sems.at[a * (N_DEV - 1) + k - 1], recv_sem=recv_sems.at[a * (N_DEV - 1) + k - 1],
                device_id=(px, py, pc), device_id_type=MESH))
    return copies


def _qkvu(x, g1, w_in, shards):
    S = x.shape[0]
    ns = len(shards)
    nsteps = S // TM

    def body(x_ref, g_ref, w_ref, *rest):
        q_ref, k_ref, v_ref, u_ref, hn_ref = rest[ns:ns + 5]
        plan = _gather2_copies(rest[:ns], rest[ns + 5:2 * ns + 5], *rest[2 * ns + 5:])
        _gather2_begin(plan, pl.program_id(0), nsteps - 2)
        r, n = _rms(x_ref[...])
        hn = (n * g_ref[...]).astype(BF16)
        hn_ref[...] = hn
        outs = (q_ref, k_ref, v_ref, u_ref)
        for j in range(N_DEV):
            z = jnp.dot(hn, w_ref[j], preferred_element_type=F32)
            if j < 2:
                z = z * (HEAD_DIM ** -0.5)
            outs[j // 2][:, (j % 2) * 256:(j % 2 + 1) * 256] = z
        _gather2_end(plan, pl.program_id(0), nsteps)

    tok = lambda w: pl.BlockSpec((TM, w), lambda i: (i, 0))
    res = pl.pallas_call(
        body, name="qkvu", grid=(nsteps,),
        in_specs=[tok(D_MODEL), _const((1, D_MODEL)), _const(w_in.shape)] + [ANY_SPEC] * ns,
        out_specs=[tok(512)] * 4 + [tok(D_MODEL)] + [ANY_SPEC] * ns,
        out_shape=[jax.ShapeDtypeStruct((S, 512), F32)] * 4 + [jax.ShapeDtypeStruct((S, D_MODEL), BF16)]
        + _exchange_shapes(shards, (False,) * ns),
        scratch_shapes=_exchange_sems(ns),
        compiler_params=_params("arbitrary"),
    )(x, g1, w_in, *shards)
    return res[:5], res[5:]


def _attn_fill_bias(bias_s, slope_ref, hp, d):
    qi = lax.broadcasted_iota(jnp.int32, (SPAN, 2 * SPAN), 0)
    kj = lax.broadcasted_iota(jnp.int32, (SPAN, 2 * SPAN), 1)
    for t, diff in enumerate((qi + SPAN - kj, qi - kj)):
        valid = (diff >= 0) & (diff <= SPAN)
        dist = diff.astype(F32) * float(d)
        for h in range(2):
            bias_s[t, h * SPAN:(h + 1) * SPAN, :] = jnp.where(valid, -slope_ref[2 * hp + h] * dist, NEG)


def _stack_heads(x, is0):
    return jnp.concatenate([jnp.where(is0, x, 0.0), jnp.where(is0, 0.0, x)], axis=0)


def _unstack_heads(y, is0):
    return jnp.where(is0, y[0:SPAN], y[SPAN:2 * SPAN])


def _attn_block(i, g, d, nb, group):
    gr = min(d, group)
    gn = group // gr
    per = d // gr
    r = (i & (per - 1)) * gr + g % gr
    n = (i >> (per.bit_length() - 1)) + (g // gr) * (nb // gn)
    k0 = jnp.maximum(n - 1, 0)

    def ds(block, nrows):
        start = block * (SPAN * d) + r
        return pl.ds(start, nrows, stride=d) if d > 1 else pl.ds(start, nrows)

    return ds(n, SPAN), ds(k0, 2 * SPAN), jnp.where(n == 0, 1, 0)


def _attn_groups(S, d, group):
    nb = S // d // SPAN
    gn = group // min(d, group)
    assert nb >= 2 and nb % gn == 0 and (gn == 1 or nb // gn >= 2)
    return nb, d * nb // group


def _attn_fwd(slopes, q, k, v, shards):
    S = q.shape[0]
    ns = len(shards)
    steps = ATT_WIDTH // LANES

    def body(slope_ref, q_ref, k_ref, v_ref, *rest):
        o_ref, lse_ref = rest[ns:ns + 2]
        m_s, l_s, bias_s = rest[2 * ns + 2:2 * ns + 5]
        hp = pl.program_id(0)
        plan = _gather2_copies(rest[:ns], rest[ns + 2:2 * ns + 2], *rest[2 * ns + 5:])
        _gather2_begin(plan, hp, steps - 1)

        is0 = lax.broadcasted_iota(jnp.int32, (SPAN, LANES), 1) < HEAD_DIM
        for pi, d in enumerate(DILATIONS):
            nb, ngroups = _attn_groups(S, d, ATTN_GROUP_FWD)
            _attn_fill_bias(bias_s, slope_ref, hp, d)

            def group(i, carry, d=d, pi=pi, nb=nb):
                blocks = [_attn_block(i, g, d, nb, ATTN_GROUP_FWD) for g in range(ATTN_GROUP_FWD)]
                loaded = [(q_ref[rows, :], k_ref[krows, :].astype(BF16), v_ref[krows, :].astype(BF16))
                          for rows, krows, _ in blocks]
                new = []
                for (rows, krows, tab), (qb, kb, vb) in zip(blocks, loaded):
                    qs = _stack_heads(qb, is0).astype(BF16)
                    s = lax.dot_general(qs, kb, NT, preferred_element_type=F32) + bias_s[tab]
                    m = jnp.max(s, axis=-1, keepdims=True)
                    e = jnp.exp(s - m)
                    l = jnp.sum(e, axis=-1, keepdims=True)
                    pv = jnp.dot(e.astype(BF16), vb, preferred_element_type=F32)
                    new.append([_unstack_heads(jnp.broadcast_to(m, pv.shape), is0),
                                _unstack_heads(jnp.broadcast_to(l, pv.shape), is0), _unstack_heads(pv, is0)])
                if pi > 0:
                    old = [(m_s[rows, :], l_s[rows, :], o_ref[rows, :]) for rows, _, _ in blocks]
                    for st, (m_o, l_o, o_o) in zip(new, old):
                        m_n = jnp.maximum(m_o, st[0])
                        a_o = jnp.exp(m_o - m_n)
                        a_b = jnp.exp(st[0] - m_n)
                        st[:] = [m_n, a_o * l_o + a_b * st[1], a_o * o_o + a_b * st[2]]
                for (rows, _, _), (m_b, l_b, acc) in zip(blocks, new):
                    if pi == len(DILATIONS) - 1:
                        o_ref[rows, :] = acc / l_b
                        lse_ref[rows, :] = m_b + jnp.log(l_b)
                    else:
                        o_ref[rows, :] = acc
                        m_s[rows, :] = m_b
                        l_s[rows, :] = l_b
                return carry

            lax.fori_loop(0, ngroups, group, 0)

        _gather2_end(plan, hp, steps)

    col = pl.BlockSpec((S, LANES), lambda i: (0, i))
    res = pl.pallas_call(
        body, name="attn_fwd", grid=(steps,),
        in_specs=[pl.BlockSpec(memory_space=pltpu.SMEM), col, col, col] + [ANY_SPEC] * ns,
        out_specs=[col, col] + [ANY_SPEC] * ns,
        out_shape=[jax.ShapeDtypeStruct((S, ATT_WIDTH), F32)] * 2 + _exchange_shapes(shards, (False,) * ns),
        scratch_shapes=[pltpu.VMEM((S, LANES), F32), pltpu.VMEM((S, LANES), F32),
                        pltpu.VMEM((2, 2 * SPAN, 2 * SPAN), F32)] + _exchange_sems(ns),
        compiler_params=_params("arbitrary"),
    )(slopes, q, k, v, *shards)
    return res[0], res[1], res[2:]


def _pool_count(i, w):
    t = i * TM + lax.broadcasted_iota(jnp.int32, (TM, 1), 0)
    return jnp.minimum(t + 1, w).astype(F32)


def _mix_out(x, att, u, pool_w, pool_scale, w_out, shards):
    S = x.shape[0]
    ns = len(shards)
    nsteps = S // TM

    def body(x_ref, att_ref, u_ref, pw_ref, ps_ref, w_ref, *rest):
        h1_ref, mix_ref, dlt_ref = rest[ns:ns + 3]
        ubuf = rest[2 * ns + 3]
        i = pl.program_id(0)
        plan = _gather2_copies(rest[:ns], rest[ns + 3:2 * ns + 3], *rest[2 * ns + 4:])
        _gather2_begin(plan, i, nsteps - 1)

        @pl.when(i == 0)
        def _():
            ubuf[0:HALO, :] = jnp.zeros((HALO, POOL_WIDTH), F32)

        ubuf[HALO:HALO + TM, :] = u_ref[...]
        mix_ref[:, 0:ATT_WIDTH] = att_ref[...].astype(BF16)
        for g, w in enumerate(POOL_WINDOWS):
            cols = slice(g * POOL_GROUP, (g + 1) * POOL_GROUP)
            ug = ubuf[HALO:HALO + TM, cols]
            acc = ug
            for j in range(1, w):
                acc = acc + ubuf[HALO - j:HALO - j + TM, cols]
            dlt = (acc / _pool_count(i, w) - ug).astype(BF16)
            dlt_ref[:, cols] = dlt
            yg = jnp.dot(dlt, pw_ref[g].astype(BF16), preferred_element_type=F32) * ps_ref[:, cols]
            mix_ref[:, ATT_WIDTH + g * POOL_GROUP:ATT_WIDTH + (g + 1) * POOL_GROUP] = yg.astype(BF16)
        ubuf[0:HALO, :] = ubuf[TM:TM + HALO, :]
        h1_ref[...] = x_ref[...] + jnp.dot(mix_ref[...], w_ref[...], preferred_element_type=F32)
        _gather2_end(plan, i, nsteps)

    tok = lambda w: pl.BlockSpec((TM, w), lambda i: (i, 0))
    res = pl.pallas_call(
        body, name="mix_out", grid=(nsteps,),
        in_specs=[tok(D_MODEL), tok(ATT_WIDTH), tok(POOL_WIDTH), _const(pool_w.shape), _const((1, POOL_WIDTH)),
                  _const(w_out.shape)] + [ANY_SPEC] * ns,
        out_specs=[tok(D_MODEL), tok(D_MODEL), tok(POOL_WIDTH)] + [ANY_SPEC] * ns,
        out_shape=[jax.ShapeDtypeStruct((S, D_MODEL), F32), jax.ShapeDtypeStruct((S, D_MODEL), BF16),
                   jax.ShapeDtypeStruct((S, POOL_WIDTH), BF16)] + _exchange_shapes(shards, (False,) * ns),
        scratch_shapes=[pltpu.VMEM((TM + HALO, POOL_WIDTH), F32)] + _exchange_sems(ns),
        compiler_params=_params("arbitrary"),
    )(x, att, u, pool_w, pool_scale, w_out, *shards)
    return res[0], res[1], res[2], res[3:]


def _conv_fwd(stage, upre, prev, cw, cb):
    T = upre.shape[0]
    stage[0:HALO, :] = prev
    stage[HALO:HALO + T, :] = upre
    return cb + cw[0:1, :] * stage[HALO - 2:HALO - 2 + T, :] + cw[1:2, :] * stage[HALO - 1:HALO - 1 + T, :] + cw[2:3, :] * upre


def _ffn_fwd(h1, g2, w_up, conv_w, conv_b, w_down, shards):
    S = h1.shape[0]
    T = TM_FF
    ns = len(shards)
    nsteps = S // T

    def body(h1_ref, g_ref, wu_ref, cw_ref, cb_ref, wd_ref, *rest):
        h2_ref, hn_ref, up_ref, upc_ref = rest[ns:ns + 4]
        carry, stage = rest[2 * ns + 4:2 * ns + 6]
        i = pl.program_id(0)
        plan = _gather2_copies(rest[:ns], rest[ns + 4:2 * ns + 4], *rest[2 * ns + 6:])
        _gather2_begin(plan, i, nsteps // 2)

        @pl.when(i == 0)
        def _():
            carry[...] = jnp.zeros(carry.shape, F32)

        h1t = h1_ref[...]
        r, n = _rms(h1t)
        hn = (n * g_ref[...]).astype(BF16)
        hn_ref[...] = hn
        acc = h1t
        for j in range(4):
            conv = []
            for jj in (j, j + 4):
                upre = jnp.dot(hn, wu_ref[jj], preferred_element_type=F32)
                up_ref[jj] = upre.astype(BF16)
                conv.append(_conv_fwd(stage, upre, carry[jj], cw_ref[jj], cb_ref[jj]))
                upc_ref[jj] = conv[-1].astype(BF16)
                carry[jj] = stage[T:T + HALO, :]
            gate, val = conv
            a = gate * jax.nn.sigmoid(gate) * val
            acc = acc + jnp.dot(a.astype(BF16), wd_ref[j], preferred_element_type=F32)
        h2_ref[...] = acc
        _gather2_end(plan, i, nsteps)

    tok = lambda w: pl.BlockSpec((T, w), lambda i: (i, 0))
    res = pl.pallas_call(
        body, name="ffn_fwd", grid=(nsteps,),
        in_specs=[tok(D_MODEL), _const((1, D_MODEL)), _const(w_up.shape), _const(conv_w.shape), _const(conv_b.shape),
                  _const(w_down.shape)] + [ANY_SPEC] * ns,
        out_specs=[tok(D_MODEL), tok(D_MODEL)] + [pl.BlockSpec((N_DEV, T, FF_SHARD), lambda i: (0, i, 0))] * 2
        + [ANY_SPEC] * ns,
        out_shape=[jax.ShapeDtypeStruct((S, D_MODEL), F32), jax.ShapeDtypeStruct((S, D_MODEL), BF16)]
        + [jax.ShapeDtypeStruct((N_DEV, S, FF_SHARD), BF16)] * 2 + _exchange_shapes(shards, (False,) * ns),
        scratch_shapes=[pltpu.VMEM((N_DEV, HALO, FF_SHARD), F32), pltpu.VMEM((T + HALO, FF_SHARD), F32)]
        + _exchange_sems(ns),
        compiler_params=_params("arbitrary"),
    )(h1, g2, w_up, conv_w, conv_b, w_down, *shards)
    return res[0], res[1], res[2], res[3], res[4:]


def _head(h2, p, g3, w_pg, w_ple, g4, target):
    S = h2.shape[0]
    nt = S // TM

    def body(h2_ref, p_ref, g3_ref, wpg_ref, wple_ref, g4_ref, t_ref,
             loss_ref, dh2_ref, dh2b_ref, dwpg_ref, dwple_ref, dg3_ref, dg4_ref, lacc, pg_acc, ple_acc):
        i = pl.program_id(0)

        @pl.when(i == 0)
        def _():
            lacc[...] = jnp.zeros(lacc.shape, F32)
            pg_acc[...] = jnp.zeros(pg_acc.shape, F32)
            ple_acc[...] = jnp.zeros(ple_acc.shape, F32)
            dg3_ref[...] = jnp.zeros(dg3_ref.shape, F32)
            dg4_ref[...] = jnp.zeros(dg4_ref.shape, F32)

        h2t = h2_ref[...]
        g3, g4 = g3_ref[...], g4_ref[...]
        r3, n3 = _rms(h2t)
        hn3 = (n3 * g3).astype(BF16)
        pb = p_ref[...].astype(BF16)
        gs = jax.nn.sigmoid(jnp.dot(hn3, wpg_ref[...], preferred_element_type=F32))
        pe = jnp.dot(pb, wple_ref[...], preferred_element_type=F32)
        h3 = h2t + gs * pe
        r4, n4 = _rms(h3)
        err = n4 * g4 - t_ref[...]
        lacc[...] += _colsum(err * err)
        dy = err * (1.0 / D_MODEL)
        dg4_ref[...] += _colsum(dy * n4)
        dh3 = _rms_bwd(r4, n4, g4, dy)
        dpe = (dh3 * gs).astype(BF16)
        dgl = (dh3 * pe * gs * (1.0 - gs)).astype(BF16)
        ple_acc[...] += lax.dot_general(pb, dpe, TN, preferred_element_type=F32)
        pg_acc[...] += lax.dot_general(hn3, dgl, TN, preferred_element_type=F32)
        dhn3 = lax.dot_general(dgl, wpg_ref[...], NT, preferred_element_type=F32)
        dg3_ref[...] += _colsum(dhn3 * n3)
        dh2 = dh3 + _rms_bwd(r3, n3, g3, dhn3)
        dh2_ref[...] = dh2
        dh2b_ref[...] = dh2.astype(BF16)

        @pl.when(i == nt - 1)
        def _():
            tot = 0.5 / D_MODEL * jnp.sum(lacc[...], axis=-1, keepdims=True)
            loss_ref[...] = jnp.broadcast_to(tot, loss_ref.shape)
            dwpg_ref[...] = pg_acc[...].astype(BF16)
            dwple_ref[...] = ple_acc[...].astype(BF16)

    tok = lambda w: pl.BlockSpec((TM, w), lambda i: (i, 0))
    row = pl.BlockSpec((1, D_MODEL), lambda i: (0, 0))
    act = lambda dt: jax.ShapeDtypeStruct((S, D_MODEL), dt)
    whole = lambda r: pl.BlockSpec((r, D_MODEL), lambda i: (0, 0))
    return pl.pallas_call(
        body, name="head", grid=(nt,),
        in_specs=[tok(D_MODEL), tok(PLE_DIM), _const((1, D_MODEL)), _const(w_pg.shape), _const(w_ple.shape),
                  _const((1, D_MODEL)), tok(D_MODEL)],
        out_specs=[pl.BlockSpec((8, LANES), lambda i: (0, 0)), tok(D_MODEL), tok(D_MODEL), whole(D_MODEL),
                   whole(PLE_DIM), row, row],
        out_shape=[jax.ShapeDtypeStruct((8, LANES), F32), act(F32), act(BF16),
                   jax.ShapeDtypeStruct((D_MODEL, D_MODEL), BF16), jax.ShapeDtypeStruct((PLE_DIM, D_MODEL), BF16),
                   jax.ShapeDtypeStruct((1, D_MODEL), F32), jax.ShapeDtypeStruct((1, D_MODEL), F32)],
        scratch_shapes=[pltpu.VMEM((1, D_MODEL), F32), pltpu.VMEM((D_MODEL, D_MODEL), F32),
                        pltpu.VMEM((PLE_DIM, D_MODEL), F32)],
        compiler_params=_params("arbitrary"),
    )(h2, p, g3, w_pg, w_ple, g4, target)


def _wgrad(name, x, dy, x_kind, dy_kind, nj, k_dim, n_dim, tk=TK):
    S = x.shape[-2]
    nt = S // tk

    def spec(kind, width):
        if kind == "full":
            return pl.BlockSpec((tk, width), lambda j, t: (t, 0))
        return pl.BlockSpec((None, tk, width), lambda j, t: (j, t, 0))

    def body(x_ref, dy_ref, o_ref, acc):
        t = pl.program_id(1)

        @pl.when(t == 0)
        def _():
            acc[...] = jnp.zeros(acc.shape, F32)

        acc[...] += lax.dot_general(x_ref[...].astype(BF16), dy_ref[...], TN, preferred_element_type=F32)

        @pl.when(t == nt - 1)
        def _():
            o_ref[...] = acc[...].astype(BF16)

    return pl.pallas_call(
        body, name=name, grid=(nj, nt),
        in_specs=[spec(x_kind, k_dim), spec(dy_kind, n_dim)],
        out_specs=pl.BlockSpec((None, k_dim, n_dim), lambda j, t: (j, 0, 0)),
        out_shape=jax.ShapeDtypeStruct((nj, k_dim, n_dim), BF16),
        scratch_shapes=[pltpu.VMEM((k_dim, n_dim), F32)],
        compiler_params=_params("arbitrary", "arbitrary"),
    )(x, dy)


def _row_picker(T, off0, off1):
    r = lax.broadcasted_iota(jnp.int32, (2 * T, T + HALO), 0)
    c = lax.broadcasted_iota(jnp.int32, (2 * T, T + HALO), 1)
    want = jnp.where(r < T, r + off0, r - T + off1)
    return jnp.where(c == want, 1.0, 0.0).astype(BF16)


def _ffn_bwd_a(dh2b, up, upc, w_down, grads):
    S = dh2b.shape[0]
    T = TM_FF
    hb = T // HALO
    nsteps = S // T
    ng = len(grads)

    def body(dh_ref, up_ref, halo_ref, upc_ref, wd_ref, *rest):
        dwd_ref, dup_ref, dcw_ref, dcb_ref = rest[ng:ng + 4]
        stage, dwd_acc = rest[2 * ng + 4:2 * ng + 6]
        i = pl.program_id(0)
        copies = _exchange_copies(rest[:ng], rest[ng + 4:2 * ng + 4], (True,) * ng, *rest[2 * ng + 6:])

        @pl.when(i == 0)
        def _():
            dcw_ref[...] = jnp.zeros(dcw_ref.shape, F32)
            dcb_ref[...] = jnp.zeros(dcb_ref.shape, F32)
            dwd_acc[...] = jnp.zeros(dwd_acc.shape, F32)
            for cp in copies:
                cp.start()

        @pl.when(i == nsteps - 1)
        def _():
            for cp in copies:
                cp.wait()

        dh = dh_ref[...]
        pick = _row_picker(T, HALO - 2, HALO - 1)
        for j in range(4):
            da = lax.dot_general(dh, wd_ref[j], NT, preferred_element_type=F32)
            taps = []
            for jj in (j, j + 4):
                upre = up_ref[jj]
                stage[0:HALO, :] = jnp.where(i > 0, halo_ref[jj], jnp.zeros((HALO, FF_SHARD), BF16))
                stage[HALO:HALO + T, :] = upre
                prv = jnp.dot(pick, stage[...], preferred_element_type=F32)
                taps.append((prv[0:T], prv[T:2 * T], upre.astype(F32)))
            gate, val = upc_ref[j].astype(F32), upc_ref[j + 4].astype(F32)
            sg = jax.nn.sigmoid(gate)
            silu = gate * sg
            dwd_acc[j] += lax.dot_general((silu * val).astype(BF16), dh, TN, preferred_element_type=F32)
            dgate = (da * val) * (sg + silu * (1.0 - sg))
            dval = da * silu
            for jj, dup, tp in ((j, dgate, taps[0]), (j + 4, dval, taps[1])):
                dup_ref[jj] = dup.astype(BF16)
                dcb_ref[jj] += _colsum(dup)
                for kk in range(3):
                    dcw_ref[jj, kk:kk + 1, :] += _colsum(dup * tp[kk])

        @pl.when(i == nsteps - 1)
        def _():
            dwd_ref[...] = dwd_acc[...].astype(BF16)

    tok = lambda w: pl.BlockSpec((T, w), lambda i: (i, 0))
    shard = pl.BlockSpec((N_DEV, T, FF_SHARD), lambda i: (0, i, 0))
    res = pl.pallas_call(
        body, name="ffn_bwd_a", grid=(nsteps,),
        in_specs=[tok(D_MODEL), shard,
                  pl.BlockSpec((N_DEV, HALO, FF_SHARD), lambda i: (0, jnp.maximum(i * hb - 1, 0), 0)),
                  shard, _const(w_down.shape)] + [ANY_SPEC] * ng,
        out_specs=[_const(w_down.shape), shard,
                   pl.BlockSpec((N_DEV, 3, FF_SHARD), lambda i: (0, 0, 0)),
                   pl.BlockSpec((N_DEV, 1, FF_SHARD), lambda i: (0, 0, 0))] + [ANY_SPEC] * ng,
        out_shape=[jax.ShapeDtypeStruct(w_down.shape, BF16), jax.ShapeDtypeStruct((N_DEV, S, FF_SHARD), BF16),
                   jax.ShapeDtypeStruct((N_DEV, 3, FF_SHARD), F32), jax.ShapeDtypeStruct((N_DEV, 1, FF_SHARD), F32)]
        + _exchange_shapes(grads, (True,) * ng),
        scratch_shapes=[pltpu.VMEM((T + HALO, FF_SHARD), BF16), pltpu.VMEM(w_down.shape, F32)] + _exchange_sems(ng),
        compiler_params=_params("arbitrary"),
    )(dh2b, up, up, upc, w_down, *grads)
    return res[0], res[1], res[2], res[3], res[4:]


def _ffn_bwd_b(dup, conv_w, w_up, h1, g2, dh2, grads):
    S = h1.shape[0]
    T = TM_FF
    hb = T // HALO
    nt = S // T
    ng = len(grads)

    def body(dup_ref, halo_ref, cw_ref, wu_ref, h1_ref, g_ref, dh2_ref, *rest):
        dpre_ref, dh1_ref, dh1b_ref, dg_ref = rest[ng:ng + 4]
        stage = rest[2 * ng + 4]
        i = pl.program_id(0)
        copies = _exchange_copies(rest[:ng], rest[ng + 4:2 * ng + 4], (True,) * ng, *rest[2 * ng + 5:])

        @pl.when(i == 0)
        def _():
            dg_ref[...] = jnp.zeros(dg_ref.shape, F32)
            for cp in copies:
                cp.start()

        dhn = jnp.zeros((T, D_MODEL), F32)
        for jj in range(N_DEV):
            dup = dup_ref[jj].astype(F32)
            stage[0:T, :] = dup
            stage[T:T + HALO, :] = jnp.where(i < nt - 1, halo_ref[jj].astype(F32), 0.0)
            cw = cw_ref[jj]
            dpre = (cw[2:3, :] * dup + cw[1:2, :] * stage[1:1 + T, :] + cw[0:1, :] * stage[2:2 + T, :]).astype(BF16)
            dpre_ref[jj] = dpre
            dhn = dhn + lax.dot_general(dpre, wu_ref[jj], NT, preferred_element_type=F32)
        g = g_ref[...]
        r, n = _rms(h1_ref[...])
        dg_ref[...] += _colsum(dhn * n)
        dh1 = dh2_ref[...] + _rms_bwd(r, n, g, dhn)
        dh1_ref[...] = dh1
        dh1b_ref[...] = dh1.astype(BF16)

        @pl.when(i == nt - 1)
        def _():
            for cp in copies:
                cp.wait()

    tok = lambda w: pl.BlockSpec((T, w), lambda i: (i, 0))
    shard = pl.BlockSpec((N_DEV, T, FF_SHARD), lambda i: (0, i, 0))
    res = pl.pallas_call(
        body, name="ffn_bwd_b", grid=(nt,),
        in_specs=[shard,
                  pl.BlockSpec((N_DEV, HALO, FF_SHARD), lambda i: (0, jnp.minimum((i + 1) * hb, S // HALO - 1), 0)),
                  _const(conv_w.shape), _const(w_up.shape), tok(D_MODEL), _const((1, D_MODEL)), tok(D_MODEL)]
        + [ANY_SPEC] * ng,
        out_specs=[shard, tok(D_MODEL), tok(D_MODEL), pl.BlockSpec((1, D_MODEL), lambda i: (0, 0))] + [ANY_SPEC] * ng,
        out_shape=[jax.ShapeDtypeStruct((N_DEV, S, FF_SHARD), BF16), jax.ShapeDtypeStruct((S, D_MODEL), F32),
                   jax.ShapeDtypeStruct((S, D_MODEL), BF16), jax.ShapeDtypeStruct((1, D_MODEL), F32)]
        + _exchange_shapes(grads, (True,) * ng),
        scratch_shapes=[pltpu.VMEM((T + HALO, FF_SHARD), F32)] + _exchange_sems(ng),
        compiler_params=_params("arbitrary"),
    )(dup, dup, conv_w, w_up, h1, g2, dh2, *grads)
    return res[0], res[1], res[2], res[3], res[4:]


def _mix_bwd(dh1b, w_out, dlt, pool_w, pool_scale):
    S = dh1b.shape[0]
    nt = S // TM

    def body(dh_ref, w_ref, dlt_ref, pw_ref, ps_ref, datt_ref, du_ref, dpw_ref, dps_ref, stage, carry):
        i = pl.program_id(0)
        tile = nt - 1 - i

        @pl.when(i == 0)
        def _():
            dpw_ref[...] = jnp.zeros(dpw_ref.shape, F32)
            dps_ref[...] = jnp.zeros(dps_ref.shape, F32)
            carry[...] = jnp.zeros(carry.shape, F32)

        dmix = lax.dot_general(dh_ref[...], w_ref[...], NT, preferred_element_type=F32)
        datt_ref[...] = dmix[:, 0:ATT_WIDTH]
        for g, w in enumerate(POOL_WINDOWS):
            cols = slice(g * POOL_GROUP, (g + 1) * POOL_GROUP)
            dpool = dmix[:, ATT_WIDTH + g * POOL_GROUP:ATT_WIDTH + (g + 1) * POOL_GROUP]
            dl = dlt_ref[:, cols]
            pw = pw_ref[g].astype(BF16)
            yg = jnp.dot(dl, pw, preferred_element_type=F32)
            dps_ref[:, cols] += _colsum(dpool * yg)
            dy = (dpool * ps_ref[:, cols]).astype(BF16)
            dpw_ref[g] += lax.dot_general(dl, dy, TN, preferred_element_type=F32)
            ddlt = lax.dot_general(dy, pw, NT, preferred_element_type=F32)
            cg = ddlt / _pool_count(tile, w)
            stage[0:TM, :] = cg
            stage[TM:TM + HALO, :] = carry[:, cols]
            acc = cg
            for j in range(1, w):
                acc = acc + stage[j:j + TM, :]
            du_ref[:, cols] = acc - ddlt
            carry[:, cols] = cg[0:HALO, :]

    tok = lambda w: pl.BlockSpec((TM, w), lambda i: (nt - 1 - i, 0))
    return pl.pallas_call(
        body, name="mix_bwd", grid=(nt,),
        in_specs=[tok(D_MODEL), _const(w_out.shape), tok(POOL_WIDTH), _const(pool_w.shape), _const((1, POOL_WIDTH))],
        out_specs=[tok(ATT_WIDTH), tok(POOL_WIDTH), pl.BlockSpec(pool_w.shape, lambda i: (0, 0, 0)),
                   pl.BlockSpec((1, POOL_WIDTH), lambda i: (0, 0))],
        out_shape=[jax.ShapeDtypeStruct((S, ATT_WIDTH), F32), jax.ShapeDtypeStruct((S, POOL_WIDTH), F32),
                   jax.ShapeDtypeStruct(pool_w.shape, F32), jax.ShapeDtypeStruct((1, POOL_WIDTH), F32)],
        scratch_shapes=[pltpu.VMEM((TM + HALO, POOL_GROUP), F32), pltpu.VMEM((HALO, POOL_WIDTH), F32)],
        compiler_params=_params("arbitrary"),
    )(dh1b, w_out, dlt, pool_w, pool_scale)


def _attn_bwd(slopes, q, k, v, o, lse, do, grads, scatter):
    S = q.shape[0]
    CH = 512
    ng = len(grads)
    steps = ATT_WIDTH // LANES

    def body(slope_ref, q_ref, k_ref, v_ref, o_ref, lse_ref, do_ref, *rest):
        dq_ref, dk_ref, dv_ref = rest[ng:ng + 3]
        dl_s, bias_s = rest[2 * ng + 3:2 * ng + 5]
        hp = pl.program_id(0)
        copies = _exchange_copies(rest[:ng], rest[ng + 3:2 * ng + 3], scatter, *rest[2 * ng + 5:])

        @pl.when(hp == 0)
        def _():
            for cp in copies:
                cp.start()

        is0 = lax.broadcasted_iota(jnp.int32, (SPAN, LANES), 1) < HEAD_DIM
        is0c = lax.broadcasted_iota(jnp.int32, (CH, LANES), 1) < HEAD_DIM

        def prep(ci, carry):
            rows = pl.ds(pl.multiple_of(ci * CH, CH), CH)
            prod = do_ref[rows, :] * o_ref[rows, :]
            d0 = jnp.sum(jnp.where(is0c, prod, 0.0), axis=-1, keepdims=True)
            d1 = jnp.sum(jnp.where(is0c, 0.0, prod), axis=-1, keepdims=True)
            dl_s[rows, :] = jnp.where(is0c, d0, d1)
            zero = jnp.zeros((CH, LANES), F32)
            dq_ref[rows, :] = zero
            dk_ref[rows, :] = zero
            dv_ref[rows, :] = zero
            return carry

        lax.fori_loop(0, S // CH, prep, 0)

        for d in DILATIONS:
            nb, ngroups = _attn_groups(S, d, ATTN_GROUP_BWD)
            _attn_fill_bias(bias_s, slope_ref, hp, d)

            def group(i, carry, d=d, nb=nb):
                blocks = [_attn_block(i, g, d, nb, ATTN_GROUP_BWD) for g in range(ATTN_GROUP_BWD)]
                loaded = [(q_ref[rows, :], do_ref[rows, :], lse_ref[rows, :], dl_s[rows, :], k_ref[krows, :],
                           v_ref[krows, :].astype(BF16)) for rows, krows, _ in blocks]
                new = []
                for (rows, krows, tab), (qb, dob, lse_b, dl_b, kf, vb) in zip(blocks, loaded):
                    kb = kf.astype(BF16)
                    qs = _stack_heads(qb, is0).astype(BF16)
                    dos = _stack_heads(dob, is0).astype(BF16)
                    lse_s = jnp.concatenate([lse_b[:, 0:1], lse_b[:, HEAD_DIM:HEAD_DIM + 1]], axis=0)
                    dl_s2 = jnp.concatenate([dl_b[:, 0:1], dl_b[:, HEAD_DIM:HEAD_DIM + 1]], axis=0)
                    s = lax.dot_general(qs, kb, NT, preferred_element_type=F32) + bias_s[tab]
                    pr = jnp.exp(s - lse_s)
                    dp = lax.dot_general(dos, vb, NT, preferred_element_type=F32)
                    ds = (pr * (dp - dl_s2)).astype(BF16)
                    dv_c = lax.dot_general(pr.astype(BF16), dos, TN, preferred_element_type=F32)
                    dk_c = lax.dot_general(ds, qs, TN, preferred_element_type=F32)
                    dq_c = _unstack_heads(jnp.dot(ds, kb, preferred_element_type=F32), is0)
                    new.append((dq_c, dk_c, dv_c))
                old = [(dq_ref[rows, :], dk_ref[krows, :], dv_ref[krows, :]) for rows, krows, _ in blocks]
                for (rows, krows, _), (dq_c, dk_c, dv_c), (dq_o, dk_o, dv_o) in zip(blocks, new, old):
                    dq_ref[rows, :] = dq_o + dq_c
                    dk_ref[krows, :] = dk_o + dk_c
                    dv_ref[krows, :] = dv_o + dv_c
                return carry

            lax.fori_loop(0, ngroups, group, 0)

        @pl.when(hp == steps - 1)
        def _():
            for cp in copies:
                cp.wait()

    col = pl.BlockSpec((S, LANES), lambda i: (0, i))
    res = pl.pallas_call(
        body, name="attn_bwd", grid=(steps,),
        in_specs=[pl.BlockSpec(memory_space=pltpu.SMEM)] + [col] * 6 + [ANY_SPEC] * ng,
        out_specs=[col] * 3 + [ANY_SPEC] * ng,
        out_shape=[jax.ShapeDtypeStruct((S, ATT_WIDTH), F32)] * 3 + _exchange_shapes(grads, scatter),
        scratch_shapes=[pltpu.VMEM((S, LANES), F32), pltpu.VMEM((2, 2 * SPAN, 2 * SPAN), F32)] + _exchange_sems(ng),
        compiler_params=_params("arbitrary"),
    )(slopes, q, k, v, o, lse, do, *grads)
    return res[0], res[1], res[2], res[3:]


def _in_bwd(dq, dk, dv, du, w_in, x, g1, dh1):
    S = x.shape[0]

    def body(dq_ref, dk_ref, dv_ref, du_ref, w_ref, x_ref, g_ref, dh1_ref, dz_ref, dx_ref, dg_ref):
        @pl.when(pl.program_id(0) == 0)
        def _():
            dg_ref[...] = jnp.zeros(dg_ref.shape, F32)

        srcs = (dq_ref, dk_ref, dv_ref, du_ref)
        dhn = jnp.zeros((TM, D_MODEL), F32)
        for j in range(N_DEV):
            dz = srcs[j // 2][:, (j % 2) * 256:(j % 2 + 1) * 256]
            if j < 2:
                dz = dz * (HEAD_DIM ** -0.5)
            dz = dz.astype(BF16)
            dz_ref[j] = dz
            dhn = dhn + lax.dot_general(dz, w_ref[j], NT, preferred_element_type=F32)
        g = g_ref[...]
        r, n = _rms(x_ref[...])
        dg_ref[...] += _colsum(dhn * n)
        dx_ref[...] = dh1_ref[...] + _rms_bwd(r, n, g, dhn)

    tok = lambda w: pl.BlockSpec((TM, w), lambda i: (i, 0))
    return pl.pallas_call(
        body, name="in_bwd", grid=(S // TM,),
        in_specs=[tok(512)] * 4 + [_const(w_in.shape), tok(D_MODEL), _const((1, D_MODEL)), tok(D_MODEL)],
        out_specs=[pl.BlockSpec((N_DEV, TM, 256), lambda i: (0, i, 0)), tok(D_MODEL),
                   pl.BlockSpec((1, D_MODEL), lambda i: (0, 0))],
        out_shape=[jax.ShapeDtypeStruct((N_DEV, S, 256), BF16), jax.ShapeDtypeStruct((S, D_MODEL), F32),
                   jax.ShapeDtypeStruct((1, D_MODEL), F32)],
        compiler_params=_params("arbitrary"),
    )(dq, dk, dv, du, w_in, x, g1, dh1)


def _adamw(name, parts, w, m, v):
    R, C = w.shape
    rb = R
    for cand in (256, 128, 64, 32, 16, 8):
        if R % cand == 0 and R > cand:
            rb = cand
            break

    def body(p_ref, w_ref, m_ref, v_ref, g_ref, d_ref, mo_ref, vo_ref):
        g = p_ref[0].astype(F32)
        for s in range(1, N_DEV):
            g = g + p_ref[s].astype(F32)
        g_ref[...] = g
        d_ref[...], mo_ref[...], vo_ref[...] = _adam_update(g, w_ref[...], m_ref[...], v_ref[...])

    blk = pl.BlockSpec((rb, C), lambda i: (i, 0))
    return pl.pallas_call(
        body, name=name, grid=(R // rb,),
        in_specs=[pl.BlockSpec((N_DEV, rb, C), lambda i: (0, i, 0)), blk, blk, blk],
        out_specs=[blk] * 4,
        out_shape=[jax.ShapeDtypeStruct((R, C), F32)] * 4,
        compiler_params=_params("arbitrary"),
    )(parts, w, m, v)


def _adam_update(g, w, m, v):
    m_new = ADAM_B1 * m + (1.0 - ADAM_B1) * g
    v_new = ADAM_B2 * v + (1.0 - ADAM_B2) * (g * g)
    m_hat = m_new / (1.0 - ADAM_B1 ** ADAM_STEP)
    v_hat = v_new / (1.0 - ADAM_B2 ** ADAM_STEP)
    return -ADAM_LR * (m_hat / (jnp.sqrt(v_hat) + ADAM_EPS) + ADAM_WD * w), m_new, v_new


def _adamw_small(parts, loss_parts, ws, ms, vs):
    n = len(ws)

    def body(*refs):
        p_refs, lp_ref = refs[:n], refs[n]
        w_refs, m_refs, v_refs = refs[n + 1:2 * n + 1], refs[2 * n + 1:3 * n + 1], refs[3 * n + 1:4 * n + 1]
        outs = refs[4 * n + 1:]
        for i in range(n):
            g = p_refs[i][0]
            for s in range(1, N_DEV):
                g = g + p_refs[i][s]
            d, m_new, v_new = _adam_update(g, w_refs[i][...], m_refs[i][...], v_refs[i][...])
            outs[i][...] = g
            outs[n + i][...] = d
            outs[2 * n + i][...] = m_new
            outs[3 * n + i][...] = v_new
        tot = lp_ref[0]
        for s in range(1, N_DEV):
            tot = tot + lp_ref[s]
        outs[4 * n][...] = tot

    shapes = [jax.ShapeDtypeStruct(w.shape, F32) for w in ws]
    res = pl.pallas_call(
        body, name="adamw_replicated",
        out_shape=shapes * 4 + [jax.ShapeDtypeStruct(loss_parts.shape[1:], F32)],
        compiler_params=_params(),
    )(*parts, loss_parts, *ws, *ms, *vs)
    return res[:n], res[n:2 * n], res[2 * n:3 * n], res[3 * n:4 * n], res[4 * n]


def _gather2(name, arrays):
    n = len(arrays)

    def body(*refs):
        first, passed, last = _gather2_copies(refs[:n], refs[n:2 * n], *refs[2 * n:])
        for cp in first:
            cp.start()
        for arrival, cp in passed:
            arrival.wait_recv()
            cp.start()
        for wait in last:
            wait()

    return pl.pallas_call(
        body, name=name,
        in_specs=[ANY_SPEC] * n, out_specs=[ANY_SPEC] * n, out_shape=_exchange_shapes(arrays, (False,) * n),
        scratch_shapes=_exchange_sems(n),
    )(*arrays)


def _dw_in_exchange(hn, dz, small):
    S = hn.shape[0]
    nt = S // TK
    ns = len(small)
    kd, nd = hn.shape[1], dz.shape[2]
    me_arr = (4 * lax.axis_index("x") + 2 * lax.axis_index("y") + lax.axis_index("c")).astype(jnp.int32).reshape(1)

    def body(me_ref, x_ref, dy_ref, *rest):
        recv_ref = rest[ns]
        acc, stage, send_sems, recv_sems, own_sem = rest[2 * ns + 1:2 * ns + 6]
        j, t = pl.program_id(0), pl.program_id(1)
        x, y, c = lax.axis_index("x"), lax.axis_index("y"), lax.axis_index("c")
        me = 4 * x + 2 * y + c
        small_copies = _exchange_copies(rest[:ns], rest[ns + 1:2 * ns + 1], (False,) * ns, *rest[2 * ns + 6:])

        @pl.when((j == 0) & (t == 0))
        def _():
            for cp in small_copies:
                cp.start()

        @pl.when(t == 0)
        def _():
            acc[...] = jnp.zeros(acc.shape, F32)

        acc[...] += lax.dot_general(x_ref[...], dy_ref[...], TN, preferred_element_type=F32)

        def to_owner(k, owner):
            return pltpu.make_async_remote_copy(
                src_ref=stage.at[owner], dst_ref=recv_ref.at[me], send_sem=send_sems.at[k], recv_sem=recv_sems.at[k],
                device_id=(owner // 4, (owner // 2) % 2, owner % 2), device_id_type=MESH)

        own = pltpu.make_async_copy(stage.at[me], recv_ref.at[me], own_sem)

        @pl.when(t == nt - 1)
        def _():
            owner = (me + 1 + j) % N_DEV
            stage[owner] = acc[...].astype(BF16)

            @pl.when(j < N_DEV - 1)
            def _():
                to_owner(j, owner).start()

            @pl.when(j == N_DEV - 1)
            def _():
                own.start()
                own.wait()
                for k in range(N_DEV - 1):
                    to_owner(k, me).wait_send()
                    to_owner(k, me).wait_recv()
                for cp in small_copies:
                    cp.wait()

    slab = lambda j, me_ref: (me_ref[0] + 1 + j) % N_DEV
    grid_spec = pltpu.PrefetchScalarGridSpec(
        num_scalar_prefetch=1, grid=(N_DEV, nt),
        in_specs=[pl.BlockSpec((TK, kd), lambda j, t, me_ref: (t, 0)),
                  pl.BlockSpec((None, TK, nd), lambda j, t, me_ref: (slab(j, me_ref), t, 0))] + [ANY_SPEC] * ns,
        out_specs=[ANY_SPEC] * (ns + 1),
        scratch_shapes=[pltpu.VMEM((kd, nd), F32), pltpu.VMEM((N_DEV, kd, nd), BF16),
                        pltpu.SemaphoreType.DMA((N_DEV - 1,)), pltpu.SemaphoreType.DMA((N_DEV - 1,)),
                        pltpu.SemaphoreType.DMA] + _exchange_sems(ns))
    res = pl.pallas_call(
        body, name="dw_in_exchange", grid_spec=grid_spec,
        out_shape=[jax.ShapeDtypeStruct((N_DEV, kd, nd), BF16)] + _exchange_shapes(small, (False,) * ns),
        compiler_params=_params("arbitrary", "arbitrary"),
    )(me_arr, hn, dz, *small)
    return res[0], res[1:]


def kernel(x, p, ln_mix, w_in, pool_w, pool_scale, w_out, ln_ffn, w_up, conv_w, conv_b, w_down, ln_ple, w_ple_gate, w_ple, ln_final, loss_target, m_ln_mix, m_w_in, m_pool_w, m_pool_scale, m_w_out, m_ln_ffn, m_w_up, m_conv_w, m_conv_b, m_w_down, m_ln_ple, m_w_ple_gate, m_w_ple, m_ln_final, v_ln_mix, v_w_in, v_pool_w, v_pool_scale, v_w_out, v_ln_ffn, v_w_up, v_conv_w, v_conv_b, v_w_down, v_ln_ple, v_w_ple_gate, v_w_ple, v_ln_final):
    xs, ps, tgt, pool_w0 = x[0], p[0, 0], loss_target[0], pool_w[0]
    slopes = jnp.exp2(-8.0 * (jnp.arange(N_HEADS, dtype=F32) + 1.0) / N_HEADS)
    conv_b_s = conv_b.reshape(N_DEV, 1, FF_SHARD)

    (w_in_g,) = _gather2("gather_w_in", [w_in[0].astype(BF16)])
    (q, k, v, u, hn1), (w_out_g,) = _qkvu(xs, ln_mix, w_in_g, [w_out[0].astype(BF16)])
    att, lse, (w_up_g, conv_w_g) = _attn_fwd(slopes, q, k, v, [w_up[0].astype(BF16), conv_w[0]])
    w_out_f = w_out_g.reshape(D_MODEL, D_MODEL)
    h1, mix, dlt, (w_down_g,) = _mix_out(xs, att, u, pool_w0, pool_scale, w_out_f, [w_down[0].astype(BF16)])
    w_down_f = w_down_g.reshape(4, FF_SHARD, D_MODEL)
    h2, hn2, up, upc, (w_pg_g, w_ple_g) = _ffn_fwd(h1, ln_ffn, w_up_g, conv_w_g, conv_b_s, w_down_f,
                                                   [w_ple_gate[0].astype(BF16), w_ple[0].astype(BF16)])
    w_pg_f = w_pg_g.reshape(D_MODEL, D_MODEL)
    w_ple_f = jnp.transpose(w_ple_g, (1, 0, 2)).reshape(PLE_DIM, D_MODEL)
    loss_blk, dh2, dh2b, d_w_pg, d_w_ple, d_ln_ple, d_ln_final = _head(
        h2, ps, ln_ple, w_pg_f, w_ple_f, ln_final.reshape(1, D_MODEL), tgt)

    d_w_pg = d_w_pg.reshape(N_DEV, D_MODEL // N_DEV, D_MODEL)
    d_w_ple = jnp.transpose(d_w_ple.reshape(PLE_DIM, N_DEV, LANES), (1, 0, 2))
    d_w_down, dup, d_conv_w, d_conv_b, (r_w_pg, r_w_ple) = _ffn_bwd_a(dh2b, up, upc, w_down_f, [d_w_pg, d_w_ple])
    d_w_down = d_w_down.reshape(N_DEV, D_FF // N_DEV, D_MODEL)
    dpre, dh1, dh1b, d_ln_ffn, (r_conv_w, r_w_down) = _ffn_bwd_b(
        dup, conv_w_g, w_up_g, h1, ln_ffn, dh2, [d_conv_w, d_w_down])
    datt, du, d_pool_w, d_pool_scale = _mix_bwd(dh1b, w_out_f, dlt, pool_w0, pool_scale)
    d_w_out = _wgrad("dw_out", mix, dh1b, "full", "full", 1, D_MODEL, D_MODEL, tk=TM * 2)
    d_w_out = d_w_out.reshape(N_DEV, D_MODEL // N_DEV, D_MODEL)
    d_w_up = _wgrad("dw_up", dpre, hn2, "lead", "full", N_DEV, FF_SHARD, D_MODEL)
    rep_late = [d_pool_w, d_pool_scale, d_ln_ffn, d_conv_b.reshape(1, 2 * D_FF), d_ln_ple, d_ln_final, loss_blk]
    dq, dk, dv, received = _attn_bwd(slopes, q, k, v, att, lse, datt, [d_w_out, d_w_up] + rep_late,
                                     (True, True) + (False,) * len(rep_late))
    r_w_out, r_w_up, r_rep = received[0], received[1], list(received[2:])
    dz, grad_x, d_ln_mix = _in_bwd(dq, dk, dv, du, w_in_g, xs, ln_mix, dh1)

    rep_names = ("ln_mix", "pool_w", "pool_scale", "ln_ffn", "conv_b", "ln_ple", "ln_final")
    rep_w = [ln_mix, pool_w0, pool_scale, ln_ffn, conv_b, ln_ple, ln_final.reshape(1, D_MODEL)]
    rep_m = [m_ln_mix, m_pool_w[0], m_pool_scale, m_ln_ffn, m_conv_b, m_ln_ple, m_ln_final.reshape(1, D_MODEL)]
    rep_v = [v_ln_mix, v_pool_w[0], v_pool_scale, v_ln_ffn, v_conv_b, v_ln_ple, v_ln_final.reshape(1, D_MODEL)]
    r_w_in, (r_ln_mix,) = _dw_in_exchange(hn1, dz, [d_ln_mix])
    small = _adamw_small([r_ln_mix] + r_rep[:-1], r_rep[-1], rep_w, rep_m, rep_v)
    loss = small[4][0, 0]

    sharded = {}
    sharded["w_in"] = _adamw("adamw_w_in", r_w_in, w_in[0], m_w_in[0], v_w_in[0])
    sharded["w_out"] = _adamw("adamw_w_out", r_w_out, w_out[0], m_w_out[0], v_w_out[0])
    sharded["w_up"] = [t.T for t in _adamw("adamw_w_up", r_w_up, w_up[0].T, m_w_up[0].T, v_w_up[0].T)]
    sharded["conv_w"] = _adamw("adamw_conv_w", r_conv_w, conv_w[0], m_conv_w[0], v_conv_w[0])
    sharded["w_down"] = _adamw("adamw_w_down", r_w_down, w_down[0], m_w_down[0], v_w_down[0])
    sharded["w_ple_gate"] = _adamw("adamw_w_ple_gate", r_w_pg, w_ple_gate[0], m_w_ple_gate[0], v_w_ple_gate[0])
    sharded["w_ple"] = _adamw("adamw_w_ple", r_w_ple, w_ple[0], m_w_ple[0], v_w_ple[0])

    shapes = dict(w_in=w_in, w_out=w_out, w_up=w_up, conv_w=conv_w, w_down=w_down, w_ple_gate=w_ple_gate, w_ple=w_ple,
                  ln_mix=ln_mix, pool_w=pool_w, pool_scale=pool_scale, ln_ffn=ln_ffn, conv_b=conv_b, ln_ple=ln_ple,
                  ln_final=ln_final)

    def leaf(kind, n):
        src = sharded[n][kind] if n in sharded else small[kind][rep_names.index(n)]
        return src.reshape(shapes[n].shape)

    order = ("ln_mix", "w_in", "pool_w", "pool_scale", "w_out", "ln_ffn", "w_up", "conv_w", "conv_b", "w_down", "ln_ple",
             "w_ple_gate", "w_ple", "ln_final")
    outs = [loss, grad_x[None]]
    for kind in range(4):
        outs += [leaf(kind, n) for n in order]
    return tuple(outs)
```

```python
import jax
import jax.numpy as jnp
from jax import lax
from jax.experimental import pallas as pl
from jax.experimental.pallas import tpu as pltpu

F32 = jnp.float32
BF16 = jnp.bfloat16

N_DEV = 8
D_MODEL = 1024
ATT_WIDTH = 512
POOL_WIDTH = 512
N_HEADS = 8
HEAD_DIM = 64
SPAN = 128
DILATIONS = (1, 4, 16)
POOL_WINDOWS = (2, 4, 8, 16)
POOL_GROUP = 128
D_FF = 2816
FF_SHARD = 2 * D_FF // N_DEV
PLE_DIM = 256
EPS = 1e-6
NEG = -1e30

ADAM_LR = 0.001
ADAM_B1 = 0.9
ADAM_B2 = 0.999
ADAM_EPS = 1e-08
ADAM_WD = 0.01
ADAM_STEP = 10

LANES = 128
HALO = 16
TM = 512
TM_FF = 256
TK = 4096
ATTN_GROUP_FWD = 16
ATTN_GROUP_BWD = 8
VMEM_LIMIT = 56 * 1024 * 1024

MESH = pl.DeviceIdType.MESH
NT = (((1,), (1,)), ((), ()))
TN = (((0,), (0,)), ((), ()))


def _params(*sem):
    return pltpu.CompilerParams(dimension_semantics=sem or None, vmem_limit_bytes=VMEM_LIMIT)


def _const(shape):
    n = len(shape)
    return pl.BlockSpec(shape, lambda *_: (0,) * n, pipeline_mode=pl.Buffered(1))


def _rms(h):
    r = lax.rsqrt(jnp.mean(h * h, axis=-1, keepdims=True) + EPS)
    return r, h * r


def _rms_bwd(r, n, g, dhn):
    dn = dhn * g
    return r * (dn - n * jnp.mean(dn * n, axis=-1, keepdims=True))


def _colsum(a):
    return jnp.sum(a, axis=0, keepdims=True)


def _gather2_copies(ins, outs, send_sems, recv_sems, local_sems):
    n = len(ins)
    x, y, c = lax.axis_index("x"), lax.axis_index("y"), lax.axis_index("c")
    slot = lambda px, py, pc: 4 * px + 2 * py + pc
    chips = [(x, 1 - y), (1 - x, y), (1 - x, 1 - y)]
    first, passed, last = [], [], []

    def remote(a, r, src, dst_slot, to):
        return pltpu.make_async_remote_copy(
            src_ref=src, dst_ref=outs[a].at[dst_slot],
            send_sem=send_sems.at[a * (N_DEV - 1) + r], recv_sem=recv_sems.at[a * (N_DEV - 1) + r],
            device_id=to, device_id_type=MESH)

    for a in range(n):
        mine = pltpu.make_async_copy(ins[a], outs[a].at[slot(x, y, c)], local_sems.at[a])
        to_sibling = remote(a, 0, ins[a], slot(x, y, c), (x, y, 1 - c))
        first += [mine, to_sibling]
        last += [mine.wait, to_sibling.wait_send, to_sibling.wait_recv]
        for r, (px, py) in enumerate(chips, start=1):
            to_chip = remote(a, r, ins[a], slot(x, y, c), (px, py, c))
            onward = remote(a, 3 + r, outs[a].at[slot(px, py, c)], slot(px, py, c), (x, y, 1 - c))
            first.append(to_chip)
            passed.append((to_chip, onward))
            last += [to_chip.wait_send, onward.wait_send, onward.wait_recv]
    return first, passed, last


def _gather2_begin(plan, step, pass_step):
    first, passed, _ = plan

    @pl.when(step == 0)
    def _():
        for cp in first:
            cp.start()

    @pl.when(step == pass_step)
    def _():
        for arrival, cp in passed:
            arrival.wait_recv()
            cp.start()


def _gather2_end(plan, step, nsteps):
    @pl.when(step == nsteps - 1)
    def _():
        for wait in plan[2]:
            wait()


ANY_SPEC = pl.BlockSpec(memory_space=pl.ANY)


def _exchange_shapes(arrays, scatter):
    out = []
    for a, s in zip(arrays, scatter):
        slab = a.shape[1:] if s else a.shape
        out.append(jax.ShapeDtypeStruct((N_DEV,) + tuple(slab), a.dtype))
    return out


def _exchange_sems(n):
    return [pltpu.SemaphoreType.DMA((n * (N_DEV - 1),)), pltpu.SemaphoreType.DMA((n * (N_DEV - 1),)),
            pltpu.SemaphoreType.DMA((n,))]


def _exchange_copies(ins, outs, scatter, send_sems, recv_sems, local_sems):
    n = len(ins)
    x, y, c = lax.axis_index("x"), lax.axis_index("y"), lax.axis_index("c")
    me = 4 * x + 2 * y + c
    copies = []
    for a in range(n):
        src = ins[a].at[me] if scatter[a] else ins[a]
        copies.append(pltpu.make_async_copy(src, outs[a].at[me], local_sems.at[a]))
    for k in range(1, N_DEV):
        px = 1 - x if k & 4 else x
        py = 1 - y if k & 2 else y
        pc = 1 - c if k & 1 else c
        pid = 4 * px + 2 * py + pc
        for a in range(n):
            src = ins[a].at[pid] if scatter[a] else ins[a]
            copies.append(pltpu.make_async_remote_copy(
                src_ref=src, dst_ref=outs[a].at[me],
                send_sem=send_sems.at[a * (N_DEV - 1) + k - 1], recv_sem=recv_sems.at[a * (N_DEV - 1) + k - 1],
                device_id=(px, py, pc), device_id_type=MESH))
    return copies


def _qkvu(x, g1, w_in, shards):
    S = x.shape[0]
    ns = len(shards)
    nsteps = S // TM

    def body(x_ref, g_ref, w_ref, *rest):
        q_ref, k_ref, v_ref, u_ref, hn_ref = rest[ns:ns + 5]
        plan = _gather2_copies(rest[:ns], rest[ns + 5:2 * ns + 5], *rest[2 * ns + 5:])
        _gather2_begin(plan, pl.program_id(0), nsteps - 2)
        r, n = _rms(x_ref[...])
        hn = (n * g_ref[...]).astype(BF16)
        hn_ref[...] = hn
        outs = (q_ref, k_ref, v_ref, u_ref)
        for j in range(N_DEV):
            z = jnp.dot(hn, w_ref[j], preferred_element_type=F32)
            if j < 2:
                z = z * (HEAD_DIM ** -0.5)
            outs[j // 2][:, (j % 2) * 256:(j % 2 + 1) * 256] = z
        _gather2_end(plan, pl.program_id(0), nsteps)

    tok = lambda w: pl.BlockSpec((TM, w), lambda i: (i, 0))
    res = pl.pallas_call(
        body, name="qkvu", grid=(nsteps,),
        in_specs=[tok(D_MODEL), _const((1, D_MODEL)), _const(w_in.shape)] + [ANY_SPEC] * ns,
        out_specs=[tok(512)] * 4 + [tok(D_MODEL)] + [ANY_SPEC] * ns,
        out_shape=[jax.ShapeDtypeStruct((S, 512), F32)] * 4 + [jax.ShapeDtypeStruct((S, D_MODEL), BF16)]
        + _exchange_shapes(shards, (False,) * ns),
        scratch_shapes=_exchange_sems(ns),
        compiler_params=_params("arbitrary"),
    )(x, g1, w_in, *shards)
    return res[:5], res[5:]


def _attn_fill_bias(bias_s, slope_ref, hp, d):
    qi = lax.broadcasted_iota(jnp.int32, (SPAN, 2 * SPAN), 0)
    kj = lax.broadcasted_iota(jnp.int32, (SPAN, 2 * SPAN), 1)
    for t, diff in enumerate((qi + SPAN - kj, qi - kj)):
        valid = (diff >= 0) & (diff <= SPAN)
        dist = diff.astype(F32) * float(d)
        for h in range(2):
            bias_s[t, h * SPAN:(h + 1) * SPAN, :] = jnp.where(valid, -slope_ref[2 * hp + h] * dist, NEG)


def _stack_heads(x, is0):
    return jnp.concatenate([jnp.where(is0, x, 0.0), jnp.where(is0, 0.0, x)], axis=0)


def _unstack_heads(y, is0):
    return jnp.where(is0, y[0:SPAN], y[SPAN:2 * SPAN])


def _attn_block(i, g, d, nb, group):
    gr = min(d, group)
    gn = group // gr
    per = d // gr
    r = (i & (per - 1)) * gr + g % gr
    n = (i >> (per.bit_length() - 1)) + (g // gr) * (nb // gn)
    k0 = jnp.maximum(n - 1, 0)

    def ds(block, nrows):
        start = block * (SPAN * d) + r
        return pl.ds(start, nrows, stride=d) if d > 1 else pl.ds(start, nrows)

    return ds(n, SPAN), ds(k0, 2 * SPAN), jnp.where(n == 0, 1, 0)


def _attn_groups(S, d, group):
    nb = S // d // SPAN
    gn = group // min(d, group)
    assert nb >= 2 and nb % gn == 0 and (gn == 1 or nb // gn >= 2)
    return nb, d * nb // group


def _attn_fwd(slopes, q, k, v, shards):
    S = q.shape[0]
    ns = len(shards)
    steps = ATT_WIDTH // LANES

    def body(slope_ref, q_ref, k_ref, v_ref, *rest):
        o_ref, lse_ref = rest[ns:ns + 2]
        m_s, l_s, bias_s = rest[2 * ns + 2:2 * ns + 5]
        hp = pl.program_id(0)
        plan = _gather2_copies(rest[:ns], rest[ns + 2:2 * ns + 2], *rest[2 * ns + 5:])
        _gather2_begin(plan, hp, steps - 1)

        is0 = lax.broadcasted_iota(jnp.int32, (SPAN, LANES), 1) < HEAD_DIM
        for pi, d in enumerate(DILATIONS):
            nb, ngroups = _attn_groups(S, d, ATTN_GROUP_FWD)
            _attn_fill_bias(bias_s, slope_ref, hp, d)

            def group(i, carry, d=d, pi=pi, nb=nb):
                blocks = [_attn_block(i, g, d, nb, ATTN_GROUP_FWD) for g in range(ATTN_GROUP_FWD)]
                loaded = [(q_ref[rows, :], k_ref[krows, :].astype(BF16), v_ref[krows, :].astype(BF16))
                          for rows, krows, _ in blocks]
                new = []
                for (rows, krows, tab), (qb, kb, vb) in zip(blocks, loaded):
                    qs = _stack_heads(qb, is0).astype(BF16)
                    s = lax.dot_general(qs, kb, NT, preferred_element_type=F32) + bias_s[tab]
                    m = jnp.max(s, axis=-1, keepdims=True)
                    e = jnp.exp(s - m)
                    l = jnp.sum(e, axis=-1, keepdims=True)
                    pv = jnp.dot(e.astype(BF16), vb, preferred_element_type=F32)
                    new.append([_unstack_heads(jnp.broadcast_to(m, pv.shape), is0),
                                _unstack_heads(jnp.broadcast_to(l, pv.shape), is0), _unstack_heads(pv, is0)])
                if pi > 0:
                    old = [(m_s[rows, :], l_s[rows, :], o_ref[rows, :]) for rows, _, _ in blocks]
                    for st, (m_o, l_o, o_o) in zip(new, old):
                        m_n = jnp.maximum(m_o, st[0])
                        a_o = jnp.exp(m_o - m_n)
                        a_b = jnp.exp(st[0] - m_n)
                        st[:] = [m_n, a_o * l_o + a_b * st[1], a_o * o_o + a_b * st[2]]
                for (rows, _, _), (m_b, l_b, acc) in zip(blocks, new):
                    if pi == len(DILATIONS) - 1:
                        o_ref[rows, :] = acc / l_b
                        lse_ref[rows, :] = m_b + jnp.log(l_b)
                    else:
                        o_ref[rows, :] = acc
                        m_s[rows, :] = m_b
                        l_s[rows, :] = l_b
                return carry

            lax.fori_loop(0, ngroups, group, 0)

        _gather2_end(plan, hp, steps)

    col = pl.BlockSpec((S, LANES), lambda i: (0, i))
    res = pl.pallas_call(
        body, name="attn_fwd", grid=(steps,),
        in_specs=[pl.BlockSpec(memory_space=pltpu.SMEM), col, col, col] + [ANY_SPEC] * ns,
        out_specs=[col, col] + [ANY_SPEC] * ns,
        out_shape=[jax.ShapeDtypeStruct((S, ATT_WIDTH), F32)] * 2 + _exchange_shapes(shards, (False,) * ns),
        scratch_shapes=[pltpu.VMEM((S, LANES), F32), pltpu.VMEM((S, LANES), F32),
                        pltpu.VMEM((2, 2 * SPAN, 2 * SPAN), F32)] + _exchange_sems(ns),
        compiler_params=_params("arbitrary"),
    )(slopes, q, k, v, *shards)
    return res[0], res[1], res[2:]


def _pool_count(i, w):
    t = i * TM + lax.broadcasted_iota(jnp.int32, (TM, 1), 0)
    return jnp.minimum(t + 1, w).astype(F32)


def _mix_out(x, att, u, pool_w, pool_scale, w_out, shards):
    S = x.shape[0]
    ns = len(shards)
    nsteps = S // TM

    def body(x_ref, att_ref, u_ref, pw_ref, ps_ref, w_ref, *rest):
        h1_ref, mix_ref, dlt_ref = rest[ns:ns + 3]
        ubuf = rest[2 * ns + 3]
        i = pl.program_id(0)
        plan = _gather2_copies(rest[:ns], rest[ns + 3:2 * ns + 3], *rest[2 * ns + 4:])
        _gather2_begin(plan, i, nsteps - 1)

        @pl.when(i == 0)
        def _():
            ubuf[0:HALO, :] = jnp.zeros((HALO, POOL_WIDTH), F32)

        ubuf[HALO:HALO + TM, :] = u_ref[...]
        mix_ref[:, 0:ATT_WIDTH] = att_ref[...].astype(BF16)
        for g, w in enumerate(POOL_WINDOWS):
            cols = slice(g * POOL_GROUP, (g + 1) * POOL_GROUP)
            ug = ubuf[HALO:HALO + TM, cols]
            acc = ug
            for j in range(1, w):
                acc = acc + ubuf[HALO - j:HALO - j + TM, cols]
            dlt = (acc / _pool_count(i, w) - ug).astype(BF16)
            dlt_ref[:, cols] = dlt
            yg = jnp.dot(dlt, pw_ref[g].astype(BF16), preferred_element_type=F32) * ps_ref[:, cols]
            mix_ref[:, ATT_WIDTH + g * POOL_GROUP:ATT_WIDTH + (g + 1) * POOL_GROUP] = yg.astype(BF16)
        ubuf[0:HALO, :] = ubuf[TM:TM + HALO, :]
        h1_ref[...] = x_ref[...] + jnp.dot(mix_ref[...], w_ref[...], preferred_element_type=F32)
        _gather2_end(plan, i, nsteps)

    tok = lambda w: pl.BlockSpec((TM, w), lambda i: (i, 0))
    res = pl.pallas_call(
        body, name="mix_out", grid=(nsteps,),
        in_specs=[tok(D_MODEL), tok(ATT_WIDTH), tok(POOL_WIDTH), _const(pool_w.shape), _const((1, POOL_WIDTH)),
                  _const(w_out.shape)] + [ANY_SPEC] * ns,
        out_specs=[tok(D_MODEL), tok(D_MODEL), tok(POOL_WIDTH)] + [ANY_SPEC] * ns,
        out_shape=[jax.ShapeDtypeStruct((S, D_MODEL), F32), jax.ShapeDtypeStruct((S, D_MODEL), BF16),
                   jax.ShapeDtypeStruct((S, POOL_WIDTH), BF16)] + _exchange_shapes(shards, (False,) * ns),
        scratch_shapes=[pltpu.VMEM((TM + HALO, POOL_WIDTH), F32)] + _exchange_sems(ns),
        compiler_params=_params("arbitrary"),
    )(x, att, u, pool_w, pool_scale, w_out, *shards)
    return res[0], res[1], res[2], res[3:]


def _conv_fwd(stage, upre, prev, cw, cb):
    T = upre.shape[0]
    stage[0:HALO, :] = prev
    stage[HALO:HALO + T, :] = upre
    return cb + cw[0:1, :] * stage[HALO - 2:HALO - 2 + T, :] + cw[1:2, :] * stage[HALO - 1:HALO - 1 + T, :] + cw[2:3, :] * upre


def _ffn_fwd(h1, g2, w_up, conv_w, conv_b, w_down, shards):
    S = h1.shape[0]
    T = TM_FF
    ns = len(shards)
    nsteps = S // T

    def body(h1_ref, g_ref, wu_ref, cw_ref, cb_ref, wd_ref, *rest):
        h2_ref, hn_ref, up_ref, upc_ref = rest[ns:ns + 4]
        carry, stage = rest[2 * ns + 4:2 * ns + 6]
        i = pl.program_id(0)
        plan = _gather2_copies(rest[:ns], rest[ns + 4:2 * ns + 4], *rest[2 * ns + 6:])
        _gather2_begin(plan, i, nsteps // 2)

        @pl.when(i == 0)
        def _():
            carry[...] = jnp.zeros(carry.shape, F32)

        h1t = h1_ref[...]
        r, n = _rms(h1t)
        hn = (n * g_ref[...]).astype(BF16)
        hn_ref[...] = hn
        acc = h1t
        for j in range(4):
            conv = []
            for jj in (j, j + 4):
                upre = jnp.dot(hn, wu_ref[jj], preferred_element_type=F32)
                up_ref[jj] = upre.astype(BF16)
                conv.append(_conv_fwd(stage, upre, carry[jj], cw_ref[jj], cb_ref[jj]))
                upc_ref[jj] = conv[-1].astype(BF16)
                carry[jj] = stage[T:T + HALO, :]
            gate, val = conv
            a = gate * jax.nn.sigmoid(gate) * val
            acc = acc + jnp.dot(a.astype(BF16), wd_ref[j], preferred_element_type=F32)
        h2_ref[...] = acc
        _gather2_end(plan, i, nsteps)

    tok = lambda w: pl.BlockSpec((T, w), lambda i: (i, 0))
    res = pl.pallas_call(
        body, name="ffn_fwd", grid=(nsteps,),
        in_specs=[tok(D_MODEL), _const((1, D_MODEL)), _const(w_up.shape), _const(conv_w.shape), _const(conv_b.shape),
                  _const(w_down.shape)] + [ANY_SPEC] * ns,
        out_specs=[tok(D_MODEL), tok(D_MODEL)] + [pl.BlockSpec((N_DEV, T, FF_SHARD), lambda i: (0, i, 0))] * 2
        + [ANY_SPEC] * ns,
        out_shape=[jax.ShapeDtypeStruct((S, D_MODEL), F32), jax.ShapeDtypeStruct((S, D_MODEL), BF16)]
        + [jax.ShapeDtypeStruct((N_DEV, S, FF_SHARD), BF16)] * 2 + _exchange_shapes(shards, (False,) * ns),
        scratch_shapes=[pltpu.VMEM((N_DEV, HALO, FF_SHARD), F32), pltpu.VMEM((T + HALO, FF_SHARD), F32)]
        + _exchange_sems(ns),
        compiler_params=_params("arbitrary"),
    )(h1, g2, w_up, conv_w, conv_b, w_down, *shards)
    return res[0], res[1], res[2], res[3], res[4:]


def _head(h2, p, g3, w_pg, w_ple, g4, target):
    S = h2.shape[0]
    nt = S // TM

    def body(h2_ref, p_ref, g3_ref, wpg_ref, wple_ref, g4_ref, t_ref,
             loss_ref, dh2_ref, dh2b_ref, dwpg_ref, dwple_ref, dg3_ref, dg4_ref, lacc, pg_acc, ple_acc):
        i = pl.program_id(0)

        @pl.when(i == 0)
        def _():
            lacc[...] = jnp.zeros(lacc.shape, F32)
            pg_acc[...] = jnp.zeros(pg_acc.shape, F32)
            ple_acc[...] = jnp.zeros(ple_acc.shape, F32)
            dg3_ref[...] = jnp.zeros(dg3_ref.shape, F32)
            dg4_ref[...] = jnp.zeros(dg4_ref.shape, F32)

        h2t = h2_ref[...]
        g3, g4 = g3_ref[...], g4_ref[...]
        r3, n3 = _rms(h2t)
        hn3 = (n3 * g3).astype(BF16)
        pb = p_ref[...].astype(BF16)
        gs = jax.nn.sigmoid(jnp.dot(hn3, wpg_ref[...], preferred_element_type=F32))
        pe = jnp.dot(pb, wple_ref[...], preferred_element_type=F32)
        h3 = h2t + gs * pe
        r4, n4 = _rms(h3)
        err = n4 * g4 - t_ref[...]
        lacc[...] += _colsum(err * err)
        dy = err * (1.0 / D_MODEL)
        dg4_ref[...] += _colsum(dy * n4)
        dh3 = _rms_bwd(r4, n4, g4, dy)
        dpe = (dh3 * gs).astype(BF16)
        dgl = (dh3 * pe * gs * (1.0 - gs)).astype(BF16)
        ple_acc[...] += lax.dot_general(pb, dpe, TN, preferred_element_type=F32)
        pg_acc[...] += lax.dot_general(hn3, dgl, TN, preferred_element_type=F32)
        dhn3 = lax.dot_general(dgl, wpg_ref[...], NT, preferred_element_type=F32)
        dg3_ref[...] += _colsum(dhn3 * n3)
        dh2 = dh3 + _rms_bwd(r3, n3, g3, dhn3)
        dh2_ref[...] = dh2
        dh2b_ref[...] = dh2.astype(BF16)

        @pl.when(i == nt - 1)
        def _():
            tot = 0.5 / D_MODEL * jnp.sum(lacc[...], axis=-1, keepdims=True)
            loss_ref[...] = jnp.broadcast_to(tot, loss_ref.shape)
            dwpg_ref[...] = pg_acc[...].astype(BF16)
            dwple_ref[...] = ple_acc[...].astype(BF16)

    tok = lambda w: pl.BlockSpec((TM, w), lambda i: (i, 0))
    row = pl.BlockSpec((1, D_MODEL), lambda i: (0, 0))
    act = lambda dt: jax.ShapeDtypeStruct((S, D_MODEL), dt)
    whole = lambda r: pl.BlockSpec((r, D_MODEL), lambda i: (0, 0))
    return pl.pallas_call(
        body, name="head", grid=(nt,),
        in_specs=[tok(D_MODEL), tok(PLE_DIM), _const((1, D_MODEL)), _const(w_pg.shape), _const(w_ple.shape),
                  _const((1, D_MODEL)), tok(D_MODEL)],
        out_specs=[pl.BlockSpec((8, LANES), lambda i: (0, 0)), tok(D_MODEL), tok(D_MODEL), whole(D_MODEL),
                   whole(PLE_DIM), row, row],
        out_shape=[jax.ShapeDtypeStruct((8, LANES), F32), act(F32), act(BF16),
                   jax.ShapeDtypeStruct((D_MODEL, D_MODEL), BF16), jax.ShapeDtypeStruct((PLE_DIM, D_MODEL), BF16),
                   jax.ShapeDtypeStruct((1, D_MODEL), F32), jax.ShapeDtypeStruct((1, D_MODEL), F32)],
        scratch_shapes=[pltpu.VMEM((1, D_MODEL), F32), pltpu.VMEM((D_MODEL, D_MODEL), F32),
                        pltpu.VMEM((PLE_DIM, D_MODEL), F32)],
        compiler_params=_params("arbitrary"),
    )(h2, p, g3, w_pg, w_ple, g4, target)


def _wgrad(name, x, dy, x_kind, dy_kind, nj, k_dim, n_dim, tk=TK):
    S = x.shape[-2]
    nt = S // tk

    def spec(kind, width):
        if kind == "full":
            return pl.BlockSpec((tk, width), lambda j, t: (t, 0))
        return pl.BlockSpec((None, tk, width), lambda j, t: (j, t, 0))

    def body(x_ref, dy_ref, o_ref, acc):
        t = pl.program_id(1)

        @pl.when(t == 0)
        def _():
            acc[...] = jnp.zeros(acc.shape, F32)

        acc[...] += lax.dot_general(x_ref[...].astype(BF16), dy_ref[...], TN, preferred_element_type=F32)

        @pl.when(t == nt - 1)
        def _():
            o_ref[...] = acc[...].astype(BF16)

    return pl.pallas_call(
        body, name=name, grid=(nj, nt),
        in_specs=[spec(x_kind, k_dim), spec(dy_kind, n_dim)],
        out_specs=pl.BlockSpec((None, k_dim, n_dim), lambda j, t: (j, 0, 0)),
        out_shape=jax.ShapeDtypeStruct((nj, k_dim, n_dim), BF16),
        scratch_shapes=[pltpu.VMEM((k_dim, n_dim), F32)],
        compiler_params=_params("arbitrary", "arbitrary"),
    )(x, dy)


def _row_picker(T, off0, off1):
    r = lax.broadcasted_iota(jnp.int32, (2 * T, T + HALO), 0)
    c = lax.broadcasted_iota(jnp.int32, (2 * T, T + HALO), 1)
    want = jnp.where(r < T, r + off0, r - T + off1)
    return jnp.where(c == want, 1.0, 0.0).astype(BF16)


def _ffn_bwd_a(dh2b, up, upc, w_down, grads):
    S = dh2b.shape[0]
    T = TM_FF
    hb = T // HALO
    nsteps = S // T
    ng = len(grads)

    def body(dh_ref, up_ref, halo_ref, upc_ref, wd_ref, *rest):
        dwd_ref, dup_ref, dcw_ref, dcb_ref = rest[ng:ng + 4]
        stage, dwd_acc = rest[2 * ng + 4:2 * ng + 6]
        i = pl.program_id(0)
        copies = _exchange_copies(rest[:ng], rest[ng + 4:2 * ng + 4], (True,) * ng, *rest[2 * ng + 6:])

        @pl.when(i == 0)
        def _():
            dcw_ref[...] = jnp.zeros(dcw_ref.shape, F32)
            dcb_ref[...] = jnp.zeros(dcb_ref.shape, F32)
            dwd_acc[...] = jnp.zeros(dwd_acc.shape, F32)
            for cp in copies:
                cp.start()

        @pl.when(i == nsteps - 1)
        def _():
            for cp in copies:
                cp.wait()

        dh = dh_ref[...]
        pick = _row_picker(T, HALO - 2, HALO - 1)
        for j in range(4):
            da = lax.dot_general(dh, wd_ref[j], NT, preferred_element_type=F32)
            taps = []
            for jj in (j, j + 4):
                upre = up_ref[jj]
                stage[0:HALO, :] = jnp.where(i > 0, halo_ref[jj], jnp.zeros((HALO, FF_SHARD), BF16))
                stage[HALO:HALO + T, :] = upre
                prv = jnp.dot(pick, stage[...], preferred_element_type=F32)
                taps.append((prv[0:T], prv[T:2 * T], upre.astype(F32)))
            gate, val = upc_ref[j].astype(F32), upc_ref[j + 4].astype(F32)
            sg = jax.nn.sigmoid(gate)
            silu = gate * sg
            dwd_acc[j] += lax.dot_general((silu * val).astype(BF16), dh, TN, preferred_element_type=F32)
            dgate = (da * val) * (sg + silu * (1.0 - sg))
            dval = da * silu
            for jj, dup, tp in ((j, dgate, taps[0]), (j + 4, dval, taps[1])):
                dup_ref[jj] = dup.astype(BF16)
                dcb_ref[jj] += _colsum(dup)
                for kk in range(3):
                    dcw_ref[jj, kk:kk + 1, :] += _colsum(dup * tp[kk])

        @pl.when(i == nsteps - 1)
        def _():
            dwd_ref[...] = dwd_acc[...].astype(BF16)

    tok = lambda w: pl.BlockSpec((T, w), lambda i: (i, 0))
    shard = pl.BlockSpec((N_DEV, T, FF_SHARD), lambda i: (0, i, 0))
    res = pl.pallas_call(
        body, name="ffn_bwd_a", grid=(nsteps,),
        in_specs=[tok(D_MODEL), shard,
                  pl.BlockSpec((N_DEV, HALO, FF_SHARD), lambda i: (0, jnp.maximum(i * hb - 1, 0), 0)),
                  shard, _const(w_down.shape)] + [ANY_SPEC] * ng,
        out_specs=[_const(w_down.shape), shard,
                   pl.BlockSpec((N_DEV, 3, FF_SHARD), lambda i: (0, 0, 0)),
                   pl.BlockSpec((N_DEV, 1, FF_SHARD), lambda i: (0, 0, 0))] + [ANY_SPEC] * ng,
        out_shape=[jax.ShapeDtypeStruct(w_down.shape, BF16), jax.ShapeDtypeStruct((N_DEV, S, FF_SHARD), BF16),
                   jax.ShapeDtypeStruct((N_DEV, 3, FF_SHARD), F32), jax.ShapeDtypeStruct((N_DEV, 1, FF_SHARD), F32)]
        + _exchange_shapes(grads, (True,) * ng),
        scratch_shapes=[pltpu.VMEM((T + HALO, FF_SHARD), BF16), pltpu.VMEM(w_down.shape, F32)] + _exchange_sems(ng),
        compiler_params=_params("arbitrary"),
    )(dh2b, up, up, upc, w_down, *grads)
    return res[0], res[1], res[2], res[3], res[4:]


def _ffn_bwd_b(dup, conv_w, w_up, h1, hn, g2, dh2, grads):
    S = h1.shape[0]
    T = TM_FF
    hb = T // HALO
    nt = S // T
    ng = len(grads)

    def body(dup_ref, halo_ref, cw_ref, wu_ref, h1_ref, hn_ref, g_ref, dh2_ref, *rest):
        dwu_ref, dh1_ref, dh1b_ref, dg_ref = rest[ng:ng + 4]
        stage, acc, out_stage, out_sem = rest[2 * ng + 4:2 * ng + 8]
        i = pl.program_id(0)
        copies = _exchange_copies(rest[:ng], rest[ng + 4:2 * ng + 4], (True,) * ng, *rest[2 * ng + 8:])

        @pl.when(i == 0)
        def _():
            dg_ref[...] = jnp.zeros(dg_ref.shape, F32)
            acc[...] = jnp.zeros(acc.shape, F32)
            for cp in copies:
                cp.start()

        hnt = hn_ref[...]
        dhn = jnp.zeros((T, D_MODEL), F32)
        for jj in range(N_DEV):
            dup = dup_ref[jj].astype(F32)
            stage[0:T, :] = dup
            stage[T:T + HALO, :] = jnp.where(i < nt - 1, halo_ref[jj].astype(F32), 0.0)
            cw = cw_ref[jj]
            dpre = (cw[2:3, :] * dup + cw[1:2, :] * stage[1:1 + T, :] + cw[0:1, :] * stage[2:2 + T, :]).astype(BF16)
            acc[jj] += lax.dot_general(dpre, hnt, TN, preferred_element_type=F32)
            dhn = dhn + lax.dot_general(dpre, wu_ref[jj], NT, preferred_element_type=F32)
        g = g_ref[...]
        r, n = _rms(h1_ref[...])
        dg_ref[...] += _colsum(dhn * n)
        dh1 = dh2_ref[...] + _rms_bwd(r, n, g, dhn)
        dh1_ref[...] = dh1
        dh1b_ref[...] = dh1.astype(BF16)

        @pl.when(i == nt - 1)
        def _():
            for jj in range(N_DEV):
                out_stage[...] = acc[jj].astype(BF16)
                out = pltpu.make_async_copy(out_stage, dwu_ref.at[jj], out_sem)
                out.start()
                out.wait()
            for cp in copies:
                cp.wait()

    tok = lambda w: pl.BlockSpec((T, w), lambda i: (i, 0))
    shard = pl.BlockSpec((N_DEV, T, FF_SHARD), lambda i: (0, i, 0))
    res = pl.pallas_call(
        body, name="ffn_bwd_b", grid=(nt,),
        in_specs=[shard,
                  pl.BlockSpec((N_DEV, HALO, FF_SHARD), lambda i: (0, jnp.minimum((i + 1) * hb, S // HALO - 1), 0)),
                  _const(conv_w.shape), _const(w_up.shape), tok(D_MODEL), tok(D_MODEL), _const((1, D_MODEL)),
                  tok(D_MODEL)] + [ANY_SPEC] * ng,
        out_specs=[ANY_SPEC, tok(D_MODEL), tok(D_MODEL), pl.BlockSpec((1, D_MODEL), lambda i: (0, 0))]
        + [ANY_SPEC] * ng,
        out_shape=[jax.ShapeDtypeStruct((N_DEV, FF_SHARD, D_MODEL), BF16), jax.ShapeDtypeStruct((S, D_MODEL), F32),
                   jax.ShapeDtypeStruct((S, D_MODEL), BF16), jax.ShapeDtypeStruct((1, D_MODEL), F32)]
        + _exchange_shapes(grads, (True,) * ng),
        scratch_shapes=[pltpu.VMEM((T + HALO, FF_SHARD), F32), pltpu.VMEM((N_DEV, FF_SHARD, D_MODEL), F32),
                        pltpu.VMEM((FF_SHARD, D_MODEL), BF16), pltpu.SemaphoreType.DMA] + _exchange_sems(ng),
        compiler_params=_params("arbitrary"),
    )(dup, dup, conv_w, w_up, h1, hn, g2, dh2, *grads)
    return res[0], res[1], res[2], res[3], res[4:]


def _mix_bwd(dh1b, w_out, dlt, pool_w, pool_scale):
    S = dh1b.shape[0]
    nt = S // TM

    def body(dh_ref, w_ref, dlt_ref, pw_ref, ps_ref, datt_ref, du_ref, dpw_ref, dps_ref, stage, carry):
        i = pl.program_id(0)
        tile = nt - 1 - i

        @pl.when(i == 0)
        def _():
            dpw_ref[...] = jnp.zeros(dpw_ref.shape, F32)
            dps_ref[...] = jnp.zeros(dps_ref.shape, F32)
            carry[...] = jnp.zeros(carry.shape, F32)

        dmix = lax.dot_general(dh_ref[...], w_ref[...], NT, preferred_element_type=F32)
        datt_ref[...] = dmix[:, 0:ATT_WIDTH]
        for g, w in enumerate(POOL_WINDOWS):
            cols = slice(g * POOL_GROUP, (g + 1) * POOL_GROUP)
            dpool = dmix[:, ATT_WIDTH + g * POOL_GROUP:ATT_WIDTH + (g + 1) * POOL_GROUP]
            dl = dlt_ref[:, cols]
            pw = pw_ref[g].astype(BF16)
            yg = jnp.dot(dl, pw, preferred_element_type=F32)
            dps_ref[:, cols] += _colsum(dpool * yg)
            dy = (dpool * ps_ref[:, cols]).astype(BF16)
            dpw_ref[g] += lax.dot_general(dl, dy, TN, preferred_element_type=F32)
            ddlt = lax.dot_general(dy, pw, NT, preferred_element_type=F32)
            cg = ddlt / _pool_count(tile, w)
            stage[0:TM, :] = cg
            stage[TM:TM + HALO, :] = carry[:, cols]
            acc = cg
            for j in range(1, w):
                acc = acc + stage[j:j + TM, :]
            du_ref[:, cols] = acc - ddlt
            carry[:, cols] = cg[0:HALO, :]

    tok = lambda w: pl.BlockSpec((TM, w), lambda i: (nt - 1 - i, 0))
    return pl.pallas_call(
        body, name="mix_bwd", grid=(nt,),
        in_specs=[tok(D_MODEL), _const(w_out.shape), tok(POOL_WIDTH), _const(pool_w.shape), _const((1, POOL_WIDTH))],
        out_specs=[tok(ATT_WIDTH), tok(POOL_WIDTH), pl.BlockSpec(pool_w.shape, lambda i: (0, 0, 0)),
                   pl.BlockSpec((1, POOL_WIDTH), lambda i: (0, 0))],
        out_shape=[jax.ShapeDtypeStruct((S, ATT_WIDTH), F32), jax.ShapeDtypeStruct((S, POOL_WIDTH), F32),
                   jax.ShapeDtypeStruct(pool_w.shape, F32), jax.ShapeDtypeStruct((1, POOL_WIDTH), F32)],
        scratch_shapes=[pltpu.VMEM((TM + HALO, POOL_GROUP), F32), pltpu.VMEM((HALO, POOL_WIDTH), F32)],
        compiler_params=_params("arbitrary"),
    )(dh1b, w_out, dlt, pool_w, pool_scale)


def _attn_bwd(slopes, q, k, v, o, lse, do, grads, scatter):
    S = q.shape[0]
    CH = 512
    ng = len(grads)
    steps = ATT_WIDTH // LANES

    def body(slope_ref, q_ref, k_ref, v_ref, o_ref, lse_ref, do_ref, *rest):
        dq_ref, dk_ref, dv_ref = rest[ng:ng + 3]
        dl_s, bias_s = rest[2 * ng + 3:2 * ng + 5]
        hp = pl.program_id(0)
        copies = _exchange_copies(rest[:ng], rest[ng + 3:2 * ng + 3], scatter, *rest[2 * ng + 5:])

        @pl.when(hp == 0)
        def _():
            for cp in copies:
                cp.start()

        is0 = lax.broadcasted_iota(jnp.int32, (SPAN, LANES), 1) < HEAD_DIM
        is0c = lax.broadcasted_iota(jnp.int32, (CH, LANES), 1) < HEAD_DIM

        def prep(ci, carry):
            rows = pl.ds(pl.multiple_of(ci * CH, CH), CH)
            prod = do_ref[rows, :] * o_ref[rows, :]
            d0 = jnp.sum(jnp.where(is0c, prod, 0.0), axis=-1, keepdims=True)
            d1 = jnp.sum(jnp.where(is0c, 0.0, prod), axis=-1, keepdims=True)
            dl_s[rows, :] = jnp.where(is0c, d0, d1)
            zero = jnp.zeros((CH, LANES), F32)
            dq_ref[rows, :] = zero
            dk_ref[rows, :] = zero
            dv_ref[rows, :] = zero
            return carry

        lax.fori_loop(0, S // CH, prep, 0)

        for d in DILATIONS:
            nb, ngroups = _attn_groups(S, d, ATTN_GROUP_BWD)
            _attn_fill_bias(bias_s, slope_ref, hp, d)

            def group(i, carry, d=d, nb=nb):
                blocks = [_attn_block(i, g, d, nb, ATTN_GROUP_BWD) for g in range(ATTN_GROUP_BWD)]
                loaded = [(q_ref[rows, :], do_ref[rows, :], lse_ref[rows, :], dl_s[rows, :], k_ref[krows, :],
                           v_ref[krows, :].astype(BF16)) for rows, krows, _ in blocks]
                new = []
                for (rows, krows, tab), (qb, dob, lse_b, dl_b, kf, vb) in zip(blocks, loaded):
                    kb = kf.astype(BF16)
                    qs = _stack_heads(qb, is0).astype(BF16)
                    dos = _stack_heads(dob, is0).astype(BF16)
                    lse_s = jnp.concatenate([lse_b[:, 0:1], lse_b[:, HEAD_DIM:HEAD_DIM + 1]], axis=0)
                    dl_s2 = jnp.concatenate([dl_b[:, 0:1], dl_b[:, HEAD_DIM:HEAD_DIM + 1]], axis=0)
                    s = lax.dot_general(qs, kb, NT, preferred_element_type=F32) + bias_s[tab]
                    pr = jnp.exp(s - lse_s)
                    dp = lax.dot_general(dos, vb, NT, preferred_element_type=F32)
                    ds = (pr * (dp - dl_s2)).astype(BF16)
                    dv_c = lax.dot_general(pr.astype(BF16), dos, TN, preferred_element_type=F32)
                    dk_c = lax.dot_general(ds, qs, TN, preferred_element_type=F32)
                    dq_c = _unstack_heads(jnp.dot(ds, kb, preferred_element_type=F32), is0)
                    new.append((dq_c, dk_c, dv_c))
                old = [(dq_ref[rows, :], dk_ref[krows, :], dv_ref[krows, :]) for rows, krows, _ in blocks]
                for (rows, krows, _), (dq_c, dk_c, dv_c), (dq_o, dk_o, dv_o) in zip(blocks, new, old):
                    dq_ref[rows, :] = dq_o + dq_c
                    dk_ref[krows, :] = dk_o + dk_c
                    dv_ref[krows, :] = dv_o + dv_c
                return carry

            lax.fori_loop(0, ngroups, group, 0)

        @pl.when(hp == steps - 1)
        def _():
            for cp in copies:
                cp.wait()

    col = pl.BlockSpec((S, LANES), lambda i: (0, i))
    res = pl.pallas_call(
        body, name="attn_bwd", grid=(steps,),
        in_specs=[pl.BlockSpec(memory_space=pltpu.SMEM)] + [col] * 6 + [ANY_SPEC] * ng,
        out_specs=[col] * 3 + [ANY_SPEC] * ng,
        out_shape=[jax.ShapeDtypeStruct((S, ATT_WIDTH), F32)] * 3 + _exchange_shapes(grads, scatter),
        scratch_shapes=[pltpu.VMEM((S, LANES), F32), pltpu.VMEM((2, 2 * SPAN, 2 * SPAN), F32)] + _exchange_sems(ng),
        compiler_params=_params("arbitrary"),
    )(slopes, q, k, v, o, lse, do, *grads)
    return res[0], res[1], res[2], res[3:]


def _in_bwd(dq, dk, dv, du, w_in, x, g1, dh1):
    S = x.shape[0]

    def body(dq_ref, dk_ref, dv_ref, du_ref, w_ref, x_ref, g_ref, dh1_ref, dz_ref, dx_ref, dg_ref):
        @pl.when(pl.program_id(0) == 0)
        def _():
            dg_ref[...] = jnp.zeros(dg_ref.shape, F32)

        srcs = (dq_ref, dk_ref, dv_ref, du_ref)
        dhn = jnp.zeros((TM, D_MODEL), F32)
        for j in range(N_DEV):
            dz = srcs[j // 2][:, (j % 2) * 256:(j % 2 + 1) * 256]
            if j < 2:
                dz = dz * (HEAD_DIM ** -0.5)
            dz = dz.astype(BF16)
            dz_ref[j] = dz
            dhn = dhn + lax.dot_general(dz, w_ref[j], NT, preferred_element_type=F32)
        g = g_ref[...]
        r, n = _rms(x_ref[...])
        dg_ref[...] += _colsum(dhn * n)
        dx_ref[...] = dh1_ref[...] + _rms_bwd(r, n, g, dhn)

    tok = lambda w: pl.BlockSpec((TM, w), lambda i: (i, 0))
    return pl.pallas_call(
        body, name="in_bwd", grid=(S // TM,),
        in_specs=[tok(512)] * 4 + [_const(w_in.shape), tok(D_MODEL), _const((1, D_MODEL)), tok(D_MODEL)],
        out_specs=[pl.BlockSpec((N_DEV, TM, 256), lambda i: (0, i, 0)), tok(D_MODEL),
                   pl.BlockSpec((1, D_MODEL), lambda i: (0, 0))],
        out_shape=[jax.ShapeDtypeStruct((N_DEV, S, 256), BF16), jax.ShapeDtypeStruct((S, D_MODEL), F32),
                   jax.ShapeDtypeStruct((1, D_MODEL), F32)],
        compiler_params=_params("arbitrary"),
    )(dq, dk, dv, du, w_in, x, g1, dh1)


def _adamw(name, parts, w, m, v):
    R, C = w.shape
    rb = R
    for cand in (256, 128, 64, 32, 16, 8):
        if R % cand == 0 and R > cand:
            rb = cand
            break

    def body(p_ref, w_ref, m_ref, v_ref, g_ref, d_ref, mo_ref, vo_ref):
        g = p_ref[0].astype(F32)
        for s in range(1, N_DEV):
            g = g + p_ref[s].astype(F32)
        g_ref[...] = g
        d_ref[...], mo_ref[...], vo_ref[...] = _adam_update(g, w_ref[...], m_ref[...], v_ref[...])

    blk = pl.BlockSpec((rb, C), lambda i: (i, 0))
    return pl.pallas_call(
        body, name=name, grid=(R // rb,),
        in_specs=[pl.BlockSpec((N_DEV, rb, C), lambda i: (0, i, 0)), blk, blk, blk],
        out_specs=[blk] * 4,
        out_shape=[jax.ShapeDtypeStruct((R, C), F32)] * 4,
        compiler_params=_params("arbitrary"),
    )(parts, w, m, v)


def _adam_update(g, w, m, v):
    m_new = ADAM_B1 * m + (1.0 - ADAM_B1) * g
    v_new = ADAM_B2 * v + (1.0 - ADAM_B2) * (g * g)
    m_hat = m_new / (1.0 - ADAM_B1 ** ADAM_STEP)
    v_hat = v_new / (1.0 - ADAM_B2 ** ADAM_STEP)
    return -ADAM_LR * (m_hat / (jnp.sqrt(v_hat) + ADAM_EPS) + ADAM_WD * w), m_new, v_new


def _adamw_small(parts, loss_parts, ws, ms, vs):
    n = len(ws)

    def body(*refs):
        p_refs, lp_ref = refs[:n], refs[n]
        w_refs, m_refs, v_refs = refs[n + 1:2 * n + 1], refs[2 * n + 1:3 * n + 1], refs[3 * n + 1:4 * n + 1]
        outs = refs[4 * n + 1:]
        for i in range(n):
            g = p_refs[i][0]
            for s in range(1, N_DEV):
                g = g + p_refs[i][s]
            d, m_new, v_new = _adam_update(g, w_refs[i][...], m_refs[i][...], v_refs[i][...])
            outs[i][...] = g
            outs[n + i][...] = d
            outs[2 * n + i][...] = m_new
            outs[3 * n + i][...] = v_new
        tot = lp_ref[0]
        for s in range(1, N_DEV):
            tot = tot + lp_ref[s]
        outs[4 * n][...] = tot

    shapes = [jax.ShapeDtypeStruct(w.shape, F32) for w in ws]
    res = pl.pallas_call(
        body, name="adamw_replicated",
        out_shape=shapes * 4 + [jax.ShapeDtypeStruct(loss_parts.shape[1:], F32)],
        compiler_params=_params(),
    )(*parts, loss_parts, *ws, *ms, *vs)
    return res[:n], res[n:2 * n], res[2 * n:3 * n], res[3 * n:4 * n], res[4 * n]


def _gather2(name, arrays):
    n = len(arrays)

    def body(*refs):
        first, passed, last = _gather2_copies(refs[:n], refs[n:2 * n], *refs[2 * n:])
        for cp in first:
            cp.start()
        for arrival, cp in passed:
            arrival.wait_recv()
            cp.start()
        for wait in last:
            wait()

    return pl.pallas_call(
        body, name=name,
        in_specs=[ANY_SPEC] * n, out_specs=[ANY_SPEC] * n, out_shape=_exchange_shapes(arrays, (False,) * n),
        scratch_shapes=_exchange_sems(n),
    )(*arrays)


def _dw_in_exchange(hn, dz, small):
    S = hn.shape[0]
    nt = S // TK
    ns = len(small)
    kd, nd = hn.shape[1], dz.shape[2]
    me_arr = (4 * lax.axis_index("x") + 2 * lax.axis_index("y") + lax.axis_index("c")).astype(jnp.int32).reshape(1)

    def body(me_ref, x_ref, dy_ref, *rest):
        recv_ref = rest[ns]
        acc, stage, send_sems, recv_sems, own_sem = rest[2 * ns + 1:2 * ns + 6]
        j, t = pl.program_id(0), pl.program_id(1)
        x, y, c = lax.axis_index("x"), lax.axis_index("y"), lax.axis_index("c")
        me = 4 * x + 2 * y + c
        small_copies = _exchange_copies(rest[:ns], rest[ns + 1:2 * ns + 1], (False,) * ns, *rest[2 * ns + 6:])

        @pl.when((j == 0) & (t == 0))
        def _():
            for cp in small_copies:
                cp.start()

        @pl.when(t == 0)
        def _():
            acc[...] = jnp.zeros(acc.shape, F32)

        acc[...] += lax.dot_general(x_ref[...], dy_ref[...], TN, preferred_element_type=F32)

        def to_owner(k, owner):
            return pltpu.make_async_remote_copy(
                src_ref=stage.at[owner], dst_ref=recv_ref.at[me], send_sem=send_sems.at[k], recv_sem=recv_sems.at[k],
                device_id=(owner // 4, (owner // 2) % 2, owner % 2), device_id_type=MESH)

        own = pltpu.make_async_copy(stage.at[me], recv_ref.at[me], own_sem)

        @pl.when(t == nt - 1)
        def _():
            owner = (me + 1 + j) % N_DEV
            stage[owner] = acc[...].astype(BF16)

            @pl.when(j < N_DEV - 1)
            def _():
                to_owner(j, owner).start()

            @pl.when(j == N_DEV - 1)
            def _():
                own.start()
                own.wait()
                for k in range(N_DEV - 1):
                    to_owner(k, me).wait_send()
                    to_owner(k, me).wait_recv()
                for cp in small_copies:
                    cp.wait()

    slab = lambda j, me_ref: (me_ref[0] + 1 + j) % N_DEV
    grid_spec = pltpu.PrefetchScalarGridSpec(
        num_scalar_prefetch=1, grid=(N_DEV, nt),
        in_specs=[pl.BlockSpec((TK, kd), lambda j, t, me_ref: (t, 0)),
                  pl.BlockSpec((None, TK, nd), lambda j, t, me_ref: (slab(j, me_ref), t, 0))] + [ANY_SPEC] * ns,
        out_specs=[ANY_SPEC] * (ns + 1),
        scratch_shapes=[pltpu.VMEM((kd, nd), F32), pltpu.VMEM((N_DEV, kd, nd), BF16),
                        pltpu.SemaphoreType.DMA((N_DEV - 1,)), pltpu.SemaphoreType.DMA((N_DEV - 1,)),
                        pltpu.SemaphoreType.DMA] + _exchange_sems(ns))
    res = pl.pallas_call(
        body, name="dw_in_exchange", grid_spec=grid_spec,
        out_shape=[jax.ShapeDtypeStruct((N_DEV, kd, nd), BF16)] + _exchange_shapes(small, (False,) * ns),
        compiler_params=_params("arbitrary", "arbitrary"),
    )(me_arr, hn, dz, *small)
    return res[0], res[1:]


def kernel(x, p, ln_mix, w_in, pool_w, pool_scale, w_out, ln_ffn, w_up, conv_w, conv_b, w_down, ln_ple, w_ple_gate, w_ple, ln_final, loss_target, m_ln_mix, m_w_in, m_pool_w, m_pool_scale, m_w_out, m_ln_ffn, m_w_up, m_conv_w, m_conv_b, m_w_down, m_ln_ple, m_w_ple_gate, m_w_ple, m_ln_final, v_ln_mix, v_w_in, v_pool_w, v_pool_scale, v_w_out, v_ln_ffn, v_w_up, v_conv_w, v_conv_b, v_w_down, v_ln_ple, v_w_ple_gate, v_w_ple, v_ln_final):
    xs, ps, tgt, pool_w0 = x[0], p[0, 0], loss_target[0], pool_w[0]
    slopes = jnp.exp2(-8.0 * (jnp.arange(N_HEADS, dtype=F32) + 1.0) / N_HEADS)
    conv_b_s = conv_b.reshape(N_DEV, 1, FF_SHARD)

    (w_in_g,) = _gather2("gather_w_in", [w_in[0].astype(BF16)])
    (q, k, v, u, hn1), (w_out_g,) = _qkvu(xs, ln_mix, w_in_g, [w_out[0].astype(BF16)])
    att, lse, (w_up_g, conv_w_g) = _attn_fwd(slopes, q, k, v, [w_up[0].astype(BF16), conv_w[0]])
    w_out_f = w_out_g.reshape(D_MODEL, D_MODEL)
    h1, mix, dlt, (w_down_g,) = _mix_out(xs, att, u, pool_w0, pool_scale, w_out_f, [w_down[0].astype(BF16)])
    w_down_f = w_down_g.reshape(4, FF_SHARD, D_MODEL)
    h2, hn2, up, upc, (w_pg_g, w_ple_g) = _ffn_fwd(h1, ln_ffn, w_up_g, conv_w_g, conv_b_s, w_down_f,
                                                   [w_ple_gate[0].astype(BF16), w_ple[0].astype(BF16)])
    w_pg_f = w_pg_g.reshape(D_MODEL, D_MODEL)
    w_ple_f = jnp.transpose(w_ple_g, (1, 0, 2)).reshape(PLE_DIM, D_MODEL)
    loss_blk, dh2, dh2b, d_w_pg, d_w_ple, d_ln_ple, d_ln_final = _head(
        h2, ps, ln_ple, w_pg_f, w_ple_f, ln_final.reshape(1, D_MODEL), tgt)

    d_w_pg = d_w_pg.reshape(N_DEV, D_MODEL // N_DEV, D_MODEL)
    d_w_ple = jnp.transpose(d_w_ple.reshape(PLE_DIM, N_DEV, LANES), (1, 0, 2))
    d_w_down, dup, d_conv_w, d_conv_b, (r_w_pg, r_w_ple) = _ffn_bwd_a(dh2b, up, upc, w_down_f, [d_w_pg, d_w_ple])
    d_w_down = d_w_down.reshape(N_DEV, D_FF // N_DEV, D_MODEL)
    d_w_up, dh1, dh1b, d_ln_ffn, (r_conv_w, r_w_down) = _ffn_bwd_b(
        dup, conv_w_g, w_up_g, h1, hn2, ln_ffn, dh2, [d_conv_w, d_w_down])
    datt, du, d_pool_w, d_pool_scale = _mix_bwd(dh1b, w_out_f, dlt, pool_w0, pool_scale)
    d_w_out = _wgrad("dw_out", mix, dh1b, "full", "full", 1, D_MODEL, D_MODEL, tk=TM * 2)
    d_w_out = d_w_out.reshape(N_DEV, D_MODEL // N_DEV, D_MODEL)
    rep_late = [d_pool_w, d_pool_scale, d_ln_ffn, d_conv_b.reshape(1, 2 * D_FF), d_ln_ple, d_ln_final, loss_blk]
    dq, dk, dv, received = _attn_bwd(slopes, q, k, v, att, lse, datt, [d_w_out, d_w_up] + rep_late,
                                     (True, True) + (False,) * len(rep_late))
    r_w_out, r_w_up, r_rep = received[0], received[1], list(received[2:])
    dz, grad_x, d_ln_mix = _in_bwd(dq, dk, dv, du, w_in_g, xs, ln_mix, dh1)

    rep_names = ("ln_mix", "pool_w", "pool_scale", "ln_ffn", "conv_b", "ln_ple", "ln_final")
    rep_w = [ln_mix, pool_w0, pool_scale, ln_ffn, conv_b, ln_ple, ln_final.reshape(1, D_MODEL)]
    rep_m = [m_ln_mix, m_pool_w[0], m_pool_scale, m_ln_ffn, m_conv_b, m_ln_ple, m_ln_final.reshape(1, D_MODEL)]
    rep_v = [v_ln_mix, v_pool_w[0], v_pool_scale, v_ln_ffn, v_conv_b, v_ln_ple, v_ln_final.reshape(1, D_MODEL)]
    r_w_in, (r_ln_mix,) = _dw_in_exchange(hn1, dz, [d_ln_mix])
    small = _adamw_small([r_ln_mix] + r_rep[:-1], r_rep[-1], rep_w, rep_m, rep_v)
    loss = small[4][0, 0]

    sharded = {}
    sharded["w_in"] = _adamw("adamw_w_in", r_w_in, w_in[0], m_w_in[0], v_w_in[0])
    sharded["w_out"] = _adamw("adamw_w_out", r_w_out, w_out[0], m_w_out[0], v_w_out[0])
    sharded["w_up"] = [t.T for t in _adamw("adamw_w_up", r_w_up, w_up[0].T, m_w_up[0].T, v_w_up[0].T)]
    sharded["conv_w"] = _adamw("adamw_conv_w", r_conv_w, conv_w[0], m_conv_w[0], v_conv_w[0])
    sharded["w_down"] = _adamw("adamw_w_down", r_w_down, w_down[0], m_w_down[0], v_w_down[0])
    sharded["w_ple_gate"] = _adamw("adamw_w_ple_gate", r_w_pg, w_ple_gate[0], m_w_ple_gate[0], v_w_ple_gate[0])
    sharded["w_ple"] = _adamw("adamw_w_ple", r_w_ple, w_ple[0], m_w_ple[0], v_w_ple[0])

    shapes = dict(w_in=w_in, w_out=w_out, w_up=w_up, conv_w=conv_w, w_down=w_down, w_ple_gate=w_ple_gate, w_ple=w_ple,
                  ln_mix=ln_mix, pool_w=pool_w, pool_scale=pool_scale, ln_ffn=ln_ffn, conv_b=conv_b, ln_ple=ln_ple,
                  ln_final=ln_final)

    def leaf(kind, n):
        src = sharded[n][kind] if n in sharded else small[kind][rep_names.index(n)]
        return src.reshape(shapes[n].shape)

    order = ("ln_mix", "w_in", "pool_w", "pool_scale", "w_out", "ln_ffn", "w_up", "conv_w", "conv_b", "w_down", "ln_ple",
             "w_ple_gate", "w_ple", "ln_final")
    outs = [loss, grad_x[None]]
    for kind in range(4):
        outs += [leaf(kind, n) for n in order]
    return tuple(outs)
```

```python
import jax
import jax.numpy as jnp
from jax import lax
from jax.experimental import pallas as pl
from jax.experimental.pallas import tpu as pltpu

F32 = jnp.float32
BF16 = jnp.bfloat16

N_DEV = 8
D_MODEL = 1024
ATT_WIDTH = 512
POOL_WIDTH = 512
N_HEADS = 8
HEAD_DIM = 64
SPAN = 128
DILATIONS = (1, 4, 16)
POOL_WINDOWS = (2, 4, 8, 16)
POOL_GROUP = 128
D_FF = 2816
FF_SHARD = 2 * D_FF // N_DEV
PLE_DIM = 256
EPS = 1e-6
NEG = -1e30

ADAM_LR = 0.001
ADAM_B1 = 0.9
ADAM_B2 = 0.999
ADAM_EPS = 1e-08
ADAM_WD = 0.01
ADAM_STEP = 10

LANES = 128
HALO = 16
TM = 512
TM_FF = 256
TK = 4096
ATTN_GROUP_FWD = 16
ATTN_GROUP_BWD = 8
VMEM_LIMIT = 56 * 1024 * 1024

MESH = pl.DeviceIdType.MESH
NT = (((1,), (1,)), ((), ()))
TN = (((0,), (0,)), ((), ()))


def _params(*sem):
    return pltpu.CompilerParams(dimension_semantics=sem or None, vmem_limit_bytes=VMEM_LIMIT)


def _const(shape):
    n = len(shape)
    return pl.BlockSpec(shape, lambda *_: (0,) * n, pipeline_mode=pl.Buffered(1))


def _rms(h):
    r = lax.rsqrt(jnp.mean(h * h, axis=-1, keepdims=True) + EPS)
    return r, h * r


def _rms_bwd(r, n, g, dhn):
    dn = dhn * g
    return r * (dn - n * jnp.mean(dn * n, axis=-1, keepdims=True))


def _colsum(a):
    return jnp.sum(a, axis=0, keepdims=True)


def _gather2_copies(ins, outs, send_sems, recv_sems, local_sems):
    n = len(ins)
    x, y, c = lax.axis_index("x"), lax.axis_index("y"), lax.axis_index("c")
    slot = lambda px, py, pc: 4 * px + 2 * py + pc
    chips = [(x, 1 - y), (1 - x, y), (1 - x, 1 - y)]
    first, passed, last = [], [], []

    def remote(a, r, src, dst_slot, to):
        return pltpu.make_async_remote_copy(
            src_ref=src, dst_ref=outs[a].at[dst_slot],
            send_sem=send_sems.at[a * (N_DEV - 1) + r], recv_sem=recv_sems.at[a * (N_DEV - 1) + r],
            device_id=to, device_id_type=MESH)

    for a in range(n):
        mine = pltpu.make_async_copy(ins[a], outs[a].at[slot(x, y, c)], local_sems.at[a])
        to_sibling = remote(a, 0, ins[a], slot(x, y, c), (x, y, 1 - c))
        first += [mine, to_sibling]
        last += [mine.wait, to_sibling.wait_send, to_sibling.wait_recv]
        for r, (px, py) in enumerate(chips, start=1):
            to_chip = remote(a, r, ins[a], slot(x, y, c), (px, py, c))
            onward = remote(a, 3 + r, outs[a].at[slot(px, py, c)], slot(px, py, c), (x, y, 1 - c))
            first.append(to_chip)
            passed.append((to_chip, onward))
            last += [to_chip.wait_send, onward.wait_send, onward.wait_recv]
    return first, passed, last


def _gather2_begin(plan, step, pass_step):
    first, passed, _ = plan

    @pl.when(step == 0)
    def _():
        for cp in first:
            cp.start()

    @pl.when(step == pass_step)
    def _():
        for arrival, cp in passed:
            arrival.wait_recv()
            cp.start()


def _gather2_end(plan, step, nsteps):
    @pl.when(step == nsteps - 1)
    def _():
        for wait in plan[2]:
            wait()


ANY_SPEC = pl.BlockSpec(memory_space=pl.ANY)


def _exchange_shapes(arrays, scatter):
    out = []
    for a, s in zip(arrays, scatter):
        slab = a.shape[1:] if s else a.shape
        out.append(jax.ShapeDtypeStruct((N_DEV,) + tuple(slab), a.dtype))
    return out


def _exchange_sems(n):
    return [pltpu.SemaphoreType.DMA((n * (N_DEV - 1),)), pltpu.SemaphoreType.DMA((n * (N_DEV - 1),)),
            pltpu.SemaphoreType.DMA((n,))]


def _exchange_copies(ins, outs, scatter, send_sems, recv_sems, local_sems):
    n = len(ins)
    x, y, c = lax.axis_index("x"), lax.axis_index("y"), lax.axis_index("c")
    me = 4 * x + 2 * y + c
    copies = []
    for a in range(n):
        src = ins[a].at[me] if scatter[a] else ins[a]
        copies.append(pltpu.make_async_copy(src, outs[a].at[me], local_sems.at[a]))
    for k in range(1, N_DEV):
        px = 1 - x if k & 4 else x
        py = 1 - y if k & 2 else y
        pc = 1 - c if k & 1 else c
        pid = 4 * px + 2 * py + pc
        for a in range(n):
            src = ins[a].at[pid] if scatter[a] else ins[a]
            copies.append(pltpu.make_async_remote_copy(
                src_ref=src, dst_ref=outs[a].at[me],
                send_sem=send_sems.at[a * (N_DEV - 1) + k - 1], recv_sem=recv_sems.at[a * (N_DEV - 1) + k - 1],
                device_id=(px, py, pc), device_id_type=MESH))
    return copies


def _qkvu(x, g1, w_in, shards):
    S = x.shape[0]
    ns = len(shards)
    nsteps = S // TM

    def body(x_ref, g_ref, w_ref, *rest):
        q_ref, k_ref, v_ref, u_ref, hn_ref = rest[ns:ns + 5]
        plan = _gather2_copies(rest[:ns], rest[ns + 5:2 * ns + 5], *rest[2 * ns + 5:])
        _gather2_begin(plan, pl.program_id(0), nsteps - 2)
        r, n = _rms(x_ref[...])
        hn = (n * g_ref[...]).astype(BF16)
        hn_ref[...] = hn
        outs = (q_ref, k_ref, v_ref, u_ref)
        for j in range(N_DEV):
            z = jnp.dot(hn, w_ref[j], preferred_element_type=F32)
            if j < 2:
                z = z * (HEAD_DIM ** -0.5)
            outs[j // 2][:, (j % 2) * 256:(j % 2 + 1) * 256] = z
        _gather2_end(plan, pl.program_id(0), nsteps)

    tok = lambda w: pl.BlockSpec((TM, w), lambda i: (i, 0))
    res = pl.pallas_call(
        body, name="qkvu", grid=(nsteps,),
        in_specs=[tok(D_MODEL), _const((1, D_MODEL)), _const(w_in.shape)] + [ANY_SPEC] * ns,
        out_specs=[tok(512)] * 4 + [tok(D_MODEL)] + [ANY_SPEC] * ns,
        out_shape=[jax.ShapeDtypeStruct((S, 512), F32)] * 4 + [jax.ShapeDtypeStruct((S, D_MODEL), BF16)]
        + _exchange_shapes(shards, (False,) * ns),
        scratch_shapes=_exchange_sems(ns),
        compiler_params=_params("arbitrary"),
    )(x, g1, w_in, *shards)
    return res[:5], res[5:]


def _attn_fill_bias(bias_s, slope_ref, hp, d):
    qi = lax.broadcasted_iota(jnp.int32, (SPAN, 2 * SPAN), 0)
    kj = lax.broadcasted_iota(jnp.int32, (SPAN, 2 * SPAN), 1)
    for t, diff in enumerate((qi + SPAN - kj, qi - kj)):
        valid = (diff >= 0) & (diff <= SPAN)
        dist = diff.astype(F32) * float(d)
        for h in range(2):
            bias_s[t, h * SPAN:(h + 1) * SPAN, :] = jnp.where(valid, -slope_ref[2 * hp + h] * dist, NEG)


def _stack_heads(x, is0):
    return jnp.concatenate([jnp.where(is0, x, 0.0), jnp.where(is0, 0.0, x)], axis=0)


def _unstack_heads(y, is0):
    return jnp.where(is0, y[0:SPAN], y[SPAN:2 * SPAN])


def _attn_block(i, g, d, nb, group):
    gr = min(d, group)
    gn = group // gr
    per = d // gr
    r = (i & (per - 1)) * gr + g % gr
    n = (i >> (per.bit_length() - 1)) + (g // gr) * (nb // gn)
    k0 = jnp.maximum(n - 1, 0)

    def ds(block, nrows):
        start = block * (SPAN * d) + r
        return pl.ds(start, nrows, stride=d) if d > 1 else pl.ds(start, nrows)

    return ds(n, SPAN), ds(k0, 2 * SPAN), jnp.where(n == 0, 1, 0)


def _attn_groups(S, d, group, writes_key_rows=False):
    nb = S // d // SPAN
    gn = group // min(d, group)
    assert nb >= 2 and nb % gn == 0 and (gn == 1 or nb // gn >= (3 if writes_key_rows else 2))
    return nb, d * nb // group


def _attn_fwd(slopes, q, k, v, shards):
    S = q.shape[0]
    ns = len(shards)
    steps = ATT_WIDTH // LANES

    def body(slope_ref, q_ref, k_ref, v_ref, *rest):
        o_ref, lse_ref = rest[ns:ns + 2]
        m_s, l_s, bias_s = rest[2 * ns + 2:2 * ns + 5]
        hp = pl.program_id(0)
        plan = _gather2_copies(rest[:ns], rest[ns + 2:2 * ns + 2], *rest[2 * ns + 5:])
        _gather2_begin(plan, hp, steps - 1)

        is0 = lax.broadcasted_iota(jnp.int32, (SPAN, LANES), 1) < HEAD_DIM
        for pi, d in enumerate(DILATIONS):
            nb, ngroups = _attn_groups(S, d, ATTN_GROUP_FWD)
            _attn_fill_bias(bias_s, slope_ref, hp, d)

            def group(i, carry, d=d, pi=pi, nb=nb):
                blocks = [_attn_block(i, g, d, nb, ATTN_GROUP_FWD) for g in range(ATTN_GROUP_FWD)]
                loaded = [(q_ref[rows, :], k_ref[krows, :].astype(BF16), v_ref[krows, :].astype(BF16))
                          for rows, krows, _ in blocks]
                new = []
                for (rows, krows, tab), (qb, kb, vb) in zip(blocks, loaded):
                    qs = _stack_heads(qb, is0).astype(BF16)
                    s = lax.dot_general(qs, kb, NT, preferred_element_type=F32) + bias_s[tab]
                    m = jnp.max(s, axis=-1, keepdims=True)
                    e = jnp.exp(s - m)
                    l = jnp.sum(e, axis=-1, keepdims=True)
                    pv = jnp.dot(e.astype(BF16), vb, preferred_element_type=F32)
                    new.append([_unstack_heads(jnp.broadcast_to(m, pv.shape), is0),
                                _unstack_heads(jnp.broadcast_to(l, pv.shape), is0), _unstack_heads(pv, is0)])
                if pi > 0:
                    old = [(m_s[rows, :], l_s[rows, :], o_ref[rows, :]) for rows, _, _ in blocks]
                    for st, (m_o, l_o, o_o) in zip(new, old):
                        m_n = jnp.maximum(m_o, st[0])
                        a_o = jnp.exp(m_o - m_n)
                        a_b = jnp.exp(st[0] - m_n)
                        st[:] = [m_n, a_o * l_o + a_b * st[1], a_o * o_o + a_b * st[2]]
                for (rows, _, _), (m_b, l_b, acc) in zip(blocks, new):
                    if pi == len(DILATIONS) - 1:
                        o_ref[rows, :] = acc / l_b
                        lse_ref[rows, :] = m_b + jnp.log(l_b)
                    else:
                        o_ref[rows, :] = acc
                        m_s[rows, :] = m_b
                        l_s[rows, :] = l_b
                return carry

            lax.fori_loop(0, ngroups, group, 0)

        _gather2_end(plan, hp, steps)

    col = pl.BlockSpec((S, LANES), lambda i: (0, i))
    res = pl.pallas_call(
        body, name="attn_fwd", grid=(steps,),
        in_specs=[pl.BlockSpec(memory_space=pltpu.SMEM), col, col, col] + [ANY_SPEC] * ns,
        out_specs=[col, col] + [ANY_SPEC] * ns,
        out_shape=[jax.ShapeDtypeStruct((S, ATT_WIDTH), F32)] * 2 + _exchange_shapes(shards, (False,) * ns),
        scratch_shapes=[pltpu.VMEM((S, LANES), F32), pltpu.VMEM((S, LANES), F32),
                        pltpu.VMEM((2, 2 * SPAN, 2 * SPAN), F32)] + _exchange_sems(ns),
        compiler_params=_params("arbitrary"),
    )(slopes, q, k, v, *shards)
    return res[0], res[1], res[2:]


def _pool_count(i, w):
    t = i * TM + lax.broadcasted_iota(jnp.int32, (TM, 1), 0)
    return jnp.minimum(t + 1, w).astype(F32)


def _mix_out(x, att, u, pool_w, pool_scale, w_out, shards):
    S = x.shape[0]
    ns = len(shards)
    nsteps = S // TM

    def body(x_ref, att_ref, u_ref, pw_ref, ps_ref, w_ref, *rest):
        h1_ref, mix_ref, dlt_ref = rest[ns:ns + 3]
        ubuf = rest[2 * ns + 3]
        i = pl.program_id(0)
        plan = _gather2_copies(rest[:ns], rest[ns + 3:2 * ns + 3], *rest[2 * ns + 4:])
        _gather2_begin(plan, i, nsteps - 1)

        @pl.when(i == 0)
        def _():
            ubuf[0:HALO, :] = jnp.zeros((HALO, POOL_WIDTH), F32)

        ubuf[HALO:HALO + TM, :] = u_ref[...]
        mix_ref[:, 0:ATT_WIDTH] = att_ref[...].astype(BF16)
        for g, w in enumerate(POOL_WINDOWS):
            cols = slice(g * POOL_GROUP, (g + 1) * POOL_GROUP)
            ug = ubuf[HALO:HALO + TM, cols]
            acc = ug
            for j in range(1, w):
                acc = acc + ubuf[HALO - j:HALO - j + TM, cols]
            dlt = (acc / _pool_count(i, w) - ug).astype(BF16)
            dlt_ref[:, cols] = dlt
            yg = jnp.dot(dlt, pw_ref[g].astype(BF16), preferred_element_type=F32) * ps_ref[:, cols]
            mix_ref[:, ATT_WIDTH + g * POOL_GROUP:ATT_WIDTH + (g + 1) * POOL_GROUP] = yg.astype(BF16)
        ubuf[0:HALO, :] = ubuf[TM:TM + HALO, :]
        h1_ref[...] = x_ref[...] + jnp.dot(mix_ref[...], w_ref[...], preferred_element_type=F32)
        _gather2_end(plan, i, nsteps)

    tok = lambda w: pl.BlockSpec((TM, w), lambda i: (i, 0))
    res = pl.pallas_call(
        body, name="mix_out", grid=(nsteps,),
        in_specs=[tok(D_MODEL), tok(ATT_WIDTH), tok(POOL_WIDTH), _const(pool_w.shape), _const((1, POOL_WIDTH)),
                  _const(w_out.shape)] + [ANY_SPEC] * ns,
        out_specs=[tok(D_MODEL), tok(D_MODEL), tok(POOL_WIDTH)] + [ANY_SPEC] * ns,
        out_shape=[jax.ShapeDtypeStruct((S, D_MODEL), F32), jax.ShapeDtypeStruct((S, D_MODEL), BF16),
                   jax.ShapeDtypeStruct((S, POOL_WIDTH), BF16)] + _exchange_shapes(shards, (False,) * ns),
        scratch_shapes=[pltpu.VMEM((TM + HALO, POOL_WIDTH), F32)] + _exchange_sems(ns),
        compiler_params=_params("arbitrary"),
    )(x, att, u, pool_w, pool_scale, w_out, *shards)
    return res[0], res[1], res[2], res[3:]


def _conv_fwd(stage, upre, prev, cw, cb):
    T = upre.shape[0]
    stage[0:HALO, :] = prev
    stage[HALO:HALO + T, :] = upre
    return cb + cw[0:1, :] * stage[HALO - 2:HALO - 2 + T, :] + cw[1:2, :] * stage[HALO - 1:HALO - 1 + T, :] + cw[2:3, :] * upre


def _ffn_fwd(h1, g2, w_up, conv_w, conv_b, w_down, shards):
    S = h1.shape[0]
    T = TM_FF
    ns = len(shards)
    nsteps = S // T

    def body(h1_ref, g_ref, wu_ref, cw_ref, cb_ref, wd_ref, *rest):
        h2_ref, hn_ref, up_ref, upc_ref = rest[ns:ns + 4]
        carry, stage = rest[2 * ns + 4:2 * ns + 6]
        i = pl.program_id(0)
        plan = _gather2_copies(rest[:ns], rest[ns + 4:2 * ns + 4], *rest[2 * ns + 6:])
        _gather2_begin(plan, i, nsteps // 2)

        @pl.when(i == 0)
        def _():
            carry[...] = jnp.zeros(carry.shape, F32)

        h1t = h1_ref[...]
        r, n = _rms(h1t)
        hn = (n * g_ref[...]).astype(BF16)
        hn_ref[...] = hn
        acc = h1t
        for j in range(4):
            conv = []
            for jj in (j, j + 4):
                upre = jnp.dot(hn, wu_ref[jj], preferred_element_type=F32)
                up_ref[jj] = upre.astype(BF16)
                conv.append(_conv_fwd(stage, upre, carry[jj], cw_ref[jj], cb_ref[jj]))
                upc_ref[jj] = conv[-1].astype(BF16)
                carry[jj] = stage[T:T + HALO, :]
            gate, val = conv
            a = gate * jax.nn.sigmoid(gate) * val
            acc = acc + jnp.dot(a.astype(BF16), wd_ref[j], preferred_element_type=F32)
        h2_ref[...] = acc
        _gather2_end(plan, i, nsteps)

    tok = lambda w: pl.BlockSpec((T, w), lambda i: (i, 0))
    res = pl.pallas_call(
        body, name="ffn_fwd", grid=(nsteps,),
        in_specs=[tok(D_MODEL), _const((1, D_MODEL)), _const(w_up.shape), _const(conv_w.shape), _const(conv_b.shape),
                  _const(w_down.shape)] + [ANY_SPEC] * ns,
        out_specs=[tok(D_MODEL), tok(D_MODEL)] + [pl.BlockSpec((N_DEV, T, FF_SHARD), lambda i: (0, i, 0))] * 2
        + [ANY_SPEC] * ns,
        out_shape=[jax.ShapeDtypeStruct((S, D_MODEL), F32), jax.ShapeDtypeStruct((S, D_MODEL), BF16)]
        + [jax.ShapeDtypeStruct((N_DEV, S, FF_SHARD), BF16)] * 2 + _exchange_shapes(shards, (False,) * ns),
        scratch_shapes=[pltpu.VMEM((N_DEV, HALO, FF_SHARD), F32), pltpu.VMEM((T + HALO, FF_SHARD), F32)]
        + _exchange_sems(ns),
        compiler_params=_params("arbitrary"),
    )(h1, g2, w_up, conv_w, conv_b, w_down, *shards)
    return res[0], res[1], res[2], res[3], res[4:]


def _head(h2, p, g3, w_pg, w_ple, g4, target):
    S = h2.shape[0]
    nt = S // TM

    def body(h2_ref, p_ref, g3_ref, wpg_ref, wple_ref, g4_ref, t_ref,
             loss_ref, dh2_ref, dh2b_ref, dwpg_ref, dwple_ref, dg3_ref, dg4_ref, lacc, pg_acc, ple_acc):
        i = pl.program_id(0)

        @pl.when(i == 0)
        def _():
            lacc[...] = jnp.zeros(lacc.shape, F32)
            pg_acc[...] = jnp.zeros(pg_acc.shape, F32)
            ple_acc[...] = jnp.zeros(ple_acc.shape, F32)
            dg3_ref[...] = jnp.zeros(dg3_ref.shape, F32)
            dg4_ref[...] = jnp.zeros(dg4_ref.shape, F32)

        h2t = h2_ref[...]
        g3, g4 = g3_ref[...], g4_ref[...]
        r3, n3 = _rms(h2t)
        hn3 = (n3 * g3).astype(BF16)
        pb = p_ref[...].astype(BF16)
        gs = jax.nn.sigmoid(jnp.dot(hn3, wpg_ref[...], preferred_element_type=F32))
        pe = jnp.dot(pb, wple_ref[...], preferred_element_type=F32)
        h3 = h2t + gs * pe
        r4, n4 = _rms(h3)
        err = n4 * g4 - t_ref[...]
        lacc[...] += _colsum(err * err)
        dy = err * (1.0 / D_MODEL)
        dg4_ref[...] += _colsum(dy * n4)
        dh3 = _rms_bwd(r4, n4, g4, dy)
        dpe = (dh3 * gs).astype(BF16)
        dgl = (dh3 * pe * gs * (1.0 - gs)).astype(BF16)
        ple_acc[...] += lax.dot_general(pb, dpe, TN, preferred_element_type=F32)
        pg_acc[...] += lax.dot_general(hn3, dgl, TN, preferred_element_type=F32)
        dhn3 = lax.dot_general(dgl, wpg_ref[...], NT, preferred_element_type=F32)
        dg3_ref[...] += _colsum(dhn3 * n3)
        dh2 = dh3 + _rms_bwd(r3, n3, g3, dhn3)
        dh2_ref[...] = dh2
        dh2b_ref[...] = dh2.astype(BF16)

        @pl.when(i == nt - 1)
        def _():
            tot = 0.5 / D_MODEL * jnp.sum(lacc[...], axis=-1, keepdims=True)
            loss_ref[...] = jnp.broadcast_to(tot, loss_ref.shape)
            dwpg_ref[...] = pg_acc[...].astype(BF16)
            dwple_ref[...] = ple_acc[...].astype(BF16)

    tok = lambda w: pl.BlockSpec((TM, w), lambda i: (i, 0))
    row = pl.BlockSpec((1, D_MODEL), lambda i: (0, 0))
    act = lambda dt: jax.ShapeDtypeStruct((S, D_MODEL), dt)
    whole = lambda r: pl.BlockSpec((r, D_MODEL), lambda i: (0, 0))
    return pl.pallas_call(
        body, name="head", grid=(nt,),
        in_specs=[tok(D_MODEL), tok(PLE_DIM), _const((1, D_MODEL)), _const(w_pg.shape), _const(w_ple.shape),
                  _const((1, D_MODEL)), tok(D_MODEL)],
        out_specs=[pl.BlockSpec((8, LANES), lambda i: (0, 0)), tok(D_MODEL), tok(D_MODEL), whole(D_MODEL),
                   whole(PLE_DIM), row, row],
        out_shape=[jax.ShapeDtypeStruct((8, LANES), F32), act(F32), act(BF16),
                   jax.ShapeDtypeStruct((D_MODEL, D_MODEL), BF16), jax.ShapeDtypeStruct((PLE_DIM, D_MODEL), BF16),
                   jax.ShapeDtypeStruct((1, D_MODEL), F32), jax.ShapeDtypeStruct((1, D_MODEL), F32)],
        scratch_shapes=[pltpu.VMEM((1, D_MODEL), F32), pltpu.VMEM((D_MODEL, D_MODEL), F32),
                        pltpu.VMEM((PLE_DIM, D_MODEL), F32)],
        compiler_params=_params("arbitrary"),
    )(h2, p, g3, w_pg, w_ple, g4, target)


def _wgrad(name, x, dy, x_kind, dy_kind, nj, k_dim, n_dim, tk=TK):
    S = x.shape[-2]
    nt = S // tk

    def spec(kind, width):
        if kind == "full":
            return pl.BlockSpec((tk, width), lambda j, t: (t, 0))
        return pl.BlockSpec((None, tk, width), lambda j, t: (j, t, 0))

    def body(x_ref, dy_ref, o_ref, acc):
        t = pl.program_id(1)

        @pl.when(t == 0)
        def _():
            acc[...] = jnp.zeros(acc.shape, F32)

        acc[...] += lax.dot_general(x_ref[...].astype(BF16), dy_ref[...], TN, preferred_element_type=F32)

        @pl.when(t == nt - 1)
        def _():
            o_ref[...] = acc[...].astype(BF16)

    return pl.pallas_call(
        body, name=name, grid=(nj, nt),
        in_specs=[spec(x_kind, k_dim), spec(dy_kind, n_dim)],
        out_specs=pl.BlockSpec((None, k_dim, n_dim), lambda j, t: (j, 0, 0)),
        out_shape=jax.ShapeDtypeStruct((nj, k_dim, n_dim), BF16),
        scratch_shapes=[pltpu.VMEM((k_dim, n_dim), F32)],
        compiler_params=_params("arbitrary", "arbitrary"),
    )(x, dy)


def _row_picker(T, off0, off1):
    r = lax.broadcasted_iota(jnp.int32, (2 * T, T + HALO), 0)
    c = lax.broadcasted_iota(jnp.int32, (2 * T, T + HALO), 1)
    want = jnp.where(r < T, r + off0, r - T + off1)
    return jnp.where(c == want, 1.0, 0.0).astype(BF16)


def _ffn_bwd_a(dh2b, up, upc, w_down, grads):
    S = dh2b.shape[0]
    T = TM_FF
    hb = T // HALO
    nsteps = S // T
    ng = len(grads)

    def body(dh_ref, up_ref, halo_ref, upc_ref, wd_ref, *rest):
        dwd_ref, dup_ref, dcw_ref, dcb_ref = rest[ng:ng + 4]
        stage, dwd_acc = rest[2 * ng + 4:2 * ng + 6]
        i = pl.program_id(0)
        copies = _exchange_copies(rest[:ng], rest[ng + 4:2 * ng + 4], (True,) * ng, *rest[2 * ng + 6:])

        @pl.when(i == 0)
        def _():
            dcw_ref[...] = jnp.zeros(dcw_ref.shape, F32)
            dcb_ref[...] = jnp.zeros(dcb_ref.shape, F32)
            dwd_acc[...] = jnp.zeros(dwd_acc.shape, F32)
            for cp in copies:
                cp.start()

        @pl.when(i == nsteps - 1)
        def _():
            for cp in copies:
                cp.wait()

        dh = dh_ref[...]
        pick = _row_picker(T, HALO - 2, HALO - 1)
        for j in range(4):
            da = lax.dot_general(dh, wd_ref[j], NT, preferred_element_type=F32)
            taps = []
            for jj in (j, j + 4):
                upre = up_ref[jj]
                stage[0:HALO, :] = jnp.where(i > 0, halo_ref[jj], jnp.zeros((HALO, FF_SHARD), BF16))
                stage[HALO:HALO + T, :] = upre
                prv = jnp.dot(pick, stage[...], preferred_element_type=F32)
                taps.append((prv[0:T], prv[T:2 * T], upre.astype(F32)))
            gate, val = upc_ref[j].astype(F32), upc_ref[j + 4].astype(F32)
            sg = jax.nn.sigmoid(gate)
            silu = gate * sg
            dwd_acc[j] += lax.dot_general((silu * val).astype(BF16), dh, TN, preferred_element_type=F32)
            dgate = (da * val) * (sg + silu * (1.0 - sg))
            dval = da * silu
            for jj, dup, tp in ((j, dgate, taps[0]), (j + 4, dval, taps[1])):
                dup_ref[jj] = dup.astype(BF16)
                dcb_ref[jj] += _colsum(dup)
                for kk in range(3):
                    dcw_ref[jj, kk:kk + 1, :] += _colsum(dup * tp[kk])

        @pl.when(i == nsteps - 1)
        def _():
            dwd_ref[...] = dwd_acc[...].astype(BF16)

    tok = lambda w: pl.BlockSpec((T, w), lambda i: (i, 0))
    shard = pl.BlockSpec((N_DEV, T, FF_SHARD), lambda i: (0, i, 0))
    res = pl.pallas_call(
        body, name="ffn_bwd_a", grid=(nsteps,),
        in_specs=[tok(D_MODEL), shard,
                  pl.BlockSpec((N_DEV, HALO, FF_SHARD), lambda i: (0, jnp.maximum(i * hb - 1, 0), 0)),
                  shard, _const(w_down.shape)] + [ANY_SPEC] * ng,
        out_specs=[_const(w_down.shape), shard,
                   pl.BlockSpec((N_DEV, 3, FF_SHARD), lambda i: (0, 0, 0)),
                   pl.BlockSpec((N_DEV, 1, FF_SHARD), lambda i: (0, 0, 0))] + [ANY_SPEC] * ng,
        out_shape=[jax.ShapeDtypeStruct(w_down.shape, BF16), jax.ShapeDtypeStruct((N_DEV, S, FF_SHARD), BF16),
                   jax.ShapeDtypeStruct((N_DEV, 3, FF_SHARD), F32), jax.ShapeDtypeStruct((N_DEV, 1, FF_SHARD), F32)]
        + _exchange_shapes(grads, (True,) * ng),
        scratch_shapes=[pltpu.VMEM((T + HALO, FF_SHARD), BF16), pltpu.VMEM(w_down.shape, F32)] + _exchange_sems(ng),
        compiler_params=_params("arbitrary"),
    )(dh2b, up, up, upc, w_down, *grads)
    return res[0], res[1], res[2], res[3], res[4:]


def _ffn_bwd_b(dup, conv_w, w_up, h1, g2, dh2, grads):
    S = h1.shape[0]
    T = TM_FF
    hb = T // HALO
    nt = S // T
    ng = len(grads)

    def body(dup_ref, halo_ref, cw_ref, wu_ref, h1_ref, g_ref, dh2_ref, *rest):
        dpre_ref, dh1_ref, dh1b_ref, dg_ref = rest[ng:ng + 4]
        stage = rest[2 * ng + 4]
        i = pl.program_id(0)
        copies = _exchange_copies(rest[:ng], rest[ng + 4:2 * ng + 4], (True,) * ng, *rest[2 * ng + 5:])

        @pl.when(i == 0)
        def _():
            dg_ref[...] = jnp.zeros(dg_ref.shape, F32)
            for cp in copies:
                cp.start()

        dhn = jnp.zeros((T, D_MODEL), F32)
        for jj in range(N_DEV):
            dup = dup_ref[jj].astype(F32)
            stage[0:T, :] = dup
            stage[T:T + HALO, :] = jnp.where(i < nt - 1, halo_ref[jj].astype(F32), 0.0)
            cw = cw_ref[jj]
            dpre = (cw[2:3, :] * dup + cw[1:2, :] * stage[1:1 + T, :] + cw[0:1, :] * stage[2:2 + T, :]).astype(BF16)
            dpre_ref[jj] = dpre
            dhn = dhn + lax.dot_general(dpre, wu_ref[jj], NT, preferred_element_type=F32)
        g = g_ref[...]
        r, n = _rms(h1_ref[...])
        dg_ref[...] += _colsum(dhn * n)
        dh1 = dh2_ref[...] + _rms_bwd(r, n, g, dhn)
        dh1_ref[...] = dh1
        dh1b_ref[...] = dh1.astype(BF16)

        @pl.when(i == nt - 1)
        def _():
            for cp in copies:
                cp.wait()

    tok = lambda w: pl.BlockSpec((T, w), lambda i: (i, 0))
    shard = pl.BlockSpec((N_DEV, T, FF_SHARD), lambda i: (0, i, 0))
    res = pl.pallas_call(
        body, name="ffn_bwd_b", grid=(nt,),
        in_specs=[shard,
                  pl.BlockSpec((N_DEV, HALO, FF_SHARD), lambda i: (0, jnp.minimum((i + 1) * hb, S // HALO - 1), 0)),
                  _const(conv_w.shape), _const(w_up.shape), tok(D_MODEL), _const((1, D_MODEL)), tok(D_MODEL)]
        + [ANY_SPEC] * ng,
        out_specs=[shard, tok(D_MODEL), tok(D_MODEL), pl.BlockSpec((1, D_MODEL), lambda i: (0, 0))] + [ANY_SPEC] * ng,
        out_shape=[jax.ShapeDtypeStruct((N_DEV, S, FF_SHARD), BF16), jax.ShapeDtypeStruct((S, D_MODEL), F32),
                   jax.ShapeDtypeStruct((S, D_MODEL), BF16), jax.ShapeDtypeStruct((1, D_MODEL), F32)]
        + _exchange_shapes(grads, (True,) * ng),
        scratch_shapes=[pltpu.VMEM((T + HALO, FF_SHARD), F32)] + _exchange_sems(ng),
        compiler_params=_params("arbitrary"),
    )(dup, dup, conv_w, w_up, h1, g2, dh2, *grads)
    return res[0], res[1], res[2], res[3], res[4:]


def _mix_bwd(dh1b, w_out, dlt, mix, pool_w, pool_scale):
    S = dh1b.shape[0]
    nt = S // TM

    def body(dh_ref, w_ref, dlt_ref, mix_ref, pw_ref, ps_ref, datt_ref, du_ref, dpw_ref, dps_ref, dwo_ref,
             stage, carry, wo_acc):
        i = pl.program_id(0)
        tile = nt - 1 - i

        @pl.when(i == 0)
        def _():
            dpw_ref[...] = jnp.zeros(dpw_ref.shape, F32)
            dps_ref[...] = jnp.zeros(dps_ref.shape, F32)
            carry[...] = jnp.zeros(carry.shape, F32)
            wo_acc[...] = jnp.zeros(wo_acc.shape, F32)

        wo_acc[...] += lax.dot_general(mix_ref[...], dh_ref[...], TN, preferred_element_type=F32)

        @pl.when(i == nt - 1)
        def _():
            dwo_ref[...] = wo_acc[...].astype(BF16)

        dmix = lax.dot_general(dh_ref[...], w_ref[...], NT, preferred_element_type=F32)
        datt_ref[...] = dmix[:, 0:ATT_WIDTH]
        for g, w in enumerate(POOL_WINDOWS):
            cols = slice(g * POOL_GROUP, (g + 1) * POOL_GROUP)
            dpool = dmix[:, ATT_WIDTH + g * POOL_GROUP:ATT_WIDTH + (g + 1) * POOL_GROUP]
            dl = dlt_ref[:, cols]
            pw = pw_ref[g].astype(BF16)
            yg = jnp.dot(dl, pw, preferred_element_type=F32)
            dps_ref[:, cols] += _colsum(dpool * yg)
            dy = (dpool * ps_ref[:, cols]).astype(BF16)
            dpw_ref[g] += lax.dot_general(dl, dy, TN, preferred_element_type=F32)
            ddlt = lax.dot_general(dy, pw, NT, preferred_element_type=F32)
            cg = ddlt / _pool_count(tile, w)
            stage[0:TM, :] = cg
            stage[TM:TM + HALO, :] = carry[:, cols]
            acc = cg
            for j in range(1, w):
                acc = acc + stage[j:j + TM, :]
            du_ref[:, cols] = acc - ddlt
            carry[:, cols] = cg[0:HALO, :]

    tok = lambda w: pl.BlockSpec((TM, w), lambda i: (nt - 1 - i, 0))
    return pl.pallas_call(
        body, name="mix_bwd", grid=(nt,),
        in_specs=[tok(D_MODEL), _const(w_out.shape), tok(POOL_WIDTH), tok(D_MODEL), _const(pool_w.shape),
                  _const((1, POOL_WIDTH))],
        out_specs=[tok(ATT_WIDTH), tok(POOL_WIDTH), pl.BlockSpec(pool_w.shape, lambda i: (0, 0, 0)),
                   pl.BlockSpec((1, POOL_WIDTH), lambda i: (0, 0)), pl.BlockSpec(w_out.shape, lambda i: (0, 0))],
        out_shape=[jax.ShapeDtypeStruct((S, ATT_WIDTH), F32), jax.ShapeDtypeStruct((S, POOL_WIDTH), F32),
                   jax.ShapeDtypeStruct(pool_w.shape, F32), jax.ShapeDtypeStruct((1, POOL_WIDTH), F32),
                   jax.ShapeDtypeStruct(w_out.shape, BF16)],
        scratch_shapes=[pltpu.VMEM((TM + HALO, POOL_GROUP), F32), pltpu.VMEM((HALO, POOL_WIDTH), F32),
                        pltpu.VMEM(w_out.shape, F32)],
        compiler_params=_params("arbitrary"),
    )(dh1b, w_out, dlt, mix, pool_w, pool_scale)


def _attn_bwd(slopes, q, k, v, o, lse, do, grads, scatter):
    S = q.shape[0]
    CH = 512
    ng = len(grads)
    steps = ATT_WIDTH // LANES

    def body(slope_ref, q_ref, k_ref, v_ref, o_ref, lse_ref, do_ref, *rest):
        dq_ref, dk_ref, dv_ref = rest[ng:ng + 3]
        dl_s, bias_s = rest[2 * ng + 3:2 * ng + 5]
        hp = pl.program_id(0)
        copies = _exchange_copies(rest[:ng], rest[ng + 3:2 * ng + 3], scatter, *rest[2 * ng + 5:])

        @pl.when(hp == 0)
        def _():
            for cp in copies:
                cp.start()

        is0 = lax.broadcasted_iota(jnp.int32, (SPAN, LANES), 1) < HEAD_DIM
        is0c = lax.broadcasted_iota(jnp.int32, (CH, LANES), 1) < HEAD_DIM

        def prep(ci, carry):
            rows = pl.ds(pl.multiple_of(ci * CH, CH), CH)
            prod = do_ref[rows, :] * o_ref[rows, :]
            d0 = jnp.sum(jnp.where(is0c, prod, 0.0), axis=-1, keepdims=True)
            d1 = jnp.sum(jnp.where(is0c, 0.0, prod), axis=-1, keepdims=True)
            dl_s[rows, :] = jnp.where(is0c, d0, d1)
            zero = jnp.zeros((CH, LANES), F32)
            dq_ref[rows, :] = zero
            dk_ref[rows, :] = zero
            dv_ref[rows, :] = zero
            return carry

        lax.fori_loop(0, S // CH, prep, 0)

        for d in DILATIONS:
            nb, ngroups = _attn_groups(S, d, ATTN_GROUP_BWD, writes_key_rows=True)
            _attn_fill_bias(bias_s, slope_ref, hp, d)

            def group(i, carry, d=d, nb=nb):
                blocks = [_attn_block(i, g, d, nb, ATTN_GROUP_BWD) for g in range(ATTN_GROUP_BWD)]
                loaded = [(q_ref[rows, :], do_ref[rows, :], lse_ref[rows, :], dl_s[rows, :], k_ref[krows, :],
                           v_ref[krows, :].astype(BF16)) for rows, krows, _ in blocks]
                new = []
                for (rows, krows, tab), (qb, dob, lse_b, dl_b, kf, vb) in zip(blocks, loaded):
                    kb = kf.astype(BF16)
                    qs = _stack_heads(qb, is0).astype(BF16)
                    dos = _stack_heads(dob, is0).astype(BF16)
                    lse_s = jnp.concatenate([lse_b[:, 0:1], lse_b[:, HEAD_DIM:HEAD_DIM + 1]], axis=0)
                    dl_s2 = jnp.concatenate([dl_b[:, 0:1], dl_b[:, HEAD_DIM:HEAD_DIM + 1]], axis=0)
                    s = lax.dot_general(qs, kb, NT, preferred_element_type=F32) + bias_s[tab]
                    pr = jnp.exp(s - lse_s)
                    dp = lax.dot_general(dos, vb, NT, preferred_element_type=F32)
                    ds = (pr * (dp - dl_s2)).astype(BF16)
                    dv_c = lax.dot_general(pr.astype(BF16), dos, TN, preferred_element_type=F32)
                    dk_c = lax.dot_general(ds, qs, TN, preferred_element_type=F32)
                    dq_c = _unstack_heads(jnp.dot(ds, kb, preferred_element_type=F32), is0)
                    new.append((dq_c, dk_c, dv_c))
                old = [(dq_ref[rows, :], dk_ref[krows, :], dv_ref[krows, :]) for rows, krows, _ in blocks]
                for (rows, krows, _), (dq_c, dk_c, dv_c), (dq_o, dk_o, dv_o) in zip(blocks, new, old):
                    dq_ref[rows, :] = dq_o + dq_c
                    dk_ref[krows, :] = dk_o + dk_c
                    dv_ref[krows, :] = dv_o + dv_c
                return carry

            lax.fori_loop(0, ngroups, group, 0)

        @pl.when(hp == steps - 1)
        def _():
            for cp in copies:
                cp.wait()

    col = pl.BlockSpec((S, LANES), lambda i: (0, i))
    res = pl.pallas_call(
        body, name="attn_bwd", grid=(steps,),
        in_specs=[pl.BlockSpec(memory_space=pltpu.SMEM)] + [col] * 6 + [ANY_SPEC] * ng,
        out_specs=[col] * 3 + [ANY_SPEC] * ng,
        out_shape=[jax.ShapeDtypeStruct((S, ATT_WIDTH), F32)] * 3 + _exchange_shapes(grads, scatter),
        scratch_shapes=[pltpu.VMEM((S, LANES), F32), pltpu.VMEM((2, 2 * SPAN, 2 * SPAN), F32)] + _exchange_sems(ng),
        compiler_params=_params("arbitrary"),
    )(slopes, q, k, v, o, lse, do, *grads)
    return res[0], res[1], res[2], res[3:]


def _in_bwd(dq, dk, dv, du, w_in, x, g1, dh1):
    S = x.shape[0]

    def body(dq_ref, dk_ref, dv_ref, du_ref, w_ref, x_ref, g_ref, dh1_ref, dz_ref, dx_ref, dg_ref):
        @pl.when(pl.program_id(0) == 0)
        def _():
            dg_ref[...] = jnp.zeros(dg_ref.shape, F32)

        srcs = (dq_ref, dk_ref, dv_ref, du_ref)
        dhn = jnp.zeros((TM, D_MODEL), F32)
        for j in range(N_DEV):
            dz = srcs[j // 2][:, (j % 2) * 256:(j % 2 + 1) * 256]
            if j < 2:
                dz = dz * (HEAD_DIM ** -0.5)
            dz = dz.astype(BF16)
            dz_ref[j] = dz
            dhn = dhn + lax.dot_general(dz, w_ref[j], NT, preferred_element_type=F32)
        g = g_ref[...]
        r, n = _rms(x_ref[...])
        dg_ref[...] += _colsum(dhn * n)
        dx_ref[...] = dh1_ref[...] + _rms_bwd(r, n, g, dhn)

    tok = lambda w: pl.BlockSpec((TM, w), lambda i: (i, 0))
    return pl.pallas_call(
        body, name="in_bwd", grid=(S // TM,),
        in_specs=[tok(512)] * 4 + [_const(w_in.shape), tok(D_MODEL), _const((1, D_MODEL)), tok(D_MODEL)],
        out_specs=[pl.BlockSpec((N_DEV, TM, 256), lambda i: (0, i, 0)), tok(D_MODEL),
                   pl.BlockSpec((1, D_MODEL), lambda i: (0, 0))],
        out_shape=[jax.ShapeDtypeStruct((N_DEV, S, 256), BF16), jax.ShapeDtypeStruct((S, D_MODEL), F32),
                   jax.ShapeDtypeStruct((1, D_MODEL), F32)],
        compiler_params=_params("arbitrary"),
    )(dq, dk, dv, du, w_in, x, g1, dh1)


def _adamw(name, parts, w, m, v):
    R, C = w.shape
    rb = R
    for cand in (256, 128, 64, 32, 16, 8):
        if R % cand == 0 and R > cand:
            rb = cand
            break

    def body(p_ref, w_ref, m_ref, v_ref, g_ref, d_ref, mo_ref, vo_ref):
        g = p_ref[0].astype(F32)
        for s in range(1, N_DEV):
            g = g + p_ref[s].astype(F32)
        g_ref[...] = g
        d_ref[...], mo_ref[...], vo_ref[...] = _adam_update(g, w_ref[...], m_ref[...], v_ref[...])

    blk = pl.BlockSpec((rb, C), lambda i: (i, 0))
    return pl.pallas_call(
        body, name=name, grid=(R // rb,),
        in_specs=[pl.BlockSpec((N_DEV, rb, C), lambda i: (0, i, 0)), blk, blk, blk],
        out_specs=[blk] * 4,
        out_shape=[jax.ShapeDtypeStruct((R, C), F32)] * 4,
        compiler_params=_params("arbitrary"),
    )(parts, w, m, v)


def _adam_update(g, w, m, v):
    m_new = ADAM_B1 * m + (1.0 - ADAM_B1) * g
    v_new = ADAM_B2 * v + (1.0 - ADAM_B2) * (g * g)
    m_hat = m_new / (1.0 - ADAM_B1 ** ADAM_STEP)
    v_hat = v_new / (1.0 - ADAM_B2 ** ADAM_STEP)
    return -ADAM_LR * (m_hat / (jnp.sqrt(v_hat) + ADAM_EPS) + ADAM_WD * w), m_new, v_new


def _adamw_small(parts, loss_parts, ws, ms, vs):
    n = len(ws)

    def body(*refs):
        p_refs, lp_ref = refs[:n], refs[n]
        w_refs, m_refs, v_refs = refs[n + 1:2 * n + 1], refs[2 * n + 1:3 * n + 1], refs[3 * n + 1:4 * n + 1]
        outs = refs[4 * n + 1:]
        for i in range(n):
            g = p_refs[i][0]
            for s in range(1, N_DEV):
                g = g + p_refs[i][s]
            d, m_new, v_new = _adam_update(g, w_refs[i][...], m_refs[i][...], v_refs[i][...])
            outs[i][...] = g
            outs[n + i][...] = d
            outs[2 * n + i][...] = m_new
            outs[3 * n + i][...] = v_new
        tot = lp_ref[0]
        for s in range(1, N_DEV):
            tot = tot + lp_ref[s]
        outs[4 * n][...] = tot

    shapes = [jax.ShapeDtypeStruct(w.shape, F32) for w in ws]
    res = pl.pallas_call(
        body, name="adamw_replicated",
        out_shape=shapes * 4 + [jax.ShapeDtypeStruct(loss_parts.shape[1:], F32)],
        compiler_params=_params(),
    )(*parts, loss_parts, *ws, *ms, *vs)
    return res[:n], res[n:2 * n], res[2 * n:3 * n], res[3 * n:4 * n], res[4 * n]


def _gather2(name, arrays):
    n = len(arrays)

    def body(*refs):
        first, passed, last = _gather2_copies(refs[:n], refs[n:2 * n], *refs[2 * n:])
        for cp in first:
            cp.start()
        for arrival, cp in passed:
            arrival.wait_recv()
            cp.start()
        for wait in last:
            wait()

    return pl.pallas_call(
        body, name=name,
        in_specs=[ANY_SPEC] * n, out_specs=[ANY_SPEC] * n, out_shape=_exchange_shapes(arrays, (False,) * n),
        scratch_shapes=_exchange_sems(n),
    )(*arrays)


def _dw_in_exchange(hn, dz, small):
    S = hn.shape[0]
    nt = S // TK
    ns = len(small)
    kd, nd = hn.shape[1], dz.shape[2]
    me_arr = (4 * lax.axis_index("x") + 2 * lax.axis_index("y") + lax.axis_index("c")).astype(jnp.int32).reshape(1)

    def body(me_ref, x_ref, dy_ref, *rest):
        recv_ref = rest[ns]
        acc, stage, send_sems, recv_sems, own_sem = rest[2 * ns + 1:2 * ns + 6]
        j, t = pl.program_id(0), pl.program_id(1)
        x, y, c = lax.axis_index("x"), lax.axis_index("y"), lax.axis_index("c")
        me = 4 * x + 2 * y + c
        small_copies = _exchange_copies(rest[:ns], rest[ns + 1:2 * ns + 1], (False,) * ns, *rest[2 * ns + 6:])

        @pl.when((j == 0) & (t == 0))
        def _():
            for cp in small_copies:
                cp.start()

        @pl.when(t == 0)
        def _():
            acc[...] = jnp.zeros(acc.shape, F32)

        acc[...] += lax.dot_general(x_ref[...], dy_ref[...], TN, preferred_element_type=F32)

        def to_owner(k, owner):
            return pltpu.make_async_remote_copy(
                src_ref=stage.at[owner], dst_ref=recv_ref.at[me], send_sem=send_sems.at[k], recv_sem=recv_sems.at[k],
                device_id=(owner // 4, (owner // 2) % 2, owner % 2), device_id_type=MESH)

        own = pltpu.make_async_copy(stage.at[me], recv_ref.at[me], own_sem)

        @pl.when(t == nt - 1)
        def _():
            owner = (me + 1 + j) % N_DEV
            stage[owner] = acc[...].astype(BF16)

            @pl.when(j < N_DEV - 1)
            def _():
                to_owner(j, owner).start()

            @pl.when(j == N_DEV - 1)
            def _():
                own.start()
                own.wait()
                for k in range(N_DEV - 1):
                    to_owner(k, me).wait_send()
                    to_owner(k, me).wait_recv()
                for cp in small_copies:
                    cp.wait()

    slab = lambda j, me_ref: (me_ref[0] + 1 + j) % N_DEV
    grid_spec = pltpu.PrefetchScalarGridSpec(
        num_scalar_prefetch=1, grid=(N_DEV, nt),
        in_specs=[pl.BlockSpec((TK, kd), lambda j, t, me_ref: (t, 0)),
                  pl.BlockSpec((None, TK, nd), lambda j, t, me_ref: (slab(j, me_ref), t, 0))] + [ANY_SPEC] * ns,
        out_specs=[ANY_SPEC] * (ns + 1),
        scratch_shapes=[pltpu.VMEM((kd, nd), F32), pltpu.VMEM((N_DEV, kd, nd), BF16),
                        pltpu.SemaphoreType.DMA((N_DEV - 1,)), pltpu.SemaphoreType.DMA((N_DEV - 1,)),
                        pltpu.SemaphoreType.DMA] + _exchange_sems(ns))
    res = pl.pallas_call(
        body, name="dw_in_exchange", grid_spec=grid_spec,
        out_shape=[jax.ShapeDtypeStruct((N_DEV, kd, nd), BF16)] + _exchange_shapes(small, (False,) * ns),
        compiler_params=_params("arbitrary", "arbitrary"),
    )(me_arr, hn, dz, *small)
    return res[0], res[1:]


def kernel(x, p, ln_mix, w_in, pool_w, pool_scale, w_out, ln_ffn, w_up, conv_w, conv_b, w_down, ln_ple, w_ple_gate, w_ple, ln_final, loss_target, m_ln_mix, m_w_in, m_pool_w, m_pool_scale, m_w_out, m_ln_ffn, m_w_up, m_conv_w, m_conv_b, m_w_down, m_ln_ple, m_w_ple_gate, m_w_ple, m_ln_final, v_ln_mix, v_w_in, v_pool_w, v_pool_scale, v_w_out, v_ln_ffn, v_w_up, v_conv_w, v_conv_b, v_w_down, v_ln_ple, v_w_ple_gate, v_w_ple, v_ln_final):
    xs, ps, tgt, pool_w0 = x[0], p[0, 0], loss_target[0], pool_w[0]
    slopes = jnp.exp2(-8.0 * (jnp.arange(N_HEADS, dtype=F32) + 1.0) / N_HEADS)
    conv_b_s = conv_b.reshape(N_DEV, 1, FF_SHARD)

    (w_in_g,) = _gather2("gather_w_in", [w_in[0].astype(BF16)])
    (q, k, v, u, hn1), (w_out_g,) = _qkvu(xs, ln_mix, w_in_g, [w_out[0].astype(BF16)])
    att, lse, (w_up_g, conv_w_g) = _attn_fwd(slopes, q, k, v, [w_up[0].astype(BF16), conv_w[0]])
    w_out_f = w_out_g.reshape(D_MODEL, D_MODEL)
    h1, mix, dlt, (w_down_g,) = _mix_out(xs, att, u, pool_w0, pool_scale, w_out_f, [w_down[0].astype(BF16)])
    w_down_f = w_down_g.reshape(4, FF_SHARD, D_MODEL)
    h2, hn2, up, upc, (w_pg_g, w_ple_g) = _ffn_fwd(h1, ln_ffn, w_up_g, conv_w_g, conv_b_s, w_down_f,
                                                   [w_ple_gate[0].astype(BF16), w_ple[0].astype(BF16)])
    w_pg_f = w_pg_g.reshape(D_MODEL, D_MODEL)
    w_ple_f = jnp.transpose(w_ple_g, (1, 0, 2)).reshape(PLE_DIM, D_MODEL)
    loss_blk, dh2, dh2b, d_w_pg, d_w_ple, d_ln_ple, d_ln_final = _head(
        h2, ps, ln_ple, w_pg_f, w_ple_f, ln_final.reshape(1, D_MODEL), tgt)

    d_w_pg = d_w_pg.reshape(N_DEV, D_MODEL // N_DEV, D_MODEL)
    d_w_ple = jnp.transpose(d_w_ple.reshape(PLE_DIM, N_DEV, LANES), (1, 0, 2))
    d_w_down, dup, d_conv_w, d_conv_b, (r_w_pg, r_w_ple) = _ffn_bwd_a(dh2b, up, upc, w_down_f, [d_w_pg, d_w_ple])
    d_w_down = d_w_down.reshape(N_DEV, D_FF // N_DEV, D_MODEL)
    dpre, dh1, dh1b, d_ln_ffn, (r_conv_w, r_w_down) = _ffn_bwd_b(
        dup, conv_w_g, w_up_g, h1, ln_ffn, dh2, [d_conv_w, d_w_down])
    datt, du, d_pool_w, d_pool_scale, d_w_out = _mix_bwd(dh1b, w_out_f, dlt, mix, pool_w0, pool_scale)
    d_w_out = d_w_out.reshape(N_DEV, D_MODEL // N_DEV, D_MODEL)
    d_w_up = _wgrad("dw_up", dpre, hn2, "lead", "full", N_DEV, FF_SHARD, D_MODEL)
    rep_late = [d_pool_w, d_pool_scale, d_ln_ffn, d_conv_b.reshape(1, 2 * D_FF), d_ln_ple, d_ln_final, loss_blk]
    dq, dk, dv, received = _attn_bwd(slopes, q, k, v, att, lse, datt, [d_w_out, d_w_up] + rep_late,
                                     (True, True) + (False,) * len(rep_late))
    r_w_out, r_w_up, r_rep = received[0], received[1], list(received[2:])
    dz, grad_x, d_ln_mix = _in_bwd(dq, dk, dv, du, w_in_g, xs, ln_mix, dh1)

    rep_names = ("ln_mix", "pool_w", "pool_scale", "ln_ffn", "conv_b", "ln_ple", "ln_final")
    rep_w = [ln_mix, pool_w0, pool_scale, ln_ffn, conv_b, ln_ple, ln_final.reshape(1, D_MODEL)]
    rep_m = [m_ln_mix, m_pool_w[0], m_pool_scale, m_ln_ffn, m_conv_b, m_ln_ple, m_ln_final.reshape(1, D_MODEL)]
    rep_v = [v_ln_mix, v_pool_w[0], v_pool_scale, v_ln_ffn, v_conv_b, v_ln_ple, v_ln_final.reshape(1, D_MODEL)]
    r_w_in, (r_ln_mix,) = _dw_in_exchange(hn1, dz, [d_ln_mix])
    small = _adamw_small([r_ln_mix] + r_rep[:-1], r_rep[-1], rep_w, rep_m, rep_v)
    loss = small[4][0, 0]

    sharded = {}
    sharded["w_in"] = _adamw("adamw_w_in", r_w_in, w_in[0], m_w_in[0], v_w_in[0])
    sharded["w_out"] = _adamw("adamw_w_out", r_w_out, w_out[0], m_w_out[0], v_w_out[0])
    sharded["w_up"] = [t.T for t in _adamw("adamw_w_up", r_w_up, w_up[0].T, m_w_up[0].T, v_w_up[0].T)]
    sharded["conv_w"] = _adamw("adamw_conv_w", r_conv_w, conv_w[0], m_conv_w[0], v_conv_w[0])
    sharded["w_down"] = _adamw("adamw_w_down", r_w_down, w_down[0], m_w_down[0], v_w_down[0])
    sharded["w_ple_gate"] = _adamw("adamw_w_ple_gate", r_w_pg, w_ple_gate[0], m_w_ple_gate[0], v_w_ple_gate[0])
    sharded["w_ple"] = _adamw("adamw_w_ple", r_w_ple, w_ple[0], m_w_ple[0], v_w_ple[0])

    shapes = dict(w_in=w_in, w_out=w_out, w_up=w_up, conv_w=conv_w, w_down=w_down, w_ple_gate=w_ple_gate, w_ple=w_ple,
                  ln_mix=ln_mix, pool_w=pool_w, pool_scale=pool_scale, ln_ffn=ln_ffn, conv_b=conv_b, ln_ple=ln_ple,
                  ln_final=ln_final)

    def leaf(kind, n):
        src = sharded[n][kind] if n in sharded else small[kind][rep_names.index(n)]
        return src.reshape(shapes[n].shape)

    order = ("ln_mix", "w_in", "pool_w", "pool_scale", "w_out", "ln_ffn", "w_up", "conv_w", "conv_b", "w_down", "ln_ple",
             "w_ple_gate", "w_ple", "ln_final")
    outs = [loss, grad_x[None]]
    for kind in range(4):
        outs += [leaf(kind, n) for n in order]
    return tuple(outs)
```

```python
import jax
import jax.numpy as jnp
from jax import lax
from jax.experimental import pallas as pl
from jax.experimental.pallas import tpu as pltpu

F32 = jnp.float32
BF16 = jnp.bfloat16

N_DEV = 8
D_MODEL = 1024
ATT_WIDTH = 512
POOL_WIDTH = 512
N_HEADS = 8
HEAD_DIM = 64
SPAN = 128
DILATIONS = (1, 4, 16)
POOL_WINDOWS = (2, 4, 8, 16)
POOL_GROUP = 128
D_FF = 2816
FF_SHARD = 2 * D_FF // N_DEV
PLE_DIM = 256
EPS = 1e-6
NEG = -1e30

ADAM_LR = 0.001
ADAM_B1 = 0.9
ADAM_B2 = 0.999
ADAM_EPS = 1e-08
ADAM_WD = 0.01
ADAM_STEP = 10

LANES = 128
HALO = 16
TM = 512
TM_FF = 256
TK = 4096
W_DOWN_CUT = 112
ATTN_GROUP_FWD = 16
ATTN_GROUP_BWD = 8
VMEM_LIMIT = 56 * 1024 * 1024

MESH = pl.DeviceIdType.MESH
NT = (((1,), (1,)), ((), ()))
TN = (((0,), (0,)), ((), ()))


def _params(*sem):
    return pltpu.CompilerParams(dimension_semantics=sem or None, vmem_limit_bytes=VMEM_LIMIT)


def _const(shape):
    n = len(shape)
    return pl.BlockSpec(shape, lambda *_: (0,) * n, pipeline_mode=pl.Buffered(1))


def _rms(h):
    r = lax.rsqrt(jnp.mean(h * h, axis=-1, keepdims=True) + EPS)
    return r, h * r


def _rms_bwd(r, n, g, dhn):
    dn = dhn * g
    return r * (dn - n * jnp.mean(dn * n, axis=-1, keepdims=True))


def _colsum(a):
    return jnp.sum(a, axis=0, keepdims=True)


def _gather2_copies(ins, outs, send_sems, recv_sems, local_sems):
    n = len(ins)
    x, y, c = lax.axis_index("x"), lax.axis_index("y"), lax.axis_index("c")
    slot = lambda px, py, pc: 4 * px + 2 * py + pc
    chips = [(x, 1 - y), (1 - x, y), (1 - x, 1 - y)]
    first, passed, last = [], [], []

    def remote(a, r, src, dst_slot, to):
        return pltpu.make_async_remote_copy(
            src_ref=src, dst_ref=outs[a].at[dst_slot],
            send_sem=send_sems.at[a * (N_DEV - 1) + r], recv_sem=recv_sems.at[a * (N_DEV - 1) + r],
            device_id=to, device_id_type=MESH)

    for a in range(n):
        mine = pltpu.make_async_copy(ins[a], outs[a].at[slot(x, y, c)], local_sems.at[a])
        to_sibling = remote(a, 0, ins[a], slot(x, y, c), (x, y, 1 - c))
        first += [mine, to_sibling]
        last += [mine.wait, to_sibling.wait_send, to_sibling.wait_recv]
        for r, (px, py) in enumerate(chips, start=1):
            to_chip = remote(a, r, ins[a], slot(x, y, c), (px, py, c))
            onward = remote(a, 3 + r, outs[a].at[slot(px, py, c)], slot(px, py, c), (x, y, 1 - c))
            first.append(to_chip)
            passed.append((to_chip, onward))
            last += [to_chip.wait_send, onward.wait_send, onward.wait_recv]
    return first, passed, last


def _gather2_begin(plan, step, pass_step):
    first, passed, _ = plan

    @pl.when(step == 0)
    def _():
        for cp in first:
            cp.start()

    @pl.when(step == pass_step)
    def _():
        for arrival, cp in passed:
            arrival.wait_recv()
            cp.start()


def _gather2_end(plan, step, nsteps):
    @pl.when(step == nsteps - 1)
    def _():
        for wait in plan[2]:
            wait()


ANY_SPEC = pl.BlockSpec(memory_space=pl.ANY)


def _exchange_shapes(arrays, scatter):
    out = []
    for a, s in zip(arrays, scatter):
        slab = a.shape[1:] if s else a.shape
        out.append(jax.ShapeDtypeStruct((N_DEV,) + tuple(slab), a.dtype))
    return out


def _exchange_sems(n):
    return [pltpu.SemaphoreType.DMA((n * (N_DEV - 1),)), pltpu.SemaphoreType.DMA((n * (N_DEV - 1),)),
            pltpu.SemaphoreType.DMA((n,))]


def _exchange_copies(ins, outs, scatter, send_sems, recv_sems, local_sems):
    n = len(ins)
    x, y, c = lax.axis_index("x"), lax.axis_index("y"), lax.axis_index("c")
    me = 4 * x + 2 * y + c
    copies = []
    for a in range(n):
        src = ins[a].at[me] if scatter[a] else ins[a]
        copies.append(pltpu.make_async_copy(src, outs[a].at[me], local_sems.at[a]))
    for k in range(1, N_DEV):
        px = 1 - x if k & 4 else x
        py = 1 - y if k & 2 else y
        pc = 1 - c if k & 1 else c
        pid = 4 * px + 2 * py + pc
        for a in range(n):
            src = ins[a].at[pid] if scatter[a] else ins[a]
            copies.append(pltpu.make_async_remote_copy(
                src_ref=src, dst_ref=outs[a].at[me],
                send_sem=send_sems.at[a * (N_DEV - 1) + k - 1], recv_sem=recv_sems.at[a * (N_DEV - 1) + k - 1],
                device_id=(px, py, pc), device_id_type=MESH))
    return copies


def _qkvu(x, g1, w_in, shards):
    S = x.shape[0]
    ns = len(shards)
    nsteps = S // TM

    def body(x_ref, g_ref, w_ref, *rest):
        q_ref, k_ref, v_ref, u_ref, hn_ref = rest[ns:ns + 5]
        plan = _gather2_copies(rest[:ns], rest[ns + 5:2 * ns + 5], *rest[2 * ns + 5:])
        _gather2_begin(plan, pl.program_id(0), nsteps - 2)
        r, n = _rms(x_ref[...])
        hn = (n * g_ref[...]).astype(BF16)
        hn_ref[...] = hn
        outs = (q_ref, k_ref, v_ref, u_ref)
        for j in range(N_DEV):
            z = jnp.dot(hn, w_ref[j], preferred_element_type=F32)
            if j < 2:
                z = z * (HEAD_DIM ** -0.5)
            outs[j // 2][:, (j % 2) * 256:(j % 2 + 1) * 256] = z
        _gather2_end(plan, pl.program_id(0), nsteps)

    tok = lambda w: pl.BlockSpec((TM, w), lambda i: (i, 0))
    res = pl.pallas_call(
        body, name="qkvu", grid=(nsteps,),
        in_specs=[tok(D_MODEL), _const((1, D_MODEL)), _const(w_in.shape)] + [ANY_SPEC] * ns,
        out_specs=[tok(512)] * 4 + [tok(D_MODEL)] + [ANY_SPEC] * ns,
        out_shape=[jax.ShapeDtypeStruct((S, 512), F32)] * 4 + [jax.ShapeDtypeStruct((S, D_MODEL), BF16)]
        + _exchange_shapes(shards, (False,) * ns),
        scratch_shapes=_exchange_sems(ns),
        compiler_params=_params("arbitrary"),
    )(x, g1, w_in, *shards)
    return res[:5], res[5:]


def _attn_fill_bias(bias_s, slope_ref, hp, d):
    qi = lax.broadcasted_iota(jnp.int32, (SPAN, 2 * SPAN), 0)
    kj = lax.broadcasted_iota(jnp.int32, (SPAN, 2 * SPAN), 1)
    for t, diff in enumerate((qi + SPAN - kj, qi - kj)):
        valid = (diff >= 0) & (diff <= SPAN)
        dist = diff.astype(F32) * float(d)
        for h in range(2):
            bias_s[t, h * SPAN:(h + 1) * SPAN, :] = jnp.where(valid, -slope_ref[2 * hp + h] * dist, NEG)


def _stack_heads(x, is0):
    return jnp.concatenate([jnp.where(is0, x, 0.0), jnp.where(is0, 0.0, x)], axis=0)


def _unstack_heads(y, is0):
    return jnp.where(is0, y[0:SPAN], y[SPAN:2 * SPAN])


def _attn_block(i, g, d, nb, group):
    gr = min(d, group)
    gn = group // gr
    per = d // gr
    r = (i & (per - 1)) * gr + g % gr
    n = (i >> (per.bit_length() - 1)) + (g // gr) * (nb // gn)
    k0 = jnp.maximum(n - 1, 0)

    def ds(block, nrows):
        start = block * (SPAN * d) + r
        return pl.ds(start, nrows, stride=d) if d > 1 else pl.ds(start, nrows)

    return ds(n, SPAN), ds(k0, 2 * SPAN), jnp.where(n == 0, 1, 0)


def _attn_groups(S, d, group, writes_key_rows=False):
    nb = S // d // SPAN
    gn = group // min(d, group)
    assert nb >= 2 and nb % gn == 0 and (gn == 1 or nb // gn >= (3 if writes_key_rows else 2))
    return nb, d * nb // group


def _attn_fwd(slopes, q, k, v, shards):
    S = q.shape[0]
    ns = len(shards)
    steps = ATT_WIDTH // LANES

    def body(slope_ref, q_ref, k_ref, v_ref, *rest):
        o_ref, lse_ref = rest[ns:ns + 2]
        m_s, l_s, bias_s = rest[2 * ns + 2:2 * ns + 5]
        hp = pl.program_id(0)
        plan = _gather2_copies(rest[:ns], rest[ns + 2:2 * ns + 2], *rest[2 * ns + 5:])
        _gather2_begin(plan, hp, steps - 1)

        is0 = lax.broadcasted_iota(jnp.int32, (SPAN, LANES), 1) < HEAD_DIM
        for pi, d in enumerate(DILATIONS):
            nb, ngroups = _attn_groups(S, d, ATTN_GROUP_FWD)
            _attn_fill_bias(bias_s, slope_ref, hp, d)

            def group(i, carry, d=d, pi=pi, nb=nb):
                blocks = [_attn_block(i, g, d, nb, ATTN_GROUP_FWD) for g in range(ATTN_GROUP_FWD)]
                loaded = [(q_ref[rows, :], k_ref[krows, :].astype(BF16), v_ref[krows, :].astype(BF16))
                          for rows, krows, _ in blocks]
                new = []
                for (rows, krows, tab), (qb, kb, vb) in zip(blocks, loaded):
                    qs = _stack_heads(qb, is0).astype(BF16)
                    s = lax.dot_general(qs, kb, NT, preferred_element_type=F32) + bias_s[tab]
                    m = jnp.max(s, axis=-1, keepdims=True)
                    e = jnp.exp(s - m)
                    l = jnp.sum(e, axis=-1, keepdims=True)
                    pv = jnp.dot(e.astype(BF16), vb, preferred_element_type=F32)
                    new.append([_unstack_heads(jnp.broadcast_to(m, pv.shape), is0),
                                _unstack_heads(jnp.broadcast_to(l, pv.shape), is0), _unstack_heads(pv, is0)])
                if pi > 0:
                    old = [(m_s[rows, :], l_s[rows, :], o_ref[rows, :]) for rows, _, _ in blocks]
                    for st, (m_o, l_o, o_o) in zip(new, old):
                        m_n = jnp.maximum(m_o, st[0])
                        a_o = jnp.exp(m_o - m_n)
                        a_b = jnp.exp(st[0] - m_n)
                        st[:] = [m_n, a_o * l_o + a_b * st[1], a_o * o_o + a_b * st[2]]
                for (rows, _, _), (m_b, l_b, acc) in zip(blocks, new):
                    if pi == len(DILATIONS) - 1:
                        o_ref[rows, :] = acc / l_b
                        lse_ref[rows, :] = m_b + jnp.log(l_b)
                    else:
                        o_ref[rows, :] = acc
                        m_s[rows, :] = m_b
                        l_s[rows, :] = l_b
                return carry

            lax.fori_loop(0, ngroups, group, 0)

        _gather2_end(plan, hp, steps)

    col = pl.BlockSpec((S, LANES), lambda i: (0, i))
    res = pl.pallas_call(
        body, name="attn_fwd", grid=(steps,),
        in_specs=[pl.BlockSpec(memory_space=pltpu.SMEM), col, col, col] + [ANY_SPEC] * ns,
        out_specs=[col, col] + [ANY_SPEC] * ns,
        out_shape=[jax.ShapeDtypeStruct((S, ATT_WIDTH), F32)] * 2 + _exchange_shapes(shards, (False,) * ns),
        scratch_shapes=[pltpu.VMEM((S, LANES), F32), pltpu.VMEM((S, LANES), F32),
                        pltpu.VMEM((2, 2 * SPAN, 2 * SPAN), F32)] + _exchange_sems(ns),
        compiler_params=_params("arbitrary"),
    )(slopes, q, k, v, *shards)
    return res[0], res[1], res[2:]


def _pool_count(i, w):
    t = i * TM + lax.broadcasted_iota(jnp.int32, (TM, 1), 0)
    return jnp.minimum(t + 1, w).astype(F32)


def _mix_out(x, att, u, pool_w, pool_scale, w_out, shards):
    S = x.shape[0]
    ns = len(shards)
    nsteps = S // TM

    def body(x_ref, att_ref, u_ref, pw_ref, ps_ref, w_ref, *rest):
        h1_ref, mix_ref, dlt_ref = rest[ns:ns + 3]
        ubuf = rest[2 * ns + 3]
        i = pl.program_id(0)
        plan = _gather2_copies(rest[:ns], rest[ns + 3:2 * ns + 3], *rest[2 * ns + 4:])
        _gather2_begin(plan, i, nsteps - 1)

        @pl.when(i == 0)
        def _():
            ubuf[0:HALO, :] = jnp.zeros((HALO, POOL_WIDTH), F32)

        ubuf[HALO:HALO + TM, :] = u_ref[...]
        mix_ref[:, 0:ATT_WIDTH] = att_ref[...].astype(BF16)
        for g, w in enumerate(POOL_WINDOWS):
            cols = slice(g * POOL_GROUP, (g + 1) * POOL_GROUP)
            ug = ubuf[HALO:HALO + TM, cols]
            acc = ug
            for j in range(1, w):
                acc = acc + ubuf[HALO - j:HALO - j + TM, cols]
            dlt = (acc / _pool_count(i, w) - ug).astype(BF16)
            dlt_ref[:, cols] = dlt
            yg = jnp.dot(dlt, pw_ref[g].astype(BF16), preferred_element_type=F32) * ps_ref[:, cols]
            mix_ref[:, ATT_WIDTH + g * POOL_GROUP:ATT_WIDTH + (g + 1) * POOL_GROUP] = yg.astype(BF16)
        ubuf[0:HALO, :] = ubuf[TM:TM + HALO, :]
        h1_ref[...] = x_ref[...] + jnp.dot(mix_ref[...], w_ref[...], preferred_element_type=F32)
        _gather2_end(plan, i, nsteps)

    tok = lambda w: pl.BlockSpec((TM, w), lambda i: (i, 0))
    res = pl.pallas_call(
        body, name="mix_out", grid=(nsteps,),
        in_specs=[tok(D_MODEL), tok(ATT_WIDTH), tok(POOL_WIDTH), _const(pool_w.shape), _const((1, POOL_WIDTH)),
                  _const(w_out.shape)] + [ANY_SPEC] * ns,
        out_specs=[tok(D_MODEL), tok(D_MODEL), tok(POOL_WIDTH)] + [ANY_SPEC] * ns,
        out_shape=[jax.ShapeDtypeStruct((S, D_MODEL), F32), jax.ShapeDtypeStruct((S, D_MODEL), BF16),
                   jax.ShapeDtypeStruct((S, POOL_WIDTH), BF16)] + _exchange_shapes(shards, (False,) * ns),
        scratch_shapes=[pltpu.VMEM((TM + HALO, POOL_WIDTH), F32)] + _exchange_sems(ns),
        compiler_params=_params("arbitrary"),
    )(x, att, u, pool_w, pool_scale, w_out, *shards)
    return res[0], res[1], res[2], res[3:]


def _conv_fwd(stage, upre, prev, cw, cb):
    T = upre.shape[0]
    stage[0:HALO, :] = prev
    stage[HALO:HALO + T, :] = upre
    return cb + cw[0:1, :] * stage[HALO - 2:HALO - 2 + T, :] + cw[1:2, :] * stage[HALO - 1:HALO - 1 + T, :] + cw[2:3, :] * upre


def _ffn_fwd(h1, g2, w_up, conv_w, conv_b, w_down, shards):
    S = h1.shape[0]
    T = TM_FF
    ns = len(shards)
    nsteps = S // T

    def body(h1_ref, g_ref, wu_ref, cw_ref, cb_ref, wd_ref, *rest):
        h2_ref, hn_ref, up_ref, upc_ref = rest[ns:ns + 4]
        carry, stage = rest[2 * ns + 4:2 * ns + 6]
        i = pl.program_id(0)
        plan = _gather2_copies(rest[:ns], rest[ns + 4:2 * ns + 4], *rest[2 * ns + 6:])
        _gather2_begin(plan, i, nsteps // 2)

        @pl.when(i == 0)
        def _():
            carry[...] = jnp.zeros(carry.shape, F32)

        h1t = h1_ref[...]
        r, n = _rms(h1t)
        hn = (n * g_ref[...]).astype(BF16)
        hn_ref[...] = hn
        acc = h1t
        for j in range(4):
            conv = []
            for jj in (j, j + 4):
                upre = jnp.dot(hn, wu_ref[jj], preferred_element_type=F32)
                up_ref[jj] = upre.astype(BF16)
                conv.append(_conv_fwd(stage, upre, carry[jj], cw_ref[jj], cb_ref[jj]))
                upc_ref[jj] = conv[-1].astype(BF16)
                carry[jj] = stage[T:T + HALO, :]
            gate, val = conv
            a = gate * jax.nn.sigmoid(gate) * val
            acc = acc + jnp.dot(a.astype(BF16), wd_ref[j], preferred_element_type=F32)
        h2_ref[...] = acc
        _gather2_end(plan, i, nsteps)

    tok = lambda w: pl.BlockSpec((T, w), lambda i: (i, 0))
    res = pl.pallas_call(
        body, name="ffn_fwd", grid=(nsteps,),
        in_specs=[tok(D_MODEL), _const((1, D_MODEL)), _const(w_up.shape), _const(conv_w.shape), _const(conv_b.shape),
                  _const(w_down.shape)] + [ANY_SPEC] * ns,
        out_specs=[tok(D_MODEL), tok(D_MODEL)] + [pl.BlockSpec((N_DEV, T, FF_SHARD), lambda i: (0, i, 0))] * 2
        + [ANY_SPEC] * ns,
        out_shape=[jax.ShapeDtypeStruct((S, D_MODEL), F32), jax.ShapeDtypeStruct((S, D_MODEL), BF16)]
        + [jax.ShapeDtypeStruct((N_DEV, S, FF_SHARD), BF16)] * 2 + _exchange_shapes(shards, (False,) * ns),
        scratch_shapes=[pltpu.VMEM((N_DEV, HALO, FF_SHARD), F32), pltpu.VMEM((T + HALO, FF_SHARD), F32)]
        + _exchange_sems(ns),
        compiler_params=_params("arbitrary"),
    )(h1, g2, w_up, conv_w, conv_b, w_down, *shards)
    return res[0], res[1], res[2], res[3], res[4:]


def _head(h2, p, g3, w_pg, w_ple, g4, target):
    S = h2.shape[0]
    nt = S // TM

    def body(h2_ref, p_ref, g3_ref, wpg_ref, wple_ref, g4_ref, t_ref,
             loss_ref, dh2_ref, dh2b_ref, dwpg_ref, dwple_ref, dg3_ref, dg4_ref, lacc, pg_acc, ple_acc):
        i = pl.program_id(0)

        @pl.when(i == 0)
        def _():
            lacc[...] = jnp.zeros(lacc.shape, F32)
            pg_acc[...] = jnp.zeros(pg_acc.shape, F32)
            ple_acc[...] = jnp.zeros(ple_acc.shape, F32)
            dg3_ref[...] = jnp.zeros(dg3_ref.shape, F32)
            dg4_ref[...] = jnp.zeros(dg4_ref.shape, F32)

        h2t = h2_ref[...]
        g3, g4 = g3_ref[...], g4_ref[...]
        r3, n3 = _rms(h2t)
        hn3 = (n3 * g3).astype(BF16)
        pb = p_ref[...].astype(BF16)
        gs = jax.nn.sigmoid(jnp.dot(hn3, wpg_ref[...], preferred_element_type=F32))
        pe = jnp.dot(pb, wple_ref[...], preferred_element_type=F32)
        h3 = h2t + gs * pe
        r4, n4 = _rms(h3)
        err = n4 * g4 - t_ref[...]
        lacc[...] += _colsum(err * err)
        dy = err * (1.0 / D_MODEL)
        dg4_ref[...] += _colsum(dy * n4)
        dh3 = _rms_bwd(r4, n4, g4, dy)
        dpe = (dh3 * gs).astype(BF16)
        dgl = (dh3 * pe * gs * (1.0 - gs)).astype(BF16)
        ple_acc[...] += lax.dot_general(pb, dpe, TN, preferred_element_type=F32)
        pg_acc[...] += lax.dot_general(hn3, dgl, TN, preferred_element_type=F32)
        dhn3 = lax.dot_general(dgl, wpg_ref[...], NT, preferred_element_type=F32)
        dg3_ref[...] += _colsum(dhn3 * n3)
        dh2 = dh3 + _rms_bwd(r3, n3, g3, dhn3)
        dh2_ref[...] = dh2
        dh2b_ref[...] = dh2.astype(BF16)

        @pl.when(i == nt - 1)
        def _():
            tot = 0.5 / D_MODEL * jnp.sum(lacc[...], axis=-1, keepdims=True)
            loss_ref[...] = jnp.broadcast_to(tot, loss_ref.shape)
            dwpg_ref[...] = pg_acc[...].astype(BF16)
            dwple_ref[...] = ple_acc[...].astype(BF16)

    tok = lambda w: pl.BlockSpec((TM, w), lambda i: (i, 0))
    row = pl.BlockSpec((1, D_MODEL), lambda i: (0, 0))
    act = lambda dt: jax.ShapeDtypeStruct((S, D_MODEL), dt)
    whole = lambda r: pl.BlockSpec((r, D_MODEL), lambda i: (0, 0))
    return pl.pallas_call(
        body, name="head", grid=(nt,),
        in_specs=[tok(D_MODEL), tok(PLE_DIM), _const((1, D_MODEL)), _const(w_pg.shape), _const(w_ple.shape),
                  _const((1, D_MODEL)), tok(D_MODEL)],
        out_specs=[pl.BlockSpec((8, LANES), lambda i: (0, 0)), tok(D_MODEL), tok(D_MODEL), whole(D_MODEL),
                   whole(PLE_DIM), row, row],
        out_shape=[jax.ShapeDtypeStruct((8, LANES), F32), act(F32), act(BF16),
                   jax.ShapeDtypeStruct((D_MODEL, D_MODEL), BF16), jax.ShapeDtypeStruct((PLE_DIM, D_MODEL), BF16),
                   jax.ShapeDtypeStruct((1, D_MODEL), F32), jax.ShapeDtypeStruct((1, D_MODEL), F32)],
        scratch_shapes=[pltpu.VMEM((1, D_MODEL), F32), pltpu.VMEM((D_MODEL, D_MODEL), F32),
                        pltpu.VMEM((PLE_DIM, D_MODEL), F32)],
        compiler_params=_params("arbitrary"),
    )(h2, p, g3, w_pg, w_ple, g4, target)


def _wgrad(name, x, dy, x_kind, dy_kind, nj, k_dim, n_dim, tk=TK):
    S = x.shape[-2]
    nt = S // tk

    def spec(kind, width):
        if kind == "full":
            return pl.BlockSpec((tk, width), lambda j, t: (t, 0))
        return pl.BlockSpec((None, tk, width), lambda j, t: (j, t, 0))

    def body(x_ref, dy_ref, o_ref, acc):
        t = pl.program_id(1)

        @pl.when(t == 0)
        def _():
            acc[...] = jnp.zeros(acc.shape, F32)

        acc[...] += lax.dot_general(x_ref[...].astype(BF16), dy_ref[...], TN, preferred_element_type=F32)

        @pl.when(t == nt - 1)
        def _():
            o_ref[...] = acc[...].astype(BF16)

    return pl.pallas_call(
        body, name=name, grid=(nj, nt),
        in_specs=[spec(x_kind, k_dim), spec(dy_kind, n_dim)],
        out_specs=pl.BlockSpec((None, k_dim, n_dim), lambda j, t: (j, 0, 0)),
        out_shape=jax.ShapeDtypeStruct((nj, k_dim, n_dim), BF16),
        scratch_shapes=[pltpu.VMEM((k_dim, n_dim), F32)],
        compiler_params=_params("arbitrary", "arbitrary"),
    )(x, dy)


def _row_picker(T, off0, off1):
    r = lax.broadcasted_iota(jnp.int32, (2 * T, T + HALO), 0)
    c = lax.broadcasted_iota(jnp.int32, (2 * T, T + HALO), 1)
    want = jnp.where(r < T, r + off0, r - T + off1)
    return jnp.where(c == want, 1.0, 0.0).astype(BF16)


def _ffn_bwd_a(dh2b, up, upc, w_down, grads):
    S = dh2b.shape[0]
    T = TM_FF
    hb = T // HALO
    nsteps = S // T
    ng = len(grads)

    def body(dh_ref, up_ref, halo_ref, upc_ref, wd_ref, *rest):
        dwd_ref, dup_ref, dcw_ref, dcb_ref = rest[ng:ng + 4]
        stage, dwd_acc = rest[2 * ng + 4:2 * ng + 6]
        i = pl.program_id(0)
        copies = _exchange_copies(rest[:ng], rest[ng + 4:2 * ng + 4], (True,) * ng, *rest[2 * ng + 6:])

        @pl.when(i == 0)
        def _():
            dcw_ref[...] = jnp.zeros(dcw_ref.shape, F32)
            dcb_ref[...] = jnp.zeros(dcb_ref.shape, F32)
            dwd_acc[...] = jnp.zeros(dwd_acc.shape, F32)
            for cp in copies:
                cp.start()

        @pl.when(i == nsteps - 1)
        def _():
            for cp in copies:
                cp.wait()

        dh = dh_ref[...]
        pick = _row_picker(T, HALO - 2, HALO - 1)
        for j in range(4):
            da = lax.dot_general(dh, wd_ref[j], NT, preferred_element_type=F32)
            taps = []
            for jj in (j, j + 4):
                upre = up_ref[jj]
                stage[0:HALO, :] = jnp.where(i > 0, halo_ref[jj], jnp.zeros((HALO, FF_SHARD), BF16))
                stage[HALO:HALO + T, :] = upre
                prv = jnp.dot(pick, stage[...], preferred_element_type=F32)
                taps.append((prv[0:T], prv[T:2 * T], upre.astype(F32)))
            gate, val = upc_ref[j].astype(F32), upc_ref[j + 4].astype(F32)
            sg = jax.nn.sigmoid(gate)
            silu = gate * sg
            dwd_acc[j] += lax.dot_general((silu * val).astype(BF16), dh, TN, preferred_element_type=F32)
            dgate = (da * val) * (sg + silu * (1.0 - sg))
            dval = da * silu
            for jj, dup, tp in ((j, dgate, taps[0]), (j + 4, dval, taps[1])):
                dup_ref[jj] = dup.astype(BF16)
                dcb_ref[jj] += _colsum(dup)
                for kk in range(3):
                    dcw_ref[jj, kk:kk + 1, :] += _colsum(dup * tp[kk])

        @pl.when(i == nsteps - 1)
        def _():
            dwd_ref[...] = dwd_acc[...].astype(BF16)

    tok = lambda w: pl.BlockSpec((T, w), lambda i: (i, 0))
    shard = pl.BlockSpec((N_DEV, T, FF_SHARD), lambda i: (0, i, 0))
    res = pl.pallas_call(
        body, name="ffn_bwd_a", grid=(nsteps,),
        in_specs=[tok(D_MODEL), shard,
                  pl.BlockSpec((N_DEV, HALO, FF_SHARD), lambda i: (0, jnp.maximum(i * hb - 1, 0), 0)),
                  shard, _const(w_down.shape)] + [ANY_SPEC] * ng,
        out_specs=[_const(w_down.shape), shard,
                   pl.BlockSpec((N_DEV, 3, FF_SHARD), lambda i: (0, 0, 0)),
                   pl.BlockSpec((N_DEV, 1, FF_SHARD), lambda i: (0, 0, 0))] + [ANY_SPEC] * ng,
        out_shape=[jax.ShapeDtypeStruct(w_down.shape, BF16), jax.ShapeDtypeStruct((N_DEV, S, FF_SHARD), BF16),
                   jax.ShapeDtypeStruct((N_DEV, 3, FF_SHARD), F32), jax.ShapeDtypeStruct((N_DEV, 1, FF_SHARD), F32)]
        + _exchange_shapes(grads, (True,) * ng),
        scratch_shapes=[pltpu.VMEM((T + HALO, FF_SHARD), BF16), pltpu.VMEM(w_down.shape, F32)] + _exchange_sems(ng),
        compiler_params=_params("arbitrary"),
    )(dh2b, up, up, upc, w_down, *grads)
    return res[0], res[1], res[2], res[3], res[4:]


def _ffn_bwd_b(dup, conv_w, w_up, h1, g2, dh2, grads):
    S = h1.shape[0]
    T = TM_FF
    hb = T // HALO
    nt = S // T
    ng = len(grads)

    def body(dup_ref, halo_ref, cw_ref, wu_ref, h1_ref, g_ref, dh2_ref, *rest):
        dpre_ref, dh1_ref, dh1b_ref, dg_ref = rest[ng:ng + 4]
        stage = rest[2 * ng + 4]
        i = pl.program_id(0)
        copies = _exchange_copies(rest[:ng], rest[ng + 4:2 * ng + 4], (True,) * ng, *rest[2 * ng + 5:])

        @pl.when(i == 0)
        def _():
            dg_ref[...] = jnp.zeros(dg_ref.shape, F32)
            for cp in copies:
                cp.start()

        dhn = jnp.zeros((T, D_MODEL), F32)
        for jj in range(N_DEV):
            dup = dup_ref[jj].astype(F32)
            stage[0:T, :] = dup
            stage[T:T + HALO, :] = jnp.where(i < nt - 1, halo_ref[jj].astype(F32), 0.0)
            cw = cw_ref[jj]
            dpre = (cw[2:3, :] * dup + cw[1:2, :] * stage[1:1 + T, :] + cw[0:1, :] * stage[2:2 + T, :]).astype(BF16)
            dpre_ref[jj] = dpre
            dhn = dhn + lax.dot_general(dpre, wu_ref[jj], NT, preferred_element_type=F32)
        g = g_ref[...]
        r, n = _rms(h1_ref[...])
        dg_ref[...] += _colsum(dhn * n)
        dh1 = dh2_ref[...] + _rms_bwd(r, n, g, dhn)
        dh1_ref[...] = dh1
        dh1b_ref[...] = dh1.astype(BF16)

        @pl.when(i == nt - 1)
        def _():
            for cp in copies:
                cp.wait()

    tok = lambda w: pl.BlockSpec((T, w), lambda i: (i, 0))
    shard = pl.BlockSpec((N_DEV, T, FF_SHARD), lambda i: (0, i, 0))
    res = pl.pallas_call(
        body, name="ffn_bwd_b", grid=(nt,),
        in_specs=[shard,
                  pl.BlockSpec((N_DEV, HALO, FF_SHARD), lambda i: (0, jnp.minimum((i + 1) * hb, S // HALO - 1), 0)),
                  _const(conv_w.shape), _const(w_up.shape), tok(D_MODEL), _const((1, D_MODEL)), tok(D_MODEL)]
        + [ANY_SPEC] * ng,
        out_specs=[shard, tok(D_MODEL), tok(D_MODEL), pl.BlockSpec((1, D_MODEL), lambda i: (0, 0))] + [ANY_SPEC] * ng,
        out_shape=[jax.ShapeDtypeStruct((N_DEV, S, FF_SHARD), BF16), jax.ShapeDtypeStruct((S, D_MODEL), F32),
                   jax.ShapeDtypeStruct((S, D_MODEL), BF16), jax.ShapeDtypeStruct((1, D_MODEL), F32)]
        + _exchange_shapes(grads, (True,) * ng),
        scratch_shapes=[pltpu.VMEM((T + HALO, FF_SHARD), F32)] + _exchange_sems(ng),
        compiler_params=_params("arbitrary"),
    )(dup, dup, conv_w, w_up, h1, g2, dh2, *grads)
    return res[0], res[1], res[2], res[3], res[4:]


def _mix_bwd(dh1b, w_out, dlt, mix, pool_w, pool_scale):
    S = dh1b.shape[0]
    nt = S // TM

    def body(dh_ref, w_ref, dlt_ref, mix_ref, pw_ref, ps_ref, datt_ref, du_ref, dpw_ref, dps_ref, dwo_ref,
             stage, carry, wo_acc):
        i = pl.program_id(0)
        tile = nt - 1 - i

        @pl.when(i == 0)
        def _():
            dpw_ref[...] = jnp.zeros(dpw_ref.shape, F32)
            dps_ref[...] = jnp.zeros(dps_ref.shape, F32)
            carry[...] = jnp.zeros(carry.shape, F32)
            wo_acc[...] = jnp.zeros(wo_acc.shape, F32)

        wo_acc[...] += lax.dot_general(mix_ref[...], dh_ref[...], TN, preferred_element_type=F32)

        @pl.when(i == nt - 1)
        def _():
            dwo_ref[...] = wo_acc[...].astype(BF16)

        dmix = lax.dot_general(dh_ref[...], w_ref[...], NT, preferred_element_type=F32)
        datt_ref[...] = dmix[:, 0:ATT_WIDTH]
        for g, w in enumerate(POOL_WINDOWS):
            cols = slice(g * POOL_GROUP, (g + 1) * POOL_GROUP)
            dpool = dmix[:, ATT_WIDTH + g * POOL_GROUP:ATT_WIDTH + (g + 1) * POOL_GROUP]
            dl = dlt_ref[:, cols]
            pw = pw_ref[g].astype(BF16)
            yg = jnp.dot(dl, pw, preferred_element_type=F32)
            dps_ref[:, cols] += _colsum(dpool * yg)
            dy = (dpool * ps_ref[:, cols]).astype(BF16)
            dpw_ref[g] += lax.dot_general(dl, dy, TN, preferred_element_type=F32)
            ddlt = lax.dot_general(dy, pw, NT, preferred_element_type=F32)
            cg = ddlt / _pool_count(tile, w)
            stage[0:TM, :] = cg
            stage[TM:TM + HALO, :] = carry[:, cols]
            acc = cg
            for j in range(1, w):
                acc = acc + stage[j:j + TM, :]
            du_ref[:, cols] = acc - ddlt
            carry[:, cols] = cg[0:HALO, :]

    tok = lambda w: pl.BlockSpec((TM, w), lambda i: (nt - 1 - i, 0))
    return pl.pallas_call(
        body, name="mix_bwd", grid=(nt,),
        in_specs=[tok(D_MODEL), _const(w_out.shape), tok(POOL_WIDTH), tok(D_MODEL), _const(pool_w.shape),
                  _const((1, POOL_WIDTH))],
        out_specs=[tok(ATT_WIDTH), tok(POOL_WIDTH), pl.BlockSpec(pool_w.shape, lambda i: (0, 0, 0)),
                   pl.BlockSpec((1, POOL_WIDTH), lambda i: (0, 0)), pl.BlockSpec(w_out.shape, lambda i: (0, 0))],
        out_shape=[jax.ShapeDtypeStruct((S, ATT_WIDTH), F32), jax.ShapeDtypeStruct((S, POOL_WIDTH), F32),
                   jax.ShapeDtypeStruct(pool_w.shape, F32), jax.ShapeDtypeStruct((1, POOL_WIDTH), F32),
                   jax.ShapeDtypeStruct(w_out.shape, BF16)],
        scratch_shapes=[pltpu.VMEM((TM + HALO, POOL_GROUP), F32), pltpu.VMEM((HALO, POOL_WIDTH), F32),
                        pltpu.VMEM(w_out.shape, F32)],
        compiler_params=_params("arbitrary"),
    )(dh1b, w_out, dlt, mix, pool_w, pool_scale)


def _attn_bwd(slopes, q, k, v, o, lse, do, grads, scatter):
    S = q.shape[0]
    CH = 512
    ng = len(grads)
    steps = ATT_WIDTH // LANES

    def body(slope_ref, q_ref, k_ref, v_ref, o_ref, lse_ref, do_ref, *rest):
        dq_ref, dk_ref, dv_ref = rest[ng:ng + 3]
        dl_s, bias_s = rest[2 * ng + 3:2 * ng + 5]
        hp = pl.program_id(0)
        copies = _exchange_copies(rest[:ng], rest[ng + 3:2 * ng + 3], scatter, *rest[2 * ng + 5:])

        @pl.when(hp == 0)
        def _():
            for cp in copies:
                cp.start()

        is0 = lax.broadcasted_iota(jnp.int32, (SPAN, LANES), 1) < HEAD_DIM
        is0c = lax.broadcasted_iota(jnp.int32, (CH, LANES), 1) < HEAD_DIM

        def prep(ci, carry):
            rows = pl.ds(pl.multiple_of(ci * CH, CH), CH)
            prod = do_ref[rows, :] * o_ref[rows, :]
            d0 = jnp.sum(jnp.where(is0c, prod, 0.0), axis=-1, keepdims=True)
            d1 = jnp.sum(jnp.where(is0c, 0.0, prod), axis=-1, keepdims=True)
            dl_s[rows, :] = jnp.where(is0c, d0, d1)
            zero = jnp.zeros((CH, LANES), F32)
            dq_ref[rows, :] = zero
            dk_ref[rows, :] = zero
            dv_ref[rows, :] = zero
            return carry

        lax.fori_loop(0, S // CH, prep, 0)

        for d in DILATIONS:
            nb, ngroups = _attn_groups(S, d, ATTN_GROUP_BWD, writes_key_rows=True)
            _attn_fill_bias(bias_s, slope_ref, hp, d)

            def group(i, carry, d=d, nb=nb):
                blocks = [_attn_block(i, g, d, nb, ATTN_GROUP_BWD) for g in range(ATTN_GROUP_BWD)]
                loaded = [(q_ref[rows, :], do_ref[rows, :], lse_ref[rows, :], dl_s[rows, :], k_ref[krows, :],
                           v_ref[krows, :].astype(BF16)) for rows, krows, _ in blocks]
                new = []
                for (rows, krows, tab), (qb, dob, lse_b, dl_b, kf, vb) in zip(blocks, loaded):
                    kb = kf.astype(BF16)
                    qs = _stack_heads(qb, is0).astype(BF16)
                    dos = _stack_heads(dob, is0).astype(BF16)
                    lse_s = jnp.concatenate([lse_b[:, 0:1], lse_b[:, HEAD_DIM:HEAD_DIM + 1]], axis=0)
                    dl_s2 = jnp.concatenate([dl_b[:, 0:1], dl_b[:, HEAD_DIM:HEAD_DIM + 1]], axis=0)
                    s = lax.dot_general(qs, kb, NT, preferred_element_type=F32) + bias_s[tab]
                    pr = jnp.exp(s - lse_s)
                    dp = lax.dot_general(dos, vb, NT, preferred_element_type=F32)
                    ds = (pr * (dp - dl_s2)).astype(BF16)
                    dv_c = lax.dot_general(pr.astype(BF16), dos, TN, preferred_element_type=F32)
                    dk_c = lax.dot_general(ds, qs, TN, preferred_element_type=F32)
                    dq_c = _unstack_heads(jnp.dot(ds, kb, preferred_element_type=F32), is0)
                    new.append((dq_c, dk_c, dv_c))
                old = [(dq_ref[rows, :], dk_ref[krows, :], dv_ref[krows, :]) for rows, krows, _ in blocks]
                for (rows, krows, _), (dq_c, dk_c, dv_c), (dq_o, dk_o, dv_o) in zip(blocks, new, old):
                    dq_ref[rows, :] = dq_o + dq_c
                    dk_ref[krows, :] = dk_o + dk_c
                    dv_ref[krows, :] = dv_o + dv_c
                return carry

            lax.fori_loop(0, ngroups, group, 0)

        @pl.when(hp == steps - 1)
        def _():
            for cp in copies:
                cp.wait()

    col = pl.BlockSpec((S, LANES), lambda i: (0, i))
    res = pl.pallas_call(
        body, name="attn_bwd", grid=(steps,),
        in_specs=[pl.BlockSpec(memory_space=pltpu.SMEM)] + [col] * 6 + [ANY_SPEC] * ng,
        out_specs=[col] * 3 + [ANY_SPEC] * ng,
        out_shape=[jax.ShapeDtypeStruct((S, ATT_WIDTH), F32)] * 3 + _exchange_shapes(grads, scatter),
        scratch_shapes=[pltpu.VMEM((S, LANES), F32), pltpu.VMEM((2, 2 * SPAN, 2 * SPAN), F32)] + _exchange_sems(ng),
        compiler_params=_params("arbitrary"),
    )(slopes, q, k, v, o, lse, do, *grads)
    return res[0], res[1], res[2], res[3:]


def _in_bwd(dq, dk, dv, du, w_in, x, g1, dh1):
    S = x.shape[0]

    def body(dq_ref, dk_ref, dv_ref, du_ref, w_ref, x_ref, g_ref, dh1_ref, dz_ref, dx_ref, dg_ref):
        @pl.when(pl.program_id(0) == 0)
        def _():
            dg_ref[...] = jnp.zeros(dg_ref.shape, F32)

        srcs = (dq_ref, dk_ref, dv_ref, du_ref)
        dhn = jnp.zeros((TM, D_MODEL), F32)
        for j in range(N_DEV):
            dz = srcs[j // 2][:, (j % 2) * 256:(j % 2 + 1) * 256]
            if j < 2:
                dz = dz * (HEAD_DIM ** -0.5)
            dz = dz.astype(BF16)
            dz_ref[j] = dz
            dhn = dhn + lax.dot_general(dz, w_ref[j], NT, preferred_element_type=F32)
        g = g_ref[...]
        r, n = _rms(x_ref[...])
        dg_ref[...] += _colsum(dhn * n)
        dx_ref[...] = dh1_ref[...] + _rms_bwd(r, n, g, dhn)

    tok = lambda w: pl.BlockSpec((TM, w), lambda i: (i, 0))
    return pl.pallas_call(
        body, name="in_bwd", grid=(S // TM,),
        in_specs=[tok(512)] * 4 + [_const(w_in.shape), tok(D_MODEL), _const((1, D_MODEL)), tok(D_MODEL)],
        out_specs=[pl.BlockSpec((N_DEV, TM, 256), lambda i: (0, i, 0)), tok(D_MODEL),
                   pl.BlockSpec((1, D_MODEL), lambda i: (0, 0))],
        out_shape=[jax.ShapeDtypeStruct((N_DEV, S, 256), BF16), jax.ShapeDtypeStruct((S, D_MODEL), F32),
                   jax.ShapeDtypeStruct((1, D_MODEL), F32)],
        compiler_params=_params("arbitrary"),
    )(dq, dk, dv, du, w_in, x, g1, dh1)


def _adamw(name, parts, w, m, v):
    R, C = w.shape
    rb = R
    for cand in (256, 128, 64, 32, 16, 8):
        if R % cand == 0 and R > cand:
            rb = cand
            break

    def body(p_ref, w_ref, m_ref, v_ref, g_ref, d_ref, mo_ref, vo_ref):
        g = p_ref[0].astype(F32)
        for s in range(1, N_DEV):
            g = g + p_ref[s].astype(F32)
        g_ref[...] = g
        d_ref[...], mo_ref[...], vo_ref[...] = _adam_update(g, w_ref[...], m_ref[...], v_ref[...])

    blk = pl.BlockSpec((rb, C), lambda i: (i, 0))
    return pl.pallas_call(
        body, name=name, grid=(R // rb,),
        in_specs=[pl.BlockSpec((N_DEV, rb, C), lambda i: (0, i, 0)), blk, blk, blk],
        out_specs=[blk] * 4,
        out_shape=[jax.ShapeDtypeStruct((R, C), F32)] * 4,
        compiler_params=_params("arbitrary"),
    )(parts, w, m, v)


def _adam_update(g, w, m, v):
    m_new = ADAM_B1 * m + (1.0 - ADAM_B1) * g
    v_new = ADAM_B2 * v + (1.0 - ADAM_B2) * (g * g)
    m_hat = m_new / (1.0 - ADAM_B1 ** ADAM_STEP)
    v_hat = v_new / (1.0 - ADAM_B2 ** ADAM_STEP)
    return -ADAM_LR * (m_hat / (jnp.sqrt(v_hat) + ADAM_EPS) + ADAM_WD * w), m_new, v_new


def _adamw_small(parts, loss_parts, ws, ms, vs):
    n = len(ws)

    def body(*refs):
        p_refs, lp_ref = refs[:n], refs[n]
        w_refs, m_refs, v_refs = refs[n + 1:2 * n + 1], refs[2 * n + 1:3 * n + 1], refs[3 * n + 1:4 * n + 1]
        outs = refs[4 * n + 1:]
        for i in range(n):
            g = p_refs[i][0]
            for s in range(1, N_DEV):
                g = g + p_refs[i][s]
            d, m_new, v_new = _adam_update(g, w_refs[i][...], m_refs[i][...], v_refs[i][...])
            outs[i][...] = g
            outs[n + i][...] = d
            outs[2 * n + i][...] = m_new
            outs[3 * n + i][...] = v_new
        tot = lp_ref[0]
        for s in range(1, N_DEV):
            tot = tot + lp_ref[s]
        outs[4 * n][...] = tot

    shapes = [jax.ShapeDtypeStruct(w.shape, F32) for w in ws]
    res = pl.pallas_call(
        body, name="adamw_replicated",
        out_shape=shapes * 4 + [jax.ShapeDtypeStruct(loss_parts.shape[1:], F32)],
        compiler_params=_params(),
    )(*parts, loss_parts, *ws, *ms, *vs)
    return res[:n], res[n:2 * n], res[2 * n:3 * n], res[3 * n:4 * n], res[4 * n]


def _gather2(name, arrays):
    n = len(arrays)

    def body(*refs):
        first, passed, last = _gather2_copies(refs[:n], refs[n:2 * n], *refs[2 * n:])
        for cp in first:
            cp.start()
        for arrival, cp in passed:
            arrival.wait_recv()
            cp.start()
        for wait in last:
            wait()

    return pl.pallas_call(
        body, name=name,
        in_specs=[ANY_SPEC] * n, out_specs=[ANY_SPEC] * n, out_shape=_exchange_shapes(arrays, (False,) * n),
        scratch_shapes=_exchange_sems(n),
    )(*arrays)


def _dw_in_exchange(hn, dz, small):
    S = hn.shape[0]
    nt = S // TK
    ns = len(small)
    kd, nd = hn.shape[1], dz.shape[2]
    me_arr = (4 * lax.axis_index("x") + 2 * lax.axis_index("y") + lax.axis_index("c")).astype(jnp.int32).reshape(1)

    def body(me_ref, x_ref, dy_ref, *rest):
        recv_ref = rest[ns]
        acc, stage, send_sems, recv_sems, own_sem = rest[2 * ns + 1:2 * ns + 6]
        j, t = pl.program_id(0), pl.program_id(1)
        x, y, c = lax.axis_index("x"), lax.axis_index("y"), lax.axis_index("c")
        me = 4 * x + 2 * y + c
        small_copies = _exchange_copies(rest[:ns], rest[ns + 1:2 * ns + 1], (False,) * ns, *rest[2 * ns + 6:])

        @pl.when((j == 0) & (t == 0))
        def _():
            for cp in small_copies:
                cp.start()

        @pl.when(t == 0)
        def _():
            acc[...] = jnp.zeros(acc.shape, F32)

        acc[...] += lax.dot_general(x_ref[...], dy_ref[...], TN, preferred_element_type=F32)

        def to_owner(k, owner):
            return pltpu.make_async_remote_copy(
                src_ref=stage.at[owner], dst_ref=recv_ref.at[me], send_sem=send_sems.at[k], recv_sem=recv_sems.at[k],
                device_id=(owner // 4, (owner // 2) % 2, owner % 2), device_id_type=MESH)

        own = pltpu.make_async_copy(stage.at[me], recv_ref.at[me], own_sem)

        @pl.when(t == nt - 1)
        def _():
            owner = (me + 1 + j) % N_DEV
            stage[owner] = acc[...].astype(BF16)

            @pl.when(j < N_DEV - 1)
            def _():
                to_owner(j, owner).start()

            @pl.when(j == N_DEV - 1)
            def _():
                own.start()
                own.wait()
                for k in range(N_DEV - 1):
                    to_owner(k, me).wait_send()
                    to_owner(k, me).wait_recv()
                for cp in small_copies:
                    cp.wait()

    slab = lambda j, me_ref: (me_ref[0] + 1 + j) % N_DEV
    grid_spec = pltpu.PrefetchScalarGridSpec(
        num_scalar_prefetch=1, grid=(N_DEV, nt),
        in_specs=[pl.BlockSpec((TK, kd), lambda j, t, me_ref: (t, 0)),
                  pl.BlockSpec((None, TK, nd), lambda j, t, me_ref: (slab(j, me_ref), t, 0))] + [ANY_SPEC] * ns,
        out_specs=[ANY_SPEC] * (ns + 1),
        scratch_shapes=[pltpu.VMEM((kd, nd), F32), pltpu.VMEM((N_DEV, kd, nd), BF16),
                        pltpu.SemaphoreType.DMA((N_DEV - 1,)), pltpu.SemaphoreType.DMA((N_DEV - 1,)),
                        pltpu.SemaphoreType.DMA] + _exchange_sems(ns))
    res = pl.pallas_call(
        body, name="dw_in_exchange", grid_spec=grid_spec,
        out_shape=[jax.ShapeDtypeStruct((N_DEV, kd, nd), BF16)] + _exchange_shapes(small, (False,) * ns),
        compiler_params=_params("arbitrary", "arbitrary"),
    )(me_arr, hn, dz, *small)
    return res[0], res[1:]


def kernel(x, p, ln_mix, w_in, pool_w, pool_scale, w_out, ln_ffn, w_up, conv_w, conv_b, w_down, ln_ple, w_ple_gate, w_ple, ln_final, loss_target, m_ln_mix, m_w_in, m_pool_w, m_pool_scale, m_w_out, m_ln_ffn, m_w_up, m_conv_w, m_conv_b, m_w_down, m_ln_ple, m_w_ple_gate, m_w_ple, m_ln_final, v_ln_mix, v_w_in, v_pool_w, v_pool_scale, v_w_out, v_ln_ffn, v_w_up, v_conv_w, v_conv_b, v_w_down, v_ln_ple, v_w_ple_gate, v_w_ple, v_ln_final):
    xs, ps, tgt, pool_w0 = x[0], p[0, 0], loss_target[0], pool_w[0]
    slopes = jnp.exp2(-8.0 * (jnp.arange(N_HEADS, dtype=F32) + 1.0) / N_HEADS)
    conv_b_s = conv_b.reshape(N_DEV, 1, FF_SHARD)

    (w_in_g,) = _gather2("gather_w_in", [w_in[0].astype(BF16)])
    w_down_b = w_down[0].astype(BF16)
    (q, k, v, u, hn1), (w_out_g, w_down_g0) = _qkvu(xs, ln_mix, w_in_g, [w_out[0].astype(BF16), w_down_b[:W_DOWN_CUT]])
    att, lse, (w_up_g, conv_w_g) = _attn_fwd(slopes, q, k, v, [w_up[0].astype(BF16), conv_w[0]])
    w_out_f = w_out_g.reshape(D_MODEL, D_MODEL)
    h1, mix, dlt, (w_down_g1,) = _mix_out(xs, att, u, pool_w0, pool_scale, w_out_f, [w_down_b[W_DOWN_CUT:]])
    w_down_f = jnp.concatenate([w_down_g0, w_down_g1], axis=1).reshape(4, FF_SHARD, D_MODEL)
    h2, hn2, up, upc, (w_pg_g, w_ple_g) = _ffn_fwd(h1, ln_ffn, w_up_g, conv_w_g, conv_b_s, w_down_f,
                                                   [w_ple_gate[0].astype(BF16), w_ple[0].astype(BF16)])
    w_pg_f = w_pg_g.reshape(D_MODEL, D_MODEL)
    w_ple_f = jnp.transpose(w_ple_g, (1, 0, 2)).reshape(PLE_DIM, D_MODEL)
    loss_blk, dh2, dh2b, d_w_pg, d_w_ple, d_ln_ple, d_ln_final = _head(
        h2, ps, ln_ple, w_pg_f, w_ple_f, ln_final.reshape(1, D_MODEL), tgt)

    d_w_pg = d_w_pg.reshape(N_DEV, D_MODEL // N_DEV, D_MODEL)
    d_w_ple = jnp.transpose(d_w_ple.reshape(PLE_DIM, N_DEV, LANES), (1, 0, 2))
    d_w_down, dup, d_conv_w, d_conv_b, (r_w_pg, r_w_ple) = _ffn_bwd_a(dh2b, up, upc, w_down_f, [d_w_pg, d_w_ple])
    d_w_down = d_w_down.reshape(N_DEV, D_FF // N_DEV, D_MODEL)
    dpre, dh1, dh1b, d_ln_ffn, (r_conv_w, r_w_down) = _ffn_bwd_b(
        dup, conv_w_g, w_up_g, h1, ln_ffn, dh2, [d_conv_w, d_w_down])
    datt, du, d_pool_w, d_pool_scale, d_w_out = _mix_bwd(dh1b, w_out_f, dlt, mix, pool_w0, pool_scale)
    d_w_out = d_w_out.reshape(N_DEV, D_MODEL // N_DEV, D_MODEL)
    d_w_up = _wgrad("dw_up", dpre, hn2, "lead", "full", N_DEV, FF_SHARD, D_MODEL)
    rep_late = [d_pool_w, d_pool_scale, d_ln_ffn, d_conv_b.reshape(1, 2 * D_FF), d_ln_ple, d_ln_final, loss_blk]
    dq, dk, dv, received = _attn_bwd(slopes, q, k, v, att, lse, datt, [d_w_out, d_w_up] + rep_late,
                                     (True, True) + (False,) * len(rep_late))
    r_w_out, r_w_up, r_rep = received[0], received[1], list(received[2:])
    dz, grad_x, d_ln_mix = _in_bwd(dq, dk, dv, du, w_in_g, xs, ln_mix, dh1)

    rep_names = ("ln_mix", "pool_w", "pool_scale", "ln_ffn", "conv_b", "ln_ple", "ln_final")
    rep_w = [ln_mix, pool_w0, pool_scale, ln_ffn, conv_b, ln_ple, ln_final.reshape(1, D_MODEL)]
    rep_m = [m_ln_mix, m_pool_w[0], m_pool_scale, m_ln_ffn, m_conv_b, m_ln_ple, m_ln_final.reshape(1, D_MODEL)]
    rep_v = [v_ln_mix, v_pool_w[0], v_pool_scale, v_ln_ffn, v_conv_b, v_ln_ple, v_ln_final.reshape(1, D_MODEL)]
    r_w_in, (r_ln_mix,) = _dw_in_exchange(hn1, dz, [d_ln_mix])
    small = _adamw_small([r_ln_mix] + r_rep[:-1], r_rep[-1], rep_w, rep_m, rep_v)
    loss = small[4][0, 0]

    sharded = {}
    sharded["w_in"] = _adamw("adamw_w_in", r_w_in, w_in[0], m_w_in[0], v_w_in[0])
    sharded["w_out"] = _adamw("adamw_w_out", r_w_out, w_out[0], m_w_out[0], v_w_out[0])
    sharded["w_up"] = [t.T for t in _adamw("adamw_w_up", r_w_up, w_up[0].T, m_w_up[0].T, v_w_up[0].T)]
    sharded["conv_w"] = _adamw("adamw_conv_w", r_conv_w, conv_w[0], m_conv_w[0], v_conv_w[0])
    sharded["w_down"] = _adamw("adamw_w_down", r_w_down, w_down[0], m_w_down[0], v_w_down[0])
    sharded["w_ple_gate"] = _adamw("adamw_w_ple_gate", r_w_pg, w_ple_gate[0], m_w_ple_gate[0], v_w_ple_gate[0])
    sharded["w_ple"] = _adamw("adamw_w_ple", r_w_ple, w_ple[0], m_w_ple[0], v_w_ple[0])

    shapes = dict(w_in=w_in, w_out=w_out, w_up=w_up, conv_w=conv_w, w_down=w_down, w_ple_gate=w_ple_gate, w_ple=w_ple,
                  ln_mix=ln_mix, pool_w=pool_w, pool_scale=pool_scale, ln_ffn=ln_ffn, conv_b=conv_b, ln_ple=ln_ple,
                  ln_final=ln_final)

    def leaf(kind, n):
        src = sharded[n][kind] if n in sharded else small[kind][rep_names.index(n)]
        return src.reshape(shapes[n].shape)

    order = ("ln_mix", "w_in", "pool_w", "pool_scale", "w_out", "ln_ffn", "w_up", "conv_w", "conv_b", "w_down", "ln_ple",
             "w_ple_gate", "w_ple", "ln_final")
    outs = [loss, grad_x[None]]
    for kind in range(4):
        outs += [leaf(kind, n) for n in order]
    return tuple(outs)
```

```python
import jax
import jax.numpy as jnp
from jax import lax
from jax.experimental import pallas as pl
from jax.experimental.pallas import tpu as pltpu

F32 = jnp.float32
BF16 = jnp.bfloat16

N_DEV = 8
D_MODEL = 1024
ATT_WIDTH = 512
POOL_WIDTH = 512
N_HEADS = 8
HEAD_DIM = 64
SPAN = 128
DILATIONS = (1, 4, 16)
POOL_WINDOWS = (2, 4, 8, 16)
POOL_GROUP = 128
D_FF = 2816
FF_SHARD = 2 * D_FF // N_DEV
PLE_DIM = 256
EPS = 1e-6
NEG = -1e30

ADAM_LR = 0.001
ADAM_B1 = 0.9
ADAM_B2 = 0.999
ADAM_EPS = 1e-08
ADAM_WD = 0.01
ADAM_STEP = 10

LANES = 128
HALO = 16
TM = 512
TM_FF = 256
TK = 4096
W_DOWN_CUT = 112
ATTN_GROUP_FWD = 16
ATTN_GROUP_BWD = 8
VMEM_LIMIT = 56 * 1024 * 1024

MESH = pl.DeviceIdType.MESH
NT = (((1,), (1,)), ((), ()))
TN = (((0,), (0,)), ((), ()))


def _params(*sem):
    return pltpu.CompilerParams(dimension_semantics=sem or None, vmem_limit_bytes=VMEM_LIMIT)


def _const(shape):
    n = len(shape)
    return pl.BlockSpec(shape, lambda *_: (0,) * n, pipeline_mode=pl.Buffered(1))


def _rms(h):
    r = lax.rsqrt(jnp.mean(h * h, axis=-1, keepdims=True) + EPS)
    return r, h * r


def _rms_bwd(r, n, g, dhn):
    dn = dhn * g
    return r * (dn - n * jnp.mean(dn * n, axis=-1, keepdims=True))


def _colsum(a):
    return jnp.sum(a, axis=0, keepdims=True)


def _gather2_copies(ins, outs, send_sems, recv_sems, local_sems):
    n = len(ins)
    x, y, c = lax.axis_index("x"), lax.axis_index("y"), lax.axis_index("c")
    slot = lambda px, py, pc: 4 * px + 2 * py + pc
    chips = [(x, 1 - y), (1 - x, y), (1 - x, 1 - y)]
    first, passed, last = [], [], []

    def remote(a, r, src, dst_slot, to):
        return pltpu.make_async_remote_copy(
            src_ref=src, dst_ref=outs[a].at[dst_slot],
            send_sem=send_sems.at[a * (N_DEV - 1) + r], recv_sem=recv_sems.at[a * (N_DEV - 1) + r],
            device_id=to, device_id_type=MESH)

    for a in range(n):
        mine = pltpu.make_async_copy(ins[a], outs[a].at[slot(x, y, c)], local_sems.at[a])
        to_sibling = remote(a, 0, ins[a], slot(x, y, c), (x, y, 1 - c))
        first += [mine, to_sibling]
        last += [mine.wait, to_sibling.wait_send, to_sibling.wait_recv]
        for r, (px, py) in enumerate(chips, start=1):
            to_chip = remote(a, r, ins[a], slot(x, y, c), (px, py, c))
            onward = remote(a, 3 + r, outs[a].at[slot(px, py, c)], slot(px, py, c), (x, y, 1 - c))
            first.append(to_chip)
            passed.append((to_chip, onward))
            last += [to_chip.wait_send, onward.wait_send, onward.wait_recv]
    return first, passed, last


def _gather2_begin(plan, step, pass_step):
    first, passed, _ = plan

    @pl.when(step == 0)
    def _():
        for cp in first:
            cp.start()

    @pl.when(step == pass_step)
    def _():
        for arrival, cp in passed:
            arrival.wait_recv()
            cp.start()


def _gather2_end(plan, step, nsteps):
    @pl.when(step == nsteps - 1)
    def _():
        for wait in plan[2]:
            wait()


ANY_SPEC = pl.BlockSpec(memory_space=pl.ANY)


def _exchange_shapes(arrays, scatter):
    out = []
    for a, s in zip(arrays, scatter):
        slab = a.shape[1:] if s else a.shape
        out.append(jax.ShapeDtypeStruct((N_DEV,) + tuple(slab), a.dtype))
    return out


def _exchange_sems(n):
    return [pltpu.SemaphoreType.DMA((n * (N_DEV - 1),)), pltpu.SemaphoreType.DMA((n * (N_DEV - 1),)),
            pltpu.SemaphoreType.DMA((n,))]


def _exchange_copies(ins, outs, scatter, send_sems, recv_sems, local_sems):
    n = len(ins)
    x, y, c = lax.axis_index("x"), lax.axis_index("y"), lax.axis_index("c")
    me = 4 * x + 2 * y + c
    copies = []
    for a in range(n):
        src = ins[a].at[me] if scatter[a] else ins[a]
        copies.append(pltpu.make_async_copy(src, outs[a].at[me], local_sems.at[a]))
    for k in range(1, N_DEV):
        px = 1 - x if k & 4 else x
        py = 1 - y if k & 2 else y
        pc = 1 - c if k & 1 else c
        pid = 4 * px + 2 * py + pc
        for a in range(n):
            src = ins[a].at[pid] if scatter[a] else ins[a]
            copies.append(pltpu.make_async_remote_copy(
                src_ref=src, dst_ref=outs[a].at[me],
                send_sem=send_sems.at[a * (N_DEV - 1) + k - 1], recv_sem=recv_sems.at[a * (N_DEV - 1) + k - 1],
                device_id=(px, py, pc), device_id_type=MESH))
    return copies


def _qkvu(x, g1, w_in, shards):
    S = x.shape[0]
    ns = len(shards)
    nsteps = S // TM

    def body(x_ref, g_ref, w_ref, *rest):
        q_ref, k_ref, v_ref, u_ref, hn_ref = rest[ns:ns + 5]
        plan = _gather2_copies(rest[:ns], rest[ns + 5:2 * ns + 5], *rest[2 * ns + 5:])
        _gather2_begin(plan, pl.program_id(0), nsteps - 2)
        r, n = _rms(x_ref[...])
        hn = (n * g_ref[...]).astype(BF16)
        hn_ref[...] = hn
        outs = (q_ref, k_ref, v_ref, u_ref)
        for j in range(N_DEV):
            z = jnp.dot(hn, w_ref[j], preferred_element_type=F32)
            if j < 2:
                z = z * (HEAD_DIM ** -0.5)
            outs[j // 2][:, (j % 2) * 256:(j % 2 + 1) * 256] = z
        _gather2_end(plan, pl.program_id(0), nsteps)

    tok = lambda w: pl.BlockSpec((TM, w), lambda i: (i, 0))
    res = pl.pallas_call(
        body, name="qkvu", grid=(nsteps,),
        in_specs=[tok(D_MODEL), _const((1, D_MODEL)), _const(w_in.shape)] + [ANY_SPEC] * ns,
        out_specs=[tok(512)] * 4 + [tok(D_MODEL)] + [ANY_SPEC] * ns,
        out_shape=[jax.ShapeDtypeStruct((S, 512), F32)] * 4 + [jax.ShapeDtypeStruct((S, D_MODEL), BF16)]
        + _exchange_shapes(shards, (False,) * ns),
        scratch_shapes=_exchange_sems(ns),
        compiler_params=_params("arbitrary"),
    )(x, g1, w_in, *shards)
    return res[:5], res[5:]


def _attn_fill_bias(bias_s, slope_ref, hp, d):
    qi = lax.broadcasted_iota(jnp.int32, (SPAN, 2 * SPAN), 0)
    kj = lax.broadcasted_iota(jnp.int32, (SPAN, 2 * SPAN), 1)
    for t, diff in enumerate((qi + SPAN - kj, qi - kj)):
        valid = (diff >= 0) & (diff <= SPAN)
        dist = diff.astype(F32) * float(d)
        for h in range(2):
            bias_s[t, h * SPAN:(h + 1) * SPAN, :] = jnp.where(valid, -slope_ref[2 * hp + h] * dist, NEG)


def _stack_heads(x, is0):
    return jnp.concatenate([jnp.where(is0, x, 0.0), jnp.where(is0, 0.0, x)], axis=0)


def _unstack_heads(y, is0):
    return jnp.where(is0, y[0:SPAN], y[SPAN:2 * SPAN])


def _attn_block(i, g, d, nb, group):
    gr = min(d, group)
    gn = group // gr
    per = d // gr
    r = (i & (per - 1)) * gr + g % gr
    n = (i >> (per.bit_length() - 1)) + (g // gr) * (nb // gn)
    k0 = jnp.maximum(n - 1, 0)

    def ds(block, nrows):
        start = block * (SPAN * d) + r
        return pl.ds(start, nrows, stride=d) if d > 1 else pl.ds(start, nrows)

    return ds(n, SPAN), ds(k0, 2 * SPAN), jnp.where(n == 0, 1, 0)


def _attn_groups(S, d, group, writes_key_rows=False):
    nb = S // d // SPAN
    gn = group // min(d, group)
    assert nb >= 2 and nb % gn == 0 and (gn == 1 or nb // gn >= (3 if writes_key_rows else 2))
    return nb, d * nb // group


def _attn_fwd(slopes, q, k, v, shards):
    S = q.shape[0]
    ns = len(shards)
    steps = ATT_WIDTH // LANES

    def body(slope_ref, q_ref, k_ref, v_ref, *rest):
        o_ref, lse_ref = rest[ns:ns + 2]
        m_s, l_s, bias_s = rest[2 * ns + 2:2 * ns + 5]
        hp = pl.program_id(0)
        plan = _gather2_copies(rest[:ns], rest[ns + 2:2 * ns + 2], *rest[2 * ns + 5:])
        _gather2_begin(plan, hp, steps - 1)

        is0 = lax.broadcasted_iota(jnp.int32, (SPAN, LANES), 1) < HEAD_DIM
        for pi, d in enumerate(DILATIONS):
            nb, ngroups = _attn_groups(S, d, ATTN_GROUP_FWD)
            _attn_fill_bias(bias_s, slope_ref, hp, d)

            def group(i, carry, d=d, pi=pi, nb=nb):
                blocks = [_attn_block(i, g, d, nb, ATTN_GROUP_FWD) for g in range(ATTN_GROUP_FWD)]
                loaded = [(q_ref[rows, :], k_ref[krows, :].astype(BF16), v_ref[krows, :].astype(BF16))
                          for rows, krows, _ in blocks]
                new = []
                for (rows, krows, tab), (qb, kb, vb) in zip(blocks, loaded):
                    qs = _stack_heads(qb, is0).astype(BF16)
                    s = lax.dot_general(qs, kb, NT, preferred_element_type=F32) + bias_s[tab]
                    m = jnp.max(s, axis=-1, keepdims=True)
                    e = jnp.exp(s - m)
                    l = jnp.sum(e, axis=-1, keepdims=True)
                    pv = jnp.dot(e.astype(BF16), vb, preferred_element_type=F32)
                    new.append([_unstack_heads(jnp.broadcast_to(m, pv.shape), is0),
                                _unstack_heads(jnp.broadcast_to(l, pv.shape), is0), _unstack_heads(pv, is0)])
                if pi > 0:
                    old = [(m_s[rows, :], l_s[rows, :], o_ref[rows, :]) for rows, _, _ in blocks]
                    for st, (m_o, l_o, o_o) in zip(new, old):
                        m_n = jnp.maximum(m_o, st[0])
                        a_o = jnp.exp(m_o - m_n)
                        a_b = jnp.exp(st[0] - m_n)
                        st[:] = [m_n, a_o * l_o + a_b * st[1], a_o * o_o + a_b * st[2]]
                for (rows, _, _), (m_b, l_b, acc) in zip(blocks, new):
                    if pi == len(DILATIONS) - 1:
                        o_ref[rows, :] = acc / l_b
                        lse_ref[rows, :] = m_b + jnp.log(l_b)
                    else:
                        o_ref[rows, :] = acc
                        m_s[rows, :] = m_b
                        l_s[rows, :] = l_b
                return carry

            lax.fori_loop(0, ngroups, group, 0)

        _gather2_end(plan, hp, steps)

    col = pl.BlockSpec((S, LANES), lambda i: (0, i))
    res = pl.pallas_call(
        body, name="attn_fwd", grid=(steps,),
        in_specs=[pl.BlockSpec(memory_space=pltpu.SMEM), col, col, col] + [ANY_SPEC] * ns,
        out_specs=[col, col] + [ANY_SPEC] * ns,
        out_shape=[jax.ShapeDtypeStruct((S, ATT_WIDTH), F32)] * 2 + _exchange_shapes(shards, (False,) * ns),
        scratch_shapes=[pltpu.VMEM((S, LANES), F32), pltpu.VMEM((S, LANES), F32),
                        pltpu.VMEM((2, 2 * SPAN, 2 * SPAN), F32)] + _exchange_sems(ns),
        compiler_params=_params("arbitrary"),
    )(slopes, q, k, v, *shards)
    return res[0], res[1], res[2:]


def _pool_count(i, w):
    t = i * TM + lax.broadcasted_iota(jnp.int32, (TM, 1), 0)
    return jnp.minimum(t + 1, w).astype(F32)


def _mix_out(x, att, u, pool_w, pool_scale, w_out, shards):
    S = x.shape[0]
    ns = len(shards)
    nsteps = S // TM

    def body(x_ref, att_ref, u_ref, pw_ref, ps_ref, w_ref, *rest):
        h1_ref, mix_ref, dlt_ref = rest[ns:ns + 3]
        ubuf = rest[2 * ns + 3]
        i = pl.program_id(0)
        plan = _gather2_copies(rest[:ns], rest[ns + 3:2 * ns + 3], *rest[2 * ns + 4:])
        _gather2_begin(plan, i, nsteps - 1)

        @pl.when(i == 0)
        def _():
            ubuf[0:HALO, :] = jnp.zeros((HALO, POOL_WIDTH), F32)

        ubuf[HALO:HALO + TM, :] = u_ref[...]
        mix_ref[:, 0:ATT_WIDTH] = att_ref[...].astype(BF16)
        for g, w in enumerate(POOL_WINDOWS):
            cols = slice(g * POOL_GROUP, (g + 1) * POOL_GROUP)
            ug = ubuf[HALO:HALO + TM, cols]
            acc = ug
            for j in range(1, w):
                acc = acc + ubuf[HALO - j:HALO - j + TM, cols]
            dlt = (acc / _pool_count(i, w) - ug).astype(BF16)
            dlt_ref[:, cols] = dlt
            yg = jnp.dot(dlt, pw_ref[g].astype(BF16), preferred_element_type=F32) * ps_ref[:, cols]
            mix_ref[:, ATT_WIDTH + g * POOL_GROUP:ATT_WIDTH + (g + 1) * POOL_GROUP] = yg.astype(BF16)
        ubuf[0:HALO, :] = ubuf[TM:TM + HALO, :]
        h1_ref[...] = x_ref[...] + jnp.dot(mix_ref[...], w_ref[...], preferred_element_type=F32)
        _gather2_end(plan, i, nsteps)

    tok = lambda w: pl.BlockSpec((TM, w), lambda i: (i, 0))
    res = pl.pallas_call(
        body, name="mix_out", grid=(nsteps,),
        in_specs=[tok(D_MODEL), tok(ATT_WIDTH), tok(POOL_WIDTH), _const(pool_w.shape), _const((1, POOL_WIDTH)),
                  _const(w_out.shape)] + [ANY_SPEC] * ns,
        out_specs=[tok(D_MODEL), tok(D_MODEL), tok(POOL_WIDTH)] + [ANY_SPEC] * ns,
        out_shape=[jax.ShapeDtypeStruct((S, D_MODEL), F32), jax.ShapeDtypeStruct((S, D_MODEL), BF16),
                   jax.ShapeDtypeStruct((S, POOL_WIDTH), BF16)] + _exchange_shapes(shards, (False,) * ns),
        scratch_shapes=[pltpu.VMEM((TM + HALO, POOL_WIDTH), F32)] + _exchange_sems(ns),
        compiler_params=_params("arbitrary"),
    )(x, att, u, pool_w, pool_scale, w_out, *shards)
    return res[0], res[1], res[2], res[3:]


def _conv_fwd(stage, upre, prev, cw, cb):
    T = upre.shape[0]
    stage[0:HALO, :] = prev
    stage[HALO:HALO + T, :] = upre
    return cb + cw[0:1, :] * stage[HALO - 2:HALO - 2 + T, :] + cw[1:2, :] * stage[HALO - 1:HALO - 1 + T, :] + cw[2:3, :] * upre


def _ffn_fwd(h1, g2, w_up, conv_w, conv_b, w_down, shards):
    S = h1.shape[0]
    T = TM_FF
    ns = len(shards)
    nsteps = S // T

    def body(h1_ref, g_ref, wu_ref, cw_ref, cb_ref, wd_ref, *rest):
        h2_ref, hn_ref, up_ref, upc_ref = rest[ns:ns + 4]
        carry, stage = rest[2 * ns + 4:2 * ns + 6]
        i = pl.program_id(0)
        plan = _gather2_copies(rest[:ns], rest[ns + 4:2 * ns + 4], *rest[2 * ns + 6:])
        _gather2_begin(plan, i, nsteps // 2)

        @pl.when(i == 0)
        def _():
            carry[...] = jnp.zeros(carry.shape, F32)

        h1t = h1_ref[...]
        r, n = _rms(h1t)
        hn = (n * g_ref[...]).astype(BF16)
        hn_ref[...] = hn
        acc = h1t
        for j in range(4):
            conv = []
            for jj in (j, j + 4):
                upre = jnp.dot(hn, wu_ref[jj], preferred_element_type=F32)
                up_ref[jj] = upre.astype(BF16)
                conv.append(_conv_fwd(stage, upre, carry[jj], cw_ref[jj], cb_ref[jj]))
                upc_ref[jj] = conv[-1].astype(BF16)
                carry[jj] = stage[T:T + HALO, :]
            gate, val = conv
            a = gate * jax.nn.sigmoid(gate) * val
            acc = acc + jnp.dot(a.astype(BF16), wd_ref[j], preferred_element_type=F32)
        h2_ref[...] = acc
        _gather2_end(plan, i, nsteps)

    tok = lambda w: pl.BlockSpec((T, w), lambda i: (i, 0))
    res = pl.pallas_call(
        body, name="ffn_fwd", grid=(nsteps,),
        in_specs=[tok(D_MODEL), _const((1, D_MODEL)), _const(w_up.shape), _const(conv_w.shape), _const(conv_b.shape),
                  _const(w_down.shape)] + [ANY_SPEC] * ns,
        out_specs=[tok(D_MODEL), tok(D_MODEL)] + [pl.BlockSpec((N_DEV, T, FF_SHARD), lambda i: (0, i, 0))] * 2
        + [ANY_SPEC] * ns,
        out_shape=[jax.ShapeDtypeStruct((S, D_MODEL), F32), jax.ShapeDtypeStruct((S, D_MODEL), BF16)]
        + [jax.ShapeDtypeStruct((N_DEV, S, FF_SHARD), BF16)] * 2 + _exchange_shapes(shards, (False,) * ns),
        scratch_shapes=[pltpu.VMEM((N_DEV, HALO, FF_SHARD), F32), pltpu.VMEM((T + HALO, FF_SHARD), F32)]
        + _exchange_sems(ns),
        compiler_params=_params("arbitrary"),
    )(h1, g2, w_up, conv_w, conv_b, w_down, *shards)
    return res[0], res[1], res[2], res[3], res[4:]


def _head(h2, p, g3, w_pg, w_ple, g4, target):
    S = h2.shape[0]
    nt = S // TM

    def body(h2_ref, p_ref, g3_ref, wpg_ref, wple_ref, g4_ref, t_ref,
             loss_ref, dh2_ref, dh2b_ref, dwpg_ref, dwple_ref, dg3_ref, dg4_ref, lacc, pg_acc, ple_acc):
        i = pl.program_id(0)

        @pl.when(i == 0)
        def _():
            lacc[...] = jnp.zeros(lacc.shape, F32)
            pg_acc[...] = jnp.zeros(pg_acc.shape, F32)
            ple_acc[...] = jnp.zeros(ple_acc.shape, F32)
            dg3_ref[...] = jnp.zeros(dg3_ref.shape, F32)
            dg4_ref[...] = jnp.zeros(dg4_ref.shape, F32)

        h2t = h2_ref[...]
        g3, g4 = g3_ref[...], g4_ref[...]
        r3, n3 = _rms(h2t)
        hn3 = (n3 * g3).astype(BF16)
        pb = p_ref[...].astype(BF16)
        gs = jax.nn.sigmoid(jnp.dot(hn3, wpg_ref[...], preferred_element_type=F32))
        pe = jnp.dot(pb, wple_ref[...], preferred_element_type=F32)
        h3 = h2t + gs * pe
        r4, n4 = _rms(h3)
        err = n4 * g4 - t_ref[...]
        lacc[...] += _colsum(err * err)
        dy = err * (1.0 / D_MODEL)
        dg4_ref[...] += _colsum(dy * n4)
        dh3 = _rms_bwd(r4, n4, g4, dy)
        dpe = (dh3 * gs).astype(BF16)
        dgl = (dh3 * pe * gs * (1.0 - gs)).astype(BF16)
        ple_acc[...] += lax.dot_general(pb, dpe, TN, preferred_element_type=F32)
        pg_acc[...] += lax.dot_general(hn3, dgl, TN, preferred_element_type=F32)
        dhn3 = lax.dot_general(dgl, wpg_ref[...], NT, preferred_element_type=F32)
        dg3_ref[...] += _colsum(dhn3 * n3)
        dh2 = dh3 + _rms_bwd(r3, n3, g3, dhn3)
        dh2_ref[...] = dh2
        dh2b_ref[...] = dh2.astype(BF16)

        @pl.when(i == nt - 1)
        def _():
            tot = 0.5 / D_MODEL * jnp.sum(lacc[...], axis=-1, keepdims=True)
            loss_ref[...] = jnp.broadcast_to(tot, loss_ref.shape)
            dwpg_ref[...] = pg_acc[...].astype(BF16)
            dwple_ref[...] = ple_acc[...].astype(BF16)

    tok = lambda w: pl.BlockSpec((TM, w), lambda i: (i, 0))
    row = pl.BlockSpec((1, D_MODEL), lambda i: (0, 0))
    act = lambda dt: jax.ShapeDtypeStruct((S, D_MODEL), dt)
    whole = lambda r: pl.BlockSpec((r, D_MODEL), lambda i: (0, 0))
    return pl.pallas_call(
        body, name="head", grid=(nt,),
        in_specs=[tok(D_MODEL), tok(PLE_DIM), _const((1, D_MODEL)), _const(w_pg.shape), _const(w_ple.shape),
                  _const((1, D_MODEL)), tok(D_MODEL)],
        out_specs=[pl.BlockSpec((8, LANES), lambda i: (0, 0)), tok(D_MODEL), tok(D_MODEL), whole(D_MODEL),
                   whole(PLE_DIM), row, row],
        out_shape=[jax.ShapeDtypeStruct((8, LANES), F32), act(F32), act(BF16),
                   jax.ShapeDtypeStruct((D_MODEL, D_MODEL), BF16), jax.ShapeDtypeStruct((PLE_DIM, D_MODEL), BF16),
                   jax.ShapeDtypeStruct((1, D_MODEL), F32), jax.ShapeDtypeStruct((1, D_MODEL), F32)],
        scratch_shapes=[pltpu.VMEM((1, D_MODEL), F32), pltpu.VMEM((D_MODEL, D_MODEL), F32),
                        pltpu.VMEM((PLE_DIM, D_MODEL), F32)],
        compiler_params=_params("arbitrary"),
    )(h2, p, g3, w_pg, w_ple, g4, target)


def _wgrad(name, x, dy, x_kind, dy_kind, nj, k_dim, n_dim, grads, scatter, tk=TK):
    S = x.shape[-2]
    nt = S // tk
    ng = len(grads)

    def spec(kind, width):
        if kind == "full":
            return pl.BlockSpec((tk, width), lambda j, t: (t, 0))
        return pl.BlockSpec((None, tk, width), lambda j, t: (j, t, 0))

    def body(x_ref, dy_ref, *rest):
        o_ref, acc = rest[ng], rest[2 * ng + 1]
        j, t = pl.program_id(0), pl.program_id(1)
        copies = _exchange_copies(rest[:ng], rest[ng + 1:2 * ng + 1], scatter, *rest[2 * ng + 2:])

        @pl.when((j == 0) & (t == 0))
        def _():
            for cp in copies:
                cp.start()

        @pl.when(t == 0)
        def _():
            acc[...] = jnp.zeros(acc.shape, F32)

        acc[...] += lax.dot_general(x_ref[...].astype(BF16), dy_ref[...], TN, preferred_element_type=F32)

        @pl.when(t == nt - 1)
        def _():
            o_ref[...] = acc[...].astype(BF16)

        @pl.when((j == nj - 1) & (t == nt - 1))
        def _():
            for cp in copies:
                cp.wait()

    res = pl.pallas_call(
        body, name=name, grid=(nj, nt),
        in_specs=[spec(x_kind, k_dim), spec(dy_kind, n_dim)] + [ANY_SPEC] * ng,
        out_specs=[pl.BlockSpec((None, k_dim, n_dim), lambda j, t: (j, 0, 0))] + [ANY_SPEC] * ng,
        out_shape=[jax.ShapeDtypeStruct((nj, k_dim, n_dim), BF16)] + _exchange_shapes(grads, scatter),
        scratch_shapes=[pltpu.VMEM((k_dim, n_dim), F32)] + _exchange_sems(ng),
        compiler_params=_params("arbitrary", "arbitrary"),
    )(x, dy, *grads)
    return res[0], res[1:]


def _row_picker(T, off0, off1):
    r = lax.broadcasted_iota(jnp.int32, (2 * T, T + HALO), 0)
    c = lax.broadcasted_iota(jnp.int32, (2 * T, T + HALO), 1)
    want = jnp.where(r < T, r + off0, r - T + off1)
    return jnp.where(c == want, 1.0, 0.0).astype(BF16)


def _ffn_bwd_a(dh2b, up, upc, w_down, grads):
    S = dh2b.shape[0]
    T = TM_FF
    hb = T // HALO
    nsteps = S // T
    ng = len(grads)

    def body(dh_ref, up_ref, halo_ref, upc_ref, wd_ref, *rest):
        dwd_ref, dup_ref, dcw_ref, dcb_ref = rest[ng:ng + 4]
        stage, dwd_acc = rest[2 * ng + 4:2 * ng + 6]
        i = pl.program_id(0)
        copies = _exchange_copies(rest[:ng], rest[ng + 4:2 * ng + 4], (True,) * ng, *rest[2 * ng + 6:])

        @pl.when(i == 0)
        def _():
            dcw_ref[...] = jnp.zeros(dcw_ref.shape, F32)
            dcb_ref[...] = jnp.zeros(dcb_ref.shape, F32)
            dwd_acc[...] = jnp.zeros(dwd_acc.shape, F32)
            for cp in copies:
                cp.start()

        @pl.when(i == nsteps - 1)
        def _():
            for cp in copies:
                cp.wait()

        dh = dh_ref[...]
        pick = _row_picker(T, HALO - 2, HALO - 1)
        for j in range(4):
            da = lax.dot_general(dh, wd_ref[j], NT, preferred_element_type=F32)
            taps = []
            for jj in (j, j + 4):
                upre = up_ref[jj]
                stage[0:HALO, :] = jnp.where(i > 0, halo_ref[jj], jnp.zeros((HALO, FF_SHARD), BF16))
                stage[HALO:HALO + T, :] = upre
                prv = jnp.dot(pick, stage[...], preferred_element_type=F32)
                taps.append((prv[0:T], prv[T:2 * T], upre.astype(F32)))
            gate, val = upc_ref[j].astype(F32), upc_ref[j + 4].astype(F32)
            sg = jax.nn.sigmoid(gate)
            silu = gate * sg
            dwd_acc[j] += lax.dot_general((silu * val).astype(BF16), dh, TN, preferred_element_type=F32)
            dgate = (da * val) * (sg + silu * (1.0 - sg))
            dval = da * silu
            for jj, dup, tp in ((j, dgate, taps[0]), (j + 4, dval, taps[1])):
                dup_ref[jj] = dup.astype(BF16)
                dcb_ref[jj] += _colsum(dup)
                for kk in range(3):
                    dcw_ref[jj, kk:kk + 1, :] += _colsum(dup * tp[kk])

        @pl.when(i == nsteps - 1)
        def _():
            dwd_ref[...] = dwd_acc[...].astype(BF16)

    tok = lambda w: pl.BlockSpec((T, w), lambda i: (i, 0))
    shard = pl.BlockSpec((N_DEV, T, FF_SHARD), lambda i: (0, i, 0))
    res = pl.pallas_call(
        body, name="ffn_bwd_a", grid=(nsteps,),
        in_specs=[tok(D_MODEL), shard,
                  pl.BlockSpec((N_DEV, HALO, FF_SHARD), lambda i: (0, jnp.maximum(i * hb - 1, 0), 0)),
                  shard, _const(w_down.shape)] + [ANY_SPEC] * ng,
        out_specs=[_const(w_down.shape), shard,
                   pl.BlockSpec((N_DEV, 3, FF_SHARD), lambda i: (0, 0, 0)),
                   pl.BlockSpec((N_DEV, 1, FF_SHARD), lambda i: (0, 0, 0))] + [ANY_SPEC] * ng,
        out_shape=[jax.ShapeDtypeStruct(w_down.shape, BF16), jax.ShapeDtypeStruct((N_DEV, S, FF_SHARD), BF16),
                   jax.ShapeDtypeStruct((N_DEV, 3, FF_SHARD), F32), jax.ShapeDtypeStruct((N_DEV, 1, FF_SHARD), F32)]
        + _exchange_shapes(grads, (True,) * ng),
        scratch_shapes=[pltpu.VMEM((T + HALO, FF_SHARD), BF16), pltpu.VMEM(w_down.shape, F32)] + _exchange_sems(ng),
        compiler_params=_params("arbitrary"),
    )(dh2b, up, up, upc, w_down, *grads)
    return res[0], res[1], res[2], res[3], res[4:]


def _ffn_bwd_b(dup, conv_w, w_up, h1, g2, dh2, grads):
    S = h1.shape[0]
    T = TM_FF
    hb = T // HALO
    nt = S // T
    ng = len(grads)

    def body(dup_ref, halo_ref, cw_ref, wu_ref, h1_ref, g_ref, dh2_ref, *rest):
        dpre_ref, dh1_ref, dh1b_ref, dg_ref = rest[ng:ng + 4]
        stage = rest[2 * ng + 4]
        i = pl.program_id(0)
        copies = _exchange_copies(rest[:ng], rest[ng + 4:2 * ng + 4], (True,) * ng, *rest[2 * ng + 5:])

        @pl.when(i == 0)
        def _():
            dg_ref[...] = jnp.zeros(dg_ref.shape, F32)
            for cp in copies:
                cp.start()

        dhn = jnp.zeros((T, D_MODEL), F32)
        for jj in range(N_DEV):
            dup = dup_ref[jj].astype(F32)
            stage[0:T, :] = dup
            stage[T:T + HALO, :] = jnp.where(i < nt - 1, halo_ref[jj].astype(F32), 0.0)
            cw = cw_ref[jj]
            dpre = (cw[2:3, :] * dup + cw[1:2, :] * stage[1:1 + T, :] + cw[0:1, :] * stage[2:2 + T, :]).astype(BF16)
            dpre_ref[jj] = dpre
            dhn = dhn + lax.dot_general(dpre, wu_ref[jj], NT, preferred_element_type=F32)
        g = g_ref[...]
        r, n = _rms(h1_ref[...])
        dg_ref[...] += _colsum(dhn * n)
        dh1 = dh2_ref[...] + _rms_bwd(r, n, g, dhn)
        dh1_ref[...] = dh1
        dh1b_ref[...] = dh1.astype(BF16)

        @pl.when(i == nt - 1)
        def _():
            for cp in copies:
                cp.wait()

    tok = lambda w: pl.BlockSpec((T, w), lambda i: (i, 0))
    shard = pl.BlockSpec((N_DEV, T, FF_SHARD), lambda i: (0, i, 0))
    res = pl.pallas_call(
        body, name="ffn_bwd_b", grid=(nt,),
        in_specs=[shard,
                  pl.BlockSpec((N_DEV, HALO, FF_SHARD), lambda i: (0, jnp.minimum((i + 1) * hb, S // HALO - 1), 0)),
                  _const(conv_w.shape), _const(w_up.shape), tok(D_MODEL), _const((1, D_MODEL)), tok(D_MODEL)]
        + [ANY_SPEC] * ng,
        out_specs=[shard, tok(D_MODEL), tok(D_MODEL), pl.BlockSpec((1, D_MODEL), lambda i: (0, 0))] + [ANY_SPEC] * ng,
        out_shape=[jax.ShapeDtypeStruct((N_DEV, S, FF_SHARD), BF16), jax.ShapeDtypeStruct((S, D_MODEL), F32),
                   jax.ShapeDtypeStruct((S, D_MODEL), BF16), jax.ShapeDtypeStruct((1, D_MODEL), F32)]
        + _exchange_shapes(grads, (True,) * ng),
        scratch_shapes=[pltpu.VMEM((T + HALO, FF_SHARD), F32)] + _exchange_sems(ng),
        compiler_params=_params("arbitrary"),
    )(dup, dup, conv_w, w_up, h1, g2, dh2, *grads)
    return res[0], res[1], res[2], res[3], res[4:]


def _mix_bwd(dh1b, w_out, dlt, mix, pool_w, pool_scale):
    S = dh1b.shape[0]
    nt = S // TM

    def body(dh_ref, w_ref, dlt_ref, mix_ref, pw_ref, ps_ref, datt_ref, du_ref, dpw_ref, dps_ref, dwo_ref,
             stage, carry, wo_acc):
        i = pl.program_id(0)
        tile = nt - 1 - i

        @pl.when(i == 0)
        def _():
            dpw_ref[...] = jnp.zeros(dpw_ref.shape, F32)
            dps_ref[...] = jnp.zeros(dps_ref.shape, F32)
            carry[...] = jnp.zeros(carry.shape, F32)
            wo_acc[...] = jnp.zeros(wo_acc.shape, F32)

        wo_acc[...] += lax.dot_general(mix_ref[...], dh_ref[...], TN, preferred_element_type=F32)

        @pl.when(i == nt - 1)
        def _():
            dwo_ref[...] = wo_acc[...].astype(BF16)

        dmix = lax.dot_general(dh_ref[...], w_ref[...], NT, preferred_element_type=F32)
        datt_ref[...] = dmix[:, 0:ATT_WIDTH]
        for g, w in enumerate(POOL_WINDOWS):
            cols = slice(g * POOL_GROUP, (g + 1) * POOL_GROUP)
            dpool = dmix[:, ATT_WIDTH + g * POOL_GROUP:ATT_WIDTH + (g + 1) * POOL_GROUP]
            dl = dlt_ref[:, cols]
            pw = pw_ref[g].astype(BF16)
            yg = jnp.dot(dl, pw, preferred_element_type=F32)
            dps_ref[:, cols] += _colsum(dpool * yg)
            dy = (dpool * ps_ref[:, cols]).astype(BF16)
            dpw_ref[g] += lax.dot_general(dl, dy, TN, preferred_element_type=F32)
            ddlt = lax.dot_general(dy, pw, NT, preferred_element_type=F32)
            cg = ddlt / _pool_count(tile, w)
            stage[0:TM, :] = cg
            stage[TM:TM + HALO, :] = carry[:, cols]
            acc = cg
            for j in range(1, w):
                acc = acc + stage[j:j + TM, :]
            du_ref[:, cols] = acc - ddlt
            carry[:, cols] = cg[0:HALO, :]

    tok = lambda w: pl.BlockSpec((TM, w), lambda i: (nt - 1 - i, 0))
    return pl.pallas_call(
        body, name="mix_bwd", grid=(nt,),
        in_specs=[tok(D_MODEL), _const(w_out.shape), tok(POOL_WIDTH), tok(D_MODEL), _const(pool_w.shape),
                  _const((1, POOL_WIDTH))],
        out_specs=[tok(ATT_WIDTH), tok(POOL_WIDTH), pl.BlockSpec(pool_w.shape, lambda i: (0, 0, 0)),
                   pl.BlockSpec((1, POOL_WIDTH), lambda i: (0, 0)), pl.BlockSpec(w_out.shape, lambda i: (0, 0))],
        out_shape=[jax.ShapeDtypeStruct((S, ATT_WIDTH), F32), jax.ShapeDtypeStruct((S, POOL_WIDTH), F32),
                   jax.ShapeDtypeStruct(pool_w.shape, F32), jax.ShapeDtypeStruct((1, POOL_WIDTH), F32),
                   jax.ShapeDtypeStruct(w_out.shape, BF16)],
        scratch_shapes=[pltpu.VMEM((TM + HALO, POOL_GROUP), F32), pltpu.VMEM((HALO, POOL_WIDTH), F32),
                        pltpu.VMEM(w_out.shape, F32)],
        compiler_params=_params("arbitrary"),
    )(dh1b, w_out, dlt, mix, pool_w, pool_scale)


def _attn_bwd(slopes, q, k, v, o, lse, do, grads, scatter):
    S = q.shape[0]
    CH = 512
    ng = len(grads)
    steps = ATT_WIDTH // LANES

    def body(slope_ref, q_ref, k_ref, v_ref, o_ref, lse_ref, do_ref, *rest):
        dq_ref, dk_ref, dv_ref = rest[ng:ng + 3]
        dl_s, bias_s = rest[2 * ng + 3:2 * ng + 5]
        hp = pl.program_id(0)
        copies = _exchange_copies(rest[:ng], rest[ng + 3:2 * ng + 3], scatter, *rest[2 * ng + 5:])

        @pl.when(hp == 0)
        def _():
            for cp in copies:
                cp.start()

        is0 = lax.broadcasted_iota(jnp.int32, (SPAN, LANES), 1) < HEAD_DIM
        is0c = lax.broadcasted_iota(jnp.int32, (CH, LANES), 1) < HEAD_DIM

        def prep(ci, carry):
            rows = pl.ds(pl.multiple_of(ci * CH, CH), CH)
            prod = do_ref[rows, :] * o_ref[rows, :]
            d0 = jnp.sum(jnp.where(is0c, prod, 0.0), axis=-1, keepdims=True)
            d1 = jnp.sum(jnp.where(is0c, 0.0, prod), axis=-1, keepdims=True)
            dl_s[rows, :] = jnp.where(is0c, d0, d1)
            zero = jnp.zeros((CH, LANES), F32)
            dq_ref[rows, :] = zero
            dk_ref[rows, :] = zero
            dv_ref[rows, :] = zero
            return carry

        lax.fori_loop(0, S // CH, prep, 0)

        for d in DILATIONS:
            nb, ngroups = _attn_groups(S, d, ATTN_GROUP_BWD, writes_key_rows=True)
            _attn_fill_bias(bias_s, slope_ref, hp, d)

            def group(i, carry, d=d, nb=nb):
                blocks = [_attn_block(i, g, d, nb, ATTN_GROUP_BWD) for g in range(ATTN_GROUP_BWD)]
                loaded = [(q_ref[rows, :], do_ref[rows, :], lse_ref[rows, :], dl_s[rows, :], k_ref[krows, :],
                           v_ref[krows, :].astype(BF16)) for rows, krows, _ in blocks]
                new = []
                for (rows, krows, tab), (qb, dob, lse_b, dl_b, kf, vb) in zip(blocks, loaded):
                    kb = kf.astype(BF16)
                    qs = _stack_heads(qb, is0).astype(BF16)
                    dos = _stack_heads(dob, is0).astype(BF16)
                    lse_s = jnp.concatenate([lse_b[:, 0:1], lse_b[:, HEAD_DIM:HEAD_DIM + 1]], axis=0)
                    dl_s2 = jnp.concatenate([dl_b[:, 0:1], dl_b[:, HEAD_DIM:HEAD_DIM + 1]], axis=0)
                    s = lax.dot_general(qs, kb, NT, preferred_element_type=F32) + bias_s[tab]
                    pr = jnp.exp(s - lse_s)
                    dp = lax.dot_general(dos, vb, NT, preferred_element_type=F32)
                    ds = (pr * (dp - dl_s2)).astype(BF16)
                    dv_c = lax.dot_general(pr.astype(BF16), dos, TN, preferred_element_type=F32)
                    dk_c = lax.dot_general(ds, qs, TN, preferred_element_type=F32)
                    dq_c = _unstack_heads(jnp.dot(ds, kb, preferred_element_type=F32), is0)
                    new.append((dq_c, dk_c, dv_c))
                old = [(dq_ref[rows, :], dk_ref[krows, :], dv_ref[krows, :]) for rows, krows, _ in blocks]
                for (rows, krows, _), (dq_c, dk_c, dv_c), (dq_o, dk_o, dv_o) in zip(blocks, new, old):
                    dq_ref[rows, :] = dq_o + dq_c
                    dk_ref[krows, :] = dk_o + dk_c
                    dv_ref[krows, :] = dv_o + dv_c
                return carry

            lax.fori_loop(0, ngroups, group, 0)

        @pl.when(hp == steps - 1)
        def _():
            for cp in copies:
                cp.wait()

    col = pl.BlockSpec((S, LANES), lambda i: (0, i))
    res = pl.pallas_call(
        body, name="attn_bwd", grid=(steps,),
        in_specs=[pl.BlockSpec(memory_space=pltpu.SMEM)] + [col] * 6 + [ANY_SPEC] * ng,
        out_specs=[col] * 3 + [ANY_SPEC] * ng,
        out_shape=[jax.ShapeDtypeStruct((S, ATT_WIDTH), F32)] * 3 + _exchange_shapes(grads, scatter),
        scratch_shapes=[pltpu.VMEM((S, LANES), F32), pltpu.VMEM((2, 2 * SPAN, 2 * SPAN), F32)] + _exchange_sems(ng),
        compiler_params=_params("arbitrary"),
    )(slopes, q, k, v, o, lse, do, *grads)
    return res[0], res[1], res[2], res[3:]


def _in_bwd(dq, dk, dv, du, w_in, x, g1, dh1):
    S = x.shape[0]

    def body(dq_ref, dk_ref, dv_ref, du_ref, w_ref, x_ref, g_ref, dh1_ref, dz_ref, dx_ref, dg_ref):
        @pl.when(pl.program_id(0) == 0)
        def _():
            dg_ref[...] = jnp.zeros(dg_ref.shape, F32)

        srcs = (dq_ref, dk_ref, dv_ref, du_ref)
        dhn = jnp.zeros((TM, D_MODEL), F32)
        for j in range(N_DEV):
            dz = srcs[j // 2][:, (j % 2) * 256:(j % 2 + 1) * 256]
            if j < 2:
                dz = dz * (HEAD_DIM ** -0.5)
            dz = dz.astype(BF16)
            dz_ref[j] = dz
            dhn = dhn + lax.dot_general(dz, w_ref[j], NT, preferred_element_type=F32)
        g = g_ref[...]
        r, n = _rms(x_ref[...])
        dg_ref[...] += _colsum(dhn * n)
        dx_ref[...] = dh1_ref[...] + _rms_bwd(r, n, g, dhn)

    tok = lambda w: pl.BlockSpec((TM, w), lambda i: (i, 0))
    return pl.pallas_call(
        body, name="in_bwd", grid=(S // TM,),
        in_specs=[tok(512)] * 4 + [_const(w_in.shape), tok(D_MODEL), _const((1, D_MODEL)), tok(D_MODEL)],
        out_specs=[pl.BlockSpec((N_DEV, TM, 256), lambda i: (0, i, 0)), tok(D_MODEL),
                   pl.BlockSpec((1, D_MODEL), lambda i: (0, 0))],
        out_shape=[jax.ShapeDtypeStruct((N_DEV, S, 256), BF16), jax.ShapeDtypeStruct((S, D_MODEL), F32),
                   jax.ShapeDtypeStruct((1, D_MODEL), F32)],
        compiler_params=_params("arbitrary"),
    )(dq, dk, dv, du, w_in, x, g1, dh1)


def _adamw(name, parts, w, m, v):
    R, C = w.shape
    rb = R
    for cand in (256, 128, 64, 32, 16, 8):
        if R % cand == 0 and R > cand:
            rb = cand
            break

    def body(p_ref, w_ref, m_ref, v_ref, g_ref, d_ref, mo_ref, vo_ref):
        g = p_ref[0].astype(F32)
        for s in range(1, N_DEV):
            g = g + p_ref[s].astype(F32)
        g_ref[...] = g
        d_ref[...], mo_ref[...], vo_ref[...] = _adam_update(g, w_ref[...], m_ref[...], v_ref[...])

    blk = pl.BlockSpec((rb, C), lambda i: (i, 0))
    return pl.pallas_call(
        body, name=name, grid=(R // rb,),
        in_specs=[pl.BlockSpec((N_DEV, rb, C), lambda i: (0, i, 0)), blk, blk, blk],
        out_specs=[blk] * 4,
        out_shape=[jax.ShapeDtypeStruct((R, C), F32)] * 4,
        compiler_params=_params("arbitrary"),
    )(parts, w, m, v)


def _adam_update(g, w, m, v):
    m_new = ADAM_B1 * m + (1.0 - ADAM_B1) * g
    v_new = ADAM_B2 * v + (1.0 - ADAM_B2) * (g * g)
    m_hat = m_new / (1.0 - ADAM_B1 ** ADAM_STEP)
    v_hat = v_new / (1.0 - ADAM_B2 ** ADAM_STEP)
    return -ADAM_LR * (m_hat / (jnp.sqrt(v_hat) + ADAM_EPS) + ADAM_WD * w), m_new, v_new


def _adamw_small(parts, loss_parts, ws, ms, vs):
    n = len(ws)

    def body(*refs):
        p_refs, lp_ref = refs[:n], refs[n]
        w_refs, m_refs, v_refs = refs[n + 1:2 * n + 1], refs[2 * n + 1:3 * n + 1], refs[3 * n + 1:4 * n + 1]
        outs = refs[4 * n + 1:]
        for i in range(n):
            g = p_refs[i][0]
            for s in range(1, N_DEV):
                g = g + p_refs[i][s]
            d, m_new, v_new = _adam_update(g, w_refs[i][...], m_refs[i][...], v_refs[i][...])
            outs[i][...] = g
            outs[n + i][...] = d
            outs[2 * n + i][...] = m_new
            outs[3 * n + i][...] = v_new
        tot = lp_ref[0]
        for s in range(1, N_DEV):
            tot = tot + lp_ref[s]
        outs[4 * n][...] = tot

    shapes = [jax.ShapeDtypeStruct(w.shape, F32) for w in ws]
    res = pl.pallas_call(
        body, name="adamw_replicated",
        out_shape=shapes * 4 + [jax.ShapeDtypeStruct(loss_parts.shape[1:], F32)],
        compiler_params=_params(),
    )(*parts, loss_parts, *ws, *ms, *vs)
    return res[:n], res[n:2 * n], res[2 * n:3 * n], res[3 * n:4 * n], res[4 * n]


def _gather2(name, arrays):
    n = len(arrays)

    def body(*refs):
        first, passed, last = _gather2_copies(refs[:n], refs[n:2 * n], *refs[2 * n:])
        for cp in first:
            cp.start()
        for arrival, cp in passed:
            arrival.wait_recv()
            cp.start()
        for wait in last:
            wait()

    return pl.pallas_call(
        body, name=name,
        in_specs=[ANY_SPEC] * n, out_specs=[ANY_SPEC] * n, out_shape=_exchange_shapes(arrays, (False,) * n),
        scratch_shapes=_exchange_sems(n),
    )(*arrays)


def _dw_in_exchange(hn, dz, small):
    S = hn.shape[0]
    nt = S // TK
    ns = len(small)
    kd, nd = hn.shape[1], dz.shape[2]
    me_arr = (4 * lax.axis_index("x") + 2 * lax.axis_index("y") + lax.axis_index("c")).astype(jnp.int32).reshape(1)

    def body(me_ref, x_ref, dy_ref, *rest):
        recv_ref = rest[ns]
        acc, stage, send_sems, recv_sems, own_sem = rest[2 * ns + 1:2 * ns + 6]
        j, t = pl.program_id(0), pl.program_id(1)
        x, y, c = lax.axis_index("x"), lax.axis_index("y"), lax.axis_index("c")
        me = 4 * x + 2 * y + c
        small_copies = _exchange_copies(rest[:ns], rest[ns + 1:2 * ns + 1], (False,) * ns, *rest[2 * ns + 6:])

        @pl.when((j == 0) & (t == 0))
        def _():
            for cp in small_copies:
                cp.start()

        @pl.when(t == 0)
        def _():
            acc[...] = jnp.zeros(acc.shape, F32)

        acc[...] += lax.dot_general(x_ref[...], dy_ref[...], TN, preferred_element_type=F32)

        def to_owner(k, owner):
            return pltpu.make_async_remote_copy(
                src_ref=stage.at[owner], dst_ref=recv_ref.at[me], send_sem=send_sems.at[k], recv_sem=recv_sems.at[k],
                device_id=(owner // 4, (owner // 2) % 2, owner % 2), device_id_type=MESH)

        own = pltpu.make_async_copy(stage.at[me], recv_ref.at[me], own_sem)

        @pl.when(t == nt - 1)
        def _():
            owner = (me + 1 + j) % N_DEV
            stage[owner] = acc[...].astype(BF16)

            @pl.when(j < N_DEV - 1)
            def _():
                to_owner(j, owner).start()

            @pl.when(j == N_DEV - 1)
            def _():
                own.start()
                own.wait()
                for k in range(N_DEV - 1):
                    to_owner(k, me).wait_send()
                    to_owner(k, me).wait_recv()
                for cp in small_copies:
                    cp.wait()

    slab = lambda j, me_ref: (me_ref[0] + 1 + j) % N_DEV
    grid_spec = pltpu.PrefetchScalarGridSpec(
        num_scalar_prefetch=1, grid=(N_DEV, nt),
        in_specs=[pl.BlockSpec((TK, kd), lambda j, t, me_ref: (t, 0)),
                  pl.BlockSpec((None, TK, nd), lambda j, t, me_ref: (slab(j, me_ref), t, 0))] + [ANY_SPEC] * ns,
        out_specs=[ANY_SPEC] * (ns + 1),
        scratch_shapes=[pltpu.VMEM((kd, nd), F32), pltpu.VMEM((N_DEV, kd, nd), BF16),
                        pltpu.SemaphoreType.DMA((N_DEV - 1,)), pltpu.SemaphoreType.DMA((N_DEV - 1,)),
                        pltpu.SemaphoreType.DMA] + _exchange_sems(ns))
    res = pl.pallas_call(
        body, name="dw_in_exchange", grid_spec=grid_spec,
        out_shape=[jax.ShapeDtypeStruct((N_DEV, kd, nd), BF16)] + _exchange_shapes(small, (False,) * ns),
        compiler_params=_params("arbitrary", "arbitrary"),
    )(me_arr, hn, dz, *small)
    return res[0], res[1:]


def kernel(x, p, ln_mix, w_in, pool_w, pool_scale, w_out, ln_ffn, w_up, conv_w, conv_b, w_down, ln_ple, w_ple_gate, w_ple, ln_final, loss_target, m_ln_mix, m_w_in, m_pool_w, m_pool_scale, m_w_out, m_ln_ffn, m_w_up, m_conv_w, m_conv_b, m_w_down, m_ln_ple, m_w_ple_gate, m_w_ple, m_ln_final, v_ln_mix, v_w_in, v_pool_w, v_pool_scale, v_w_out, v_ln_ffn, v_w_up, v_conv_w, v_conv_b, v_w_down, v_ln_ple, v_w_ple_gate, v_w_ple, v_ln_final):
    xs, ps, tgt, pool_w0 = x[0], p[0, 0], loss_target[0], pool_w[0]
    slopes = jnp.exp2(-8.0 * (jnp.arange(N_HEADS, dtype=F32) + 1.0) / N_HEADS)
    conv_b_s = conv_b.reshape(N_DEV, 1, FF_SHARD)

    (w_in_g,) = _gather2("gather_w_in", [w_in[0].astype(BF16)])
    w_down_b = w_down[0].astype(BF16)
    (q, k, v, u, hn1), (w_out_g, w_down_g0) = _qkvu(xs, ln_mix, w_in_g, [w_out[0].astype(BF16), w_down_b[:W_DOWN_CUT]])
    att, lse, (w_up_g, conv_w_g) = _attn_fwd(slopes, q, k, v, [w_up[0].astype(BF16), conv_w[0]])
    w_out_f = w_out_g.reshape(D_MODEL, D_MODEL)
    h1, mix, dlt, (w_down_g1,) = _mix_out(xs, att, u, pool_w0, pool_scale, w_out_f, [w_down_b[W_DOWN_CUT:]])
    w_down_f = jnp.concatenate([w_down_g0, w_down_g1], axis=1).reshape(4, FF_SHARD, D_MODEL)
    h2, hn2, up, upc, (w_pg_g, w_ple_g) = _ffn_fwd(h1, ln_ffn, w_up_g, conv_w_g, conv_b_s, w_down_f,
                                                   [w_ple_gate[0].astype(BF16), w_ple[0].astype(BF16)])
    w_pg_f = w_pg_g.reshape(D_MODEL, D_MODEL)
    w_ple_f = jnp.transpose(w_ple_g, (1, 0, 2)).reshape(PLE_DIM, D_MODEL)
    loss_blk, dh2, dh2b, d_w_pg, d_w_ple, d_ln_ple, d_ln_final = _head(
        h2, ps, ln_ple, w_pg_f, w_ple_f, ln_final.reshape(1, D_MODEL), tgt)

    d_w_pg = d_w_pg.reshape(N_DEV, D_MODEL // N_DEV, D_MODEL)
    d_w_ple = jnp.transpose(d_w_ple.reshape(PLE_DIM, N_DEV, LANES), (1, 0, 2))
    d_w_down, dup, d_conv_w, d_conv_b, (r_w_pg, r_w_ple) = _ffn_bwd_a(dh2b, up, upc, w_down_f, [d_w_pg, d_w_ple])
    d_w_down = d_w_down.reshape(N_DEV, D_FF // N_DEV, D_MODEL)
    dpre, dh1, dh1b, d_ln_ffn, (r_conv_w, r_w_down) = _ffn_bwd_b(
        dup, conv_w_g, w_up_g, h1, ln_ffn, dh2, [d_conv_w, d_w_down])
    datt, du, d_pool_w, d_pool_scale, d_w_out = _mix_bwd(dh1b, w_out_f, dlt, mix, pool_w0, pool_scale)
    d_w_out = d_w_out.reshape(N_DEV, D_MODEL // N_DEV, D_MODEL)
    rep_late = [d_pool_w, d_pool_scale, d_ln_ffn, d_conv_b.reshape(1, 2 * D_FF), d_ln_ple, d_ln_final, loss_blk]
    d_w_up, received = _wgrad("dw_up", dpre, hn2, "lead", "full", N_DEV, FF_SHARD, D_MODEL,
                              [d_w_out] + rep_late, (True,) + (False,) * len(rep_late))
    r_w_out, r_rep = received[0], list(received[1:])
    dq, dk, dv, (r_w_up,) = _attn_bwd(slopes, q, k, v, att, lse, datt, [d_w_up], (True,))
    dz, grad_x, d_ln_mix = _in_bwd(dq, dk, dv, du, w_in_g, xs, ln_mix, dh1)

    rep_names = ("ln_mix", "pool_w", "pool_scale", "ln_ffn", "conv_b", "ln_ple", "ln_final")
    rep_w = [ln_mix, pool_w0, pool_scale, ln_ffn, conv_b, ln_ple, ln_final.reshape(1, D_MODEL)]
    rep_m = [m_ln_mix, m_pool_w[0], m_pool_scale, m_ln_ffn, m_conv_b, m_ln_ple, m_ln_final.reshape(1, D_MODEL)]
    rep_v = [v_ln_mix, v_pool_w[0], v_pool_scale, v_ln_ffn, v_conv_b, v_ln_ple, v_ln_final.reshape(1, D_MODEL)]
    r_w_in, (r_ln_mix,) = _dw_in_exchange(hn1, dz, [d_ln_mix])
    small = _adamw_small([r_ln_mix] + r_rep[:-1], r_rep[-1], rep_w, rep_m, rep_v)
    loss = small[4][0, 0]

    sharded = {}
    sharded["w_in"] = _adamw("adamw_w_in", r_w_in, w_in[0], m_w_in[0], v_w_in[0])
    sharded["w_out"] = _adamw("adamw_w_out", r_w_out, w_out[0], m_w_out[0], v_w_out[0])
    sharded["w_up"] = [t.T for t in _adamw("adamw_w_up", r_w_up, w_up[0].T, m_w_up[0].T, v_w_up[0].T)]
    sharded["conv_w"] = _adamw("adamw_conv_w", r_conv_w, conv_w[0], m_conv_w[0], v_conv_w[0])
    sharded["w_down"] = _adamw("adamw_w_down", r_w_down, w_down[0], m_w_down[0], v_w_down[0])
    sharded["w_ple_gate"] = _adamw("adamw_w_ple_gate", r_w_pg, w_ple_gate[0], m_w_ple_gate[0], v_w_ple_gate[0])
    sharded["w_ple"] = _adamw("adamw_w_ple", r_w_ple, w_ple[0], m_w_ple[0], v_w_ple[0])

    shapes = dict(w_in=w_in, w_out=w_out, w_up=w_up, conv_w=conv_w, w_down=w_down, w_ple_gate=w_ple_gate, w_ple=w_ple,
                  ln_mix=ln_mix, pool_w=pool_w, pool_scale=pool_scale, ln_ffn=ln_ffn, conv_b=conv_b, ln_ple=ln_ple,
                  ln_final=ln_final)

    def leaf(kind, n):
        src = sharded[n][kind] if n in sharded else small[kind][rep_names.index(n)]
        return src.reshape(shapes[n].shape)

    order = ("ln_mix", "w_in", "pool_w", "pool_scale", "w_out", "ln_ffn", "w_up", "conv_w", "conv_b", "w_down", "ln_ple",
             "w_ple_gate", "w_ple", "ln_final")
    outs = [loss, grad_x[None]]
    for kind in range(4):
        outs += [leaf(kind, n) for n in order]
    return tuple(outs)
```

```python
import jax
import jax.numpy as jnp
from jax import lax
from jax.experimental import pallas as pl
from jax.experimental.pallas import tpu as pltpu

F32 = jnp.float32
BF16 = jnp.bfloat16

N_DEV = 8
D_MODEL = 1024
ATT_WIDTH = 512
POOL_WIDTH = 512
N_HEADS = 8
HEAD_DIM = 64
SPAN = 128
DILATIONS = (1, 4, 16)
POOL_WINDOWS = (2, 4, 8, 16)
POOL_GROUP = 128
D_FF = 2816
FF_SHARD = 2 * D_FF // N_DEV
PLE_DIM = 256
EPS = 1e-6
NEG = -1e30

ADAM_LR = 0.001
ADAM_B1 = 0.9
ADAM_B2 = 0.999
ADAM_EPS = 1e-08
ADAM_WD = 0.01
ADAM_STEP = 10

LANES = 128
HALO = 16
TM = 512
TM_FF = 256
TK = 4096
W_DOWN_CUT = 112
ATTN_GROUP_FWD = 16
ATTN_GROUP_BWD = 8
VMEM_LIMIT = 56 * 1024 * 1024

MESH = pl.DeviceIdType.MESH
NT = (((1,), (1,)), ((), ()))
TN = (((0,), (0,)), ((), ()))


def _params(*sem):
    return pltpu.CompilerParams(dimension_semantics=sem or None, vmem_limit_bytes=VMEM_LIMIT)


def _const(shape):
    n = len(shape)
    return pl.BlockSpec(shape, lambda *_: (0,) * n, pipeline_mode=pl.Buffered(1))


def _rms(h):
    r = lax.rsqrt(jnp.mean(h * h, axis=-1, keepdims=True) + EPS)
    return r, h * r


def _rms_bwd(r, n, g, dhn):
    dn = dhn * g
    return r * (dn - n * jnp.mean(dn * n, axis=-1, keepdims=True))


def _colsum(a):
    return jnp.sum(a, axis=0, keepdims=True)


def _gather2_copies(ins, outs, send_sems, recv_sems, local_sems):
    n = len(ins)
    x, y, c = lax.axis_index("x"), lax.axis_index("y"), lax.axis_index("c")
    slot = lambda px, py, pc: 4 * px + 2 * py + pc
    chips = [(x, 1 - y), (1 - x, y), (1 - x, 1 - y)]
    first, passed, last = [], [], []

    def remote(a, r, src, dst_slot, to):
        return pltpu.make_async_remote_copy(
            src_ref=src, dst_ref=outs[a].at[dst_slot],
            send_sem=send_sems.at[a * (N_DEV - 1) + r], recv_sem=recv_sems.at[a * (N_DEV - 1) + r],
            device_id=to, device_id_type=MESH)

    for a in range(n):
        mine = pltpu.make_async_copy(ins[a], outs[a].at[slot(x, y, c)], local_sems.at[a])
        to_sibling = remote(a, 0, ins[a], slot(x, y, c), (x, y, 1 - c))
        first += [mine, to_sibling]
        last += [mine.wait, to_sibling.wait_send, to_sibling.wait_recv]
        for r, (px, py) in enumerate(chips, start=1):
            to_chip = remote(a, r, ins[a], slot(x, y, c), (px, py, c))
            onward = remote(a, 3 + r, outs[a].at[slot(px, py, c)], slot(px, py, c), (x, y, 1 - c))
            first.append(to_chip)
            passed.append((to_chip, onward))
            last += [to_chip.wait_send, onward.wait_send, onward.wait_recv]
    return first, passed, last


def _gather2_begin(plan, step, pass_step):
    first, passed, _ = plan

    @pl.when(step == 0)
    def _():
        for cp in first:
            cp.start()

    @pl.when(step == pass_step)
    def _():
        for arrival, cp in passed:
            arrival.wait_recv()
            cp.start()


def _gather2_end(plan, step, nsteps):
    @pl.when(step == nsteps - 1)
    def _():
        for wait in plan[2]:
            wait()


ANY_SPEC = pl.BlockSpec(memory_space=pl.ANY)


def _exchange_shapes(arrays, scatter):
    out = []
    for a, s in zip(arrays, scatter):
        slab = a.shape[1:] if s else a.shape
        out.append(jax.ShapeDtypeStruct((N_DEV,) + tuple(slab), a.dtype))
    return out


def _exchange_sems(n):
    return [pltpu.SemaphoreType.DMA((n * (N_DEV - 1),)), pltpu.SemaphoreType.DMA((n * (N_DEV - 1),)),
            pltpu.SemaphoreType.DMA((n,))]


def _exchange_copies(ins, outs, scatter, send_sems, recv_sems, local_sems):
    n = len(ins)
    x, y, c = lax.axis_index("x"), lax.axis_index("y"), lax.axis_index("c")
    me = 4 * x + 2 * y + c
    copies = []
    for a in range(n):
        src = ins[a].at[me] if scatter[a] else ins[a]
        copies.append(pltpu.make_async_copy(src, outs[a].at[me], local_sems.at[a]))
    for k in range(1, N_DEV):
        px = 1 - x if k & 4 else x
        py = 1 - y if k & 2 else y
        pc = 1 - c if k & 1 else c
        pid = 4 * px + 2 * py + pc
        for a in range(n):
            src = ins[a].at[pid] if scatter[a] else ins[a]
            copies.append(pltpu.make_async_remote_copy(
                src_ref=src, dst_ref=outs[a].at[me],
                send_sem=send_sems.at[a * (N_DEV - 1) + k - 1], recv_sem=recv_sems.at[a * (N_DEV - 1) + k - 1],
                device_id=(px, py, pc), device_id_type=MESH))
    return copies


def _qkvu(x, g1, w_in, shards):
    S = x.shape[0]
    ns = len(shards)
    nsteps = S // TM

    def body(x_ref, g_ref, w_ref, *rest):
        q_ref, k_ref, v_ref, u_ref, hn_ref = rest[ns:ns + 5]
        plan = _gather2_copies(rest[:ns], rest[ns + 5:2 * ns + 5], *rest[2 * ns + 5:])
        _gather2_begin(plan, pl.program_id(0), nsteps - 2)
        r, n = _rms(x_ref[...])
        hn = (n * g_ref[...]).astype(BF16)
        hn_ref[...] = hn
        outs = (q_ref, k_ref, v_ref, u_ref)
        for j in range(N_DEV):
            z = jnp.dot(hn, w_ref[j], preferred_element_type=F32)
            if j < 2:
                z = z * (HEAD_DIM ** -0.5)
            outs[j // 2][:, (j % 2) * 256:(j % 2 + 1) * 256] = z
        _gather2_end(plan, pl.program_id(0), nsteps)

    tok = lambda w: pl.BlockSpec((TM, w), lambda i: (i, 0))
    res = pl.pallas_call(
        body, name="qkvu", grid=(nsteps,),
        in_specs=[tok(D_MODEL), _const((1, D_MODEL)), _const(w_in.shape)] + [ANY_SPEC] * ns,
        out_specs=[tok(512)] * 4 + [tok(D_MODEL)] + [ANY_SPEC] * ns,
        out_shape=[jax.ShapeDtypeStruct((S, 512), F32)] * 4 + [jax.ShapeDtypeStruct((S, D_MODEL), BF16)]
        + _exchange_shapes(shards, (False,) * ns),
        scratch_shapes=_exchange_sems(ns),
        compiler_params=_params("arbitrary"),
    )(x, g1, w_in, *shards)
    return res[:5], res[5:]


def _attn_fill_bias(bias_s, slope_ref, hp, d):
    qi = lax.broadcasted_iota(jnp.int32, (SPAN, 2 * SPAN), 0)
    kj = lax.broadcasted_iota(jnp.int32, (SPAN, 2 * SPAN), 1)
    for t, diff in enumerate((qi + SPAN - kj, qi - kj)):
        valid = (diff >= 0) & (diff <= SPAN)
        dist = diff.astype(F32) * float(d)
        for h in range(2):
            bias_s[t, h * SPAN:(h + 1) * SPAN, :] = jnp.where(valid, -slope_ref[2 * hp + h] * dist, NEG)


def _stack_heads(x, is0):
    return jnp.concatenate([jnp.where(is0, x, 0.0), jnp.where(is0, 0.0, x)], axis=0)


def _unstack_heads(y, is0):
    return jnp.where(is0, y[0:SPAN], y[SPAN:2 * SPAN])


def _attn_block(i, g, d, nb, group):
    gr = min(d, group)
    gn = group // gr
    per = d // gr
    r = (i & (per - 1)) * gr + g % gr
    n = (i >> (per.bit_length() - 1)) + (g // gr) * (nb // gn)
    k0 = jnp.maximum(n - 1, 0)

    def ds(block, nrows):
        start = block * (SPAN * d) + r
        return pl.ds(start, nrows, stride=d) if d > 1 else pl.ds(start, nrows)

    return ds(n, SPAN), ds(k0, 2 * SPAN), jnp.where(n == 0, 1, 0)


def _attn_groups(S, d, group, writes_key_rows=False):
    nb = S // d // SPAN
    gn = group // min(d, group)
    assert nb >= 2 and nb % gn == 0 and (gn == 1 or nb // gn >= (3 if writes_key_rows else 2))
    return nb, d * nb // group


def _attn_fwd(slopes, q, k, v, shards):
    S = q.shape[0]
    ns = len(shards)
    steps = ATT_WIDTH // LANES

    def body(slope_ref, q_ref, k_ref, v_ref, *rest):
        o_ref, lse_ref = rest[ns:ns + 2]
        m_s, l_s, bias_s = rest[2 * ns + 2:2 * ns + 5]
        hp = pl.program_id(0)
        plan = _gather2_copies(rest[:ns], rest[ns + 2:2 * ns + 2], *rest[2 * ns + 5:])
        _gather2_begin(plan, hp, steps - 1)

        is0 = lax.broadcasted_iota(jnp.int32, (SPAN, LANES), 1) < HEAD_DIM
        for pi, d in enumerate(DILATIONS):
            nb, ngroups = _attn_groups(S, d, ATTN_GROUP_FWD)
            _attn_fill_bias(bias_s, slope_ref, hp, d)

            def group(i, carry, d=d, pi=pi, nb=nb):
                blocks = [_attn_block(i, g, d, nb, ATTN_GROUP_FWD) for g in range(ATTN_GROUP_FWD)]
                loaded = [(q_ref[rows, :], k_ref[krows, :].astype(BF16), v_ref[krows, :].astype(BF16))
                          for rows, krows, _ in blocks]
                new = []
                for (rows, krows, tab), (qb, kb, vb) in zip(blocks, loaded):
                    qs = _stack_heads(qb, is0).astype(BF16)
                    s = lax.dot_general(qs, kb, NT, preferred_element_type=F32) + bias_s[tab]
                    m = jnp.max(s, axis=-1, keepdims=True)
                    e = jnp.exp(s - m)
                    l = jnp.sum(e, axis=-1, keepdims=True)
                    pv = jnp.dot(e.astype(BF16), vb, preferred_element_type=F32)
                    new.append([_unstack_heads(jnp.broadcast_to(m, pv.shape), is0),
                                _unstack_heads(jnp.broadcast_to(l, pv.shape), is0), _unstack_heads(pv, is0)])
                if pi > 0:
                    old = [(m_s[rows, :], l_s[rows, :], o_ref[rows, :]) for rows, _, _ in blocks]
                    for st, (m_o, l_o, o_o) in zip(new, old):
                        m_n = jnp.maximum(m_o, st[0])
                        a_o = jnp.exp(m_o - m_n)
                        a_b = jnp.exp(st[0] - m_n)
                        st[:] = [m_n, a_o * l_o + a_b * st[1], a_o * o_o + a_b * st[2]]
                for (rows, _, _), (m_b, l_b, acc) in zip(blocks, new):
                    if pi == len(DILATIONS) - 1:
                        o_ref[rows, :] = acc / l_b
                        lse_ref[rows, :] = m_b + jnp.log(l_b)
                    else:
                        o_ref[rows, :] = acc
                        m_s[rows, :] = m_b
                        l_s[rows, :] = l_b
                return carry

            lax.fori_loop(0, ngroups, group, 0)

        _gather2_end(plan, hp, steps)

    col = pl.BlockSpec((S, LANES), lambda i: (0, i))
    res = pl.pallas_call(
        body, name="attn_fwd", grid=(steps,),
        in_specs=[pl.BlockSpec(memory_space=pltpu.SMEM), col, col, col] + [ANY_SPEC] * ns,
        out_specs=[col, col] + [ANY_SPEC] * ns,
        out_shape=[jax.ShapeDtypeStruct((S, ATT_WIDTH), F32)] * 2 + _exchange_shapes(shards, (False,) * ns),
        scratch_shapes=[pltpu.VMEM((S, LANES), F32), pltpu.VMEM((S, LANES), F32),
                        pltpu.VMEM((2, 2 * SPAN, 2 * SPAN), F32)] + _exchange_sems(ns),
        compiler_params=_params("arbitrary"),
    )(slopes, q, k, v, *shards)
    return res[0], res[1], res[2:]


def _pool_count(i, w):
    t = i * TM + lax.broadcasted_iota(jnp.int32, (TM, 1), 0)
    return jnp.minimum(t + 1, w).astype(F32)


def _mix_out(x, att, u, pool_w, pool_scale, w_out, shards):
    S = x.shape[0]
    ns = len(shards)
    nsteps = S // TM

    def body(x_ref, att_ref, u_ref, pw_ref, ps_ref, w_ref, *rest):
        h1_ref, mix_ref, dlt_ref = rest[ns:ns + 3]
        ubuf = rest[2 * ns + 3]
        i = pl.program_id(0)
        plan = _gather2_copies(rest[:ns], rest[ns + 3:2 * ns + 3], *rest[2 * ns + 4:])
        _gather2_begin(plan, i, nsteps - 1)

        @pl.when(i == 0)
        def _():
            ubuf[0:HALO, :] = jnp.zeros((HALO, POOL_WIDTH), F32)

        ubuf[HALO:HALO + TM, :] = u_ref[...]
        mix_ref[:, 0:ATT_WIDTH] = att_ref[...].astype(BF16)
        for g, w in enumerate(POOL_WINDOWS):
            cols = slice(g * POOL_GROUP, (g + 1) * POOL_GROUP)
            ug = ubuf[HALO:HALO + TM, cols]
            acc = ug
            for j in range(1, w):
                acc = acc + ubuf[HALO - j:HALO - j + TM, cols]
            dlt = (acc / _pool_count(i, w) - ug).astype(BF16)
            dlt_ref[:, cols] = dlt
            yg = jnp.dot(dlt, pw_ref[g].astype(BF16), preferred_element_type=F32) * ps_ref[:, cols]
            mix_ref[:, ATT_WIDTH + g * POOL_GROUP:ATT_WIDTH + (g + 1) * POOL_GROUP] = yg.astype(BF16)
        ubuf[0:HALO, :] = ubuf[TM:TM + HALO, :]
        h1_ref[...] = x_ref[...] + jnp.dot(mix_ref[...], w_ref[...], preferred_element_type=F32)
        _gather2_end(plan, i, nsteps)

    tok = lambda w: pl.BlockSpec((TM, w), lambda i: (i, 0))
    res = pl.pallas_call(
        body, name="mix_out", grid=(nsteps,),
        in_specs=[tok(D_MODEL), tok(ATT_WIDTH), tok(POOL_WIDTH), _const(pool_w.shape), _const((1, POOL_WIDTH)),
                  _const(w_out.shape)] + [ANY_SPEC] * ns,
        out_specs=[tok(D_MODEL), tok(D_MODEL), tok(POOL_WIDTH)] + [ANY_SPEC] * ns,
        out_shape=[jax.ShapeDtypeStruct((S, D_MODEL), F32), jax.ShapeDtypeStruct((S, D_MODEL), BF16),
                   jax.ShapeDtypeStruct((S, POOL_WIDTH), BF16)] + _exchange_shapes(shards, (False,) * ns),
        scratch_shapes=[pltpu.VMEM((TM + HALO, POOL_WIDTH), F32)] + _exchange_sems(ns),
        compiler_params=_params("arbitrary"),
    )(x, att, u, pool_w, pool_scale, w_out, *shards)
    return res[0], res[1], res[2], res[3:]


def _conv_fwd(stage, upre, prev, cw, cb):
    T = upre.shape[0]
    stage[0:HALO, :] = prev
    stage[HALO:HALO + T, :] = upre
    return cb + cw[0:1, :] * stage[HALO - 2:HALO - 2 + T, :] + cw[1:2, :] * stage[HALO - 1:HALO - 1 + T, :] + cw[2:3, :] * upre


def _ffn_fwd(h1, g2, w_up, conv_w, conv_b, w_down, shards):
    S = h1.shape[0]
    T = TM_FF
    ns = len(shards)
    nsteps = S // T

    def body(h1_ref, g_ref, wu_ref, cw_ref, cb_ref, wd_ref, *rest):
        h2_ref, hn_ref, up_ref, upc_ref = rest[ns:ns + 4]
        carry, stage = rest[2 * ns + 4:2 * ns + 6]
        i = pl.program_id(0)
        plan = _gather2_copies(rest[:ns], rest[ns + 4:2 * ns + 4], *rest[2 * ns + 6:])
        _gather2_begin(plan, i, nsteps // 2)

        @pl.when(i == 0)
        def _():
            carry[...] = jnp.zeros(carry.shape, F32)

        h1t = h1_ref[...]
        r, n = _rms(h1t)
        hn = (n * g_ref[...]).astype(BF16)
        hn_ref[...] = hn
        acc = h1t
        for j in range(4):
            conv = []
            for jj in (j, j + 4):
                upre = jnp.dot(hn, wu_ref[jj], preferred_element_type=F32)
                up_ref[jj] = upre.astype(BF16)
                conv.append(_conv_fwd(stage, upre, carry[jj], cw_ref[jj], cb_ref[jj]))
                upc_ref[jj] = conv[-1].astype(BF16)
                carry[jj] = stage[T:T + HALO, :]
            gate, val = conv
            a = gate * jax.nn.sigmoid(gate) * val
            acc = acc + jnp.dot(a.astype(BF16), wd_ref[j], preferred_element_type=F32)
        h2_ref[...] = acc
        _gather2_end(plan, i, nsteps)

    tok = lambda w: pl.BlockSpec((T, w), lambda i: (i, 0))
    res = pl.pallas_call(
        body, name="ffn_fwd", grid=(nsteps,),
        in_specs=[tok(D_MODEL), _const((1, D_MODEL)), _const(w_up.shape), _const(conv_w.shape), _const(conv_b.shape),
                  _const(w_down.shape)] + [ANY_SPEC] * ns,
        out_specs=[tok(D_MODEL), tok(D_MODEL)] + [pl.BlockSpec((N_DEV, T, FF_SHARD), lambda i: (0, i, 0))] * 2
        + [ANY_SPEC] * ns,
        out_shape=[jax.ShapeDtypeStruct((S, D_MODEL), F32), jax.ShapeDtypeStruct((S, D_MODEL), BF16)]
        + [jax.ShapeDtypeStruct((N_DEV, S, FF_SHARD), BF16)] * 2 + _exchange_shapes(shards, (False,) * ns),
        scratch_shapes=[pltpu.VMEM((N_DEV, HALO, FF_SHARD), F32), pltpu.VMEM((T + HALO, FF_SHARD), F32)]
        + _exchange_sems(ns),
        compiler_params=_params("arbitrary"),
    )(h1, g2, w_up, conv_w, conv_b, w_down, *shards)
    return res[0], res[1], res[2], res[3], res[4:]


def _head(h2, p, g3, w_pg, w_ple, g4, target):
    S = h2.shape[0]
    nt = S // TM

    def body(h2_ref, p_ref, g3_ref, wpg_ref, wple_ref, g4_ref, t_ref,
             loss_ref, dh2_ref, dh2b_ref, dwpg_ref, dwple_ref, dg3_ref, dg4_ref, lacc, pg_acc, ple_acc):
        i = pl.program_id(0)

        @pl.when(i == 0)
        def _():
            lacc[...] = jnp.zeros(lacc.shape, F32)
            pg_acc[...] = jnp.zeros(pg_acc.shape, F32)
            ple_acc[...] = jnp.zeros(ple_acc.shape, F32)
            dg3_ref[...] = jnp.zeros(dg3_ref.shape, F32)
            dg4_ref[...] = jnp.zeros(dg4_ref.shape, F32)

        h2t = h2_ref[...]
        g3, g4 = g3_ref[...], g4_ref[...]
        r3, n3 = _rms(h2t)
        hn3 = (n3 * g3).astype(BF16)
        pb = p_ref[...].astype(BF16)
        gs = jax.nn.sigmoid(jnp.dot(hn3, wpg_ref[...], preferred_element_type=F32))
        pe = jnp.dot(pb, wple_ref[...], preferred_element_type=F32)
        h3 = h2t + gs * pe
        r4, n4 = _rms(h3)
        err = n4 * g4 - t_ref[...]
        lacc[...] += _colsum(err * err)
        dy = err * (1.0 / D_MODEL)
        dg4_ref[...] += _colsum(dy * n4)
        dh3 = _rms_bwd(r4, n4, g4, dy)
        dpe = (dh3 * gs).astype(BF16)
        dgl = (dh3 * pe * gs * (1.0 - gs)).astype(BF16)
        ple_acc[...] += lax.dot_general(pb, dpe, TN, preferred_element_type=F32)
        pg_acc[...] += lax.dot_general(hn3, dgl, TN, preferred_element_type=F32)
        dhn3 = lax.dot_general(dgl, wpg_ref[...], NT, preferred_element_type=F32)
        dg3_ref[...] += _colsum(dhn3 * n3)
        dh2 = dh3 + _rms_bwd(r3, n3, g3, dhn3)
        dh2_ref[...] = dh2
        dh2b_ref[...] = dh2.astype(BF16)

        @pl.when(i == nt - 1)
        def _():
            tot = 0.5 / D_MODEL * jnp.sum(lacc[...], axis=-1, keepdims=True)
            loss_ref[...] = jnp.broadcast_to(tot, loss_ref.shape)
            dwpg_ref[...] = pg_acc[...].astype(BF16)
            dwple_ref[...] = ple_acc[...].astype(BF16)

    tok = lambda w: pl.BlockSpec((TM, w), lambda i: (i, 0))
    row = pl.BlockSpec((1, D_MODEL), lambda i: (0, 0))
    act = lambda dt: jax.ShapeDtypeStruct((S, D_MODEL), dt)
    whole = lambda r: pl.BlockSpec((r, D_MODEL), lambda i: (0, 0))
    return pl.pallas_call(
        body, name="head", grid=(nt,),
        in_specs=[tok(D_MODEL), tok(PLE_DIM), _const((1, D_MODEL)), _const(w_pg.shape), _const(w_ple.shape),
                  _const((1, D_MODEL)), tok(D_MODEL)],
        out_specs=[pl.BlockSpec((8, LANES), lambda i: (0, 0)), tok(D_MODEL), tok(D_MODEL), whole(D_MODEL),
                   whole(PLE_DIM), row, row],
        out_shape=[jax.ShapeDtypeStruct((8, LANES), F32), act(F32), act(BF16),
                   jax.ShapeDtypeStruct((D_MODEL, D_MODEL), BF16), jax.ShapeDtypeStruct((PLE_DIM, D_MODEL), BF16),
                   jax.ShapeDtypeStruct((1, D_MODEL), F32), jax.ShapeDtypeStruct((1, D_MODEL), F32)],
        scratch_shapes=[pltpu.VMEM((1, D_MODEL), F32), pltpu.VMEM((D_MODEL, D_MODEL), F32),
                        pltpu.VMEM((PLE_DIM, D_MODEL), F32)],
        compiler_params=_params("arbitrary"),
    )(h2, p, g3, w_pg, w_ple, g4, target)


def _wgrad(name, x, dy, x_kind, dy_kind, nj, k_dim, n_dim, tk=TK):
    S = x.shape[-2]
    nt = S // tk

    def spec(kind, width):
        if kind == "full":
            return pl.BlockSpec((tk, width), lambda j, t: (t, 0))
        return pl.BlockSpec((None, tk, width), lambda j, t: (j, t, 0))

    def body(x_ref, dy_ref, o_ref, acc):
        t = pl.program_id(1)

        @pl.when(t == 0)
        def _():
            acc[...] = jnp.zeros(acc.shape, F32)

        acc[...] += lax.dot_general(x_ref[...].astype(BF16), dy_ref[...], TN, preferred_element_type=F32)

        @pl.when(t == nt - 1)
        def _():
            o_ref[...] = acc[...].astype(BF16)

    return pl.pallas_call(
        body, name=name, grid=(nj, nt),
        in_specs=[spec(x_kind, k_dim), spec(dy_kind, n_dim)],
        out_specs=pl.BlockSpec((None, k_dim, n_dim), lambda j, t: (j, 0, 0)),
        out_shape=jax.ShapeDtypeStruct((nj, k_dim, n_dim), BF16),
        scratch_shapes=[pltpu.VMEM((k_dim, n_dim), F32)],
        compiler_params=_params("arbitrary", "arbitrary"),
    )(x, dy)


def _row_picker(T, off0, off1):
    r = lax.broadcasted_iota(jnp.int32, (2 * T, T + HALO), 0)
    c = lax.broadcasted_iota(jnp.int32, (2 * T, T + HALO), 1)
    want = jnp.where(r < T, r + off0, r - T + off1)
    return jnp.where(c == want, 1.0, 0.0).astype(BF16)


def _ffn_bwd_a(dh2b, up, upc, w_down, grads):
    S = dh2b.shape[0]
    T = TM_FF
    hb = T // HALO
    nsteps = S // T
    ng = len(grads)

    def body(dh_ref, up_ref, halo_ref, upc_ref, wd_ref, *rest):
        dwd_ref, dup_ref, dcw_ref, dcb_ref = rest[ng:ng + 4]
        stage, dwd_acc = rest[2 * ng + 4:2 * ng + 6]
        i = pl.program_id(0)
        copies = _exchange_copies(rest[:ng], rest[ng + 4:2 * ng + 4], (True,) * ng, *rest[2 * ng + 6:])

        @pl.when(i == 0)
        def _():
            dcw_ref[...] = jnp.zeros(dcw_ref.shape, F32)
            dcb_ref[...] = jnp.zeros(dcb_ref.shape, F32)
            dwd_acc[...] = jnp.zeros(dwd_acc.shape, F32)
            for cp in copies:
                cp.start()

        @pl.when(i == nsteps - 1)
        def _():
            for cp in copies:
                cp.wait()

        dh = dh_ref[...]
        pick = _row_picker(T, HALO - 2, HALO - 1)
        for j in range(4):
            da = lax.dot_general(dh, wd_ref[j], NT, preferred_element_type=F32)
            taps = []
            for jj in (j, j + 4):
                upre = up_ref[jj]
                stage[0:HALO, :] = jnp.where(i > 0, halo_ref[jj], jnp.zeros((HALO, FF_SHARD), BF16))
                stage[HALO:HALO + T, :] = upre
                prv = jnp.dot(pick, stage[...], preferred_element_type=F32)
                taps.append((prv[0:T], prv[T:2 * T], upre.astype(F32)))
            gate, val = upc_ref[j].astype(F32), upc_ref[j + 4].astype(F32)
            sg = jax.nn.sigmoid(gate)
            silu = gate * sg
            dwd_acc[j] += lax.dot_general((silu * val).astype(BF16), dh, TN, preferred_element_type=F32)
            dgate = (da * val) * (sg + silu * (1.0 - sg))
            dval = da * silu
            for jj, dup, tp in ((j, dgate, taps[0]), (j + 4, dval, taps[1])):
                dup_ref[jj] = dup.astype(BF16)
                dcb_ref[jj] += _colsum(dup)
                for kk in range(3):
                    dcw_ref[jj, kk:kk + 1, :] += _colsum(dup * tp[kk])

        @pl.when(i == nsteps - 1)
        def _():
            dwd_ref[...] = dwd_acc[...].astype(BF16)

    tok = lambda w: pl.BlockSpec((T, w), lambda i: (i, 0))
    shard = pl.BlockSpec((N_DEV, T, FF_SHARD), lambda i: (0, i, 0))
    res = pl.pallas_call(
        body, name="ffn_bwd_a", grid=(nsteps,),
        in_specs=[tok(D_MODEL), shard,
                  pl.BlockSpec((N_DEV, HALO, FF_SHARD), lambda i: (0, jnp.maximum(i * hb - 1, 0), 0)),
                  shard, _const(w_down.shape)] + [ANY_SPEC] * ng,
        out_specs=[_const(w_down.shape), shard,
                   pl.BlockSpec((N_DEV, 3, FF_SHARD), lambda i: (0, 0, 0)),
                   pl.BlockSpec((N_DEV, 1, FF_SHARD), lambda i: (0, 0, 0))] + [ANY_SPEC] * ng,
        out_shape=[jax.ShapeDtypeStruct(w_down.shape, BF16), jax.ShapeDtypeStruct((N_DEV, S, FF_SHARD), BF16),
                   jax.ShapeDtypeStruct((N_DEV, 3, FF_SHARD), F32), jax.ShapeDtypeStruct((N_DEV, 1, FF_SHARD), F32)]
        + _exchange_shapes(grads, (True,) * ng),
        scratch_shapes=[pltpu.VMEM((T + HALO, FF_SHARD), BF16), pltpu.VMEM(w_down.shape, F32)] + _exchange_sems(ng),
        compiler_params=_params("arbitrary"),
    )(dh2b, up, up, upc, w_down, *grads)
    return res[0], res[1], res[2], res[3], res[4:]


def _ffn_bwd_b(dup, conv_w, w_up, h1, g2, dh2, grads):
    S = h1.shape[0]
    T = TM_FF
    hb = T // HALO
    nt = S // T
    ng = len(grads)

    def body(dup_ref, halo_ref, cw_ref, wu_ref, h1_ref, g_ref, dh2_ref, *rest):
        dpre_ref, dh1_ref, dh1b_ref, dg_ref = rest[ng:ng + 4]
        stage = rest[2 * ng + 4]
        i = pl.program_id(0)
        copies = _exchange_copies(rest[:ng], rest[ng + 4:2 * ng + 4], (True,) * ng, *rest[2 * ng + 5:])

        @pl.when(i == 0)
        def _():
            dg_ref[...] = jnp.zeros(dg_ref.shape, F32)
            for cp in copies:
                cp.start()

        dhn = jnp.zeros((T, D_MODEL), F32)
        for jj in range(N_DEV):
            dup = dup_ref[jj].astype(F32)
            stage[0:T, :] = dup
            stage[T:T + HALO, :] = jnp.where(i < nt - 1, halo_ref[jj].astype(F32), 0.0)
            cw = cw_ref[jj]
            dpre = (cw[2:3, :] * dup + cw[1:2, :] * stage[1:1 + T, :] + cw[0:1, :] * stage[2:2 + T, :]).astype(BF16)
            dpre_ref[jj] = dpre
            dhn = dhn + lax.dot_general(dpre, wu_ref[jj], NT, preferred_element_type=F32)
        g = g_ref[...]
        r, n = _rms(h1_ref[...])
        dg_ref[...] += _colsum(dhn * n)
        dh1 = dh2_ref[...] + _rms_bwd(r, n, g, dhn)
        dh1_ref[...] = dh1
        dh1b_ref[...] = dh1.astype(BF16)

        @pl.when(i == nt - 1)
        def _():
            for cp in copies:
                cp.wait()

    tok = lambda w: pl.BlockSpec((T, w), lambda i: (i, 0))
    shard = pl.BlockSpec((N_DEV, T, FF_SHARD), lambda i: (0, i, 0))
    res = pl.pallas_call(
        body, name="ffn_bwd_b", grid=(nt,),
        in_specs=[shard,
                  pl.BlockSpec((N_DEV, HALO, FF_SHARD), lambda i: (0, jnp.minimum((i + 1) * hb, S // HALO - 1), 0)),
                  _const(conv_w.shape), _const(w_up.shape), tok(D_MODEL), _const((1, D_MODEL)), tok(D_MODEL)]
        + [ANY_SPEC] * ng,
        out_specs=[shard, tok(D_MODEL), tok(D_MODEL), pl.BlockSpec((1, D_MODEL), lambda i: (0, 0))] + [ANY_SPEC] * ng,
        out_shape=[jax.ShapeDtypeStruct((N_DEV, S, FF_SHARD), BF16), jax.ShapeDtypeStruct((S, D_MODEL), F32),
                   jax.ShapeDtypeStruct((S, D_MODEL), BF16), jax.ShapeDtypeStruct((1, D_MODEL), F32)]
        + _exchange_shapes(grads, (True,) * ng),
        scratch_shapes=[pltpu.VMEM((T + HALO, FF_SHARD), F32)] + _exchange_sems(ng),
        compiler_params=_params("arbitrary"),
    )(dup, dup, conv_w, w_up, h1, g2, dh2, *grads)
    return res[0], res[1], res[2], res[3], res[4:]


def _mix_bwd(dh1b, w_out, dlt, mix, pool_w, pool_scale):
    S = dh1b.shape[0]
    nt = S // TM

    def body(dh_ref, w_ref, dlt_ref, mix_ref, pw_ref, ps_ref, datt_ref, du_ref, dpw_ref, dps_ref, dwo_ref,
             stage, carry, wo_acc):
        i = pl.program_id(0)
        tile = nt - 1 - i

        @pl.when(i == 0)
        def _():
            dpw_ref[...] = jnp.zeros(dpw_ref.shape, F32)
            dps_ref[...] = jnp.zeros(dps_ref.shape, F32)
            carry[...] = jnp.zeros(carry.shape, F32)
            wo_acc[...] = jnp.zeros(wo_acc.shape, F32)

        wo_acc[...] += lax.dot_general(mix_ref[...], dh_ref[...], TN, preferred_element_type=F32)

        @pl.when(i == nt - 1)
        def _():
            dwo_ref[...] = wo_acc[...].astype(BF16)

        dmix = lax.dot_general(dh_ref[...], w_ref[...], NT, preferred_element_type=F32)
        datt_ref[...] = dmix[:, 0:ATT_WIDTH]
        for g, w in enumerate(POOL_WINDOWS):
            cols = slice(g * POOL_GROUP, (g + 1) * POOL_GROUP)
            dpool = dmix[:, ATT_WIDTH + g * POOL_GROUP:ATT_WIDTH + (g + 1) * POOL_GROUP]
            dl = dlt_ref[:, cols]
            pw = pw_ref[g].astype(BF16)
            yg = jnp.dot(dl, pw, preferred_element_type=F32)
            dps_ref[:, cols] += _colsum(dpool * yg)
            dy = (dpool * ps_ref[:, cols]).astype(BF16)
            dpw_ref[g] += lax.dot_general(dl, dy, TN, preferred_element_type=F32)
            ddlt = lax.dot_general(dy, pw, NT, preferred_element_type=F32)
            cg = ddlt / _pool_count(tile, w)
            stage[0:TM, :] = cg
            stage[TM:TM + HALO, :] = carry[:, cols]
            acc = cg
            for j in range(1, w):
                acc = acc + stage[j:j + TM, :]
            du_ref[:, cols] = acc - ddlt
            carry[:, cols] = cg[0:HALO, :]

    tok = lambda w: pl.BlockSpec((TM, w), lambda i: (nt - 1 - i, 0))
    return pl.pallas_call(
        body, name="mix_bwd", grid=(nt,),
        in_specs=[tok(D_MODEL), _const(w_out.shape), tok(POOL_WIDTH), tok(D_MODEL), _const(pool_w.shape),
                  _const((1, POOL_WIDTH))],
        out_specs=[tok(ATT_WIDTH), tok(POOL_WIDTH), pl.BlockSpec(pool_w.shape, lambda i: (0, 0, 0)),
                   pl.BlockSpec((1, POOL_WIDTH), lambda i: (0, 0)), pl.BlockSpec(w_out.shape, lambda i: (0, 0))],
        out_shape=[jax.ShapeDtypeStruct((S, ATT_WIDTH), F32), jax.ShapeDtypeStruct((S, POOL_WIDTH), F32),
                   jax.ShapeDtypeStruct(pool_w.shape, F32), jax.ShapeDtypeStruct((1, POOL_WIDTH), F32),
                   jax.ShapeDtypeStruct(w_out.shape, BF16)],
        scratch_shapes=[pltpu.VMEM((TM + HALO, POOL_GROUP), F32), pltpu.VMEM((HALO, POOL_WIDTH), F32),
                        pltpu.VMEM(w_out.shape, F32)],
        compiler_params=_params("arbitrary"),
    )(dh1b, w_out, dlt, mix, pool_w, pool_scale)


def _attn_bwd(slopes, q, k, v, o, lse, do, grads, scatter):
    S = q.shape[0]
    CH = 512
    ng = len(grads)
    steps = ATT_WIDTH // LANES

    def body(slope_ref, q_ref, k_ref, v_ref, o_ref, lse_ref, do_ref, *rest):
        dq_ref, dk_ref, dv_ref = rest[ng:ng + 3]
        dl_s, bias_s = rest[2 * ng + 3:2 * ng + 5]
        hp = pl.program_id(0)
        copies = _exchange_copies(rest[:ng], rest[ng + 3:2 * ng + 3], scatter, *rest[2 * ng + 5:])

        @pl.when(hp == 0)
        def _():
            for cp in copies:
                cp.start()

        is0 = lax.broadcasted_iota(jnp.int32, (SPAN, LANES), 1) < HEAD_DIM
        is0c = lax.broadcasted_iota(jnp.int32, (CH, LANES), 1) < HEAD_DIM

        def prep(ci, carry):
            rows = pl.ds(pl.multiple_of(ci * CH, CH), CH)
            prod = do_ref[rows, :] * o_ref[rows, :]
            d0 = jnp.sum(jnp.where(is0c, prod, 0.0), axis=-1, keepdims=True)
            d1 = jnp.sum(jnp.where(is0c, 0.0, prod), axis=-1, keepdims=True)
            lse_t = lse_ref[rows, :]
            for c, col in enumerate((lse_t[:, 0:1], lse_t[:, HEAD_DIM:HEAD_DIM + 1], d0, d1)):
                dl_s[c, rows, :] = jnp.broadcast_to(col, (CH, LANES))
            zero = jnp.zeros((CH, LANES), F32)
            dq_ref[rows, :] = zero
            dk_ref[rows, :] = zero
            dv_ref[rows, :] = zero
            return carry

        lax.fori_loop(0, S // CH, prep, 0)

        for d in DILATIONS:
            nb, ngroups = _attn_groups(S, d, ATTN_GROUP_BWD, writes_key_rows=True)
            _attn_fill_bias(bias_s, slope_ref, hp, d)

            def group(i, carry, d=d, nb=nb):
                blocks = [_attn_block(i, g, d, nb, ATTN_GROUP_BWD) for g in range(ATTN_GROUP_BWD)]
                loaded = [(q_ref[rows, :], do_ref[rows, :], [dl_s[c, rows, :] for c in range(4)], k_ref[krows, :],
                           v_ref[krows, :].astype(BF16)) for rows, krows, _ in blocks]
                new = []
                for (rows, krows, tab), (qb, dob, st, kf, vb) in zip(blocks, loaded):
                    kb = kf.astype(BF16)
                    qs = _stack_heads(qb, is0).astype(BF16)
                    dos = _stack_heads(dob, is0).astype(BF16)
                    lse_s = jnp.tile(jnp.concatenate(st[0:2], axis=0), (1, 2))
                    dl_s2 = jnp.tile(jnp.concatenate(st[2:4], axis=0), (1, 2))
                    s = lax.dot_general(qs, kb, NT, preferred_element_type=F32) + bias_s[tab]
                    pr = jnp.exp(s - lse_s)
                    dp = lax.dot_general(dos, vb, NT, preferred_element_type=F32)
                    ds = (pr * (dp - dl_s2)).astype(BF16)
                    dv_c = lax.dot_general(pr.astype(BF16), dos, TN, preferred_element_type=F32)
                    dk_c = lax.dot_general(ds, qs, TN, preferred_element_type=F32)
                    dq_c = _unstack_heads(jnp.dot(ds, kb, preferred_element_type=F32), is0)
                    new.append((dq_c, dk_c, dv_c))
                old = [(dq_ref[rows, :], dk_ref[krows, :], dv_ref[krows, :]) for rows, krows, _ in blocks]
                for (rows, krows, _), (dq_c, dk_c, dv_c), (dq_o, dk_o, dv_o) in zip(blocks, new, old):
                    dq_ref[rows, :] = dq_o + dq_c
                    dk_ref[krows, :] = dk_o + dk_c
                    dv_ref[krows, :] = dv_o + dv_c
                return carry

            lax.fori_loop(0, ngroups, group, 0)

        @pl.when(hp == steps - 1)
        def _():
            for cp in copies:
                cp.wait()

    col = pl.BlockSpec((S, LANES), lambda i: (0, i))
    res = pl.pallas_call(
        body, name="attn_bwd", grid=(steps,),
        in_specs=[pl.BlockSpec(memory_space=pltpu.SMEM)] + [col] * 6 + [ANY_SPEC] * ng,
        out_specs=[col] * 3 + [ANY_SPEC] * ng,
        out_shape=[jax.ShapeDtypeStruct((S, ATT_WIDTH), F32)] * 3 + _exchange_shapes(grads, scatter),
        scratch_shapes=[pltpu.VMEM((4, S, LANES), F32), pltpu.VMEM((2, 2 * SPAN, 2 * SPAN), F32)] + _exchange_sems(ng),
        compiler_params=_params("arbitrary"),
    )(slopes, q, k, v, o, lse, do, *grads)
    return res[0], res[1], res[2], res[3:]


def _in_bwd(dq, dk, dv, du, w_in, x, g1, dh1):
    S = x.shape[0]

    def body(dq_ref, dk_ref, dv_ref, du_ref, w_ref, x_ref, g_ref, dh1_ref, dz_ref, dx_ref, dg_ref):
        @pl.when(pl.program_id(0) == 0)
        def _():
            dg_ref[...] = jnp.zeros(dg_ref.shape, F32)

        srcs = (dq_ref, dk_ref, dv_ref, du_ref)
        dhn = jnp.zeros((TM, D_MODEL), F32)
        for j in range(N_DEV):
            dz = srcs[j // 2][:, (j % 2) * 256:(j % 2 + 1) * 256]
            if j < 2:
                dz = dz * (HEAD_DIM ** -0.5)
            dz = dz.astype(BF16)
            dz_ref[j] = dz
            dhn = dhn + lax.dot_general(dz, w_ref[j], NT, preferred_element_type=F32)
        g = g_ref[...]
        r, n = _rms(x_ref[...])
        dg_ref[...] += _colsum(dhn * n)
        dx_ref[...] = dh1_ref[...] + _rms_bwd(r, n, g, dhn)

    tok = lambda w: pl.BlockSpec((TM, w), lambda i: (i, 0))
    return pl.pallas_call(
        body, name="in_bwd", grid=(S // TM,),
        in_specs=[tok(512)] * 4 + [_const(w_in.shape), tok(D_MODEL), _const((1, D_MODEL)), tok(D_MODEL)],
        out_specs=[pl.BlockSpec((N_DEV, TM, 256), lambda i: (0, i, 0)), tok(D_MODEL),
                   pl.BlockSpec((1, D_MODEL), lambda i: (0, 0))],
        out_shape=[jax.ShapeDtypeStruct((N_DEV, S, 256), BF16), jax.ShapeDtypeStruct((S, D_MODEL), F32),
                   jax.ShapeDtypeStruct((1, D_MODEL), F32)],
        compiler_params=_params("arbitrary"),
    )(dq, dk, dv, du, w_in, x, g1, dh1)


def _adamw(name, parts, w, m, v):
    R, C = w.shape
    rb = R
    for cand in (256, 128, 64, 32, 16, 8):
        if R % cand == 0 and R > cand:
            rb = cand
            break

    def body(p_ref, w_ref, m_ref, v_ref, g_ref, d_ref, mo_ref, vo_ref):
        g = p_ref[0].astype(F32)
        for s in range(1, N_DEV):
            g = g + p_ref[s].astype(F32)
        g_ref[...] = g
        d_ref[...], mo_ref[...], vo_ref[...] = _adam_update(g, w_ref[...], m_ref[...], v_ref[...])

    blk = pl.BlockSpec((rb, C), lambda i: (i, 0))
    return pl.pallas_call(
        body, name=name, grid=(R // rb,),
        in_specs=[pl.BlockSpec((N_DEV, rb, C), lambda i: (0, i, 0)), blk, blk, blk],
        out_specs=[blk] * 4,
        out_shape=[jax.ShapeDtypeStruct((R, C), F32)] * 4,
        compiler_params=_params("arbitrary"),
    )(parts, w, m, v)


def _adam_update(g, w, m, v):
    m_new = ADAM_B1 * m + (1.0 - ADAM_B1) * g
    v_new = ADAM_B2 * v + (1.0 - ADAM_B2) * (g * g)
    m_hat = m_new / (1.0 - ADAM_B1 ** ADAM_STEP)
    v_hat = v_new / (1.0 - ADAM_B2 ** ADAM_STEP)
    return -ADAM_LR * (m_hat / (jnp.sqrt(v_hat) + ADAM_EPS) + ADAM_WD * w), m_new, v_new


def _adamw_small(parts, loss_parts, ws, ms, vs):
    n = len(ws)

    def body(*refs):
        p_refs, lp_ref = refs[:n], refs[n]
        w_refs, m_refs, v_refs = refs[n + 1:2 * n + 1], refs[2 * n + 1:3 * n + 1], refs[3 * n + 1:4 * n + 1]
        outs = refs[4 * n + 1:]
        for i in range(n):
            g = p_refs[i][0]
            for s in range(1, N_DEV):
                g = g + p_refs[i][s]
            d, m_new, v_new = _adam_update(g, w_refs[i][...], m_refs[i][...], v_refs[i][...])
            outs[i][...] = g
            outs[n + i][...] = d
            outs[2 * n + i][...] = m_new
            outs[3 * n + i][...] = v_new
        tot = lp_ref[0]
        for s in range(1, N_DEV):
            tot = tot + lp_ref[s]
        outs[4 * n][...] = tot

    shapes = [jax.ShapeDtypeStruct(w.shape, F32) for w in ws]
    res = pl.pallas_call(
        body, name="adamw_replicated",
        out_shape=shapes * 4 + [jax.ShapeDtypeStruct(loss_parts.shape[1:], F32)],
        compiler_params=_params(),
    )(*parts, loss_parts, *ws, *ms, *vs)
    return res[:n], res[n:2 * n], res[2 * n:3 * n], res[3 * n:4 * n], res[4 * n]


def _gather2(name, arrays):
    n = len(arrays)

    def body(*refs):
        first, passed, last = _gather2_copies(refs[:n], refs[n:2 * n], *refs[2 * n:])
        for cp in first:
            cp.start()
        for arrival, cp in passed:
            arrival.wait_recv()
            cp.start()
        for wait in last:
            wait()

    return pl.pallas_call(
        body, name=name,
        in_specs=[ANY_SPEC] * n, out_specs=[ANY_SPEC] * n, out_shape=_exchange_shapes(arrays, (False,) * n),
        scratch_shapes=_exchange_sems(n),
    )(*arrays)


def _dw_in_exchange(hn, dz, small):
    S = hn.shape[0]
    nt = S // TK
    ns = len(small)
    kd, nd = hn.shape[1], dz.shape[2]
    me_arr = (4 * lax.axis_index("x") + 2 * lax.axis_index("y") + lax.axis_index("c")).astype(jnp.int32).reshape(1)

    def body(me_ref, x_ref, dy_ref, *rest):
        recv_ref = rest[ns]
        acc, stage, send_sems, recv_sems, own_sem = rest[2 * ns + 1:2 * ns + 6]
        j, t = pl.program_id(0), pl.program_id(1)
        x, y, c = lax.axis_index("x"), lax.axis_index("y"), lax.axis_index("c")
        me = 4 * x + 2 * y + c
        small_copies = _exchange_copies(rest[:ns], rest[ns + 1:2 * ns + 1], (False,) * ns, *rest[2 * ns + 6:])

        @pl.when((j == 0) & (t == 0))
        def _():
            for cp in small_copies:
                cp.start()

        @pl.when(t == 0)
        def _():
            acc[...] = jnp.zeros(acc.shape, F32)

        acc[...] += lax.dot_general(x_ref[...], dy_ref[...], TN, preferred_element_type=F32)

        def to_owner(k, owner):
            return pltpu.make_async_remote_copy(
                src_ref=stage.at[owner], dst_ref=recv_ref.at[me], send_sem=send_sems.at[k], recv_sem=recv_sems.at[k],
                device_id=(owner // 4, (owner // 2) % 2, owner % 2), device_id_type=MESH)

        own = pltpu.make_async_copy(stage.at[me], recv_ref.at[me], own_sem)

        @pl.when(t == nt - 1)
        def _():
            owner = (me + 1 + j) % N_DEV
            stage[owner] = acc[...].astype(BF16)

            @pl.when(j < N_DEV - 1)
            def _():
                to_owner(j, owner).start()

            @pl.when(j == N_DEV - 1)
            def _():
                own.start()
                own.wait()
                for k in range(N_DEV - 1):
                    to_owner(k, me).wait_send()
                    to_owner(k, me).wait_recv()
                for cp in small_copies:
                    cp.wait()

    slab = lambda j, me_ref: (me_ref[0] + 1 + j) % N_DEV
    grid_spec = pltpu.PrefetchScalarGridSpec(
        num_scalar_prefetch=1, grid=(N_DEV, nt),
        in_specs=[pl.BlockSpec((TK, kd), lambda j, t, me_ref: (t, 0)),
                  pl.BlockSpec((None, TK, nd), lambda j, t, me_ref: (slab(j, me_ref), t, 0))] + [ANY_SPEC] * ns,
        out_specs=[ANY_SPEC] * (ns + 1),
        scratch_shapes=[pltpu.VMEM((kd, nd), F32), pltpu.VMEM((N_DEV, kd, nd), BF16),
                        pltpu.SemaphoreType.DMA((N_DEV - 1,)), pltpu.SemaphoreType.DMA((N_DEV - 1,)),
                        pltpu.SemaphoreType.DMA] + _exchange_sems(ns))
    res = pl.pallas_call(
        body, name="dw_in_exchange", grid_spec=grid_spec,
        out_shape=[jax.ShapeDtypeStruct((N_DEV, kd, nd), BF16)] + _exchange_shapes(small, (False,) * ns),
        compiler_params=_params("arbitrary", "arbitrary"),
    )(me_arr, hn, dz, *small)
    return res[0], res[1:]


def kernel(x, p, ln_mix, w_in, pool_w, pool_scale, w_out, ln_ffn, w_up, conv_w, conv_b, w_down, ln_ple, w_ple_gate, w_ple, ln_final, loss_target, m_ln_mix, m_w_in, m_pool_w, m_pool_scale, m_w_out, m_ln_ffn, m_w_up, m_conv_w, m_conv_b, m_w_down, m_ln_ple, m_w_ple_gate, m_w_ple, m_ln_final, v_ln_mix, v_w_in, v_pool_w, v_pool_scale, v_w_out, v_ln_ffn, v_w_up, v_conv_w, v_conv_b, v_w_down, v_ln_ple, v_w_ple_gate, v_w_ple, v_ln_final):
    xs, ps, tgt, pool_w0 = x[0], p[0, 0], loss_target[0], pool_w[0]
    slopes = jnp.exp2(-8.0 * (jnp.arange(N_HEADS, dtype=F32) + 1.0) / N_HEADS)
    conv_b_s = conv_b.reshape(N_DEV, 1, FF_SHARD)

    (w_in_g,) = _gather2("gather_w_in", [w_in[0].astype(BF16)])
    w_down_b = w_down[0].astype(BF16)
    (q, k, v, u, hn1), (w_out_g, w_down_g0) = _qkvu(xs, ln_mix, w_in_g, [w_out[0].astype(BF16), w_down_b[:W_DOWN_CUT]])
    att, lse, (w_up_g, conv_w_g) = _attn_fwd(slopes, q, k, v, [w_up[0].astype(BF16), conv_w[0]])
    w_out_f = w_out_g.reshape(D_MODEL, D_MODEL)
    h1, mix, dlt, (w_down_g1,) = _mix_out(xs, att, u, pool_w0, pool_scale, w_out_f, [w_down_b[W_DOWN_CUT:]])
    w_down_f = jnp.concatenate([w_down_g0, w_down_g1], axis=1).reshape(4, FF_SHARD, D_MODEL)
    h2, hn2, up, upc, (w_pg_g, w_ple_g) = _ffn_fwd(h1, ln_ffn, w_up_g, conv_w_g, conv_b_s, w_down_f,
                                                   [w_ple_gate[0].astype(BF16), w_ple[0].astype(BF16)])
    w_pg_f = w_pg_g.reshape(D_MODEL, D_MODEL)
    w_ple_f = jnp.transpose(w_ple_g, (1, 0, 2)).reshape(PLE_DIM, D_MODEL)
    loss_blk, dh2, dh2b, d_w_pg, d_w_ple, d_ln_ple, d_ln_final = _head(
        h2, ps, ln_ple, w_pg_f, w_ple_f, ln_final.reshape(1, D_MODEL), tgt)

    d_w_pg = d_w_pg.reshape(N_DEV, D_MODEL // N_DEV, D_MODEL)
    d_w_ple = jnp.transpose(d_w_ple.reshape(PLE_DIM, N_DEV, LANES), (1, 0, 2))
    d_w_down, dup, d_conv_w, d_conv_b, (r_w_pg, r_w_ple) = _ffn_bwd_a(dh2b, up, upc, w_down_f, [d_w_pg, d_w_ple])
    d_w_down = d_w_down.reshape(N_DEV, D_FF // N_DEV, D_MODEL)
    dpre, dh1, dh1b, d_ln_ffn, (r_conv_w, r_w_down) = _ffn_bwd_b(
        dup, conv_w_g, w_up_g, h1, ln_ffn, dh2, [d_conv_w, d_w_down])
    datt, du, d_pool_w, d_pool_scale, d_w_out = _mix_bwd(dh1b, w_out_f, dlt, mix, pool_w0, pool_scale)
    d_w_out = d_w_out.reshape(N_DEV, D_MODEL // N_DEV, D_MODEL)
    d_w_up = _wgrad("dw_up", dpre, hn2, "lead", "full", N_DEV, FF_SHARD, D_MODEL)
    rep_late = [d_pool_w, d_pool_scale, d_ln_ffn, d_conv_b.reshape(1, 2 * D_FF), d_ln_ple, d_ln_final, loss_blk]
    dq, dk, dv, received = _attn_bwd(slopes, q, k, v, att, lse, datt, [d_w_out, d_w_up] + rep_late,
                                     (True, True) + (False,) * len(rep_late))
    r_w_out, r_w_up, r_rep = received[0], received[1], list(received[2:])
    dz, grad_x, d_ln_mix = _in_bwd(dq, dk, dv, du, w_in_g, xs, ln_mix, dh1)

    rep_names = ("ln_mix", "pool_w", "pool_scale", "ln_ffn", "conv_b", "ln_ple", "ln_final")
    rep_w = [ln_mix, pool_w0, pool_scale, ln_ffn, conv_b, ln_ple, ln_final.reshape(1, D_MODEL)]
    rep_m = [m_ln_mix, m_pool_w[0], m_pool_scale, m_ln_ffn, m_conv_b, m_ln_ple, m_ln_final.reshape(1, D_MODEL)]
    rep_v = [v_ln_mix, v_pool_w[0], v_pool_scale, v_ln_ffn, v_conv_b, v_ln_ple, v_ln_final.reshape(1, D_MODEL)]
    r_w_in, (r_ln_mix,) = _dw_in_exchange(hn1, dz, [d_ln_mix])
    small = _adamw_small([r_ln_mix] + r_rep[:-1], r_rep[-1], rep_w, rep_m, rep_v)
    loss = small[4][0, 0]

    sharded = {}
    sharded["w_in"] = _adamw("adamw_w_in", r_w_in, w_in[0], m_w_in[0], v_w_in[0])
    sharded["w_out"] = _adamw("adamw_w_out", r_w_out, w_out[0], m_w_out[0], v_w_out[0])
    sharded["w_up"] = [t.T for t in _adamw("adamw_w_up", r_w_up, w_up[0].T, m_w_up[0].T, v_w_up[0].T)]
    sharded["conv_w"] = _adamw("adamw_conv_w", r_conv_w, conv_w[0], m_conv_w[0], v_conv_w[0])
    sharded["w_down"] = _adamw("adamw_w_down", r_w_down, w_down[0], m_w_down[0], v_w_down[0])
    sharded["w_ple_gate"] = _adamw("adamw_w_ple_gate", r_w_pg, w_ple_gate[0], m_w_ple_gate[0], v_w_ple_gate[0])
    sharded["w_ple"] = _adamw("adamw_w_ple", r_w_ple, w_ple[0], m_w_ple[0], v_w_ple[0])

    shapes = dict(w_in=w_in, w_out=w_out, w_up=w_up, conv_w=conv_w, w_down=w_down, w_ple_gate=w_ple_gate, w_ple=w_ple,
                  ln_mix=ln_mix, pool_w=pool_w, pool_scale=pool_scale, ln_ffn=ln_ffn, conv_b=conv_b, ln_ple=ln_ple,
                  ln_final=ln_final)

    def leaf(kind, n):
        src = sharded[n][kind] if n in sharded else small[kind][rep_names.index(n)]
        return src.reshape(shapes[n].shape)

    order = ("ln_mix", "w_in", "pool_w", "pool_scale", "w_out", "ln_ffn", "w_up", "conv_w", "conv_b", "w_down", "ln_ple",
             "w_ple_gate", "w_ple", "ln_final")
    outs = [loss, grad_x[None]]
    for kind in range(4):
        outs += [leaf(kind, n) for n in order]
    return tuple(outs)
```

```python
import jax
import jax.numpy as jnp
from jax import lax
from jax.experimental import pallas as pl
from jax.experimental.pallas import tpu as pltpu

F32 = jnp.float32
BF16 = jnp.bfloat16

N_DEV = 8
D_MODEL = 1024
ATT_WIDTH = 512
POOL_WIDTH = 512
N_HEADS = 8
HEAD_DIM = 64
SPAN = 128
DILATIONS = (1, 4, 16)
POOL_WINDOWS = (2, 4, 8, 16)
POOL_GROUP = 128
D_FF = 2816
FF_SHARD = 2 * D_FF // N_DEV
PLE_DIM = 256
EPS = 1e-6
NEG = -1e30

ADAM_LR = 0.001
ADAM_B1 = 0.9
ADAM_B2 = 0.999
ADAM_EPS = 1e-08
ADAM_WD = 0.01
ADAM_STEP = 10

LANES = 128
HALO = 16
TM = 512
TM_FF = 256
TK = 4096
W_DOWN_CUT = 112
ATTN_GROUP_FWD = 16
ATTN_GROUP_BWD = 8
VMEM_LIMIT = 56 * 1024 * 1024

MESH = pl.DeviceIdType.MESH
NT = (((1,), (1,)), ((), ()))
TN = (((0,), (0,)), ((), ()))


def _params(*sem):
    return pltpu.CompilerParams(dimension_semantics=sem or None, vmem_limit_bytes=VMEM_LIMIT)


def _const(shape):
    n = len(shape)
    return pl.BlockSpec(shape, lambda *_: (0,) * n, pipeline_mode=pl.Buffered(1))


def _rms(h):
    r = lax.rsqrt(jnp.mean(h * h, axis=-1, keepdims=True) + EPS)
    return r, h * r


def _rms_bwd(r, n, g, dhn):
    dn = dhn * g
    return r * (dn - n * jnp.mean(dn * n, axis=-1, keepdims=True))


def _colsum(a):
    return jnp.sum(a, axis=0, keepdims=True)


def _gather2_copies(ins, outs, send_sems, recv_sems, local_sems):
    n = len(ins)
    x, y, c = lax.axis_index("x"), lax.axis_index("y"), lax.axis_index("c")
    slot = lambda px, py, pc: 4 * px + 2 * py + pc
    chips = [(x, 1 - y), (1 - x, y), (1 - x, 1 - y)]
    first, passed, last = [], [], []

    def remote(a, r, src, dst_slot, to):
        return pltpu.make_async_remote_copy(
            src_ref=src, dst_ref=outs[a].at[dst_slot],
            send_sem=send_sems.at[a * (N_DEV - 1) + r], recv_sem=recv_sems.at[a * (N_DEV - 1) + r],
            device_id=to, device_id_type=MESH)

    for a in range(n):
        mine = pltpu.make_async_copy(ins[a], outs[a].at[slot(x, y, c)], local_sems.at[a])
        to_sibling = remote(a, 0, ins[a], slot(x, y, c), (x, y, 1 - c))
        first += [mine, to_sibling]
        last += [mine.wait, to_sibling.wait_send, to_sibling.wait_recv]
        for r, (px, py) in enumerate(chips, start=1):
            to_chip = remote(a, r, ins[a], slot(x, y, c), (px, py, c))
            onward = remote(a, 3 + r, outs[a].at[slot(px, py, c)], slot(px, py, c), (x, y, 1 - c))
            first.append(to_chip)
            passed.append((to_chip, onward))
            last += [to_chip.wait_send, onward.wait_send, onward.wait_recv]
    return first, passed, last


def _gather2_begin(plan, step, pass_step):
    first, passed, _ = plan

    @pl.when(step == 0)
    def _():
        for cp in first:
            cp.start()

    @pl.when(step == pass_step)
    def _():
        for arrival, cp in passed:
            arrival.wait_recv()
            cp.start()


def _gather2_end(plan, step, nsteps):
    @pl.when(step == nsteps - 1)
    def _():
        for wait in plan[2]:
            wait()


ANY_SPEC = pl.BlockSpec(memory_space=pl.ANY)


def _exchange_shapes(arrays, scatter):
    out = []
    for a, s in zip(arrays, scatter):
        slab = a.shape[1:] if s else a.shape
        out.append(jax.ShapeDtypeStruct((N_DEV,) + tuple(slab), a.dtype))
    return out


def _exchange_sems(n):
    return [pltpu.SemaphoreType.DMA((n * (N_DEV - 1),)), pltpu.SemaphoreType.DMA((n * (N_DEV - 1),)),
            pltpu.SemaphoreType.DMA((n,))]


def _exchange_copies(ins, outs, scatter, send_sems, recv_sems, local_sems):
    n = len(ins)
    x, y, c = lax.axis_index("x"), lax.axis_index("y"), lax.axis_index("c")
    me = 4 * x + 2 * y + c
    copies = []
    for a in range(n):
        src = ins[a].at[me] if scatter[a] else ins[a]
        copies.append(pltpu.make_async_copy(src, outs[a].at[me], local_sems.at[a]))
    for k in range(1, N_DEV):
        px = 1 - x if k & 4 else x
        py = 1 - y if k & 2 else y
        pc = 1 - c if k & 1 else c
        pid = 4 * px + 2 * py + pc
        for a in range(n):
            src = ins[a].at[pid] if scatter[a] else ins[a]
            copies.append(pltpu.make_async_remote_copy(
                src_ref=src, dst_ref=outs[a].at[me],
                send_sem=send_sems.at[a * (N_DEV - 1) + k - 1], recv_sem=recv_sems.at[a * (N_DEV - 1) + k - 1],
                device_id=(px, py, pc), device_id_type=MESH))
    return copies


def _qkvu(x, g1, w_in, shards):
    S = x.shape[0]
    ns = len(shards)
    nsteps = S // TM

    def body(x_ref, g_ref, w_ref, *rest):
        q_ref, k_ref, v_ref, u_ref, hn_ref = rest[ns:ns + 5]
        plan = _gather2_copies(rest[:ns], rest[ns + 5:2 * ns + 5], *rest[2 * ns + 5:])
        _gather2_begin(plan, pl.program_id(0), nsteps - 2)
        r, n = _rms(x_ref[...])
        hn = (n * g_ref[...]).astype(BF16)
        hn_ref[...] = hn
        outs = (q_ref, k_ref, v_ref, u_ref)
        for j in range(N_DEV):
            z = jnp.dot(hn, w_ref[j], preferred_element_type=F32)
            if j < 2:
                z = z * (HEAD_DIM ** -0.5)
            outs[j // 2][:, (j % 2) * 256:(j % 2 + 1) * 256] = z
        _gather2_end(plan, pl.program_id(0), nsteps)

    tok = lambda w: pl.BlockSpec((TM, w), lambda i: (i, 0))
    res = pl.pallas_call(
        body, name="qkvu", grid=(nsteps,),
        in_specs=[tok(D_MODEL), _const((1, D_MODEL)), _const(w_in.shape)] + [ANY_SPEC] * ns,
        out_specs=[tok(512)] * 4 + [tok(D_MODEL)] + [ANY_SPEC] * ns,
        out_shape=[jax.ShapeDtypeStruct((S, 512), F32)] * 4 + [jax.ShapeDtypeStruct((S, D_MODEL), BF16)]
        + _exchange_shapes(shards, (False,) * ns),
        scratch_shapes=_exchange_sems(ns),
        compiler_params=_params("arbitrary"),
    )(x, g1, w_in, *shards)
    return res[:5], res[5:]


def _attn_fill_bias(bias_s, slope_ref, hp, d):
    qi = lax.broadcasted_iota(jnp.int32, (SPAN, 2 * SPAN), 0)
    kj = lax.broadcasted_iota(jnp.int32, (SPAN, 2 * SPAN), 1)
    for t, diff in enumerate((qi + SPAN - kj, qi - kj)):
        valid = (diff >= 0) & (diff <= SPAN)
        dist = diff.astype(F32) * float(d)
        for h in range(2):
            bias_s[t, h * SPAN:(h + 1) * SPAN, :] = jnp.where(valid, -slope_ref[2 * hp + h] * dist, NEG)


def _stack_heads(x, is0):
    return jnp.concatenate([jnp.where(is0, x, 0.0), jnp.where(is0, 0.0, x)], axis=0)


def _unstack_heads(y, is0):
    return jnp.where(is0, y[0:SPAN], y[SPAN:2 * SPAN])


def _attn_block(i, g, d, nb, group):
    gr = min(d, group)
    gn = group // gr
    per = d // gr
    r = (i & (per - 1)) * gr + g % gr
    n = (i >> (per.bit_length() - 1)) + (g // gr) * (nb // gn)
    k0 = jnp.maximum(n - 1, 0)

    def ds(block, nrows):
        start = block * (SPAN * d) + r
        return pl.ds(start, nrows, stride=d) if d > 1 else pl.ds(start, nrows)

    return ds(n, SPAN), ds(k0, 2 * SPAN), jnp.where(n == 0, 1, 0)


def _attn_groups(S, d, group, writes_key_rows=False):
    nb = S // d // SPAN
    gn = group // min(d, group)
    assert nb >= 2 and nb % gn == 0 and (gn == 1 or nb // gn >= (3 if writes_key_rows else 2))
    return nb, d * nb // group


def _attn_fwd(slopes, q, k, v, shards):
    S = q.shape[0]
    ns = len(shards)
    steps = ATT_WIDTH // LANES

    def body(slope_ref, q_ref, k_ref, v_ref, *rest):
        o_ref, lse_ref = rest[ns:ns + 2]
        m_s, l_s, bias_s = rest[2 * ns + 2:2 * ns + 5]
        hp = pl.program_id(0)
        plan = _gather2_copies(rest[:ns], rest[ns + 2:2 * ns + 2], *rest[2 * ns + 5:])
        _gather2_begin(plan, hp, steps - 1)

        is0 = lax.broadcasted_iota(jnp.int32, (SPAN, LANES), 1) < HEAD_DIM
        for pi, d in enumerate(DILATIONS):
            nb, ngroups = _attn_groups(S, d, ATTN_GROUP_FWD)
            _attn_fill_bias(bias_s, slope_ref, hp, d)

            def group(i, carry, d=d, pi=pi, nb=nb):
                blocks = [_attn_block(i, g, d, nb, ATTN_GROUP_FWD) for g in range(ATTN_GROUP_FWD)]
                loaded = [(q_ref[rows, :], k_ref[krows, :].astype(BF16), v_ref[krows, :].astype(BF16))
                          for rows, krows, _ in blocks]
                new = []
                for (rows, krows, tab), (qb, kb, vb) in zip(blocks, loaded):
                    qs = _stack_heads(qb, is0).astype(BF16)
                    s = lax.dot_general(qs, kb, NT, preferred_element_type=F32) + bias_s[tab]
                    m = jnp.max(s, axis=-1, keepdims=True)
                    e = jnp.exp(s - m)
                    l = jnp.sum(e, axis=-1, keepdims=True)
                    pv = jnp.dot(e.astype(BF16), vb, preferred_element_type=F32)
                    new.append([_unstack_heads(jnp.broadcast_to(m, pv.shape), is0),
                                _unstack_heads(jnp.broadcast_to(l, pv.shape), is0), _unstack_heads(pv, is0)])
                if pi > 0:
                    old = [(m_s[rows, :], l_s[rows, :], o_ref[rows, :]) for rows, _, _ in blocks]
                    for st, (m_o, l_o, o_o) in zip(new, old):
                        m_n = jnp.maximum(m_o, st[0])
                        a_o = jnp.exp(m_o - m_n)
                        a_b = jnp.exp(st[0] - m_n)
                        st[:] = [m_n, a_o * l_o + a_b * st[1], a_o * o_o + a_b * st[2]]
                for (rows, _, _), (m_b, l_b, acc) in zip(blocks, new):
                    if pi == len(DILATIONS) - 1:
                        o_ref[rows, :] = acc / l_b
                        lse_ref[rows, :] = m_b + jnp.log(l_b)
                    else:
                        o_ref[rows, :] = acc
                        m_s[rows, :] = m_b
                        l_s[rows, :] = l_b
                return carry

            lax.fori_loop(0, ngroups, group, 0)

        _gather2_end(plan, hp, steps)

    col = pl.BlockSpec((S, LANES), lambda i: (0, i))
    res = pl.pallas_call(
        body, name="attn_fwd", grid=(steps,),
        in_specs=[pl.BlockSpec(memory_space=pltpu.SMEM), col, col, col] + [ANY_SPEC] * ns,
        out_specs=[col, col] + [ANY_SPEC] * ns,
        out_shape=[jax.ShapeDtypeStruct((S, ATT_WIDTH), F32)] * 2 + _exchange_shapes(shards, (False,) * ns),
        scratch_shapes=[pltpu.VMEM((S, LANES), F32), pltpu.VMEM((S, LANES), F32),
                        pltpu.VMEM((2, 2 * SPAN, 2 * SPAN), F32)] + _exchange_sems(ns),
        compiler_params=_params("arbitrary"),
    )(slopes, q, k, v, *shards)
    return res[0], res[1], res[2:]


def _pool_count(i, w):
    t = i * TM + lax.broadcasted_iota(jnp.int32, (TM, 1), 0)
    return jnp.minimum(t + 1, w).astype(F32)


def _mix_out(x, att, u, pool_w, pool_scale, w_out, shards):
    S = x.shape[0]
    ns = len(shards)
    nsteps = S // TM

    def body(x_ref, att_ref, u_ref, pw_ref, ps_ref, w_ref, *rest):
        h1_ref, mix_ref, dlt_ref = rest[ns:ns + 3]
        ubuf = rest[2 * ns + 3]
        i = pl.program_id(0)
        plan = _gather2_copies(rest[:ns], rest[ns + 3:2 * ns + 3], *rest[2 * ns + 4:])
        _gather2_begin(plan, i, nsteps - 1)

        @pl.when(i == 0)
        def _():
            ubuf[0:HALO, :] = jnp.zeros((HALO, POOL_WIDTH), F32)

        ubuf[HALO:HALO + TM, :] = u_ref[...]
        mix_ref[:, 0:ATT_WIDTH] = att_ref[...].astype(BF16)
        for g, w in enumerate(POOL_WINDOWS):
            cols = slice(g * POOL_GROUP, (g + 1) * POOL_GROUP)
            ug = ubuf[HALO:HALO + TM, cols]
            acc = ug
            for j in range(1, w):
                acc = acc + ubuf[HALO - j:HALO - j + TM, cols]
            dlt = (acc / _pool_count(i, w) - ug).astype(BF16)
            dlt_ref[:, cols] = dlt
            yg = jnp.dot(dlt, pw_ref[g].astype(BF16), preferred_element_type=F32) * ps_ref[:, cols]
            mix_ref[:, ATT_WIDTH + g * POOL_GROUP:ATT_WIDTH + (g + 1) * POOL_GROUP] = yg.astype(BF16)
        ubuf[0:HALO, :] = ubuf[TM:TM + HALO, :]
        h1_ref[...] = x_ref[...] + jnp.dot(mix_ref[...], w_ref[...], preferred_element_type=F32)
        _gather2_end(plan, i, nsteps)

    tok = lambda w: pl.BlockSpec((TM, w), lambda i: (i, 0))
    res = pl.pallas_call(
        body, name="mix_out", grid=(nsteps,),
        in_specs=[tok(D_MODEL), tok(ATT_WIDTH), tok(POOL_WIDTH), _const(pool_w.shape), _const((1, POOL_WIDTH)),
                  _const(w_out.shape)] + [ANY_SPEC] * ns,
        out_specs=[tok(D_MODEL), tok(D_MODEL), tok(POOL_WIDTH)] + [ANY_SPEC] * ns,
        out_shape=[jax.ShapeDtypeStruct((S, D_MODEL), F32), jax.ShapeDtypeStruct((S, D_MODEL), BF16),
                   jax.ShapeDtypeStruct((S, POOL_WIDTH), BF16)] + _exchange_shapes(shards, (False,) * ns),
        scratch_shapes=[pltpu.VMEM((TM + HALO, POOL_WIDTH), F32)] + _exchange_sems(ns),
        compiler_params=_params("arbitrary"),
    )(x, att, u, pool_w, pool_scale, w_out, *shards)
    return res[0], res[1], res[2], res[3:]


def _conv_fwd(stage, upre, prev, cw, cb):
    T = upre.shape[0]
    stage[0:HALO, :] = prev
    stage[HALO:HALO + T, :] = upre
    return cb + cw[0:1, :] * stage[HALO - 2:HALO - 2 + T, :] + cw[1:2, :] * stage[HALO - 1:HALO - 1 + T, :] + cw[2:3, :] * upre


def _ffn_fwd(h1, g2, w_up, conv_w, conv_b, w_down, shards):
    S = h1.shape[0]
    T = TM_FF
    ns = len(shards)
    nsteps = S // T

    def body(h1_ref, g_ref, wu_ref, cw_ref, cb_ref, wd_ref, *rest):
        h2_ref, hn_ref, up_ref, upc_ref = rest[ns:ns + 4]
        carry, stage = rest[2 * ns + 4:2 * ns + 6]
        i = pl.program_id(0)
        plan = _gather2_copies(rest[:ns], rest[ns + 4:2 * ns + 4], *rest[2 * ns + 6:])
        _gather2_begin(plan, i, nsteps // 2)

        @pl.when(i == 0)
        def _():
            carry[...] = jnp.zeros(carry.shape, F32)

        h1t = h1_ref[...]
        r, n = _rms(h1t)
        hn = (n * g_ref[...]).astype(BF16)
        hn_ref[...] = hn
        acc = h1t
        for j in range(4):
            conv = []
            for jj in (j, j + 4):
                upre = jnp.dot(hn, wu_ref[jj], preferred_element_type=F32)
                up_ref[jj] = upre.astype(BF16)
                conv.append(_conv_fwd(stage, upre, carry[jj], cw_ref[jj], cb_ref[jj]))
                upc_ref[jj] = conv[-1].astype(BF16)
                carry[jj] = stage[T:T + HALO, :]
            gate, val = conv
            a = gate * jax.nn.sigmoid(gate) * val
            acc = acc + jnp.dot(a.astype(BF16), wd_ref[j], preferred_element_type=F32)
        h2_ref[...] = acc
        _gather2_end(plan, i, nsteps)

    tok = lambda w: pl.BlockSpec((T, w), lambda i: (i, 0))
    res = pl.pallas_call(
        body, name="ffn_fwd", grid=(nsteps,),
        in_specs=[tok(D_MODEL), _const((1, D_MODEL)), _const(w_up.shape), _const(conv_w.shape), _const(conv_b.shape),
                  _const(w_down.shape)] + [ANY_SPEC] * ns,
        out_specs=[tok(D_MODEL), tok(D_MODEL)] + [pl.BlockSpec((N_DEV, T, FF_SHARD), lambda i: (0, i, 0))] * 2
        + [ANY_SPEC] * ns,
        out_shape=[jax.ShapeDtypeStruct((S, D_MODEL), F32), jax.ShapeDtypeStruct((S, D_MODEL), BF16)]
        + [jax.ShapeDtypeStruct((N_DEV, S, FF_SHARD), BF16)] * 2 + _exchange_shapes(shards, (False,) * ns),
        scratch_shapes=[pltpu.VMEM((N_DEV, HALO, FF_SHARD), F32), pltpu.VMEM((T + HALO, FF_SHARD), F32)]
        + _exchange_sems(ns),
        compiler_params=_params("arbitrary"),
    )(h1, g2, w_up, conv_w, conv_b, w_down, *shards)
    return res[0], res[1], res[2], res[3], res[4:]


def _head(h2, p, g3, w_pg, w_ple, g4, target):
    S = h2.shape[0]
    nt = S // TM

    def body(h2_ref, p_ref, g3_ref, wpg_ref, wple_ref, g4_ref, t_ref,
             loss_ref, dh2_ref, dh2b_ref, dwpg_ref, dwple_ref, dg3_ref, dg4_ref, lacc, pg_acc, ple_acc):
        i = pl.program_id(0)

        @pl.when(i == 0)
        def _():
            lacc[...] = jnp.zeros(lacc.shape, F32)
            pg_acc[...] = jnp.zeros(pg_acc.shape, F32)
            ple_acc[...] = jnp.zeros(ple_acc.shape, F32)
            dg3_ref[...] = jnp.zeros(dg3_ref.shape, F32)
            dg4_ref[...] = jnp.zeros(dg4_ref.shape, F32)

        h2t = h2_ref[...]
        g3, g4 = g3_ref[...], g4_ref[...]
        r3, n3 = _rms(h2t)
        hn3 = (n3 * g3).astype(BF16)
        pb = p_ref[...].astype(BF16)
        gs = jax.nn.sigmoid(jnp.dot(hn3, wpg_ref[...], preferred_element_type=F32))
        pe = jnp.dot(pb, wple_ref[...], preferred_element_type=F32)
        h3 = h2t + gs * pe
        r4, n4 = _rms(h3)
        err = n4 * g4 - t_ref[...]
        lacc[...] += _colsum(err * err)
        dy = err * (1.0 / D_MODEL)
        dg4_ref[...] += _colsum(dy * n4)
        dh3 = _rms_bwd(r4, n4, g4, dy)
        dpe = (dh3 * gs).astype(BF16)
        dgl = (dh3 * pe * gs * (1.0 - gs)).astype(BF16)
        ple_acc[...] += lax.dot_general(pb, dpe, TN, preferred_element_type=F32)
        pg_acc[...] += lax.dot_general(hn3, dgl, TN, preferred_element_type=F32)
        dhn3 = lax.dot_general(dgl, wpg_ref[...], NT, preferred_element_type=F32)
        dg3_ref[...] += _colsum(dhn3 * n3)
        dh2 = dh3 + _rms_bwd(r3, n3, g3, dhn3)
        dh2_ref[...] = dh2
        dh2b_ref[...] = dh2.astype(BF16)

        @pl.when(i == nt - 1)
        def _():
            tot = 0.5 / D_MODEL * jnp.sum(lacc[...], axis=-1, keepdims=True)
            loss_ref[...] = jnp.broadcast_to(tot, loss_ref.shape)
            dwpg_ref[...] = pg_acc[...].astype(BF16)
            dwple_ref[...] = ple_acc[...].astype(BF16)

    tok = lambda w: pl.BlockSpec((TM, w), lambda i: (i, 0))
    row = pl.BlockSpec((1, D_MODEL), lambda i: (0, 0))
    act = lambda dt: jax.ShapeDtypeStruct((S, D_MODEL), dt)
    whole = lambda r: pl.BlockSpec((r, D_MODEL), lambda i: (0, 0))
    return pl.pallas_call(
        body, name="head", grid=(nt,),
        in_specs=[tok(D_MODEL), tok(PLE_DIM), _const((1, D_MODEL)), _const(w_pg.shape), _const(w_ple.shape),
                  _const((1, D_MODEL)), tok(D_MODEL)],
        out_specs=[pl.BlockSpec((8, LANES), lambda i: (0, 0)), tok(D_MODEL), tok(D_MODEL), whole(D_MODEL),
                   whole(PLE_DIM), row, row],
        out_shape=[jax.ShapeDtypeStruct((8, LANES), F32), act(F32), act(BF16),
                   jax.ShapeDtypeStruct((D_MODEL, D_MODEL), BF16), jax.ShapeDtypeStruct((PLE_DIM, D_MODEL), BF16),
                   jax.ShapeDtypeStruct((1, D_MODEL), F32), jax.ShapeDtypeStruct((1, D_MODEL), F32)],
        scratch_shapes=[pltpu.VMEM((1, D_MODEL), F32), pltpu.VMEM((D_MODEL, D_MODEL), F32),
                        pltpu.VMEM((PLE_DIM, D_MODEL), F32)],
        compiler_params=_params("arbitrary"),
    )(h2, p, g3, w_pg, w_ple, g4, target)


def _wgrad(name, x, dy, x_kind, dy_kind, nj, k_dim, n_dim, tk=TK):
    S = x.shape[-2]
    nt = S // tk

    def spec(kind, width):
        if kind == "full":
            return pl.BlockSpec((tk, width), lambda j, t: (t, 0))
        return pl.BlockSpec((None, tk, width), lambda j, t: (j, t, 0))

    def body(x_ref, dy_ref, o_ref, acc):
        t = pl.program_id(1)

        @pl.when(t == 0)
        def _():
            acc[...] = jnp.zeros(acc.shape, F32)

        acc[...] += lax.dot_general(x_ref[...].astype(BF16), dy_ref[...], TN, preferred_element_type=F32)

        @pl.when(t == nt - 1)
        def _():
            o_ref[...] = acc[...].astype(BF16)

    return pl.pallas_call(
        body, name=name, grid=(nj, nt),
        in_specs=[spec(x_kind, k_dim), spec(dy_kind, n_dim)],
        out_specs=pl.BlockSpec((None, k_dim, n_dim), lambda j, t: (j, 0, 0)),
        out_shape=jax.ShapeDtypeStruct((nj, k_dim, n_dim), BF16),
        scratch_shapes=[pltpu.VMEM((k_dim, n_dim), F32)],
        compiler_params=_params("arbitrary", "arbitrary"),
    )(x, dy)


def _row_picker(T, off0, off1):
    r = lax.broadcasted_iota(jnp.int32, (2 * T, T + HALO), 0)
    c = lax.broadcasted_iota(jnp.int32, (2 * T, T + HALO), 1)
    want = jnp.where(r < T, r + off0, r - T + off1)
    return jnp.where(c == want, 1.0, 0.0).astype(BF16)


def _ffn_bwd_a(dh2b, up, upc, w_down, grads):
    S = dh2b.shape[0]
    T = TM_FF
    hb = T // HALO
    nsteps = S // T
    ng = len(grads)

    def body(dh_ref, up_ref, halo_ref, upc_ref, wd_ref, *rest):
        dwd_ref, dup_ref, dcw_ref, dcb_ref = rest[ng:ng + 4]
        stage, dwd_acc = rest[2 * ng + 4:2 * ng + 6]
        i = pl.program_id(0)
        copies = _exchange_copies(rest[:ng], rest[ng + 4:2 * ng + 4], (True,) * ng, *rest[2 * ng + 6:])

        @pl.when(i == 0)
        def _():
            dcw_ref[...] = jnp.zeros(dcw_ref.shape, F32)
            dcb_ref[...] = jnp.zeros(dcb_ref.shape, F32)
            dwd_acc[...] = jnp.zeros(dwd_acc.shape, F32)
            for cp in copies:
                cp.start()

        @pl.when(i == nsteps - 1)
        def _():
            for cp in copies:
                cp.wait()

        dh = dh_ref[...]
        pick = _row_picker(T, HALO - 2, HALO - 1)
        for j in range(4):
            da = lax.dot_general(dh, wd_ref[j], NT, preferred_element_type=F32)
            taps = []
            for jj in (j, j + 4):
                upre = up_ref[jj]
                stage[0:HALO, :] = jnp.where(i > 0, halo_ref[jj], jnp.zeros((HALO, FF_SHARD), BF16))
                stage[HALO:HALO + T, :] = upre
                prv = jnp.dot(pick, stage[...], preferred_element_type=F32)
                taps.append((prv[0:T], prv[T:2 * T], upre.astype(F32)))
            gate, val = upc_ref[j].astype(F32), upc_ref[j + 4].astype(F32)
            sg = jax.nn.sigmoid(gate)
            silu = gate * sg
            dwd_acc[j] += lax.dot_general((silu * val).astype(BF16), dh, TN, preferred_element_type=F32)
            dgate = (da * val) * (sg + silu * (1.0 - sg))
            dval = da * silu
            for jj, dup, tp in ((j, dgate, taps[0]), (j + 4, dval, taps[1])):
                dup_ref[jj] = dup.astype(BF16)
                dcb_ref[jj] += _colsum(dup)
                for kk in range(3):
                    dcw_ref[jj, kk:kk + 1, :] += _colsum(dup * tp[kk])

        @pl.when(i == nsteps - 1)
        def _():
            dwd_ref[...] = dwd_acc[...].astype(BF16)

    tok = lambda w: pl.BlockSpec((T, w), lambda i: (i, 0))
    shard = pl.BlockSpec((N_DEV, T, FF_SHARD), lambda i: (0, i, 0))
    res = pl.pallas_call(
        body, name="ffn_bwd_a", grid=(nsteps,),
        in_specs=[tok(D_MODEL), shard,
                  pl.BlockSpec((N_DEV, HALO, FF_SHARD), lambda i: (0, jnp.maximum(i * hb - 1, 0), 0)),
                  shard, _const(w_down.shape)] + [ANY_SPEC] * ng,
        out_specs=[_const(w_down.shape), shard,
                   pl.BlockSpec((N_DEV, 3, FF_SHARD), lambda i: (0, 0, 0)),
                   pl.BlockSpec((N_DEV, 1, FF_SHARD), lambda i: (0, 0, 0))] + [ANY_SPEC] * ng,
        out_shape=[jax.ShapeDtypeStruct(w_down.shape, BF16), jax.ShapeDtypeStruct((N_DEV, S, FF_SHARD), BF16),
                   jax.ShapeDtypeStruct((N_DEV, 3, FF_SHARD), F32), jax.ShapeDtypeStruct((N_DEV, 1, FF_SHARD), F32)]
        + _exchange_shapes(grads, (True,) * ng),
        scratch_shapes=[pltpu.VMEM((T + HALO, FF_SHARD), BF16), pltpu.VMEM(w_down.shape, F32)] + _exchange_sems(ng),
        compiler_params=_params("arbitrary"),
    )(dh2b, up, up, upc, w_down, *grads)
    return res[0], res[1], res[2], res[3], res[4:]


def _ffn_bwd_b(dup, conv_w, w_up, h1, g2, dh2, grads):
    S = h1.shape[0]
    T = TM_FF
    hb = T // HALO
    nt = S // T
    ng = len(grads)

    def body(dup_ref, halo_ref, cw_ref, wu_ref, h1_ref, g_ref, dh2_ref, *rest):
        dpre_ref, dh1_ref, dh1b_ref, dg_ref = rest[ng:ng + 4]
        stage = rest[2 * ng + 4]
        i = pl.program_id(0)
        copies = _exchange_copies(rest[:ng], rest[ng + 4:2 * ng + 4], (True,) * ng, *rest[2 * ng + 5:])

        @pl.when(i == 0)
        def _():
            dg_ref[...] = jnp.zeros(dg_ref.shape, F32)
            for cp in copies:
                cp.start()

        dhn = jnp.zeros((T, D_MODEL), F32)
        for jj in range(N_DEV):
            dup = dup_ref[jj].astype(F32)
            stage[0:T, :] = dup
            stage[T:T + HALO, :] = jnp.where(i < nt - 1, halo_ref[jj].astype(F32), 0.0)
            cw = cw_ref[jj]
            dpre = (cw[2:3, :] * dup + cw[1:2, :] * stage[1:1 + T, :] + cw[0:1, :] * stage[2:2 + T, :]).astype(BF16)
            dpre_ref[jj] = dpre
            dhn = dhn + lax.dot_general(dpre, wu_ref[jj], NT, preferred_element_type=F32)
        g = g_ref[...]
        r, n = _rms(h1_ref[...])
        dg_ref[...] += _colsum(dhn * n)
        dh1 = dh2_ref[...] + _rms_bwd(r, n, g, dhn)
        dh1_ref[...] = dh1
        dh1b_ref[...] = dh1.astype(BF16)

        @pl.when(i == nt - 1)
        def _():
            for cp in copies:
                cp.wait()

    tok = lambda w: pl.BlockSpec((T, w), lambda i: (i, 0))
    shard = pl.BlockSpec((N_DEV, T, FF_SHARD), lambda i: (0, i, 0))
    res = pl.pallas_call(
        body, name="ffn_bwd_b", grid=(nt,),
        in_specs=[shard,
                  pl.BlockSpec((N_DEV, HALO, FF_SHARD), lambda i: (0, jnp.minimum((i + 1) * hb, S // HALO - 1), 0)),
                  _const(conv_w.shape), _const(w_up.shape), tok(D_MODEL), _const((1, D_MODEL)), tok(D_MODEL)]
        + [ANY_SPEC] * ng,
        out_specs=[shard, tok(D_MODEL), tok(D_MODEL), pl.BlockSpec((1, D_MODEL), lambda i: (0, 0))] + [ANY_SPEC] * ng,
        out_shape=[jax.ShapeDtypeStruct((N_DEV, S, FF_SHARD), BF16), jax.ShapeDtypeStruct((S, D_MODEL), F32),
                   jax.ShapeDtypeStruct((S, D_MODEL), BF16), jax.ShapeDtypeStruct((1, D_MODEL), F32)]
        + _exchange_shapes(grads, (True,) * ng),
        scratch_shapes=[pltpu.VMEM((T + HALO, FF_SHARD), F32)] + _exchange_sems(ng),
        compiler_params=_params("arbitrary"),
    )(dup, dup, conv_w, w_up, h1, g2, dh2, *grads)
    return res[0], res[1], res[2], res[3], res[4:]


def _mix_bwd(dh1b, w_out, dlt, mix, pool_w, pool_scale):
    S = dh1b.shape[0]
    nt = S // TM

    def body(dh_ref, w_ref, dlt_ref, mix_ref, pw_ref, ps_ref, datt_ref, du_ref, dpw_ref, dps_ref, dwo_ref,
             stage, carry, wo_acc):
        i = pl.program_id(0)
        tile = nt - 1 - i

        @pl.when(i == 0)
        def _():
            dpw_ref[...] = jnp.zeros(dpw_ref.shape, F32)
            dps_ref[...] = jnp.zeros(dps_ref.shape, F32)
            carry[...] = jnp.zeros(carry.shape, F32)
            wo_acc[...] = jnp.zeros(wo_acc.shape, F32)

        wo_acc[...] += lax.dot_general(mix_ref[...], dh_ref[...], TN, preferred_element_type=F32)

        @pl.when(i == nt - 1)
        def _():
            dwo_ref[...] = wo_acc[...].astype(BF16)

        dmix = lax.dot_general(dh_ref[...], w_ref[...], NT, preferred_element_type=F32)
        datt_ref[...] = dmix[:, 0:ATT_WIDTH]
        for g, w in enumerate(POOL_WINDOWS):
            cols = slice(g * POOL_GROUP, (g + 1) * POOL_GROUP)
            dpool = dmix[:, ATT_WIDTH + g * POOL_GROUP:ATT_WIDTH + (g + 1) * POOL_GROUP]
            dl = dlt_ref[:, cols]
            pw = pw_ref[g].astype(BF16)
            yg = jnp.dot(dl, pw, preferred_element_type=F32)
            dps_ref[:, cols] += _colsum(dpool * yg)
            dy = (dpool * ps_ref[:, cols]).astype(BF16)
            dpw_ref[g] += lax.dot_general(dl, dy, TN, preferred_element_type=F32)
            ddlt = lax.dot_general(dy, pw, NT, preferred_element_type=F32)
            cg = ddlt / _pool_count(tile, w)
            stage[0:TM, :] = cg
            stage[TM:TM + HALO, :] = carry[:, cols]
            acc = cg
            for j in range(1, w):
                acc = acc + stage[j:j + TM, :]
            du_ref[:, cols] = acc - ddlt
            carry[:, cols] = cg[0:HALO, :]

    tok = lambda w: pl.BlockSpec((TM, w), lambda i: (nt - 1 - i, 0))
    return pl.pallas_call(
        body, name="mix_bwd", grid=(nt,),
        in_specs=[tok(D_MODEL), _const(w_out.shape), tok(POOL_WIDTH), tok(D_MODEL), _const(pool_w.shape),
                  _const((1, POOL_WIDTH))],
        out_specs=[tok(ATT_WIDTH), tok(POOL_WIDTH), pl.BlockSpec(pool_w.shape, lambda i: (0, 0, 0)),
                   pl.BlockSpec((1, POOL_WIDTH), lambda i: (0, 0)), pl.BlockSpec(w_out.shape, lambda i: (0, 0))],
        out_shape=[jax.ShapeDtypeStruct((S, ATT_WIDTH), F32), jax.ShapeDtypeStruct((S, POOL_WIDTH), F32),
                   jax.ShapeDtypeStruct(pool_w.shape, F32), jax.ShapeDtypeStruct((1, POOL_WIDTH), F32),
                   jax.ShapeDtypeStruct(w_out.shape, BF16)],
        scratch_shapes=[pltpu.VMEM((TM + HALO, POOL_GROUP), F32), pltpu.VMEM((HALO, POOL_WIDTH), F32),
                        pltpu.VMEM(w_out.shape, F32)],
        compiler_params=_params("arbitrary"),
    )(dh1b, w_out, dlt, mix, pool_w, pool_scale)


def _attn_bwd(slopes, q, k, v, o, lse, do, grads, scatter):
    S = q.shape[0]
    CH = 512
    ng = len(grads)
    steps = ATT_WIDTH // LANES

    def body(slope_ref, q_ref, k_ref, v_ref, o_ref, lse_ref, do_ref, *rest):
        dq_ref, dk_ref, dv_ref = rest[ng:ng + 3]
        dl_s, bias_s = rest[2 * ng + 3:2 * ng + 5]
        hp = pl.program_id(0)
        copies = _exchange_copies(rest[:ng], rest[ng + 3:2 * ng + 3], scatter, *rest[2 * ng + 5:])

        @pl.when(hp == 0)
        def _():
            for cp in copies:
                cp.start()

        is0 = lax.broadcasted_iota(jnp.int32, (SPAN, LANES), 1) < HEAD_DIM
        is0c = lax.broadcasted_iota(jnp.int32, (CH, LANES), 1) < HEAD_DIM

        def prep(ci, carry):
            rows = pl.ds(pl.multiple_of(ci * CH, CH), CH)
            prod = do_ref[rows, :] * o_ref[rows, :]
            d0 = jnp.sum(jnp.where(is0c, prod, 0.0), axis=-1, keepdims=True)
            d1 = jnp.sum(jnp.where(is0c, 0.0, prod), axis=-1, keepdims=True)
            dl_s[rows, :] = jnp.where(is0c, d0, d1)
            zero = jnp.zeros((CH, LANES), F32)
            dq_ref[rows, :] = zero
            dk_ref[rows, :] = zero
            dv_ref[rows, :] = zero
            return carry

        lax.fori_loop(0, S // CH, prep, 0)

        for d in DILATIONS:
            nb, ngroups = _attn_groups(S, d, ATTN_GROUP_BWD, writes_key_rows=True)
            _attn_fill_bias(bias_s, slope_ref, hp, d)

            def group(i, carry, d=d, nb=nb):
                blocks = [_attn_block(i, g, d, nb, ATTN_GROUP_BWD) for g in range(ATTN_GROUP_BWD)]
                loaded = [(q_ref[rows, :], do_ref[rows, :], lse_ref[rows, :], dl_s[rows, :], k_ref[krows, :],
                           v_ref[krows, :].astype(BF16)) for rows, krows, _ in blocks]
                new = []
                for (rows, krows, tab), (qb, dob, lse_b, dl_b, kf, vb) in zip(blocks, loaded):
                    kb = kf.astype(BF16)
                    qs = _stack_heads(qb, is0).astype(BF16)
                    dos = _stack_heads(dob, is0).astype(BF16)
                    lse_s = jnp.concatenate([lse_b[:, 0:1], lse_b[:, HEAD_DIM:HEAD_DIM + 1]], axis=0)
                    dl_s2 = jnp.concatenate([dl_b[:, 0:1], dl_b[:, HEAD_DIM:HEAD_DIM + 1]], axis=0)
                    s = lax.dot_general(qs, kb, NT, preferred_element_type=F32) + bias_s[tab]
                    pr = jnp.exp(s - lse_s)
                    dp = lax.dot_general(dos, vb, NT, preferred_element_type=F32)
                    ds = (pr * (dp - dl_s2)).astype(BF16)
                    dv_c = lax.dot_general(pr.astype(BF16), dos, TN, preferred_element_type=F32)
                    dk_c = lax.dot_general(ds, qs, TN, preferred_element_type=F32)
                    dq_c = _unstack_heads(jnp.dot(ds, kb, preferred_element_type=F32), is0)
                    new.append((dq_c, dk_c, dv_c))
                old = [(dq_ref[rows, :], dk_ref[krows, :], dv_ref[krows, :]) for rows, krows, _ in blocks]
                for (rows, krows, _), (dq_c, dk_c, dv_c), (dq_o, dk_o, dv_o) in zip(blocks, new, old):
                    dq_ref[rows, :] = dq_o + dq_c
                    dk_ref[krows, :] = dk_o + dk_c
                    dv_ref[krows, :] = dv_o + dv_c
                return carry

            lax.fori_loop(0, ngroups, group, 0)

        @pl.when(hp == steps - 1)
        def _():
            for cp in copies:
                cp.wait()

    col = pl.BlockSpec((S, LANES), lambda i: (0, i))
    res = pl.pallas_call(
        body, name="attn_bwd", grid=(steps,),
        in_specs=[pl.BlockSpec(memory_space=pltpu.SMEM)] + [col] * 6 + [ANY_SPEC] * ng,
        out_specs=[col] * 3 + [ANY_SPEC] * ng,
        out_shape=[jax.ShapeDtypeStruct((S, ATT_WIDTH), F32)] * 3 + _exchange_shapes(grads, scatter),
        scratch_shapes=[pltpu.VMEM((S, LANES), F32), pltpu.VMEM((2, 2 * SPAN, 2 * SPAN), F32)] + _exchange_sems(ng),
        compiler_params=_params("arbitrary"),
    )(slopes, q, k, v, o, lse, do, *grads)
    return res[0], res[1], res[2], res[3:]


def _in_bwd(dq, dk, dv, du, w_in, x, g1, dh1):
    S = x.shape[0]

    def body(dq_ref, dk_ref, dv_ref, du_ref, w_ref, x_ref, g_ref, dh1_ref, dz_ref, dx_ref, dg_ref):
        @pl.when(pl.program_id(0) == 0)
        def _():
            dg_ref[...] = jnp.zeros(dg_ref.shape, F32)

        srcs = (dq_ref, dk_ref, dv_ref, du_ref)
        dhn = jnp.zeros((TM, D_MODEL), F32)
        for j in range(N_DEV):
            dz = srcs[j // 2][:, (j % 2) * 256:(j % 2 + 1) * 256]
            if j < 2:
                dz = dz * (HEAD_DIM ** -0.5)
            dz = dz.astype(BF16)
            dz_ref[j] = dz
            dhn = dhn + lax.dot_general(dz, w_ref[j], NT, preferred_element_type=F32)
        g = g_ref[...]
        r, n = _rms(x_ref[...])
        dg_ref[...] += _colsum(dhn * n)
        dx_ref[...] = dh1_ref[...] + _rms_bwd(r, n, g, dhn)

    tok = lambda w: pl.BlockSpec((TM, w), lambda i: (i, 0))
    return pl.pallas_call(
        body, name="in_bwd", grid=(S // TM,),
        in_specs=[tok(512)] * 4 + [_const(w_in.shape), tok(D_MODEL), _const((1, D_MODEL)), tok(D_MODEL)],
        out_specs=[pl.BlockSpec((N_DEV, TM, 256), lambda i: (0, i, 0)), tok(D_MODEL),
                   pl.BlockSpec((1, D_MODEL), lambda i: (0, 0))],
        out_shape=[jax.ShapeDtypeStruct((N_DEV, S, 256), BF16), jax.ShapeDtypeStruct((S, D_MODEL), F32),
                   jax.ShapeDtypeStruct((1, D_MODEL), F32)],
        compiler_params=_params("arbitrary"),
    )(dq, dk, dv, du, w_in, x, g1, dh1)


def _adamw(name, parts, w, m, v):
    R, C = w.shape
    rb = max([r for r in range(16, R // 2 + 1, 16) if R % r == 0 and r * C <= 512 * 1024], default=R)

    def body(p_ref, w_ref, m_ref, v_ref, g_ref, d_ref, mo_ref, vo_ref):
        g = p_ref[0].astype(F32)
        for s in range(1, N_DEV):
            g = g + p_ref[s].astype(F32)
        g_ref[...] = g
        d_ref[...], mo_ref[...], vo_ref[...] = _adam_update(g, w_ref[...], m_ref[...], v_ref[...])

    blk = pl.BlockSpec((rb, C), lambda i: (i, 0))
    return pl.pallas_call(
        body, name=name, grid=(R // rb,),
        in_specs=[pl.BlockSpec((N_DEV, rb, C), lambda i: (0, i, 0)), blk, blk, blk],
        out_specs=[blk] * 4,
        out_shape=[jax.ShapeDtypeStruct((R, C), F32)] * 4,
        compiler_params=_params("arbitrary"),
    )(parts, w, m, v)


def _adam_update(g, w, m, v):
    m_new = ADAM_B1 * m + (1.0 - ADAM_B1) * g
    v_new = ADAM_B2 * v + (1.0 - ADAM_B2) * (g * g)
    m_hat = m_new / (1.0 - ADAM_B1 ** ADAM_STEP)
    v_hat = v_new / (1.0 - ADAM_B2 ** ADAM_STEP)
    return -ADAM_LR * (m_hat / (jnp.sqrt(v_hat) + ADAM_EPS) + ADAM_WD * w), m_new, v_new


def _adamw_small(parts, loss_parts, ws, ms, vs):
    n = len(ws)

    def body(*refs):
        p_refs, lp_ref = refs[:n], refs[n]
        w_refs, m_refs, v_refs = refs[n + 1:2 * n + 1], refs[2 * n + 1:3 * n + 1], refs[3 * n + 1:4 * n + 1]
        outs = refs[4 * n + 1:]
        for i in range(n):
            g = p_refs[i][0]
            for s in range(1, N_DEV):
                g = g + p_refs[i][s]
            d, m_new, v_new = _adam_update(g, w_refs[i][...], m_refs[i][...], v_refs[i][...])
            outs[i][...] = g
            outs[n + i][...] = d
            outs[2 * n + i][...] = m_new
            outs[3 * n + i][...] = v_new
        tot = lp_ref[0]
        for s in range(1, N_DEV):
            tot = tot + lp_ref[s]
        outs[4 * n][...] = tot

    shapes = [jax.ShapeDtypeStruct(w.shape, F32) for w in ws]
    res = pl.pallas_call(
        body, name="adamw_replicated",
        out_shape=shapes * 4 + [jax.ShapeDtypeStruct(loss_parts.shape[1:], F32)],
        compiler_params=_params(),
    )(*parts, loss_parts, *ws, *ms, *vs)
    return res[:n], res[n:2 * n], res[2 * n:3 * n], res[3 * n:4 * n], res[4 * n]


def _gather2(name, arrays):
    n = len(arrays)

    def body(*refs):
        first, passed, last = _gather2_copies(refs[:n], refs[n:2 * n], *refs[2 * n:])
        for cp in first:
            cp.start()
        for arrival, cp in passed:
            arrival.wait_recv()
            cp.start()
        for wait in last:
            wait()

    return pl.pallas_call(
        body, name=name,
        in_specs=[ANY_SPEC] * n, out_specs=[ANY_SPEC] * n, out_shape=_exchange_shapes(arrays, (False,) * n),
        scratch_shapes=_exchange_sems(n),
    )(*arrays)


def _dw_in_exchange(hn, dz, small):
    S = hn.shape[0]
    nt = S // TK
    ns = len(small)
    kd, nd = hn.shape[1], dz.shape[2]
    me_arr = (4 * lax.axis_index("x") + 2 * lax.axis_index("y") + lax.axis_index("c")).astype(jnp.int32).reshape(1)

    def body(me_ref, x_ref, dy_ref, *rest):
        recv_ref = rest[ns]
        acc, stage, send_sems, recv_sems, own_sem = rest[2 * ns + 1:2 * ns + 6]
        j, t = pl.program_id(0), pl.program_id(1)
        x, y, c = lax.axis_index("x"), lax.axis_index("y"), lax.axis_index("c")
        me = 4 * x + 2 * y + c
        small_copies = _exchange_copies(rest[:ns], rest[ns + 1:2 * ns + 1], (False,) * ns, *rest[2 * ns + 6:])

        @pl.when((j == 0) & (t == 0))
        def _():
            for cp in small_copies:
                cp.start()

        @pl.when(t == 0)
        def _():
            acc[...] = jnp.zeros(acc.shape, F32)

        acc[...] += lax.dot_general(x_ref[...], dy_ref[...], TN, preferred_element_type=F32)

        def to_owner(k, owner):
            return pltpu.make_async_remote_copy(
                src_ref=stage.at[owner], dst_ref=recv_ref.at[me], send_sem=send_sems.at[k], recv_sem=recv_sems.at[k],
                device_id=(owner // 4, (owner // 2) % 2, owner % 2), device_id_type=MESH)

        own = pltpu.make_async_copy(stage.at[me], recv_ref.at[me], own_sem)

        @pl.when(t == nt - 1)
        def _():
            owner = (me + 1 + j) % N_DEV
            stage[owner] = acc[...].astype(BF16)

            @pl.when(j < N_DEV - 1)
            def _():
                to_owner(j, owner).start()

            @pl.when(j == N_DEV - 1)
            def _():
                own.start()
                own.wait()
                for k in range(N_DEV - 1):
                    to_owner(k, me).wait_send()
                    to_owner(k, me).wait_recv()
                for cp in small_copies:
                    cp.wait()

    slab = lambda j, me_ref: (me_ref[0] + 1 + j) % N_DEV
    grid_spec = pltpu.PrefetchScalarGridSpec(
        num_scalar_prefetch=1, grid=(N_DEV, nt),
        in_specs=[pl.BlockSpec((TK, kd), lambda j, t, me_ref: (t, 0)),
                  pl.BlockSpec((None, TK, nd), lambda j, t, me_ref: (slab(j, me_ref), t, 0))] + [ANY_SPEC] * ns,
        out_specs=[ANY_SPEC] * (ns + 1),
        scratch_shapes=[pltpu.VMEM((kd, nd), F32), pltpu.VMEM((N_DEV, kd, nd), BF16),
                        pltpu.SemaphoreType.DMA((N_DEV - 1,)), pltpu.SemaphoreType.DMA((N_DEV - 1,)),
                        pltpu.SemaphoreType.DMA] + _exchange_sems(ns))
    res = pl.pallas_call(
        body, name="dw_in_exchange", grid_spec=grid_spec,
        out_shape=[jax.ShapeDtypeStruct((N_DEV, kd, nd), BF16)] + _exchange_shapes(small, (False,) * ns),
        compiler_params=_params("arbitrary", "arbitrary"),
    )(me_arr, hn, dz, *small)
    return res[0], res[1:]


def kernel(x, p, ln_mix, w_in, pool_w, pool_scale, w_out, ln_ffn, w_up, conv_w, conv_b, w_down, ln_ple, w_ple_gate, w_ple, ln_final, loss_target, m_ln_mix, m_w_in, m_pool_w, m_pool_scale, m_w_out, m_ln_ffn, m_w_up, m_conv_w, m_conv_b, m_w_down, m_ln_ple, m_w_ple_gate, m_w_ple, m_ln_final, v_ln_mix, v_w_in, v_pool_w, v_pool_scale, v_w_out, v_ln_ffn, v_w_up, v_conv_w, v_conv_b, v_w_down, v_ln_ple, v_w_ple_gate, v_w_ple, v_ln_final):
    xs, ps, tgt, pool_w0 = x[0], p[0, 0], loss_target[0], pool_w[0]
    slopes = jnp.exp2(-8.0 * (jnp.arange(N_HEADS, dtype=F32) + 1.0) / N_HEADS)
    conv_b_s = conv_b.reshape(N_DEV, 1, FF_SHARD)

    (w_in_g,) = _gather2("gather_w_in", [w_in[0].astype(BF16)])
    w_down_b = w_down[0].astype(BF16)
    (q, k, v, u, hn1), (w_out_g, w_down_g0) = _qkvu(xs, ln_mix, w_in_g, [w_out[0].astype(BF16), w_down_b[:W_DOWN_CUT]])
    att, lse, (w_up_g, conv_w_g) = _attn_fwd(slopes, q, k, v, [w_up[0].astype(BF16), conv_w[0]])
    w_out_f = w_out_g.reshape(D_MODEL, D_MODEL)
    h1, mix, dlt, (w_down_g1,) = _mix_out(xs, att, u, pool_w0, pool_scale, w_out_f, [w_down_b[W_DOWN_CUT:]])
    w_down_f = jnp.concatenate([w_down_g0, w_down_g1], axis=1).reshape(4, FF_SHARD, D_MODEL)
    h2, hn2, up, upc, (w_pg_g, w_ple_g) = _ffn_fwd(h1, ln_ffn, w_up_g, conv_w_g, conv_b_s, w_down_f,
                                                   [w_ple_gate[0].astype(BF16), w_ple[0].astype(BF16)])
    w_pg_f = w_pg_g.reshape(D_MODEL, D_MODEL)
    w_ple_f = jnp.transpose(w_ple_g, (1, 0, 2)).reshape(PLE_DIM, D_MODEL)
    loss_blk, dh2, dh2b, d_w_pg, d_w_ple, d_ln_ple, d_ln_final = _head(
        h2, ps, ln_ple, w_pg_f, w_ple_f, ln_final.reshape(1, D_MODEL), tgt)

    d_w_pg = d_w_pg.reshape(N_DEV, D_MODEL // N_DEV, D_MODEL)
    d_w_ple = jnp.transpose(d_w_ple.reshape(PLE_DIM, N_DEV, LANES), (1, 0, 2))
    d_w_down, dup, d_conv_w, d_conv_b, (r_w_pg, r_w_ple) = _ffn_bwd_a(dh2b, up, upc, w_down_f, [d_w_pg, d_w_ple])
    d_w_down = d_w_down.reshape(N_DEV, D_FF // N_DEV, D_MODEL)
    dpre, dh1, dh1b, d_ln_ffn, (r_conv_w, r_w_down) = _ffn_bwd_b(
        dup, conv_w_g, w_up_g, h1, ln_ffn, dh2, [d_conv_w, d_w_down])
    datt, du, d_pool_w, d_pool_scale, d_w_out = _mix_bwd(dh1b, w_out_f, dlt, mix, pool_w0, pool_scale)
    d_w_out = d_w_out.reshape(N_DEV, D_MODEL // N_DEV, D_MODEL)
    d_w_up = _wgrad("dw_up", dpre, hn2, "lead", "full", N_DEV, FF_SHARD, D_MODEL)
    rep_late = [d_pool_w, d_pool_scale, d_ln_ffn, d_conv_b.reshape(1, 2 * D_FF), d_ln_ple, d_ln_final, loss_blk]
    dq, dk, dv, received = _attn_bwd(slopes, q, k, v, att, lse, datt, [d_w_out, d_w_up] + rep_late,
                                     (True, True) + (False,) * len(rep_late))
    r_w_out, r_w_up, r_rep = received[0], received[1], list(received[2:])
    dz, grad_x, d_ln_mix = _in_bwd(dq, dk, dv, du, w_in_g, xs, ln_mix, dh1)

    rep_names = ("ln_mix", "pool_w", "pool_scale", "ln_ffn", "conv_b", "ln_ple", "ln_final")
    rep_w = [ln_mix, pool_w0, pool_scale, ln_ffn, conv_b, ln_ple, ln_final.reshape(1, D_MODEL)]
    rep_m = [m_ln_mix, m_pool_w[0], m_pool_scale, m_ln_ffn, m_conv_b, m_ln_ple, m_ln_final.reshape(1, D_MODEL)]
    rep_v = [v_ln_mix, v_pool_w[0], v_pool_scale, v_ln_ffn, v_conv_b, v_ln_ple, v_ln_final.reshape(1, D_MODEL)]
    r_w_in, (r_ln_mix,) = _dw_in_exchange(hn1, dz, [d_ln_mix])
    small = _adamw_small([r_ln_mix] + r_rep[:-1], r_rep[-1], rep_w, rep_m, rep_v)
    loss = small[4][0, 0]

    sharded = {}
    sharded["w_in"] = _adamw("adamw_w_in", r_w_in, w_in[0], m_w_in[0], v_w_in[0])
    sharded["w_out"] = _adamw("adamw_w_out", r_w_out, w_out[0], m_w_out[0], v_w_out[0])
    sharded["w_up"] = [t.T for t in _adamw("adamw_w_up", r_w_up, w_up[0].T, m_w_up[0].T, v_w_up[0].T)]
    sharded["conv_w"] = _adamw("adamw_conv_w", r_conv_w, conv_w[0], m_conv_w[0], v_conv_w[0])
    sharded["w_down"] = _adamw("adamw_w_down", r_w_down, w_down[0], m_w_down[0], v_w_down[0])
    sharded["w_ple_gate"] = _adamw("adamw_w_ple_gate", r_w_pg, w_ple_gate[0], m_w_ple_gate[0], v_w_ple_gate[0])
    sharded["w_ple"] = _adamw("adamw_w_ple", r_w_ple, w_ple[0], m_w_ple[0], v_w_ple[0])

    shapes = dict(w_in=w_in, w_out=w_out, w_up=w_up, conv_w=conv_w, w_down=w_down, w_ple_gate=w_ple_gate, w_ple=w_ple,
                  ln_mix=ln_mix, pool_w=pool_w, pool_scale=pool_scale, ln_ffn=ln_ffn, conv_b=conv_b, ln_ple=ln_ple,
                  ln_final=ln_final)

    def leaf(kind, n):
        src = sharded[n][kind] if n in sharded else small[kind][rep_names.index(n)]
        return src.reshape(shapes[n].shape)

    order = ("ln_mix", "w_in", "pool_w", "pool_scale", "w_out", "ln_ffn", "w_up", "conv_w", "conv_b", "w_down", "ln_ple",
             "w_ple_gate", "w_ple", "ln_final")
    outs = [loss, grad_x[None]]
    for kind in range(4):
        outs += [leaf(kind, n) for n in order]
    return tuple(outs)
```

```python
import jax
import jax.numpy as jnp
from jax import lax
from jax.experimental import pallas as pl
from jax.experimental.pallas import tpu as pltpu

F32 = jnp.float32
BF16 = jnp.bfloat16

N_DEV = 8
D_MODEL = 1024
ATT_WIDTH = 512
POOL_WIDTH = 512
N_HEADS = 8
HEAD_DIM = 64
SPAN = 128
DILATIONS = (1, 4, 16)
POOL_WINDOWS = (2, 4, 8, 16)
POOL_GROUP = 128
D_FF = 2816
FF_SHARD = 2 * D_FF // N_DEV
PLE_DIM = 256
EPS = 1e-6
NEG = -1e30

ADAM_LR = 0.001
ADAM_B1 = 0.9
ADAM_B2 = 0.999
ADAM_EPS = 1e-08
ADAM_WD = 0.01
ADAM_STEP = 10

LANES = 128
HALO = 16
TM = 512
TM_FF = 256
TK = 4096
ATTN_GROUP_FWD = 16
ATTN_GROUP_BWD = 8
VMEM_LIMIT = 56 * 1024 * 1024

MESH = pl.DeviceIdType.MESH
NT = (((1,), (1,)), ((), ()))
TN = (((0,), (0,)), ((), ()))


def _params(*sem):
    return pltpu.CompilerParams(dimension_semantics=sem or None, vmem_limit_bytes=VMEM_LIMIT)


def _const(shape):
    n = len(shape)
    return pl.BlockSpec(shape, lambda *_: (0,) * n, pipeline_mode=pl.Buffered(1))


def _rms(h):
    r = lax.rsqrt(jnp.mean(h * h, axis=-1, keepdims=True) + EPS)
    return r, h * r


def _rms_bwd(r, n, g, dhn):
    dn = dhn * g
    return r * (dn - n * jnp.mean(dn * n, axis=-1, keepdims=True))


def _colsum(a):
    return jnp.sum(a, axis=0, keepdims=True)


def _gather2_copies(ins, outs, send_sems, recv_sems, local_sems):
    n = len(ins)
    x, y, c = lax.axis_index("x"), lax.axis_index("y"), lax.axis_index("c")
    slot = lambda px, py, pc: 4 * px + 2 * py + pc
    chips = [(x, 1 - y), (1 - x, y), (1 - x, 1 - y)]
    first, passed, last = [], [], []

    def remote(a, r, src, dst_slot, to):
        return pltpu.make_async_remote_copy(
            src_ref=src, dst_ref=outs[a].at[dst_slot],
            send_sem=send_sems.at[a * (N_DEV - 1) + r], recv_sem=recv_sems.at[a * (N_DEV - 1) + r],
            device_id=to, device_id_type=MESH)

    for a in range(n):
        mine = pltpu.make_async_copy(ins[a], outs[a].at[slot(x, y, c)], local_sems.at[a])
        to_sibling = remote(a, 0, ins[a], slot(x, y, c), (x, y, 1 - c))
        first += [mine, to_sibling]
        last += [mine.wait, to_sibling.wait_send, to_sibling.wait_recv]
        for r, (px, py) in enumerate(chips, start=1):
            to_chip = remote(a, r, ins[a], slot(x, y, c), (px, py, c))
            onward = remote(a, 3 + r, outs[a].at[slot(px, py, c)], slot(px, py, c), (x, y, 1 - c))
            first.append(to_chip)
            passed.append((to_chip, onward))
            last += [to_chip.wait_send, onward.wait_send, onward.wait_recv]
    return first, passed, last


def _gather2_begin(plan, step, pass_step):
    first, passed, _ = plan

    @pl.when(step == 0)
    def _():
        for cp in first:
            cp.start()

    @pl.when(step == pass_step)
    def _():
        for arrival, cp in passed:
            arrival.wait_recv()
            cp.start()


def _gather2_end(plan, step, nsteps):
    @pl.when(step == nsteps - 1)
    def _():
        for wait in plan[2]:
            wait()


ANY_SPEC = pl.BlockSpec(memory_space=pl.ANY)


def _exchange_shapes(arrays, scatter):
    out = []
    for a, s in zip(arrays, scatter):
        slab = a.shape[1:] if s else a.shape
        out.append(jax.ShapeDtypeStruct((N_DEV,) + tuple(slab), a.dtype))
    return out


def _exchange_sems(n):
    return [pltpu.SemaphoreType.DMA((n * (N_DEV - 1),)), pltpu.SemaphoreType.DMA((n * (N_DEV - 1),)),
            pltpu.SemaphoreType.DMA((n,))]


def _exchange_copies(ins, outs, scatter, send_sems, recv_sems, local_sems):
    n = len(ins)
    x, y, c = lax.axis_index("x"), lax.axis_index("y"), lax.axis_index("c")
    me = 4 * x + 2 * y + c
    copies = []
    for a in range(n):
        src = ins[a].at[me] if scatter[a] else ins[a]
        copies.append(pltpu.make_async_copy(src, outs[a].at[me], local_sems.at[a]))
    for k in range(1, N_DEV):
        px = 1 - x if k & 4 else x
        py = 1 - y if k & 2 else y
        pc = 1 - c if k & 1 else c
        pid = 4 * px + 2 * py + pc
        for a in range(n):
            src = ins[a].at[pid] if scatter[a] else ins[a]
            copies.append(pltpu.make_async_remote_copy(
                src_ref=src, dst_ref=outs[a].at[me],
                send_sem=send_sems.at[a * (N_DEV - 1) + k - 1], recv_sem=recv_sems.at[a * (N_DEV - 1) + k - 1],
                device_id=(px, py, pc), device_id_type=MESH))
    return copies


def _qkvu(x, g1, w_in, shards):
    S = x.shape[0]
    ns = len(shards)
    nsteps = S // TM

    def body(x_ref, g_ref, w_ref, *rest):
        q_ref, k_ref, v_ref, u_ref, hn_ref = rest[ns:ns + 5]
        plan = _gather2_copies(rest[:ns], rest[ns + 5:2 * ns + 5], *rest[2 * ns + 5:])
        _gather2_begin(plan, pl.program_id(0), nsteps - 2)
        r, n = _rms(x_ref[...])
        hn = (n * g_ref[...]).astype(BF16)
        hn_ref[...] = hn
        outs = (q_ref, k_ref, v_ref, u_ref)
        for j in range(N_DEV):
            z = jnp.dot(hn, w_ref[j], preferred_element_type=F32)
            if j < 2:
                z = z * (HEAD_DIM ** -0.5)
            outs[j // 2][:, (j % 2) * 256:(j % 2 + 1) * 256] = z
        _gather2_end(plan, pl.program_id(0), nsteps)

    tok = lambda w: pl.BlockSpec((TM, w), lambda i: (i, 0))
    res = pl.pallas_call(
        body, name="qkvu", grid=(nsteps,),
        in_specs=[tok(D_MODEL), _const((1, D_MODEL)), _const(w_in.shape)] + [ANY_SPEC] * ns,
        out_specs=[tok(512)] * 4 + [tok(D_MODEL)] + [ANY_SPEC] * ns,
        out_shape=[jax.ShapeDtypeStruct((S, 512), F32)] * 4 + [jax.ShapeDtypeStruct((S, D_MODEL), BF16)]
        + _exchange_shapes(shards, (False,) * ns),
        scratch_shapes=_exchange_sems(ns),
        compiler_params=_params("arbitrary"),
    )(x, g1, w_in, *shards)
    return res[:5], res[5:]


def _attn_fill_bias(bias_s, slope_ref, hp, d):
    qi = lax.broadcasted_iota(jnp.int32, (SPAN, 2 * SPAN), 0)
    kj = lax.broadcasted_iota(jnp.int32, (SPAN, 2 * SPAN), 1)
    for t, diff in enumerate((qi + SPAN - kj, qi - kj)):
        valid = (diff >= 0) & (diff <= SPAN)
        dist = diff.astype(F32) * float(d)
        for h in range(2):
            bias_s[t, h * SPAN:(h + 1) * SPAN, :] = jnp.where(valid, -slope_ref[2 * hp + h] * dist, NEG)


def _stack_heads(x, is0):
    return jnp.concatenate([jnp.where(is0, x, 0.0), jnp.where(is0, 0.0, x)], axis=0)


def _unstack_heads(y, is0):
    return jnp.where(is0, y[0:SPAN], y[SPAN:2 * SPAN])


def _attn_block(i, g, d, nb, group):
    gr = min(d, group)
    gn = group // gr
    per = d // gr
    r = (i & (per - 1)) * gr + g % gr
    n = (i >> (per.bit_length() - 1)) + (g // gr) * (nb // gn)
    k0 = jnp.maximum(n - 1, 0)

    def ds(block, nrows):
        start = block * (SPAN * d) + r
        return pl.ds(start, nrows, stride=d) if d > 1 else pl.ds(start, nrows)

    return ds(n, SPAN), ds(k0, 2 * SPAN), jnp.where(n == 0, 1, 0)


def _attn_groups(S, d, group, writes_key_rows=False):
    nb = S // d // SPAN
    gn = group // min(d, group)
    assert nb >= 2 and nb % gn == 0 and (gn == 1 or nb // gn >= (3 if writes_key_rows else 2))
    return nb, d * nb // group


def _attn_fwd(slopes, q, k, v, shards):
    S = q.shape[0]
    ns = len(shards)
    steps = ATT_WIDTH // LANES

    def body(slope_ref, q_ref, k_ref, v_ref, *rest):
        o_ref, lse_ref = rest[ns:ns + 2]
        m_s, l_s, bias_s = rest[2 * ns + 2:2 * ns + 5]
        hp = pl.program_id(0)
        plan = _gather2_copies(rest[:ns], rest[ns + 2:2 * ns + 2], *rest[2 * ns + 5:])
        _gather2_begin(plan, hp, steps - 1)

        is0 = lax.broadcasted_iota(jnp.int32, (SPAN, LANES), 1) < HEAD_DIM
        for pi, d in enumerate(DILATIONS):
            nb, ngroups = _attn_groups(S, d, ATTN_GROUP_FWD)
            _attn_fill_bias(bias_s, slope_ref, hp, d)

            def group(i, carry, d=d, pi=pi, nb=nb):
                blocks = [_attn_block(i, g, d, nb, ATTN_GROUP_FWD) for g in range(ATTN_GROUP_FWD)]
                loaded = [(q_ref[rows, :], k_ref[krows, :].astype(BF16), v_ref[krows, :].astype(BF16))
                          for rows, krows, _ in blocks]
                new = []
                for (rows, krows, tab), (qb, kb, vb) in zip(blocks, loaded):
                    qs = _stack_heads(qb, is0).astype(BF16)
                    s = lax.dot_general(qs, kb, NT, preferred_element_type=F32) + bias_s[tab]
                    m = jnp.max(s, axis=-1, keepdims=True)
                    e = jnp.exp(s - m)
                    l = jnp.sum(e, axis=-1, keepdims=True)
                    pv = jnp.dot(e.astype(BF16), vb, preferred_element_type=F32)
                    new.append([_unstack_heads(jnp.broadcast_to(m, pv.shape), is0),
                                _unstack_heads(jnp.broadcast_to(l, pv.shape), is0), _unstack_heads(pv, is0)])
                if pi > 0:
                    old = [(m_s[rows, :], l_s[rows, :], o_ref[rows, :]) for rows, _, _ in blocks]
                    for st, (m_o, l_o, o_o) in zip(new, old):
                        m_n = jnp.maximum(m_o, st[0])
                        a_o = jnp.exp(m_o - m_n)
                        a_b = jnp.exp(st[0] - m_n)
                        st[:] = [m_n, a_o * l_o + a_b * st[1], a_o * o_o + a_b * st[2]]
                for (rows, _, _), (m_b, l_b, acc) in zip(blocks, new):
                    if pi == len(DILATIONS) - 1:
                        o_ref[rows, :] = acc / l_b
                        lse_ref[rows, :] = m_b + jnp.log(l_b)
                    else:
                        o_ref[rows, :] = acc
                        m_s[rows, :] = m_b
                        l_s[rows, :] = l_b
                return carry

            lax.fori_loop(0, ngroups, group, 0)

        _gather2_end(plan, hp, steps)

    col = pl.BlockSpec((S, LANES), lambda i: (0, i))
    res = pl.pallas_call(
        body, name="attn_fwd", grid=(steps,),
        in_specs=[pl.BlockSpec(memory_space=pltpu.SMEM), col, col, col] + [ANY_SPEC] * ns,
        out_specs=[col, col] + [ANY_SPEC] * ns,
        out_shape=[jax.ShapeDtypeStruct((S, ATT_WIDTH), F32)] * 2 + _exchange_shapes(shards, (False,) * ns),
        scratch_shapes=[pltpu.VMEM((S, LANES), F32), pltpu.VMEM((S, LANES), F32),
                        pltpu.VMEM((2, 2 * SPAN, 2 * SPAN), F32)] + _exchange_sems(ns),
        compiler_params=_params("arbitrary"),
    )(slopes, q, k, v, *shards)
    return res[0], res[1], res[2:]


def _pool_count(i, w):
    t = i * TM + lax.broadcasted_iota(jnp.int32, (TM, 1), 0)
    return jnp.minimum(t + 1, w).astype(F32)


def _mix_out(x, att, u, pool_w, pool_scale, w_out, shards):
    S = x.shape[0]
    ns = len(shards)
    nsteps = S // TM

    def body(x_ref, att_ref, u_ref, pw_ref, ps_ref, w_ref, *rest):
        h1_ref, mix_ref, dlt_ref = rest[ns:ns + 3]
        ubuf = rest[2 * ns + 3]
        i = pl.program_id(0)
        plan = _gather2_copies(rest[:ns], rest[ns + 3:2 * ns + 3], *rest[2 * ns + 4:])
        _gather2_begin(plan, i, nsteps - 1)

        @pl.when(i == 0)
        def _():
            ubuf[0:HALO, :] = jnp.zeros((HALO, POOL_WIDTH), F32)

        ubuf[HALO:HALO + TM, :] = u_ref[...]
        mix_ref[:, 0:ATT_WIDTH] = att_ref[...].astype(BF16)
        for g, w in enumerate(POOL_WINDOWS):
            cols = slice(g * POOL_GROUP, (g + 1) * POOL_GROUP)
            ug = ubuf[HALO:HALO + TM, cols]
            acc = ug
            for j in range(1, w):
                acc = acc + ubuf[HALO - j:HALO - j + TM, cols]
            dlt = (acc / _pool_count(i, w) - ug).astype(BF16)
            dlt_ref[:, cols] = dlt
            yg = jnp.dot(dlt, pw_ref[g].astype(BF16), preferred_element_type=F32) * ps_ref[:, cols]
            mix_ref[:, ATT_WIDTH + g * POOL_GROUP:ATT_WIDTH + (g + 1) * POOL_GROUP] = yg.astype(BF16)
        ubuf[0:HALO, :] = ubuf[TM:TM + HALO, :]
        h1_ref[...] = x_ref[...] + jnp.dot(mix_ref[...], w_ref[...], preferred_element_type=F32)
        _gather2_end(plan, i, nsteps)

    tok = lambda w: pl.BlockSpec((TM, w), lambda i: (i, 0))
    res = pl.pallas_call(
        body, name="mix_out", grid=(nsteps,),
        in_specs=[tok(D_MODEL), tok(ATT_WIDTH), tok(POOL_WIDTH), _const(pool_w.shape), _const((1, POOL_WIDTH)),
                  _const(w_out.shape)] + [ANY_SPEC] * ns,
        out_specs=[tok(D_MODEL), tok(D_MODEL), tok(POOL_WIDTH)] + [ANY_SPEC] * ns,
        out_shape=[jax.ShapeDtypeStruct((S, D_MODEL), F32), jax.ShapeDtypeStruct((S, D_MODEL), BF16),
                   jax.ShapeDtypeStruct((S, POOL_WIDTH), BF16)] + _exchange_shapes(shards, (False,) * ns),
        scratch_shapes=[pltpu.VMEM((TM + HALO, POOL_WIDTH), F32)] + _exchange_sems(ns),
        compiler_params=_params("arbitrary"),
    )(x, att, u, pool_w, pool_scale, w_out, *shards)
    return res[0], res[1], res[2], res[3:]


def _conv_fwd(stage, upre, prev, cw, cb):
    T = upre.shape[0]
    stage[0:HALO, :] = prev
    stage[HALO:HALO + T, :] = upre
    return cb + cw[0:1, :] * stage[HALO - 2:HALO - 2 + T, :] + cw[1:2, :] * stage[HALO - 1:HALO - 1 + T, :] + cw[2:3, :] * upre


def _ffn_down(h1, a, w_down):
    S = h1.shape[0]

    def body(h1_ref, a_ref, wd_ref, h2_ref):
        acc = h1_ref[...]
        for j in range(4):
            acc = acc + jnp.dot(a_ref[j], wd_ref[j], preferred_element_type=F32)
        h2_ref[...] = acc

    tok = pl.BlockSpec((TM, D_MODEL), lambda i: (i, 0))
    return pl.pallas_call(
        body, name="ffn_down", grid=(S // TM,),
        in_specs=[tok, pl.BlockSpec((4, TM, FF_SHARD), lambda i: (0, i, 0)), _const(w_down.shape)],
        out_specs=tok, out_shape=jax.ShapeDtypeStruct((S, D_MODEL), F32),
        compiler_params=_params("arbitrary"),
    )(h1, a, w_down)


def _ffn_up(h1, g2, w_up, conv_w, conv_b, shards):
    S = h1.shape[0]
    T = TM_FF
    ns = len(shards)
    nsteps = S // T

    def body(h1_ref, g_ref, wu_ref, cw_ref, cb_ref, *rest):
        a_ref, hn_ref, up_ref, upc_ref = rest[ns:ns + 4]
        carry, stage = rest[2 * ns + 4:2 * ns + 6]
        i = pl.program_id(0)
        plan = _gather2_copies(rest[:ns], rest[ns + 4:2 * ns + 4], *rest[2 * ns + 6:])
        _gather2_begin(plan, i, nsteps // 2)

        @pl.when(i == 0)
        def _():
            carry[...] = jnp.zeros(carry.shape, F32)

        h1t = h1_ref[...]
        r, n = _rms(h1t)
        hn = (n * g_ref[...]).astype(BF16)
        hn_ref[...] = hn
        for j in range(4):
            conv = []
            for jj in (j, j + 4):
                upre = jnp.dot(hn, wu_ref[jj], preferred_element_type=F32)
                up_ref[jj] = upre.astype(BF16)
                conv.append(_conv_fwd(stage, upre, carry[jj], cw_ref[jj], cb_ref[jj]))
                upc_ref[jj] = conv[-1].astype(BF16)
                carry[jj] = stage[T:T + HALO, :]
            gate, val = conv
            a_ref[j] = (gate * jax.nn.sigmoid(gate) * val).astype(BF16)
        _gather2_end(plan, i, nsteps)

    tok = lambda w: pl.BlockSpec((T, w), lambda i: (i, 0))
    shard = lambda n: pl.BlockSpec((n, T, FF_SHARD), lambda i: (0, i, 0))
    res = pl.pallas_call(
        body, name="ffn_up", grid=(nsteps,),
        in_specs=[tok(D_MODEL), _const((1, D_MODEL)), _const(w_up.shape), _const(conv_w.shape), _const(conv_b.shape)]
        + [ANY_SPEC] * ns,
        out_specs=[shard(4), tok(D_MODEL), shard(N_DEV), shard(N_DEV)] + [ANY_SPEC] * ns,
        out_shape=[jax.ShapeDtypeStruct((4, S, FF_SHARD), BF16), jax.ShapeDtypeStruct((S, D_MODEL), BF16)]
        + [jax.ShapeDtypeStruct((N_DEV, S, FF_SHARD), BF16)] * 2 + _exchange_shapes(shards, (False,) * ns),
        scratch_shapes=[pltpu.VMEM((N_DEV, HALO, FF_SHARD), F32), pltpu.VMEM((T + HALO, FF_SHARD), F32)]
        + _exchange_sems(ns),
        compiler_params=_params("arbitrary"),
    )(h1, g2, w_up, conv_w, conv_b, *shards)
    return res[0], res[1], res[2], res[3], res[4:]


def _head(h2, p, g3, w_pg, w_ple, g4, target):
    S = h2.shape[0]
    nt = S // TM

    def body(h2_ref, p_ref, g3_ref, wpg_ref, wple_ref, g4_ref, t_ref,
             loss_ref, dh2_ref, dh2b_ref, dwpg_ref, dwple_ref, dg3_ref, dg4_ref, lacc, pg_acc, ple_acc):
        i = pl.program_id(0)

        @pl.when(i == 0)
        def _():
            lacc[...] = jnp.zeros(lacc.shape, F32)
            pg_acc[...] = jnp.zeros(pg_acc.shape, F32)
            ple_acc[...] = jnp.zeros(ple_acc.shape, F32)
            dg3_ref[...] = jnp.zeros(dg3_ref.shape, F32)
            dg4_ref[...] = jnp.zeros(dg4_ref.shape, F32)

        h2t = h2_ref[...]
        g3, g4 = g3_ref[...], g4_ref[...]
        r3, n3 = _rms(h2t)
        hn3 = (n3 * g3).astype(BF16)
        pb = p_ref[...].astype(BF16)
        gs = jax.nn.sigmoid(jnp.dot(hn3, wpg_ref[...], preferred_element_type=F32))
        pe = jnp.dot(pb, wple_ref[...], preferred_element_type=F32)
        h3 = h2t + gs * pe
        r4, n4 = _rms(h3)
        err = n4 * g4 - t_ref[...]
        lacc[...] += _colsum(err * err)
        dy = err * (1.0 / D_MODEL)
        dg4_ref[...] += _colsum(dy * n4)
        dh3 = _rms_bwd(r4, n4, g4, dy)
        dpe = (dh3 * gs).astype(BF16)
        dgl = (dh3 * pe * gs * (1.0 - gs)).astype(BF16)
        ple_acc[...] += lax.dot_general(pb, dpe, TN, preferred_element_type=F32)
        pg_acc[...] += lax.dot_general(hn3, dgl, TN, preferred_element_type=F32)
        dhn3 = lax.dot_general(dgl, wpg_ref[...], NT, preferred_element_type=F32)
        dg3_ref[...] += _colsum(dhn3 * n3)
        dh2 = dh3 + _rms_bwd(r3, n3, g3, dhn3)
        dh2_ref[...] = dh2
        dh2b_ref[...] = dh2.astype(BF16)

        @pl.when(i == nt - 1)
        def _():
            tot = 0.5 / D_MODEL * jnp.sum(lacc[...], axis=-1, keepdims=True)
            loss_ref[...] = jnp.broadcast_to(tot, loss_ref.shape)
            dwpg_ref[...] = pg_acc[...].astype(BF16)
            dwple_ref[...] = ple_acc[...].astype(BF16)

    tok = lambda w: pl.BlockSpec((TM, w), lambda i: (i, 0))
    row = pl.BlockSpec((1, D_MODEL), lambda i: (0, 0))
    act = lambda dt: jax.ShapeDtypeStruct((S, D_MODEL), dt)
    whole = lambda r: pl.BlockSpec((r, D_MODEL), lambda i: (0, 0))
    return pl.pallas_call(
        body, name="head", grid=(nt,),
        in_specs=[tok(D_MODEL), tok(PLE_DIM), _const((1, D_MODEL)), _const(w_pg.shape), _const(w_ple.shape),
                  _const((1, D_MODEL)), tok(D_MODEL)],
        out_specs=[pl.BlockSpec((8, LANES), lambda i: (0, 0)), tok(D_MODEL), tok(D_MODEL), whole(D_MODEL),
                   whole(PLE_DIM), row, row],
        out_shape=[jax.ShapeDtypeStruct((8, LANES), F32), act(F32), act(BF16),
                   jax.ShapeDtypeStruct((D_MODEL, D_MODEL), BF16), jax.ShapeDtypeStruct((PLE_DIM, D_MODEL), BF16),
                   jax.ShapeDtypeStruct((1, D_MODEL), F32), jax.ShapeDtypeStruct((1, D_MODEL), F32)],
        scratch_shapes=[pltpu.VMEM((1, D_MODEL), F32), pltpu.VMEM((D_MODEL, D_MODEL), F32),
                        pltpu.VMEM((PLE_DIM, D_MODEL), F32)],
        compiler_params=_params("arbitrary"),
    )(h2, p, g3, w_pg, w_ple, g4, target)


def _wgrad(name, x, dy, x_kind, dy_kind, nj, k_dim, n_dim, tk=TK):
    S = x.shape[-2]
    nt = S // tk

    def spec(kind, width):
        if kind == "full":
            return pl.BlockSpec((tk, width), lambda j, t: (t, 0))
        return pl.BlockSpec((None, tk, width), lambda j, t: (j, t, 0))

    def body(x_ref, dy_ref, o_ref, acc):
        t = pl.program_id(1)

        @pl.when(t == 0)
        def _():
            acc[...] = jnp.zeros(acc.shape, F32)

        acc[...] += lax.dot_general(x_ref[...].astype(BF16), dy_ref[...], TN, preferred_element_type=F32)

        @pl.when(t == nt - 1)
        def _():
            o_ref[...] = acc[...].astype(BF16)

    return pl.pallas_call(
        body, name=name, grid=(nj, nt),
        in_specs=[spec(x_kind, k_dim), spec(dy_kind, n_dim)],
        out_specs=pl.BlockSpec((None, k_dim, n_dim), lambda j, t: (j, 0, 0)),
        out_shape=jax.ShapeDtypeStruct((nj, k_dim, n_dim), BF16),
        scratch_shapes=[pltpu.VMEM((k_dim, n_dim), F32)],
        compiler_params=_params("arbitrary", "arbitrary"),
    )(x, dy)


def _row_picker(T, off0, off1):
    r = lax.broadcasted_iota(jnp.int32, (2 * T, T + HALO), 0)
    c = lax.broadcasted_iota(jnp.int32, (2 * T, T + HALO), 1)
    want = jnp.where(r < T, r + off0, r - T + off1)
    return jnp.where(c == want, 1.0, 0.0).astype(BF16)


def _ffn_bwd_a(dh2b, up, upc, w_down, grads):
    S = dh2b.shape[0]
    T = TM_FF
    hb = T // HALO
    nsteps = S // T
    ng = len(grads)

    def body(dh_ref, up_ref, halo_ref, upc_ref, wd_ref, *rest):
        dwd_ref, dup_ref, dcw_ref, dcb_ref = rest[ng:ng + 4]
        stage, dwd_acc = rest[2 * ng + 4:2 * ng + 6]
        i = pl.program_id(0)
        copies = _exchange_copies(rest[:ng], rest[ng + 4:2 * ng + 4], (True,) * ng, *rest[2 * ng + 6:])

        @pl.when(i == 0)
        def _():
            dcw_ref[...] = jnp.zeros(dcw_ref.shape, F32)
            dcb_ref[...] = jnp.zeros(dcb_ref.shape, F32)
            dwd_acc[...] = jnp.zeros(dwd_acc.shape, F32)
            for cp in copies:
                cp.start()

        @pl.when(i == nsteps - 1)
        def _():
            for cp in copies:
                cp.wait()

        dh = dh_ref[...]
        pick = _row_picker(T, HALO - 2, HALO - 1)
        for j in range(4):
            da = lax.dot_general(dh, wd_ref[j], NT, preferred_element_type=F32)
            taps = []
            for jj in (j, j + 4):
                upre = up_ref[jj]
                stage[0:HALO, :] = jnp.where(i > 0, halo_ref[jj], jnp.zeros((HALO, FF_SHARD), BF16))
                stage[HALO:HALO + T, :] = upre
                prv = jnp.dot(pick, stage[...], preferred_element_type=F32)
                taps.append((prv[0:T], prv[T:2 * T], upre.astype(F32)))
            gate, val = upc_ref[j].astype(F32), upc_ref[j + 4].astype(F32)
            sg = jax.nn.sigmoid(gate)
            silu = gate * sg
            dwd_acc[j] += lax.dot_general((silu * val).astype(BF16), dh, TN, preferred_element_type=F32)
            dgate = (da * val) * (sg + silu * (1.0 - sg))
            dval = da * silu
            for jj, dup, tp in ((j, dgate, taps[0]), (j + 4, dval, taps[1])):
                dup_ref[jj] = dup.astype(BF16)
                dcb_ref[jj] += _colsum(dup)
                for kk in range(3):
                    dcw_ref[jj, kk:kk + 1, :] += _colsum(dup * tp[kk])

        @pl.when(i == nsteps - 1)
        def _():
            dwd_ref[...] = dwd_acc[...].astype(BF16)

    tok = lambda w: pl.BlockSpec((T, w), lambda i: (i, 0))
    shard = pl.BlockSpec((N_DEV, T, FF_SHARD), lambda i: (0, i, 0))
    res = pl.pallas_call(
        body, name="ffn_bwd_a", grid=(nsteps,),
        in_specs=[tok(D_MODEL), shard,
                  pl.BlockSpec((N_DEV, HALO, FF_SHARD), lambda i: (0, jnp.maximum(i * hb - 1, 0), 0)),
                  shard, _const(w_down.shape)] + [ANY_SPEC] * ng,
        out_specs=[_const(w_down.shape), shard,
                   pl.BlockSpec((N_DEV, 3, FF_SHARD), lambda i: (0, 0, 0)),
                   pl.BlockSpec((N_DEV, 1, FF_SHARD), lambda i: (0, 0, 0))] + [ANY_SPEC] * ng,
        out_shape=[jax.ShapeDtypeStruct(w_down.shape, BF16), jax.ShapeDtypeStruct((N_DEV, S, FF_SHARD), BF16),
                   jax.ShapeDtypeStruct((N_DEV, 3, FF_SHARD), F32), jax.ShapeDtypeStruct((N_DEV, 1, FF_SHARD), F32)]
        + _exchange_shapes(grads, (True,) * ng),
        scratch_shapes=[pltpu.VMEM((T + HALO, FF_SHARD), BF16), pltpu.VMEM(w_down.shape, F32)] + _exchange_sems(ng),
        compiler_params=_params("arbitrary"),
    )(dh2b, up, up, upc, w_down, *grads)
    return res[0], res[1], res[2], res[3], res[4:]


def _ffn_bwd_b(dup, conv_w, w_up, h1, g2, dh2, grads):
    S = h1.shape[0]
    T = TM_FF
    hb = T // HALO
    nt = S // T
    ng = len(grads)

    def body(dup_ref, halo_ref, cw_ref, wu_ref, h1_ref, g_ref, dh2_ref, *rest):
        dpre_ref, dh1_ref, dh1b_ref, dg_ref = rest[ng:ng + 4]
        stage = rest[2 * ng + 4]
        i = pl.program_id(0)
        copies = _exchange_copies(rest[:ng], rest[ng + 4:2 * ng + 4], (True,) * ng, *rest[2 * ng + 5:])

        @pl.when(i == 0)
        def _():
            dg_ref[...] = jnp.zeros(dg_ref.shape, F32)
            for cp in copies:
                cp.start()

        dhn = jnp.zeros((T, D_MODEL), F32)
        for jj in range(N_DEV):
            dup = dup_ref[jj].astype(F32)
            stage[0:T, :] = dup
            stage[T:T + HALO, :] = jnp.where(i < nt - 1, halo_ref[jj].astype(F32), 0.0)
            cw = cw_ref[jj]
            dpre = (cw[2:3, :] * dup + cw[1:2, :] * stage[1:1 + T, :] + cw[0:1, :] * stage[2:2 + T, :]).astype(BF16)
            dpre_ref[jj] = dpre
            dhn = dhn + lax.dot_general(dpre, wu_ref[jj], NT, preferred_element_type=F32)
        g = g_ref[...]
        r, n = _rms(h1_ref[...])
        dg_ref[...] += _colsum(dhn * n)
        dh1 = dh2_ref[...] + _rms_bwd(r, n, g, dhn)
        dh1_ref[...] = dh1
        dh1b_ref[...] = dh1.astype(BF16)

        @pl.when(i == nt - 1)
        def _():
            for cp in copies:
                cp.wait()

    tok = lambda w: pl.BlockSpec((T, w), lambda i: (i, 0))
    shard = pl.BlockSpec((N_DEV, T, FF_SHARD), lambda i: (0, i, 0))
    res = pl.pallas_call(
        body, name="ffn_bwd_b", grid=(nt,),
        in_specs=[shard,
                  pl.BlockSpec((N_DEV, HALO, FF_SHARD), lambda i: (0, jnp.minimum((i + 1) * hb, S // HALO - 1), 0)),
                  _const(conv_w.shape), _const(w_up.shape), tok(D_MODEL), _const((1, D_MODEL)), tok(D_MODEL)]
        + [ANY_SPEC] * ng,
        out_specs=[shard, tok(D_MODEL), tok(D_MODEL), pl.BlockSpec((1, D_MODEL), lambda i: (0, 0))] + [ANY_SPEC] * ng,
        out_shape=[jax.ShapeDtypeStruct((N_DEV, S, FF_SHARD), BF16), jax.ShapeDtypeStruct((S, D_MODEL), F32),
                   jax.ShapeDtypeStruct((S, D_MODEL), BF16), jax.ShapeDtypeStruct((1, D_MODEL), F32)]
        + _exchange_shapes(grads, (True,) * ng),
        scratch_shapes=[pltpu.VMEM((T + HALO, FF_SHARD), F32)] + _exchange_sems(ng),
        compiler_params=_params("arbitrary"),
    )(dup, dup, conv_w, w_up, h1, g2, dh2, *grads)
    return res[0], res[1], res[2], res[3], res[4:]


def _mix_bwd(dh1b, w_out, dlt, mix, pool_w, pool_scale):
    S = dh1b.shape[0]
    nt = S // TM

    def body(dh_ref, w_ref, dlt_ref, mix_ref, pw_ref, ps_ref, datt_ref, du_ref, dpw_ref, dps_ref, dwo_ref,
             stage, carry, wo_acc):
        i = pl.program_id(0)
        tile = nt - 1 - i

        @pl.when(i == 0)
        def _():
            dpw_ref[...] = jnp.zeros(dpw_ref.shape, F32)
            dps_ref[...] = jnp.zeros(dps_ref.shape, F32)
            carry[...] = jnp.zeros(carry.shape, F32)
            wo_acc[...] = jnp.zeros(wo_acc.shape, F32)

        wo_acc[...] += lax.dot_general(mix_ref[...], dh_ref[...], TN, preferred_element_type=F32)

        @pl.when(i == nt - 1)
        def _():
            dwo_ref[...] = wo_acc[...].astype(BF16)

        dmix = lax.dot_general(dh_ref[...], w_ref[...], NT, preferred_element_type=F32)
        datt_ref[...] = dmix[:, 0:ATT_WIDTH]
        for g, w in enumerate(POOL_WINDOWS):
            cols = slice(g * POOL_GROUP, (g + 1) * POOL_GROUP)
            dpool = dmix[:, ATT_WIDTH + g * POOL_GROUP:ATT_WIDTH + (g + 1) * POOL_GROUP]
            dl = dlt_ref[:, cols]
            pw = pw_ref[g].astype(BF16)
            yg = jnp.dot(dl, pw, preferred_element_type=F32)
            dps_ref[:, cols] += _colsum(dpool * yg)
            dy = (dpool * ps_ref[:, cols]).astype(BF16)
            dpw_ref[g] += lax.dot_general(dl, dy, TN, preferred_element_type=F32)
            ddlt = lax.dot_general(dy, pw, NT, preferred_element_type=F32)
            cg = ddlt / _pool_count(tile, w)
            stage[0:TM, :] = cg
            stage[TM:TM + HALO, :] = carry[:, cols]
            acc = cg
            for j in range(1, w):
                acc = acc + stage[j:j + TM, :]
            du_ref[:, cols] = acc - ddlt
            carry[:, cols] = cg[0:HALO, :]

    tok = lambda w: pl.BlockSpec((TM, w), lambda i: (nt - 1 - i, 0))
    return pl.pallas_call(
        body, name="mix_bwd", grid=(nt,),
        in_specs=[tok(D_MODEL), _const(w_out.shape), tok(POOL_WIDTH), tok(D_MODEL), _const(pool_w.shape),
                  _const((1, POOL_WIDTH))],
        out_specs=[tok(ATT_WIDTH), tok(POOL_WIDTH), pl.BlockSpec(pool_w.shape, lambda i: (0, 0, 0)),
                   pl.BlockSpec((1, POOL_WIDTH), lambda i: (0, 0)), pl.BlockSpec(w_out.shape, lambda i: (0, 0))],
        out_shape=[jax.ShapeDtypeStruct((S, ATT_WIDTH), F32), jax.ShapeDtypeStruct((S, POOL_WIDTH), F32),
                   jax.ShapeDtypeStruct(pool_w.shape, F32), jax.ShapeDtypeStruct((1, POOL_WIDTH), F32),
                   jax.ShapeDtypeStruct(w_out.shape, BF16)],
        scratch_shapes=[pltpu.VMEM((TM + HALO, POOL_GROUP), F32), pltpu.VMEM((HALO, POOL_WIDTH), F32),
                        pltpu.VMEM(w_out.shape, F32)],
        compiler_params=_params("arbitrary"),
    )(dh1b, w_out, dlt, mix, pool_w, pool_scale)


def _attn_bwd(slopes, q, k, v, o, lse, do, grads, scatter):
    S = q.shape[0]
    CH = 512
    ng = len(grads)
    steps = ATT_WIDTH // LANES

    def body(slope_ref, q_ref, k_ref, v_ref, o_ref, lse_ref, do_ref, *rest):
        dq_ref, dk_ref, dv_ref = rest[ng:ng + 3]
        dl_s, bias_s = rest[2 * ng + 3:2 * ng + 5]
        hp = pl.program_id(0)
        copies = _exchange_copies(rest[:ng], rest[ng + 3:2 * ng + 3], scatter, *rest[2 * ng + 5:])

        @pl.when(hp == 0)
        def _():
            for cp in copies:
                cp.start()

        is0 = lax.broadcasted_iota(jnp.int32, (SPAN, LANES), 1) < HEAD_DIM
        is0c = lax.broadcasted_iota(jnp.int32, (CH, LANES), 1) < HEAD_DIM

        def prep(ci, carry):
            rows = pl.ds(pl.multiple_of(ci * CH, CH), CH)
            prod = do_ref[rows, :] * o_ref[rows, :]
            d0 = jnp.sum(jnp.where(is0c, prod, 0.0), axis=-1, keepdims=True)
            d1 = jnp.sum(jnp.where(is0c, 0.0, prod), axis=-1, keepdims=True)
            dl_s[rows, :] = jnp.where(is0c, d0, d1)
            zero = jnp.zeros((CH, LANES), F32)
            dq_ref[rows, :] = zero
            dk_ref[rows, :] = zero
            dv_ref[rows, :] = zero
            return carry

        lax.fori_loop(0, S // CH, prep, 0)

        for d in DILATIONS:
            nb, ngroups = _attn_groups(S, d, ATTN_GROUP_BWD, writes_key_rows=True)
            _attn_fill_bias(bias_s, slope_ref, hp, d)

            def group(i, carry, d=d, nb=nb):
                blocks = [_attn_block(i, g, d, nb, ATTN_GROUP_BWD) for g in range(ATTN_GROUP_BWD)]
                loaded = [(q_ref[rows, :], do_ref[rows, :], lse_ref[rows, :], dl_s[rows, :], k_ref[krows, :],
                           v_ref[krows, :].astype(BF16)) for rows, krows, _ in blocks]
                new = []
                for (rows, krows, tab), (qb, dob, lse_b, dl_b, kf, vb) in zip(blocks, loaded):
                    kb = kf.astype(BF16)
                    qs = _stack_heads(qb, is0).astype(BF16)
                    dos = _stack_heads(dob, is0).astype(BF16)
                    lse_s = jnp.concatenate([lse_b[:, 0:1], lse_b[:, HEAD_DIM:HEAD_DIM + 1]], axis=0)
                    dl_s2 = jnp.concatenate([dl_b[:, 0:1], dl_b[:, HEAD_DIM:HEAD_DIM + 1]], axis=0)
                    s = lax.dot_general(qs, kb, NT, preferred_element_type=F32) + bias_s[tab]
                    pr = jnp.exp(s - lse_s)
                    dp = lax.dot_general(dos, vb, NT, preferred_element_type=F32)
                    ds = (pr * (dp - dl_s2)).astype(BF16)
                    dv_c = lax.dot_general(pr.astype(BF16), dos, TN, preferred_element_type=F32)
                    dk_c = lax.dot_general(ds, qs, TN, preferred_element_type=F32)
                    dq_c = _unstack_heads(jnp.dot(ds, kb, preferred_element_type=F32), is0)
                    new.append((dq_c, dk_c, dv_c))
                old = [(dq_ref[rows, :], dk_ref[krows, :], dv_ref[krows, :]) for rows, krows, _ in blocks]
                for (rows, krows, _), (dq_c, dk_c, dv_c), (dq_o, dk_o, dv_o) in zip(blocks, new, old):
                    dq_ref[rows, :] = dq_o + dq_c
                    dk_ref[krows, :] = dk_o + dk_c
                    dv_ref[krows, :] = dv_o + dv_c
                return carry

            lax.fori_loop(0, ngroups, group, 0)

        @pl.when(hp == steps - 1)
        def _():
            for cp in copies:
                cp.wait()

    col = pl.BlockSpec((S, LANES), lambda i: (0, i))
    res = pl.pallas_call(
        body, name="attn_bwd", grid=(steps,),
        in_specs=[pl.BlockSpec(memory_space=pltpu.SMEM)] + [col] * 6 + [ANY_SPEC] * ng,
        out_specs=[col] * 3 + [ANY_SPEC] * ng,
        out_shape=[jax.ShapeDtypeStruct((S, ATT_WIDTH), F32)] * 3 + _exchange_shapes(grads, scatter),
        scratch_shapes=[pltpu.VMEM((S, LANES), F32), pltpu.VMEM((2, 2 * SPAN, 2 * SPAN), F32)] + _exchange_sems(ng),
        compiler_params=_params("arbitrary"),
    )(slopes, q, k, v, o, lse, do, *grads)
    return res[0], res[1], res[2], res[3:]


def _in_bwd(dq, dk, dv, du, w_in, x, g1, dh1):
    S = x.shape[0]

    def body(dq_ref, dk_ref, dv_ref, du_ref, w_ref, x_ref, g_ref, dh1_ref, dz_ref, dx_ref, dg_ref):
        @pl.when(pl.program_id(0) == 0)
        def _():
            dg_ref[...] = jnp.zeros(dg_ref.shape, F32)

        srcs = (dq_ref, dk_ref, dv_ref, du_ref)
        dhn = jnp.zeros((TM, D_MODEL), F32)
        for j in range(N_DEV):
            dz = srcs[j // 2][:, (j % 2) * 256:(j % 2 + 1) * 256]
            if j < 2:
                dz = dz * (HEAD_DIM ** -0.5)
            dz = dz.astype(BF16)
            dz_ref[j] = dz
            dhn = dhn + lax.dot_general(dz, w_ref[j], NT, preferred_element_type=F32)
        g = g_ref[...]
        r, n = _rms(x_ref[...])
        dg_ref[...] += _colsum(dhn * n)
        dx_ref[...] = dh1_ref[...] + _rms_bwd(r, n, g, dhn)

    tok = lambda w: pl.BlockSpec((TM, w), lambda i: (i, 0))
    return pl.pallas_call(
        body, name="in_bwd", grid=(S // TM,),
        in_specs=[tok(512)] * 4 + [_const(w_in.shape), tok(D_MODEL), _const((1, D_MODEL)), tok(D_MODEL)],
        out_specs=[pl.BlockSpec((N_DEV, TM, 256), lambda i: (0, i, 0)), tok(D_MODEL),
                   pl.BlockSpec((1, D_MODEL), lambda i: (0, 0))],
        out_shape=[jax.ShapeDtypeStruct((N_DEV, S, 256), BF16), jax.ShapeDtypeStruct((S, D_MODEL), F32),
                   jax.ShapeDtypeStruct((1, D_MODEL), F32)],
        compiler_params=_params("arbitrary"),
    )(dq, dk, dv, du, w_in, x, g1, dh1)


def _adamw(name, parts, w, m, v):
    R, C = w.shape
    rb = max([r for r in range(16, R // 2 + 1, 16) if R % r == 0 and r * C <= 512 * 1024], default=R)

    def body(p_ref, w_ref, m_ref, v_ref, g_ref, d_ref, mo_ref, vo_ref):
        g = p_ref[0].astype(F32)
        for s in range(1, N_DEV):
            g = g + p_ref[s].astype(F32)
        g_ref[...] = g
        d_ref[...], mo_ref[...], vo_ref[...] = _adam_update(g, w_ref[...], m_ref[...], v_ref[...])

    blk = pl.BlockSpec((rb, C), lambda i: (i, 0))
    return pl.pallas_call(
        body, name=name, grid=(R // rb,),
        in_specs=[pl.BlockSpec((N_DEV, rb, C), lambda i: (0, i, 0)), blk, blk, blk],
        out_specs=[blk] * 4,
        out_shape=[jax.ShapeDtypeStruct((R, C), F32)] * 4,
        compiler_params=_params("arbitrary"),
    )(parts, w, m, v)


def _adam_update(g, w, m, v):
    m_new = ADAM_B1 * m + (1.0 - ADAM_B1) * g
    v_new = ADAM_B2 * v + (1.0 - ADAM_B2) * (g * g)
    m_hat = m_new / (1.0 - ADAM_B1 ** ADAM_STEP)
    v_hat = v_new / (1.0 - ADAM_B2 ** ADAM_STEP)
    return -ADAM_LR * (m_hat / (jnp.sqrt(v_hat) + ADAM_EPS) + ADAM_WD * w), m_new, v_new


def _adamw_small(parts, loss_parts, ws, ms, vs):
    n = len(ws)

    def body(*refs):
        p_refs, lp_ref = refs[:n], refs[n]
        w_refs, m_refs, v_refs = refs[n + 1:2 * n + 1], refs[2 * n + 1:3 * n + 1], refs[3 * n + 1:4 * n + 1]
        outs = refs[4 * n + 1:]
        for i in range(n):
            g = p_refs[i][0]
            for s in range(1, N_DEV):
                g = g + p_refs[i][s]
            d, m_new, v_new = _adam_update(g, w_refs[i][...], m_refs[i][...], v_refs[i][...])
            outs[i][...] = g
            outs[n + i][...] = d
            outs[2 * n + i][...] = m_new
            outs[3 * n + i][...] = v_new
        tot = lp_ref[0]
        for s in range(1, N_DEV):
            tot = tot + lp_ref[s]
        outs[4 * n][...] = tot

    shapes = [jax.ShapeDtypeStruct(w.shape, F32) for w in ws]
    res = pl.pallas_call(
        body, name="adamw_replicated",
        out_shape=shapes * 4 + [jax.ShapeDtypeStruct(loss_parts.shape[1:], F32)],
        compiler_params=_params(),
    )(*parts, loss_parts, *ws, *ms, *vs)
    return res[:n], res[n:2 * n], res[2 * n:3 * n], res[3 * n:4 * n], res[4 * n]


def _gather2(name, arrays):
    n = len(arrays)

    def body(*refs):
        first, passed, last = _gather2_copies(refs[:n], refs[n:2 * n], *refs[2 * n:])
        for cp in first:
            cp.start()
        for arrival, cp in passed:
            arrival.wait_recv()
            cp.start()
        for wait in last:
            wait()

    return pl.pallas_call(
        body, name=name,
        in_specs=[ANY_SPEC] * n, out_specs=[ANY_SPEC] * n, out_shape=_exchange_shapes(arrays, (False,) * n),
        scratch_shapes=_exchange_sems(n),
    )(*arrays)


def _dw_in_exchange(hn, dz, small):
    S = hn.shape[0]
    nt = S // TK
    ns = len(small)
    kd, nd = hn.shape[1], dz.shape[2]
    me_arr = (4 * lax.axis_index("x") + 2 * lax.axis_index("y") + lax.axis_index("c")).astype(jnp.int32).reshape(1)

    def body(me_ref, x_ref, dy_ref, *rest):
        recv_ref = rest[ns]
        acc, stage, send_sems, recv_sems, own_sem = rest[2 * ns + 1:2 * ns + 6]
        j, t = pl.program_id(0), pl.program_id(1)
        x, y, c = lax.axis_index("x"), lax.axis_index("y"), lax.axis_index("c")
        me = 4 * x + 2 * y + c
        small_copies = _exchange_copies(rest[:ns], rest[ns + 1:2 * ns + 1], (False,) * ns, *rest[2 * ns + 6:])

        @pl.when((j == 0) & (t == 0))
        def _():
            for cp in small_copies:
                cp.start()

        @pl.when(t == 0)
        def _():
            acc[...] = jnp.zeros(acc.shape, F32)

        acc[...] += lax.dot_general(x_ref[...], dy_ref[...], TN, preferred_element_type=F32)

        def to_owner(k, owner):
            return pltpu.make_async_remote_copy(
                src_ref=stage.at[owner], dst_ref=recv_ref.at[me], send_sem=send_sems.at[k], recv_sem=recv_sems.at[k],
                device_id=(owner // 4, (owner // 2) % 2, owner % 2), device_id_type=MESH)

        own = pltpu.make_async_copy(stage.at[me], recv_ref.at[me], own_sem)

        @pl.when(t == nt - 1)
        def _():
            owner = (me + 1 + j) % N_DEV
            stage[owner] = acc[...].astype(BF16)

            @pl.when(j < N_DEV - 1)
            def _():
                to_owner(j, owner).start()

            @pl.when(j == N_DEV - 1)
            def _():
                own.start()
                own.wait()
                for k in range(N_DEV - 1):
                    to_owner(k, me).wait_send()
                    to_owner(k, me).wait_recv()
                for cp in small_copies:
                    cp.wait()

    slab = lambda j, me_ref: (me_ref[0] + 1 + j) % N_DEV
    grid_spec = pltpu.PrefetchScalarGridSpec(
        num_scalar_prefetch=1, grid=(N_DEV, nt),
        in_specs=[pl.BlockSpec((TK, kd), lambda j, t, me_ref: (t, 0)),
                  pl.BlockSpec((None, TK, nd), lambda j, t, me_ref: (slab(j, me_ref), t, 0))] + [ANY_SPEC] * ns,
        out_specs=[ANY_SPEC] * (ns + 1),
        scratch_shapes=[pltpu.VMEM((kd, nd), F32), pltpu.VMEM((N_DEV, kd, nd), BF16),
                        pltpu.SemaphoreType.DMA((N_DEV - 1,)), pltpu.SemaphoreType.DMA((N_DEV - 1,)),
                        pltpu.SemaphoreType.DMA] + _exchange_sems(ns))
    res = pl.pallas_call(
        body, name="dw_in_exchange", grid_spec=grid_spec,
        out_shape=[jax.ShapeDtypeStruct((N_DEV, kd, nd), BF16)] + _exchange_shapes(small, (False,) * ns),
        compiler_params=_params("arbitrary", "arbitrary"),
    )(me_arr, hn, dz, *small)
    return res[0], res[1:]


def kernel(x, p, ln_mix, w_in, pool_w, pool_scale, w_out, ln_ffn, w_up, conv_w, conv_b, w_down, ln_ple, w_ple_gate, w_ple, ln_final, loss_target, m_ln_mix, m_w_in, m_pool_w, m_pool_scale, m_w_out, m_ln_ffn, m_w_up, m_conv_w, m_conv_b, m_w_down, m_ln_ple, m_w_ple_gate, m_w_ple, m_ln_final, v_ln_mix, v_w_in, v_pool_w, v_pool_scale, v_w_out, v_ln_ffn, v_w_up, v_conv_w, v_conv_b, v_w_down, v_ln_ple, v_w_ple_gate, v_w_ple, v_ln_final):
    xs, ps, tgt, pool_w0 = x[0], p[0, 0], loss_target[0], pool_w[0]
    slopes = jnp.exp2(-8.0 * (jnp.arange(N_HEADS, dtype=F32) + 1.0) / N_HEADS)
    conv_b_s = conv_b.reshape(N_DEV, 1, FF_SHARD)

    (w_in_g,) = _gather2("gather_w_in", [w_in[0].astype(BF16)])
    (q, k, v, u, hn1), (w_out_g,) = _qkvu(xs, ln_mix, w_in_g, [w_out[0].astype(BF16)])
    att, lse, (w_up_g,) = _attn_fwd(slopes, q, k, v, [w_up[0].astype(BF16)])
    w_out_f = w_out_g.reshape(D_MODEL, D_MODEL)
    h1, mix, dlt, (conv_w_g,) = _mix_out(xs, att, u, pool_w0, pool_scale, w_out_f, [conv_w[0]])
    a_fwd, hn2, up, upc, (w_down_g, w_pg_g, w_ple_g) = _ffn_up(
        h1, ln_ffn, w_up_g, conv_w_g, conv_b_s,
        [w_down[0].astype(BF16), w_ple_gate[0].astype(BF16), w_ple[0].astype(BF16)])
    w_down_f = w_down_g.reshape(4, FF_SHARD, D_MODEL)
    h2 = _ffn_down(h1, a_fwd, w_down_f)
    w_pg_f = w_pg_g.reshape(D_MODEL, D_MODEL)
    w_ple_f = jnp.transpose(w_ple_g, (1, 0, 2)).reshape(PLE_DIM, D_MODEL)
    loss_blk, dh2, dh2b, d_w_pg, d_w_ple, d_ln_ple, d_ln_final = _head(
        h2, ps, ln_ple, w_pg_f, w_ple_f, ln_final.reshape(1, D_MODEL), tgt)

    d_w_pg = d_w_pg.reshape(N_DEV, D_MODEL // N_DEV, D_MODEL)
    d_w_ple = jnp.transpose(d_w_ple.reshape(PLE_DIM, N_DEV, LANES), (1, 0, 2))
    d_w_down, dup, d_conv_w, d_conv_b, (r_w_pg, r_w_ple) = _ffn_bwd_a(dh2b, up, upc, w_down_f, [d_w_pg, d_w_ple])
    d_w_down = d_w_down.reshape(N_DEV, D_FF // N_DEV, D_MODEL)
    dpre, dh1, dh1b, d_ln_ffn, (r_conv_w, r_w_down) = _ffn_bwd_b(
        dup, conv_w_g, w_up_g, h1, ln_ffn, dh2, [d_conv_w, d_w_down])
    datt, du, d_pool_w, d_pool_scale, d_w_out = _mix_bwd(dh1b, w_out_f, dlt, mix, pool_w0, pool_scale)
    d_w_out = d_w_out.reshape(N_DEV, D_MODEL // N_DEV, D_MODEL)
    d_w_up = _wgrad("dw_up", dpre, hn2, "lead", "full", N_DEV, FF_SHARD, D_MODEL)
    rep_late = [d_pool_w, d_pool_scale, d_ln_ffn, d_conv_b.reshape(1, 2 * D_FF), d_ln_ple, d_ln_final, loss_blk]
    dq, dk, dv, received = _attn_bwd(slopes, q, k, v, att, lse, datt, [d_w_out, d_w_up] + rep_late,
                                     (True, True) + (False,) * len(rep_late))
    r_w_out, r_w_up, r_rep = received[0], received[1], list(received[2:])
    dz, grad_x, d_ln_mix = _in_bwd(dq, dk, dv, du, w_in_g, xs, ln_mix, dh1)

    rep_names = ("ln_mix", "pool_w", "pool_scale", "ln_ffn", "conv_b", "ln_ple", "ln_final")
    rep_w = [ln_mix, pool_w0, pool_scale, ln_ffn, conv_b, ln_ple, ln_final.reshape(1, D_MODEL)]
    rep_m = [m_ln_mix, m_pool_w[0], m_pool_scale, m_ln_ffn, m_conv_b, m_ln_ple, m_ln_final.reshape(1, D_MODEL)]
    rep_v = [v_ln_mix, v_pool_w[0], v_pool_scale, v_ln_ffn, v_conv_b, v_ln_ple, v_ln_final.reshape(1, D_MODEL)]
    r_w_in, (r_ln_mix,) = _dw_in_exchange(hn1, dz, [d_ln_mix])
    small = _adamw_small([r_ln_mix] + r_rep[:-1], r_rep[-1], rep_w, rep_m, rep_v)
    loss = small[4][0, 0]

    sharded = {}
    sharded["w_in"] = _adamw("adamw_w_in", r_w_in, w_in[0], m_w_in[0], v_w_in[0])
    sharded["w_out"] = _adamw("adamw_w_out", r_w_out, w_out[0], m_w_out[0], v_w_out[0])
    sharded["w_up"] = [t.T for t in _adamw("adamw_w_up", r_w_up, w_up[0].T, m_w_up[0].T, v_w_up[0].T)]
    sharded["conv_w"] = _adamw("adamw_conv_w", r_conv_w, conv_w[0], m_conv_w[0], v_conv_w[0])
    sharded["w_down"] = _adamw("adamw_w_down", r_w_down, w_down[0], m_w_down[0], v_w_down[0])
    sharded["w_ple_gate"] = _adamw("adamw_w_ple_gate", r_w_pg, w_ple_gate[0], m_w_ple_gate[0], v_w_ple_gate[0])
    sharded["w_ple"] = _adamw("adamw_w_ple", r_w_ple, w_ple[0], m_w_ple[0], v_w_ple[0])

    shapes = dict(w_in=w_in, w_out=w_out, w_up=w_up, conv_w=conv_w, w_down=w_down, w_ple_gate=w_ple_gate, w_ple=w_ple,
                  ln_mix=ln_mix, pool_w=pool_w, pool_scale=pool_scale, ln_ffn=ln_ffn, conv_b=conv_b, ln_ple=ln_ple,
                  ln_final=ln_final)

    def leaf(kind, n):
        src = sharded[n][kind] if n in sharded else small[kind][rep_names.index(n)]
        return src.reshape(shapes[n].shape)

    order = ("ln_mix", "w_in", "pool_w", "pool_scale", "w_out", "ln_ffn", "w_up", "conv_w", "conv_b", "w_down", "ln_ple",
             "w_ple_gate", "w_ple", "ln_final")
    outs = [loss, grad_x[None]]
    for kind in range(4):
        outs += [leaf(kind, n) for n in order]
    return tuple(outs)
```

```python
import jax
import jax.numpy as jnp
from jax import lax
from jax.experimental import pallas as pl
from jax.experimental.pallas import tpu as pltpu

F32 = jnp.float32
BF16 = jnp.bfloat16

N_DEV = 8
D_MODEL = 1024
ATT_WIDTH = 512
POOL_WIDTH = 512
N_HEADS = 8
HEAD_DIM = 64
SPAN = 128
DILATIONS = (1, 4, 16)
POOL_WINDOWS = (2, 4, 8, 16)
POOL_GROUP = 128
D_FF = 2816
FF_SHARD = 2 * D_FF // N_DEV
PLE_DIM = 256
EPS = 1e-6
NEG = -1e30

ADAM_LR = 0.001
ADAM_B1 = 0.9
ADAM_B2 = 0.999
ADAM_EPS = 1e-08
ADAM_WD = 0.01
ADAM_STEP = 10

LANES = 128
HALO = 16
TM = 512
TM_FF = 256
TK = 4096
ATTN_GROUP_FWD = 16
ATTN_GROUP_BWD = 8
VMEM_LIMIT = 56 * 1024 * 1024

MESH = pl.DeviceIdType.MESH
NT = (((1,), (1,)), ((), ()))
TN = (((0,), (0,)), ((), ()))


def _params(*sem):
    return pltpu.CompilerParams(dimension_semantics=sem or None, vmem_limit_bytes=VMEM_LIMIT)


def _const(shape):
    n = len(shape)
    return pl.BlockSpec(shape, lambda *_: (0,) * n, pipeline_mode=pl.Buffered(1))


def _rms(h):
    r = lax.rsqrt(jnp.mean(h * h, axis=-1, keepdims=True) + EPS)
    return r, h * r


def _rms_bwd(r, n, g, dhn):
    dn = dhn * g
    return r * (dn - n * jnp.mean(dn * n, axis=-1, keepdims=True))


def _colsum(a):
    return jnp.sum(a, axis=0, keepdims=True)


def _gather2_copies(ins, outs, send_sems, recv_sems, local_sems):
    n = len(ins)
    x, y, c = lax.axis_index("x"), lax.axis_index("y"), lax.axis_index("c")
    slot = lambda px, py, pc: 4 * px + 2 * py + pc
    chips = [(x, 1 - y), (1 - x, y), (1 - x, 1 - y)]
    first, passed, last = [], [], []

    def remote(a, r, src, dst_slot, to):
        return pltpu.make_async_remote_copy(
            src_ref=src, dst_ref=outs[a].at[dst_slot],
            send_sem=send_sems.at[a * (N_DEV - 1) + r], recv_sem=recv_sems.at[a * (N_DEV - 1) + r],
            device_id=to, device_id_type=MESH)

    for a in range(n):
        mine = pltpu.make_async_copy(ins[a], outs[a].at[slot(x, y, c)], local_sems.at[a])
        to_sibling = remote(a, 0, ins[a], slot(x, y, c), (x, y, 1 - c))
        first += [mine, to_sibling]
        last += [mine.wait, to_sibling.wait_send, to_sibling.wait_recv]
        for r, (px, py) in enumerate(chips, start=1):
            to_chip = remote(a, r, ins[a], slot(x, y, c), (px, py, c))
            onward = remote(a, 3 + r, outs[a].at[slot(px, py, c)], slot(px, py, c), (x, y, 1 - c))
            first.append(to_chip)
            passed.append((to_chip, onward))
            last += [to_chip.wait_send, onward.wait_send, onward.wait_recv]
    return first, passed, last


def _gather2_begin(plan, step, pass_step):
    first, passed, _ = plan

    @pl.when(step == 0)
    def _():
        for cp in first:
            cp.start()

    @pl.when(step == pass_step)
    def _():
        for arrival, cp in passed:
            arrival.wait_recv()
            cp.start()


def _gather2_end(plan, step, nsteps):
    @pl.when(step == nsteps - 1)
    def _():
        for wait in plan[2]:
            wait()


ANY_SPEC = pl.BlockSpec(memory_space=pl.ANY)


def _exchange_shapes(arrays, scatter):
    out = []
    for a, s in zip(arrays, scatter):
        slab = a.shape[1:] if s else a.shape
        out.append(jax.ShapeDtypeStruct((N_DEV,) + tuple(slab), a.dtype))
    return out


def _exchange_sems(n):
    return [pltpu.SemaphoreType.DMA((n * (N_DEV - 1),)), pltpu.SemaphoreType.DMA((n * (N_DEV - 1),)),
            pltpu.SemaphoreType.DMA((n,))]


def _exchange_copies(ins, outs, scatter, send_sems, recv_sems, local_sems):
    n = len(ins)
    x, y, c = lax.axis_index("x"), lax.axis_index("y"), lax.axis_index("c")
    me = 4 * x + 2 * y + c
    copies = []
    for a in range(n):
        src = ins[a].at[me] if scatter[a] else ins[a]
        copies.append(pltpu.make_async_copy(src, outs[a].at[me], local_sems.at[a]))
    for k in range(1, N_DEV):
        px = 1 - x if k & 4 else x
        py = 1 - y if k & 2 else y
        pc = 1 - c if k & 1 else c
        pid = 4 * px + 2 * py + pc
        for a in range(n):
            src = ins[a].at[pid] if scatter[a] else ins[a]
            copies.append(pltpu.make_async_remote_copy(
                src_ref=src, dst_ref=outs[a].at[me],
                send_sem=send_sems.at[a * (N_DEV - 1) + k - 1], recv_sem=recv_sems.at[a * (N_DEV - 1) + k - 1],
                device_id=(px, py, pc), device_id_type=MESH))
    return copies


def _qkvu(x, g1, w_in, shards):
    S = x.shape[0]
    ns = len(shards)
    nsteps = S // TM

    def body(x_ref, g_ref, w_ref, *rest):
        q_ref, k_ref, v_ref, u_ref, hn_ref = rest[ns:ns + 5]
        plan = _gather2_copies(rest[:ns], rest[ns + 5:2 * ns + 5], *rest[2 * ns + 5:])
        _gather2_begin(plan, pl.program_id(0), nsteps - 2)
        r, n = _rms(x_ref[...])
        hn = (n * g_ref[...]).astype(BF16)
        hn_ref[...] = hn
        outs = (q_ref, k_ref, v_ref, u_ref)
        for j in range(N_DEV):
            z = jnp.dot(hn, w_ref[j], preferred_element_type=F32)
            if j < 2:
                z = z * (HEAD_DIM ** -0.5)
            outs[j // 2][:, (j % 2) * 256:(j % 2 + 1) * 256] = z
        _gather2_end(plan, pl.program_id(0), nsteps)

    tok = lambda w: pl.BlockSpec((TM, w), lambda i: (i, 0))
    res = pl.pallas_call(
        body, name="qkvu", grid=(nsteps,),
        in_specs=[tok(D_MODEL), _const((1, D_MODEL)), _const(w_in.shape)] + [ANY_SPEC] * ns,
        out_specs=[tok(512)] * 4 + [tok(D_MODEL)] + [ANY_SPEC] * ns,
        out_shape=[jax.ShapeDtypeStruct((S, 512), F32)] * 4 + [jax.ShapeDtypeStruct((S, D_MODEL), BF16)]
        + _exchange_shapes(shards, (False,) * ns),
        scratch_shapes=_exchange_sems(ns),
        compiler_params=_params("arbitrary"),
    )(x, g1, w_in, *shards)
    return res[:5], res[5:]


def _attn_fill_bias(bias_s, slope_ref, hp, d):
    qi = lax.broadcasted_iota(jnp.int32, (SPAN, 2 * SPAN), 0)
    kj = lax.broadcasted_iota(jnp.int32, (SPAN, 2 * SPAN), 1)
    for t, diff in enumerate((qi + SPAN - kj, qi - kj)):
        valid = (diff >= 0) & (diff <= SPAN)
        dist = diff.astype(F32) * float(d)
        for h in range(2):
            bias_s[t, h * SPAN:(h + 1) * SPAN, :] = jnp.where(valid, -slope_ref[2 * hp + h] * dist, NEG)


def _stack_heads(x, is0):
    return jnp.concatenate([jnp.where(is0, x, 0.0), jnp.where(is0, 0.0, x)], axis=0)


def _unstack_heads(y, is0):
    return jnp.where(is0, y[0:SPAN], y[SPAN:2 * SPAN])


def _attn_block(i, g, d, nb, group):
    gr = min(d, group)
    gn = group // gr
    per = d // gr
    r = (i & (per - 1)) * gr + g % gr
    n = (i >> (per.bit_length() - 1)) + (g // gr) * (nb // gn)
    k0 = jnp.maximum(n - 1, 0)

    def ds(block, nrows):
        start = block * (SPAN * d) + r
        return pl.ds(start, nrows, stride=d) if d > 1 else pl.ds(start, nrows)

    return ds(n, SPAN), ds(k0, 2 * SPAN), jnp.where(n == 0, 1, 0)


def _attn_groups(S, d, group, writes_key_rows=False):
    nb = S // d // SPAN
    gn = group // min(d, group)
    assert nb >= 2 and nb % gn == 0 and (gn == 1 or nb // gn >= (3 if writes_key_rows else 2))
    return nb, d * nb // group


def _attn_fwd(slopes, q, k, v, shards):
    S = q.shape[0]
    ns = len(shards)
    steps = ATT_WIDTH // LANES

    def body(slope_ref, q_ref, k_ref, v_ref, *rest):
        o_ref, lse_ref = rest[ns:ns + 2]
        m_s, l_s, bias_s = rest[2 * ns + 2:2 * ns + 5]
        hp = pl.program_id(0)
        plan = _gather2_copies(rest[:ns], rest[ns + 2:2 * ns + 2], *rest[2 * ns + 5:])
        _gather2_begin(plan, hp, steps - 1)

        is0 = lax.broadcasted_iota(jnp.int32, (SPAN, LANES), 1) < HEAD_DIM
        for pi, d in enumerate(DILATIONS):
            nb, ngroups = _attn_groups(S, d, ATTN_GROUP_FWD)
            _attn_fill_bias(bias_s, slope_ref, hp, d)

            def group(i, carry, d=d, pi=pi, nb=nb):
                blocks = [_attn_block(i, g, d, nb, ATTN_GROUP_FWD) for g in range(ATTN_GROUP_FWD)]
                loaded = [(q_ref[rows, :], k_ref[krows, :].astype(BF16), v_ref[krows, :].astype(BF16))
                          for rows, krows, _ in blocks]
                new = []
                for (rows, krows, tab), (qb, kb, vb) in zip(blocks, loaded):
                    qs = _stack_heads(qb, is0).astype(BF16)
                    s = lax.dot_general(qs, kb, NT, preferred_element_type=F32) + bias_s[tab]
                    m = jnp.max(s, axis=-1, keepdims=True)
                    e = jnp.exp(s - m)
                    l = jnp.sum(e, axis=-1, keepdims=True)
                    pv = jnp.dot(e.astype(BF16), vb, preferred_element_type=F32)
                    new.append([_unstack_heads(jnp.broadcast_to(m, pv.shape), is0),
                                _unstack_heads(jnp.broadcast_to(l, pv.shape), is0), _unstack_heads(pv, is0)])
                if pi > 0:
                    old = [(m_s[rows, :], l_s[rows, :], o_ref[rows, :]) for rows, _, _ in blocks]
                    for st, (m_o, l_o, o_o) in zip(new, old):
                        m_n = jnp.maximum(m_o, st[0])
                        a_o = jnp.exp(m_o - m_n)
                        a_b = jnp.exp(st[0] - m_n)
                        st[:] = [m_n, a_o * l_o + a_b * st[1], a_o * o_o + a_b * st[2]]
                for (rows, _, _), (m_b, l_b, acc) in zip(blocks, new):
                    if pi == len(DILATIONS) - 1:
                        o_ref[rows, :] = acc / l_b
                        lse_ref[rows, :] = m_b + jnp.log(l_b)
                    else:
                        o_ref[rows, :] = acc
                        m_s[rows, :] = m_b
                        l_s[rows, :] = l_b
                return carry

            lax.fori_loop(0, ngroups, group, 0)

        _gather2_end(plan, hp, steps)

    col = pl.BlockSpec((S, LANES), lambda i: (0, i))
    res = pl.pallas_call(
        body, name="attn_fwd", grid=(steps,),
        in_specs=[pl.BlockSpec(memory_space=pltpu.SMEM), col, col, col] + [ANY_SPEC] * ns,
        out_specs=[col, col] + [ANY_SPEC] * ns,
        out_shape=[jax.ShapeDtypeStruct((S, ATT_WIDTH), F32)] * 2 + _exchange_shapes(shards, (False,) * ns),
        scratch_shapes=[pltpu.VMEM((S, LANES), F32), pltpu.VMEM((S, LANES), F32),
                        pltpu.VMEM((2, 2 * SPAN, 2 * SPAN), F32)] + _exchange_sems(ns),
        compiler_params=_params("arbitrary"),
    )(slopes, q, k, v, *shards)
    return res[0], res[1], res[2:]


def _pool_count(i, w):
    t = i * TM + lax.broadcasted_iota(jnp.int32, (TM, 1), 0)
    return jnp.minimum(t + 1, w).astype(F32)


def _mix_out(x, att, u, pool_w, pool_scale, w_out, shards):
    S = x.shape[0]
    ns = len(shards)
    nsteps = S // TM

    def body(x_ref, att_ref, u_ref, pw_ref, ps_ref, w_ref, *rest):
        h1_ref, mix_ref, dlt_ref = rest[ns:ns + 3]
        ubuf = rest[2 * ns + 3]
        i = pl.program_id(0)
        plan = _gather2_copies(rest[:ns], rest[ns + 3:2 * ns + 3], *rest[2 * ns + 4:])
        _gather2_begin(plan, i, nsteps - 1)

        @pl.when(i == 0)
        def _():
            ubuf[0:HALO, :] = jnp.zeros((HALO, POOL_WIDTH), F32)

        ubuf[HALO:HALO + TM, :] = u_ref[...]
        mix_ref[:, 0:ATT_WIDTH] = att_ref[...].astype(BF16)
        for g, w in enumerate(POOL_WINDOWS):
            cols = slice(g * POOL_GROUP, (g + 1) * POOL_GROUP)
            ug = ubuf[HALO:HALO + TM, cols]
            acc = ug
            for j in range(1, w):
                acc = acc + ubuf[HALO - j:HALO - j + TM, cols]
            dlt = (acc / _pool_count(i, w) - ug).astype(BF16)
            dlt_ref[:, cols] = dlt
            yg = jnp.dot(dlt, pw_ref[g].astype(BF16), preferred_element_type=F32) * ps_ref[:, cols]
            mix_ref[:, ATT_WIDTH + g * POOL_GROUP:ATT_WIDTH + (g + 1) * POOL_GROUP] = yg.astype(BF16)
        ubuf[0:HALO, :] = ubuf[TM:TM + HALO, :]
        h1_ref[...] = x_ref[...] + jnp.dot(mix_ref[...], w_ref[...], preferred_element_type=F32)
        _gather2_end(plan, i, nsteps)

    tok = lambda w: pl.BlockSpec((TM, w), lambda i: (i, 0))
    res = pl.pallas_call(
        body, name="mix_out", grid=(nsteps,),
        in_specs=[tok(D_MODEL), tok(ATT_WIDTH), tok(POOL_WIDTH), _const(pool_w.shape), _const((1, POOL_WIDTH)),
                  _const(w_out.shape)] + [ANY_SPEC] * ns,
        out_specs=[tok(D_MODEL), tok(D_MODEL), tok(POOL_WIDTH)] + [ANY_SPEC] * ns,
        out_shape=[jax.ShapeDtypeStruct((S, D_MODEL), F32), jax.ShapeDtypeStruct((S, D_MODEL), BF16),
                   jax.ShapeDtypeStruct((S, POOL_WIDTH), BF16)] + _exchange_shapes(shards, (False,) * ns),
        scratch_shapes=[pltpu.VMEM((TM + HALO, POOL_WIDTH), F32)] + _exchange_sems(ns),
        compiler_params=_params("arbitrary"),
    )(x, att, u, pool_w, pool_scale, w_out, *shards)
    return res[0], res[1], res[2], res[3:]


def _conv_fwd(stage, upre, prev, cw, cb):
    T = upre.shape[0]
    stage[0:HALO, :] = prev
    stage[HALO:HALO + T, :] = upre
    return cb + cw[0:1, :] * stage[HALO - 2:HALO - 2 + T, :] + cw[1:2, :] * stage[HALO - 1:HALO - 1 + T, :] + cw[2:3, :] * upre


def _ffn_down(h1, a, w_down):
    S = h1.shape[0]

    def body(h1_ref, a_ref, wd_ref, h2_ref):
        acc = h1_ref[...]
        for j in range(4):
            acc = acc + jnp.dot(a_ref[j], wd_ref[j], preferred_element_type=F32)
        h2_ref[...] = acc

    tok = pl.BlockSpec((TM, D_MODEL), lambda i: (i, 0))
    return pl.pallas_call(
        body, name="ffn_down", grid=(S // TM,),
        in_specs=[tok, pl.BlockSpec((4, TM, FF_SHARD), lambda i: (0, i, 0)), _const(w_down.shape)],
        out_specs=tok, out_shape=jax.ShapeDtypeStruct((S, D_MODEL), F32),
        compiler_params=_params("arbitrary"),
    )(h1, a, w_down)


def _ffn_up(h1, g2, w_up, conv_w, conv_b, shards):
    S = h1.shape[0]
    T = TM_FF
    ns = len(shards)
    nsteps = S // T

    half = D_MODEL // 2

    def body(h1_ref, g_ref, wlo_ref, whi_ref, cw_ref, cb_ref, *rest):
        a_ref, hn_ref, up_ref, upc_ref = rest[ns:ns + 4]
        carry, stage = rest[2 * ns + 4:2 * ns + 6]
        i = pl.program_id(0)
        plan = _gather2_copies(rest[:ns], rest[ns + 4:2 * ns + 4], *rest[2 * ns + 6:])
        _gather2_begin(plan, i, nsteps // 2)

        @pl.when(i == 0)
        def _():
            carry[...] = jnp.zeros(carry.shape, F32)

        h1t = h1_ref[...]
        r, n = _rms(h1t)
        hn = (n * g_ref[...]).astype(BF16)
        hn_ref[...] = hn
        for j in range(4):
            conv = []
            for jj in (j, j + 4):
                upre = (jnp.dot(hn[:, :half], wlo_ref[jj], preferred_element_type=F32)
                        + jnp.dot(hn[:, half:], whi_ref[jj], preferred_element_type=F32))
                up_ref[jj] = upre.astype(BF16)
                conv.append(_conv_fwd(stage, upre, carry[jj], cw_ref[jj], cb_ref[jj]))
                upc_ref[jj] = conv[-1].astype(BF16)
                carry[jj] = stage[T:T + HALO, :]
            gate, val = conv
            a_ref[j] = (gate * jax.nn.sigmoid(gate) * val).astype(BF16)
        _gather2_end(plan, i, nsteps)

    tok = lambda w: pl.BlockSpec((T, w), lambda i: (i, 0))
    shard = lambda n: pl.BlockSpec((n, T, FF_SHARD), lambda i: (0, i, 0))
    res = pl.pallas_call(
        body, name="ffn_up", grid=(nsteps,),
        in_specs=[tok(D_MODEL), _const((1, D_MODEL)), _const(w_up[0].shape), _const(w_up[1].shape),
                  _const(conv_w.shape), _const(conv_b.shape)] + [ANY_SPEC] * ns,
        out_specs=[shard(4), tok(D_MODEL), shard(N_DEV), shard(N_DEV)] + [ANY_SPEC] * ns,
        out_shape=[jax.ShapeDtypeStruct((4, S, FF_SHARD), BF16), jax.ShapeDtypeStruct((S, D_MODEL), BF16)]
        + [jax.ShapeDtypeStruct((N_DEV, S, FF_SHARD), BF16)] * 2 + _exchange_shapes(shards, (False,) * ns),
        scratch_shapes=[pltpu.VMEM((N_DEV, HALO, FF_SHARD), F32), pltpu.VMEM((T + HALO, FF_SHARD), F32)]
        + _exchange_sems(ns),
        compiler_params=_params("arbitrary"),
    )(h1, g2, *w_up, conv_w, conv_b, *shards)
    return res[0], res[1], res[2], res[3], res[4:]


def _head(h2, p, g3, w_pg, w_ple, g4, target):
    S = h2.shape[0]
    nt = S // TM

    def body(h2_ref, p_ref, g3_ref, wpg_ref, wple_ref, g4_ref, t_ref,
             loss_ref, dh2_ref, dh2b_ref, dwpg_ref, dwple_ref, dg3_ref, dg4_ref, lacc, pg_acc, ple_acc):
        i = pl.program_id(0)

        @pl.when(i == 0)
        def _():
            lacc[...] = jnp.zeros(lacc.shape, F32)
            pg_acc[...] = jnp.zeros(pg_acc.shape, F32)
            ple_acc[...] = jnp.zeros(ple_acc.shape, F32)
            dg3_ref[...] = jnp.zeros(dg3_ref.shape, F32)
            dg4_ref[...] = jnp.zeros(dg4_ref.shape, F32)

        h2t = h2_ref[...]
        g3, g4 = g3_ref[...], g4_ref[...]
        r3, n3 = _rms(h2t)
        hn3 = (n3 * g3).astype(BF16)
        pb = p_ref[...].astype(BF16)
        gs = jax.nn.sigmoid(jnp.dot(hn3, wpg_ref[...], preferred_element_type=F32))
        pe = jnp.dot(pb, wple_ref[...], preferred_element_type=F32)
        h3 = h2t + gs * pe
        r4, n4 = _rms(h3)
        err = n4 * g4 - t_ref[...]
        lacc[...] += _colsum(err * err)
        dy = err * (1.0 / D_MODEL)
        dg4_ref[...] += _colsum(dy * n4)
        dh3 = _rms_bwd(r4, n4, g4, dy)
        dpe = (dh3 * gs).astype(BF16)
        dgl = (dh3 * pe * gs * (1.0 - gs)).astype(BF16)
        ple_acc[...] += lax.dot_general(pb, dpe, TN, preferred_element_type=F32)
        pg_acc[...] += lax.dot_general(hn3, dgl, TN, preferred_element_type=F32)
        dhn3 = lax.dot_general(dgl, wpg_ref[...], NT, preferred_element_type=F32)
        dg3_ref[...] += _colsum(dhn3 * n3)
        dh2 = dh3 + _rms_bwd(r3, n3, g3, dhn3)
        dh2_ref[...] = dh2
        dh2b_ref[...] = dh2.astype(BF16)

        @pl.when(i == nt - 1)
        def _():
            tot = 0.5 / D_MODEL * jnp.sum(lacc[...], axis=-1, keepdims=True)
            loss_ref[...] = jnp.broadcast_to(tot, loss_ref.shape)
            dwpg_ref[...] = pg_acc[...].astype(BF16)
            dwple_ref[...] = ple_acc[...].astype(BF16)

    tok = lambda w: pl.BlockSpec((TM, w), lambda i: (i, 0))
    row = pl.BlockSpec((1, D_MODEL), lambda i: (0, 0))
    act = lambda dt: jax.ShapeDtypeStruct((S, D_MODEL), dt)
    whole = lambda r: pl.BlockSpec((r, D_MODEL), lambda i: (0, 0))
    return pl.pallas_call(
        body, name="head", grid=(nt,),
        in_specs=[tok(D_MODEL), tok(PLE_DIM), _const((1, D_MODEL)), _const(w_pg.shape), _const(w_ple.shape),
                  _const((1, D_MODEL)), tok(D_MODEL)],
        out_specs=[pl.BlockSpec((8, LANES), lambda i: (0, 0)), tok(D_MODEL), tok(D_MODEL), whole(D_MODEL),
                   whole(PLE_DIM), row, row],
        out_shape=[jax.ShapeDtypeStruct((8, LANES), F32), act(F32), act(BF16),
                   jax.ShapeDtypeStruct((D_MODEL, D_MODEL), BF16), jax.ShapeDtypeStruct((PLE_DIM, D_MODEL), BF16),
                   jax.ShapeDtypeStruct((1, D_MODEL), F32), jax.ShapeDtypeStruct((1, D_MODEL), F32)],
        scratch_shapes=[pltpu.VMEM((1, D_MODEL), F32), pltpu.VMEM((D_MODEL, D_MODEL), F32),
                        pltpu.VMEM((PLE_DIM, D_MODEL), F32)],
        compiler_params=_params("arbitrary"),
    )(h2, p, g3, w_pg, w_ple, g4, target)


def _wgrad(name, x, dy, x_kind, dy_kind, nj, k_dim, n_dim, tk=TK):
    S = x.shape[-2]
    nt = S // tk

    def spec(kind, width):
        if kind == "full":
            return pl.BlockSpec((tk, width), lambda j, t: (t, 0))
        return pl.BlockSpec((None, tk, width), lambda j, t: (j, t, 0))

    def body(x_ref, dy_ref, o_ref, acc):
        t = pl.program_id(1)

        @pl.when(t == 0)
        def _():
            acc[...] = jnp.zeros(acc.shape, F32)

        acc[...] += lax.dot_general(x_ref[...].astype(BF16), dy_ref[...], TN, preferred_element_type=F32)

        @pl.when(t == nt - 1)
        def _():
            o_ref[...] = acc[...].astype(BF16)

    return pl.pallas_call(
        body, name=name, grid=(nj, nt),
        in_specs=[spec(x_kind, k_dim), spec(dy_kind, n_dim)],
        out_specs=pl.BlockSpec((None, k_dim, n_dim), lambda j, t: (j, 0, 0)),
        out_shape=jax.ShapeDtypeStruct((nj, k_dim, n_dim), BF16),
        scratch_shapes=[pltpu.VMEM((k_dim, n_dim), F32)],
        compiler_params=_params("arbitrary", "arbitrary"),
    )(x, dy)


def _row_picker(T, off0, off1):
    r = lax.broadcasted_iota(jnp.int32, (2 * T, T + HALO), 0)
    c = lax.broadcasted_iota(jnp.int32, (2 * T, T + HALO), 1)
    want = jnp.where(r < T, r + off0, r - T + off1)
    return jnp.where(c == want, 1.0, 0.0).astype(BF16)


def _ffn_bwd_a(dh2b, up, upc, w_down, grads):
    S = dh2b.shape[0]
    T = TM_FF
    hb = T // HALO
    nsteps = S // T
    ng = len(grads)

    def body(dh_ref, up_ref, halo_ref, upc_ref, wd_ref, *rest):
        dwd_ref, dup_ref, dcw_ref, dcb_ref = rest[ng:ng + 4]
        stage, dwd_acc = rest[2 * ng + 4:2 * ng + 6]
        i = pl.program_id(0)
        copies = _exchange_copies(rest[:ng], rest[ng + 4:2 * ng + 4], (True,) * ng, *rest[2 * ng + 6:])

        @pl.when(i == 0)
        def _():
            dcw_ref[...] = jnp.zeros(dcw_ref.shape, F32)
            dcb_ref[...] = jnp.zeros(dcb_ref.shape, F32)
            dwd_acc[...] = jnp.zeros(dwd_acc.shape, F32)
            for cp in copies:
                cp.start()

        @pl.when(i == nsteps - 1)
        def _():
            for cp in copies:
                cp.wait()

        dh = dh_ref[...]
        pick = _row_picker(T, HALO - 2, HALO - 1)
        for j in range(4):
            da = lax.dot_general(dh, wd_ref[j], NT, preferred_element_type=F32)
            taps = []
            for jj in (j, j + 4):
                upre = up_ref[jj]
                stage[0:HALO, :] = jnp.where(i > 0, halo_ref[jj], jnp.zeros((HALO, FF_SHARD), BF16))
                stage[HALO:HALO + T, :] = upre
                prv = jnp.dot(pick, stage[...], preferred_element_type=F32)
                taps.append((prv[0:T], prv[T:2 * T], upre.astype(F32)))
            gate, val = upc_ref[j].astype(F32), upc_ref[j + 4].astype(F32)
            sg = jax.nn.sigmoid(gate)
            silu = gate * sg
            dwd_acc[j] += lax.dot_general((silu * val).astype(BF16), dh, TN, preferred_element_type=F32)
            dgate = (da * val) * (sg + silu * (1.0 - sg))
            dval = da * silu
            for jj, dup, tp in ((j, dgate, taps[0]), (j + 4, dval, taps[1])):
                dup_ref[jj] = dup.astype(BF16)
                dcb_ref[jj] += _colsum(dup)
                for kk in range(3):
                    dcw_ref[jj, kk:kk + 1, :] += _colsum(dup * tp[kk])

        @pl.when(i == nsteps - 1)
        def _():
            dwd_ref[...] = dwd_acc[...].astype(BF16)

    tok = lambda w: pl.BlockSpec((T, w), lambda i: (i, 0))
    shard = pl.BlockSpec((N_DEV, T, FF_SHARD), lambda i: (0, i, 0))
    res = pl.pallas_call(
        body, name="ffn_bwd_a", grid=(nsteps,),
        in_specs=[tok(D_MODEL), shard,
                  pl.BlockSpec((N_DEV, HALO, FF_SHARD), lambda i: (0, jnp.maximum(i * hb - 1, 0), 0)),
                  shard, _const(w_down.shape)] + [ANY_SPEC] * ng,
        out_specs=[_const(w_down.shape), shard,
                   pl.BlockSpec((N_DEV, 3, FF_SHARD), lambda i: (0, 0, 0)),
                   pl.BlockSpec((N_DEV, 1, FF_SHARD), lambda i: (0, 0, 0))] + [ANY_SPEC] * ng,
        out_shape=[jax.ShapeDtypeStruct(w_down.shape, BF16), jax.ShapeDtypeStruct((N_DEV, S, FF_SHARD), BF16),
                   jax.ShapeDtypeStruct((N_DEV, 3, FF_SHARD), F32), jax.ShapeDtypeStruct((N_DEV, 1, FF_SHARD), F32)]
        + _exchange_shapes(grads, (True,) * ng),
        scratch_shapes=[pltpu.VMEM((T + HALO, FF_SHARD), BF16), pltpu.VMEM(w_down.shape, F32)] + _exchange_sems(ng),
        compiler_params=_params("arbitrary"),
    )(dh2b, up, up, upc, w_down, *grads)
    return res[0], res[1], res[2], res[3], res[4:]


def _ffn_bwd_b(dup, conv_w, w_up, h1, g2, dh2, grads):
    S = h1.shape[0]
    T = TM_FF
    hb = T // HALO
    nt = S // T
    ng = len(grads)

    def body(dup_ref, halo_ref, cw_ref, wlo_ref, whi_ref, h1_ref, g_ref, dh2_ref, *rest):
        dpre_ref, dh1_ref, dh1b_ref, dg_ref = rest[ng:ng + 4]
        stage = rest[2 * ng + 4]
        i = pl.program_id(0)
        copies = _exchange_copies(rest[:ng], rest[ng + 4:2 * ng + 4], (True,) * ng, *rest[2 * ng + 5:])

        @pl.when(i == 0)
        def _():
            dg_ref[...] = jnp.zeros(dg_ref.shape, F32)
            for cp in copies:
                cp.start()

        dhn_lo = jnp.zeros((T, D_MODEL // 2), F32)
        dhn_hi = jnp.zeros((T, D_MODEL // 2), F32)
        for jj in range(N_DEV):
            dup = dup_ref[jj].astype(F32)
            stage[0:T, :] = dup
            stage[T:T + HALO, :] = jnp.where(i < nt - 1, halo_ref[jj].astype(F32), 0.0)
            cw = cw_ref[jj]
            dpre = (cw[2:3, :] * dup + cw[1:2, :] * stage[1:1 + T, :] + cw[0:1, :] * stage[2:2 + T, :]).astype(BF16)
            dpre_ref[jj] = dpre
            dhn_lo = dhn_lo + lax.dot_general(dpre, wlo_ref[jj], NT, preferred_element_type=F32)
            dhn_hi = dhn_hi + lax.dot_general(dpre, whi_ref[jj], NT, preferred_element_type=F32)
        dhn = jnp.concatenate([dhn_lo, dhn_hi], axis=1)
        g = g_ref[...]
        r, n = _rms(h1_ref[...])
        dg_ref[...] += _colsum(dhn * n)
        dh1 = dh2_ref[...] + _rms_bwd(r, n, g, dhn)
        dh1_ref[...] = dh1
        dh1b_ref[...] = dh1.astype(BF16)

        @pl.when(i == nt - 1)
        def _():
            for cp in copies:
                cp.wait()

    tok = lambda w: pl.BlockSpec((T, w), lambda i: (i, 0))
    shard = pl.BlockSpec((N_DEV, T, FF_SHARD), lambda i: (0, i, 0))
    res = pl.pallas_call(
        body, name="ffn_bwd_b", grid=(nt,),
        in_specs=[shard,
                  pl.BlockSpec((N_DEV, HALO, FF_SHARD), lambda i: (0, jnp.minimum((i + 1) * hb, S // HALO - 1), 0)),
                  _const(conv_w.shape), _const(w_up[0].shape), _const(w_up[1].shape), tok(D_MODEL),
                  _const((1, D_MODEL)), tok(D_MODEL)]
        + [ANY_SPEC] * ng,
        out_specs=[shard, tok(D_MODEL), tok(D_MODEL), pl.BlockSpec((1, D_MODEL), lambda i: (0, 0))] + [ANY_SPEC] * ng,
        out_shape=[jax.ShapeDtypeStruct((N_DEV, S, FF_SHARD), BF16), jax.ShapeDtypeStruct((S, D_MODEL), F32),
                   jax.ShapeDtypeStruct((S, D_MODEL), BF16), jax.ShapeDtypeStruct((1, D_MODEL), F32)]
        + _exchange_shapes(grads, (True,) * ng),
        scratch_shapes=[pltpu.VMEM((T + HALO, FF_SHARD), F32)] + _exchange_sems(ng),
        compiler_params=_params("arbitrary"),
    )(dup, dup, conv_w, *w_up, h1, g2, dh2, *grads)
    return res[0], res[1], res[2], res[3], res[4:]


def _mix_bwd(dh1b, w_out, dlt, mix, pool_w, pool_scale):
    S = dh1b.shape[0]
    nt = S // TM

    def body(dh_ref, w_ref, dlt_ref, mix_ref, pw_ref, ps_ref, datt_ref, du_ref, dpw_ref, dps_ref, dwo_ref,
             stage, carry, wo_acc):
        i = pl.program_id(0)
        tile = nt - 1 - i

        @pl.when(i == 0)
        def _():
            dpw_ref[...] = jnp.zeros(dpw_ref.shape, F32)
            dps_ref[...] = jnp.zeros(dps_ref.shape, F32)
            carry[...] = jnp.zeros(carry.shape, F32)
            wo_acc[...] = jnp.zeros(wo_acc.shape, F32)

        wo_acc[...] += lax.dot_general(mix_ref[...], dh_ref[...], TN, preferred_element_type=F32)

        @pl.when(i == nt - 1)
        def _():
            dwo_ref[...] = wo_acc[...].astype(BF16)

        dmix = lax.dot_general(dh_ref[...], w_ref[...], NT, preferred_element_type=F32)
        datt_ref[...] = dmix[:, 0:ATT_WIDTH]
        for g, w in enumerate(POOL_WINDOWS):
            cols = slice(g * POOL_GROUP, (g + 1) * POOL_GROUP)
            dpool = dmix[:, ATT_WIDTH + g * POOL_GROUP:ATT_WIDTH + (g + 1) * POOL_GROUP]
            dl = dlt_ref[:, cols]
            pw = pw_ref[g].astype(BF16)
            yg = jnp.dot(dl, pw, preferred_element_type=F32)
            dps_ref[:, cols] += _colsum(dpool * yg)
            dy = (dpool * ps_ref[:, cols]).astype(BF16)
            dpw_ref[g] += lax.dot_general(dl, dy, TN, preferred_element_type=F32)
            ddlt = lax.dot_general(dy, pw, NT, preferred_element_type=F32)
            cg = ddlt / _pool_count(tile, w)
            stage[0:TM, :] = cg
            stage[TM:TM + HALO, :] = carry[:, cols]
            acc = cg
            for j in range(1, w):
                acc = acc + stage[j:j + TM, :]
            du_ref[:, cols] = acc - ddlt
            carry[:, cols] = cg[0:HALO, :]

    tok = lambda w: pl.BlockSpec((TM, w), lambda i: (nt - 1 - i, 0))
    return pl.pallas_call(
        body, name="mix_bwd", grid=(nt,),
        in_specs=[tok(D_MODEL), _const(w_out.shape), tok(POOL_WIDTH), tok(D_MODEL), _const(pool_w.shape),
                  _const((1, POOL_WIDTH))],
        out_specs=[tok(ATT_WIDTH), tok(POOL_WIDTH), pl.BlockSpec(pool_w.shape, lambda i: (0, 0, 0)),
                   pl.BlockSpec((1, POOL_WIDTH), lambda i: (0, 0)), pl.BlockSpec(w_out.shape, lambda i: (0, 0))],
        out_shape=[jax.ShapeDtypeStruct((S, ATT_WIDTH), F32), jax.ShapeDtypeStruct((S, POOL_WIDTH), F32),
                   jax.ShapeDtypeStruct(pool_w.shape, F32), jax.ShapeDtypeStruct((1, POOL_WIDTH), F32),
                   jax.ShapeDtypeStruct(w_out.shape, BF16)],
        scratch_shapes=[pltpu.VMEM((TM + HALO, POOL_GROUP), F32), pltpu.VMEM((HALO, POOL_WIDTH), F32),
                        pltpu.VMEM(w_out.shape, F32)],
        compiler_params=_params("arbitrary"),
    )(dh1b, w_out, dlt, mix, pool_w, pool_scale)


def _attn_bwd(slopes, q, k, v, o, lse, do, grads, scatter):
    S = q.shape[0]
    CH = 512
    ng = len(grads)
    steps = ATT_WIDTH // LANES

    def body(slope_ref, q_ref, k_ref, v_ref, o_ref, lse_ref, do_ref, *rest):
        dq_ref, dk_ref, dv_ref = rest[ng:ng + 3]
        dl_s, bias_s = rest[2 * ng + 3:2 * ng + 5]
        hp = pl.program_id(0)
        copies = _exchange_copies(rest[:ng], rest[ng + 3:2 * ng + 3], scatter, *rest[2 * ng + 5:])

        @pl.when(hp == 0)
        def _():
            for cp in copies:
                cp.start()

        is0 = lax.broadcasted_iota(jnp.int32, (SPAN, LANES), 1) < HEAD_DIM
        is0c = lax.broadcasted_iota(jnp.int32, (CH, LANES), 1) < HEAD_DIM

        def prep(ci, carry):
            rows = pl.ds(pl.multiple_of(ci * CH, CH), CH)
            prod = do_ref[rows, :] * o_ref[rows, :]
            d0 = jnp.sum(jnp.where(is0c, prod, 0.0), axis=-1, keepdims=True)
            d1 = jnp.sum(jnp.where(is0c, 0.0, prod), axis=-1, keepdims=True)
            dl_s[rows, :] = jnp.where(is0c, d0, d1)
            zero = jnp.zeros((CH, LANES), F32)
            dq_ref[rows, :] = zero
            dk_ref[rows, :] = zero
            dv_ref[rows, :] = zero
            return carry

        lax.fori_loop(0, S // CH, prep, 0)

        for d in DILATIONS:
            nb, ngroups = _attn_groups(S, d, ATTN_GROUP_BWD, writes_key_rows=True)
            _attn_fill_bias(bias_s, slope_ref, hp, d)

            def group(i, carry, d=d, nb=nb):
                blocks = [_attn_block(i, g, d, nb, ATTN_GROUP_BWD) for g in range(ATTN_GROUP_BWD)]
                loaded = [(q_ref[rows, :], do_ref[rows, :], lse_ref[rows, :], dl_s[rows, :], k_ref[krows, :],
                           v_ref[krows, :].astype(BF16)) for rows, krows, _ in blocks]
                new = []
                for (rows, krows, tab), (qb, dob, lse_b, dl_b, kf, vb) in zip(blocks, loaded):
                    kb = kf.astype(BF16)
                    qs = _stack_heads(qb, is0).astype(BF16)
                    dos = _stack_heads(dob, is0).astype(BF16)
                    lse_s = jnp.concatenate([lse_b[:, 0:1], lse_b[:, HEAD_DIM:HEAD_DIM + 1]], axis=0)
                    dl_s2 = jnp.concatenate([dl_b[:, 0:1], dl_b[:, HEAD_DIM:HEAD_DIM + 1]], axis=0)
                    s = lax.dot_general(qs, kb, NT, preferred_element_type=F32) + bias_s[tab]
                    pr = jnp.exp(s - lse_s)
                    dp = lax.dot_general(dos, vb, NT, preferred_element_type=F32)
                    ds = (pr * (dp - dl_s2)).astype(BF16)
                    dv_c = lax.dot_general(pr.astype(BF16), dos, TN, preferred_element_type=F32)
                    dk_c = lax.dot_general(ds, qs, TN, preferred_element_type=F32)
                    dq_c = _unstack_heads(jnp.dot(ds, kb, preferred_element_type=F32), is0)
                    new.append((dq_c, dk_c, dv_c))
                old = [(dq_ref[rows, :], dk_ref[krows, :], dv_ref[krows, :]) for rows, krows, _ in blocks]
                for (rows, krows, _), (dq_c, dk_c, dv_c), (dq_o, dk_o, dv_o) in zip(blocks, new, old):
                    dq_ref[rows, :] = dq_o + dq_c
                    dk_ref[krows, :] = dk_o + dk_c
                    dv_ref[krows, :] = dv_o + dv_c
                return carry

            lax.fori_loop(0, ngroups, group, 0)

        @pl.when(hp == steps - 1)
        def _():
            for cp in copies:
                cp.wait()

    col = pl.BlockSpec((S, LANES), lambda i: (0, i))
    res = pl.pallas_call(
        body, name="attn_bwd", grid=(steps,),
        in_specs=[pl.BlockSpec(memory_space=pltpu.SMEM)] + [col] * 6 + [ANY_SPEC] * ng,
        out_specs=[col] * 3 + [ANY_SPEC] * ng,
        out_shape=[jax.ShapeDtypeStruct((S, ATT_WIDTH), F32)] * 3 + _exchange_shapes(grads, scatter),
        scratch_shapes=[pltpu.VMEM((S, LANES), F32), pltpu.VMEM((2, 2 * SPAN, 2 * SPAN), F32)] + _exchange_sems(ng),
        compiler_params=_params("arbitrary"),
    )(slopes, q, k, v, o, lse, do, *grads)
    return res[0], res[1], res[2], res[3:]


def _in_bwd(dq, dk, dv, du, w_in, x, g1, dh1):
    S = x.shape[0]

    def body(dq_ref, dk_ref, dv_ref, du_ref, w_ref, x_ref, g_ref, dh1_ref, dz_ref, dx_ref, dg_ref):
        @pl.when(pl.program_id(0) == 0)
        def _():
            dg_ref[...] = jnp.zeros(dg_ref.shape, F32)

        srcs = (dq_ref, dk_ref, dv_ref, du_ref)
        dhn = jnp.zeros((TM, D_MODEL), F32)
        for j in range(N_DEV):
            dz = srcs[j // 2][:, (j % 2) * 256:(j % 2 + 1) * 256]
            if j < 2:
                dz = dz * (HEAD_DIM ** -0.5)
            dz = dz.astype(BF16)
            dz_ref[j] = dz
            dhn = dhn + lax.dot_general(dz, w_ref[j], NT, preferred_element_type=F32)
        g = g_ref[...]
        r, n = _rms(x_ref[...])
        dg_ref[...] += _colsum(dhn * n)
        dx_ref[...] = dh1_ref[...] + _rms_bwd(r, n, g, dhn)

    tok = lambda w: pl.BlockSpec((TM, w), lambda i: (i, 0))
    return pl.pallas_call(
        body, name="in_bwd", grid=(S // TM,),
        in_specs=[tok(512)] * 4 + [_const(w_in.shape), tok(D_MODEL), _const((1, D_MODEL)), tok(D_MODEL)],
        out_specs=[pl.BlockSpec((N_DEV, TM, 256), lambda i: (0, i, 0)), tok(D_MODEL),
                   pl.BlockSpec((1, D_MODEL), lambda i: (0, 0))],
        out_shape=[jax.ShapeDtypeStruct((N_DEV, S, 256), BF16), jax.ShapeDtypeStruct((S, D_MODEL), F32),
                   jax.ShapeDtypeStruct((1, D_MODEL), F32)],
        compiler_params=_params("arbitrary"),
    )(dq, dk, dv, du, w_in, x, g1, dh1)


def _adamw(name, parts, w, m, v):
    R, C = w.shape
    rb = max([r for r in range(16, R // 2 + 1, 16) if R % r == 0 and r * C <= 512 * 1024], default=R)

    def body(p_ref, w_ref, m_ref, v_ref, g_ref, d_ref, mo_ref, vo_ref):
        g = p_ref[0].astype(F32)
        for s in range(1, N_DEV):
            g = g + p_ref[s].astype(F32)
        g_ref[...] = g
        d_ref[...], mo_ref[...], vo_ref[...] = _adam_update(g, w_ref[...], m_ref[...], v_ref[...])

    blk = pl.BlockSpec((rb, C), lambda i: (i, 0))
    return pl.pallas_call(
        body, name=name, grid=(R // rb,),
        in_specs=[pl.BlockSpec((N_DEV, rb, C), lambda i: (0, i, 0)), blk, blk, blk],
        out_specs=[blk] * 4,
        out_shape=[jax.ShapeDtypeStruct((R, C), F32)] * 4,
        compiler_params=_params("arbitrary"),
    )(parts, w, m, v)


def _adam_update(g, w, m, v):
    m_new = ADAM_B1 * m + (1.0 - ADAM_B1) * g
    v_new = ADAM_B2 * v + (1.0 - ADAM_B2) * (g * g)
    m_hat = m_new / (1.0 - ADAM_B1 ** ADAM_STEP)
    v_hat = v_new / (1.0 - ADAM_B2 ** ADAM_STEP)
    return -ADAM_LR * (m_hat / (jnp.sqrt(v_hat) + ADAM_EPS) + ADAM_WD * w), m_new, v_new


def _adamw_small(parts, loss_parts, ws, ms, vs):
    n = len(ws)

    def body(*refs):
        p_refs, lp_ref = refs[:n], refs[n]
        w_refs, m_refs, v_refs = refs[n + 1:2 * n + 1], refs[2 * n + 1:3 * n + 1], refs[3 * n + 1:4 * n + 1]
        outs = refs[4 * n + 1:]
        for i in range(n):
            g = p_refs[i][0]
            for s in range(1, N_DEV):
                g = g + p_refs[i][s]
            d, m_new, v_new = _adam_update(g, w_refs[i][...], m_refs[i][...], v_refs[i][...])
            outs[i][...] = g
            outs[n + i][...] = d
            outs[2 * n + i][...] = m_new
            outs[3 * n + i][...] = v_new
        tot = lp_ref[0]
        for s in range(1, N_DEV):
            tot = tot + lp_ref[s]
        outs[4 * n][...] = tot

    shapes = [jax.ShapeDtypeStruct(w.shape, F32) for w in ws]
    res = pl.pallas_call(
        body, name="adamw_replicated",
        out_shape=shapes * 4 + [jax.ShapeDtypeStruct(loss_parts.shape[1:], F32)],
        compiler_params=_params(),
    )(*parts, loss_parts, *ws, *ms, *vs)
    return res[:n], res[n:2 * n], res[2 * n:3 * n], res[3 * n:4 * n], res[4 * n]


def _gather2(name, arrays):
    n = len(arrays)

    def body(*refs):
        first, passed, last = _gather2_copies(refs[:n], refs[n:2 * n], *refs[2 * n:])
        for cp in first:
            cp.start()
        for arrival, cp in passed:
            arrival.wait_recv()
            cp.start()
        for wait in last:
            wait()

    return pl.pallas_call(
        body, name=name,
        in_specs=[ANY_SPEC] * n, out_specs=[ANY_SPEC] * n, out_shape=_exchange_shapes(arrays, (False,) * n),
        scratch_shapes=_exchange_sems(n),
    )(*arrays)


def _dw_in_exchange(hn, dz, small):
    S = hn.shape[0]
    nt = S // TK
    ns = len(small)
    kd, nd = hn.shape[1], dz.shape[2]
    me_arr = (4 * lax.axis_index("x") + 2 * lax.axis_index("y") + lax.axis_index("c")).astype(jnp.int32).reshape(1)

    def body(me_ref, x_ref, dy_ref, *rest):
        recv_ref = rest[ns]
        acc, stage, send_sems, recv_sems, own_sem = rest[2 * ns + 1:2 * ns + 6]
        j, t = pl.program_id(0), pl.program_id(1)
        x, y, c = lax.axis_index("x"), lax.axis_index("y"), lax.axis_index("c")
        me = 4 * x + 2 * y + c
        small_copies = _exchange_copies(rest[:ns], rest[ns + 1:2 * ns + 1], (False,) * ns, *rest[2 * ns + 6:])

        @pl.when((j == 0) & (t == 0))
        def _():
            for cp in small_copies:
                cp.start()

        @pl.when(t == 0)
        def _():
            acc[...] = jnp.zeros(acc.shape, F32)

        acc[...] += lax.dot_general(x_ref[...], dy_ref[...], TN, preferred_element_type=F32)

        def to_owner(k, owner):
            return pltpu.make_async_remote_copy(
                src_ref=stage.at[owner], dst_ref=recv_ref.at[me], send_sem=send_sems.at[k], recv_sem=recv_sems.at[k],
                device_id=(owner // 4, (owner // 2) % 2, owner % 2), device_id_type=MESH)

        own = pltpu.make_async_copy(stage.at[me], recv_ref.at[me], own_sem)

        @pl.when(t == nt - 1)
        def _():
            owner = (me + 1 + j) % N_DEV
            stage[owner] = acc[...].astype(BF16)

            @pl.when(j < N_DEV - 1)
            def _():
                to_owner(j, owner).start()

            @pl.when(j == N_DEV - 1)
            def _():
                own.start()
                own.wait()
                for k in range(N_DEV - 1):
                    to_owner(k, me).wait_send()
                    to_owner(k, me).wait_recv()
                for cp in small_copies:
                    cp.wait()

    slab = lambda j, me_ref: (me_ref[0] + 1 + j) % N_DEV
    grid_spec = pltpu.PrefetchScalarGridSpec(
        num_scalar_prefetch=1, grid=(N_DEV, nt),
        in_specs=[pl.BlockSpec((TK, kd), lambda j, t, me_ref: (t, 0)),
                  pl.BlockSpec((None, TK, nd), lambda j, t, me_ref: (slab(j, me_ref), t, 0))] + [ANY_SPEC] * ns,
        out_specs=[ANY_SPEC] * (ns + 1),
        scratch_shapes=[pltpu.VMEM((kd, nd), F32), pltpu.VMEM((N_DEV, kd, nd), BF16),
                        pltpu.SemaphoreType.DMA((N_DEV - 1,)), pltpu.SemaphoreType.DMA((N_DEV - 1,)),
                        pltpu.SemaphoreType.DMA] + _exchange_sems(ns))
    res = pl.pallas_call(
        body, name="dw_in_exchange", grid_spec=grid_spec,
        out_shape=[jax.ShapeDtypeStruct((N_DEV, kd, nd), BF16)] + _exchange_shapes(small, (False,) * ns),
        compiler_params=_params("arbitrary", "arbitrary"),
    )(me_arr, hn, dz, *small)
    return res[0], res[1:]


def kernel(x, p, ln_mix, w_in, pool_w, pool_scale, w_out, ln_ffn, w_up, conv_w, conv_b, w_down, ln_ple, w_ple_gate, w_ple, ln_final, loss_target, m_ln_mix, m_w_in, m_pool_w, m_pool_scale, m_w_out, m_ln_ffn, m_w_up, m_conv_w, m_conv_b, m_w_down, m_ln_ple, m_w_ple_gate, m_w_ple, m_ln_final, v_ln_mix, v_w_in, v_pool_w, v_pool_scale, v_w_out, v_ln_ffn, v_w_up, v_conv_w, v_conv_b, v_w_down, v_ln_ple, v_w_ple_gate, v_w_ple, v_ln_final):
    xs, ps, tgt, pool_w0 = x[0], p[0, 0], loss_target[0], pool_w[0]
    slopes = jnp.exp2(-8.0 * (jnp.arange(N_HEADS, dtype=F32) + 1.0) / N_HEADS)
    conv_b_s = conv_b.reshape(N_DEV, 1, FF_SHARD)

    (w_in_g,) = _gather2("gather_w_in", [w_in[0].astype(BF16)])
    (q, k, v, u, hn1), (w_out_g,) = _qkvu(xs, ln_mix, w_in_g, [w_out[0].astype(BF16)])
    w_up_b = w_up[0].astype(BF16)
    att, lse, (w_up_lo,) = _attn_fwd(slopes, q, k, v, [w_up_b[:D_MODEL // 2]])
    w_out_f = w_out_g.reshape(D_MODEL, D_MODEL)
    h1, mix, dlt, (conv_w_g, w_up_hi) = _mix_out(xs, att, u, pool_w0, pool_scale, w_out_f,
                                                 [conv_w[0], w_up_b[D_MODEL // 2:]])
    w_up_g = (w_up_lo, w_up_hi)
    a_fwd, hn2, up, upc, (w_down_g, w_pg_g, w_ple_g) = _ffn_up(
        h1, ln_ffn, w_up_g, conv_w_g, conv_b_s,
        [w_down[0].astype(BF16), w_ple_gate[0].astype(BF16), w_ple[0].astype(BF16)])
    w_down_f = w_down_g.reshape(4, FF_SHARD, D_MODEL)
    h2 = _ffn_down(h1, a_fwd, w_down_f)
    w_pg_f = w_pg_g.reshape(D_MODEL, D_MODEL)
    w_ple_f = jnp.transpose(w_ple_g, (1, 0, 2)).reshape(PLE_DIM, D_MODEL)
    loss_blk, dh2, dh2b, d_w_pg, d_w_ple, d_ln_ple, d_ln_final = _head(
        h2, ps, ln_ple, w_pg_f, w_ple_f, ln_final.reshape(1, D_MODEL), tgt)

    d_w_pg = d_w_pg.reshape(N_DEV, D_MODEL // N_DEV, D_MODEL)
    d_w_ple = jnp.transpose(d_w_ple.reshape(PLE_DIM, N_DEV, LANES), (1, 0, 2))
    d_w_down, dup, d_conv_w, d_conv_b, (r_w_pg, r_w_ple) = _ffn_bwd_a(dh2b, up, upc, w_down_f, [d_w_pg, d_w_ple])
    d_w_down = d_w_down.reshape(N_DEV, D_FF // N_DEV, D_MODEL)
    dpre, dh1, dh1b, d_ln_ffn, (r_conv_w, r_w_down) = _ffn_bwd_b(
        dup, conv_w_g, w_up_g, h1, ln_ffn, dh2, [d_conv_w, d_w_down])
    datt, du, d_pool_w, d_pool_scale, d_w_out = _mix_bwd(dh1b, w_out_f, dlt, mix, pool_w0, pool_scale)
    d_w_out = d_w_out.reshape(N_DEV, D_MODEL // N_DEV, D_MODEL)
    d_w_up = _wgrad("dw_up", dpre, hn2, "lead", "full", N_DEV, FF_SHARD, D_MODEL)
    rep_late = [d_pool_w, d_pool_scale, d_ln_ffn, d_conv_b.reshape(1, 2 * D_FF), d_ln_ple, d_ln_final, loss_blk]
    dq, dk, dv, received = _attn_bwd(slopes, q, k, v, att, lse, datt, [d_w_out, d_w_up] + rep_late,
                                     (True, True) + (False,) * len(rep_late))
    r_w_out, r_w_up, r_rep = received[0], received[1], list(received[2:])
    dz, grad_x, d_ln_mix = _in_bwd(dq, dk, dv, du, w_in_g, xs, ln_mix, dh1)

    rep_names = ("ln_mix", "pool_w", "pool_scale", "ln_ffn", "conv_b", "ln_ple", "ln_final")
    rep_w = [ln_mix, pool_w0, pool_scale, ln_ffn, conv_b, ln_ple, ln_final.reshape(1, D_MODEL)]
    rep_m = [m_ln_mix, m_pool_w[0], m_pool_scale, m_ln_ffn, m_conv_b, m_ln_ple, m_ln_final.reshape(1, D_MODEL)]
    rep_v = [v_ln_mix, v_pool_w[0], v_pool_scale, v_ln_ffn, v_conv_b, v_ln_ple, v_ln_final.reshape(1, D_MODEL)]
    r_w_in, (r_ln_mix,) = _dw_in_exchange(hn1, dz, [d_ln_mix])
    small = _adamw_small([r_ln_mix] + r_rep[:-1], r_rep[-1], rep_w, rep_m, rep_v)
    loss = small[4][0, 0]

    sharded = {}
    sharded["w_in"] = _adamw("adamw_w_in", r_w_in, w_in[0], m_w_in[0], v_w_in[0])
    sharded["w_out"] = _adamw("adamw_w_out", r_w_out, w_out[0], m_w_out[0], v_w_out[0])
    sharded["w_up"] = [t.T for t in _adamw("adamw_w_up", r_w_up, w_up[0].T, m_w_up[0].T, v_w_up[0].T)]
    sharded["conv_w"] = _adamw("adamw_conv_w", r_conv_w, conv_w[0], m_conv_w[0], v_conv_w[0])
    sharded["w_down"] = _adamw("adamw_w_down", r_w_down, w_down[0], m_w_down[0], v_w_down[0])
    sharded["w_ple_gate"] = _adamw("adamw_w_ple_gate", r_w_pg, w_ple_gate[0], m_w_ple_gate[0], v_w_ple_gate[0])
    sharded["w_ple"] = _adamw("adamw_w_ple", r_w_ple, w_ple[0], m_w_ple[0], v_w_ple[0])

    shapes = dict(w_in=w_in, w_out=w_out, w_up=w_up, conv_w=conv_w, w_down=w_down, w_ple_gate=w_ple_gate, w_ple=w_ple,
                  ln_mix=ln_mix, pool_w=pool_w, pool_scale=pool_scale, ln_ffn=ln_ffn, conv_b=conv_b, ln_ple=ln_ple,
                  ln_final=ln_final)

    def leaf(kind, n):
        src = sharded[n][kind] if n in sharded else small[kind][rep_names.index(n)]
        return src.reshape(shapes[n].shape)

    order = ("ln_mix", "w_in", "pool_w", "pool_scale", "w_out", "ln_ffn", "w_up", "conv_w", "conv_b", "w_down", "ln_ple",
             "w_ple_gate", "w_ple", "ln_final")
    outs = [loss, grad_x[None]]
    for kind in range(4):
        outs += [leaf(kind, n) for n in order]
    return tuple(outs)
```

```python
import jax
import jax.numpy as jnp
from jax import lax
from jax.experimental import pallas as pl
from jax.experimental.pallas import tpu as pltpu

F32 = jnp.float32
BF16 = jnp.bfloat16

N_DEV = 8
D_MODEL = 1024
ATT_WIDTH = 512
POOL_WIDTH = 512
N_HEADS = 8
HEAD_DIM = 64
SPAN = 128
DILATIONS = (1, 4, 16)
POOL_WINDOWS = (2, 4, 8, 16)
POOL_GROUP = 128
D_FF = 2816
FF_SHARD = 2 * D_FF // N_DEV
PLE_DIM = 256
EPS = 1e-6
NEG = -1e30

ADAM_LR = 0.001
ADAM_B1 = 0.9
ADAM_B2 = 0.999
ADAM_EPS = 1e-08
ADAM_WD = 0.01
ADAM_STEP = 10

LANES = 128
HALO = 16
TM = 512
TM_FF = 256
TK = 4096
ATTN_GROUP_FWD = 16
ATTN_GROUP_BWD = 8
VMEM_LIMIT = 56 * 1024 * 1024

MESH = pl.DeviceIdType.MESH
NT = (((1,), (1,)), ((), ()))
TN = (((0,), (0,)), ((), ()))


def _params(*sem):
    return pltpu.CompilerParams(dimension_semantics=sem or None, vmem_limit_bytes=VMEM_LIMIT)


def _const(shape):
    n = len(shape)
    return pl.BlockSpec(shape, lambda *_: (0,) * n, pipeline_mode=pl.Buffered(1))


def _rms(h):
    r = lax.rsqrt(jnp.mean(h * h, axis=-1, keepdims=True) + EPS)
    return r, h * r


def _rms_bwd(r, n, g, dhn):
    dn = dhn * g
    return r * (dn - n * jnp.mean(dn * n, axis=-1, keepdims=True))


def _colsum(a):
    return jnp.sum(a, axis=0, keepdims=True)


def _gather2_copies(ins, outs, send_sems, recv_sems, local_sems):
    n = len(ins)
    x, y, c = lax.axis_index("x"), lax.axis_index("y"), lax.axis_index("c")
    slot = lambda px, py, pc: 4 * px + 2 * py + pc
    chips = [(x, 1 - y), (1 - x, y), (1 - x, 1 - y)]
    first, passed, last = [], [], []

    def remote(a, r, src, dst_slot, to):
        return pltpu.make_async_remote_copy(
            src_ref=src, dst_ref=outs[a].at[dst_slot],
            send_sem=send_sems.at[a * (N_DEV - 1) + r], recv_sem=recv_sems.at[a * (N_DEV - 1) + r],
            device_id=to, device_id_type=MESH)

    for a in range(n):
        mine = pltpu.make_async_copy(ins[a], outs[a].at[slot(x, y, c)], local_sems.at[a])
        to_sibling = remote(a, 0, ins[a], slot(x, y, c), (x, y, 1 - c))
        first += [mine, to_sibling]
        last += [mine.wait, to_sibling.wait_send, to_sibling.wait_recv]
        for r, (px, py) in enumerate(chips, start=1):
            to_chip = remote(a, r, ins[a], slot(x, y, c), (px, py, c))
            onward = remote(a, 3 + r, outs[a].at[slot(px, py, c)], slot(px, py, c), (x, y, 1 - c))
            first.append(to_chip)
            passed.append((to_chip, onward))
            last += [to_chip.wait_send, onward.wait_send, onward.wait_recv]
    return first, passed, last


def _gather2_begin(plan, step, pass_step):
    first, passed, _ = plan

    @pl.when(step == 0)
    def _():
        for cp in first:
            cp.start()

    @pl.when(step == pass_step)
    def _():
        for arrival, cp in passed:
            arrival.wait_recv()
            cp.start()


def _gather2_end(plan, step, nsteps):
    @pl.when(step == nsteps - 1)
    def _():
        for wait in plan[2]:
            wait()


ANY_SPEC = pl.BlockSpec(memory_space=pl.ANY)


def _exchange_shapes(arrays, scatter):
    out = []
    for a, s in zip(arrays, scatter):
        slab = a.shape[1:] if s else a.shape
        out.append(jax.ShapeDtypeStruct((N_DEV,) + tuple(slab), a.dtype))
    return out


def _exchange_sems(n):
    return [pltpu.SemaphoreType.DMA((n * (N_DEV - 1),)), pltpu.SemaphoreType.DMA((n * (N_DEV - 1),)),
            pltpu.SemaphoreType.DMA((n,))]


def _exchange_copies(ins, outs, scatter, send_sems, recv_sems, local_sems):
    n = len(ins)
    x, y, c = lax.axis_index("x"), lax.axis_index("y"), lax.axis_index("c")
    me = 4 * x + 2 * y + c
    copies = []
    for a in range(n):
        src = ins[a].at[me] if scatter[a] else ins[a]
        copies.append(pltpu.make_async_copy(src, outs[a].at[me], local_sems.at[a]))
    for k in range(1, N_DEV):
        px = 1 - x if k & 4 else x
        py = 1 - y if k & 2 else y
        pc = 1 - c if k & 1 else c
        pid = 4 * px + 2 * py + pc
        for a in range(n):
            src = ins[a].at[pid] if scatter[a] else ins[a]
            copies.append(pltpu.make_async_remote_copy(
                src_ref=src, dst_ref=outs[a].at[me],
                send_sem=send_sems.at[a * (N_DEV - 1) + k - 1], recv_sem=recv_sems.at[a * (N_DEV - 1) + k - 1],
                device_id=(px, py, pc), device_id_type=MESH))
    return copies


def _qkvu(x, g1, w_in, shards):
    S = x.shape[0]
    ns = len(shards)
    nsteps = S // TM

    def body(x_ref, g_ref, w_ref, *rest):
        q_ref, k_ref, v_ref, u_ref, hn_ref = rest[ns:ns + 5]
        plan = _gather2_copies(rest[:ns], rest[ns + 5:2 * ns + 5], *rest[2 * ns + 5:])
        _gather2_begin(plan, pl.program_id(0), nsteps - 2)
        r, n = _rms(x_ref[...])
        hn = (n * g_ref[...]).astype(BF16)
        hn_ref[...] = hn
        outs = (q_ref, k_ref, v_ref, u_ref)
        for j in range(N_DEV):
            z = jnp.dot(hn, w_ref[j], preferred_element_type=F32)
            if j < 2:
                z = z * (HEAD_DIM ** -0.5)
            outs[j // 2][:, (j % 2) * 256:(j % 2 + 1) * 256] = z
        _gather2_end(plan, pl.program_id(0), nsteps)

    tok = lambda w: pl.BlockSpec((TM, w), lambda i: (i, 0))
    res = pl.pallas_call(
        body, name="qkvu", grid=(nsteps,),
        in_specs=[tok(D_MODEL), _const((1, D_MODEL)), _const(w_in.shape)] + [ANY_SPEC] * ns,
        out_specs=[tok(512)] * 4 + [tok(D_MODEL)] + [ANY_SPEC] * ns,
        out_shape=[jax.ShapeDtypeStruct((S, 512), F32)] * 4 + [jax.ShapeDtypeStruct((S, D_MODEL), BF16)]
        + _exchange_shapes(shards, (False,) * ns),
        scratch_shapes=_exchange_sems(ns),
        compiler_params=_params("arbitrary"),
    )(x, g1, w_in, *shards)
    return res[:5], res[5:]


def _attn_fill_bias(bias_s, slope_ref, hp, d):
    qi = lax.broadcasted_iota(jnp.int32, (SPAN, 2 * SPAN), 0)
    kj = lax.broadcasted_iota(jnp.int32, (SPAN, 2 * SPAN), 1)
    for t, diff in enumerate((qi + SPAN - kj, qi - kj)):
        valid = (diff >= 0) & (diff <= SPAN)
        dist = diff.astype(F32) * float(d)
        for h in range(2):
            bias_s[t, h * SPAN:(h + 1) * SPAN, :] = jnp.where(valid, -slope_ref[2 * hp + h] * dist, NEG)


def _stack_heads(x, is0):
    return jnp.concatenate([jnp.where(is0, x, 0.0), jnp.where(is0, 0.0, x)], axis=0)


def _unstack_heads(y, is0):
    return jnp.where(is0, y[0:SPAN], y[SPAN:2 * SPAN])


def _attn_block(i, g, d, nb, group):
    gr = min(d, group)
    gn = group // gr
    per = d // gr
    r = (i & (per - 1)) * gr + g % gr
    n = (i >> (per.bit_length() - 1)) + (g // gr) * (nb // gn)
    k0 = jnp.maximum(n - 1, 0)

    def ds(block, nrows):
        start = block * (SPAN * d) + r
        return pl.ds(start, nrows, stride=d) if d > 1 else pl.ds(start, nrows)

    return ds(n, SPAN), ds(k0, 2 * SPAN), jnp.where(n == 0, 1, 0)


def _attn_groups(S, d, group, writes_key_rows=False):
    nb = S // d // SPAN
    gn = group // min(d, group)
    assert nb >= 2 and nb % gn == 0 and (gn == 1 or nb // gn >= (3 if writes_key_rows else 2))
    return nb, d * nb // group


def _attn_fwd(slopes, q, k, v, shards):
    S = q.shape[0]
    ns = len(shards)
    steps = ATT_WIDTH // LANES

    def body(slope_ref, q_ref, k_ref, v_ref, *rest):
        o_ref, lse_ref = rest[ns:ns + 2]
        m_s, l_s, bias_s = rest[2 * ns + 2:2 * ns + 5]
        hp = pl.program_id(0)
        plan = _gather2_copies(rest[:ns], rest[ns + 2:2 * ns + 2], *rest[2 * ns + 5:])
        _gather2_begin(plan, hp, steps - 1)

        is0 = lax.broadcasted_iota(jnp.int32, (SPAN, LANES), 1) < HEAD_DIM
        for pi, d in enumerate(DILATIONS):
            nb, ngroups = _attn_groups(S, d, ATTN_GROUP_FWD)
            _attn_fill_bias(bias_s, slope_ref, hp, d)

            def group(i, carry, d=d, pi=pi, nb=nb):
                blocks = [_attn_block(i, g, d, nb, ATTN_GROUP_FWD) for g in range(ATTN_GROUP_FWD)]
                loaded = [(q_ref[rows, :], k_ref[krows, :].astype(BF16), v_ref[krows, :].astype(BF16))
                          for rows, krows, _ in blocks]
                new = []
                for (rows, krows, tab), (qb, kb, vb) in zip(blocks, loaded):
                    qs = _stack_heads(qb, is0).astype(BF16)
                    s = lax.dot_general(qs, kb, NT, preferred_element_type=F32) + bias_s[tab]
                    m = jnp.max(s, axis=-1, keepdims=True)
                    e = jnp.exp(s - m)
                    l = jnp.sum(e, axis=-1, keepdims=True)
                    pv = jnp.dot(e.astype(BF16), vb, preferred_element_type=F32)
                    new.append([_unstack_heads(jnp.broadcast_to(m, pv.shape), is0),
                                _unstack_heads(jnp.broadcast_to(l, pv.shape), is0), _unstack_heads(pv, is0)])
                if pi > 0:
                    old = [(m_s[rows, :], l_s[rows, :], o_ref[rows, :]) for rows, _, _ in blocks]
                    for st, (m_o, l_o, o_o) in zip(new, old):
                        m_n = jnp.maximum(m_o, st[0])
                        a_o = jnp.exp(m_o - m_n)
                        a_b = jnp.exp(st[0] - m_n)
                        st[:] = [m_n, a_o * l_o + a_b * st[1], a_o * o_o + a_b * st[2]]
                for (rows, _, _), (m_b, l_b, acc) in zip(blocks, new):
                    if pi == len(DILATIONS) - 1:
                        o_ref[rows, :] = acc / l_b
                        lse_ref[rows, :] = m_b + jnp.log(l_b)
                    else:
                        o_ref[rows, :] = acc
                        m_s[rows, :] = m_b
                        l_s[rows, :] = l_b
                return carry

            lax.fori_loop(0, ngroups, group, 0)

        _gather2_end(plan, hp, steps)

    col = pl.BlockSpec((S, LANES), lambda i: (0, i))
    res = pl.pallas_call(
        body, name="attn_fwd", grid=(steps,),
        in_specs=[pl.BlockSpec(memory_space=pltpu.SMEM), col, col, col] + [ANY_SPEC] * ns,
        out_specs=[col, col] + [ANY_SPEC] * ns,
        out_shape=[jax.ShapeDtypeStruct((S, ATT_WIDTH), F32)] * 2 + _exchange_shapes(shards, (False,) * ns),
        scratch_shapes=[pltpu.VMEM((S, LANES), F32), pltpu.VMEM((S, LANES), F32),
                        pltpu.VMEM((2, 2 * SPAN, 2 * SPAN), F32)] + _exchange_sems(ns),
        compiler_params=_params("arbitrary"),
    )(slopes, q, k, v, *shards)
    return res[0], res[1], res[2:]


def _pool_count(i, w):
    t = i * TM + lax.broadcasted_iota(jnp.int32, (TM, 1), 0)
    return jnp.minimum(t + 1, w).astype(F32)


def _mix_out(x, att, u, pool_w, pool_scale, w_out, shards):
    S = x.shape[0]
    ns = len(shards)
    nsteps = S // TM

    def body(x_ref, att_ref, u_ref, pw_ref, ps_ref, w_ref, *rest):
        h1_ref, mix_ref, dlt_ref = rest[ns:ns + 3]
        ubuf = rest[2 * ns + 3]
        i = pl.program_id(0)
        plan = _gather2_copies(rest[:ns], rest[ns + 3:2 * ns + 3], *rest[2 * ns + 4:])
        _gather2_begin(plan, i, nsteps - 1)

        @pl.when(i == 0)
        def _():
            ubuf[0:HALO, :] = jnp.zeros((HALO, POOL_WIDTH), F32)

        ubuf[HALO:HALO + TM, :] = u_ref[...]
        mix_ref[:, 0:ATT_WIDTH] = att_ref[...].astype(BF16)
        for g, w in enumerate(POOL_WINDOWS):
            cols = slice(g * POOL_GROUP, (g + 1) * POOL_GROUP)
            ug = ubuf[HALO:HALO + TM, cols]
            acc = ug
            for j in range(1, w):
                acc = acc + ubuf[HALO - j:HALO - j + TM, cols]
            dlt = (acc / _pool_count(i, w) - ug).astype(BF16)
            dlt_ref[:, cols] = dlt
            yg = jnp.dot(dlt, pw_ref[g].astype(BF16), preferred_element_type=F32) * ps_ref[:, cols]
            mix_ref[:, ATT_WIDTH + g * POOL_GROUP:ATT_WIDTH + (g + 1) * POOL_GROUP] = yg.astype(BF16)
        ubuf[0:HALO, :] = ubuf[TM:TM + HALO, :]
        h1_ref[...] = x_ref[...] + jnp.dot(mix_ref[...], w_ref[...], preferred_element_type=F32)
        _gather2_end(plan, i, nsteps)

    tok = lambda w: pl.BlockSpec((TM, w), lambda i: (i, 0))
    res = pl.pallas_call(
        body, name="mix_out", grid=(nsteps,),
        in_specs=[tok(D_MODEL), tok(ATT_WIDTH), tok(POOL_WIDTH), _const(pool_w.shape), _const((1, POOL_WIDTH)),
                  _const(w_out.shape)] + [ANY_SPEC] * ns,
        out_specs=[tok(D_MODEL), tok(D_MODEL), tok(POOL_WIDTH)] + [ANY_SPEC] * ns,
        out_shape=[jax.ShapeDtypeStruct((S, D_MODEL), F32), jax.ShapeDtypeStruct((S, D_MODEL), BF16),
                   jax.ShapeDtypeStruct((S, POOL_WIDTH), BF16)] + _exchange_shapes(shards, (False,) * ns),
        scratch_shapes=[pltpu.VMEM((TM + HALO, POOL_WIDTH), F32)] + _exchange_sems(ns),
        compiler_params=_params("arbitrary"),
    )(x, att, u, pool_w, pool_scale, w_out, *shards)
    return res[0], res[1], res[2], res[3:]


def _conv_fwd(stage, upre, prev, cw, cb):
    T = upre.shape[0]
    stage[0:HALO, :] = prev
    stage[HALO:HALO + T, :] = upre
    return cb + cw[0:1, :] * stage[HALO - 2:HALO - 2 + T, :] + cw[1:2, :] * stage[HALO - 1:HALO - 1 + T, :] + cw[2:3, :] * upre


def _ffn_down(h1, a, w_down):
    S = h1.shape[0]

    def body(h1_ref, a_ref, wd_ref, h2_ref):
        acc = h1_ref[...]
        for j in range(4):
            acc = acc + jnp.dot(a_ref[j], wd_ref[j], preferred_element_type=F32)
        h2_ref[...] = acc

    tok = pl.BlockSpec((TM, D_MODEL), lambda i: (i, 0))
    return pl.pallas_call(
        body, name="ffn_down", grid=(S // TM,),
        in_specs=[tok, pl.BlockSpec((4, TM, FF_SHARD), lambda i: (0, i, 0)), _const(w_down.shape)],
        out_specs=tok, out_shape=jax.ShapeDtypeStruct((S, D_MODEL), F32),
        compiler_params=_params("arbitrary"),
    )(h1, a, w_down)


def _ffn_up(h1, g2, w_up, conv_w, conv_b, shards):
    S = h1.shape[0]
    T = TM_FF
    ns = len(shards)
    nsteps = S // T

    def body(h1_ref, g_ref, wu_ref, cw_ref, cb_ref, *rest):
        a_ref, hn_ref, up_ref, upc_ref = rest[ns:ns + 4]
        carry, stage = rest[2 * ns + 4:2 * ns + 6]
        i = pl.program_id(0)
        plan = _gather2_copies(rest[:ns], rest[ns + 4:2 * ns + 4], *rest[2 * ns + 6:])
        _gather2_begin(plan, i, nsteps // 2)

        @pl.when(i == 0)
        def _():
            carry[...] = jnp.zeros(carry.shape, F32)

        h1t = h1_ref[...]
        r, n = _rms(h1t)
        hn = (n * g_ref[...]).astype(BF16)
        hn_ref[...] = hn
        for j in range(4):
            conv = []
            for jj in (j, j + 4):
                upre = jnp.dot(hn, wu_ref[jj], preferred_element_type=F32)
                up_ref[jj] = upre.astype(BF16)
                conv.append(_conv_fwd(stage, upre, carry[jj], cw_ref[jj], cb_ref[jj]))
                upc_ref[jj] = conv[-1].astype(BF16)
                carry[jj] = stage[T:T + HALO, :]
            gate, val = conv
            a_ref[j] = (gate * jax.nn.sigmoid(gate) * val).astype(BF16)
        _gather2_end(plan, i, nsteps)

    tok = lambda w: pl.BlockSpec((T, w), lambda i: (i, 0))
    shard = lambda n: pl.BlockSpec((n, T, FF_SHARD), lambda i: (0, i, 0))
    res = pl.pallas_call(
        body, name="ffn_up", grid=(nsteps,),
        in_specs=[tok(D_MODEL), _const((1, D_MODEL)), _const(w_up.shape), _const(conv_w.shape), _const(conv_b.shape)]
        + [ANY_SPEC] * ns,
        out_specs=[shard(4), tok(D_MODEL), shard(N_DEV), shard(N_DEV)] + [ANY_SPEC] * ns,
        out_shape=[jax.ShapeDtypeStruct((4, S, FF_SHARD), BF16), jax.ShapeDtypeStruct((S, D_MODEL), BF16)]
        + [jax.ShapeDtypeStruct((N_DEV, S, FF_SHARD), BF16)] * 2 + _exchange_shapes(shards, (False,) * ns),
        scratch_shapes=[pltpu.VMEM((N_DEV, HALO, FF_SHARD), F32), pltpu.VMEM((T + HALO, FF_SHARD), F32)]
        + _exchange_sems(ns),
        compiler_params=_params("arbitrary"),
    )(h1, g2, w_up, conv_w, conv_b, *shards)
    return res[0], res[1], res[2], res[3], res[4:]


def _head(h2, p, g3, w_pg, w_ple, g4, target):
    S = h2.shape[0]
    nt = S // TM

    def body(h2_ref, p_ref, g3_ref, wpg_ref, wple_ref, g4_ref, t_ref,
             loss_ref, dh2_ref, dh2b_ref, dwpg_ref, dwple_ref, dg3_ref, dg4_ref, lacc, pg_acc, ple_acc):
        i = pl.program_id(0)

        @pl.when(i == 0)
        def _():
            lacc[...] = jnp.zeros(lacc.shape, F32)
            pg_acc[...] = jnp.zeros(pg_acc.shape, F32)
            ple_acc[...] = jnp.zeros(ple_acc.shape, F32)
            dg3_ref[...] = jnp.zeros(dg3_ref.shape, F32)
            dg4_ref[...] = jnp.zeros(dg4_ref.shape, F32)

        h2t = h2_ref[...]
        g3, g4 = g3_ref[...], g4_ref[...]
        r3, n3 = _rms(h2t)
        hn3 = (n3 * g3).astype(BF16)
        pb = p_ref[...].astype(BF16)
        gs = jax.nn.sigmoid(jnp.dot(hn3, wpg_ref[...], preferred_element_type=F32))
        pe = jnp.dot(pb, wple_ref[...], preferred_element_type=F32)
        h3 = h2t + gs * pe
        r4, n4 = _rms(h3)
        err = n4 * g4 - t_ref[...]
        lacc[...] += _colsum(err * err)
        dy = err * (1.0 / D_MODEL)
        dg4_ref[...] += _colsum(dy * n4)
        dh3 = _rms_bwd(r4, n4, g4, dy)
        dpe = (dh3 * gs).astype(BF16)
        dgl = (dh3 * pe * gs * (1.0 - gs)).astype(BF16)
        ple_acc[...] += lax.dot_general(pb, dpe, TN, preferred_element_type=F32)
        pg_acc[...] += lax.dot_general(hn3, dgl, TN, preferred_element_type=F32)
        dhn3 = lax.dot_general(dgl, wpg_ref[...], NT, preferred_element_type=F32)
        dg3_ref[...] += _colsum(dhn3 * n3)
        dh2 = dh3 + _rms_bwd(r3, n3, g3, dhn3)
        dh2_ref[...] = dh2
        dh2b_ref[...] = dh2.astype(BF16)

        @pl.when(i == nt - 1)
        def _():
            tot = 0.5 / D_MODEL * jnp.sum(lacc[...], axis=-1, keepdims=True)
            loss_ref[...] = jnp.broadcast_to(tot, loss_ref.shape)
            dwpg_ref[...] = pg_acc[...].astype(BF16)
            dwple_ref[...] = ple_acc[...].astype(BF16)

    tok = lambda w: pl.BlockSpec((TM, w), lambda i: (i, 0))
    row = pl.BlockSpec((1, D_MODEL), lambda i: (0, 0))
    act = lambda dt: jax.ShapeDtypeStruct((S, D_MODEL), dt)
    whole = lambda r: pl.BlockSpec((r, D_MODEL), lambda i: (0, 0))
    return pl.pallas_call(
        body, name="head", grid=(nt,),
        in_specs=[tok(D_MODEL), tok(PLE_DIM), _const((1, D_MODEL)), _const(w_pg.shape), _const(w_ple.shape),
                  _const((1, D_MODEL)), tok(D_MODEL)],
        out_specs=[pl.BlockSpec((8, LANES), lambda i: (0, 0)), tok(D_MODEL), tok(D_MODEL), whole(D_MODEL),
                   whole(PLE_DIM), row, row],
        out_shape=[jax.ShapeDtypeStruct((8, LANES), F32), act(F32), act(BF16),
                   jax.ShapeDtypeStruct((D_MODEL, D_MODEL), BF16), jax.ShapeDtypeStruct((PLE_DIM, D_MODEL), BF16),
                   jax.ShapeDtypeStruct((1, D_MODEL), F32), jax.ShapeDtypeStruct((1, D_MODEL), F32)],
        scratch_shapes=[pltpu.VMEM((1, D_MODEL), F32), pltpu.VMEM((D_MODEL, D_MODEL), F32),
                        pltpu.VMEM((PLE_DIM, D_MODEL), F32)],
        compiler_params=_params("arbitrary"),
    )(h2, p, g3, w_pg, w_ple, g4, target)


def _wgrad(name, x, dy, x_kind, dy_kind, nj, k_dim, n_dim, tk=TK):
    S = x.shape[-2]
    nt = S // tk

    def spec(kind, width):
        if kind == "full":
            return pl.BlockSpec((tk, width), lambda j, t: (t, 0))
        return pl.BlockSpec((None, tk, width), lambda j, t: (j, t, 0))

    def body(x_ref, dy_ref, o_ref, acc):
        t = pl.program_id(1)

        @pl.when(t == 0)
        def _():
            acc[...] = jnp.zeros(acc.shape, F32)

        acc[...] += lax.dot_general(x_ref[...].astype(BF16), dy_ref[...], TN, preferred_element_type=F32)

        @pl.when(t == nt - 1)
        def _():
            o_ref[...] = acc[...].astype(BF16)

    return pl.pallas_call(
        body, name=name, grid=(nj, nt),
        in_specs=[spec(x_kind, k_dim), spec(dy_kind, n_dim)],
        out_specs=pl.BlockSpec((None, k_dim, n_dim), lambda j, t: (j, 0, 0)),
        out_shape=jax.ShapeDtypeStruct((nj, k_dim, n_dim), BF16),
        scratch_shapes=[pltpu.VMEM((k_dim, n_dim), F32)],
        compiler_params=_params("arbitrary", "arbitrary"),
    )(x, dy)


def _row_picker(T, off0, off1):
    r = lax.broadcasted_iota(jnp.int32, (2 * T, T + HALO), 0)
    c = lax.broadcasted_iota(jnp.int32, (2 * T, T + HALO), 1)
    want = jnp.where(r < T, r + off0, r - T + off1)
    return jnp.where(c == want, 1.0, 0.0).astype(BF16)


def _ffn_bwd_a(dh2b, up, upc, a, w_down, grads):
    S = dh2b.shape[0]
    T = TM_FF
    hb = T // HALO
    nsteps = S // T
    ng = len(grads)

    def body(dh_ref, up_ref, halo_ref, upc_ref, a_ref, wd_ref, *rest):
        dwd_ref, dup_ref, dcw_ref, dcb_ref = rest[ng:ng + 4]
        stage, dwd_acc = rest[2 * ng + 4:2 * ng + 6]
        i = pl.program_id(0)
        copies = _exchange_copies(rest[:ng], rest[ng + 4:2 * ng + 4], (True,) * ng, *rest[2 * ng + 6:])

        @pl.when(i == 0)
        def _():
            dcw_ref[...] = jnp.zeros(dcw_ref.shape, F32)
            dcb_ref[...] = jnp.zeros(dcb_ref.shape, F32)
            dwd_acc[...] = jnp.zeros(dwd_acc.shape, F32)
            for cp in copies:
                cp.start()

        @pl.when(i == nsteps - 1)
        def _():
            for cp in copies:
                cp.wait()

        dh = dh_ref[...]
        pick = _row_picker(T, HALO - 2, HALO - 1)
        for j in range(4):
            da = lax.dot_general(dh, wd_ref[j], NT, preferred_element_type=F32)
            taps = []
            for jj in (j, j + 4):
                upre = up_ref[jj]
                stage[0:HALO, :] = jnp.where(i > 0, halo_ref[jj], jnp.zeros((HALO, FF_SHARD), BF16))
                stage[HALO:HALO + T, :] = upre
                prv = jnp.dot(pick, stage[...], preferred_element_type=F32)
                taps.append((prv[0:T], prv[T:2 * T], upre.astype(F32)))
            gate, val = upc_ref[j].astype(F32), upc_ref[j + 4].astype(F32)
            sg = jax.nn.sigmoid(gate)
            silu = gate * sg
            dwd_acc[j] += lax.dot_general(a_ref[j], dh, TN, preferred_element_type=F32)
            dgate = (da * val) * (sg + silu * (1.0 - sg))
            dval = da * silu
            for jj, dup, tp in ((j, dgate, taps[0]), (j + 4, dval, taps[1])):
                dup_ref[jj] = dup.astype(BF16)
                dcb_ref[jj] += _colsum(dup)
                for kk in range(3):
                    dcw_ref[jj, kk:kk + 1, :] += _colsum(dup * tp[kk])

        @pl.when(i == nsteps - 1)
        def _():
            dwd_ref[...] = dwd_acc[...].astype(BF16)

    tok = lambda w: pl.BlockSpec((T, w), lambda i: (i, 0))
    shard = pl.BlockSpec((N_DEV, T, FF_SHARD), lambda i: (0, i, 0))
    res = pl.pallas_call(
        body, name="ffn_bwd_a", grid=(nsteps,),
        in_specs=[tok(D_MODEL), shard,
                  pl.BlockSpec((N_DEV, HALO, FF_SHARD), lambda i: (0, jnp.maximum(i * hb - 1, 0), 0)),
                  shard, pl.BlockSpec((4, T, FF_SHARD), lambda i: (0, i, 0)), _const(w_down.shape)] + [ANY_SPEC] * ng,
        out_specs=[_const(w_down.shape), shard,
                   pl.BlockSpec((N_DEV, 3, FF_SHARD), lambda i: (0, 0, 0)),
                   pl.BlockSpec((N_DEV, 1, FF_SHARD), lambda i: (0, 0, 0))] + [ANY_SPEC] * ng,
        out_shape=[jax.ShapeDtypeStruct(w_down.shape, BF16), jax.ShapeDtypeStruct((N_DEV, S, FF_SHARD), BF16),
                   jax.ShapeDtypeStruct((N_DEV, 3, FF_SHARD), F32), jax.ShapeDtypeStruct((N_DEV, 1, FF_SHARD), F32)]
        + _exchange_shapes(grads, (True,) * ng),
        scratch_shapes=[pltpu.VMEM((T + HALO, FF_SHARD), BF16), pltpu.VMEM(w_down.shape, F32)] + _exchange_sems(ng),
        compiler_params=_params("arbitrary"),
    )(dh2b, up, up, upc, a, w_down, *grads)
    return res[0], res[1], res[2], res[3], res[4:]


def _ffn_bwd_b(dup, conv_w, w_up, h1, g2, dh2, grads):
    S = h1.shape[0]
    T = TM_FF
    hb = T // HALO
    nt = S // T
    ng = len(grads)

    def body(dup_ref, halo_ref, cw_ref, wu_ref, h1_ref, g_ref, dh2_ref, *rest):
        dpre_ref, dh1_ref, dh1b_ref, dg_ref = rest[ng:ng + 4]
        stage = rest[2 * ng + 4]
        i = pl.program_id(0)
        copies = _exchange_copies(rest[:ng], rest[ng + 4:2 * ng + 4], (True,) * ng, *rest[2 * ng + 5:])

        @pl.when(i == 0)
        def _():
            dg_ref[...] = jnp.zeros(dg_ref.shape, F32)
            for cp in copies:
                cp.start()

        dhn = jnp.zeros((T, D_MODEL), F32)
        for jj in range(N_DEV):
            dup = dup_ref[jj].astype(F32)
            stage[0:T, :] = dup
            stage[T:T + HALO, :] = jnp.where(i < nt - 1, halo_ref[jj].astype(F32), 0.0)
            cw = cw_ref[jj]
            dpre = (cw[2:3, :] * dup + cw[1:2, :] * stage[1:1 + T, :] + cw[0:1, :] * stage[2:2 + T, :]).astype(BF16)
            dpre_ref[jj] = dpre
            dhn = dhn + lax.dot_general(dpre, wu_ref[jj], NT, preferred_element_type=F32)
        g = g_ref[...]
        r, n = _rms(h1_ref[...])
        dg_ref[...] += _colsum(dhn * n)
        dh1 = dh2_ref[...] + _rms_bwd(r, n, g, dhn)
        dh1_ref[...] = dh1
        dh1b_ref[...] = dh1.astype(BF16)

        @pl.when(i == nt - 1)
        def _():
            for cp in copies:
                cp.wait()

    tok = lambda w: pl.BlockSpec((T, w), lambda i: (i, 0))
    shard = pl.BlockSpec((N_DEV, T, FF_SHARD), lambda i: (0, i, 0))
    res = pl.pallas_call(
        body, name="ffn_bwd_b", grid=(nt,),
        in_specs=[shard,
                  pl.BlockSpec((N_DEV, HALO, FF_SHARD), lambda i: (0, jnp.minimum((i + 1) * hb, S // HALO - 1), 0)),
                  _const(conv_w.shape), _const(w_up.shape), tok(D_MODEL), _const((1, D_MODEL)), tok(D_MODEL)]
        + [ANY_SPEC] * ng,
        out_specs=[shard, tok(D_MODEL), tok(D_MODEL), pl.BlockSpec((1, D_MODEL), lambda i: (0, 0))] + [ANY_SPEC] * ng,
        out_shape=[jax.ShapeDtypeStruct((N_DEV, S, FF_SHARD), BF16), jax.ShapeDtypeStruct((S, D_MODEL), F32),
                   jax.ShapeDtypeStruct((S, D_MODEL), BF16), jax.ShapeDtypeStruct((1, D_MODEL), F32)]
        + _exchange_shapes(grads, (True,) * ng),
        scratch_shapes=[pltpu.VMEM((T + HALO, FF_SHARD), F32)] + _exchange_sems(ng),
        compiler_params=_params("arbitrary"),
    )(dup, dup, conv_w, w_up, h1, g2, dh2, *grads)
    return res[0], res[1], res[2], res[3], res[4:]


def _mix_bwd(dh1b, w_out, dlt, mix, pool_w, pool_scale):
    S = dh1b.shape[0]
    nt = S // TM

    def body(dh_ref, w_ref, dlt_ref, mix_ref, pw_ref, ps_ref, datt_ref, du_ref, dpw_ref, dps_ref, dwo_ref,
             stage, carry, wo_acc):
        i = pl.program_id(0)
        tile = nt - 1 - i

        @pl.when(i == 0)
        def _():
            dpw_ref[...] = jnp.zeros(dpw_ref.shape, F32)
            dps_ref[...] = jnp.zeros(dps_ref.shape, F32)
            carry[...] = jnp.zeros(carry.shape, F32)
            wo_acc[...] = jnp.zeros(wo_acc.shape, F32)

        wo_acc[...] += lax.dot_general(mix_ref[...], dh_ref[...], TN, preferred_element_type=F32)

        @pl.when(i == nt - 1)
        def _():
            dwo_ref[...] = wo_acc[...].astype(BF16)

        dmix = lax.dot_general(dh_ref[...], w_ref[...], NT, preferred_element_type=F32)
        datt_ref[...] = dmix[:, 0:ATT_WIDTH]
        for g, w in enumerate(POOL_WINDOWS):
            cols = slice(g * POOL_GROUP, (g + 1) * POOL_GROUP)
            dpool = dmix[:, ATT_WIDTH + g * POOL_GROUP:ATT_WIDTH + (g + 1) * POOL_GROUP]
            dl = dlt_ref[:, cols]
            pw = pw_ref[g].astype(BF16)
            yg = jnp.dot(dl, pw, preferred_element_type=F32)
            dps_ref[:, cols] += _colsum(dpool * yg)
            dy = (dpool * ps_ref[:, cols]).astype(BF16)
            dpw_ref[g] += lax.dot_general(dl, dy, TN, preferred_element_type=F32)
            ddlt = lax.dot_general(dy, pw, NT, preferred_element_type=F32)
            cg = ddlt / _pool_count(tile, w)
            stage[0:TM, :] = cg
            stage[TM:TM + HALO, :] = carry[:, cols]
            acc = cg
            for j in range(1, w):
                acc = acc + stage[j:j + TM, :]
            du_ref[:, cols] = acc - ddlt
            carry[:, cols] = cg[0:HALO, :]

    tok = lambda w: pl.BlockSpec((TM, w), lambda i: (nt - 1 - i, 0))
    return pl.pallas_call(
        body, name="mix_bwd", grid=(nt,),
        in_specs=[tok(D_MODEL), _const(w_out.shape), tok(POOL_WIDTH), tok(D_MODEL), _const(pool_w.shape),
                  _const((1, POOL_WIDTH))],
        out_specs=[tok(ATT_WIDTH), tok(POOL_WIDTH), pl.BlockSpec(pool_w.shape, lambda i: (0, 0, 0)),
                   pl.BlockSpec((1, POOL_WIDTH), lambda i: (0, 0)), pl.BlockSpec(w_out.shape, lambda i: (0, 0))],
        out_shape=[jax.ShapeDtypeStruct((S, ATT_WIDTH), F32), jax.ShapeDtypeStruct((S, POOL_WIDTH), F32),
                   jax.ShapeDtypeStruct(pool_w.shape, F32), jax.ShapeDtypeStruct((1, POOL_WIDTH), F32),
                   jax.ShapeDtypeStruct(w_out.shape, BF16)],
        scratch_shapes=[pltpu.VMEM((TM + HALO, POOL_GROUP), F32), pltpu.VMEM((HALO, POOL_WIDTH), F32),
                        pltpu.VMEM(w_out.shape, F32)],
        compiler_params=_params("arbitrary"),
    )(dh1b, w_out, dlt, mix, pool_w, pool_scale)


def _attn_bwd(slopes, q, k, v, o, lse, do, grads, scatter):
    S = q.shape[0]
    CH = 512
    ng = len(grads)
    steps = ATT_WIDTH // LANES

    def body(slope_ref, q_ref, k_ref, v_ref, o_ref, lse_ref, do_ref, *rest):
        dq_ref, dk_ref, dv_ref = rest[ng:ng + 3]
        dl_s, bias_s = rest[2 * ng + 3:2 * ng + 5]
        hp = pl.program_id(0)
        copies = _exchange_copies(rest[:ng], rest[ng + 3:2 * ng + 3], scatter, *rest[2 * ng + 5:])

        @pl.when(hp == 0)
        def _():
            for cp in copies:
                cp.start()

        is0 = lax.broadcasted_iota(jnp.int32, (SPAN, LANES), 1) < HEAD_DIM
        is0c = lax.broadcasted_iota(jnp.int32, (CH, LANES), 1) < HEAD_DIM

        def prep(ci, carry):
            rows = pl.ds(pl.multiple_of(ci * CH, CH), CH)
            prod = do_ref[rows, :] * o_ref[rows, :]
            d0 = jnp.sum(jnp.where(is0c, prod, 0.0), axis=-1, keepdims=True)
            d1 = jnp.sum(jnp.where(is0c, 0.0, prod), axis=-1, keepdims=True)
            dl_s[rows, :] = jnp.where(is0c, d0, d1)
            zero = jnp.zeros((CH, LANES), F32)
            dq_ref[rows, :] = zero
            dk_ref[rows, :] = zero
            dv_ref[rows, :] = zero
            return carry

        lax.fori_loop(0, S // CH, prep, 0)

        for d in DILATIONS:
            nb, ngroups = _attn_groups(S, d, ATTN_GROUP_BWD, writes_key_rows=True)
            _attn_fill_bias(bias_s, slope_ref, hp, d)

            def group(i, carry, d=d, nb=nb):
                blocks = [_attn_block(i, g, d, nb, ATTN_GROUP_BWD) for g in range(ATTN_GROUP_BWD)]
                loaded = [(q_ref[rows, :], do_ref[rows, :], lse_ref[rows, :], dl_s[rows, :], k_ref[krows, :],
                           v_ref[krows, :].astype(BF16)) for rows, krows, _ in blocks]
                new = []
                for (rows, krows, tab), (qb, dob, lse_b, dl_b, kf, vb) in zip(blocks, loaded):
                    kb = kf.astype(BF16)
                    qs = _stack_heads(qb, is0).astype(BF16)
                    dos = _stack_heads(dob, is0).astype(BF16)
                    lse_s = jnp.concatenate([lse_b[:, 0:1], lse_b[:, HEAD_DIM:HEAD_DIM + 1]], axis=0)
                    dl_s2 = jnp.concatenate([dl_b[:, 0:1], dl_b[:, HEAD_DIM:HEAD_DIM + 1]], axis=0)
                    s = lax.dot_general(qs, kb, NT, preferred_element_type=F32) + bias_s[tab]
                    pr = jnp.exp(s - lse_s)
                    dp = lax.dot_general(dos, vb, NT, preferred_element_type=F32)
                    ds = (pr * (dp - dl_s2)).astype(BF16)
                    dv_c = lax.dot_general(pr.astype(BF16), dos, TN, preferred_element_type=F32)
                    dk_c = lax.dot_general(ds, qs, TN, preferred_element_type=F32)
                    dq_c = _unstack_heads(jnp.dot(ds, kb, preferred_element_type=F32), is0)
                    new.append((dq_c, dk_c, dv_c))
                old = [(dq_ref[rows, :], dk_ref[krows, :], dv_ref[krows, :]) for rows, krows, _ in blocks]
                for (rows, krows, _), (dq_c, dk_c, dv_c), (dq_o, dk_o, dv_o) in zip(blocks, new, old):
                    dq_ref[rows, :] = dq_o + dq_c
                    dk_ref[krows, :] = dk_o + dk_c
                    dv_ref[krows, :] = dv_o + dv_c
                return carry

            lax.fori_loop(0, ngroups, group, 0)

        @pl.when(hp == steps - 1)
        def _():
            for cp in copies:
                cp.wait()

    col = pl.BlockSpec((S, LANES), lambda i: (0, i))
    res = pl.pallas_call(
        body, name="attn_bwd", grid=(steps,),
        in_specs=[pl.BlockSpec(memory_space=pltpu.SMEM)] + [col] * 6 + [ANY_SPEC] * ng,
        out_specs=[col] * 3 + [ANY_SPEC] * ng,
        out_shape=[jax.ShapeDtypeStruct((S, ATT_WIDTH), F32)] * 3 + _exchange_shapes(grads, scatter),
        scratch_shapes=[pltpu.VMEM((S, LANES), F32), pltpu.VMEM((2, 2 * SPAN, 2 * SPAN), F32)] + _exchange_sems(ng),
        compiler_params=_params("arbitrary"),
    )(slopes, q, k, v, o, lse, do, *grads)
    return res[0], res[1], res[2], res[3:]


def _in_bwd(dq, dk, dv, du, w_in, x, g1, dh1):
    S = x.shape[0]

    def body(dq_ref, dk_ref, dv_ref, du_ref, w_ref, x_ref, g_ref, dh1_ref, dz_ref, dx_ref, dg_ref):
        @pl.when(pl.program_id(0) == 0)
        def _():
            dg_ref[...] = jnp.zeros(dg_ref.shape, F32)

        srcs = (dq_ref, dk_ref, dv_ref, du_ref)
        dhn = jnp.zeros((TM, D_MODEL), F32)
        for j in range(N_DEV):
            dz = srcs[j // 2][:, (j % 2) * 256:(j % 2 + 1) * 256]
            if j < 2:
                dz = dz * (HEAD_DIM ** -0.5)
            dz = dz.astype(BF16)
            dz_ref[j] = dz
            dhn = dhn + lax.dot_general(dz, w_ref[j], NT, preferred_element_type=F32)
        g = g_ref[...]
        r, n = _rms(x_ref[...])
        dg_ref[...] += _colsum(dhn * n)
        dx_ref[...] = dh1_ref[...] + _rms_bwd(r, n, g, dhn)

    tok = lambda w: pl.BlockSpec((TM, w), lambda i: (i, 0))
    return pl.pallas_call(
        body, name="in_bwd", grid=(S // TM,),
        in_specs=[tok(512)] * 4 + [_const(w_in.shape), tok(D_MODEL), _const((1, D_MODEL)), tok(D_MODEL)],
        out_specs=[pl.BlockSpec((N_DEV, TM, 256), lambda i: (0, i, 0)), tok(D_MODEL),
                   pl.BlockSpec((1, D_MODEL), lambda i: (0, 0))],
        out_shape=[jax.ShapeDtypeStruct((N_DEV, S, 256), BF16), jax.ShapeDtypeStruct((S, D_MODEL), F32),
                   jax.ShapeDtypeStruct((1, D_MODEL), F32)],
        compiler_params=_params("arbitrary"),
    )(dq, dk, dv, du, w_in, x, g1, dh1)


def _adamw(name, parts, w, m, v):
    R, C = w.shape
    rb = max([r for r in range(16, R // 2 + 1, 16) if R % r == 0 and r * C <= 512 * 1024], default=R)

    def body(p_ref, w_ref, m_ref, v_ref, g_ref, d_ref, mo_ref, vo_ref):
        g = p_ref[0].astype(F32)
        for s in range(1, N_DEV):
            g = g + p_ref[s].astype(F32)
        g_ref[...] = g
        d_ref[...], mo_ref[...], vo_ref[...] = _adam_update(g, w_ref[...], m_ref[...], v_ref[...])

    blk = pl.BlockSpec((rb, C), lambda i: (i, 0))
    return pl.pallas_call(
        body, name=name, grid=(R // rb,),
        in_specs=[pl.BlockSpec((N_DEV, rb, C), lambda i: (0, i, 0)), blk, blk, blk],
        out_specs=[blk] * 4,
        out_shape=[jax.ShapeDtypeStruct((R, C), F32)] * 4,
        compiler_params=_params("arbitrary"),
    )(parts, w, m, v)


def _adam_update(g, w, m, v):
    m_new = ADAM_B1 * m + (1.0 - ADAM_B1) * g
    v_new = ADAM_B2 * v + (1.0 - ADAM_B2) * (g * g)
    m_hat = m_new / (1.0 - ADAM_B1 ** ADAM_STEP)
    v_hat = v_new / (1.0 - ADAM_B2 ** ADAM_STEP)
    return -ADAM_LR * (m_hat / (jnp.sqrt(v_hat) + ADAM_EPS) + ADAM_WD * w), m_new, v_new


def _adamw_small(parts, loss_parts, ws, ms, vs):
    n = len(ws)

    def body(*refs):
        p_refs, lp_ref = refs[:n], refs[n]
        w_refs, m_refs, v_refs = refs[n + 1:2 * n + 1], refs[2 * n + 1:3 * n + 1], refs[3 * n + 1:4 * n + 1]
        outs = refs[4 * n + 1:]
        for i in range(n):
            g = p_refs[i][0]
            for s in range(1, N_DEV):
                g = g + p_refs[i][s]
            d, m_new, v_new = _adam_update(g, w_refs[i][...], m_refs[i][...], v_refs[i][...])
            outs[i][...] = g
            outs[n + i][...] = d
            outs[2 * n + i][...] = m_new
            outs[3 * n + i][...] = v_new
        tot = lp_ref[0]
        for s in range(1, N_DEV):
            tot = tot + lp_ref[s]
        outs[4 * n][...] = tot

    shapes = [jax.ShapeDtypeStruct(w.shape, F32) for w in ws]
    res = pl.pallas_call(
        body, name="adamw_replicated",
        out_shape=shapes * 4 + [jax.ShapeDtypeStruct(loss_parts.shape[1:], F32)],
        compiler_params=_params(),
    )(*parts, loss_parts, *ws, *ms, *vs)
    return res[:n], res[n:2 * n], res[2 * n:3 * n], res[3 * n:4 * n], res[4 * n]


def _gather2(name, arrays):
    n = len(arrays)

    def body(*refs):
        first, passed, last = _gather2_copies(refs[:n], refs[n:2 * n], *refs[2 * n:])
        for cp in first:
            cp.start()
        for arrival, cp in passed:
            arrival.wait_recv()
            cp.start()
        for wait in last:
            wait()

    return pl.pallas_call(
        body, name=name,
        in_specs=[ANY_SPEC] * n, out_specs=[ANY_SPEC] * n, out_shape=_exchange_shapes(arrays, (False,) * n),
        scratch_shapes=_exchange_sems(n),
    )(*arrays)


def _dw_in_exchange(hn, dz, small):
    S = hn.shape[0]
    nt = S // TK
    ns = len(small)
    kd, nd = hn.shape[1], dz.shape[2]
    me_arr = (4 * lax.axis_index("x") + 2 * lax.axis_index("y") + lax.axis_index("c")).astype(jnp.int32).reshape(1)

    def body(me_ref, x_ref, dy_ref, *rest):
        recv_ref = rest[ns]
        acc, stage, send_sems, recv_sems, own_sem = rest[2 * ns + 1:2 * ns + 6]
        j, t = pl.program_id(0), pl.program_id(1)
        x, y, c = lax.axis_index("x"), lax.axis_index("y"), lax.axis_index("c")
        me = 4 * x + 2 * y + c
        small_copies = _exchange_copies(rest[:ns], rest[ns + 1:2 * ns + 1], (False,) * ns, *rest[2 * ns + 6:])

        @pl.when((j == 0) & (t == 0))
        def _():
            for cp in small_copies:
                cp.start()

        @pl.when(t == 0)
        def _():
            acc[...] = jnp.zeros(acc.shape, F32)

        acc[...] += lax.dot_general(x_ref[...], dy_ref[...], TN, preferred_element_type=F32)

        def to_owner(k, owner):
            return pltpu.make_async_remote_copy(
                src_ref=stage.at[owner], dst_ref=recv_ref.at[me], send_sem=send_sems.at[k], recv_sem=recv_sems.at[k],
                device_id=(owner // 4, (owner // 2) % 2, owner % 2), device_id_type=MESH)

        own = pltpu.make_async_copy(stage.at[me], recv_ref.at[me], own_sem)

        @pl.when(t == nt - 1)
        def _():
            owner = (me + 1 + j) % N_DEV
            stage[owner] = acc[...].astype(BF16)

            @pl.when(j < N_DEV - 1)
            def _():
                to_owner(j, owner).start()

            @pl.when(j == N_DEV - 1)
            def _():
                own.start()
                own.wait()
                for k in range(N_DEV - 1):
                    to_owner(k, me).wait_send()
                    to_owner(k, me).wait_recv()
                for cp in small_copies:
                    cp.wait()

    slab = lambda j, me_ref: (me_ref[0] + 1 + j) % N_DEV
    grid_spec = pltpu.PrefetchScalarGridSpec(
        num_scalar_prefetch=1, grid=(N_DEV, nt),
        in_specs=[pl.BlockSpec((TK, kd), lambda j, t, me_ref: (t, 0)),
                  pl.BlockSpec((None, TK, nd), lambda j, t, me_ref: (slab(j, me_ref), t, 0))] + [ANY_SPEC] * ns,
        out_specs=[ANY_SPEC] * (ns + 1),
        scratch_shapes=[pltpu.VMEM((kd, nd), F32), pltpu.VMEM((N_DEV, kd, nd), BF16),
                        pltpu.SemaphoreType.DMA((N_DEV - 1,)), pltpu.SemaphoreType.DMA((N_DEV - 1,)),
                        pltpu.SemaphoreType.DMA] + _exchange_sems(ns))
    res = pl.pallas_call(
        body, name="dw_in_exchange", grid_spec=grid_spec,
        out_shape=[jax.ShapeDtypeStruct((N_DEV, kd, nd), BF16)] + _exchange_shapes(small, (False,) * ns),
        compiler_params=_params("arbitrary", "arbitrary"),
    )(me_arr, hn, dz, *small)
    return res[0], res[1:]


def kernel(x, p, ln_mix, w_in, pool_w, pool_scale, w_out, ln_ffn, w_up, conv_w, conv_b, w_down, ln_ple, w_ple_gate, w_ple, ln_final, loss_target, m_ln_mix, m_w_in, m_pool_w, m_pool_scale, m_w_out, m_ln_ffn, m_w_up, m_conv_w, m_conv_b, m_w_down, m_ln_ple, m_w_ple_gate, m_w_ple, m_ln_final, v_ln_mix, v_w_in, v_pool_w, v_pool_scale, v_w_out, v_ln_ffn, v_w_up, v_conv_w, v_conv_b, v_w_down, v_ln_ple, v_w_ple_gate, v_w_ple, v_ln_final):
    xs, ps, tgt, pool_w0 = x[0], p[0, 0], loss_target[0], pool_w[0]
    slopes = jnp.exp2(-8.0 * (jnp.arange(N_HEADS, dtype=F32) + 1.0) / N_HEADS)
    conv_b_s = conv_b.reshape(N_DEV, 1, FF_SHARD)

    (w_in_g,) = _gather2("gather_w_in", [w_in[0].astype(BF16)])
    (q, k, v, u, hn1), (w_out_g,) = _qkvu(xs, ln_mix, w_in_g, [w_out[0].astype(BF16)])
    att, lse, (w_up_g,) = _attn_fwd(slopes, q, k, v, [w_up[0].astype(BF16)])
    w_out_f = w_out_g.reshape(D_MODEL, D_MODEL)
    h1, mix, dlt, (conv_w_g,) = _mix_out(xs, att, u, pool_w0, pool_scale, w_out_f, [conv_w[0]])
    a_fwd, hn2, up, upc, (w_down_g, w_pg_g, w_ple_g) = _ffn_up(
        h1, ln_ffn, w_up_g, conv_w_g, conv_b_s,
        [w_down[0].astype(BF16), w_ple_gate[0].astype(BF16), w_ple[0].astype(BF16)])
    w_down_f = w_down_g.reshape(4, FF_SHARD, D_MODEL)
    h2 = _ffn_down(h1, a_fwd, w_down_f)
    w_pg_f = w_pg_g.reshape(D_MODEL, D_MODEL)
    w_ple_f = jnp.transpose(w_ple_g, (1, 0, 2)).reshape(PLE_DIM, D_MODEL)
    loss_blk, dh2, dh2b, d_w_pg, d_w_ple, d_ln_ple, d_ln_final = _head(
        h2, ps, ln_ple, w_pg_f, w_ple_f, ln_final.reshape(1, D_MODEL), tgt)

    d_w_pg = d_w_pg.reshape(N_DEV, D_MODEL // N_DEV, D_MODEL)
    d_w_ple = jnp.transpose(d_w_ple.reshape(PLE_DIM, N_DEV, LANES), (1, 0, 2))
    d_w_down, dup, d_conv_w, d_conv_b, (r_w_pg, r_w_ple) = _ffn_bwd_a(dh2b, up, upc, a_fwd, w_down_f, [d_w_pg, d_w_ple])
    d_w_down = d_w_down.reshape(N_DEV, D_FF // N_DEV, D_MODEL)
    dpre, dh1, dh1b, d_ln_ffn, (r_conv_w, r_w_down) = _ffn_bwd_b(
        dup, conv_w_g, w_up_g, h1, ln_ffn, dh2, [d_conv_w, d_w_down])
    datt, du, d_pool_w, d_pool_scale, d_w_out = _mix_bwd(dh1b, w_out_f, dlt, mix, pool_w0, pool_scale)
    d_w_out = d_w_out.reshape(N_DEV, D_MODEL // N_DEV, D_MODEL)
    d_w_up = _wgrad("dw_up", dpre, hn2, "lead", "full", N_DEV, FF_SHARD, D_MODEL)
    rep_late = [d_pool_w, d_pool_scale, d_ln_ffn, d_conv_b.reshape(1, 2 * D_FF), d_ln_ple, d_ln_final, loss_blk]
    dq, dk, dv, received = _attn_bwd(slopes, q, k, v, att, lse, datt, [d_w_out, d_w_up] + rep_late,
                                     (True, True) + (False,) * len(rep_late))
    r_w_out, r_w_up, r_rep = received[0], received[1], list(received[2:])
    dz, grad_x, d_ln_mix = _in_bwd(dq, dk, dv, du, w_in_g, xs, ln_mix, dh1)

    rep_names = ("ln_mix", "pool_w", "pool_scale", "ln_ffn", "conv_b", "ln_ple", "ln_final")
    rep_w = [ln_mix, pool_w0, pool_scale, ln_ffn, conv_b, ln_ple, ln_final.reshape(1, D_MODEL)]
    rep_m = [m_ln_mix, m_pool_w[0], m_pool_scale, m_ln_ffn, m_conv_b, m_ln_ple, m_ln_final.reshape(1, D_MODEL)]
    rep_v = [v_ln_mix, v_pool_w[0], v_pool_scale, v_ln_ffn, v_conv_b, v_ln_ple, v_ln_final.reshape(1, D_MODEL)]
    r_w_in, (r_ln_mix,) = _dw_in_exchange(hn1, dz, [d_ln_mix])
    small = _adamw_small([r_ln_mix] + r_rep[:-1], r_rep[-1], rep_w, rep_m, rep_v)
    loss = small[4][0, 0]

    sharded = {}
    sharded["w_in"] = _adamw("adamw_w_in", r_w_in, w_in[0], m_w_in[0], v_w_in[0])
    sharded["w_out"] = _adamw("adamw_w_out", r_w_out, w_out[0], m_w_out[0], v_w_out[0])
    sharded["w_up"] = [t.T for t in _adamw("adamw_w_up", r_w_up, w_up[0].T, m_w_up[0].T, v_w_up[0].T)]
    sharded["conv_w"] = _adamw("adamw_conv_w", r_conv_w, conv_w[0], m_conv_w[0], v_conv_w[0])
    sharded["w_down"] = _adamw("adamw_w_down", r_w_down, w_down[0], m_w_down[0], v_w_down[0])
    sharded["w_ple_gate"] = _adamw("adamw_w_ple_gate", r_w_pg, w_ple_gate[0], m_w_ple_gate[0], v_w_ple_gate[0])
    sharded["w_ple"] = _adamw("adamw_w_ple", r_w_ple, w_ple[0], m_w_ple[0], v_w_ple[0])

    shapes = dict(w_in=w_in, w_out=w_out, w_up=w_up, conv_w=conv_w, w_down=w_down, w_ple_gate=w_ple_gate, w_ple=w_ple,
                  ln_mix=ln_mix, pool_w=pool_w, pool_scale=pool_scale, ln_ffn=ln_ffn, conv_b=conv_b, ln_ple=ln_ple,
                  ln_final=ln_final)

    def leaf(kind, n):
        src = sharded[n][kind] if n in sharded else small[kind][rep_names.index(n)]
        return src.reshape(shapes[n].shape)

    order = ("ln_mix", "w_in", "pool_w", "pool_scale", "w_out", "ln_ffn", "w_up", "conv_w", "conv_b", "w_down", "ln_ple",
             "w_ple_gate", "w_ple", "ln_final")
    outs = [loss, grad_x[None]]
    for kind in range(4):
        outs += [leaf(kind, n) for n in order]
    return tuple(outs)
```
